```python
import math
import jax, jax.numpy as jnp
from jax import lax
import numpy as np

D_MODEL = 1024
BATCH = 8
SEQ = 2048
DEPTH = 1

PL_DIM = 256
MIX_WIDTH = D_MODEL
DIFF_WIDTH = MIX_WIDTH // 2
RWKV_WIDTH = MIX_WIDTH - DIFF_WIDTH
DIFF_HEAD_DIM = 64
DIFF_HEADS = DIFF_WIDTH // (2 * DIFF_HEAD_DIM)
RWKV_HEAD = 64
RWKV_HEADS = RWKV_WIDTH // RWKV_HEAD
D_DECAY_LORA = 64
D_AAA_LORA = 64
D_GATE_LORA = 160
IN_COLS = 3 * DIFF_WIDTH + 3 * RWKV_WIDTH
ROPE_THETA = 10000.0
Q_BLOCK = 128
NORM_EPS = 1e-6
SUBLN_EPS = 1e-5
GN_EPS = 64e-5
N_GROUPS = 4
EXPERTS_PER_GROUP = 8
N_EXPERTS = N_GROUPS * EXPERTS_PER_GROUP
TOP_K = 2
D_EXPERT = 512
MOE_BLOCK = 128
MAX_POS_OFFSET = 1024

kernel_name = "hymba_diffattn_rwkv7_hmoe_block"


def rms_norm(x, g, eps=NORM_EPS):
    xf = x.astype(jnp.float32)
    y = xf * lax.rsqrt(jnp.mean(xf * xf, axis=-1, keepdims=True) + eps)
    return (y * g.astype(jnp.float32)).astype(x.dtype)


def token_shift(z):
    return jnp.pad(z, ((0, 0), (1, 0), (0, 0)))[:, :-1]


def rope_tables(positions):
    half = DIFF_HEAD_DIM // 2
    inv_freq = ROPE_THETA ** (-jnp.arange(half, dtype=jnp.float32) / half)
    ang = positions.astype(jnp.float32)[..., None] * inv_freq
    return jnp.cos(ang), jnp.sin(ang)


def apply_rope(t, cos, sin):
    c = cos[:, :, None, None, :]
    s = sin[:, :, None, None, :]
    t1, t2 = jnp.split(t.astype(jnp.float32), 2, axis=-1)
    return jnp.concatenate([t1 * c - t2 * s, t2 * c + t1 * s], axis=-1).astype(t.dtype)


def diff_attention(q, k, v, lam, out_g, lam_init):
    B, S = q.shape[0], q.shape[1]
    scale = DIFF_HEAD_DIM ** -0.5
    outs = []
    for start in range(0, S, Q_BLOCK):
        end = start + Q_BLOCK
        qb = q[:, start:end]
        kb = k[:, :end]
        vb = v[:, :end]
        s = jnp.einsum('bqhmd,bkhmd->bhmqk', qb, kb,
                       preferred_element_type=jnp.float32) * scale
        causal = jnp.arange(start, end)[:, None] >= jnp.arange(end)[None, :]
        s = jnp.where(causal, s, -jnp.inf)
        prob = jax.nn.softmax(s, axis=-1)
        attn = prob[:, :, 0] - lam * prob[:, :, 1]
        outs.append(jnp.einsum('bhqk,bkhe->bqhe', attn.astype(v.dtype), vb))
    o = jnp.concatenate(outs, axis=1)
    o = rms_norm(o, out_g, SUBLN_EPS) * (1.0 - lam_init)
    return o.reshape(B, S, DIFF_WIDTH)


def rwkv7_scan(r, w, k, v, a, b):
    B, S, H, N = r.shape

    def step(state, inp):
        r_t, w_t, k_t, v_t, a_t, b_t = inp
        sa = jnp.einsum('bhvk,bhk->bhv', state, a_t)
        state = (state * w_t[:, :, None, :] + sa[..., None] * b_t[:, :, None, :]
                 + v_t[..., None] * k_t[:, :, None, :])
        y = jnp.einsum('bhvk,bhk->bhv', state, r_t)
        return state, y

    xs = tuple(jnp.moveaxis(t, 1, 0) for t in (r, w, k, v, a, b))
    s0 = jnp.zeros((B, H, N, N), jnp.float32)
    _, y = lax.scan(step, s0, xs)
    return jnp.moveaxis(y, 0, 1)


def rwkv7_time_mix(h, zr, zk, zv, mu_rkv, mu_wag, w0, w1, w2, a0, a1, a2,
                   g1, g2, k_k, k_a, r_k, gn_w, gn_b):
    B, S, _ = h.shape
    H, N = RWKV_HEADS, RWKV_HEAD
    f32 = jnp.float32
    r = zr + (token_shift(zr) - zr) * mu_rkv[0]
    k = zk + (token_shift(zk) - zk) * mu_rkv[1]
    v = zv + (token_shift(zv) - zv) * mu_rkv[2]
    dh = token_shift(h) - h
    xw = h + dh * mu_wag[0]
    xa = h + dh * mu_wag[1]
    xg = h + dh * mu_wag[2]
    w_log = -jax.nn.softplus(-(w0 + jnp.tanh(xw @ w1) @ w2).astype(f32)) - 0.5
    decay = jnp.exp(-jnp.exp(w_log))
    a = jax.nn.sigmoid((a0 + (xa @ a1) @ a2).astype(f32))
    g = (jax.nn.sigmoid(xg @ g1) @ g2).astype(f32)

    def heads(t):
        return t.reshape(B, S, H, N).astype(f32)

    r, k, v, decay, a = heads(r), heads(k), heads(v), heads(decay), heads(a)
    kk = k * k_k.reshape(H, N).astype(f32)
    kk = kk * lax.rsqrt(jnp.maximum(jnp.sum(kk * kk, axis=-1, keepdims=True), 1e-24))
    k = k * (1.0 + (a - 1.0) * k_a.reshape(H, N).astype(f32))
    y = rwkv7_scan(r, decay, k, v, -kk, kk * a)
    mu = jnp.mean(y, axis=-1, keepdims=True)
    var = jnp.mean(jnp.square(y - mu), axis=-1, keepdims=True)
    y = ((y - mu) * lax.rsqrt(var + GN_EPS)).reshape(B, S, H * N)
    y = y * gn_w.astype(f32) + gn_b.astype(f32)
    bonus = jnp.sum(r * k * r_k.astype(f32), axis=-1, keepdims=True) * v
    y = y + bonus.reshape(B, S, H * N)
    return (y * g).astype(h.dtype)


def hier_moe(h, w_group, w_expert_router, w_gate, w_up, w_down):
    B, S, D = h.shape
    T = B * S
    xt = h.reshape(T, D)
    g_logits = (xt @ w_group).astype(jnp.float32)
    g_prob = jax.nn.softmax(g_logits, axis=-1)
    g_sel = jnp.argmax(g_logits, axis=-1)
    g_w = jnp.take_along_axis(g_prob, g_sel[:, None], axis=1)[:, 0]
    e_logits_all = jnp.einsum('td,gde->tge', xt, w_expert_router).astype(jnp.float32)
    e_logits = jnp.take_along_axis(e_logits_all, g_sel[:, None, None], axis=1)[:, 0]
    top_v, top_i = lax.top_k(e_logits, TOP_K)
    e_w = jax.nn.softmax(top_v, axis=-1) * g_w[:, None]
    expert_id = g_sel[:, None].astype(jnp.int32) * EXPERTS_PER_GROUP + top_i.astype(jnp.int32)

    A = T * TOP_K
    flat_e = expert_id.reshape(A)
    flat_tok = jnp.repeat(jnp.arange(T, dtype=jnp.int32), TOP_K)
    flat_w = e_w.reshape(A)
    order = jnp.argsort(flat_e)
    se = flat_e[order]
    counts = jnp.bincount(flat_e, length=N_EXPERTS)
    starts = jnp.cumsum(counts) - counts
    pcounts = (counts + MOE_BLOCK - 1) // MOE_BLOCK * MOE_BLOCK
    pends = jnp.cumsum(pcounts)
    pstarts = pends - pcounts
    dest = pstarts[se] + (jnp.arange(A) - starts[se])
    n_blocks = -(-A // MOE_BLOCK) + N_EXPERTS
    P = n_blocks * MOE_BLOCK
    buf_tok = jnp.full((P,), T, jnp.int32).at[dest].set(flat_tok[order])
    buf_w = jnp.zeros((P,), jnp.float32).at[dest].set(flat_w[order])
    block_e = jnp.minimum(
        jnp.searchsorted(pends, jnp.arange(n_blocks) * MOE_BLOCK, side='right'),
        N_EXPERTS - 1).astype(jnp.int32)
    x_pad = jnp.concatenate([xt, jnp.zeros((1, D), xt.dtype)], axis=0)
    xb = x_pad[buf_tok].reshape(n_blocks, MOE_BLOCK, D)

    def expert_block(args):
        xblk, e = args
        return (jax.nn.silu(xblk @ w_gate[e]) * (xblk @ w_up[e])) @ w_down[e]

    yb = lax.map(expert_block, (xb, block_e)).reshape(P, D)
    yb = yb.astype(jnp.float32) * buf_w[:, None]
    y = jax.ops.segment_sum(yb, buf_tok, num_segments=T + 1)[:T]
    return y.astype(h.dtype).reshape(B, S, D)


def setup_inputs(seed: int = 0) -> dict:
    key = jax.random.key(seed)
    ks = iter(jax.random.split(key, 48))
    L, D = DEPTH, D_MODEL

    def nrm(shape, scale):
        return jax.random.normal(next(ks), shape, jnp.float32) * scale

    def gain(shape):
        return 1.0 + nrm(shape, 0.02)

    def unif(shape):
        return jax.random.uniform(next(ks), shape, jnp.float32)

    x = nrm((BATCH, SEQ, D), 1.0)
    p = nrm((DEPTH, BATCH, SEQ, PL_DIM), 1.0)
    positions = (jnp.arange(SEQ, dtype=jnp.int32)[None, :]
                 + jax.random.randint(next(ks), (BATCH, 1), 0, MAX_POS_OFFSET, dtype=jnp.int32))
    return {
        "x": x,
        "p": p,
        "positions": positions,
        "attn_norm_g": gain((L, D)),
        "w_in": nrm((L, D, IN_COLS), D ** -0.5),
        "q_norm_g": gain((L, DIFF_HEAD_DIM)),
        "k_norm_g": gain((L, DIFF_HEAD_DIM)),
        "lambda_q1": nrm((L, DIFF_HEAD_DIM), 0.1),
        "lambda_k1": nrm((L, DIFF_HEAD_DIM), 0.1),
        "lambda_q2": nrm((L, DIFF_HEAD_DIM), 0.1),
        "lambda_k2": nrm((L, DIFF_HEAD_DIM), 0.1),
        "diff_out_g": gain((L, 2 * DIFF_HEAD_DIM)),
        "mu_rkv": unif((L, 3, RWKV_WIDTH)),
        "mu_wag": unif((L, 3, D)),
        "w0": jnp.linspace(-6.0, -1.0, RWKV_WIDTH, dtype=jnp.float32)[None, :] + nrm((L, RWKV_WIDTH), 0.1),
        "w_lora_a": nrm((L, D, D_DECAY_LORA), D ** -0.5),
        "w_lora_b": nrm((L, D_DECAY_LORA, RWKV_WIDTH), 0.1 * D_DECAY_LORA ** -0.5),
        "a0": nrm((L, RWKV_WIDTH), 0.1),
        "a_lora_a": nrm((L, D, D_AAA_LORA), D ** -0.5),
        "a_lora_b": nrm((L, D_AAA_LORA, RWKV_WIDTH), 0.5 * D_AAA_LORA ** -0.5),
        "g_lora_a": nrm((L, D, D_GATE_LORA), D ** -0.5),
        "g_lora_b": nrm((L, D_GATE_LORA, RWKV_WIDTH), D_GATE_LORA ** -0.5),
        "k_k": 0.85 + nrm((L, RWKV_WIDTH), 0.02),
        "k_a": 1.0 + nrm((L, RWKV_WIDTH), 0.02),
        "r_k": nrm((L, RWKV_HEADS, RWKV_HEAD), 0.1),
        "gn_w": gain((L, RWKV_WIDTH)),
        "gn_b": nrm((L, RWKV_WIDTH), 0.02),
        "w_out": nrm((L, MIX_WIDTH, D), MIX_WIDTH ** -0.5),
        "moe_norm_g": gain((L, D)),
        "w_group": nrm((L, D, N_GROUPS), D ** -0.5),
        "w_expert_router": nrm((L, N_GROUPS, D, EXPERTS_PER_GROUP), D ** -0.5),
        "w_gate": nrm((L, N_EXPERTS, D, D_EXPERT), D ** -0.5),
        "w_up": nrm((L, N_EXPERTS, D, D_EXPERT), D ** -0.5),
        "w_down": nrm((L, N_EXPERTS, D_EXPERT, D), D_EXPERT ** -0.5),
        "w_pl": nrm((L, PL_DIM, D), PL_DIM ** -0.5),
        "pl_norm_g": gain((L, D)),
        "pl_gate_norm_g": gain((L, D)),
        "w_pl_gate": nrm((L, D, D), D ** -0.5),
    }


def reference(x, p, positions, attn_norm_g, w_in, q_norm_g, k_norm_g, lambda_q1, lambda_k1,
              lambda_q2, lambda_k2, diff_out_g, mu_rkv, mu_wag, w0, w_lora_a, w_lora_b, a0,
              a_lora_a, a_lora_b, g_lora_a, g_lora_b, k_k, k_a, r_k, gn_w, gn_b, w_out,
              moe_norm_g, w_group, w_expert_router, w_gate, w_up, w_down, w_pl, pl_norm_g,
              pl_gate_norm_g, w_pl_gate):
    B, S, _ = x.shape
    cos, sin = rope_tables(positions)
    split_at = [DIFF_WIDTH, 2 * DIFF_WIDTH, 3 * DIFF_WIDTH,
                3 * DIFF_WIDTH + RWKV_WIDTH, 3 * DIFF_WIDTH + 2 * RWKV_WIDTH]
    for i in range(DEPTH):
        lam_init = 0.8 - 0.6 * math.exp(-0.3 * i)
        hn = rms_norm(x, attn_norm_g[i])
        z = hn @ w_in[i]
        dq, dk, dv, zr, zk, zv = jnp.split(z, split_at, axis=-1)
        q = apply_rope(rms_norm(dq.reshape(B, S, DIFF_HEADS, 2, DIFF_HEAD_DIM), q_norm_g[i]), cos, sin)
        k = apply_rope(rms_norm(dk.reshape(B, S, DIFF_HEADS, 2, DIFF_HEAD_DIM), k_norm_g[i]), cos, sin)
        v = dv.reshape(B, S, DIFF_HEADS, 2 * DIFF_HEAD_DIM)
        lam = (jnp.exp(jnp.sum(lambda_q1[i].astype(jnp.float32) * lambda_k1[i].astype(jnp.float32)))
               - jnp.exp(jnp.sum(lambda_q2[i].astype(jnp.float32) * lambda_k2[i].astype(jnp.float32)))
               + lam_init)
        o_diff = diff_attention(q, k, v, lam, diff_out_g[i], lam_init)
        o_rwkv = rwkv7_time_mix(hn, zr, zk, zv, mu_rkv[i], mu_wag[i], w0[i], w_lora_a[i],
                                w_lora_b[i], a0[i], a_lora_a[i], a_lora_b[i], g_lora_a[i],
                                g_lora_b[i], k_k[i], k_a[i], r_k[i], gn_w[i], gn_b[i])
        x = x + jnp.concatenate([o_diff, o_rwkv], axis=-1) @ w_out[i]
        x = x + hier_moe(rms_norm(x, moe_norm_g[i]), w_group[i], w_expert_router[i],
                         w_gate[i], w_up[i], w_down[i])
        gate = jax.nn.sigmoid(rms_norm(x, pl_gate_norm_g[i]) @ w_pl_gate[i])
        x = x + rms_norm(p[i] @ w_pl[i], pl_norm_g[i]) * gate
    return x
```

```python
import functools
import math

import jax
import jax.numpy as jnp
from jax import lax
from jax.experimental import pallas as pl
from jax.experimental.pallas import tpu as pltpu

F32 = jnp.float32
BF16 = jnp.bfloat16

D_MODEL = 1024
PL_DIM = 256
DIFF_WIDTH = 512
RWKV_WIDTH = 512
DIFF_HEAD_DIM = 64
DIFF_HEADS = 4
RWKV_HEAD = 64
RWKV_HEADS = 8
D_DECAY_LORA = 64
D_AAA_LORA = 64
D_GATE_LORA = 160
ROPE_THETA = 10000.0
NORM_EPS = 1e-6
SUBLN_EPS = 1e-5
GN_EPS = 64e-5
N_GROUPS = 4
EXPERTS_PER_GROUP = 8
N_EXPERTS = 32
D_EXPERT = 512
LAM_INIT = 0.8 - 0.6 * math.exp(0.0)

V7X_LANES = 128
V7X_SUBLANES = 8
V7X_VMEM_BYTES = 64 * 1024 * 1024

PROJ_ROWS = 256
ATTN_ROWS = 256
RWKV_CHUNK = 64
MOE_ROWS = 256
VMEM_LIMIT = 56 * 1024 * 1024


def _cparams(sem):
    return pltpu.CompilerParams(dimension_semantics=sem, vmem_limit_bytes=VMEM_LIMIT)


def _dot(a, b):
    return jnp.dot(a.astype(BF16), b.astype(BF16), preferred_element_type=F32)


def _dot_nt(a, b):
    return lax.dot_general(a.astype(BF16), b.astype(BF16), (((1,), (1,)), ((), ())),
                           preferred_element_type=F32)


def _dot_tn(a, b):
    return lax.dot_general(a.astype(BF16), b.astype(BF16), (((0,), (0,)), ((), ())),
                           preferred_element_type=F32)


def _split3(x):
    hi = x.astype(BF16)
    r1 = x - hi.astype(F32)
    mid = r1.astype(BF16)
    lo = (r1 - mid.astype(F32)).astype(BF16)
    return hi, mid, lo


def _dot_exact_rhs(x, m01):
    hi, mid, lo = _split3(x)
    return (jnp.dot(hi, m01, preferred_element_type=F32)
            + jnp.dot(mid, m01, preferred_element_type=F32)
            + jnp.dot(lo, m01, preferred_element_type=F32))


def _dot_exact_lhs(m01, x):
    hi, mid, lo = _split3(x)
    return (jnp.dot(m01, hi, preferred_element_type=F32)
            + jnp.dot(m01, mid, preferred_element_type=F32)
            + jnp.dot(m01, lo, preferred_element_type=F32))


def _rms(x, g, eps):
    return x * lax.rsqrt(jnp.mean(x * x, axis=-1, keepdims=True) + eps) * g


def _shift_rows(z, prev_row):
    rolled = pltpu.roll(z, 1, axis=0)
    row = lax.broadcasted_iota(jnp.int32, z.shape, 0)
    return jnp.where(row == 0, prev_row, rolled)


def _in_proj_kernel(x_ref, pos_ref, g_ref, win_ref, w1_ref, a1_ref, g1_ref, w2_ref, a2_ref, g2_ref,
                    mu_ref, qg_ref, kg_ref, invf_ref, seg_ref,
                    q_out, k_out, v_out, zr_out, zk_out, zv_out, wp_out, ap_out, gg_out,
                    carry_ref):
    tm = x_ref.shape[0]

    @pl.when(pl.program_id(1) == 0)
    def _():
        carry_ref[...] = jnp.zeros_like(carry_ref)

    hn = _rms(x_ref[...], g_ref[...], NORM_EPS)
    prev = carry_ref[V7X_SUBLANES - 1:V7X_SUBLANES, :]
    dh = _shift_rows(hn, prev) - hn
    carry_ref[...] = hn[tm - V7X_SUBLANES:tm, :]

    hb = hn.astype(BF16)
    w = DIFF_WIDTH

    def proj(c):
        return jnp.dot(hb, win_ref[:, c * w:(c + 1) * w], preferred_element_type=F32)

    ang = pos_ref[...] * invf_ref[...]
    cos1, sin1 = jnp.cos(ang), jnp.sin(ang)
    cosf = jnp.concatenate([cos1] * (w // V7X_LANES), axis=1)
    sinf = jnp.concatenate([sin1] * (w // V7X_LANES), axis=1)
    lane = lax.broadcasted_iota(jnp.int32, (tm, w), 1)
    first_half = (lane % DIFF_HEAD_DIM) < (DIFF_HEAD_DIM // 2)
    sin_signed = jnp.where(first_half, -sinf, sinf)
    half = DIFF_HEAD_DIM // 2

    def norm_rope(z, gain):
        ss = _dot_exact_rhs(z * z, seg_ref[...])
        zn = z * lax.rsqrt(ss * (1.0 / DIFF_HEAD_DIM) + NORM_EPS) * gain
        partner = jnp.where(first_half, pltpu.roll(zn, w - half, axis=1), pltpu.roll(zn, half, axis=1))
        return zn * cosf + partner * sin_signed

    scale = DIFF_HEAD_DIM ** -0.5
    q_out[...] = (norm_rope(proj(0), qg_ref[...]) * scale).astype(q_out.dtype)
    k_out[...] = norm_rope(proj(1), kg_ref[...]).astype(k_out.dtype)
    v_out[...] = proj(2).astype(v_out.dtype)
    zr_out[...] = proj(3)
    zk_out[...] = proj(4)
    zv_out[...] = proj(5)

    xw = hn + dh * mu_ref[0:1, :]
    xa = hn + dh * mu_ref[1:2, :]
    xg = hn + dh * mu_ref[2:3, :]
    wp_out[...] = _dot(jnp.tanh(_dot(xw, w1_ref[...])), w2_ref[...])
    ap_out[...] = _dot(_dot(xa, a1_ref[...]), a2_ref[...])
    gg_out[...] = _dot(jax.nn.sigmoid(_dot(xg, g1_ref[...])), g2_ref[...])


def _in_proj(x2, pos, attn_g, w_in, w1, a1, g1, w2, a2, g2, mu_wag, qg, kg, invf, seg, B, S):
    T, D = x2.shape
    tm = min(PROJ_ROWS, S)
    ns = S // tm
    w = DIFF_WIDTH
    row = lambda b, i: (b * ns + i, 0)
    full = lambda b, i: (0, 0)

    def fs(a):
        return pl.BlockSpec(a.shape, full)

    outs = ([jax.ShapeDtypeStruct((T, w), BF16)] * 3 + [jax.ShapeDtypeStruct((T, w), F32)] * 6)
    return pl.pallas_call(
        _in_proj_kernel,
        grid=(B, ns),
        in_specs=[pl.BlockSpec((tm, D), row), pl.BlockSpec((tm, 1), row), fs(attn_g), fs(w_in),
                  fs(w1), fs(a1), fs(g1), fs(w2), fs(a2), fs(g2), fs(mu_wag), fs(qg), fs(kg),
                  fs(invf), fs(seg)],
        out_specs=[pl.BlockSpec((tm, w), row)] * 9,
        out_shape=outs,
        scratch_shapes=[pltpu.VMEM((V7X_SUBLANES, D), F32)],
        compiler_params=_cparams(("arbitrary", "arbitrary")),
        name="in_proj",
    )(x2, pos, attn_g, w_in, w1, a1, g1, w2, a2, g2, mu_wag, qg, kg, invf, seg)


def _attn_kernel(lq1_ref, lk1_ref, lq2_ref, lk2_ref, og_ref, q_ref, k_ref, v_ref, o_ref,
                 m_ref, l_ref, acc_ref):
    tq = q_ref.shape[0]
    tk = tq
    i = pl.program_id(2)
    lam = (jnp.exp(jnp.sum(lq1_ref[...] * lk1_ref[...], axis=-1, keepdims=True))
           - jnp.exp(jnp.sum(lq2_ref[...] * lk2_ref[...], axis=-1, keepdims=True)) + LAM_INIT)

    q = q_ref[...]
    lane = lax.broadcasted_iota(jnp.int32, q.shape, 1)
    zero = jnp.zeros_like(q)
    qs = jnp.concatenate([jnp.where(lane < DIFF_HEAD_DIM, q, zero),
                          jnp.where(lane >= DIFF_HEAD_DIM, q, zero)], axis=0)

    m_ref[...] = jnp.full_like(m_ref, -jnp.inf)
    l_ref[...] = jnp.zeros_like(l_ref)
    acc_ref[...] = jnp.zeros_like(acc_ref)

    r_io = lax.broadcasted_iota(jnp.int32, (2 * tq, tk), 0)
    qrow = jnp.where(r_io >= tq, r_io - tq, r_io) + i * tq
    kcol0 = lax.broadcasted_iota(jnp.int32, (2 * tq, tk), 1)

    def body(j, carry):
        start = pl.multiple_of(j * tk, tk)
        kb = k_ref[pl.ds(start, tk), :]
        vb = v_ref[pl.ds(start, tk), :]
        s = lax.dot_general(qs, kb, (((1,), (1,)), ((), ())), preferred_element_type=F32)
        s = jnp.where(kcol0 + j * tk <= qrow, s, -jnp.inf)
        m_prev = m_ref[...]
        m_new = jnp.maximum(m_prev, jnp.max(s, axis=-1, keepdims=True))
        alpha = jnp.exp(m_prev - m_new)
        p = jnp.exp(s - m_new)
        l_ref[...] = alpha * l_ref[...] + jnp.sum(p, axis=-1, keepdims=True)
        acc_ref[...] = alpha * acc_ref[...] + jnp.dot(p.astype(BF16), vb, preferred_element_type=F32)
        m_ref[...] = m_new
        return carry

    lax.fori_loop(0, i + 1, body, 0)

    acc = acc_ref[...]
    l = l_ref[...]
    o = acc[:tq] / l[:tq] - lam * (acc[tq:] / l[tq:])
    o = _rms(o, og_ref[...], SUBLN_EPS) * (1.0 - LAM_INIT)
    o_ref[...] = o.astype(o_ref.dtype)


def _attention(q, k, v, lq1, lk1, lq2, lk2, og, B, S):
    T = q.shape[0]
    tq = min(ATTN_ROWS, S)
    nq = S // tq
    hw = 2 * DIFF_HEAD_DIM
    small = lambda b, h, i: (0, 0)
    return pl.pallas_call(
        _attn_kernel,
        grid=(B, DIFF_HEADS, nq),
        in_specs=[pl.BlockSpec(lq1.shape, small), pl.BlockSpec(lk1.shape, small),
                  pl.BlockSpec(lq2.shape, small), pl.BlockSpec(lk2.shape, small),
                  pl.BlockSpec(og.shape, small),
                  pl.BlockSpec((tq, hw), lambda b, h, i: (b * nq + i, h)),
                  pl.BlockSpec((S, hw), lambda b, h, i: (b, h)),
                  pl.BlockSpec((S, hw), lambda b, h, i: (b, h))],
        out_specs=pl.BlockSpec((tq, hw), lambda b, h, i: (b * nq + i, h)),
        out_shape=jax.ShapeDtypeStruct((T, DIFF_WIDTH), BF16),
        scratch_shapes=[pltpu.VMEM((2 * tq, 1), F32), pltpu.VMEM((2 * tq, 1), F32),
                        pltpu.VMEM((2 * tq, hw), F32)],
        compiler_params=_cparams(("arbitrary", "arbitrary", "arbitrary")),
        name="diff_attn",
    )(lq1, lk1, lq2, lk2, og, q, k, v)


def _rwkv_kernel(zr_ref, zk_ref, zv_ref, wp_ref, ap_ref, gg_ref, mu_ref, w0_ref, a0_ref, kk_ref,
                 ka_ref, rk_ref, gnw_ref, gnb_ref, seg_ref, o_ref, state_ref, carry_ref):
    C = zr_ref.shape[0]
    N = RWKV_HEAD
    W = RWKV_WIDTH

    @pl.when(pl.program_id(1) == 0)
    def _():
        state_ref[...] = jnp.zeros_like(state_ref)
        carry_ref[...] = jnp.zeros_like(carry_ref)

    zr, zk, zv = zr_ref[...], zk_ref[...], zv_ref[...]
    last = V7X_SUBLANES - 1
    r = zr + (_shift_rows(zr, carry_ref[last:last + 1, 0:W]) - zr) * mu_ref[0:1, :]
    k = zk + (_shift_rows(zk, carry_ref[last:last + 1, W:2 * W]) - zk) * mu_ref[1:2, :]
    v = zv + (_shift_rows(zv, carry_ref[last:last + 1, 2 * W:3 * W]) - zv) * mu_ref[2:3, :]
    carry_ref[:, 0:W] = zr[C - V7X_SUBLANES:C, :]
    carry_ref[:, W:2 * W] = zk[C - V7X_SUBLANES:C, :]
    carry_ref[:, 2 * W:3 * W] = zv[C - V7X_SUBLANES:C, :]

    lw = -math.exp(-0.5) * jax.nn.sigmoid(w0_ref[...] + wp_ref[...])
    a = jax.nn.sigmoid(a0_ref[...] + ap_ref[...])
    kk = k * kk_ref[...]
    kk = kk * lax.rsqrt(jnp.maximum(_dot_exact_rhs(kk * kk, seg_ref[...]), 1e-24))
    k2 = k * (1.0 + (a - 1.0) * ka_ref[...])
    a_s = -kk
    b_s = kk * a

    ri = lax.broadcasted_iota(jnp.int32, (C, C), 0)
    ci = lax.broadcasted_iota(jnp.int32, (C, C), 1)
    lower = ci <= ri
    strict = ci < ri
    eye_c = (ci == ri).astype(F32)
    L = _dot_exact_lhs(lower.astype(BF16), lw)
    LC = L[C - 1:C, :]
    eL = jnp.exp(L)
    enL = jnp.exp(-L)
    at = a_s * jnp.exp(L - lw)
    bt = b_s * enL
    kt = k2 * enL
    rt = r * eL
    eCL = jnp.exp(LC - L)
    bh = b_s * eCL
    kh = k2 * eCL
    pc = jnp.exp(LC)

    rn = lax.broadcasted_iota(jnp.int32, (N, N), 0)
    cn = lax.broadcasted_iota(jnp.int32, (N, N), 1)
    eye_n = rn == cn
    bonus_w = r * k2 * rk_ref[...]
    n_sq = int(math.log2(C)) - 1

    for h in range(RWKV_HEADS):
        sl = slice(h * N, (h + 1) * N)
        at_h, bt_h, kt_h, rt_h = at[:, sl], bt[:, sl], kt[:, sl], rt[:, sl]
        bh_h, kh_h, v_h = bh[:, sl], kh[:, sl], v[:, sl]
        A_ab = jnp.where(strict, _dot_nt(at_h, bt_h), 0.0)
        A_ak = jnp.where(strict, _dot_nt(at_h, kt_h), 0.0)
        A_rb = jnp.where(lower, _dot_nt(rt_h, bt_h), 0.0)
        A_rk = jnp.where(lower, _dot_nt(rt_h, kt_h), 0.0)
        Tm = eye_c + A_ab
        Np = A_ab
        for _ in range(n_sq):
            Np = _dot(Np, Np)
            Tm = Tm + _dot(Tm, Np)
        atp = _dot(Tm, at_h)
        u0 = _dot(Tm, _dot(A_ak, v_h))
        rp = rt_h + _dot(A_rb, atp)
        y0 = _dot(A_rb, u0) + _dot(A_rk, v_h)
        gm = _dot_tn(bh_h, atp)
        hm = _dot_tn(bh_h, u0) + _dot_tn(kh_h, v_h)
        st = state_ref[h]
        y = _dot(rp, st) + y0
        pc_col = jnp.sum(jnp.where(eye_n, pc[:, sl], 0.0), axis=1, keepdims=True)
        state_ref[h] = pc_col * st + _dot(gm, st) + hm

        mu = jnp.mean(y, axis=-1, keepdims=True)
        yc = y - mu
        var = jnp.mean(yc * yc, axis=-1, keepdims=True)
        yn = yc * lax.rsqrt(var + GN_EPS) * gnw_ref[:, sl] + gnb_ref[:, sl]
        bonus = jnp.sum(bonus_w[:, sl], axis=-1, keepdims=True) * v_h
        o_ref[:, sl] = ((yn + bonus) * gg_ref[:, sl]).astype(o_ref.dtype)


def _rwkv(zr, zk, zv, wp, ap, gg, mu_rkv, w0, a0, k_k, k_a, r_k, gn_w, gn_b, seg, B, S):
    T = zr.shape[0]
    C = min(RWKV_CHUNK, S)
    nc = S // C
    W = RWKV_WIDTH
    row = lambda b, i: (b * nc + i, 0)
    full = lambda b, i: (0, 0)

    def fs(a):
        return pl.BlockSpec(a.shape, full)

    return pl.pallas_call(
        _rwkv_kernel,
        grid=(B, nc),
        in_specs=[pl.BlockSpec((C, W), row)] * 6 + [fs(mu_rkv), fs(w0), fs(a0), fs(k_k), fs(k_a),
                                                    fs(r_k), fs(gn_w), fs(gn_b), fs(seg)],
        out_specs=pl.BlockSpec((C, W), row),
        out_shape=jax.ShapeDtypeStruct((T, W), BF16),
        scratch_shapes=[pltpu.VMEM((RWKV_HEADS, RWKV_HEAD, RWKV_HEAD), F32),
                        pltpu.VMEM((V7X_SUBLANES, 3 * W), F32)],
        compiler_params=_cparams(("arbitrary", "arbitrary")),
        name="rwkv7",
    )(zr, zk, zv, wp, ap, gg, mu_rkv, w0, a0, k_k, k_a, r_k, gn_w, gn_b, seg)


ROUTE_COLS = 8
ROUTER_GROUP_LANE = 0
ROUTER_EXPERT_LANE = N_GROUPS


def _out_router_kernel(x_ref, od_ref, orw_ref, wo_ref, mg_ref, wrh_ref, wrl_ref,
                       x1_out, hm_out, route_out, cnt_out, cnt_ref):
    tm = x_ref.shape[0]

    @pl.when(pl.program_id(0) == 0)
    def _():
        cnt_ref[...] = jnp.zeros_like(cnt_ref)

    x1 = (x_ref[...] + jnp.dot(od_ref[...], wo_ref[0:DIFF_WIDTH, :], preferred_element_type=F32)
          + jnp.dot(orw_ref[...], wo_ref[DIFF_WIDTH:, :], preferred_element_type=F32))
    x1_out[...] = x1
    hm = _rms(x1, mg_ref[...], NORM_EPS)
    hm_out[...] = hm

    hi = hm.astype(BF16)
    lo = (hm - hi.astype(F32)).astype(BF16)
    lg = (jnp.dot(hi, wrh_ref[...], preferred_element_type=F32)
          + jnp.dot(hi, wrl_ref[...], preferred_element_type=F32)
          + jnp.dot(lo, wrh_ref[...], preferred_element_type=F32))

    lane = lax.broadcasted_iota(jnp.int32, lg.shape, 1).astype(F32)
    big = float(V7X_LANES)
    ninf = -jnp.inf
    gmask = lane < N_GROUPS
    gmax = jnp.max(jnp.where(gmask, lg, ninf), axis=-1, keepdims=True)
    g_sel = jnp.min(jnp.where(gmask & (lg == gmax), lane, big), axis=-1, keepdims=True)
    g_w = 1.0 / jnp.sum(jnp.where(gmask, jnp.exp(lg - gmax), 0.0), axis=-1, keepdims=True)
    lo_lane = ROUTER_EXPERT_LANE + g_sel * EXPERTS_PER_GROUP
    emask = (lane >= lo_lane) & (lane < lo_lane + EXPERTS_PER_GROUP)
    v1 = jnp.max(jnp.where(emask, lg, ninf), axis=-1, keepdims=True)
    i1 = jnp.min(jnp.where(emask & (lg == v1), lane, big), axis=-1, keepdims=True)
    emask2 = emask & (lane != i1)
    v2 = jnp.max(jnp.where(emask2, lg, ninf), axis=-1, keepdims=True)
    i2 = jnp.min(jnp.where(emask2 & (lg == v2), lane, big), axis=-1, keepdims=True)
    e2x = jnp.exp(v2 - v1)
    den = 1.0 + e2x
    wt1 = (1.0 / den) * g_w
    wt2 = (e2x / den) * g_w

    oh1 = lane == i1
    oh2 = lane == i2
    oh = (oh1 | oh2).astype(F32)
    ri = lax.broadcasted_iota(jnp.int32, (tm, tm), 0)
    ci = lax.broadcasted_iota(jnp.int32, (tm, tm), 1)
    before = (ci < ri).astype(BF16)
    prefix = jnp.dot(before, oh.astype(BF16), preferred_element_type=F32) + cnt_ref[0:1, :]
    rank1 = jnp.sum(jnp.where(oh1, prefix, 0.0), axis=-1, keepdims=True)
    rank2 = jnp.sum(jnp.where(oh2, prefix, 0.0), axis=-1, keepdims=True)
    cnt_new = cnt_ref[0:1, :] + jnp.sum(oh, axis=0, keepdims=True)
    cnt_ref[...] = jnp.broadcast_to(cnt_new, cnt_ref.shape)
    cnt_out[...] = jnp.broadcast_to(cnt_new, cnt_out.shape)

    e1 = i1 - ROUTER_EXPERT_LANE
    e2 = i2 - ROUTER_EXPERT_LANE
    col = lax.broadcasted_iota(jnp.int32, (tm, ROUTE_COLS), 1)
    route = jnp.where(col == 0, e1, jnp.where(col == 1, e2, jnp.where(col == 2, wt1, jnp.where(
        col == 3, wt2, jnp.where(col == 4, rank1, jnp.where(col == 5, rank2, 0.0))))))
    route_out[...] = route


def _out_router(x2, od, orw, w_out, moe_g, wr_hi, wr_lo):
    T, D = x2.shape
    tm = min(PROJ_ROWS, T)
    row = lambda i: (i, 0)
    full = lambda i: (0, 0)

    def fs(a):
        return pl.BlockSpec(a.shape, full)

    return pl.pallas_call(
        _out_router_kernel,
        grid=(T // tm,),
        in_specs=[pl.BlockSpec((tm, D), row), pl.BlockSpec((tm, DIFF_WIDTH), row),
                  pl.BlockSpec((tm, RWKV_WIDTH), row), fs(w_out), fs(moe_g), fs(wr_hi), fs(wr_lo)],
        out_specs=[pl.BlockSpec((tm, D), row), pl.BlockSpec((tm, D), row),
                   pl.BlockSpec((tm, ROUTE_COLS), row), pl.BlockSpec((V7X_SUBLANES, V7X_LANES), full)],
        out_shape=[jax.ShapeDtypeStruct((T, D), F32), jax.ShapeDtypeStruct((T, D), F32),
                   jax.ShapeDtypeStruct((T, ROUTE_COLS), F32),
                   jax.ShapeDtypeStruct((V7X_SUBLANES, V7X_LANES), F32)],
        scratch_shapes=[pltpu.VMEM((V7X_SUBLANES, V7X_LANES), F32)],
        compiler_params=_cparams(("arbitrary",)),
        name="out_router",
    )(x2, od, orw, w_out, moe_g, wr_hi, wr_lo)


def _row_copy(src_ref, src_row, dst_ref, dst_row, sem):
    return pltpu.make_async_copy(src_ref.at[pl.ds(src_row, 1)], dst_ref.at[pl.ds(dst_row, 1)], sem)


def _dispatch_kernel(dest_ref, hm_ref, xb_in_ref, xb_ref, sem):
    del xb_in_ref
    tm = hm_ref.shape[0]

    def start(t, c):
        _row_copy(hm_ref, t, xb_ref, dest_ref[0, 0, 2 * t], sem).start()
        _row_copy(hm_ref, t, xb_ref, dest_ref[0, 0, 2 * t + 1], sem).start()
        return c

    lax.fori_loop(0, tm, start, 0)

    def wait(t, c):
        _row_copy(hm_ref, t, xb_ref, dest_ref[0, 0, 2 * t], sem).wait()
        _row_copy(hm_ref, t, xb_ref, dest_ref[0, 0, 2 * t + 1], sem).wait()
        return c

    lax.fori_loop(0, tm, wait, 0)


def _dispatch(dest, hm, xb0):
    T, D = hm.shape
    tm = min(PROJ_ROWS, T)
    return pl.pallas_call(
        _dispatch_kernel,
        grid=(T // tm,),
        in_specs=[pl.BlockSpec((1, 1, 2 * tm), lambda i: (i, 0, 0), memory_space=pltpu.SMEM),
                  pl.BlockSpec((tm, D), lambda i: (i, 0)),
                  pl.BlockSpec(memory_space=pl.ANY)],
        out_specs=pl.BlockSpec(memory_space=pl.ANY),
        out_shape=jax.ShapeDtypeStruct(xb0.shape, xb0.dtype),
        scratch_shapes=[pltpu.SemaphoreType.DMA(())],
        input_output_aliases={2: 0},
        compiler_params=_cparams(("arbitrary",)),
        name="moe_dispatch",
    )(dest, hm, xb0)


def _expert_kernel(be_ref, nu_ref, xb_ref, wg_ref, wu_ref, wd_ref, yb_ref, wgb, wub, wdb):
    i = pl.program_id(0)
    prev = be_ref[jnp.maximum(i - 1, 0)]
    changed = (i == 0) | (be_ref[i] != prev)

    @pl.when(changed)
    def _():
        wgb[...] = wg_ref[0].astype(BF16)
        wub[...] = wu_ref[0].astype(BF16)
        wdb[...] = wd_ref[0].astype(BF16)

    @pl.when(i < nu_ref[0])
    def _():
        xb = xb_ref[...].astype(BF16)
        gate = jnp.dot(xb, wgb[...], preferred_element_type=F32)
        up = jnp.dot(xb, wub[...], preferred_element_type=F32)
        hdn = (gate * jax.nn.sigmoid(gate)) * up
        yb_ref[...] = jnp.dot(hdn.astype(BF16), wdb[...], preferred_element_type=F32)

    @pl.when(i >= nu_ref[0])
    def _():
        yb_ref[...] = jnp.zeros_like(yb_ref)


def _experts(block_e, n_used, xb, w_gate, w_up, w_down):
    P, D = xb.shape
    bm = MOE_ROWS
    nb = P // bm
    E = D_EXPERT
    grid_spec = pltpu.PrefetchScalarGridSpec(
        num_scalar_prefetch=2,
        grid=(nb,),
        in_specs=[pl.BlockSpec((bm, D), lambda i, be, nu: (jnp.minimum(i, nu[0] - 1), 0)),
                  pl.BlockSpec((1, D, E), lambda i, be, nu: (be[i], 0, 0)),
                  pl.BlockSpec((1, D, E), lambda i, be, nu: (be[i], 0, 0)),
                  pl.BlockSpec((1, E, D), lambda i, be, nu: (be[i], 0, 0))],
        out_specs=pl.BlockSpec((bm, D), lambda i, be, nu: (i, 0)),
        scratch_shapes=[pltpu.VMEM((D, E), BF16), pltpu.VMEM((D, E), BF16), pltpu.VMEM((E, D), BF16)],
    )
    return pl.pallas_call(
        _expert_kernel,
        grid_spec=grid_spec,
        out_shape=jax.ShapeDtypeStruct((P, D), F32),
        compiler_params=_cparams(("arbitrary",)),
        name="moe_experts",
    )(block_e, n_used, xb, w_gate, w_up, w_down)


def _final_kernel(dest_ref, x1_ref, route_ref, p_ref, wpl_ref, plg_ref, pgg_ref, wgate_ref, yb_ref,
                  o_ref, y1_buf, y2_buf, sem):
    tm = x1_ref.shape[0]

    def start(t, c):
        _row_copy(yb_ref, dest_ref[0, 0, 2 * t], y1_buf, t, sem).start()
        _row_copy(yb_ref, dest_ref[0, 0, 2 * t + 1], y2_buf, t, sem).start()
        return c

    lax.fori_loop(0, tm, start, 0)

    pe = _rms(jnp.dot(p_ref[...].astype(BF16), wpl_ref[...], preferred_element_type=F32),
              plg_ref[...], NORM_EPS)

    def wait(t, c):
        _row_copy(yb_ref, dest_ref[0, 0, 2 * t], y1_buf, t, sem).wait()
        _row_copy(yb_ref, dest_ref[0, 0, 2 * t + 1], y2_buf, t, sem).wait()
        return c

    lax.fori_loop(0, tm, wait, 0)

    route = route_ref[...]
    x2 = x1_ref[...] + route[:, 2:3] * y1_buf[...] + route[:, 3:4] * y2_buf[...]
    gate = jax.nn.sigmoid(_dot(_rms(x2, pgg_ref[...], NORM_EPS), wgate_ref[...]))
    o_ref[...] = x2 + pe * gate


def _final(dest, x1, route, p2, w_pl, pl_g, pl_gate_g, w_gate, yb):
    T, D = x1.shape
    tm = min(PROJ_ROWS, T)
    row = lambda i: (i, 0)
    full = lambda i: (0, 0)

    def fs(a):
        return pl.BlockSpec(a.shape, full)

    return pl.pallas_call(
        _final_kernel,
        grid=(T // tm,),
        in_specs=[pl.BlockSpec((1, 1, 2 * tm), lambda i: (i, 0, 0), memory_space=pltpu.SMEM),
                  pl.BlockSpec((tm, D), row), pl.BlockSpec((tm, ROUTE_COLS), row),
                  pl.BlockSpec((tm, PL_DIM), row), fs(w_pl), fs(pl_g), fs(pl_gate_g), fs(w_gate),
                  pl.BlockSpec(memory_space=pl.ANY)],
        out_specs=pl.BlockSpec((tm, D), row),
        out_shape=jax.ShapeDtypeStruct((T, D), F32),
        scratch_shapes=[pltpu.VMEM((tm, D), F32), pltpu.VMEM((tm, D), F32),
                        pltpu.SemaphoreType.DMA(())],
        compiler_params=_cparams(("arbitrary",)),
        name="combine_final",
    )(dest, x1, route, p2, w_pl, pl_g, pl_gate_g, w_gate, yb)


def kernel(x, p, positions, attn_norm_g, w_in, q_norm_g, k_norm_g, lambda_q1, lambda_k1, lambda_q2, lambda_k2, diff_out_g, mu_rkv, mu_wag, w0, w_lora_a, w_lora_b, a0, a_lora_a, a_lora_b, g_lora_a, g_lora_b, k_k, k_a, r_k, gn_w, gn_b, w_out, moe_norm_g, w_group, w_expert_router, w_gate, w_up, w_down, w_pl, pl_norm_g, pl_gate_norm_g, w_pl_gate):
    B, S, D = x.shape
    T = B * S
    assert p.shape[0] == 1, "one layer"
    x2 = x.reshape(T, D)
    p2 = p[0].reshape(T, PL_DIM)
    pos = positions.astype(F32).reshape(T, 1)

    def row(a):
        return a.reshape(1, -1).astype(F32)

    half = DIFF_HEAD_DIM // 2
    inv_freq = ROPE_THETA ** (-jnp.arange(half, dtype=F32) / half)
    invf = jnp.tile(inv_freq, V7X_LANES // half).reshape(1, V7X_LANES)
    li = jnp.arange(DIFF_WIDTH) // DIFF_HEAD_DIM
    seg = (li[:, None] == li[None, :]).astype(BF16)
    reps = DIFF_WIDTH // DIFF_HEAD_DIM

    q, k, v, zr, zk, zv, wp, ap, gg = _in_proj(
        x2, pos, row(attn_norm_g[0]), w_in[0].astype(BF16),
        w_lora_a[0].astype(BF16), a_lora_a[0].astype(BF16), g_lora_a[0].astype(BF16),
        w_lora_b[0].astype(BF16), a_lora_b[0].astype(BF16), g_lora_b[0].astype(BF16),
        mu_wag[0].astype(F32), row(jnp.tile(q_norm_g[0], reps)), row(jnp.tile(k_norm_g[0], reps)),
        invf, seg, B, S)

    o_diff = _attention(q, k, v, row(lambda_q1[0]), row(lambda_k1[0]), row(lambda_q2[0]),
                        row(lambda_k2[0]), row(diff_out_g[0]), B, S)
    o_rwkv = _rwkv(zr, zk, zv, wp, ap, gg, mu_rkv[0].astype(F32), row(w0[0]), row(a0[0]),
                   row(k_k[0]), row(k_a[0]), row(r_k[0]), row(gn_w[0]), row(gn_b[0]), seg, B, S)

    wr = jnp.concatenate([w_group[0], jnp.transpose(w_expert_router[0], (1, 0, 2)).reshape(D, N_EXPERTS)],
                         axis=1).astype(F32)
    wr = jnp.pad(wr, ((0, 0), (0, V7X_LANES - wr.shape[1])))
    wr_hi = wr.astype(BF16)
    wr_lo = (wr - wr_hi.astype(F32)).astype(BF16)
    x1, hm, route, cnt = _out_router(x2, o_diff, o_rwkv, w_out[0].astype(BF16), row(moe_norm_g[0]),
                                     wr_hi, wr_lo)

    bm = MOE_ROWS
    counts = cnt[0, ROUTER_EXPERT_LANE:ROUTER_EXPERT_LANE + N_EXPERTS].astype(jnp.int32)
    pcounts = (counts + bm - 1) // bm * bm
    pends = jnp.cumsum(pcounts)
    pstarts = pends - pcounts
    nb = -(-(2 * T) // bm) + N_EXPERTS
    P = nb * bm
    n_used = (pends[-1] // bm).astype(jnp.int32).reshape(1)
    block_e = jnp.minimum(jnp.searchsorted(pends, jnp.arange(nb, dtype=jnp.int32) * bm, side='right'),
                          N_EXPERTS - 1).astype(jnp.int32)
    eid = route[:, 0:2].astype(jnp.int32)
    tm = min(PROJ_ROWS, T)
    dest = (pstarts[eid] + route[:, 4:6].astype(jnp.int32)).reshape(T // tm, 1, 2 * tm)

    xb = _dispatch(dest, hm, jnp.zeros((P, D), F32))
    yb = _experts(block_e, n_used, xb, w_gate[0], w_up[0], w_down[0])
    out = _final(dest, x1, route, p2, w_pl[0].astype(BF16), row(pl_norm_g[0]), row(pl_gate_norm_g[0]),
                 w_pl_gate[0].astype(BF16), yb)
    return out.reshape(B, S, D)
```

```python
import functools
import math

import jax
import jax.numpy as jnp
from jax import lax
from jax.experimental import pallas as pl
from jax.experimental.pallas import tpu as pltpu

F32 = jnp.float32
BF16 = jnp.bfloat16

D_MODEL = 1024
PL_DIM = 256
DIFF_WIDTH = 512
RWKV_WIDTH = 512
DIFF_HEAD_DIM = 64
DIFF_HEADS = 4
RWKV_HEAD = 64
RWKV_HEADS = 8
D_DECAY_LORA = 64
D_AAA_LORA = 64
D_GATE_LORA = 160
ROPE_THETA = 10000.0
NORM_EPS = 1e-6
SUBLN_EPS = 1e-5
GN_EPS = 64e-5
N_GROUPS = 4
EXPERTS_PER_GROUP = 8
N_EXPERTS = 32
D_EXPERT = 512
LAM_INIT = 0.8 - 0.6 * math.exp(0.0)

V7X_LANES = 128
V7X_SUBLANES = 8
V7X_VMEM_BYTES = 64 * 1024 * 1024

PROJ_ROWS = 256
ATTN_ROWS = 256
RWKV_CHUNK = 64
RWKV_ROWS = 256
MOE_ROWS = 256
VMEM_LIMIT = 56 * 1024 * 1024


def _cparams(sem):
    return pltpu.CompilerParams(dimension_semantics=sem, vmem_limit_bytes=VMEM_LIMIT)


def _dot(a, b):
    return jnp.dot(a.astype(BF16), b.astype(BF16), preferred_element_type=F32)


def _dot_nt(a, b):
    return lax.dot_general(a.astype(BF16), b.astype(BF16), (((1,), (1,)), ((), ())),
                           preferred_element_type=F32)


def _dot_tn(a, b):
    return lax.dot_general(a.astype(BF16), b.astype(BF16), (((0,), (0,)), ((), ())),
                           preferred_element_type=F32)


def _split3(x):
    hi = x.astype(BF16)
    r1 = x - hi.astype(F32)
    mid = r1.astype(BF16)
    lo = (r1 - mid.astype(F32)).astype(BF16)
    return hi, mid, lo


def _dot_exact_rhs(x, m01):
    hi, mid, lo = _split3(x)
    return (jnp.dot(hi, m01, preferred_element_type=F32)
            + jnp.dot(mid, m01, preferred_element_type=F32)
            + jnp.dot(lo, m01, preferred_element_type=F32))


def _dot_exact_lhs(m01, x):
    hi, mid, lo = _split3(x)
    return (jnp.dot(m01, hi, preferred_element_type=F32)
            + jnp.dot(m01, mid, preferred_element_type=F32)
            + jnp.dot(m01, lo, preferred_element_type=F32))


def _rms(x, g, eps):
    return x * lax.rsqrt(jnp.mean(x * x, axis=-1, keepdims=True) + eps) * g


def _shift_rows(z, prev_row):
    rolled = pltpu.roll(z, 1, axis=0)
    row = lax.broadcasted_iota(jnp.int32, z.shape, 0)
    return jnp.where(row == 0, prev_row, rolled)


def _in_proj_kernel(x_ref, pos_ref, g_ref, win_ref, w1_ref, a1_ref, g1_ref, w2_ref, a2_ref, g2_ref,
                    mu_ref, qg_ref, kg_ref, invf_ref, seg_ref,
                    q_out, k_out, v_out, zr_out, zk_out, zv_out, wp_out, ap_out, gg_out,
                    carry_ref):
    tm = x_ref.shape[0]

    @pl.when(pl.program_id(1) == 0)
    def _():
        carry_ref[...] = jnp.zeros_like(carry_ref)

    hn = _rms(x_ref[...], g_ref[...], NORM_EPS)
    prev = carry_ref[V7X_SUBLANES - 1:V7X_SUBLANES, :]
    dh = _shift_rows(hn, prev) - hn
    carry_ref[...] = hn[tm - V7X_SUBLANES:tm, :]

    hb = hn.astype(BF16)
    w = DIFF_WIDTH

    def proj(c):
        return jnp.dot(hb, win_ref[:, c * w:(c + 1) * w], preferred_element_type=F32)

    ang = pos_ref[...] * invf_ref[...]
    cos1, sin1 = jnp.cos(ang), jnp.sin(ang)
    cosf = jnp.concatenate([cos1] * (w // V7X_LANES), axis=1)
    sinf = jnp.concatenate([sin1] * (w // V7X_LANES), axis=1)
    lane = lax.broadcasted_iota(jnp.int32, (tm, w), 1)
    first_half = (lane % DIFF_HEAD_DIM) < (DIFF_HEAD_DIM // 2)
    sin_signed = jnp.where(first_half, -sinf, sinf)
    half = DIFF_HEAD_DIM // 2

    def norm_rope(z, gain):
        ss = _dot_exact_rhs(z * z, seg_ref[...])
        zn = z * lax.rsqrt(ss * (1.0 / DIFF_HEAD_DIM) + NORM_EPS) * gain
        partner = jnp.where(first_half, pltpu.roll(zn, w - half, axis=1), pltpu.roll(zn, half, axis=1))
        return zn * cosf + partner * sin_signed

    scale = DIFF_HEAD_DIM ** -0.5
    q_out[...] = (norm_rope(proj(0), qg_ref[...]) * scale).astype(q_out.dtype)
    k_out[...] = norm_rope(proj(1), kg_ref[...]).astype(k_out.dtype)
    v_out[...] = proj(2).astype(v_out.dtype)
    zr_out[...] = proj(3)
    zk_out[...] = proj(4)
    zv_out[...] = proj(5)

    xw = hn + dh * mu_ref[0:1, :]
    xa = hn + dh * mu_ref[1:2, :]
    xg = hn + dh * mu_ref[2:3, :]
    wp_out[...] = _dot(jnp.tanh(_dot(xw, w1_ref[...])), w2_ref[...])
    ap_out[...] = _dot(_dot(xa, a1_ref[...]), a2_ref[...])
    gg_out[...] = _dot(jax.nn.sigmoid(_dot(xg, g1_ref[...])), g2_ref[...])


def _in_proj(x2, pos, attn_g, w_in, w1, a1, g1, w2, a2, g2, mu_wag, qg, kg, invf, seg, B, S):
    T, D = x2.shape
    tm = min(PROJ_ROWS, S)
    ns = S // tm
    w = DIFF_WIDTH
    row = lambda b, i: (b * ns + i, 0)
    full = lambda b, i: (0, 0)

    def fs(a):
        return pl.BlockSpec(a.shape, full)

    outs = ([jax.ShapeDtypeStruct((T, w), BF16)] * 3 + [jax.ShapeDtypeStruct((T, w), F32)] * 6)
    return pl.pallas_call(
        _in_proj_kernel,
        grid=(B, ns),
        in_specs=[pl.BlockSpec((tm, D), row), pl.BlockSpec((tm, 1), row), fs(attn_g), fs(w_in),
                  fs(w1), fs(a1), fs(g1), fs(w2), fs(a2), fs(g2), fs(mu_wag), fs(qg), fs(kg),
                  fs(invf), fs(seg)],
        out_specs=[pl.BlockSpec((tm, w), row)] * 9,
        out_shape=outs,
        scratch_shapes=[pltpu.VMEM((V7X_SUBLANES, D), F32)],
        compiler_params=_cparams(("arbitrary", "arbitrary")),
        name="in_proj",
    )(x2, pos, attn_g, w_in, w1, a1, g1, w2, a2, g2, mu_wag, qg, kg, invf, seg)


def _attn_kernel(lq1_ref, lk1_ref, lq2_ref, lk2_ref, ogt_ref, q_ref, k_ref, v_ref, o_ref,
                 vt_ref, m_ref, l_ref, acc_ref):
    tq = q_ref.shape[0]
    tk = tq
    nk = vt_ref.shape[0]
    i = pl.program_id(2)
    lam = (jnp.exp(jnp.sum(lq1_ref[...] * lk1_ref[...], axis=-1, keepdims=True))
           - jnp.exp(jnp.sum(lq2_ref[...] * lk2_ref[...], axis=-1, keepdims=True)) + LAM_INIT)

    @pl.when(i == 0)
    def _():
        for c in range(nk):
            vt_ref[c] = v_ref[c * tk:(c + 1) * tk, :].astype(F32).T.astype(vt_ref.dtype)

    q = q_ref[...]
    lane = lax.broadcasted_iota(jnp.int32, q.shape, 1)
    zero = jnp.zeros_like(q)
    qs = jnp.concatenate([jnp.where(lane < DIFF_HEAD_DIM, q, zero),
                          jnp.where(lane >= DIFF_HEAD_DIM, q, zero)], axis=0)

    m_ref[...] = jnp.full_like(m_ref, -jnp.inf)
    l_ref[...] = jnp.zeros_like(l_ref)
    acc_ref[...] = jnp.zeros_like(acc_ref)

    def step(j, masked):
        start = pl.multiple_of(j * tk, tk)
        kb = k_ref[pl.ds(start, tk), :]
        st = lax.dot_general(kb, qs, (((1,), (1,)), ((), ())), preferred_element_type=F32)
        if masked:
            krow = lax.broadcasted_iota(jnp.int32, st.shape, 0)
            qcol = lax.broadcasted_iota(jnp.int32, st.shape, 1)
            qcol = jnp.where(qcol >= tq, qcol - tq, qcol)
            st = jnp.where(krow <= qcol, st, -jnp.inf)
        m_prev = m_ref[...]
        m_new = jnp.maximum(m_prev, jnp.max(st, axis=0, keepdims=True))
        alpha = jnp.exp(m_prev - m_new)
        pt = jnp.exp(st - m_new)
        l_ref[...] = alpha * l_ref[...] + jnp.sum(pt, axis=0, keepdims=True)
        acc_ref[...] = alpha * acc_ref[...] + jnp.dot(vt_ref[j], pt.astype(BF16),
                                                      preferred_element_type=F32)
        m_ref[...] = m_new

    def body(j, carry):
        step(j, False)
        return carry

    lax.fori_loop(0, i, body, 0)
    step(i, True)

    acc = acc_ref[...]
    l = l_ref[...]
    ot = acc[:, :tq] / l[:, :tq] - lam * (acc[:, tq:] / l[:, tq:])
    ot = ot * lax.rsqrt(jnp.mean(ot * ot, axis=0, keepdims=True) + SUBLN_EPS) * ogt_ref[...]
    o_ref[...] = (ot * (1.0 - LAM_INIT)).T.astype(o_ref.dtype)


def _attention(q, k, v, lq1, lk1, lq2, lk2, og, B, S):
    T = q.shape[0]
    tq = min(ATTN_ROWS, S)
    nq = S // tq
    hw = 2 * DIFF_HEAD_DIM
    small = lambda b, h, i: (0, 0)
    return pl.pallas_call(
        _attn_kernel,
        grid=(B, DIFF_HEADS, nq),
        in_specs=[pl.BlockSpec(lq1.shape, small), pl.BlockSpec(lk1.shape, small),
                  pl.BlockSpec(lq2.shape, small), pl.BlockSpec(lk2.shape, small),
                  pl.BlockSpec(og.shape, small),
                  pl.BlockSpec((tq, hw), lambda b, h, i: (b * nq + i, h)),
                  pl.BlockSpec((S, hw), lambda b, h, i: (b, h)),
                  pl.BlockSpec((S, hw), lambda b, h, i: (b, h))],
        out_specs=pl.BlockSpec((tq, hw), lambda b, h, i: (b * nq + i, h)),
        out_shape=jax.ShapeDtypeStruct((T, DIFF_WIDTH), BF16),
        scratch_shapes=[pltpu.VMEM((nq, hw, tq), BF16), pltpu.VMEM((1, 2 * tq), F32),
                        pltpu.VMEM((1, 2 * tq), F32), pltpu.VMEM((hw, 2 * tq), F32)],
        compiler_params=_cparams(("arbitrary", "arbitrary", "arbitrary")),
        name="diff_attn",
    )(lq1, lk1, lq2, lk2, og, q, k, v)


def _rwkv_kernel(zr_ref, zk_ref, zv_ref, wp_ref, ap_ref, gg_ref, mu_ref, w0_ref, a0_ref, kk_ref,
                 ka_ref, rk_ref, gnw_ref, gnb_ref, seg_ref, o_ref, state_ref, carry_ref):
    R = zr_ref.shape[0]
    C = min(RWKV_CHUNK, R)
    nch = R // C
    N = RWKV_HEAD
    W = RWKV_WIDTH
    HP = 2 * N
    n_pairs = W // HP

    @pl.when(pl.program_id(1) == 0)
    def _():
        state_ref[...] = jnp.zeros_like(state_ref)
        carry_ref[...] = jnp.zeros_like(carry_ref)

    zr, zk, zv = zr_ref[...], zk_ref[...], zv_ref[...]
    last = V7X_SUBLANES - 1
    r = zr + (_shift_rows(zr, carry_ref[last:last + 1, 0:W]) - zr) * mu_ref[0:1, :]
    k = zk + (_shift_rows(zk, carry_ref[last:last + 1, W:2 * W]) - zk) * mu_ref[1:2, :]
    v = zv + (_shift_rows(zv, carry_ref[last:last + 1, 2 * W:3 * W]) - zv) * mu_ref[2:3, :]
    carry_ref[:, 0:W] = zr[R - V7X_SUBLANES:R, :]
    carry_ref[:, W:2 * W] = zk[R - V7X_SUBLANES:R, :]
    carry_ref[:, 2 * W:3 * W] = zv[R - V7X_SUBLANES:R, :]

    lw = -math.exp(-0.5) * jax.nn.sigmoid(w0_ref[...] + wp_ref[...])
    a = jax.nn.sigmoid(a0_ref[...] + ap_ref[...])
    kk = k * kk_ref[...]
    kk = kk * lax.rsqrt(jnp.maximum(_dot_exact_rhs(kk * kk, seg_ref[...]), 1e-24))
    k2 = k * (1.0 + (a - 1.0) * ka_ref[...])
    a_s = -kk
    b_s = kk * a

    rr = lax.broadcasted_iota(jnp.int32, (R, R), 0)
    cc = lax.broadcasted_iota(jnp.int32, (R, R), 1)
    same_chunk = (rr // C) == (cc // C)
    L = _dot_exact_lhs((same_chunk & (cc <= rr)).astype(BF16), lw)
    bonus_w = r * k2 * rk_ref[...]

    P2 = 2 * C
    sr = lax.broadcasted_iota(jnp.int32, (P2, HP), 0)
    sc = lax.broadcasted_iota(jnp.int32, (P2, HP), 1)
    stack_mask = (sr < C) == (sc < N)
    br = lax.broadcasted_iota(jnp.int32, (P2, P2), 0)
    bc = lax.broadcasted_iota(jnp.int32, (P2, P2), 1)
    same_head = (br < C) == (bc < C)
    tr = jnp.where(br >= C, br - C, br)
    tc = jnp.where(bc >= C, bc - C, bc)
    strict = same_head & (tc < tr)
    lower = same_head & (tc <= tr)
    eye_p = (br == bc).astype(F32)
    kr = lax.broadcasted_iota(jnp.int32, (HP, HP), 0)
    kc = lax.broadcasted_iota(jnp.int32, (HP, HP), 1)
    eye_k = kr == kc

    def dup(x):
        return jnp.concatenate([x, x], axis=0)

    def stack(x):
        return jnp.where(stack_mask, dup(x), 0.0)

    n_sq = int(math.log2(C)) - 1
    items = [(c, p) for c in range(nch) for p in range(n_pairs)]

    def prep(c, p):
        rows = slice(c * C, (c + 1) * C)
        lanes = slice(p * HP, (p + 1) * HP)
        Lc = L[rows, lanes]
        lwc = lw[rows, lanes]
        LC = Lc[C - 1:C, :]
        enL = jnp.exp(-Lc)
        eCL = jnp.exp(LC - Lc)
        b_c, k_c = b_s[rows, lanes], k2[rows, lanes]
        return dict(
            xa=stack(a_s[rows, lanes] * jnp.exp(Lc - lwc)), xr=stack(r[rows, lanes] * jnp.exp(Lc)),
            bt=dup(b_c * enL), kt=dup(k_c * enL), bh=stack(b_c * eCL), kh=stack(k_c * eCL),
            vs=stack(v[rows, lanes]), pc=jnp.exp(LC))

    d = [prep(c, p) for c, p in items]
    aa = [_dot_nt(jnp.concatenate([e["xa"], e["xr"]], axis=0),
                  jnp.concatenate([e["bt"], e["kt"]], axis=0)) for e in d]
    a_ab = [jnp.where(strict, m[:P2, :P2], 0.0) for m in aa]
    a_ak = [jnp.where(strict, m[:P2, P2:], 0.0) for m in aa]
    a_rb = [jnp.where(lower, m[P2:, :P2], 0.0) for m in aa]
    a_rk = [jnp.where(lower, m[P2:, P2:], 0.0) for m in aa]
    tm = [eye_p + m for m in a_ab]
    npw = a_ab
    for _ in range(n_sq):
        npw = [_dot(m, m) for m in npw]
        tm = [t + _dot(t, m) for t, m in zip(tm, npw)]
    av = [_dot(m, e["vs"]) for m, e in zip(a_ak, d)]
    z = [_dot(t, jnp.concatenate([e["xa"], x], axis=1)) for t, e, x in zip(tm, d, av)]
    w = [_dot(m, x) for m, x in zip(a_rb, z)]
    rkv = [_dot(m, e["vs"]) for m, e in zip(a_rk, d)]
    gh = [_dot_tn(e["bh"], x) for e, x in zip(d, z)]
    khv = [_dot_tn(e["kh"], e["vs"]) for e in d]

    for idx, (c, p) in enumerate(items):
        e = d[idx]
        rows = slice(c * C, (c + 1) * C)
        lanes = slice(p * HP, (p + 1) * HP)
        rp = e["xr"] + w[idx][:, :HP]
        y0 = w[idx][:, HP:] + rkv[idx]
        gm = gh[idx][:, :HP]
        hm = gh[idx][:, HP:] + khv[idx]
        st = state_ref[p]
        yg = _dot(jnp.concatenate([rp, gm], axis=0), st)
        pc_col = jnp.sum(jnp.where(eye_k, e["pc"], 0.0), axis=1, keepdims=True)
        state_ref[p] = pc_col * st + yg[P2:] + hm
        ys = yg[:P2] + y0
        mu = jnp.sum(ys, axis=-1, keepdims=True) * (1.0 / N)
        yc = jnp.where(stack_mask, ys - mu, 0.0)
        var = jnp.sum(yc * yc, axis=-1, keepdims=True) * (1.0 / N)
        yn = yc * lax.rsqrt(var + GN_EPS)
        bonus = jnp.sum(stack(bonus_w[rows, lanes]), axis=-1, keepdims=True) * e["vs"]
        yn = yn[:C] + yn[C:]
        bonus = bonus[:C] + bonus[C:]
        o_ref[rows, lanes] = ((yn * gnw_ref[:, lanes] + gnb_ref[:, lanes] + bonus)
                              * gg_ref[rows, lanes]).astype(o_ref.dtype)


def _rwkv(zr, zk, zv, wp, ap, gg, mu_rkv, w0, a0, k_k, k_a, r_k, gn_w, gn_b, seg, B, S):
    T = zr.shape[0]
    C = min(RWKV_ROWS, S)
    nc = S // C
    W = RWKV_WIDTH
    HP = 2 * RWKV_HEAD
    row = lambda b, i: (b * nc + i, 0)
    full = lambda b, i: (0, 0)

    def fs(a):
        return pl.BlockSpec(a.shape, full)

    return pl.pallas_call(
        _rwkv_kernel,
        grid=(B, nc),
        in_specs=[pl.BlockSpec((C, W), row)] * 6 + [fs(mu_rkv), fs(w0), fs(a0), fs(k_k), fs(k_a),
                                                    fs(r_k), fs(gn_w), fs(gn_b), fs(seg)],
        out_specs=pl.BlockSpec((C, W), row),
        out_shape=jax.ShapeDtypeStruct((T, W), BF16),
        scratch_shapes=[pltpu.VMEM((W // HP, HP, HP), F32),
                        pltpu.VMEM((V7X_SUBLANES, 3 * W), F32)],
        compiler_params=_cparams(("arbitrary", "arbitrary")),
        name="rwkv7",
    )(zr, zk, zv, wp, ap, gg, mu_rkv, w0, a0, k_k, k_a, r_k, gn_w, gn_b, seg)


ROUTE_COLS = 8
ROUTER_GROUP_LANE = 0
ROUTER_EXPERT_LANE = N_GROUPS


def _out_router_kernel(x_ref, od_ref, orw_ref, wo_ref, mg_ref, wrh_ref, wrl_ref,
                       x1_out, hm_out, route_out, cnt_out, cnt_ref):
    tm = x_ref.shape[0]

    @pl.when(pl.program_id(0) == 0)
    def _():
        cnt_ref[...] = jnp.zeros_like(cnt_ref)

    x1 = (x_ref[...] + jnp.dot(od_ref[...], wo_ref[0:DIFF_WIDTH, :], preferred_element_type=F32)
          + jnp.dot(orw_ref[...], wo_ref[DIFF_WIDTH:, :], preferred_element_type=F32))
    x1_out[...] = x1
    hm = _rms(x1, mg_ref[...], NORM_EPS)
    hm_out[...] = hm

    hi = hm.astype(BF16)
    lo = (hm - hi.astype(F32)).astype(BF16)
    lg = (jnp.dot(hi, wrh_ref[...], preferred_element_type=F32)
          + jnp.dot(hi, wrl_ref[...], preferred_element_type=F32)
          + jnp.dot(lo, wrh_ref[...], preferred_element_type=F32))

    lane = lax.broadcasted_iota(jnp.int32, lg.shape, 1).astype(F32)
    big = float(V7X_LANES)
    ninf = -jnp.inf
    gmask = lane < N_GROUPS
    gmax = jnp.max(jnp.where(gmask, lg, ninf), axis=-1, keepdims=True)
    g_sel = jnp.min(jnp.where(gmask & (lg == gmax), lane, big), axis=-1, keepdims=True)
    g_w = 1.0 / jnp.sum(jnp.where(gmask, jnp.exp(lg - gmax), 0.0), axis=-1, keepdims=True)
    lo_lane = ROUTER_EXPERT_LANE + g_sel * EXPERTS_PER_GROUP
    emask = (lane >= lo_lane) & (lane < lo_lane + EXPERTS_PER_GROUP)
    v1 = jnp.max(jnp.where(emask, lg, ninf), axis=-1, keepdims=True)
    i1 = jnp.min(jnp.where(emask & (lg == v1), lane, big), axis=-1, keepdims=True)
    emask2 = emask & (lane != i1)
    v2 = jnp.max(jnp.where(emask2, lg, ninf), axis=-1, keepdims=True)
    i2 = jnp.min(jnp.where(emask2 & (lg == v2), lane, big), axis=-1, keepdims=True)
    e2x = jnp.exp(v2 - v1)
    den = 1.0 + e2x
    wt1 = (1.0 / den) * g_w
    wt2 = (e2x / den) * g_w

    oh1 = lane == i1
    oh2 = lane == i2
    oh = (oh1 | oh2).astype(F32)
    ri = lax.broadcasted_iota(jnp.int32, (tm, tm), 0)
    ci = lax.broadcasted_iota(jnp.int32, (tm, tm), 1)
    before = (ci < ri).astype(BF16)
    prefix = jnp.dot(before, oh.astype(BF16), preferred_element_type=F32) + cnt_ref[0:1, :]
    rank1 = jnp.sum(jnp.where(oh1, prefix, 0.0), axis=-1, keepdims=True)
    rank2 = jnp.sum(jnp.where(oh2, prefix, 0.0), axis=-1, keepdims=True)
    cnt_new = cnt_ref[0:1, :] + jnp.sum(oh, axis=0, keepdims=True)
    cnt_ref[...] = jnp.broadcast_to(cnt_new, cnt_ref.shape)
    cnt_out[...] = jnp.broadcast_to(cnt_new, cnt_out.shape)

    e1 = i1 - ROUTER_EXPERT_LANE
    e2 = i2 - ROUTER_EXPERT_LANE
    col = lax.broadcasted_iota(jnp.int32, (tm, ROUTE_COLS), 1)
    route = jnp.where(col == 0, e1, jnp.where(col == 1, e2, jnp.where(col == 2, wt1, jnp.where(
        col == 3, wt2, jnp.where(col == 4, rank1, jnp.where(col == 5, rank2, 0.0))))))
    route_out[...] = route


def _out_router(x2, od, orw, w_out, moe_g, wr_hi, wr_lo):
    T, D = x2.shape
    tm = min(PROJ_ROWS, T)
    row = lambda i: (i, 0)
    full = lambda i: (0, 0)

    def fs(a):
        return pl.BlockSpec(a.shape, full)

    return pl.pallas_call(
        _out_router_kernel,
        grid=(T // tm,),
        in_specs=[pl.BlockSpec((tm, D), row), pl.BlockSpec((tm, DIFF_WIDTH), row),
                  pl.BlockSpec((tm, RWKV_WIDTH), row), fs(w_out), fs(moe_g), fs(wr_hi), fs(wr_lo)],
        out_specs=[pl.BlockSpec((tm, D), row), pl.BlockSpec((tm, D), row),
                   pl.BlockSpec((tm, ROUTE_COLS), row), pl.BlockSpec((V7X_SUBLANES, V7X_LANES), full)],
        out_shape=[jax.ShapeDtypeStruct((T, D), F32), jax.ShapeDtypeStruct((T, D), F32),
                   jax.ShapeDtypeStruct((T, ROUTE_COLS), F32),
                   jax.ShapeDtypeStruct((V7X_SUBLANES, V7X_LANES), F32)],
        scratch_shapes=[pltpu.VMEM((V7X_SUBLANES, V7X_LANES), F32)],
        compiler_params=_cparams(("arbitrary",)),
        name="out_router",
    )(x2, od, orw, w_out, moe_g, wr_hi, wr_lo)


def _row_copy(src_ref, src_row, dst_ref, dst_row, sem):
    return pltpu.make_async_copy(src_ref.at[pl.ds(src_row, 1)], dst_ref.at[pl.ds(dst_row, 1)], sem)


def _dispatch_kernel(dest_ref, hm_ref, xb_in_ref, xb_ref, sem):
    del xb_in_ref
    tm = hm_ref.shape[0]

    def start(t, c):
        _row_copy(hm_ref, t, xb_ref, dest_ref[0, 0, 2 * t], sem).start()
        _row_copy(hm_ref, t, xb_ref, dest_ref[0, 0, 2 * t + 1], sem).start()
        return c

    lax.fori_loop(0, tm, start, 0)

    def wait(t, c):
        _row_copy(hm_ref, t, xb_ref, dest_ref[0, 0, 2 * t], sem).wait()
        _row_copy(hm_ref, t, xb_ref, dest_ref[0, 0, 2 * t + 1], sem).wait()
        return c

    lax.fori_loop(0, tm, wait, 0)


def _dispatch(dest, hm, xb0):
    T, D = hm.shape
    tm = min(PROJ_ROWS, T)
    return pl.pallas_call(
        _dispatch_kernel,
        grid=(T // tm,),
        in_specs=[pl.BlockSpec((1, 1, 2 * tm), lambda i: (i, 0, 0), memory_space=pltpu.SMEM),
                  pl.BlockSpec((tm, D), lambda i: (i, 0)),
                  pl.BlockSpec(memory_space=pl.ANY)],
        out_specs=pl.BlockSpec(memory_space=pl.ANY),
        out_shape=jax.ShapeDtypeStruct(xb0.shape, xb0.dtype),
        scratch_shapes=[pltpu.SemaphoreType.DMA(())],
        input_output_aliases={2: 0},
        compiler_params=_cparams(("arbitrary",)),
        name="moe_dispatch",
    )(dest, hm, xb0)


def _expert_kernel(be_ref, nu_ref, xb_ref, wg_ref, wu_ref, wd_ref, yb_ref, wgb, wub, wdb):
    i = pl.program_id(0)
    prev = be_ref[jnp.maximum(i - 1, 0)]
    changed = (i == 0) | (be_ref[i] != prev)

    @pl.when(changed)
    def _():
        wgb[...] = wg_ref[0].astype(BF16)
        wub[...] = wu_ref[0].astype(BF16)
        wdb[...] = wd_ref[0].astype(BF16)

    @pl.when(i < nu_ref[0])
    def _():
        xb = xb_ref[...].astype(BF16)
        gate = jnp.dot(xb, wgb[...], preferred_element_type=F32)
        up = jnp.dot(xb, wub[...], preferred_element_type=F32)
        hdn = (gate * jax.nn.sigmoid(gate)) * up
        yb_ref[...] = jnp.dot(hdn.astype(BF16), wdb[...], preferred_element_type=F32)

    @pl.when(i >= nu_ref[0])
    def _():
        yb_ref[...] = jnp.zeros_like(yb_ref)


def _experts(block_e, n_used, xb, w_gate, w_up, w_down):
    P, D = xb.shape
    bm = MOE_ROWS
    nb = P // bm
    E = D_EXPERT
    grid_spec = pltpu.PrefetchScalarGridSpec(
        num_scalar_prefetch=2,
        grid=(nb,),
        in_specs=[pl.BlockSpec((bm, D), lambda i, be, nu: (jnp.minimum(i, nu[0] - 1), 0)),
                  pl.BlockSpec((1, D, E), lambda i, be, nu: (be[i], 0, 0)),
                  pl.BlockSpec((1, D, E), lambda i, be, nu: (be[i], 0, 0)),
                  pl.BlockSpec((1, E, D), lambda i, be, nu: (be[i], 0, 0))],
        out_specs=pl.BlockSpec((bm, D), lambda i, be, nu: (i, 0)),
        scratch_shapes=[pltpu.VMEM((D, E), BF16), pltpu.VMEM((D, E), BF16), pltpu.VMEM((E, D), BF16)],
    )
    return pl.pallas_call(
        _expert_kernel,
        grid_spec=grid_spec,
        out_shape=jax.ShapeDtypeStruct((P, D), F32),
        compiler_params=_cparams(("arbitrary",)),
        name="moe_experts",
    )(block_e, n_used, xb, w_gate, w_up, w_down)


def _final_kernel(dest_ref, x1_ref, route_ref, p_ref, wpl_ref, plg_ref, pgg_ref, wgate_ref, yb_ref,
                  o_ref, y1_buf, y2_buf, sem):
    tm = x1_ref.shape[0]

    def start(t, c):
        _row_copy(yb_ref, dest_ref[0, 0, 2 * t], y1_buf, t, sem).start()
        _row_copy(yb_ref, dest_ref[0, 0, 2 * t + 1], y2_buf, t, sem).start()
        return c

    lax.fori_loop(0, tm, start, 0)

    pe = _rms(jnp.dot(p_ref[...].astype(BF16), wpl_ref[...], preferred_element_type=F32),
              plg_ref[...], NORM_EPS)

    def wait(t, c):
        _row_copy(yb_ref, dest_ref[0, 0, 2 * t], y1_buf, t, sem).wait()
        _row_copy(yb_ref, dest_ref[0, 0, 2 * t + 1], y2_buf, t, sem).wait()
        return c

    lax.fori_loop(0, tm, wait, 0)

    route = route_ref[...]
    x2 = x1_ref[...] + route[:, 2:3] * y1_buf[...] + route[:, 3:4] * y2_buf[...]
    gate = jax.nn.sigmoid(_dot(_rms(x2, pgg_ref[...], NORM_EPS), wgate_ref[...]))
    o_ref[...] = x2 + pe * gate


def _final(dest, x1, route, p2, w_pl, pl_g, pl_gate_g, w_gate, yb):
    T, D = x1.shape
    tm = min(PROJ_ROWS, T)
    row = lambda i: (i, 0)
    full = lambda i: (0, 0)

    def fs(a):
        return pl.BlockSpec(a.shape, full)

    return pl.pallas_call(
        _final_kernel,
        grid=(T // tm,),
        in_specs=[pl.BlockSpec((1, 1, 2 * tm), lambda i: (i, 0, 0), memory_space=pltpu.SMEM),
                  pl.BlockSpec((tm, D), row), pl.BlockSpec((tm, ROUTE_COLS), row),
                  pl.BlockSpec((tm, PL_DIM), row), fs(w_pl), fs(pl_g), fs(pl_gate_g), fs(w_gate),
                  pl.BlockSpec(memory_space=pl.ANY)],
        out_specs=pl.BlockSpec((tm, D), row),
        out_shape=jax.ShapeDtypeStruct((T, D), F32),
        scratch_shapes=[pltpu.VMEM((tm, D), F32), pltpu.VMEM((tm, D), F32),
                        pltpu.SemaphoreType.DMA(())],
        compiler_params=_cparams(("arbitrary",)),
        name="combine_final",
    )(dest, x1, route, p2, w_pl, pl_g, pl_gate_g, w_gate, yb)


def kernel(x, p, positions, attn_norm_g, w_in, q_norm_g, k_norm_g, lambda_q1, lambda_k1, lambda_q2, lambda_k2, diff_out_g, mu_rkv, mu_wag, w0, w_lora_a, w_lora_b, a0, a_lora_a, a_lora_b, g_lora_a, g_lora_b, k_k, k_a, r_k, gn_w, gn_b, w_out, moe_norm_g, w_group, w_expert_router, w_gate, w_up, w_down, w_pl, pl_norm_g, pl_gate_norm_g, w_pl_gate):
    B, S, D = x.shape
    T = B * S
    assert p.shape[0] == 1, "one layer"
    x2 = x.reshape(T, D)
    p2 = p[0].reshape(T, PL_DIM)
    pos = positions.astype(F32).reshape(T, 1)

    def row(a):
        return a.reshape(1, -1).astype(F32)

    half = DIFF_HEAD_DIM // 2
    inv_freq = ROPE_THETA ** (-jnp.arange(half, dtype=F32) / half)
    invf = jnp.tile(inv_freq, V7X_LANES // half).reshape(1, V7X_LANES)
    li = jnp.arange(DIFF_WIDTH) // DIFF_HEAD_DIM
    seg = (li[:, None] == li[None, :]).astype(BF16)
    reps = DIFF_WIDTH // DIFF_HEAD_DIM

    q, k, v, zr, zk, zv, wp, ap, gg = _in_proj(
        x2, pos, row(attn_norm_g[0]), w_in[0].astype(BF16),
        w_lora_a[0].astype(BF16), a_lora_a[0].astype(BF16), g_lora_a[0].astype(BF16),
        w_lora_b[0].astype(BF16), a_lora_b[0].astype(BF16), g_lora_b[0].astype(BF16),
        mu_wag[0].astype(F32), row(jnp.tile(q_norm_g[0], reps)), row(jnp.tile(k_norm_g[0], reps)),
        invf, seg, B, S)

    o_diff = _attention(q, k, v, row(lambda_q1[0]), row(lambda_k1[0]), row(lambda_q2[0]),
                        row(lambda_k2[0]), diff_out_g[0].astype(F32).reshape(-1, 1), B, S)
    o_rwkv = _rwkv(zr, zk, zv, wp, ap, gg, mu_rkv[0].astype(F32), row(w0[0]), row(a0[0]),
                   row(k_k[0]), row(k_a[0]), row(r_k[0]), row(gn_w[0]), row(gn_b[0]), seg, B, S)

    wr = jnp.concatenate([w_group[0], jnp.transpose(w_expert_router[0], (1, 0, 2)).reshape(D, N_EXPERTS)],
                         axis=1).astype(F32)
    wr = jnp.pad(wr, ((0, 0), (0, V7X_LANES - wr.shape[1])))
    wr_hi = wr.astype(BF16)
    wr_lo = (wr - wr_hi.astype(F32)).astype(BF16)
    x1, hm, route, cnt = _out_router(x2, o_diff, o_rwkv, w_out[0].astype(BF16), row(moe_norm_g[0]),
                                     wr_hi, wr_lo)

    bm = MOE_ROWS
    counts = cnt[0, ROUTER_EXPERT_LANE:ROUTER_EXPERT_LANE + N_EXPERTS].astype(jnp.int32)
    pcounts = (counts + bm - 1) // bm * bm
    pends = jnp.cumsum(pcounts)
    pstarts = pends - pcounts
    nb = -(-(2 * T) // bm) + N_EXPERTS
    P = nb * bm
    n_used = (pends[-1] // bm).astype(jnp.int32).reshape(1)
    block_start = jnp.arange(nb, dtype=jnp.int32) * bm
    block_e = jnp.minimum(jnp.sum((pends[None, :] <= block_start[:, None]).astype(jnp.int32), axis=1),
                          N_EXPERTS - 1)
    eid = route[:, 0:2].astype(jnp.int32)
    tm = min(PROJ_ROWS, T)
    slot0 = jnp.sum(jnp.where(eid[:, :, None] == jnp.arange(N_EXPERTS, dtype=jnp.int32), pstarts, 0), axis=-1)
    dest = (slot0 + route[:, 4:6].astype(jnp.int32)).reshape(T // tm, 1, 2 * tm)

    xb = _dispatch(dest, hm, jnp.zeros((P, D), F32))
    yb = _experts(block_e, n_used, xb, w_gate[0], w_up[0], w_down[0])
    out = _final(dest, x1, route, p2, w_pl[0].astype(BF16), row(pl_norm_g[0]), row(pl_gate_norm_g[0]),
                 w_pl_gate[0].astype(BF16), yb)
    return out.reshape(B, S, D)
```

```python
import functools
import math

import jax
import jax.numpy as jnp
from jax import lax
from jax.experimental import pallas as pl
from jax.experimental.pallas import tpu as pltpu

F32 = jnp.float32
BF16 = jnp.bfloat16

D_MODEL = 1024
PL_DIM = 256
DIFF_WIDTH = 512
RWKV_WIDTH = 512
DIFF_HEAD_DIM = 64
DIFF_HEADS = 4
RWKV_HEAD = 64
RWKV_HEADS = 8
D_DECAY_LORA = 64
D_AAA_LORA = 64
D_GATE_LORA = 160
ROPE_THETA = 10000.0
NORM_EPS = 1e-6
SUBLN_EPS = 1e-5
GN_EPS = 64e-5
N_GROUPS = 4
EXPERTS_PER_GROUP = 8
N_EXPERTS = 32
D_EXPERT = 512
LAM_INIT = 0.8 - 0.6 * math.exp(0.0)

V7X_LANES = 128
V7X_SUBLANES = 8
V7X_VMEM_BYTES = 64 * 1024 * 1024

PROJ_ROWS = 256
ATTN_ROWS = 256
RWKV_CHUNK = 64
RWKV_ROWS = 256
MOE_ROWS = 256
VMEM_LIMIT = 56 * 1024 * 1024


def _cparams(sem):
    return pltpu.CompilerParams(dimension_semantics=sem, vmem_limit_bytes=VMEM_LIMIT)


def _dot(a, b):
    return jnp.dot(a.astype(BF16), b.astype(BF16), preferred_element_type=F32)


def _dot_nt(a, b):
    return lax.dot_general(a.astype(BF16), b.astype(BF16), (((1,), (1,)), ((), ())),
                           preferred_element_type=F32)


def _dot_tn(a, b):
    return lax.dot_general(a.astype(BF16), b.astype(BF16), (((0,), (0,)), ((), ())),
                           preferred_element_type=F32)


def _split3(x):
    hi = x.astype(BF16)
    r1 = x - hi.astype(F32)
    mid = r1.astype(BF16)
    lo = (r1 - mid.astype(F32)).astype(BF16)
    return hi, mid, lo


def _dot_exact_rhs(x, m01):
    hi, mid, lo = _split3(x)
    return (jnp.dot(hi, m01, preferred_element_type=F32)
            + jnp.dot(mid, m01, preferred_element_type=F32)
            + jnp.dot(lo, m01, preferred_element_type=F32))


def _dot_exact_lhs(m01, x):
    hi, mid, lo = _split3(x)
    return (jnp.dot(m01, hi, preferred_element_type=F32)
            + jnp.dot(m01, mid, preferred_element_type=F32)
            + jnp.dot(m01, lo, preferred_element_type=F32))


def _rms(x, g, eps):
    return x * lax.rsqrt(jnp.mean(x * x, axis=-1, keepdims=True) + eps) * g


def _shift_rows(z, prev_row):
    rolled = pltpu.roll(z, 1, axis=0)
    row = lax.broadcasted_iota(jnp.int32, z.shape, 0)
    return jnp.where(row == 0, prev_row, rolled)


def _in_proj_kernel(x_ref, pos_ref, g_ref, win_ref, w1_ref, a1_ref, g1_ref, w2_ref, a2_ref, g2_ref,
                    mu_ref, qg_ref, kg_ref, invf_ref, seg_ref,
                    q_out, k_out, v_out, zr_out, zk_out, zv_out, wp_out, ap_out, gg_out,
                    carry_ref):
    tm = x_ref.shape[0]

    @pl.when(pl.program_id(1) == 0)
    def _():
        carry_ref[...] = jnp.zeros_like(carry_ref)

    hn = _rms(x_ref[...], g_ref[...], NORM_EPS)
    prev = carry_ref[V7X_SUBLANES - 1:V7X_SUBLANES, :]
    dh = _shift_rows(hn, prev) - hn
    carry_ref[...] = hn[tm - V7X_SUBLANES:tm, :]

    hb = hn.astype(BF16)
    w = DIFF_WIDTH

    def proj(c):
        return jnp.dot(hb, win_ref[:, c * w:(c + 1) * w], preferred_element_type=F32)

    ang = pos_ref[...] * invf_ref[...]
    cos1, sin1 = jnp.cos(ang), jnp.sin(ang)
    cosf = jnp.concatenate([cos1] * (w // V7X_LANES), axis=1)
    sinf = jnp.concatenate([sin1] * (w // V7X_LANES), axis=1)
    lane = lax.broadcasted_iota(jnp.int32, (tm, w), 1)
    first_half = (lane % DIFF_HEAD_DIM) < (DIFF_HEAD_DIM // 2)
    sin_signed = jnp.where(first_half, -sinf, sinf)
    half = DIFF_HEAD_DIM // 2

    def norm_rope(z, gain):
        ss = _dot_exact_rhs(z * z, seg_ref[...])
        zn = z * lax.rsqrt(ss * (1.0 / DIFF_HEAD_DIM) + NORM_EPS) * gain
        partner = jnp.where(first_half, pltpu.roll(zn, w - half, axis=1), pltpu.roll(zn, half, axis=1))
        return zn * cosf + partner * sin_signed

    scale = DIFF_HEAD_DIM ** -0.5
    q_out[...] = (norm_rope(proj(0), qg_ref[...]) * scale).astype(q_out.dtype)
    k_out[...] = norm_rope(proj(1), kg_ref[...]).astype(k_out.dtype)
    v_out[...] = proj(2).astype(v_out.dtype)
    zr_out[...] = proj(3)
    zk_out[...] = proj(4)
    zv_out[...] = proj(5)

    xw = hn + dh * mu_ref[0:1, :]
    xa = hn + dh * mu_ref[1:2, :]
    xg = hn + dh * mu_ref[2:3, :]
    wp_out[...] = _dot(jnp.tanh(_dot(xw, w1_ref[...])), w2_ref[...])
    ap_out[...] = _dot(_dot(xa, a1_ref[...]), a2_ref[...])
    gg_out[...] = _dot(jax.nn.sigmoid(_dot(xg, g1_ref[...])), g2_ref[...])


def _in_proj(x2, pos, attn_g, w_in, w1, a1, g1, w2, a2, g2, mu_wag, qg, kg, invf, seg, B, S):
    T, D = x2.shape
    tm = min(PROJ_ROWS, S)
    ns = S // tm
    w = DIFF_WIDTH
    row = lambda b, i: (b * ns + i, 0)
    full = lambda b, i: (0, 0)

    def fs(a):
        return pl.BlockSpec(a.shape, full)

    outs = ([jax.ShapeDtypeStruct((T, w), BF16)] * 3 + [jax.ShapeDtypeStruct((T, w), F32)] * 6)
    return pl.pallas_call(
        _in_proj_kernel,
        grid=(B, ns),
        in_specs=[pl.BlockSpec((tm, D), row), pl.BlockSpec((tm, 1), row), fs(attn_g), fs(w_in),
                  fs(w1), fs(a1), fs(g1), fs(w2), fs(a2), fs(g2), fs(mu_wag), fs(qg), fs(kg),
                  fs(invf), fs(seg)],
        out_specs=[pl.BlockSpec((tm, w), row)] * 9,
        out_shape=outs,
        scratch_shapes=[pltpu.VMEM((V7X_SUBLANES, D), F32)],
        compiler_params=_cparams(("arbitrary", "arbitrary")),
        name="in_proj",
    )(x2, pos, attn_g, w_in, w1, a1, g1, w2, a2, g2, mu_wag, qg, kg, invf, seg)


def _attn_kernel(lq1_ref, lk1_ref, lq2_ref, lk2_ref, ogt_ref, q_ref, k_ref, v_ref, o_ref,
                 vt_ref, m_ref, l_ref, acc_ref):
    tq = q_ref.shape[0]
    tk = tq
    nk = vt_ref.shape[0]
    i = pl.program_id(2)
    lam = (jnp.exp(jnp.sum(lq1_ref[...] * lk1_ref[...], axis=-1, keepdims=True))
           - jnp.exp(jnp.sum(lq2_ref[...] * lk2_ref[...], axis=-1, keepdims=True)) + LAM_INIT)

    @pl.when(i == 0)
    def _():
        for c in range(nk):
            vt_ref[c] = v_ref[c * tk:(c + 1) * tk, :].astype(F32).T.astype(vt_ref.dtype)

    q = q_ref[...]
    lane = lax.broadcasted_iota(jnp.int32, q.shape, 1)
    zero = jnp.zeros_like(q)
    qs = jnp.concatenate([jnp.where(lane < DIFF_HEAD_DIM, q, zero),
                          jnp.where(lane >= DIFF_HEAD_DIM, q, zero)], axis=0)

    m_ref[...] = jnp.full_like(m_ref, -jnp.inf)
    l_ref[...] = jnp.zeros_like(l_ref)
    acc_ref[...] = jnp.zeros_like(acc_ref)

    def step(j, masked):
        start = pl.multiple_of(j * tk, tk)
        kb = k_ref[pl.ds(start, tk), :]
        st = lax.dot_general(kb, qs, (((1,), (1,)), ((), ())), preferred_element_type=F32)
        if masked:
            krow = lax.broadcasted_iota(jnp.int32, st.shape, 0)
            qcol = lax.broadcasted_iota(jnp.int32, st.shape, 1)
            qcol = jnp.where(qcol >= tq, qcol - tq, qcol)
            st = jnp.where(krow <= qcol, st, -jnp.inf)
        m_prev = m_ref[...]
        m_new = jnp.maximum(m_prev, jnp.max(st, axis=0, keepdims=True))
        alpha = jnp.exp(m_prev - m_new)
        pt = jnp.exp(st - m_new)
        l_ref[...] = alpha * l_ref[...] + jnp.sum(pt, axis=0, keepdims=True)
        acc_ref[...] = alpha * acc_ref[...] + jnp.dot(vt_ref[j], pt.astype(BF16),
                                                      preferred_element_type=F32)
        m_ref[...] = m_new

    def body(j, carry):
        step(j, False)
        return carry

    lax.fori_loop(0, i, body, 0)
    step(i, True)

    acc = acc_ref[...]
    l = l_ref[...]
    ot = acc[:, :tq] / l[:, :tq] - lam * (acc[:, tq:] / l[:, tq:])
    ot = ot * lax.rsqrt(jnp.mean(ot * ot, axis=0, keepdims=True) + SUBLN_EPS) * ogt_ref[...]
    o_ref[...] = (ot * (1.0 - LAM_INIT)).T.astype(o_ref.dtype)


def _attention(q, k, v, lq1, lk1, lq2, lk2, og, B, S):
    T = q.shape[0]
    tq = min(ATTN_ROWS, S)
    nq = S // tq
    hw = 2 * DIFF_HEAD_DIM
    small = lambda b, h, i: (0, 0)
    return pl.pallas_call(
        _attn_kernel,
        grid=(B, DIFF_HEADS, nq),
        in_specs=[pl.BlockSpec(lq1.shape, small), pl.BlockSpec(lk1.shape, small),
                  pl.BlockSpec(lq2.shape, small), pl.BlockSpec(lk2.shape, small),
                  pl.BlockSpec(og.shape, small),
                  pl.BlockSpec((tq, hw), lambda b, h, i: (b * nq + i, h)),
                  pl.BlockSpec((S, hw), lambda b, h, i: (b, h)),
                  pl.BlockSpec((S, hw), lambda b, h, i: (b, h))],
        out_specs=pl.BlockSpec((tq, hw), lambda b, h, i: (b * nq + i, h)),
        out_shape=jax.ShapeDtypeStruct((T, DIFF_WIDTH), BF16),
        scratch_shapes=[pltpu.VMEM((nq, hw, tq), BF16), pltpu.VMEM((1, 2 * tq), F32),
                        pltpu.VMEM((1, 2 * tq), F32), pltpu.VMEM((hw, 2 * tq), F32)],
        compiler_params=_cparams(("arbitrary", "arbitrary", "arbitrary")),
        name="diff_attn",
    )(lq1, lk1, lq2, lk2, og, q, k, v)


def _rwkv_kernel(zr_ref, zk_ref, zv_ref, wp_ref, ap_ref, gg_ref, mu_ref, w0_ref, a0_ref, kk_ref,
                 ka_ref, rk_ref, gnw_ref, gnb_ref, seg_ref, o_ref, state_ref, carry_ref):
    R = zr_ref.shape[0]
    C = min(RWKV_CHUNK, R)
    nch = R // C
    N = RWKV_HEAD
    W = RWKV_WIDTH
    HP = 2 * N
    n_pairs = W // HP

    @pl.when(pl.program_id(1) == 0)
    def _():
        state_ref[...] = jnp.zeros_like(state_ref)
        carry_ref[...] = jnp.zeros_like(carry_ref)

    zr, zk, zv = zr_ref[...], zk_ref[...], zv_ref[...]
    last = V7X_SUBLANES - 1
    r = zr + (_shift_rows(zr, carry_ref[last:last + 1, 0:W]) - zr) * mu_ref[0:1, :]
    k = zk + (_shift_rows(zk, carry_ref[last:last + 1, W:2 * W]) - zk) * mu_ref[1:2, :]
    v = zv + (_shift_rows(zv, carry_ref[last:last + 1, 2 * W:3 * W]) - zv) * mu_ref[2:3, :]
    carry_ref[:, 0:W] = zr[R - V7X_SUBLANES:R, :]
    carry_ref[:, W:2 * W] = zk[R - V7X_SUBLANES:R, :]
    carry_ref[:, 2 * W:3 * W] = zv[R - V7X_SUBLANES:R, :]

    lw = -math.exp(-0.5) * jax.nn.sigmoid(w0_ref[...] + wp_ref[...])
    a = jax.nn.sigmoid(a0_ref[...] + ap_ref[...])
    kk = k * kk_ref[...]
    kk = kk * lax.rsqrt(jnp.maximum(_dot_exact_rhs(kk * kk, seg_ref[...]), 1e-24))
    k2 = k * (1.0 + (a - 1.0) * ka_ref[...])
    a_s = -kk
    b_s = kk * a

    rr = lax.broadcasted_iota(jnp.int32, (R, R), 0)
    cc = lax.broadcasted_iota(jnp.int32, (R, R), 1)
    same_chunk = (rr // C) == (cc // C)
    L = _dot_exact_lhs((same_chunk & (cc <= rr)).astype(BF16), lw)
    bonus_w = r * k2 * rk_ref[...]

    P2 = 2 * C
    sr = lax.broadcasted_iota(jnp.int32, (P2, HP), 0)
    sc = lax.broadcasted_iota(jnp.int32, (P2, HP), 1)
    stack_mask = (sr < C) == (sc < N)
    br = lax.broadcasted_iota(jnp.int32, (P2, P2), 0)
    bc = lax.broadcasted_iota(jnp.int32, (P2, P2), 1)
    same_head = (br < C) == (bc < C)
    tr = jnp.where(br >= C, br - C, br)
    tc = jnp.where(bc >= C, bc - C, bc)
    strict = same_head & (tc < tr)
    lower = same_head & (tc <= tr)
    eye_p = (br == bc).astype(F32)
    kr = lax.broadcasted_iota(jnp.int32, (HP, HP), 0)
    kc = lax.broadcasted_iota(jnp.int32, (HP, HP), 1)
    eye_k = kr == kc

    def dup(x):
        return jnp.concatenate([x, x], axis=0)

    def stack(x):
        return jnp.where(stack_mask, dup(x), 0.0)

    n_sq = int(math.log2(C)) - 1
    items = [(c, p) for c in range(nch) for p in range(n_pairs)]

    def prep(c, p):
        rows = slice(c * C, (c + 1) * C)
        lanes = slice(p * HP, (p + 1) * HP)
        Lc = L[rows, lanes]
        lwc = lw[rows, lanes]
        LC = Lc[C - 1:C, :]
        enL = jnp.exp(-Lc)
        eCL = jnp.exp(LC - Lc)
        b_c, k_c = b_s[rows, lanes], k2[rows, lanes]
        return dict(
            xa=stack(a_s[rows, lanes] * jnp.exp(Lc - lwc)), xr=stack(r[rows, lanes] * jnp.exp(Lc)),
            bt=dup(b_c * enL), kt=dup(k_c * enL), bh=stack(b_c * eCL), kh=stack(k_c * eCL),
            vs=stack(v[rows, lanes]), pc=jnp.exp(LC))

    d = [prep(c, p) for c, p in items]
    aa = [_dot_nt(jnp.concatenate([e["xa"], e["xr"]], axis=0),
                  jnp.concatenate([e["bt"], e["kt"]], axis=0)) for e in d]
    a_ab = [jnp.where(strict, m[:P2, :P2], 0.0) for m in aa]
    a_ak = [jnp.where(strict, m[:P2, P2:], 0.0) for m in aa]
    a_rb = [jnp.where(lower, m[P2:, :P2], 0.0) for m in aa]
    a_rk = [jnp.where(lower, m[P2:, P2:], 0.0) for m in aa]
    tm = [eye_p + m for m in a_ab]
    npw = a_ab
    for _ in range(n_sq):
        npw = [_dot(m, m) for m in npw]
        tm = [t + _dot(t, m) for t, m in zip(tm, npw)]
    av = [_dot(m, e["vs"]) for m, e in zip(a_ak, d)]
    z = [_dot(t, jnp.concatenate([e["xa"], x], axis=1)) for t, e, x in zip(tm, d, av)]
    w = [_dot(m, x) for m, x in zip(a_rb, z)]
    rkv = [_dot(m, e["vs"]) for m, e in zip(a_rk, d)]
    gh = [_dot_tn(e["bh"], x) for e, x in zip(d, z)]
    khv = [_dot_tn(e["kh"], e["vs"]) for e in d]

    for idx, (c, p) in enumerate(items):
        e = d[idx]
        rows = slice(c * C, (c + 1) * C)
        lanes = slice(p * HP, (p + 1) * HP)
        rp = e["xr"] + w[idx][:, :HP]
        y0 = w[idx][:, HP:] + rkv[idx]
        gm = gh[idx][:, :HP]
        hm = gh[idx][:, HP:] + khv[idx]
        st = state_ref[p]
        yg = _dot(jnp.concatenate([rp, gm], axis=0), st)
        pc_col = jnp.sum(jnp.where(eye_k, e["pc"], 0.0), axis=1, keepdims=True)
        state_ref[p] = pc_col * st + yg[P2:] + hm
        ys = yg[:P2] + y0
        mu = jnp.sum(ys, axis=-1, keepdims=True) * (1.0 / N)
        yc = jnp.where(stack_mask, ys - mu, 0.0)
        var = jnp.sum(yc * yc, axis=-1, keepdims=True) * (1.0 / N)
        yn = yc * lax.rsqrt(var + GN_EPS)
        bonus = jnp.sum(stack(bonus_w[rows, lanes]), axis=-1, keepdims=True) * e["vs"]
        yn = yn[:C] + yn[C:]
        bonus = bonus[:C] + bonus[C:]
        o_ref[rows, lanes] = ((yn * gnw_ref[:, lanes] + gnb_ref[:, lanes] + bonus)
                              * gg_ref[rows, lanes]).astype(o_ref.dtype)


def _rwkv(zr, zk, zv, wp, ap, gg, mu_rkv, w0, a0, k_k, k_a, r_k, gn_w, gn_b, seg, B, S):
    T = zr.shape[0]
    C = min(RWKV_ROWS, S)
    nc = S // C
    W = RWKV_WIDTH
    HP = 2 * RWKV_HEAD
    row = lambda b, i: (b * nc + i, 0)
    full = lambda b, i: (0, 0)

    def fs(a):
        return pl.BlockSpec(a.shape, full)

    return pl.pallas_call(
        _rwkv_kernel,
        grid=(B, nc),
        in_specs=[pl.BlockSpec((C, W), row)] * 6 + [fs(mu_rkv), fs(w0), fs(a0), fs(k_k), fs(k_a),
                                                    fs(r_k), fs(gn_w), fs(gn_b), fs(seg)],
        out_specs=pl.BlockSpec((C, W), row),
        out_shape=jax.ShapeDtypeStruct((T, W), BF16),
        scratch_shapes=[pltpu.VMEM((W // HP, HP, HP), F32),
                        pltpu.VMEM((V7X_SUBLANES, 3 * W), F32)],
        compiler_params=_cparams(("arbitrary", "arbitrary")),
        name="rwkv7",
    )(zr, zk, zv, wp, ap, gg, mu_rkv, w0, a0, k_k, k_a, r_k, gn_w, gn_b, seg)


ROUTE_COLS = 8
ROUTER_GROUP_LANE = 0
ROUTER_EXPERT_LANE = N_GROUPS


def _out_router_kernel(x_ref, od_ref, orw_ref, wo_ref, mg_ref, wrh_ref, wrl_ref,
                       x1_out, hm_out, route_out, cnt_out):
    tm = x_ref.shape[0]
    x1 =(x_ref[...] + jnp.dot(od_ref[...], wo_ref[0:DIFF_WIDTH, :], preferred_element_type=F32)
          + jnp.dot(orw_ref[...], wo_ref[DIFF_WIDTH:, :], preferred_element_type=F32))
    x1_out[...] = x1
    hm = _rms(x1, mg_ref[...], NORM_EPS)
    hm_out[...] = hm.astype(hm_out.dtype)

    hi = hm.astype(BF16)
    lo = (hm - hi.astype(F32)).astype(BF16)
    lg = (jnp.dot(hi, wrh_ref[...], preferred_element_type=F32)
          + jnp.dot(hi, wrl_ref[...], preferred_element_type=F32)
          + jnp.dot(lo, wrh_ref[...], preferred_element_type=F32))

    lane = lax.broadcasted_iota(jnp.int32, lg.shape, 1).astype(F32)
    big = float(V7X_LANES)
    ninf = -jnp.inf
    gmask = lane < N_GROUPS
    gmax = jnp.max(jnp.where(gmask, lg, ninf), axis=-1, keepdims=True)
    g_sel = jnp.min(jnp.where(gmask & (lg == gmax), lane, big), axis=-1, keepdims=True)
    g_w = 1.0 / jnp.sum(jnp.where(gmask, jnp.exp(lg - gmax), 0.0), axis=-1, keepdims=True)
    lo_lane = ROUTER_EXPERT_LANE + g_sel * EXPERTS_PER_GROUP
    emask = (lane >= lo_lane) & (lane < lo_lane + EXPERTS_PER_GROUP)
    v1 = jnp.max(jnp.where(emask, lg, ninf), axis=-1, keepdims=True)
    i1 = jnp.min(jnp.where(emask & (lg == v1), lane, big), axis=-1, keepdims=True)
    emask2 = emask & (lane != i1)
    v2 = jnp.max(jnp.where(emask2, lg, ninf), axis=-1, keepdims=True)
    i2 = jnp.min(jnp.where(emask2 & (lg == v2), lane, big), axis=-1, keepdims=True)
    e2x = jnp.exp(v2 - v1)
    den = 1.0 + e2x
    wt1 = (1.0 / den) * g_w
    wt2 = (e2x / den) * g_w

    oh1 = lane == i1
    oh2 = lane == i2
    oh = (oh1 | oh2).astype(F32)
    ri = lax.broadcasted_iota(jnp.int32, (tm, tm), 0)
    ci = lax.broadcasted_iota(jnp.int32, (tm, tm), 1)
    before = (ci < ri).astype(BF16)
    prefix = jnp.dot(before, oh.astype(BF16), preferred_element_type=F32)
    rank1 = jnp.sum(jnp.where(oh1, prefix, 0.0), axis=-1, keepdims=True)
    rank2 = jnp.sum(jnp.where(oh2, prefix, 0.0), axis=-1, keepdims=True)
    cnt_out[...] = jnp.broadcast_to(jnp.sum(oh, axis=0, keepdims=True), cnt_out.shape)

    e1 = i1 - ROUTER_EXPERT_LANE
    e2 = i2 - ROUTER_EXPERT_LANE
    col = lax.broadcasted_iota(jnp.int32, (tm, ROUTE_COLS), 1)
    route = jnp.where(col == 0, e1, jnp.where(col == 1, e2, jnp.where(col == 2, wt1, jnp.where(
        col == 3, wt2, jnp.where(col == 4, rank1, jnp.where(col == 5, rank2, 0.0))))))
    route_out[...] = route


def _out_router(x2, od, orw, w_out, moe_g, wr_hi, wr_lo):
    T, D = x2.shape
    tm = min(PROJ_ROWS, T)
    row = lambda i: (i, 0)
    full = lambda i: (0, 0)

    def fs(a):
        return pl.BlockSpec(a.shape, full)

    return pl.pallas_call(
        _out_router_kernel,
        grid=(T // tm,),
        in_specs=[pl.BlockSpec((tm, D), row), pl.BlockSpec((tm, DIFF_WIDTH), row),
                  pl.BlockSpec((tm, RWKV_WIDTH), row), fs(w_out), fs(moe_g), fs(wr_hi), fs(wr_lo)],
        out_specs=[pl.BlockSpec((tm, D), row), pl.BlockSpec((tm, D), row),
                   pl.BlockSpec((tm, ROUTE_COLS), row), pl.BlockSpec((V7X_SUBLANES, V7X_LANES), row)],
        out_shape=[jax.ShapeDtypeStruct((T, D), F32), jax.ShapeDtypeStruct((T, D), BF16),
                   jax.ShapeDtypeStruct((T, ROUTE_COLS), F32),
                   jax.ShapeDtypeStruct((T // tm * V7X_SUBLANES, V7X_LANES), F32)],
        compiler_params=_cparams(("arbitrary",)),
        name="out_router",
    )(x2, od, orw, w_out, moe_g, wr_hi, wr_lo)


RUN_ALIGN = V7X_SUBLANES
TAB_DST, TAB_SRC, TAB_LEN = 0, N_EXPERTS, 2 * N_EXPERTS


def _stage_rows(tm):
    need = 2 * tm + N_EXPERTS * (RUN_ALIGN - 1)
    return -(-need // V7X_LANES) * V7X_LANES


def _run_copies(tab_ref, tm, make_copy, op):
    n_bits = int(math.log2(tm // RUN_ALIGN)) + 1

    def per_expert(e, c):
        n = tab_ref[0, 0, TAB_LEN + e]
        dst = tab_ref[0, 0, TAB_DST + e]
        src = tab_ref[0, 0, TAB_SRC + e]
        off = jnp.int32(0)
        for bit in reversed(range(n_bits)):
            size = RUN_ALIGN << bit
            hit = (n & size) != 0

            @pl.when(hit)
            def _(off=off, size=size):
                cp = make_copy(pl.multiple_of(src + off, RUN_ALIGN), pl.multiple_of(dst + off, RUN_ALIGN), size)
                getattr(cp, op)()

            off = off + jnp.where(hit, size, 0)
        return c

    lax.fori_loop(0, N_EXPERTS, per_expert, 0)


def _dispatch_kernel(tab_ref, sl_ref, hm_ref, xb_in_ref, xb_ref, stage_ref, sem):
    del xb_in_ref
    tm = hm_ref.shape[0]
    ns = stage_ref.shape[0]
    srow = lax.broadcasted_iota(jnp.int32, (ns, tm), 0)
    sel = (srow == sl_ref[0, 0:1, :]) | (srow == sl_ref[0, 1:2, :])
    stage_ref[...] = jnp.dot(sel.astype(BF16), hm_ref[...], preferred_element_type=F32)

    def make_copy(src, dst, size):
        return pltpu.make_async_copy(stage_ref.at[pl.ds(src, size)], xb_ref.at[pl.ds(dst, size)], sem)

    _run_copies(tab_ref, tm, make_copy, "start")
    _run_copies(tab_ref, tm, make_copy, "wait")


def _dispatch(tab, sl_rows, hm, xb0):
    T, D = hm.shape
    tm = min(PROJ_ROWS, T)
    return pl.pallas_call(
        _dispatch_kernel,
        grid=(T // tm,),
        in_specs=[pl.BlockSpec((1, 1, tab.shape[-1]), lambda i: (i, 0, 0), memory_space=pltpu.SMEM),
                  pl.BlockSpec((1, 2, tm), lambda i: (i, 0, 0)),
                  pl.BlockSpec((tm, D), lambda i: (i, 0)),
                  pl.BlockSpec(memory_space=pl.ANY)],
        out_specs=pl.BlockSpec(memory_space=pl.ANY),
        out_shape=jax.ShapeDtypeStruct(xb0.shape, xb0.dtype),
        scratch_shapes=[pltpu.VMEM((_stage_rows(tm), D), F32), pltpu.SemaphoreType.DMA(())],
        input_output_aliases={3: 0},
        compiler_params=_cparams(("arbitrary",)),
        name="moe_dispatch",
    )(tab, sl_rows, hm, xb0)


def _expert_kernel(be_ref, nu_ref, xb_ref, wg_ref, wu_ref, wd_ref, yb_ref, wgb, wub, wdb):
    i = pl.program_id(0)
    prev = be_ref[jnp.maximum(i - 1, 0)]
    changed = (i == 0) | (be_ref[i] != prev)

    @pl.when(changed)
    def _():
        wgb[...] = wg_ref[0].astype(BF16)
        wub[...] = wu_ref[0].astype(BF16)
        wdb[...] = wd_ref[0].astype(BF16)

    @pl.when(i < nu_ref[0])
    def _():
        xb = xb_ref[...].astype(BF16)
        gate = jnp.dot(xb, wgb[...], preferred_element_type=F32)
        up = jnp.dot(xb, wub[...], preferred_element_type=F32)
        hdn = (gate * jax.nn.sigmoid(gate)) * up
        yb_ref[...] = jnp.dot(hdn.astype(BF16), wdb[...], preferred_element_type=F32)

    @pl.when(i >= nu_ref[0])
    def _():
        yb_ref[...] = jnp.zeros_like(yb_ref)


def _experts(block_e, n_used, xb, w_gate, w_up, w_down):
    P, D = xb.shape
    bm = MOE_ROWS
    nb = P // bm
    E = D_EXPERT
    grid_spec = pltpu.PrefetchScalarGridSpec(
        num_scalar_prefetch=2,
        grid=(nb,),
        in_specs=[pl.BlockSpec((bm, D), lambda i, be, nu: (jnp.minimum(i, nu[0] - 1), 0)),
                  pl.BlockSpec((1, D, E), lambda i, be, nu: (be[i], 0, 0)),
                  pl.BlockSpec((1, D, E), lambda i, be, nu: (be[i], 0, 0)),
                  pl.BlockSpec((1, E, D), lambda i, be, nu: (be[i], 0, 0))],
        out_specs=pl.BlockSpec((bm, D), lambda i, be, nu: (i, 0)),
        scratch_shapes=[pltpu.VMEM((D, E), BF16), pltpu.VMEM((D, E), BF16), pltpu.VMEM((E, D), BF16)],
    )
    return pl.pallas_call(
        _expert_kernel,
        grid_spec=grid_spec,
        out_shape=jax.ShapeDtypeStruct((P, D), F32),
        compiler_params=_cparams(("arbitrary",)),
        name="moe_experts",
    )(block_e, n_used, xb, w_gate, w_up, w_down)


def _final_kernel(tab_ref, x1_ref, route_ref, p_ref, wpl_ref, plg_ref, pgg_ref, wgate_ref, yb_ref,
                  o_ref, stage_ref, sem):
    tm = x1_ref.shape[0]
    ns = stage_ref.shape[0]
    stage_ref[...] = jnp.zeros_like(stage_ref)

    def make_copy(src, dst, size):
        return pltpu.make_async_copy(yb_ref.at[pl.ds(dst, size)], stage_ref.at[pl.ds(src, size)], sem)

    _run_copies(tab_ref, tm, make_copy, "start")
    pe = _rms(jnp.dot(p_ref[...].astype(BF16), wpl_ref[...], preferred_element_type=F32),
              plg_ref[...], NORM_EPS)
    _run_copies(tab_ref, tm, make_copy, "wait")

    route = route_ref[...]
    scol = lax.broadcasted_iota(jnp.int32, (tm, ns), 1).astype(F32)
    wsel = (jnp.where(scol == route[:, 0:1], route[:, 2:3], 0.0)
            + jnp.where(scol == route[:, 1:2], route[:, 3:4], 0.0))
    w_hi = wsel.astype(BF16)
    w_lo = (wsel - w_hi.astype(F32)).astype(BF16)
    yb16 = stage_ref[...].astype(BF16)
    moe = (jnp.dot(w_hi, yb16, preferred_element_type=F32)
           + jnp.dot(w_lo, yb16, preferred_element_type=F32))
    x2 = x1_ref[...] + moe
    gate = jax.nn.sigmoid(_dot(_rms(x2, pgg_ref[...], NORM_EPS), wgate_ref[...]))
    o_ref[...] = x2 + pe * gate


def _final(tab, x1, route, p2, w_pl, pl_g, pl_gate_g, w_gate, yb):
    T, D = x1.shape
    tm = min(PROJ_ROWS, T)
    row = lambda i: (i, 0)
    full = lambda i: (0, 0)

    def fs(a):
        return pl.BlockSpec(a.shape, full)

    return pl.pallas_call(
        _final_kernel,
        grid=(T // tm,),
        in_specs=[pl.BlockSpec((1, 1, tab.shape[-1]), lambda i: (i, 0, 0), memory_space=pltpu.SMEM),
                  pl.BlockSpec((tm, D), row), pl.BlockSpec((tm, ROUTE_COLS), row),
                  pl.BlockSpec((tm, PL_DIM), row), fs(w_pl), fs(pl_g), fs(pl_gate_g), fs(w_gate),
                  pl.BlockSpec(memory_space=pl.ANY)],
        out_specs=pl.BlockSpec((tm, D), row),
        out_shape=jax.ShapeDtypeStruct((T, D), F32),
        scratch_shapes=[pltpu.VMEM((_stage_rows(tm), D), F32), pltpu.SemaphoreType.DMA(())],
        compiler_params=_cparams(("arbitrary",)),
        name="combine_final",
    )(tab, x1, route, p2, w_pl, pl_g, pl_gate_g, w_gate, yb)


def kernel(x, p, positions, attn_norm_g, w_in, q_norm_g, k_norm_g, lambda_q1, lambda_k1, lambda_q2, lambda_k2, diff_out_g, mu_rkv, mu_wag, w0, w_lora_a, w_lora_b, a0, a_lora_a, a_lora_b, g_lora_a, g_lora_b, k_k, k_a, r_k, gn_w, gn_b, w_out, moe_norm_g, w_group, w_expert_router, w_gate, w_up, w_down, w_pl, pl_norm_g, pl_gate_norm_g, w_pl_gate):
    B, S, D = x.shape
    T = B * S
    assert p.shape[0] == 1, "one layer"
    x2 = x.reshape(T, D)
    p2 = p[0].reshape(T, PL_DIM)
    pos = positions.astype(F32).reshape(T, 1)

    def row(a):
        return a.reshape(1, -1).astype(F32)

    half = DIFF_HEAD_DIM // 2
    inv_freq = ROPE_THETA ** (-jnp.arange(half, dtype=F32) / half)
    invf = jnp.tile(inv_freq, V7X_LANES // half).reshape(1, V7X_LANES)
    li = jnp.arange(DIFF_WIDTH) // DIFF_HEAD_DIM
    seg = (li[:, None] == li[None, :]).astype(BF16)
    reps = DIFF_WIDTH // DIFF_HEAD_DIM

    q, k, v, zr, zk, zv, wp, ap, gg = _in_proj(
        x2, pos, row(attn_norm_g[0]), w_in[0].astype(BF16),
        w_lora_a[0].astype(BF16), a_lora_a[0].astype(BF16), g_lora_a[0].astype(BF16),
        w_lora_b[0].astype(BF16), a_lora_b[0].astype(BF16), g_lora_b[0].astype(BF16),
        mu_wag[0].astype(F32), row(jnp.tile(q_norm_g[0], reps)), row(jnp.tile(k_norm_g[0], reps)),
        invf, seg, B, S)

    o_diff = _attention(q, k, v, row(lambda_q1[0]), row(lambda_k1[0]), row(lambda_q2[0]),
                        row(lambda_k2[0]), diff_out_g[0].astype(F32).reshape(-1, 1), B, S)
    o_rwkv = _rwkv(zr, zk, zv, wp, ap, gg, mu_rkv[0].astype(F32), row(w0[0]), row(a0[0]),
                   row(k_k[0]), row(k_a[0]), row(r_k[0]), row(gn_w[0]), row(gn_b[0]), seg, B, S)

    wr = jnp.concatenate([w_group[0], jnp.transpose(w_expert_router[0], (1, 0, 2)).reshape(D, N_EXPERTS)],
                         axis=1).astype(F32)
    wr = jnp.pad(wr, ((0, 0), (0, V7X_LANES - wr.shape[1])))
    wr_hi = wr.astype(BF16)
    wr_lo = (wr - wr_hi.astype(F32)).astype(BF16)
    x1, hm, route, cnt = _out_router(x2, o_diff, o_rwkv, w_out[0].astype(BF16), row(moe_norm_g[0]),
                                     wr_hi, wr_lo)

    bm = MOE_ROWS
    tm = min(PROJ_ROWS, T)
    nt = T // tm
    i32 = jnp.int32
    cnt_te = cnt.reshape(nt, V7X_SUBLANES, V7X_LANES)[:, 0, ROUTER_EXPERT_LANE:ROUTER_EXPERT_LANE + N_EXPERTS]
    len_te = (cnt_te.astype(i32) + RUN_ALIGN - 1) // RUN_ALIGN * RUN_ALIGN
    tot_e = jnp.sum(len_te, axis=0)
    pcounts = (tot_e + bm - 1) // bm * bm
    pends = jnp.cumsum(pcounts)
    pstarts = pends - pcounts
    dst_te = pstarts[None, :] + jnp.cumsum(len_te, axis=0) - len_te
    src_te = jnp.cumsum(len_te, axis=1) - len_te
    tab = jnp.concatenate([dst_te, src_te, len_te, jnp.zeros_like(len_te)], axis=1).reshape(nt, 1, 4 * N_EXPERTS)
    nb = -(-(2 * T + nt * N_EXPERTS * (RUN_ALIGN - 1)) // bm) + N_EXPERTS
    P = nb * bm
    n_used = (pends[-1] // bm).astype(i32).reshape(1)
    block_start = jnp.arange(nb, dtype=i32) * bm
    block_e = jnp.minimum(jnp.sum((pends[None, :] <= block_start[:, None]).astype(i32), axis=1),
                          N_EXPERTS - 1)
    eid = route[:, 0:2].astype(i32).reshape(nt, tm, 2)
    run_src = jnp.sum(jnp.where(eid[..., None] == jnp.arange(N_EXPERTS, dtype=i32), src_te[:, None, None, :], 0),
                      axis=-1)
    sl = run_src + route[:, 4:6].astype(i32).reshape(nt, tm, 2)
    sl_rows = jnp.transpose(sl, (0, 2, 1))
    route2 = jnp.concatenate([sl.reshape(T, 2).astype(F32), route[:, 2:4],
                              jnp.zeros((T, ROUTE_COLS - 4), F32)], axis=1)

    xb = _dispatch(tab, sl_rows, hm, jnp.zeros((P, D), F32))
    yb = _experts(block_e, n_used, xb, w_gate[0], w_up[0], w_down[0])
    out = _final(tab, x1, route2, p2, w_pl[0].astype(BF16), row(pl_norm_g[0]), row(pl_gate_norm_g[0]),
                 w_pl_gate[0].astype(BF16), yb)
    return out.reshape(B, S, D)
```

```python
import functools
import math

import jax
import jax.numpy as jnp
from jax import lax
from jax.experimental import pallas as pl
from jax.experimental.pallas import tpu as pltpu

F32 = jnp.float32
BF16 = jnp.bfloat16

D_MODEL = 1024
PL_DIM = 256
DIFF_WIDTH = 512
RWKV_WIDTH = 512
DIFF_HEAD_DIM = 64
DIFF_HEADS = 4
RWKV_HEAD = 64
RWKV_HEADS = 8
D_DECAY_LORA = 64
D_AAA_LORA = 64
D_GATE_LORA = 160
ROPE_THETA = 10000.0
NORM_EPS = 1e-6
SUBLN_EPS = 1e-5
GN_EPS = 64e-5
N_GROUPS = 4
EXPERTS_PER_GROUP = 8
N_EXPERTS = 32
D_EXPERT = 512
LAM_INIT = 0.8 - 0.6 * math.exp(0.0)

V7X_LANES = 128
V7X_SUBLANES = 8
V7X_VMEM_BYTES = 64 * 1024 * 1024

PROJ_ROWS = 256
ATTN_ROWS = 256
ATTN_HEADS_PER_STEP = 4
RWKV_CHUNK = 64
RWKV_ROWS = 256
MOE_ROWS = 256
VMEM_LIMIT = 56 * 1024 * 1024


def _cparams(sem):
    return pltpu.CompilerParams(dimension_semantics=sem, vmem_limit_bytes=VMEM_LIMIT)


def _dot(a, b):
    return jnp.dot(a.astype(BF16), b.astype(BF16), preferred_element_type=F32)


def _dot_nt(a, b):
    return lax.dot_general(a.astype(BF16), b.astype(BF16), (((1,), (1,)), ((), ())),
                           preferred_element_type=F32)


def _dot_tn(a, b):
    return lax.dot_general(a.astype(BF16), b.astype(BF16), (((0,), (0,)), ((), ())),
                           preferred_element_type=F32)


def _split3(x):
    hi = x.astype(BF16)
    r1 = x - hi.astype(F32)
    mid = r1.astype(BF16)
    lo = (r1 - mid.astype(F32)).astype(BF16)
    return hi, mid, lo


def _dot_exact_rhs(x, m01):
    hi = x.astype(BF16)
    mid = (x - hi.astype(F32)).astype(BF16)
    return (jnp.dot(hi, m01, preferred_element_type=F32)
            + jnp.dot(mid, m01, preferred_element_type=F32))


def _dot_exact_lhs(m01, x):
    hi, mid, lo = _split3(x)
    return (jnp.dot(m01, hi, preferred_element_type=F32)
            + jnp.dot(m01, mid, preferred_element_type=F32)
            + jnp.dot(m01, lo, preferred_element_type=F32))


def _rms(x, g, eps):
    return x * lax.rsqrt(jnp.mean(x * x, axis=-1, keepdims=True) + eps) * g


def _shift_rows(z, prev_row):
    rolled = pltpu.roll(z, 1, axis=0)
    row = lax.broadcasted_iota(jnp.int32, z.shape, 0)
    return jnp.where(row == 0, prev_row, rolled)


def _in_proj_kernel(x_ref, pos_ref, g_ref, win_ref, w1_ref, a1_ref, g1_ref, w2_ref, a2_ref, g2_ref,
                    mu_ref, qg_ref, kg_ref, invf_ref, seg_ref,
                    q_out, k_out, v_out, zr_out, zk_out, zv_out, wp_out, ap_out, gg_out,
                    carry_ref):
    tm = x_ref.shape[0]

    @pl.when(pl.program_id(1) == 0)
    def _():
        carry_ref[...] = jnp.zeros_like(carry_ref)

    hn = _rms(x_ref[...], g_ref[...], NORM_EPS)
    prev = carry_ref[V7X_SUBLANES - 1:V7X_SUBLANES, :]
    dh = _shift_rows(hn, prev) - hn
    carry_ref[...] = hn[tm - V7X_SUBLANES:tm, :]

    hb = hn.astype(BF16)
    w = DIFF_WIDTH

    def proj(c):
        return jnp.dot(hb, win_ref[:, c * w:(c + 1) * w], preferred_element_type=F32)

    ang = pos_ref[...] * invf_ref[...]
    cos1, sin1 = jnp.cos(ang), jnp.sin(ang)
    cosf = jnp.concatenate([cos1] * (w // V7X_LANES), axis=1)
    sinf = jnp.concatenate([sin1] * (w // V7X_LANES), axis=1)
    lane = lax.broadcasted_iota(jnp.int32, (tm, w), 1)
    first_half = (lane % DIFF_HEAD_DIM) < (DIFF_HEAD_DIM // 2)
    sin_signed = jnp.where(first_half, -sinf, sinf)
    half = DIFF_HEAD_DIM // 2

    def norm_rope(z, gain):
        ss = _dot_exact_rhs(z * z, seg_ref[...])
        zn = z * lax.rsqrt(ss * (1.0 / DIFF_HEAD_DIM) + NORM_EPS) * gain
        partner = jnp.where(first_half, pltpu.roll(zn, w - half, axis=1), pltpu.roll(zn, half, axis=1))
        return zn * cosf + partner * sin_signed

    scale = DIFF_HEAD_DIM ** -0.5
    q_out[...] = (norm_rope(proj(0), qg_ref[...]) * scale).astype(q_out.dtype)
    k_out[...] = norm_rope(proj(1), kg_ref[...]).astype(k_out.dtype)
    v_out[...] = proj(2).astype(v_out.dtype)
    zr_out[...] = proj(3)
    zk_out[...] = proj(4)
    zv_out[...] = proj(5)

    xw = hn + dh * mu_ref[0:1, :]
    xa = hn + dh * mu_ref[1:2, :]
    xg = hn + dh * mu_ref[2:3, :]
    wp_out[...] = _dot(jnp.tanh(_dot(xw, w1_ref[...])), w2_ref[...])
    ap_out[...] = _dot(_dot(xa, a1_ref[...]), a2_ref[...])
    gg_out[...] = _dot(jax.nn.sigmoid(_dot(xg, g1_ref[...])), g2_ref[...])


def _in_proj(x2, pos, attn_g, w_in, w1, a1, g1, w2, a2, g2, mu_wag, qg, kg, invf, seg, B, S):
    T, D = x2.shape
    tm = min(PROJ_ROWS, S)
    ns = S // tm
    w = DIFF_WIDTH
    row = lambda b, i: (b * ns + i, 0)
    full = lambda b, i: (0, 0)

    def fs(a):
        return pl.BlockSpec(a.shape, full)

    outs = ([jax.ShapeDtypeStruct((T, w), BF16)] * 3 + [jax.ShapeDtypeStruct((T, w), F32)] * 6)
    return pl.pallas_call(
        _in_proj_kernel,
        grid=(B, ns),
        in_specs=[pl.BlockSpec((tm, D), row), pl.BlockSpec((tm, 1), row), fs(attn_g), fs(w_in),
                  fs(w1), fs(a1), fs(g1), fs(w2), fs(a2), fs(g2), fs(mu_wag), fs(qg), fs(kg),
                  fs(invf), fs(seg)],
        out_specs=[pl.BlockSpec((tm, w), row)] * 9,
        out_shape=outs,
        scratch_shapes=[pltpu.VMEM((V7X_SUBLANES, D), F32)],
        compiler_params=_cparams(("arbitrary", "arbitrary")),
        name="in_proj",
    )(x2, pos, attn_g, w_in, w1, a1, g1, w2, a2, g2, mu_wag, qg, kg, invf, seg)


def _attn_kernel(lq1_ref, lk1_ref, lq2_ref, lk2_ref, ogt_ref, q_ref, k_ref, v_ref, o_ref,
                 vt_ref, m_ref, l_ref, acc_ref):
    tq = q_ref.shape[0]
    tk = tq
    nh, nk, hw = vt_ref.shape[0], vt_ref.shape[1], vt_ref.shape[2]
    heads = range(nh)
    i = pl.program_id(2)
    lam = (jnp.exp(jnp.sum(lq1_ref[...] * lk1_ref[...], axis=-1, keepdims=True))
           - jnp.exp(jnp.sum(lq2_ref[...] * lk2_ref[...], axis=-1, keepdims=True)) + LAM_INIT)

    @pl.when(i == 0)
    def _():
        for h in heads:
            for c in range(nk):
                vt_ref[h, c] = (v_ref[c * tk:(c + 1) * tk, h * hw:(h + 1) * hw]
                                .astype(F32).T.astype(vt_ref.dtype))

    lane = lax.broadcasted_iota(jnp.int32, (tq, hw), 1)
    qs = []
    for h in heads:
        q = q_ref[:, h * hw:(h + 1) * hw]
        zero = jnp.zeros_like(q)
        qs.append(jnp.concatenate([jnp.where(lane < DIFF_HEAD_DIM, q, zero),
                                   jnp.where(lane >= DIFF_HEAD_DIM, q, zero)], axis=0))

    m_ref[...] = jnp.full_like(m_ref, -jnp.inf)
    l_ref[...] = jnp.zeros_like(l_ref)
    acc_ref[...] = jnp.zeros_like(acc_ref)

    def step(j, masked):
        start = pl.multiple_of(j * tk, tk)
        st = [lax.dot_general(k_ref[pl.ds(start, tk), h * hw:(h + 1) * hw], qs[h],
                              (((1,), (1,)), ((), ())), preferred_element_type=F32) for h in heads]
        if masked:
            krow = lax.broadcasted_iota(jnp.int32, st[0].shape, 0)
            qcol = lax.broadcasted_iota(jnp.int32, st[0].shape, 1)
            qcol = jnp.where(qcol >= tq, qcol - tq, qcol)
            st = [jnp.where(krow <= qcol, s, -jnp.inf) for s in st]
        m_prev = [m_ref[h] for h in heads]
        m_new = [jnp.maximum(mp, jnp.max(s, axis=0, keepdims=True)) for mp, s in zip(m_prev, st)]
        alpha = [jnp.exp(mp - mn) for mp, mn in zip(m_prev, m_new)]
        pt = [jnp.exp(s - mn) for s, mn in zip(st, m_new)]
        pv = [jnp.dot(vt_ref[h, j], pt[h].astype(BF16), preferred_element_type=F32) for h in heads]
        for h in heads:
            l_ref[h] = alpha[h] * l_ref[h] + jnp.sum(pt[h], axis=0, keepdims=True)
            acc_ref[h] = alpha[h] * acc_ref[h] + pv[h]
            m_ref[h] = m_new[h]

    def body(j, carry):
        step(j, False)
        return carry

    lax.fori_loop(0, i, body, 0)
    step(i, True)

    for h in heads:
        acc = acc_ref[h]
        l = l_ref[h]
        ot = acc[:, :tq] / l[:, :tq] - lam * (acc[:, tq:] / l[:, tq:])
        ot = ot * lax.rsqrt(jnp.mean(ot * ot, axis=0, keepdims=True) + SUBLN_EPS) * ogt_ref[...]
        o_ref[:, h * hw:(h + 1) * hw] = (ot * (1.0 - LAM_INIT)).T.astype(o_ref.dtype)


def _attention(q, k, v, lq1, lk1, lq2, lk2, og, B, S):
    T = q.shape[0]
    tq = min(ATTN_ROWS, S)
    nq = S // tq
    hw = 2 * DIFF_HEAD_DIM
    nh = ATTN_HEADS_PER_STEP
    gw = nh * hw
    small = lambda b, h, i: (0, 0)
    return pl.pallas_call(
        _attn_kernel,
        grid=(B, DIFF_HEADS // nh, nq),
        in_specs=[pl.BlockSpec(lq1.shape, small), pl.BlockSpec(lk1.shape, small),
                  pl.BlockSpec(lq2.shape, small), pl.BlockSpec(lk2.shape, small),
                  pl.BlockSpec(og.shape, small),
                  pl.BlockSpec((tq, gw), lambda b, h, i: (b * nq + i, h)),
                  pl.BlockSpec((S, gw), lambda b, h, i: (b, h)),
                  pl.BlockSpec((S, gw), lambda b, h, i: (b, h))],
        out_specs=pl.BlockSpec((tq, gw), lambda b, h, i: (b * nq + i, h)),
        out_shape=jax.ShapeDtypeStruct((T, DIFF_WIDTH), BF16),
        scratch_shapes=[pltpu.VMEM((nh, nq, hw, tq), BF16), pltpu.VMEM((nh, 1, 2 * tq), F32),
                        pltpu.VMEM((nh, 1, 2 * tq), F32), pltpu.VMEM((nh, hw, 2 * tq), F32)],
        compiler_params=_cparams(("arbitrary", "arbitrary", "arbitrary")),
        name="diff_attn",
    )(lq1, lk1, lq2, lk2, og, q, k, v)


def _rwkv_kernel(zr_ref, zk_ref, zv_ref, wp_ref, ap_ref, gg_ref, mu_ref, w0_ref, a0_ref, kk_ref,
                 ka_ref, rk_ref, gnw_ref, gnb_ref, seg_ref, o_ref, state_ref, carry_ref):
    R = zr_ref.shape[0]
    C = min(RWKV_CHUNK, R)
    nch = R // C
    N = RWKV_HEAD
    W = RWKV_WIDTH
    HP = 2 * N
    n_pairs = W // HP

    @pl.when(pl.program_id(1) == 0)
    def _():
        state_ref[...] = jnp.zeros_like(state_ref)
        carry_ref[...] = jnp.zeros_like(carry_ref)

    zr, zk, zv = zr_ref[...], zk_ref[...], zv_ref[...]
    last = V7X_SUBLANES - 1
    r = zr + (_shift_rows(zr, carry_ref[last:last + 1, 0:W]) - zr) * mu_ref[0:1, :]
    k = zk + (_shift_rows(zk, carry_ref[last:last + 1, W:2 * W]) - zk) * mu_ref[1:2, :]
    v = zv + (_shift_rows(zv, carry_ref[last:last + 1, 2 * W:3 * W]) - zv) * mu_ref[2:3, :]
    carry_ref[:, 0:W] = zr[R - V7X_SUBLANES:R, :]
    carry_ref[:, W:2 * W] = zk[R - V7X_SUBLANES:R, :]
    carry_ref[:, 2 * W:3 * W] = zv[R - V7X_SUBLANES:R, :]

    lw = -math.exp(-0.5) * jax.nn.sigmoid(w0_ref[...] + wp_ref[...])
    a = jax.nn.sigmoid(a0_ref[...] + ap_ref[...])
    kk = k * kk_ref[...]
    kk = kk * lax.rsqrt(jnp.maximum(_dot_exact_rhs(kk * kk, seg_ref[...]), 1e-24))
    k2 = k * (1.0 + (a - 1.0) * ka_ref[...])
    a_s = -kk
    b_s = kk * a

    rr = lax.broadcasted_iota(jnp.int32, (R, R), 0)
    cc = lax.broadcasted_iota(jnp.int32, (R, R), 1)
    same_chunk = (rr // C) == (cc // C)
    L = _dot_exact_lhs((same_chunk & (cc <= rr)).astype(BF16), lw)
    bonus_w = r * k2 * rk_ref[...]

    P2 = 2 * C
    sr = lax.broadcasted_iota(jnp.int32, (P2, HP), 0)
    sc = lax.broadcasted_iota(jnp.int32, (P2, HP), 1)
    stack_mask = (sr < C) == (sc < N)
    br = lax.broadcasted_iota(jnp.int32, (P2, P2), 0)
    bc = lax.broadcasted_iota(jnp.int32, (P2, P2), 1)
    same_head = (br < C) == (bc < C)
    tr = jnp.where(br >= C, br - C, br)
    tc = jnp.where(bc >= C, bc - C, bc)
    strict = same_head & (tc < tr)
    lower = same_head & (tc <= tr)
    eye_p = (br == bc).astype(F32)
    kr = lax.broadcasted_iota(jnp.int32, (HP, HP), 0)
    kc = lax.broadcasted_iota(jnp.int32, (HP, HP), 1)
    eye_k = kr == kc

    def dup(x):
        return jnp.concatenate([x, x], axis=0)

    def stack(x):
        return jnp.where(stack_mask, dup(x), 0.0)

    n_sq = int(math.log2(C)) - 1
    items = [(c, p) for c in range(nch) for p in range(n_pairs)]

    def prep(c, p):
        rows = slice(c * C, (c + 1) * C)
        lanes = slice(p * HP, (p + 1) * HP)
        Lc = L[rows, lanes]
        lwc = lw[rows, lanes]
        LC = Lc[C - 1:C, :]
        enL = jnp.exp(-Lc)
        eCL = jnp.exp(LC - Lc)
        b_c, k_c = b_s[rows, lanes], k2[rows, lanes]
        return dict(
            xa=stack(a_s[rows, lanes] * jnp.exp(Lc - lwc)), xr=stack(r[rows, lanes] * jnp.exp(Lc)),
            bt=dup(b_c * enL), kt=dup(k_c * enL), bh=stack(b_c * eCL), kh=stack(k_c * eCL),
            vs=stack(v[rows, lanes]), pc=jnp.exp(LC))

    d = [prep(c, p) for c, p in items]
    aa = [_dot_nt(jnp.concatenate([e["xa"], e["xr"]], axis=0),
                  jnp.concatenate([e["bt"], e["kt"]], axis=0)) for e in d]
    a_ab = [jnp.where(strict, m[:P2, :P2], 0.0) for m in aa]
    a_ak = [jnp.where(strict, m[:P2, P2:], 0.0) for m in aa]
    a_rb = [jnp.where(lower, m[P2:, :P2], 0.0) for m in aa]
    a_rk = [jnp.where(lower, m[P2:, P2:], 0.0) for m in aa]
    tm = [eye_p + m for m in a_ab]
    npw = a_ab
    for _ in range(n_sq):
        npw = [_dot(m, m) for m in npw]
        tm = [t + _dot(t, m) for t, m in zip(tm, npw)]
    av = [_dot(m, e["vs"]) for m, e in zip(a_ak, d)]
    z = [_dot(t, jnp.concatenate([e["xa"], x], axis=1)) for t, e, x in zip(tm, d, av)]
    w = [_dot(m, x) for m, x in zip(a_rb, z)]
    rkv = [_dot(m, e["vs"]) for m, e in zip(a_rk, d)]
    gh = [_dot_tn(e["bh"], x) for e, x in zip(d, z)]
    khv = [_dot_tn(e["kh"], e["vs"]) for e in d]

    for idx, (c, p) in enumerate(items):
        e = d[idx]
        rows = slice(c * C, (c + 1) * C)
        lanes = slice(p * HP, (p + 1) * HP)
        rp = e["xr"] + w[idx][:, :HP]
        y0 = w[idx][:, HP:] + rkv[idx]
        gm = gh[idx][:, :HP]
        hm = gh[idx][:, HP:] + khv[idx]
        st = state_ref[p]
        yg = _dot(jnp.concatenate([rp, gm], axis=0), st)
        pc_col = jnp.sum(jnp.where(eye_k, e["pc"], 0.0), axis=1, keepdims=True)
        state_ref[p] = pc_col * st + yg[P2:] + hm
        ys = yg[:P2] + y0
        mu = jnp.sum(ys, axis=-1, keepdims=True) * (1.0 / N)
        yc = jnp.where(stack_mask, ys - mu, 0.0)
        var = jnp.sum(yc * yc, axis=-1, keepdims=True) * (1.0 / N)
        yn = yc * lax.rsqrt(var + GN_EPS)
        bonus = jnp.sum(stack(bonus_w[rows, lanes]), axis=-1, keepdims=True) * e["vs"]
        yn = yn[:C] + yn[C:]
        bonus = bonus[:C] + bonus[C:]
        o_ref[rows, lanes] = ((yn * gnw_ref[:, lanes] + gnb_ref[:, lanes] + bonus)
                              * gg_ref[rows, lanes]).astype(o_ref.dtype)


def _rwkv(zr, zk, zv, wp, ap, gg, mu_rkv, w0, a0, k_k, k_a, r_k, gn_w, gn_b, seg, B, S):
    T = zr.shape[0]
    C = min(RWKV_ROWS, S)
    nc = S // C
    W = RWKV_WIDTH
    HP = 2 * RWKV_HEAD
    row = lambda b, i: (b * nc + i, 0)
    full = lambda b, i: (0, 0)

    def fs(a):
        return pl.BlockSpec(a.shape, full)

    return pl.pallas_call(
        _rwkv_kernel,
        grid=(B, nc),
        in_specs=[pl.BlockSpec((C, W), row)] * 6 + [fs(mu_rkv), fs(w0), fs(a0), fs(k_k), fs(k_a),
                                                    fs(r_k), fs(gn_w), fs(gn_b), fs(seg)],
        out_specs=pl.BlockSpec((C, W), row),
        out_shape=jax.ShapeDtypeStruct((T, W), BF16),
        scratch_shapes=[pltpu.VMEM((W // HP, HP, HP), F32),
                        pltpu.VMEM((V7X_SUBLANES, 3 * W), F32)],
        compiler_params=_cparams(("arbitrary", "arbitrary")),
        name="rwkv7",
    )(zr, zk, zv, wp, ap, gg, mu_rkv, w0, a0, k_k, k_a, r_k, gn_w, gn_b, seg)


ROUTE_COLS = 8
ROUTER_GROUP_LANE = 0
ROUTER_EXPERT_LANE = N_GROUPS


def _out_router_kernel(x_ref, od_ref, orw_ref, wo_ref, mg_ref, wrh_ref, wrl_ref,
                       x1_out, hm_out, route_out, cnt_out):
    tm = x_ref.shape[0]
    x1 =(x_ref[...] + jnp.dot(od_ref[...], wo_ref[0:DIFF_WIDTH, :], preferred_element_type=F32)
          + jnp.dot(orw_ref[...], wo_ref[DIFF_WIDTH:, :], preferred_element_type=F32))
    x1_out[...] = x1
    hm = _rms(x1, mg_ref[...], NORM_EPS)
    hm_out[...] = hm.astype(hm_out.dtype)

    hi = hm.astype(BF16)
    lo = (hm - hi.astype(F32)).astype(BF16)
    lg = (jnp.dot(hi, wrh_ref[...], preferred_element_type=F32)
          + jnp.dot(hi, wrl_ref[...], preferred_element_type=F32)
          + jnp.dot(lo, wrh_ref[...], preferred_element_type=F32))

    lane = lax.broadcasted_iota(jnp.int32, lg.shape, 1).astype(F32)
    big = float(V7X_LANES)
    ninf = -jnp.inf
    gmask = lane < N_GROUPS
    gmax = jnp.max(jnp.where(gmask, lg, ninf), axis=-1, keepdims=True)
    g_sel = jnp.min(jnp.where(gmask & (lg == gmax), lane, big), axis=-1, keepdims=True)
    g_w = 1.0 / jnp.sum(jnp.where(gmask, jnp.exp(lg - gmax), 0.0), axis=-1, keepdims=True)
    lo_lane = ROUTER_EXPERT_LANE + g_sel * EXPERTS_PER_GROUP
    emask = (lane >= lo_lane) & (lane < lo_lane + EXPERTS_PER_GROUP)
    v1 = jnp.max(jnp.where(emask, lg, ninf), axis=-1, keepdims=True)
    i1 = jnp.min(jnp.where(emask & (lg == v1), lane, big), axis=-1, keepdims=True)
    emask2 = emask & (lane != i1)
    v2 = jnp.max(jnp.where(emask2, lg, ninf), axis=-1, keepdims=True)
    i2 = jnp.min(jnp.where(emask2 & (lg == v2), lane, big), axis=-1, keepdims=True)
    e2x = jnp.exp(v2 - v1)
    den = 1.0 + e2x
    wt1 = (1.0 / den) * g_w
    wt2 = (e2x / den) * g_w

    oh1 = lane == i1
    oh2 = lane == i2
    oh = (oh1 | oh2).astype(F32)
    ri = lax.broadcasted_iota(jnp.int32, (tm, tm), 0)
    ci = lax.broadcasted_iota(jnp.int32, (tm, tm), 1)
    before = (ci < ri).astype(BF16)
    prefix = jnp.dot(before, oh.astype(BF16), preferred_element_type=F32)
    rank1 = jnp.sum(jnp.where(oh1, prefix, 0.0), axis=-1, keepdims=True)
    rank2 = jnp.sum(jnp.where(oh2, prefix, 0.0), axis=-1, keepdims=True)
    cnt_out[...] = jnp.broadcast_to(jnp.sum(oh, axis=0, keepdims=True), cnt_out.shape)

    e1 = i1 - ROUTER_EXPERT_LANE
    e2 = i2 - ROUTER_EXPERT_LANE
    col = lax.broadcasted_iota(jnp.int32, (tm, ROUTE_COLS), 1)
    route = jnp.where(col == 0, e1, jnp.where(col == 1, e2, jnp.where(col == 2, wt1, jnp.where(
        col == 3, wt2, jnp.where(col == 4, rank1, jnp.where(col == 5, rank2, 0.0))))))
    route_out[...] = route


def _out_router(x2, od, orw, w_out, moe_g, wr_hi, wr_lo):
    T, D = x2.shape
    tm = min(PROJ_ROWS, T)
    row = lambda i: (i, 0)
    full = lambda i: (0, 0)

    def fs(a):
        return pl.BlockSpec(a.shape, full)

    return pl.pallas_call(
        _out_router_kernel,
        grid=(T // tm,),
        in_specs=[pl.BlockSpec((tm, D), row), pl.BlockSpec((tm, DIFF_WIDTH), row),
                  pl.BlockSpec((tm, RWKV_WIDTH), row), fs(w_out), fs(moe_g), fs(wr_hi), fs(wr_lo)],
        out_specs=[pl.BlockSpec((tm, D), row), pl.BlockSpec((tm, D), row),
                   pl.BlockSpec((tm, ROUTE_COLS), row), pl.BlockSpec((V7X_SUBLANES, V7X_LANES), row)],
        out_shape=[jax.ShapeDtypeStruct((T, D), F32), jax.ShapeDtypeStruct((T, D), BF16),
                   jax.ShapeDtypeStruct((T, ROUTE_COLS), F32),
                   jax.ShapeDtypeStruct((T // tm * V7X_SUBLANES, V7X_LANES), F32)],
        compiler_params=_cparams(("arbitrary",)),
        name="out_router",
    )(x2, od, orw, w_out, moe_g, wr_hi, wr_lo)


RUN_ALIGN = V7X_SUBLANES
TAB_DST, TAB_SRC, TAB_LEN, TAB_NUSED = 0, N_EXPERTS, 2 * N_EXPERTS, 3 * N_EXPERTS


def _stage_rows(tm):
    need = 2 * tm + N_EXPERTS * (RUN_ALIGN - 1)
    return -(-need // V7X_LANES) * V7X_LANES


def _run_copies(tab_ref, max_len, make_copy, op):
    n_bits = int(math.log2(max_len // RUN_ALIGN)) + 1

    def per_expert(e, c):
        n = tab_ref[0, 0, TAB_LEN + e]
        dst = tab_ref[0, 0, TAB_DST + e]
        src = tab_ref[0, 0, TAB_SRC + e]
        off = jnp.int32(0)
        for bit in reversed(range(n_bits)):
            size = RUN_ALIGN << bit
            hit = (n & size) != 0

            @pl.when(hit)
            def _(off=off, size=size):
                cp = make_copy(pl.multiple_of(src + off, RUN_ALIGN), pl.multiple_of(dst + off, RUN_ALIGN), size)
                getattr(cp, op)()

            off = off + jnp.where(hit, size, 0)
        return c

    lax.fori_loop(0, N_EXPERTS, per_expert, 0)


def _dispatch_kernel(tab_ref, tail_ref, sl_ref, hm_ref, xb_ref, stage_ref, zero_ref, sem, zsem):
    tm = hm_ref.shape[0]
    ns = stage_ref.shape[0]
    bm = zero_ref.shape[0]
    nb = xb_ref.shape[0] // bm
    n_used = tail_ref[0, 0, TAB_NUSED]

    def make_copy(src, dst, size):
        return pltpu.make_async_copy(stage_ref.at[pl.ds(src, size)], xb_ref.at[pl.ds(dst, size)], sem)

    def zero_copy(src, dst, size):
        return pltpu.make_async_copy(zero_ref.at[pl.ds(src, size)], xb_ref.at[pl.ds(dst, size)], zsem)

    def zero_blocks(op):
        def one(b, c):
            getattr(zero_copy(0, pl.multiple_of(b * bm, bm), bm), op)()
            return c
        lax.fori_loop(n_used, nb, one, 0)

    @pl.when(pl.program_id(0) == 0)
    def _():
        zero_ref[...] = jnp.zeros_like(zero_ref)
        _run_copies(tail_ref, bm, zero_copy, "start")
        zero_blocks("start")

    srow = lax.broadcasted_iota(jnp.int32, (ns, tm), 0)
    sel = (srow == sl_ref[0, 0:1, :]) | (srow == sl_ref[0, 1:2, :])
    stage_ref[...] = jnp.dot(sel.astype(BF16), hm_ref[...], preferred_element_type=F32)
    _run_copies(tab_ref, tm, make_copy, "start")
    _run_copies(tab_ref, tm, make_copy, "wait")

    @pl.when(pl.program_id(0) == pl.num_programs(0) - 1)
    def _():
        _run_copies(tail_ref, bm, zero_copy, "wait")
        zero_blocks("wait")


def _dispatch(tab, tail, sl_rows, hm, P):
    T, D = hm.shape
    tm = min(PROJ_ROWS, T)
    return pl.pallas_call(
        _dispatch_kernel,
        grid=(T // tm,),
        in_specs=[pl.BlockSpec((1, 1, tab.shape[-1]), lambda i: (i, 0, 0), memory_space=pltpu.SMEM),
                  pl.BlockSpec((1, 1, tail.shape[-1]), lambda i: (0, 0, 0), memory_space=pltpu.SMEM),
                  pl.BlockSpec((1, 2, tm), lambda i: (i, 0, 0)),
                  pl.BlockSpec((tm, D), lambda i: (i, 0))],
        out_specs=pl.BlockSpec(memory_space=pl.ANY),
        out_shape=jax.ShapeDtypeStruct((P, D), F32),
        scratch_shapes=[pltpu.VMEM((_stage_rows(tm), D), F32), pltpu.VMEM((MOE_ROWS, D), F32),
                        pltpu.SemaphoreType.DMA(()), pltpu.SemaphoreType.DMA(())],
        compiler_params=_cparams(("arbitrary",)),
        name="moe_dispatch",
    )(tab, tail, sl_rows, hm)


def _expert_kernel(be_ref, nu_ref, xb_ref, wg_ref, wu_ref, wd_ref, yb_ref, wgb, wub, wdb):
    i = pl.program_id(0)
    prev = be_ref[jnp.maximum(i - 1, 0)]
    changed = (i == 0) | (be_ref[i] != prev)

    @pl.when(changed)
    def _():
        wgb[...] = wg_ref[0].astype(BF16)
        wub[...] = wu_ref[0].astype(BF16)
        wdb[...] = wd_ref[0].astype(BF16)

    @pl.when(i < nu_ref[0])
    def _():
        xb = xb_ref[...].astype(BF16)
        gate = jnp.dot(xb, wgb[...], preferred_element_type=F32)
        up = jnp.dot(xb, wub[...], preferred_element_type=F32)
        hdn = (gate * jax.nn.sigmoid(gate)) * up
        yb_ref[...] = jnp.dot(hdn.astype(BF16), wdb[...], preferred_element_type=F32)

    @pl.when(i >= nu_ref[0])
    def _():
        yb_ref[...] = jnp.zeros_like(yb_ref)


def _experts(block_e, n_used, xb, w_gate, w_up, w_down):
    P, D = xb.shape
    bm = MOE_ROWS
    nb = P // bm
    E = D_EXPERT
    grid_spec = pltpu.PrefetchScalarGridSpec(
        num_scalar_prefetch=2,
        grid=(nb,),
        in_specs=[pl.BlockSpec((bm, D), lambda i, be, nu: (jnp.minimum(i, nu[0] - 1), 0)),
                  pl.BlockSpec((1, D, E), lambda i, be, nu: (be[i], 0, 0)),
                  pl.BlockSpec((1, D, E), lambda i, be, nu: (be[i], 0, 0)),
                  pl.BlockSpec((1, E, D), lambda i, be, nu: (be[i], 0, 0))],
        out_specs=pl.BlockSpec((bm, D), lambda i, be, nu: (i, 0)),
        scratch_shapes=[pltpu.VMEM((D, E), BF16), pltpu.VMEM((D, E), BF16), pltpu.VMEM((E, D), BF16)],
    )
    return pl.pallas_call(
        _expert_kernel,
        grid_spec=grid_spec,
        out_shape=jax.ShapeDtypeStruct((P, D), F32),
        compiler_params=_cparams(("arbitrary",)),
        name="moe_experts",
    )(block_e, n_used, xb, w_gate, w_up, w_down)


def _final_kernel(tab_ref, x1_ref, route_ref, p_ref, wpl_ref, plg_ref, pgg_ref, wgate_ref, yb_ref,
                  o_ref, stage_ref, sem):
    tm = x1_ref.shape[0]
    ns = stage_ref.shape[0]
    stage_ref[...] = jnp.zeros_like(stage_ref)

    def make_copy(src, dst, size):
        return pltpu.make_async_copy(yb_ref.at[pl.ds(dst, size)], stage_ref.at[pl.ds(src, size)], sem)

    _run_copies(tab_ref, tm, make_copy, "start")
    pe = _rms(jnp.dot(p_ref[...].astype(BF16), wpl_ref[...], preferred_element_type=F32),
              plg_ref[...], NORM_EPS)
    _run_copies(tab_ref, tm, make_copy, "wait")

    route = route_ref[...]
    scol = lax.broadcasted_iota(jnp.int32, (tm, ns), 1).astype(F32)
    wsel = (jnp.where(scol == route[:, 0:1], route[:, 2:3], 0.0)
            + jnp.where(scol == route[:, 1:2], route[:, 3:4], 0.0))
    w_hi = wsel.astype(BF16)
    w_lo = (wsel - w_hi.astype(F32)).astype(BF16)
    yb16 = stage_ref[...].astype(BF16)
    moe = (jnp.dot(w_hi, yb16, preferred_element_type=F32)
           + jnp.dot(w_lo, yb16, preferred_element_type=F32))
    x2 = x1_ref[...] + moe
    gate = jax.nn.sigmoid(_dot(_rms(x2, pgg_ref[...], NORM_EPS), wgate_ref[...]))
    o_ref[...] = x2 + pe * gate


def _final(tab, x1, route, p2, w_pl, pl_g, pl_gate_g, w_gate, yb):
    T, D = x1.shape
    tm = min(PROJ_ROWS, T)
    row = lambda i: (i, 0)
    full = lambda i: (0, 0)

    def fs(a):
        return pl.BlockSpec(a.shape, full)

    return pl.pallas_call(
        _final_kernel,
        grid=(T // tm,),
        in_specs=[pl.BlockSpec((1, 1, tab.shape[-1]), lambda i: (i, 0, 0), memory_space=pltpu.SMEM),
                  pl.BlockSpec((tm, D), row), pl.BlockSpec((tm, ROUTE_COLS), row),
                  pl.BlockSpec((tm, PL_DIM), row), fs(w_pl), fs(pl_g), fs(pl_gate_g), fs(w_gate),
                  pl.BlockSpec(memory_space=pl.ANY)],
        out_specs=pl.BlockSpec((tm, D), row),
        out_shape=jax.ShapeDtypeStruct((T, D), F32),
        scratch_shapes=[pltpu.VMEM((_stage_rows(tm), D), F32), pltpu.SemaphoreType.DMA(())],
        compiler_params=_cparams(("arbitrary",)),
        name="combine_final",
    )(tab, x1, route, p2, w_pl, pl_g, pl_gate_g, w_gate, yb)


def kernel(x, p, positions, attn_norm_g, w_in, q_norm_g, k_norm_g, lambda_q1, lambda_k1, lambda_q2, lambda_k2, diff_out_g, mu_rkv, mu_wag, w0, w_lora_a, w_lora_b, a0, a_lora_a, a_lora_b, g_lora_a, g_lora_b, k_k, k_a, r_k, gn_w, gn_b, w_out, moe_norm_g, w_group, w_expert_router, w_gate, w_up, w_down, w_pl, pl_norm_g, pl_gate_norm_g, w_pl_gate):
    B, S, D = x.shape
    T = B * S
    assert p.shape[0] == 1, "one layer"
    x2 = x.reshape(T, D)
    p2 = p[0].reshape(T, PL_DIM)
    pos = positions.astype(F32).reshape(T, 1)

    def row(a):
        return a.reshape(1, -1).astype(F32)

    half = DIFF_HEAD_DIM // 2
    inv_freq = ROPE_THETA ** (-jnp.arange(half, dtype=F32) / half)
    invf = jnp.tile(inv_freq, V7X_LANES // half).reshape(1, V7X_LANES)
    li = jnp.arange(DIFF_WIDTH) // DIFF_HEAD_DIM
    seg = (li[:, None] == li[None, :]).astype(BF16)
    reps = DIFF_WIDTH // DIFF_HEAD_DIM

    q, k, v, zr, zk, zv, wp, ap, gg = _in_proj(
        x2, pos, row(attn_norm_g[0]), w_in[0].astype(BF16),
        w_lora_a[0].astype(BF16), a_lora_a[0].astype(BF16), g_lora_a[0].astype(BF16),
        w_lora_b[0].astype(BF16), a_lora_b[0].astype(BF16), g_lora_b[0].astype(BF16),
        mu_wag[0].astype(F32), row(jnp.tile(q_norm_g[0], reps)), row(jnp.tile(k_norm_g[0], reps)),
        invf, seg, B, S)

    o_diff = _attention(q, k, v, row(lambda_q1[0]), row(lambda_k1[0]), row(lambda_q2[0]),
                        row(lambda_k2[0]), diff_out_g[0].astype(F32).reshape(-1, 1), B, S)
    o_rwkv = _rwkv(zr, zk, zv, wp, ap, gg, mu_rkv[0].astype(F32), row(w0[0]), row(a0[0]),
                   row(k_k[0]), row(k_a[0]), row(r_k[0]), row(gn_w[0]), row(gn_b[0]), seg, B, S)

    wr = jnp.concatenate([w_group[0], jnp.transpose(w_expert_router[0], (1, 0, 2)).reshape(D, N_EXPERTS)],
                         axis=1).astype(F32)
    wr = jnp.pad(wr, ((0, 0), (0, V7X_LANES - wr.shape[1])))
    wr_hi = wr.astype(BF16)
    wr_lo = (wr - wr_hi.astype(F32)).astype(BF16)
    x1, hm, route, cnt = _out_router(x2, o_diff, o_rwkv, w_out[0].astype(BF16), row(moe_norm_g[0]),
                                     wr_hi, wr_lo)

    bm = MOE_ROWS
    tm = min(PROJ_ROWS, T)
    nt = T // tm
    i32 = jnp.int32
    cnt_te = cnt.reshape(nt, V7X_SUBLANES, V7X_LANES)[:, 0, ROUTER_EXPERT_LANE:ROUTER_EXPERT_LANE + N_EXPERTS]
    len_te = (cnt_te.astype(i32) + RUN_ALIGN - 1) // RUN_ALIGN * RUN_ALIGN
    tot_e = jnp.sum(len_te, axis=0)
    pcounts = (tot_e + bm - 1) // bm * bm
    pends = jnp.cumsum(pcounts)
    pstarts = pends - pcounts
    dst_te = pstarts[None, :] + jnp.cumsum(len_te, axis=0) - len_te
    src_te = jnp.cumsum(len_te, axis=1) - len_te
    tab = jnp.concatenate([dst_te, src_te, len_te, jnp.zeros_like(len_te)], axis=1).reshape(nt, 1, 4 * N_EXPERTS)
    nb = -(-(2 * T + nt * N_EXPERTS * (RUN_ALIGN - 1)) // bm) + N_EXPERTS
    P = nb * bm
    n_used = (pends[-1] // bm).astype(i32).reshape(1)
    block_start = jnp.arange(nb, dtype=i32) * bm
    block_e = jnp.minimum(jnp.sum((pends[None, :] <= block_start[:, None]).astype(i32), axis=1),
                          N_EXPERTS - 1)
    eid = route[:, 0:2].astype(i32).reshape(nt, tm, 2)
    run_src = jnp.sum(jnp.where(eid[..., None] == jnp.arange(N_EXPERTS, dtype=i32), src_te[:, None, None, :], 0),
                      axis=-1)
    sl = run_src + route[:, 4:6].astype(i32).reshape(nt, tm, 2)
    sl_rows = jnp.transpose(sl, (0, 2, 1))
    route2 = jnp.concatenate([sl.reshape(T, 2).astype(F32), route[:, 2:4],
                              jnp.zeros((T, ROUTE_COLS - 4), F32)], axis=1)

    zero_e = jnp.zeros_like(tot_e)
    tail = jnp.concatenate([pstarts + tot_e, zero_e, pcounts - tot_e,
                            jnp.broadcast_to(n_used, (N_EXPERTS,))]).reshape(1, 1, 4 * N_EXPERTS)
    xb = _dispatch(tab, tail, sl_rows, hm, P)
    yb = _experts(block_e, n_used, xb, w_gate[0], w_up[0], w_down[0])
    out = _final(tab, x1, route2, p2, w_pl[0].astype(BF16), row(pl_norm_g[0]), row(pl_gate_norm_g[0]),
                 w_pl_gate[0].astype(BF16), yb)
    return out.reshape(B, S, D)
```

```python
import functools
import math

import jax
import jax.numpy as jnp
from jax import lax
from jax.experimental import pallas as pl
from jax.experimental.pallas import tpu as pltpu

F32 = jnp.float32
BF16 = jnp.bfloat16

D_MODEL = 1024
PL_DIM = 256
DIFF_WIDTH = 512
RWKV_WIDTH = 512
DIFF_HEAD_DIM = 64
DIFF_HEADS = 4
RWKV_HEAD = 64
RWKV_HEADS = 8
D_DECAY_LORA = 64
D_AAA_LORA = 64
D_GATE_LORA = 160
ROPE_THETA = 10000.0
NORM_EPS = 1e-6
SUBLN_EPS = 1e-5
GN_EPS = 64e-5
N_GROUPS = 4
EXPERTS_PER_GROUP = 8
N_EXPERTS = 32
D_EXPERT = 512
LAM_INIT = 0.8 - 0.6 * math.exp(0.0)

V7X_LANES = 128
V7X_SUBLANES = 8
V7X_VMEM_BYTES = 64 * 1024 * 1024

PROJ_ROWS = 256
ATTN_ROWS = 256
ATTN_HEADS_PER_STEP = 4
RWKV_CHUNK = 64
RWKV_ROWS = 256
MOE_ROWS = 512
VMEM_LIMIT = 56 * 1024 * 1024


def _cparams(sem):
    return pltpu.CompilerParams(dimension_semantics=sem, vmem_limit_bytes=VMEM_LIMIT)


def _dot(a, b):
    return jnp.dot(a.astype(BF16), b.astype(BF16), preferred_element_type=F32)


def _dot_nt(a, b):
    return lax.dot_general(a.astype(BF16), b.astype(BF16), (((1,), (1,)), ((), ())),
                           preferred_element_type=F32)


def _dot_tn(a, b):
    return lax.dot_general(a.astype(BF16), b.astype(BF16), (((0,), (0,)), ((), ())),
                           preferred_element_type=F32)


def _split3(x):
    hi = x.astype(BF16)
    r1 = x - hi.astype(F32)
    mid = r1.astype(BF16)
    lo = (r1 - mid.astype(F32)).astype(BF16)
    return hi, mid, lo


def _dot_exact_rhs(x, m01):
    hi = x.astype(BF16)
    mid = (x - hi.astype(F32)).astype(BF16)
    return (jnp.dot(hi, m01, preferred_element_type=F32)
            + jnp.dot(mid, m01, preferred_element_type=F32))


def _dot_exact_lhs(m01, x):
    hi, mid, lo = _split3(x)
    return (jnp.dot(m01, hi, preferred_element_type=F32)
            + jnp.dot(m01, mid, preferred_element_type=F32)
            + jnp.dot(m01, lo, preferred_element_type=F32))


def _rms(x, g, eps):
    return x * lax.rsqrt(jnp.mean(x * x, axis=-1, keepdims=True) + eps) * g


def _shift_rows(z, prev_row):
    rolled = pltpu.roll(z, 1, axis=0)
    row = lax.broadcasted_iota(jnp.int32, z.shape, 0)
    return jnp.where(row == 0, prev_row, rolled)


def _in_proj_kernel(x_ref, pos_ref, g_ref, win_ref, w1_ref, a1_ref, g1_ref, w2_ref, a2_ref, g2_ref,
                    mu_ref, qg_ref, kg_ref, invf_ref, seg_ref,
                    q_out, k_out, v_out, zr_out, zk_out, zv_out, wp_out, ap_out, gg_out,
                    carry_ref):
    tm = x_ref.shape[0]

    @pl.when(pl.program_id(1) == 0)
    def _():
        carry_ref[...] = jnp.zeros_like(carry_ref)

    hn = _rms(x_ref[...], g_ref[...], NORM_EPS)
    prev = carry_ref[V7X_SUBLANES - 1:V7X_SUBLANES, :]
    dh = _shift_rows(hn, prev) - hn
    carry_ref[...] = hn[tm - V7X_SUBLANES:tm, :]

    hb = hn.astype(BF16)
    w = DIFF_WIDTH

    def proj(c):
        return jnp.dot(hb, win_ref[:, c * w:(c + 1) * w], preferred_element_type=F32)

    ang = pos_ref[...] * invf_ref[...]
    cos1, sin1 = jnp.cos(ang), jnp.sin(ang)
    cosf = jnp.concatenate([cos1] * (w // V7X_LANES), axis=1)
    sinf = jnp.concatenate([sin1] * (w // V7X_LANES), axis=1)
    lane = lax.broadcasted_iota(jnp.int32, (tm, w), 1)
    first_half = (lane % DIFF_HEAD_DIM) < (DIFF_HEAD_DIM // 2)
    sin_signed = jnp.where(first_half, -sinf, sinf)
    half = DIFF_HEAD_DIM // 2

    def norm_rope(z, gain):
        ss = _dot_exact_rhs(z * z, seg_ref[...])
        zn = z * lax.rsqrt(ss * (1.0 / DIFF_HEAD_DIM) + NORM_EPS) * gain
        partner = jnp.where(first_half, pltpu.roll(zn, w - half, axis=1), pltpu.roll(zn, half, axis=1))
        return zn * cosf + partner * sin_signed

    scale = DIFF_HEAD_DIM ** -0.5
    q_out[...] = (norm_rope(proj(0), qg_ref[...]) * scale).astype(q_out.dtype)
    k_out[...] = norm_rope(proj(1), kg_ref[...]).astype(k_out.dtype)
    v_out[...] = proj(2).astype(v_out.dtype)
    zr_out[...] = proj(3)
    zk_out[...] = proj(4)
    zv_out[...] = proj(5)

    xw = hn + dh * mu_ref[0:1, :]
    xa = hn + dh * mu_ref[1:2, :]
    xg = hn + dh * mu_ref[2:3, :]
    wp_out[...] = _dot(jnp.tanh(_dot(xw, w1_ref[...])), w2_ref[...])
    ap_out[...] = _dot(_dot(xa, a1_ref[...]), a2_ref[...])
    gg_out[...] = _dot(jax.nn.sigmoid(_dot(xg, g1_ref[...])), g2_ref[...])


def _in_proj(x2, pos, attn_g, w_in, w1, a1, g1, w2, a2, g2, mu_wag, qg, kg, invf, seg, B, S):
    T, D = x2.shape
    tm = min(PROJ_ROWS, S)
    ns = S // tm
    w = DIFF_WIDTH
    row = lambda b, i: (b * ns + i, 0)
    full = lambda b, i: (0, 0)

    def fs(a):
        return pl.BlockSpec(a.shape, full)

    outs = ([jax.ShapeDtypeStruct((T, w), BF16)] * 3 + [jax.ShapeDtypeStruct((T, w), F32)] * 6)
    return pl.pallas_call(
        _in_proj_kernel,
        grid=(B, ns),
        in_specs=[pl.BlockSpec((tm, D), row), pl.BlockSpec((tm, 1), row), fs(attn_g), fs(w_in),
                  fs(w1), fs(a1), fs(g1), fs(w2), fs(a2), fs(g2), fs(mu_wag), fs(qg), fs(kg),
                  fs(invf), fs(seg)],
        out_specs=[pl.BlockSpec((tm, w), row)] * 9,
        out_shape=outs,
        scratch_shapes=[pltpu.VMEM((V7X_SUBLANES, D), F32)],
        compiler_params=_cparams(("arbitrary", "arbitrary")),
        name="in_proj",
    )(x2, pos, attn_g, w_in, w1, a1, g1, w2, a2, g2, mu_wag, qg, kg, invf, seg)


def _attn_kernel(lq1_ref, lk1_ref, lq2_ref, lk2_ref, ogt_ref, q_ref, k_ref, v_ref, o_ref,
                 vt_ref, m_ref, l_ref, acc_ref):
    tq = q_ref.shape[0]
    tk = tq
    nh, nk, hw = vt_ref.shape[0], vt_ref.shape[1], vt_ref.shape[2]
    heads = range(nh)
    i = pl.program_id(2)
    lam = (jnp.exp(jnp.sum(lq1_ref[...] * lk1_ref[...], axis=-1, keepdims=True))
           - jnp.exp(jnp.sum(lq2_ref[...] * lk2_ref[...], axis=-1, keepdims=True)) + LAM_INIT)

    @pl.when(i == 0)
    def _():
        for h in heads:
            for c in range(nk):
                vt_ref[h, c] = (v_ref[c * tk:(c + 1) * tk, h * hw:(h + 1) * hw]
                                .astype(F32).T.astype(vt_ref.dtype))

    lane = lax.broadcasted_iota(jnp.int32, (tq, hw), 1)
    qs = []
    for h in heads:
        q = q_ref[:, h * hw:(h + 1) * hw]
        zero = jnp.zeros_like(q)
        qs.append(jnp.concatenate([jnp.where(lane < DIFF_HEAD_DIM, q, zero),
                                   jnp.where(lane >= DIFF_HEAD_DIM, q, zero)], axis=0))

    m_ref[...] = jnp.full_like(m_ref, -jnp.inf)
    l_ref[...] = jnp.zeros_like(l_ref)
    acc_ref[...] = jnp.zeros_like(acc_ref)

    def step(j, masked):
        start = pl.multiple_of(j * tk, tk)
        st = [lax.dot_general(k_ref[pl.ds(start, tk), h * hw:(h + 1) * hw], qs[h],
                              (((1,), (1,)), ((), ())), preferred_element_type=F32) for h in heads]
        if masked:
            krow = lax.broadcasted_iota(jnp.int32, st[0].shape, 0)
            qcol = lax.broadcasted_iota(jnp.int32, st[0].shape, 1)
            qcol = jnp.where(qcol >= tq, qcol - tq, qcol)
            st = [jnp.where(krow <= qcol, s, -jnp.inf) for s in st]
        m_prev = [m_ref[h] for h in heads]
        m_new = [jnp.maximum(mp, jnp.max(s, axis=0, keepdims=True)) for mp, s in zip(m_prev, st)]
        alpha = [jnp.exp(mp - mn) for mp, mn in zip(m_prev, m_new)]
        pt = [jnp.exp(s - mn) for s, mn in zip(st, m_new)]
        pv = [jnp.dot(vt_ref[h, j], pt[h].astype(BF16), preferred_element_type=F32) for h in heads]
        for h in heads:
            l_ref[h] = alpha[h] * l_ref[h] + jnp.sum(pt[h], axis=0, keepdims=True)
            acc_ref[h] = alpha[h] * acc_ref[h] + pv[h]
            m_ref[h] = m_new[h]

    def body(j, carry):
        step(j, False)
        return carry

    lax.fori_loop(0, i, body, 0)
    step(i, True)

    for h in heads:
        acc = acc_ref[h]
        l = l_ref[h]
        ot = acc[:, :tq] / l[:, :tq] - lam * (acc[:, tq:] / l[:, tq:])
        ot = ot * lax.rsqrt(jnp.mean(ot * ot, axis=0, keepdims=True) + SUBLN_EPS) * ogt_ref[...]
        o_ref[:, h * hw:(h + 1) * hw] = (ot * (1.0 - LAM_INIT)).T.astype(o_ref.dtype)


def _attention(q, k, v, lq1, lk1, lq2, lk2, og, B, S):
    T = q.shape[0]
    tq = min(ATTN_ROWS, S)
    nq = S // tq
    hw = 2 * DIFF_HEAD_DIM
    nh = ATTN_HEADS_PER_STEP
    gw = nh * hw
    small = lambda b, h, i: (0, 0)
    return pl.pallas_call(
        _attn_kernel,
        grid=(B, DIFF_HEADS // nh, nq),
        in_specs=[pl.BlockSpec(lq1.shape, small), pl.BlockSpec(lk1.shape, small),
                  pl.BlockSpec(lq2.shape, small), pl.BlockSpec(lk2.shape, small),
                  pl.BlockSpec(og.shape, small),
                  pl.BlockSpec((tq, gw), lambda b, h, i: (b * nq + i, h)),
                  pl.BlockSpec((S, gw), lambda b, h, i: (b, h)),
                  pl.BlockSpec((S, gw), lambda b, h, i: (b, h))],
        out_specs=pl.BlockSpec((tq, gw), lambda b, h, i: (b * nq + i, h)),
        out_shape=jax.ShapeDtypeStruct((T, DIFF_WIDTH), BF16),
        scratch_shapes=[pltpu.VMEM((nh, nq, hw, tq), BF16), pltpu.VMEM((nh, 1, 2 * tq), F32),
                        pltpu.VMEM((nh, 1, 2 * tq), F32), pltpu.VMEM((nh, hw, 2 * tq), F32)],
        compiler_params=_cparams(("arbitrary", "arbitrary", "arbitrary")),
        name="diff_attn",
    )(lq1, lk1, lq2, lk2, og, q, k, v)


def _rwkv_kernel(zr_ref, zk_ref, zv_ref, wp_ref, ap_ref, gg_ref, mu_ref, w0_ref, a0_ref, kk_ref,
                 ka_ref, rk_ref, gnw_ref, gnb_ref, seg_ref, o_ref, state_ref, carry_ref):
    R = zr_ref.shape[0]
    C = min(RWKV_CHUNK, R)
    nch = R // C
    N = RWKV_HEAD
    W = RWKV_WIDTH
    HP = 2 * N
    n_pairs = W // HP

    @pl.when(pl.program_id(1) == 0)
    def _():
        state_ref[...] = jnp.zeros_like(state_ref)
        carry_ref[...] = jnp.zeros_like(carry_ref)

    zr, zk, zv = zr_ref[...], zk_ref[...], zv_ref[...]
    last = V7X_SUBLANES - 1
    r = zr + (_shift_rows(zr, carry_ref[last:last + 1, 0:W]) - zr) * mu_ref[0:1, :]
    k = zk + (_shift_rows(zk, carry_ref[last:last + 1, W:2 * W]) - zk) * mu_ref[1:2, :]
    v = zv + (_shift_rows(zv, carry_ref[last:last + 1, 2 * W:3 * W]) - zv) * mu_ref[2:3, :]
    carry_ref[:, 0:W] = zr[R - V7X_SUBLANES:R, :]
    carry_ref[:, W:2 * W] = zk[R - V7X_SUBLANES:R, :]
    carry_ref[:, 2 * W:3 * W] = zv[R - V7X_SUBLANES:R, :]

    lw = -math.exp(-0.5) * jax.nn.sigmoid(w0_ref[...] + wp_ref[...])
    a = jax.nn.sigmoid(a0_ref[...] + ap_ref[...])
    kk = k * kk_ref[...]
    kk = kk * lax.rsqrt(jnp.maximum(_dot_exact_rhs(kk * kk, seg_ref[...]), 1e-24))
    k2 = k * (1.0 + (a - 1.0) * ka_ref[...])
    a_s = -kk
    b_s = kk * a

    rr = lax.broadcasted_iota(jnp.int32, (R, R), 0)
    cc = lax.broadcasted_iota(jnp.int32, (R, R), 1)
    same_chunk = (rr // C) == (cc // C)
    L = _dot_exact_lhs((same_chunk & (cc <= rr)).astype(BF16), lw)
    bonus_w = r * k2 * rk_ref[...]

    P2 = 2 * C
    sr = lax.broadcasted_iota(jnp.int32, (P2, HP), 0)
    sc = lax.broadcasted_iota(jnp.int32, (P2, HP), 1)
    stack_mask = (sr < C) == (sc < N)
    br = lax.broadcasted_iota(jnp.int32, (P2, P2), 0)
    bc = lax.broadcasted_iota(jnp.int32, (P2, P2), 1)
    same_head = (br < C) == (bc < C)
    tr = jnp.where(br >= C, br - C, br)
    tc = jnp.where(bc >= C, bc - C, bc)
    strict = same_head & (tc < tr)
    lower = same_head & (tc <= tr)
    eye_p = (br == bc).astype(F32)
    kr = lax.broadcasted_iota(jnp.int32, (HP, HP), 0)
    kc = lax.broadcasted_iota(jnp.int32, (HP, HP), 1)
    eye_k = kr == kc

    def dup(x):
        return jnp.concatenate([x, x], axis=0)

    def stack(x):
        return jnp.where(stack_mask, dup(x), 0.0)

    n_sq = int(math.log2(C)) - 1
    items = [(c, p) for c in range(nch) for p in range(n_pairs)]

    def prep(c, p):
        rows = slice(c * C, (c + 1) * C)
        lanes = slice(p * HP, (p + 1) * HP)
        Lc = L[rows, lanes]
        lwc = lw[rows, lanes]
        LC = Lc[C - 1:C, :]
        enL = jnp.exp(-Lc)
        eCL = jnp.exp(LC - Lc)
        b_c, k_c = b_s[rows, lanes], k2[rows, lanes]
        return dict(
            xa=stack(a_s[rows, lanes] * jnp.exp(Lc - lwc)), xr=stack(r[rows, lanes] * jnp.exp(Lc)),
            bt=dup(b_c * enL), kt=dup(k_c * enL), bh=stack(b_c * eCL), kh=stack(k_c * eCL),
            vs=stack(v[rows, lanes]), pc=jnp.exp(LC))

    d = [prep(c, p) for c, p in items]
    aa = [_dot_nt(jnp.concatenate([e["xa"], e["xr"]], axis=0),
                  jnp.concatenate([e["bt"], e["kt"]], axis=0)) for e in d]
    a_ab = [jnp.where(strict, m[:P2, :P2], 0.0) for m in aa]
    a_ak = [jnp.where(strict, m[:P2, P2:], 0.0) for m in aa]
    a_rb = [jnp.where(lower, m[P2:, :P2], 0.0) for m in aa]
    a_rk = [jnp.where(lower, m[P2:, P2:], 0.0) for m in aa]
    tm = [eye_p + m for m in a_ab]
    npw = a_ab
    for _ in range(n_sq):
        npw = [_dot(m, m) for m in npw]
        tm = [t + _dot(t, m) for t, m in zip(tm, npw)]
    av = [_dot(m, e["vs"]) for m, e in zip(a_ak, d)]
    z = [_dot(t, jnp.concatenate([e["xa"], x], axis=1)) for t, e, x in zip(tm, d, av)]
    w = [_dot(m, x) for m, x in zip(a_rb, z)]
    rkv = [_dot(m, e["vs"]) for m, e in zip(a_rk, d)]
    gh = [_dot_tn(e["bh"], x) for e, x in zip(d, z)]
    khv = [_dot_tn(e["kh"], e["vs"]) for e in d]

    for idx, (c, p) in enumerate(items):
        e = d[idx]
        rows = slice(c * C, (c + 1) * C)
        lanes = slice(p * HP, (p + 1) * HP)
        rp = e["xr"] + w[idx][:, :HP]
        y0 = w[idx][:, HP:] + rkv[idx]
        gm = gh[idx][:, :HP]
        hm = gh[idx][:, HP:] + khv[idx]
        st = state_ref[p]
        yg = _dot(jnp.concatenate([rp, gm], axis=0), st)
        pc_col = jnp.sum(jnp.where(eye_k, e["pc"], 0.0), axis=1, keepdims=True)
        state_ref[p] = pc_col * st + yg[P2:] + hm
        ys = yg[:P2] + y0
        mu = jnp.sum(ys, axis=-1, keepdims=True) * (1.0 / N)
        yc = jnp.where(stack_mask, ys - mu, 0.0)
        var = jnp.sum(yc * yc, axis=-1, keepdims=True) * (1.0 / N)
        yn = yc * lax.rsqrt(var + GN_EPS)
        bonus = jnp.sum(stack(bonus_w[rows, lanes]), axis=-1, keepdims=True) * e["vs"]
        yn = yn[:C] + yn[C:]
        bonus = bonus[:C] + bonus[C:]
        o_ref[rows, lanes] = ((yn * gnw_ref[:, lanes] + gnb_ref[:, lanes] + bonus)
                              * gg_ref[rows, lanes]).astype(o_ref.dtype)


def _rwkv(zr, zk, zv, wp, ap, gg, mu_rkv, w0, a0, k_k, k_a, r_k, gn_w, gn_b, seg, B, S):
    T = zr.shape[0]
    C = min(RWKV_ROWS, S)
    nc = S // C
    W = RWKV_WIDTH
    HP = 2 * RWKV_HEAD
    row = lambda b, i: (b * nc + i, 0)
    full = lambda b, i: (0, 0)

    def fs(a):
        return pl.BlockSpec(a.shape, full)

    return pl.pallas_call(
        _rwkv_kernel,
        grid=(B, nc),
        in_specs=[pl.BlockSpec((C, W), row)] * 6 + [fs(mu_rkv), fs(w0), fs(a0), fs(k_k), fs(k_a),
                                                    fs(r_k), fs(gn_w), fs(gn_b), fs(seg)],
        out_specs=pl.BlockSpec((C, W), row),
        out_shape=jax.ShapeDtypeStruct((T, W), BF16),
        scratch_shapes=[pltpu.VMEM((W // HP, HP, HP), F32),
                        pltpu.VMEM((V7X_SUBLANES, 3 * W), F32)],
        compiler_params=_cparams(("arbitrary", "arbitrary")),
        name="rwkv7",
    )(zr, zk, zv, wp, ap, gg, mu_rkv, w0, a0, k_k, k_a, r_k, gn_w, gn_b, seg)


ROUTE_COLS = 8
ROUTER_GROUP_LANE = 0
ROUTER_EXPERT_LANE = N_GROUPS


def _out_router_kernel(x_ref, od_ref, orw_ref, wo_ref, mg_ref, wrh_ref, wrl_ref,
                       x1_out, hm_out, route_out, cnt_out):
    tm = x_ref.shape[0]
    x1 =(x_ref[...] + jnp.dot(od_ref[...], wo_ref[0:DIFF_WIDTH, :], preferred_element_type=F32)
          + jnp.dot(orw_ref[...], wo_ref[DIFF_WIDTH:, :], preferred_element_type=F32))
    x1_out[...] = x1
    hm = _rms(x1, mg_ref[...], NORM_EPS)
    hm_out[...] = hm.astype(hm_out.dtype)

    hi = hm.astype(BF16)
    lo = (hm - hi.astype(F32)).astype(BF16)
    lg = (jnp.dot(hi, wrh_ref[...], preferred_element_type=F32)
          + jnp.dot(hi, wrl_ref[...], preferred_element_type=F32)
          + jnp.dot(lo, wrh_ref[...], preferred_element_type=F32))

    lane = lax.broadcasted_iota(jnp.int32, lg.shape, 1).astype(F32)
    big = float(V7X_LANES)
    ninf = -jnp.inf
    gmask = lane < N_GROUPS
    gmax = jnp.max(jnp.where(gmask, lg, ninf), axis=-1, keepdims=True)
    g_sel = jnp.min(jnp.where(gmask & (lg == gmax), lane, big), axis=-1, keepdims=True)
    g_w = 1.0 / jnp.sum(jnp.where(gmask, jnp.exp(lg - gmax), 0.0), axis=-1, keepdims=True)
    lo_lane = ROUTER_EXPERT_LANE + g_sel * EXPERTS_PER_GROUP
    emask = (lane >= lo_lane) & (lane < lo_lane + EXPERTS_PER_GROUP)
    v1 = jnp.max(jnp.where(emask, lg, ninf), axis=-1, keepdims=True)
    i1 = jnp.min(jnp.where(emask & (lg == v1), lane, big), axis=-1, keepdims=True)
    emask2 = emask & (lane != i1)
    v2 = jnp.max(jnp.where(emask2, lg, ninf), axis=-1, keepdims=True)
    i2 = jnp.min(jnp.where(emask2 & (lg == v2), lane, big), axis=-1, keepdims=True)
    e2x = jnp.exp(v2 - v1)
    den = 1.0 + e2x
    wt1 = (1.0 / den) * g_w
    wt2 = (e2x / den) * g_w

    oh1 = lane == i1
    oh2 = lane == i2
    oh = (oh1 | oh2).astype(F32)
    ri = lax.broadcasted_iota(jnp.int32, (tm, tm), 0)
    ci = lax.broadcasted_iota(jnp.int32, (tm, tm), 1)
    before = (ci < ri).astype(BF16)
    prefix = jnp.dot(before, oh.astype(BF16), preferred_element_type=F32)
    rank1 = jnp.sum(jnp.where(oh1, prefix, 0.0), axis=-1, keepdims=True)
    rank2 = jnp.sum(jnp.where(oh2, prefix, 0.0), axis=-1, keepdims=True)
    cnt_out[...] = jnp.broadcast_to(jnp.sum(oh, axis=0, keepdims=True), cnt_out.shape)

    e1 = i1 - ROUTER_EXPERT_LANE
    e2 = i2 - ROUTER_EXPERT_LANE
    col = lax.broadcasted_iota(jnp.int32, (tm, ROUTE_COLS), 1)
    route = jnp.where(col == 0, e1, jnp.where(col == 1, e2, jnp.where(col == 2, wt1, jnp.where(
        col == 3, wt2, jnp.where(col == 4, rank1, jnp.where(col == 5, rank2, 0.0))))))
    route_out[...] = route


def _out_router(x2, od, orw, w_out, moe_g, wr_hi, wr_lo):
    T, D = x2.shape
    tm = min(PROJ_ROWS, T)
    row = lambda i: (i, 0)
    full = lambda i: (0, 0)

    def fs(a):
        return pl.BlockSpec(a.shape, full)

    return pl.pallas_call(
        _out_router_kernel,
        grid=(T // tm,),
        in_specs=[pl.BlockSpec((tm, D), row), pl.BlockSpec((tm, DIFF_WIDTH), row),
                  pl.BlockSpec((tm, RWKV_WIDTH), row), fs(w_out), fs(moe_g), fs(wr_hi), fs(wr_lo)],
        out_specs=[pl.BlockSpec((tm, D), row), pl.BlockSpec((tm, D), row),
                   pl.BlockSpec((tm, ROUTE_COLS), row), pl.BlockSpec((V7X_SUBLANES, V7X_LANES), row)],
        out_shape=[jax.ShapeDtypeStruct((T, D), F32), jax.ShapeDtypeStruct((T, D), BF16),
                   jax.ShapeDtypeStruct((T, ROUTE_COLS), F32),
                   jax.ShapeDtypeStruct((T // tm * V7X_SUBLANES, V7X_LANES), F32)],
        compiler_params=_cparams(("arbitrary",)),
        name="out_router",
    )(x2, od, orw, w_out, moe_g, wr_hi, wr_lo)


RUN_ALIGN = V7X_SUBLANES
TAB_DST, TAB_SRC, TAB_LEN, TAB_NUSED = 0, N_EXPERTS, 2 * N_EXPERTS, 3 * N_EXPERTS


def _stage_rows(tm):
    need = 2 * tm + N_EXPERTS * (RUN_ALIGN - 1)
    return -(-need // V7X_LANES) * V7X_LANES


def _run_copies(tab_ref, max_len, make_copy, op):
    n_bits = int(math.log2(max_len // RUN_ALIGN)) + 1

    def per_expert(e, c):
        n = tab_ref[0, 0, TAB_LEN + e]
        dst = tab_ref[0, 0, TAB_DST + e]
        src = tab_ref[0, 0, TAB_SRC + e]
        off = jnp.int32(0)
        for bit in reversed(range(n_bits)):
            size = RUN_ALIGN << bit
            hit = (n & size) != 0

            @pl.when(hit)
            def _(off=off, size=size):
                cp = make_copy(pl.multiple_of(src + off, RUN_ALIGN), pl.multiple_of(dst + off, RUN_ALIGN), size)
                getattr(cp, op)()

            off = off + jnp.where(hit, size, 0)
        return c

    lax.fori_loop(0, N_EXPERTS, per_expert, 0)


def _dispatch_kernel(tab_ref, tail_ref, sl_ref, hm_ref, xb_ref, stage_ref, zero_ref, sem, zsem):
    tm = hm_ref.shape[0]
    ns = stage_ref.shape[0]
    bm = zero_ref.shape[0]
    nb = xb_ref.shape[0] // bm
    n_used = tail_ref[0, 0, TAB_NUSED]

    def make_copy(src, dst, size):
        return pltpu.make_async_copy(stage_ref.at[pl.ds(src, size)], xb_ref.at[pl.ds(dst, size)], sem)

    def zero_copy(src, dst, size):
        return pltpu.make_async_copy(zero_ref.at[pl.ds(src, size)], xb_ref.at[pl.ds(dst, size)], zsem)

    def zero_blocks(op):
        def one(b, c):
            getattr(zero_copy(0, pl.multiple_of(b * bm, bm), bm), op)()
            return c
        lax.fori_loop(n_used, nb, one, 0)

    @pl.when(pl.program_id(0) == 0)
    def _():
        zero_ref[...] = jnp.zeros_like(zero_ref)
        _run_copies(tail_ref, bm, zero_copy, "start")
        zero_blocks("start")

    srow = lax.broadcasted_iota(jnp.int32, (ns, tm), 0)
    sel = (srow == sl_ref[0, 0:1, :]) | (srow == sl_ref[0, 1:2, :])
    stage_ref[...] = jnp.dot(sel.astype(BF16), hm_ref[...], preferred_element_type=F32)
    _run_copies(tab_ref, tm, make_copy, "start")
    _run_copies(tab_ref, tm, make_copy, "wait")

    @pl.when(pl.program_id(0) == pl.num_programs(0) - 1)
    def _():
        _run_copies(tail_ref, bm, zero_copy, "wait")
        zero_blocks("wait")


def _dispatch(tab, tail, sl_rows, hm, P):
    T, D = hm.shape
    tm = min(PROJ_ROWS, T)
    return pl.pallas_call(
        _dispatch_kernel,
        grid=(T // tm,),
        in_specs=[pl.BlockSpec((1, 1, tab.shape[-1]), lambda i: (i, 0, 0), memory_space=pltpu.SMEM),
                  pl.BlockSpec((1, 1, tail.shape[-1]), lambda i: (0, 0, 0), memory_space=pltpu.SMEM),
                  pl.BlockSpec((1, 2, tm), lambda i: (i, 0, 0)),
                  pl.BlockSpec((tm, D), lambda i: (i, 0))],
        out_specs=pl.BlockSpec(memory_space=pl.ANY),
        out_shape=jax.ShapeDtypeStruct((P, D), F32),
        scratch_shapes=[pltpu.VMEM((_stage_rows(tm), D), F32), pltpu.VMEM((MOE_ROWS, D), F32),
                        pltpu.SemaphoreType.DMA(()), pltpu.SemaphoreType.DMA(())],
        compiler_params=_cparams(("arbitrary",)),
        name="moe_dispatch",
    )(tab, tail, sl_rows, hm)


def _expert_kernel(be_ref, nu_ref, xb_ref, wg_ref, wu_ref, wd_ref, yb_ref, wgb, wub, wdb):
    i = pl.program_id(0)
    prev = be_ref[jnp.maximum(i - 1, 0)]
    changed = (i == 0) | (be_ref[i] != prev)

    @pl.when(changed)
    def _():
        wgb[...] = wg_ref[0].astype(BF16)
        wub[...] = wu_ref[0].astype(BF16)
        wdb[...] = wd_ref[0].astype(BF16)

    @pl.when(i < nu_ref[0])
    def _():
        xb = xb_ref[...].astype(BF16)
        gate = jnp.dot(xb, wgb[...], preferred_element_type=F32)
        up = jnp.dot(xb, wub[...], preferred_element_type=F32)
        hdn = (gate * jax.nn.sigmoid(gate)) * up
        yb_ref[...] = jnp.dot(hdn.astype(BF16), wdb[...], preferred_element_type=F32)

    @pl.when(i >= nu_ref[0])
    def _():
        yb_ref[...] = jnp.zeros_like(yb_ref)


def _experts(block_e, n_used, xb, w_gate, w_up, w_down):
    P, D = xb.shape
    bm = MOE_ROWS
    nb = P // bm
    E = D_EXPERT
    grid_spec = pltpu.PrefetchScalarGridSpec(
        num_scalar_prefetch=2,
        grid=(nb,),
        in_specs=[pl.BlockSpec((bm, D), lambda i, be, nu: (jnp.minimum(i, nu[0] - 1), 0)),
                  pl.BlockSpec((1, D, E), lambda i, be, nu: (be[i], 0, 0)),
                  pl.BlockSpec((1, D, E), lambda i, be, nu: (be[i], 0, 0)),
                  pl.BlockSpec((1, E, D), lambda i, be, nu: (be[i], 0, 0))],
        out_specs=pl.BlockSpec((bm, D), lambda i, be, nu: (i, 0)),
        scratch_shapes=[pltpu.VMEM((D, E), BF16), pltpu.VMEM((D, E), BF16), pltpu.VMEM((E, D), BF16)],
    )
    return pl.pallas_call(
        _expert_kernel,
        grid_spec=grid_spec,
        out_shape=jax.ShapeDtypeStruct((P, D), F32),
        compiler_params=_cparams(("arbitrary",)),
        name="moe_experts",
    )(block_e, n_used, xb, w_gate, w_up, w_down)


def _final_kernel(tab_ref, x1_ref, route_ref, p_ref, wpl_ref, plg_ref, pgg_ref, wgate_ref, yb_ref,
                  o_ref, stage_ref, sem):
    tm = x1_ref.shape[0]
    ns = stage_ref.shape[0]
    stage_ref[...] = jnp.zeros_like(stage_ref)

    def make_copy(src, dst, size):
        return pltpu.make_async_copy(yb_ref.at[pl.ds(dst, size)], stage_ref.at[pl.ds(src, size)], sem)

    _run_copies(tab_ref, tm, make_copy, "start")
    pe = _rms(jnp.dot(p_ref[...].astype(BF16), wpl_ref[...], preferred_element_type=F32),
              plg_ref[...], NORM_EPS)
    _run_copies(tab_ref, tm, make_copy, "wait")

    route = route_ref[...]
    scol = lax.broadcasted_iota(jnp.int32, (tm, ns), 1).astype(F32)
    wsel = (jnp.where(scol == route[:, 0:1], route[:, 2:3], 0.0)
            + jnp.where(scol == route[:, 1:2], route[:, 3:4], 0.0))
    w_hi = wsel.astype(BF16)
    w_lo = (wsel - w_hi.astype(F32)).astype(BF16)
    yb16 = stage_ref[...].astype(BF16)
    moe = (jnp.dot(w_hi, yb16, preferred_element_type=F32)
           + jnp.dot(w_lo, yb16, preferred_element_type=F32))
    x2 = x1_ref[...] + moe
    gate = jax.nn.sigmoid(_dot(_rms(x2, pgg_ref[...], NORM_EPS), wgate_ref[...]))
    o_ref[...] = x2 + pe * gate


def _final(tab, x1, route, p2, w_pl, pl_g, pl_gate_g, w_gate, yb):
    T, D = x1.shape
    tm = min(PROJ_ROWS, T)
    row = lambda i: (i, 0)
    full = lambda i: (0, 0)

    def fs(a):
        return pl.BlockSpec(a.shape, full)

    return pl.pallas_call(
        _final_kernel,
        grid=(T // tm,),
        in_specs=[pl.BlockSpec((1, 1, tab.shape[-1]), lambda i: (i, 0, 0), memory_space=pltpu.SMEM),
                  pl.BlockSpec((tm, D), row), pl.BlockSpec((tm, ROUTE_COLS), row),
                  pl.BlockSpec((tm, PL_DIM), row), fs(w_pl), fs(pl_g), fs(pl_gate_g), fs(w_gate),
                  pl.BlockSpec(memory_space=pl.ANY)],
        out_specs=pl.BlockSpec((tm, D), row),
        out_shape=jax.ShapeDtypeStruct((T, D), F32),
        scratch_shapes=[pltpu.VMEM((_stage_rows(tm), D), F32), pltpu.SemaphoreType.DMA(())],
        compiler_params=_cparams(("arbitrary",)),
        name="combine_final",
    )(tab, x1, route, p2, w_pl, pl_g, pl_gate_g, w_gate, yb)


def kernel(x, p, positions, attn_norm_g, w_in, q_norm_g, k_norm_g, lambda_q1, lambda_k1, lambda_q2, lambda_k2, diff_out_g, mu_rkv, mu_wag, w0, w_lora_a, w_lora_b, a0, a_lora_a, a_lora_b, g_lora_a, g_lora_b, k_k, k_a, r_k, gn_w, gn_b, w_out, moe_norm_g, w_group, w_expert_router, w_gate, w_up, w_down, w_pl, pl_norm_g, pl_gate_norm_g, w_pl_gate):
    B, S, D = x.shape
    T = B * S
    assert p.shape[0] == 1, "one layer"
    x2 = x.reshape(T, D)
    p2 = p[0].reshape(T, PL_DIM)
    pos = positions.astype(F32).reshape(T, 1)

    def row(a):
        return a.reshape(1, -1).astype(F32)

    half = DIFF_HEAD_DIM // 2
    inv_freq = ROPE_THETA ** (-jnp.arange(half, dtype=F32) / half)
    invf = jnp.tile(inv_freq, V7X_LANES // half).reshape(1, V7X_LANES)
    li = jnp.arange(DIFF_WIDTH) // DIFF_HEAD_DIM
    seg = (li[:, None] == li[None, :]).astype(BF16)
    reps = DIFF_WIDTH // DIFF_HEAD_DIM

    q, k, v, zr, zk, zv, wp, ap, gg = _in_proj(
        x2, pos, row(attn_norm_g[0]), w_in[0].astype(BF16),
        w_lora_a[0].astype(BF16), a_lora_a[0].astype(BF16), g_lora_a[0].astype(BF16),
        w_lora_b[0].astype(BF16), a_lora_b[0].astype(BF16), g_lora_b[0].astype(BF16),
        mu_wag[0].astype(F32), row(jnp.tile(q_norm_g[0], reps)), row(jnp.tile(k_norm_g[0], reps)),
        invf, seg, B, S)

    o_diff = _attention(q, k, v, row(lambda_q1[0]), row(lambda_k1[0]), row(lambda_q2[0]),
                        row(lambda_k2[0]), diff_out_g[0].astype(F32).reshape(-1, 1), B, S)
    o_rwkv = _rwkv(zr, zk, zv, wp, ap, gg, mu_rkv[0].astype(F32), row(w0[0]), row(a0[0]),
                   row(k_k[0]), row(k_a[0]), row(r_k[0]), row(gn_w[0]), row(gn_b[0]), seg, B, S)

    wr = jnp.concatenate([w_group[0], jnp.transpose(w_expert_router[0], (1, 0, 2)).reshape(D, N_EXPERTS)],
                         axis=1).astype(F32)
    wr = jnp.pad(wr, ((0, 0), (0, V7X_LANES - wr.shape[1])))
    wr_hi = wr.astype(BF16)
    wr_lo = (wr - wr_hi.astype(F32)).astype(BF16)
    x1, hm, route, cnt = _out_router(x2, o_diff, o_rwkv, w_out[0].astype(BF16), row(moe_norm_g[0]),
                                     wr_hi, wr_lo)

    bm = MOE_ROWS
    tm = min(PROJ_ROWS, T)
    nt = T // tm
    i32 = jnp.int32
    cnt_te = cnt.reshape(nt, V7X_SUBLANES, V7X_LANES)[:, 0, ROUTER_EXPERT_LANE:ROUTER_EXPERT_LANE + N_EXPERTS]
    len_te = (cnt_te.astype(i32) + RUN_ALIGN - 1) // RUN_ALIGN * RUN_ALIGN
    tot_e = jnp.sum(len_te, axis=0)
    pcounts = (tot_e + bm - 1) // bm * bm
    pends = jnp.cumsum(pcounts)
    pstarts = pends - pcounts
    dst_te = pstarts[None, :] + jnp.cumsum(len_te, axis=0) - len_te
    src_te = jnp.cumsum(len_te, axis=1) - len_te
    tab = jnp.concatenate([dst_te, src_te, len_te, jnp.zeros_like(len_te)], axis=1).reshape(nt, 1, 4 * N_EXPERTS)
    nb = -(-(2 * T + nt * N_EXPERTS * (RUN_ALIGN - 1)) // bm) + N_EXPERTS
    P = nb * bm
    n_used = (pends[-1] // bm).astype(i32).reshape(1)
    block_start = jnp.arange(nb, dtype=i32) * bm
    block_e = jnp.minimum(jnp.sum((pends[None, :] <= block_start[:, None]).astype(i32), axis=1),
                          N_EXPERTS - 1)
    eid = route[:, 0:2].astype(i32).reshape(nt, tm, 2)
    run_src = jnp.sum(jnp.where(eid[..., None] == jnp.arange(N_EXPERTS, dtype=i32), src_te[:, None, None, :], 0),
                      axis=-1)
    sl = run_src + route[:, 4:6].astype(i32).reshape(nt, tm, 2)
    sl_rows = jnp.transpose(sl, (0, 2, 1))
    route2 = jnp.concatenate([sl.reshape(T, 2).astype(F32), route[:, 2:4],
                              jnp.zeros((T, ROUTE_COLS - 4), F32)], axis=1)

    zero_e = jnp.zeros_like(tot_e)
    tail = jnp.concatenate([pstarts + tot_e, zero_e, pcounts - tot_e,
                            jnp.broadcast_to(n_used, (N_EXPERTS,))]).reshape(1, 1, 4 * N_EXPERTS)
    xb = _dispatch(tab, tail, sl_rows, hm, P)
    yb = _experts(block_e, n_used, xb, w_gate[0], w_up[0], w_down[0])
    out = _final(tab, x1, route2, p2, w_pl[0].astype(BF16), row(pl_norm_g[0]), row(pl_gate_norm_g[0]),
                 w_pl_gate[0].astype(BF16), yb)
    return out.reshape(B, S, D)
```

```python
import functools
import math

import jax
import jax.numpy as jnp
from jax import lax
from jax.experimental import pallas as pl
from jax.experimental.pallas import tpu as pltpu

F32 = jnp.float32
BF16 = jnp.bfloat16

D_MODEL = 1024
PL_DIM = 256
DIFF_WIDTH = 512
RWKV_WIDTH = 512
DIFF_HEAD_DIM = 64
DIFF_HEADS = 4
RWKV_HEAD = 64
RWKV_HEADS = 8
D_DECAY_LORA = 64
D_AAA_LORA = 64
D_GATE_LORA = 160
ROPE_THETA = 10000.0
NORM_EPS = 1e-6
SUBLN_EPS = 1e-5
GN_EPS = 64e-5
N_GROUPS = 4
EXPERTS_PER_GROUP = 8
N_EXPERTS = 32
D_EXPERT = 512
LAM_INIT = 0.8 - 0.6 * math.exp(0.0)

V7X_LANES = 128
V7X_SUBLANES = 8
V7X_VMEM_BYTES = 64 * 1024 * 1024

IN_PROJ_ROWS = 256
PROJ_ROWS = 256
ATTN_ROWS = 256
ATTN_HEADS_PER_STEP = 4
RWKV_CHUNK = 64
RWKV_ROWS = 256
MOE_ROWS = 512
VMEM_LIMIT = 56 * 1024 * 1024


def _cparams(sem):
    return pltpu.CompilerParams(dimension_semantics=sem, vmem_limit_bytes=VMEM_LIMIT)


def _dot(a, b):
    return jnp.dot(a.astype(BF16), b.astype(BF16), preferred_element_type=F32)


def _dot_nt(a, b):
    return lax.dot_general(a.astype(BF16), b.astype(BF16), (((1,), (1,)), ((), ())),
                           preferred_element_type=F32)


def _dot_tn(a, b):
    return lax.dot_general(a.astype(BF16), b.astype(BF16), (((0,), (0,)), ((), ())),
                           preferred_element_type=F32)


def _split3(x):
    hi = x.astype(BF16)
    r1 = x - hi.astype(F32)
    mid = r1.astype(BF16)
    lo = (r1 - mid.astype(F32)).astype(BF16)
    return hi, mid, lo


def _dot_exact_rhs(x, m01):
    hi = x.astype(BF16)
    mid = (x - hi.astype(F32)).astype(BF16)
    return (jnp.dot(hi, m01, preferred_element_type=F32)
            + jnp.dot(mid, m01, preferred_element_type=F32))


def _dot_exact_lhs(m01, x):
    hi, mid, lo = _split3(x)
    return (jnp.dot(m01, hi, preferred_element_type=F32)
            + jnp.dot(m01, mid, preferred_element_type=F32)
            + jnp.dot(m01, lo, preferred_element_type=F32))


PACKED = jnp.uint32


def _pack16(x):
    h = x.shape[1] // 2
    lo = lax.bitcast_convert_type(x[:, :h], PACKED) >> 16
    hi = lax.bitcast_convert_type(x[:, h:], PACKED) & jnp.uint32(0xFFFF0000)
    return hi | lo


def _unpack16(w):
    lo = lax.bitcast_convert_type(w << 16, F32)
    hi = lax.bitcast_convert_type(w & jnp.uint32(0xFFFF0000), F32)
    return jnp.concatenate([lo, hi], axis=1)


def _rms(x, g, eps):
    return x * lax.rsqrt(jnp.mean(x * x, axis=-1, keepdims=True) + eps) * g


def _shift_rows(z, prev_row):
    rolled = pltpu.roll(z, 1, axis=0)
    row = lax.broadcasted_iota(jnp.int32, z.shape, 0)
    return jnp.where(row == 0, prev_row, rolled)


def _in_proj_kernel(x_ref, pos_ref, g_ref, win_ref, w1_ref, a1_ref, g1_ref, w2_ref, a2_ref, g2_ref,
                    mu_ref, qg_ref, kg_ref, invf_ref, seg_ref,
                    q_out, k_out, v_out, zr_out, zk_out, zv_out, wp_out, ap_out, gg_out,
                    carry_ref):
    tm = x_ref.shape[0]

    @pl.when(pl.program_id(1) == 0)
    def _():
        carry_ref[...] = jnp.zeros_like(carry_ref)

    hn = _rms(x_ref[...], g_ref[...], NORM_EPS)
    prev = carry_ref[V7X_SUBLANES - 1:V7X_SUBLANES, :]
    dh = _shift_rows(hn, prev) - hn
    carry_ref[...] = hn[tm - V7X_SUBLANES:tm, :]

    hb = hn.astype(BF16)
    w = DIFF_WIDTH

    def proj(c):
        return jnp.dot(hb, win_ref[:, c * w:(c + 1) * w], preferred_element_type=F32)

    ang = pos_ref[...] * invf_ref[...]
    cos1, sin1 = jnp.cos(ang), jnp.sin(ang)
    cosf = jnp.concatenate([cos1] * (w // V7X_LANES), axis=1)
    sinf = jnp.concatenate([sin1] * (w // V7X_LANES), axis=1)
    lane = lax.broadcasted_iota(jnp.int32, (tm, w), 1)
    first_half = (lane % DIFF_HEAD_DIM) < (DIFF_HEAD_DIM // 2)
    sin_signed = jnp.where(first_half, -sinf, sinf)
    half = DIFF_HEAD_DIM // 2

    def norm_rope(z, gain):
        ss = _dot_exact_rhs(z * z, seg_ref[...])
        zn = z * lax.rsqrt(ss * (1.0 / DIFF_HEAD_DIM) + NORM_EPS) * gain
        partner = jnp.where(first_half, pltpu.roll(zn, w - half, axis=1), pltpu.roll(zn, half, axis=1))
        return zn * cosf + partner * sin_signed

    scale = DIFF_HEAD_DIM ** -0.5
    q_out[...] = (norm_rope(proj(0), qg_ref[...]) * scale).astype(q_out.dtype)
    k_out[...] = norm_rope(proj(1), kg_ref[...]).astype(k_out.dtype)
    v_out[...] = proj(2).astype(v_out.dtype)
    zr_out[...] = proj(3)
    zk_out[...] = proj(4)
    zv_out[...] = proj(5)

    xw = hn + dh * mu_ref[0:1, :]
    xa = hn + dh * mu_ref[1:2, :]
    xg = hn + dh * mu_ref[2:3, :]
    wp_out[...] = _dot(jnp.tanh(_dot(xw, w1_ref[...])), w2_ref[...])
    ap_out[...] = _dot(_dot(xa, a1_ref[...]), a2_ref[...])
    gg_out[...] = _dot(jax.nn.sigmoid(_dot(xg, g1_ref[...])), g2_ref[...])


def _in_proj(x2, pos, attn_g, w_in, w1, a1, g1, w2, a2, g2, mu_wag, qg, kg, invf, seg, B, S):
    T, D = x2.shape
    tm = min(IN_PROJ_ROWS, S)
    ns = S // tm
    w = DIFF_WIDTH
    row = lambda b, i: (b * ns + i, 0)
    full = lambda b, i: (0, 0)

    def fs(a):
        return pl.BlockSpec(a.shape, full)

    outs = ([jax.ShapeDtypeStruct((T, w), BF16)] * 3 + [jax.ShapeDtypeStruct((T, w), F32)] * 6)
    return pl.pallas_call(
        _in_proj_kernel,
        grid=(B, ns),
        in_specs=[pl.BlockSpec((tm, D), row), pl.BlockSpec((tm, 1), row), fs(attn_g), fs(w_in),
                  fs(w1), fs(a1), fs(g1), fs(w2), fs(a2), fs(g2), fs(mu_wag), fs(qg), fs(kg),
                  fs(invf), fs(seg)],
        out_specs=[pl.BlockSpec((tm, w), row)] * 9,
        out_shape=outs,
        scratch_shapes=[pltpu.VMEM((V7X_SUBLANES, D), F32)],
        compiler_params=_cparams(("arbitrary", "arbitrary")),
        name="in_proj",
    )(x2, pos, attn_g, w_in, w1, a1, g1, w2, a2, g2, mu_wag, qg, kg, invf, seg)


def _attn_kernel(lq1_ref, lk1_ref, lq2_ref, lk2_ref, ogt_ref, q_ref, k_ref, v_ref, o_ref,
                 vt_ref, m_ref, l_ref, acc_ref):
    tq = q_ref.shape[0]
    tk = tq
    nh, nk, hw = vt_ref.shape[0], vt_ref.shape[1], vt_ref.shape[2]
    heads = range(nh)
    i = pl.program_id(2)
    lam = (jnp.exp(jnp.sum(lq1_ref[...] * lk1_ref[...], axis=-1, keepdims=True))
           - jnp.exp(jnp.sum(lq2_ref[...] * lk2_ref[...], axis=-1, keepdims=True)) + LAM_INIT)

    @pl.when(i == 0)
    def _():
        for h in heads:
            for c in range(nk):
                vt_ref[h, c] = (v_ref[c * tk:(c + 1) * tk, h * hw:(h + 1) * hw]
                                .astype(F32).T.astype(vt_ref.dtype))

    lane = lax.broadcasted_iota(jnp.int32, (tq, hw), 1)
    qs = []
    for h in heads:
        q = q_ref[:, h * hw:(h + 1) * hw]
        zero = jnp.zeros_like(q)
        qs.append(jnp.concatenate([jnp.where(lane < DIFF_HEAD_DIM, q, zero),
                                   jnp.where(lane >= DIFF_HEAD_DIM, q, zero)], axis=0))

    m_ref[...] = jnp.full_like(m_ref, -jnp.inf)
    l_ref[...] = jnp.zeros_like(l_ref)
    acc_ref[...] = jnp.zeros_like(acc_ref)

    def step(j, masked):
        start = pl.multiple_of(j * tk, tk)
        st = [lax.dot_general(k_ref[pl.ds(start, tk), h * hw:(h + 1) * hw], qs[h],
                              (((1,), (1,)), ((), ())), preferred_element_type=F32) for h in heads]
        if masked:
            krow = lax.broadcasted_iota(jnp.int32, st[0].shape, 0)
            qcol = lax.broadcasted_iota(jnp.int32, st[0].shape, 1)
            qcol = jnp.where(qcol >= tq, qcol - tq, qcol)
            st = [jnp.where(krow <= qcol, s, -jnp.inf) for s in st]
        m_prev = [m_ref[h] for h in heads]
        m_new = [jnp.maximum(mp, jnp.max(s, axis=0, keepdims=True)) for mp, s in zip(m_prev, st)]
        alpha = [jnp.exp(mp - mn) for mp, mn in zip(m_prev, m_new)]
        pt = [jnp.exp(s - mn) for s, mn in zip(st, m_new)]
        pv = [jnp.dot(vt_ref[h, j], pt[h].astype(BF16), preferred_element_type=F32) for h in heads]
        for h in heads:
            l_ref[h] = alpha[h] * l_ref[h] + jnp.sum(pt[h], axis=0, keepdims=True)
            acc_ref[h] = alpha[h] * acc_ref[h] + pv[h]
            m_ref[h] = m_new[h]

    def body(j, carry):
        step(j, False)
        return carry

    lax.fori_loop(0, i, body, 0)
    step(i, True)

    for h in heads:
        acc = acc_ref[h]
        l = l_ref[h]
        ot = acc[:, :tq] / l[:, :tq] - lam * (acc[:, tq:] / l[:, tq:])
        ot = ot * lax.rsqrt(jnp.mean(ot * ot, axis=0, keepdims=True) + SUBLN_EPS) * ogt_ref[...]
        o_ref[:, h * hw:(h + 1) * hw] = (ot * (1.0 - LAM_INIT)).T.astype(o_ref.dtype)


def _attention(q, k, v, lq1, lk1, lq2, lk2, og, B, S):
    T = q.shape[0]
    tq = min(ATTN_ROWS, S)
    nq = S // tq
    hw = 2 * DIFF_HEAD_DIM
    nh = ATTN_HEADS_PER_STEP
    gw = nh * hw
    small = lambda b, h, i: (0, 0)
    return pl.pallas_call(
        _attn_kernel,
        grid=(B, DIFF_HEADS // nh, nq),
        in_specs=[pl.BlockSpec(lq1.shape, small), pl.BlockSpec(lk1.shape, small),
                  pl.BlockSpec(lq2.shape, small), pl.BlockSpec(lk2.shape, small),
                  pl.BlockSpec(og.shape, small),
                  pl.BlockSpec((tq, gw), lambda b, h, i: (b * nq + i, h)),
                  pl.BlockSpec((S, gw), lambda b, h, i: (b, h)),
                  pl.BlockSpec((S, gw), lambda b, h, i: (b, h))],
        out_specs=pl.BlockSpec((tq, gw), lambda b, h, i: (b * nq + i, h)),
        out_shape=jax.ShapeDtypeStruct((T, DIFF_WIDTH), BF16),
        scratch_shapes=[pltpu.VMEM((nh, nq, hw, tq), BF16), pltpu.VMEM((nh, 1, 2 * tq), F32),
                        pltpu.VMEM((nh, 1, 2 * tq), F32), pltpu.VMEM((nh, hw, 2 * tq), F32)],
        compiler_params=_cparams(("arbitrary", "arbitrary", "arbitrary")),
        name="diff_attn",
    )(lq1, lk1, lq2, lk2, og, q, k, v)


def _rwkv_kernel(zr_ref, zk_ref, zv_ref, wp_ref, ap_ref, gg_ref, mu_ref, w0_ref, a0_ref, kk_ref,
                 ka_ref, rk_ref, gnw_ref, gnb_ref, seg_ref, o_ref, state_ref, carry_ref):
    R = zr_ref.shape[0]
    C = min(RWKV_CHUNK, R)
    nch = R // C
    N = RWKV_HEAD
    W = RWKV_WIDTH
    HP = 2 * N
    n_pairs = W // HP

    @pl.when(pl.program_id(1) == 0)
    def _():
        state_ref[...] = jnp.zeros_like(state_ref)
        carry_ref[...] = jnp.zeros_like(carry_ref)

    zr, zk, zv = zr_ref[...], zk_ref[...], zv_ref[...]
    last = V7X_SUBLANES - 1
    r = zr + (_shift_rows(zr, carry_ref[last:last + 1, 0:W]) - zr) * mu_ref[0:1, :]
    k = zk + (_shift_rows(zk, carry_ref[last:last + 1, W:2 * W]) - zk) * mu_ref[1:2, :]
    v = zv + (_shift_rows(zv, carry_ref[last:last + 1, 2 * W:3 * W]) - zv) * mu_ref[2:3, :]
    carry_ref[:, 0:W] = zr[R - V7X_SUBLANES:R, :]
    carry_ref[:, W:2 * W] = zk[R - V7X_SUBLANES:R, :]
    carry_ref[:, 2 * W:3 * W] = zv[R - V7X_SUBLANES:R, :]

    lw = -math.exp(-0.5) * jax.nn.sigmoid(w0_ref[...] + wp_ref[...])
    a = jax.nn.sigmoid(a0_ref[...] + ap_ref[...])
    kk = k * kk_ref[...]
    kk = kk * lax.rsqrt(jnp.maximum(_dot_exact_rhs(kk * kk, seg_ref[...]), 1e-24))
    k2 = k * (1.0 + (a - 1.0) * ka_ref[...])
    a_s = -kk
    b_s = kk * a

    rr = lax.broadcasted_iota(jnp.int32, (R, R), 0)
    cc = lax.broadcasted_iota(jnp.int32, (R, R), 1)
    same_chunk = (rr // C) == (cc // C)
    L = _dot_exact_lhs((same_chunk & (cc <= rr)).astype(BF16), lw)
    bonus_w = r * k2 * rk_ref[...]

    P2 = 2 * C
    sr = lax.broadcasted_iota(jnp.int32, (P2, HP), 0)
    sc = lax.broadcasted_iota(jnp.int32, (P2, HP), 1)
    stack_mask = (sr < C) == (sc < N)
    br = lax.broadcasted_iota(jnp.int32, (P2, P2), 0)
    bc = lax.broadcasted_iota(jnp.int32, (P2, P2), 1)
    same_head = (br < C) == (bc < C)
    tr = jnp.where(br >= C, br - C, br)
    tc = jnp.where(bc >= C, bc - C, bc)
    strict = same_head & (tc < tr)
    lower = same_head & (tc <= tr)
    eye_p = (br == bc).astype(F32)
    kr = lax.broadcasted_iota(jnp.int32, (HP, HP), 0)
    kc = lax.broadcasted_iota(jnp.int32, (HP, HP), 1)
    eye_k = kr == kc

    def dup(x):
        return jnp.concatenate([x, x], axis=0)

    def stack(x):
        return jnp.where(stack_mask, dup(x), 0.0)

    n_sq = int(math.log2(C)) - 1
    items = [(c, p) for c in range(nch) for p in range(n_pairs)]

    def prep(c, p):
        rows = slice(c * C, (c + 1) * C)
        lanes = slice(p * HP, (p + 1) * HP)
        Lc = L[rows, lanes]
        lwc = lw[rows, lanes]
        LC = Lc[C - 1:C, :]
        enL = jnp.exp(-Lc)
        eCL = jnp.exp(LC - Lc)
        b_c, k_c = b_s[rows, lanes], k2[rows, lanes]
        return dict(
            xa=stack(a_s[rows, lanes] * jnp.exp(Lc - lwc)), xr=stack(r[rows, lanes] * jnp.exp(Lc)),
            bt=dup(b_c * enL), kt=dup(k_c * enL), bh=stack(b_c * eCL), kh=stack(k_c * eCL),
            vs=stack(v[rows, lanes]), pc=jnp.exp(LC))

    d = [prep(c, p) for c, p in items]
    aa = [_dot_nt(jnp.concatenate([e["xa"], e["xr"]], axis=0),
                  jnp.concatenate([e["bt"], e["kt"]], axis=0)) for e in d]
    a_ab = [jnp.where(strict, m[:P2, :P2], 0.0) for m in aa]
    a_ak = [jnp.where(strict, m[:P2, P2:], 0.0) for m in aa]
    a_rb = [jnp.where(lower, m[P2:, :P2], 0.0) for m in aa]
    a_rk = [jnp.where(lower, m[P2:, P2:], 0.0) for m in aa]
    tm = [eye_p + m for m in a_ab]
    npw = a_ab
    for _ in range(n_sq):
        npw = [_dot(m, m) for m in npw]
        tm = [t + _dot(t, m) for t, m in zip(tm, npw)]
    av = [_dot(m, e["vs"]) for m, e in zip(a_ak, d)]
    z = [_dot(t, jnp.concatenate([e["xa"], x], axis=1)) for t, e, x in zip(tm, d, av)]
    w = [_dot(m, x) for m, x in zip(a_rb, z)]
    rkv = [_dot(m, e["vs"]) for m, e in zip(a_rk, d)]
    gh = [_dot_tn(e["bh"], x) for e, x in zip(d, z)]
    khv = [_dot_tn(e["kh"], e["vs"]) for e in d]

    for idx, (c, p) in enumerate(items):
        e = d[idx]
        rows = slice(c * C, (c + 1) * C)
        lanes = slice(p * HP, (p + 1) * HP)
        rp = e["xr"] + w[idx][:, :HP]
        y0 = w[idx][:, HP:] + rkv[idx]
        gm = gh[idx][:, :HP]
        hm = gh[idx][:, HP:] + khv[idx]
        st = state_ref[p]
        yg = _dot(jnp.concatenate([rp, gm], axis=0), st)
        pc_col = jnp.sum(jnp.where(eye_k, e["pc"], 0.0), axis=1, keepdims=True)
        state_ref[p] = pc_col * st + yg[P2:] + hm
        ys = yg[:P2] + y0
        mu = jnp.sum(ys, axis=-1, keepdims=True) * (1.0 / N)
        yc = jnp.where(stack_mask, ys - mu, 0.0)
        var = jnp.sum(yc * yc, axis=-1, keepdims=True) * (1.0 / N)
        yn = yc * lax.rsqrt(var + GN_EPS)
        bonus = jnp.sum(stack(bonus_w[rows, lanes]), axis=-1, keepdims=True) * e["vs"]
        yn = yn[:C] + yn[C:]
        bonus = bonus[:C] + bonus[C:]
        o_ref[rows, lanes] = ((yn * gnw_ref[:, lanes] + gnb_ref[:, lanes] + bonus)
                              * gg_ref[rows, lanes]).astype(o_ref.dtype)


def _rwkv(zr, zk, zv, wp, ap, gg, mu_rkv, w0, a0, k_k, k_a, r_k, gn_w, gn_b, seg, B, S):
    T = zr.shape[0]
    C = min(RWKV_ROWS, S)
    nc = S // C
    W = RWKV_WIDTH
    HP = 2 * RWKV_HEAD
    row = lambda b, i: (b * nc + i, 0)
    full = lambda b, i: (0, 0)

    def fs(a):
        return pl.BlockSpec(a.shape, full)

    return pl.pallas_call(
        _rwkv_kernel,
        grid=(B, nc),
        in_specs=[pl.BlockSpec((C, W), row)] * 6 + [fs(mu_rkv), fs(w0), fs(a0), fs(k_k), fs(k_a),
                                                    fs(r_k), fs(gn_w), fs(gn_b), fs(seg)],
        out_specs=pl.BlockSpec((C, W), row),
        out_shape=jax.ShapeDtypeStruct((T, W), BF16),
        scratch_shapes=[pltpu.VMEM((W // HP, HP, HP), F32),
                        pltpu.VMEM((V7X_SUBLANES, 3 * W), F32)],
        compiler_params=_cparams(("arbitrary", "arbitrary")),
        name="rwkv7",
    )(zr, zk, zv, wp, ap, gg, mu_rkv, w0, a0, k_k, k_a, r_k, gn_w, gn_b, seg)


ROUTE_COLS = 8
ROUTER_GROUP_LANE = 0
ROUTER_EXPERT_LANE = N_GROUPS


def _out_router_kernel(x_ref, od_ref, orw_ref, wo_ref, mg_ref, wrh_ref, wrl_ref,
                       x1_out, hm_out, route_out, cnt_out):
    tm = x_ref.shape[0]
    x1 =(x_ref[...] + jnp.dot(od_ref[...], wo_ref[0:DIFF_WIDTH, :], preferred_element_type=F32)
          + jnp.dot(orw_ref[...], wo_ref[DIFF_WIDTH:, :], preferred_element_type=F32))
    x1_out[...] = x1
    hm = _rms(x1, mg_ref[...], NORM_EPS)
    hm_out[...] = hm.astype(hm_out.dtype)

    hi = hm.astype(BF16)
    lo = (hm - hi.astype(F32)).astype(BF16)
    lg = (jnp.dot(hi, wrh_ref[...], preferred_element_type=F32)
          + jnp.dot(hi, wrl_ref[...], preferred_element_type=F32)
          + jnp.dot(lo, wrh_ref[...], preferred_element_type=F32))

    lane = lax.broadcasted_iota(jnp.int32, lg.shape, 1).astype(F32)
    big = float(V7X_LANES)
    ninf = -jnp.inf
    gmask = lane < N_GROUPS
    gmax = jnp.max(jnp.where(gmask, lg, ninf), axis=-1, keepdims=True)
    g_sel = jnp.min(jnp.where(gmask & (lg == gmax), lane, big), axis=-1, keepdims=True)
    g_w = 1.0 / jnp.sum(jnp.where(gmask, jnp.exp(lg - gmax), 0.0), axis=-1, keepdims=True)
    lo_lane = ROUTER_EXPERT_LANE + g_sel * EXPERTS_PER_GROUP
    emask = (lane >= lo_lane) & (lane < lo_lane + EXPERTS_PER_GROUP)
    v1 = jnp.max(jnp.where(emask, lg, ninf), axis=-1, keepdims=True)
    i1 = jnp.min(jnp.where(emask & (lg == v1), lane, big), axis=-1, keepdims=True)
    emask2 = emask & (lane != i1)
    v2 = jnp.max(jnp.where(emask2, lg, ninf), axis=-1, keepdims=True)
    i2 = jnp.min(jnp.where(emask2 & (lg == v2), lane, big), axis=-1, keepdims=True)
    e2x = jnp.exp(v2 - v1)
    den = 1.0 + e2x
    wt1 = (1.0 / den) * g_w
    wt2 = (e2x / den) * g_w

    oh1 = lane == i1
    oh2 = lane == i2
    oh = (oh1 | oh2).astype(F32)
    ri = lax.broadcasted_iota(jnp.int32, (tm, tm), 0)
    ci = lax.broadcasted_iota(jnp.int32, (tm, tm), 1)
    before = (ci < ri).astype(BF16)
    prefix = jnp.dot(before, oh.astype(BF16), preferred_element_type=F32)
    rank1 = jnp.sum(jnp.where(oh1, prefix, 0.0), axis=-1, keepdims=True)
    rank2 = jnp.sum(jnp.where(oh2, prefix, 0.0), axis=-1, keepdims=True)
    cnt_out[...] = jnp.broadcast_to(jnp.sum(oh, axis=0, keepdims=True), cnt_out.shape)

    e1 = i1 - ROUTER_EXPERT_LANE
    e2 = i2 - ROUTER_EXPERT_LANE
    col = lax.broadcasted_iota(jnp.int32, (tm, ROUTE_COLS), 1)
    route = jnp.where(col == 0, e1, jnp.where(col == 1, e2, jnp.where(col == 2, wt1, jnp.where(
        col == 3, wt2, jnp.where(col == 4, rank1, jnp.where(col == 5, rank2, 0.0))))))
    route_out[...] = route


def _out_router(x2, od, orw, w_out, moe_g, wr_hi, wr_lo):
    T, D = x2.shape
    tm = min(PROJ_ROWS, T)
    row = lambda i: (i, 0)
    full = lambda i: (0, 0)

    def fs(a):
        return pl.BlockSpec(a.shape, full)

    return pl.pallas_call(
        _out_router_kernel,
        grid=(T // tm,),
        in_specs=[pl.BlockSpec((tm, D), row), pl.BlockSpec((tm, DIFF_WIDTH), row),
                  pl.BlockSpec((tm, RWKV_WIDTH), row), fs(w_out), fs(moe_g), fs(wr_hi), fs(wr_lo)],
        out_specs=[pl.BlockSpec((tm, D), row), pl.BlockSpec((tm, D), row),
                   pl.BlockSpec((tm, ROUTE_COLS), row), pl.BlockSpec((V7X_SUBLANES, V7X_LANES), row)],
        out_shape=[jax.ShapeDtypeStruct((T, D), F32), jax.ShapeDtypeStruct((T, D), BF16),
                   jax.ShapeDtypeStruct((T, ROUTE_COLS), F32),
                   jax.ShapeDtypeStruct((T // tm * V7X_SUBLANES, V7X_LANES), F32)],
        compiler_params=_cparams(("arbitrary",)),
        name="out_router",
    )(x2, od, orw, w_out, moe_g, wr_hi, wr_lo)


RUN_ALIGN = V7X_SUBLANES
TAB_DST, TAB_SRC, TAB_LEN, TAB_NUSED = 0, N_EXPERTS, 2 * N_EXPERTS, 3 * N_EXPERTS


def _stage_rows(tm):
    need = 2 * tm + N_EXPERTS * (RUN_ALIGN - 1)
    return -(-need // V7X_LANES) * V7X_LANES


def _run_copies(tab_ref, max_len, make_copy, op):
    n_bits = int(math.log2(max_len // RUN_ALIGN)) + 1

    def per_expert(e, c):
        n = tab_ref[0, 0, TAB_LEN + e]
        dst = tab_ref[0, 0, TAB_DST + e]
        src = tab_ref[0, 0, TAB_SRC + e]
        off = jnp.int32(0)
        for bit in reversed(range(n_bits)):
            size = RUN_ALIGN << bit
            hit = (n & size) != 0

            @pl.when(hit)
            def _(off=off, size=size):
                cp = make_copy(pl.multiple_of(src + off, RUN_ALIGN), pl.multiple_of(dst + off, RUN_ALIGN), size)
                getattr(cp, op)()

            off = off + jnp.where(hit, size, 0)
        return c

    lax.fori_loop(0, N_EXPERTS, per_expert, 0)


def _dispatch_kernel(tab_ref, tail_ref, sl_ref, hm_ref, xb_ref, stage_ref, zero_ref, sem, zsem):
    tm = hm_ref.shape[0]
    ns = stage_ref.shape[0]
    bm = zero_ref.shape[0]
    nb = xb_ref.shape[0] // bm
    n_used = tail_ref[0, 0, TAB_NUSED]

    def make_copy(src, dst, size):
        return pltpu.make_async_copy(stage_ref.at[pl.ds(src, size)], xb_ref.at[pl.ds(dst, size)], sem)

    def zero_copy(src, dst, size):
        return pltpu.make_async_copy(zero_ref.at[pl.ds(src, size)], xb_ref.at[pl.ds(dst, size)], zsem)

    def zero_blocks(op):
        def one(b, c):
            getattr(zero_copy(0, pl.multiple_of(b * bm, bm), bm), op)()
            return c
        lax.fori_loop(n_used, nb, one, 0)

    @pl.when(pl.program_id(0) == 0)
    def _():
        zero_ref[...] = jnp.zeros_like(zero_ref)
        _run_copies(tail_ref, bm, zero_copy, "start")
        zero_blocks("start")

    srow = lax.broadcasted_iota(jnp.int32, (ns, tm), 0)
    sel = (srow == sl_ref[0, 0:1, :]) | (srow == sl_ref[0, 1:2, :])
    stage_ref[...] = _pack16(jnp.dot(sel.astype(BF16), hm_ref[...], preferred_element_type=F32))
    _run_copies(tab_ref, tm, make_copy, "start")
    _run_copies(tab_ref, tm, make_copy, "wait")

    @pl.when(pl.program_id(0) == pl.num_programs(0) - 1)
    def _():
        _run_copies(tail_ref, bm, zero_copy, "wait")
        zero_blocks("wait")


def _dispatch(tab, tail, sl_rows, hm, P):
    T, D = hm.shape
    tm = min(PROJ_ROWS, T)
    return pl.pallas_call(
        _dispatch_kernel,
        grid=(T // tm,),
        in_specs=[pl.BlockSpec((1, 1, tab.shape[-1]), lambda i: (i, 0, 0), memory_space=pltpu.SMEM),
                  pl.BlockSpec((1, 1, tail.shape[-1]), lambda i: (0, 0, 0), memory_space=pltpu.SMEM),
                  pl.BlockSpec((1, 2, tm), lambda i: (i, 0, 0)),
                  pl.BlockSpec((tm, D), lambda i: (i, 0))],
        out_specs=pl.BlockSpec(memory_space=pl.ANY),
        out_shape=jax.ShapeDtypeStruct((P, D // 2), PACKED),
        scratch_shapes=[pltpu.VMEM((_stage_rows(tm), D // 2), PACKED), pltpu.VMEM((MOE_ROWS, D // 2), PACKED),
                        pltpu.SemaphoreType.DMA(()), pltpu.SemaphoreType.DMA(())],
        compiler_params=_cparams(("arbitrary",)),
        name="moe_dispatch",
    )(tab, tail, sl_rows, hm)


def _expert_kernel(be_ref, nu_ref, xb_ref, wg_ref, wu_ref, wd_ref, yb_ref, wgb, wub, wdb):
    i = pl.program_id(0)
    prev = be_ref[jnp.maximum(i - 1, 0)]
    changed = (i == 0) | (be_ref[i] != prev)

    @pl.when(changed)
    def _():
        wgb[...] = wg_ref[0].astype(BF16)
        wub[...] = wu_ref[0].astype(BF16)
        wdb[...] = wd_ref[0].astype(BF16)

    @pl.when(i < nu_ref[0])
    def _():
        xb = _unpack16(xb_ref[...]).astype(BF16)
        gate = jnp.dot(xb, wgb[...], preferred_element_type=F32)
        up = jnp.dot(xb, wub[...], preferred_element_type=F32)
        hdn = (gate * jax.nn.sigmoid(gate)) * up
        y = jnp.dot(hdn.astype(BF16), wdb[...], preferred_element_type=F32)
        yb_ref[...] = _pack16(y.astype(BF16).astype(F32))

    @pl.when(i >= nu_ref[0])
    def _():
        yb_ref[...] = jnp.zeros_like(yb_ref)


def _experts(block_e, n_used, xb, w_gate, w_up, w_down):
    P, DP = xb.shape
    D = 2 * DP
    bm = MOE_ROWS
    nb = P // bm
    E = D_EXPERT
    grid_spec = pltpu.PrefetchScalarGridSpec(
        num_scalar_prefetch=2,
        grid=(nb,),
        in_specs=[pl.BlockSpec((bm, DP), lambda i, be, nu: (jnp.minimum(i, nu[0] - 1), 0)),
                  pl.BlockSpec((1, D, E), lambda i, be, nu: (be[i], 0, 0)),
                  pl.BlockSpec((1, D, E), lambda i, be, nu: (be[i], 0, 0)),
                  pl.BlockSpec((1, E, D), lambda i, be, nu: (be[i], 0, 0))],
        out_specs=pl.BlockSpec((bm, DP), lambda i, be, nu: (jnp.where(i < nu[0], i, nb - 1), 0)),
        scratch_shapes=[pltpu.VMEM((D, E), BF16), pltpu.VMEM((D, E), BF16), pltpu.VMEM((E, D), BF16)],
    )
    return pl.pallas_call(
        _expert_kernel,
        grid_spec=grid_spec,
        out_shape=jax.ShapeDtypeStruct((P, DP), PACKED),
        input_output_aliases={2: 0},
        compiler_params=_cparams(("arbitrary",)),
        name="moe_experts",
    )(block_e, n_used, xb, w_gate, w_up, w_down)


def _final_kernel(tab_ref, x1_ref, route_ref, p_ref, wpl_ref, plg_ref, pgg_ref, wgate_ref, yb_ref,
                  o_ref, stage_ref, sem):
    tm = x1_ref.shape[0]
    ns = stage_ref.shape[0]
    stage_ref[...] = jnp.zeros_like(stage_ref)

    def make_copy(src, dst, size):
        return pltpu.make_async_copy(yb_ref.at[pl.ds(dst, size)], stage_ref.at[pl.ds(src, size)], sem)

    _run_copies(tab_ref, tm, make_copy, "start")
    pe = _rms(jnp.dot(p_ref[...].astype(BF16), wpl_ref[...], preferred_element_type=F32),
              plg_ref[...], NORM_EPS)
    _run_copies(tab_ref, tm, make_copy, "wait")

    route = route_ref[...]
    scol = lax.broadcasted_iota(jnp.int32, (tm, ns), 1).astype(F32)
    wsel = (jnp.where(scol == route[:, 0:1], route[:, 2:3], 0.0)
            + jnp.where(scol == route[:, 1:2], route[:, 3:4], 0.0))
    w_hi = wsel.astype(BF16)
    w_lo = (wsel - w_hi.astype(F32)).astype(BF16)
    yb16 = _unpack16(stage_ref[...]).astype(BF16)
    moe =(jnp.dot(w_hi, yb16, preferred_element_type=F32)
           + jnp.dot(w_lo, yb16, preferred_element_type=F32))
    x2 = x1_ref[...] + moe
    gate = jax.nn.sigmoid(_dot(_rms(x2, pgg_ref[...], NORM_EPS), wgate_ref[...]))
    o_ref[...] = x2 + pe * gate


def _final(tab, x1, route, p2, w_pl, pl_g, pl_gate_g, w_gate, yb):
    T, D = x1.shape
    tm = min(PROJ_ROWS, T)
    row = lambda i: (i, 0)
    full = lambda i: (0, 0)

    def fs(a):
        return pl.BlockSpec(a.shape, full)

    return pl.pallas_call(
        _final_kernel,
        grid=(T // tm,),
        in_specs=[pl.BlockSpec((1, 1, tab.shape[-1]), lambda i: (i, 0, 0), memory_space=pltpu.SMEM),
                  pl.BlockSpec((tm, D), row), pl.BlockSpec((tm, ROUTE_COLS), row),
                  pl.BlockSpec((tm, PL_DIM), row), fs(w_pl), fs(pl_g), fs(pl_gate_g), fs(w_gate),
                  pl.BlockSpec(memory_space=pl.ANY)],
        out_specs=pl.BlockSpec((tm, D), row),
        out_shape=jax.ShapeDtypeStruct((T, D), F32),
        scratch_shapes=[pltpu.VMEM((_stage_rows(tm), D // 2), PACKED), pltpu.SemaphoreType.DMA(())],
        compiler_params=_cparams(("arbitrary",)),
        name="combine_final",
    )(tab, x1, route, p2, w_pl, pl_g, pl_gate_g, w_gate, yb)


def kernel(x, p, positions, attn_norm_g, w_in, q_norm_g, k_norm_g, lambda_q1, lambda_k1, lambda_q2, lambda_k2, diff_out_g, mu_rkv, mu_wag, w0, w_lora_a, w_lora_b, a0, a_lora_a, a_lora_b, g_lora_a, g_lora_b, k_k, k_a, r_k, gn_w, gn_b, w_out, moe_norm_g, w_group, w_expert_router, w_gate, w_up, w_down, w_pl, pl_norm_g, pl_gate_norm_g, w_pl_gate):
    B, S, D = x.shape
    T = B * S
    assert p.shape[0] == 1, "one layer"
    x2 = x.reshape(T, D)
    p2 = p[0].reshape(T, PL_DIM)
    pos = positions.astype(F32).reshape(T, 1)

    def row(a):
        return a.reshape(1, -1).astype(F32)

    half = DIFF_HEAD_DIM // 2
    inv_freq = ROPE_THETA ** (-jnp.arange(half, dtype=F32) / half)
    invf = jnp.tile(inv_freq, V7X_LANES // half).reshape(1, V7X_LANES)
    li = jnp.arange(DIFF_WIDTH) // DIFF_HEAD_DIM
    seg = (li[:, None] == li[None, :]).astype(BF16)
    reps = DIFF_WIDTH // DIFF_HEAD_DIM

    q, k, v, zr, zk, zv, wp, ap, gg = _in_proj(
        x2, pos, row(attn_norm_g[0]), w_in[0].astype(BF16),
        w_lora_a[0].astype(BF16), a_lora_a[0].astype(BF16), g_lora_a[0].astype(BF16),
        w_lora_b[0].astype(BF16), a_lora_b[0].astype(BF16), g_lora_b[0].astype(BF16),
        mu_wag[0].astype(F32), row(jnp.tile(q_norm_g[0], reps)), row(jnp.tile(k_norm_g[0], reps)),
        invf, seg, B, S)

    o_diff = _attention(q, k, v, row(lambda_q1[0]), row(lambda_k1[0]), row(lambda_q2[0]),
                        row(lambda_k2[0]), diff_out_g[0].astype(F32).reshape(-1, 1), B, S)
    o_rwkv = _rwkv(zr, zk, zv, wp, ap, gg, mu_rkv[0].astype(F32), row(w0[0]), row(a0[0]),
                   row(k_k[0]), row(k_a[0]), row(r_k[0]), row(gn_w[0]), row(gn_b[0]), seg, B, S)

    wr = jnp.concatenate([w_group[0], jnp.transpose(w_expert_router[0], (1, 0, 2)).reshape(D, N_EXPERTS)],
                         axis=1).astype(F32)
    wr = jnp.pad(wr, ((0, 0), (0, V7X_LANES - wr.shape[1])))
    wr_hi = wr.astype(BF16)
    wr_lo = (wr - wr_hi.astype(F32)).astype(BF16)
    x1, hm, route, cnt = _out_router(x2, o_diff, o_rwkv, w_out[0].astype(BF16), row(moe_norm_g[0]),
                                     wr_hi, wr_lo)

    bm = MOE_ROWS
    tm = min(PROJ_ROWS, T)
    nt = T // tm
    i32 = jnp.int32
    cnt_te = cnt.reshape(nt, V7X_SUBLANES, V7X_LANES)[:, 0, ROUTER_EXPERT_LANE:ROUTER_EXPERT_LANE + N_EXPERTS]
    len_te = (cnt_te.astype(i32) + RUN_ALIGN - 1) // RUN_ALIGN * RUN_ALIGN
    tot_e = jnp.sum(len_te, axis=0)
    pcounts = (tot_e + bm - 1) // bm * bm
    pends = jnp.cumsum(pcounts)
    pstarts = pends - pcounts
    dst_te = pstarts[None, :] + jnp.cumsum(len_te, axis=0) - len_te
    src_te = jnp.cumsum(len_te, axis=1) - len_te
    tab = jnp.concatenate([dst_te, src_te, len_te, jnp.zeros_like(len_te)], axis=1).reshape(nt, 1, 4 * N_EXPERTS)
    nb = -(-(2 * T + nt * N_EXPERTS * (RUN_ALIGN - 1)) // bm) + N_EXPERTS
    P = nb * bm
    n_used = (pends[-1] // bm).astype(i32).reshape(1)
    block_start = jnp.arange(nb, dtype=i32) * bm
    block_e = jnp.minimum(jnp.sum((pends[None, :] <= block_start[:, None]).astype(i32), axis=1),
                          N_EXPERTS - 1)
    eid = route[:, 0:2].astype(i32).reshape(nt, tm, 2)
    run_src = jnp.sum(jnp.where(eid[..., None] == jnp.arange(N_EXPERTS, dtype=i32), src_te[:, None, None, :], 0),
                      axis=-1)
    sl = run_src + route[:, 4:6].astype(i32).reshape(nt, tm, 2)
    sl_rows = jnp.transpose(sl, (0, 2, 1))
    route2 = jnp.concatenate([sl.reshape(T, 2).astype(F32), route[:, 2:4],
                              jnp.zeros((T, ROUTE_COLS - 4), F32)], axis=1)

    zero_e = jnp.zeros_like(tot_e)
    tail = jnp.concatenate([pstarts + tot_e, zero_e, pcounts - tot_e,
                            jnp.broadcast_to(n_used, (N_EXPERTS,))]).reshape(1, 1, 4 * N_EXPERTS)
    xb = _dispatch(tab, tail, sl_rows, hm, P)
    yb = _experts(block_e, n_used, xb, w_gate[0], w_up[0], w_down[0])
    out = _final(tab, x1, route2, p2, w_pl[0].astype(BF16), row(pl_norm_g[0]), row(pl_gate_norm_g[0]),
                 w_pl_gate[0].astype(BF16), yb)
    return out.reshape(B, S, D)
```

```python
import functools
import math

import jax
import jax.numpy as jnp
from jax import lax
from jax.experimental import pallas as pl
from jax.experimental.pallas import tpu as pltpu

F32 = jnp.float32
BF16 = jnp.bfloat16

D_MODEL = 1024
PL_DIM = 256
DIFF_WIDTH = 512
RWKV_WIDTH = 512
DIFF_HEAD_DIM = 64
DIFF_HEADS = 4
RWKV_HEAD = 64
RWKV_HEADS = 8
D_DECAY_LORA = 64
D_AAA_LORA = 64
D_GATE_LORA = 160
ROPE_THETA = 10000.0
NORM_EPS = 1e-6
SUBLN_EPS = 1e-5
GN_EPS = 64e-5
N_GROUPS = 4
EXPERTS_PER_GROUP = 8
N_EXPERTS = 32
D_EXPERT = 512
LAM_INIT = 0.8 - 0.6 * math.exp(0.0)

V7X_LANES = 128
V7X_SUBLANES = 8
V7X_VMEM_BYTES = 64 * 1024 * 1024

IN_PROJ_ROWS = 256
PROJ_ROWS = 256
ATTN_ROWS = 256
ATTN_HEADS_PER_STEP = 4
RWKV_CHUNK = 64
RWKV_ROWS = 256
MOE_ROWS = 512
EXPERT_SUB_ROWS = 256
VMEM_LIMIT = 56 * 1024 * 1024


def _cparams(sem):
    return pltpu.CompilerParams(dimension_semantics=sem, vmem_limit_bytes=VMEM_LIMIT)


def _dot(a, b):
    return jnp.dot(a.astype(BF16), b.astype(BF16), preferred_element_type=F32)


def _dot_nt(a, b):
    return lax.dot_general(a.astype(BF16), b.astype(BF16), (((1,), (1,)), ((), ())),
                           preferred_element_type=F32)


def _dot_tn(a, b):
    return lax.dot_general(a.astype(BF16), b.astype(BF16), (((0,), (0,)), ((), ())),
                           preferred_element_type=F32)


def _split3(x):
    hi = x.astype(BF16)
    r1 = x - hi.astype(F32)
    mid = r1.astype(BF16)
    lo = (r1 - mid.astype(F32)).astype(BF16)
    return hi, mid, lo


def _dot_exact_rhs(x, m01):
    hi = x.astype(BF16)
    mid = (x - hi.astype(F32)).astype(BF16)
    return (jnp.dot(hi, m01, preferred_element_type=F32)
            + jnp.dot(mid, m01, preferred_element_type=F32))


def _dot_exact_lhs(m01, x):
    hi, mid, lo = _split3(x)
    return (jnp.dot(m01, hi, preferred_element_type=F32)
            + jnp.dot(m01, mid, preferred_element_type=F32)
            + jnp.dot(m01, lo, preferred_element_type=F32))


PACKED = jnp.uint32


def _pack16(x):
    h = x.shape[1] // 2
    lo = lax.bitcast_convert_type(x[:, :h], PACKED) >> 16
    hi = lax.bitcast_convert_type(x[:, h:], PACKED) & jnp.uint32(0xFFFF0000)
    return hi | lo


def _unpack16(w):
    lo = lax.bitcast_convert_type(w << 16, F32)
    hi = lax.bitcast_convert_type(w & jnp.uint32(0xFFFF0000), F32)
    return jnp.concatenate([lo, hi], axis=1)


def _rms(x, g, eps):
    return x * lax.rsqrt(jnp.mean(x * x, axis=-1, keepdims=True) + eps) * g


def _shift_rows(z, prev_row):
    rolled = pltpu.roll(z, 1, axis=0)
    row = lax.broadcasted_iota(jnp.int32, z.shape, 0)
    return jnp.where(row == 0, prev_row, rolled)


def _in_proj_kernel(x_ref, pos_ref, g_ref, win_ref, w1_ref, a1_ref, g1_ref, w2_ref, a2_ref, g2_ref,
                    mu_ref, qg_ref, kg_ref, invf_ref, seg_ref,
                    q_out, k_out, v_out, zr_out, zk_out, zv_out, wp_out, ap_out, gg_out,
                    carry_ref):
    tm = x_ref.shape[0]

    @pl.when(pl.program_id(1) == 0)
    def _():
        carry_ref[...] = jnp.zeros_like(carry_ref)

    x = x_ref[...]
    xb = x.astype(BF16)
    rstd = lax.rsqrt(jnp.mean(x * x, axis=-1, keepdims=True) + NORM_EPS)
    hn = x * rstd * g_ref[...]
    prev = carry_ref[V7X_SUBLANES - 1:V7X_SUBLANES, :]
    dh = _shift_rows(hn, prev) - hn
    carry_ref[...] = hn[tm - V7X_SUBLANES:tm, :]
    w = DIFF_WIDTH

    def proj(c):
        return jnp.dot(xb, win_ref[:, c * w:(c + 1) * w], preferred_element_type=F32) * rstd

    ang = pos_ref[...] * invf_ref[...]
    cos1, sin1 = jnp.cos(ang), jnp.sin(ang)
    cosf = jnp.concatenate([cos1] * (w // V7X_LANES), axis=1)
    sinf = jnp.concatenate([sin1] * (w // V7X_LANES), axis=1)
    lane = lax.broadcasted_iota(jnp.int32, (tm, w), 1)
    first_half = (lane % DIFF_HEAD_DIM) < (DIFF_HEAD_DIM // 2)
    sin_signed = jnp.where(first_half, -sinf, sinf)
    half = DIFF_HEAD_DIM // 2

    def norm_rope(z, gain):
        ss = _dot_exact_rhs(z * z, seg_ref[...])
        zn = z * lax.rsqrt(ss * (1.0 / DIFF_HEAD_DIM) + NORM_EPS) * gain
        partner = jnp.where(first_half, pltpu.roll(zn, w - half, axis=1), pltpu.roll(zn, half, axis=1))
        return zn * cosf + partner * sin_signed

    scale = DIFF_HEAD_DIM ** -0.5 * math.log2(math.e)
    q_out[...] = (norm_rope(proj(0), qg_ref[...]) * scale).astype(q_out.dtype)
    k_out[...] = norm_rope(proj(1), kg_ref[...]).astype(k_out.dtype)
    v_out[...] = proj(2).astype(v_out.dtype)
    zr_out[...] = proj(3)
    zk_out[...] = proj(4)
    zv_out[...] = proj(5)

    xw = hn + dh * mu_ref[0:1, :]
    xa = hn + dh * mu_ref[1:2, :]
    xg = hn + dh * mu_ref[2:3, :]
    wp_out[...] = _dot(jnp.tanh(_dot(xw, w1_ref[...])), w2_ref[...])
    ap_out[...] = _dot(_dot(xa, a1_ref[...]), a2_ref[...])
    gg_out[...] = _dot(jax.nn.sigmoid(_dot(xg, g1_ref[...])), g2_ref[...])


def _in_proj(x2, pos, attn_g, w_in, w1, a1, g1, w2, a2, g2, mu_wag, qg, kg, invf, seg, B, S):
    T, D = x2.shape
    tm = min(IN_PROJ_ROWS, S)
    ns = S // tm
    w = DIFF_WIDTH
    row = lambda b, i: (b * ns + i, 0)
    full = lambda b, i: (0, 0)

    def fs(a):
        return pl.BlockSpec(a.shape, full)

    outs = ([jax.ShapeDtypeStruct((T, w), BF16)] * 3 + [jax.ShapeDtypeStruct((T, w), F32)] * 6)
    return pl.pallas_call(
        _in_proj_kernel,
        grid=(B, ns),
        in_specs=[pl.BlockSpec((tm, D), row), pl.BlockSpec((tm, 1), row), fs(attn_g), fs(w_in),
                  fs(w1), fs(a1), fs(g1), fs(w2), fs(a2), fs(g2), fs(mu_wag), fs(qg), fs(kg),
                  fs(invf), fs(seg)],
        out_specs=[pl.BlockSpec((tm, w), row)] * 9,
        out_shape=outs,
        scratch_shapes=[pltpu.VMEM((V7X_SUBLANES, D), F32)],
        compiler_params=_cparams(("arbitrary", "arbitrary")),
        name="in_proj",
    )(x2, pos, attn_g, w_in, w1, a1, g1, w2, a2, g2, mu_wag, qg, kg, invf, seg)


def _attn_kernel(lq1_ref, lk1_ref, lq2_ref, lk2_ref, ogt_ref, q_ref, k_ref, v_ref, o_ref,
                 vt_ref, s_ref, acc_ref):
    tq = q_ref.shape[0]
    tk = tq
    nh, nk, hw = vt_ref.shape[0], vt_ref.shape[1], vt_ref.shape[2]
    heads = range(nh)
    i = pl.program_id(2)
    lam = (jnp.exp(jnp.sum(lq1_ref[...] * lk1_ref[...], axis=-1, keepdims=True))
           - jnp.exp(jnp.sum(lq2_ref[...] * lk2_ref[...], axis=-1, keepdims=True)) + LAM_INIT)

    @pl.when(i == 0)
    def _():
        for h in heads:
            for c in range(nk):
                vt_ref[h, c] = (v_ref[c * tk:(c + 1) * tk, h * hw:(h + 1) * hw]
                                .astype(F32).T.astype(vt_ref.dtype))

    lane = lax.broadcasted_iota(jnp.int32, (tq, hw), 1)
    qs = []
    for h in heads:
        q = q_ref[:, h * hw:(h + 1) * hw]
        zero = jnp.zeros_like(q)
        qs.append(jnp.concatenate([jnp.where(lane < DIFF_HEAD_DIM, q, zero),
                                   jnp.where(lane >= DIFF_HEAD_DIM, q, zero)], axis=0))

    def scores(j):
        start = pl.multiple_of(j * tk, tk)
        return [lax.dot_general(k_ref[pl.ds(start, tk), h * hw:(h + 1) * hw], qs[h],
                                (((1,), (1,)), ((), ())), preferred_element_type=F32) for h in heads]

    def phase1(j, m):
        st = scores(j)
        for h in heads:
            s_ref[h, j] = st[h]
        return tuple(jnp.maximum(mh, jnp.max(s, axis=0, keepdims=True)) for mh, s in zip(m, st))

    m = lax.fori_loop(0, i, phase1, tuple(jnp.full((1, 2 * tq), -jnp.inf, F32) for _ in heads))
    st = scores(i)
    krow = lax.broadcasted_iota(jnp.int32, st[0].shape, 0)
    qcol = lax.broadcasted_iota(jnp.int32, st[0].shape, 1)
    qcol = jnp.where(qcol >= tq, qcol - tq, qcol)
    st = [jnp.where(krow <= qcol, s, -jnp.inf) for s in st]
    for h in heads:
        s_ref[h, i] = st[h]
    m = tuple(jnp.maximum(mh, jnp.max(s, axis=0, keepdims=True)) for mh, s in zip(m, st))

    acc_ref[...] = jnp.zeros_like(acc_ref)

    def phase2(j, l):
        pt = [jnp.exp2(s_ref[h, j] - m[h]) for h in heads]
        pv = [jnp.dot(vt_ref[h, j], pt[h].astype(BF16), preferred_element_type=F32) for h in heads]
        for h in heads:
            acc_ref[h] += pv[h]
        return tuple(lh + jnp.sum(p, axis=0, keepdims=True) for lh, p in zip(l, pt))

    lsum = lax.fori_loop(0, i + 1, phase2, tuple(jnp.zeros((1, 2 * tq), F32) for _ in heads))

    for h in heads:
        acc = acc_ref[h]
        l = lsum[h]
        ot = acc[:, :tq] / l[:, :tq] - lam * (acc[:, tq:] / l[:, tq:])
        ot = ot * lax.rsqrt(jnp.mean(ot * ot, axis=0, keepdims=True) + SUBLN_EPS) * ogt_ref[...]
        o_ref[:, h * hw:(h + 1) * hw] = (ot * (1.0 - LAM_INIT)).T.astype(o_ref.dtype)


def _attention(q, k, v, lq1, lk1, lq2, lk2, og, B, S):
    T = q.shape[0]
    tq = min(ATTN_ROWS, S)
    nq = S // tq
    hw = 2 * DIFF_HEAD_DIM
    nh = ATTN_HEADS_PER_STEP
    gw = nh * hw
    small = lambda b, h, i: (0, 0)
    return pl.pallas_call(
        _attn_kernel,
        grid=(B, DIFF_HEADS // nh, nq),
        in_specs=[pl.BlockSpec(lq1.shape, small), pl.BlockSpec(lk1.shape, small),
                  pl.BlockSpec(lq2.shape, small), pl.BlockSpec(lk2.shape, small),
                  pl.BlockSpec(og.shape, small),
                  pl.BlockSpec((tq, gw), lambda b, h, i: (b * nq + i, h)),
                  pl.BlockSpec((S, gw), lambda b, h, i: (b, h)),
                  pl.BlockSpec((S, gw), lambda b, h, i: (b, h))],
        out_specs=pl.BlockSpec((tq, gw), lambda b, h, i: (b * nq + i, h)),
        out_shape=jax.ShapeDtypeStruct((T, DIFF_WIDTH), BF16),
        scratch_shapes=[pltpu.VMEM((nh, nq, hw, tq), BF16), pltpu.VMEM((nh, nq, tq, 2 * tq), F32),
                        pltpu.VMEM((nh, hw, 2 * tq), F32)],
        compiler_params=_cparams(("arbitrary", "arbitrary", "arbitrary")),
        name="diff_attn",
    )(lq1, lk1, lq2, lk2, og, q, k, v)


def _rwkv_kernel(zr_ref, zk_ref, zv_ref, wp_ref, ap_ref, gg_ref, mu_ref, w0_ref, a0_ref, kk_ref,
                 ka_ref, rk_ref, gnw_ref, gnb_ref, seg_ref, o_ref, state_ref, carry_ref):
    R = zr_ref.shape[0]
    C = min(RWKV_CHUNK, R)
    nch = R // C
    N = RWKV_HEAD
    W = RWKV_WIDTH
    HP = 2 * N
    n_pairs = W // HP

    @pl.when(pl.program_id(1) == 0)
    def _():
        state_ref[...] = jnp.zeros_like(state_ref)
        carry_ref[...] = jnp.zeros_like(carry_ref)

    zr, zk, zv = zr_ref[...], zk_ref[...], zv_ref[...]
    last = V7X_SUBLANES - 1
    r = zr + (_shift_rows(zr, carry_ref[last:last + 1, 0:W]) - zr) * mu_ref[0:1, :]
    k = zk + (_shift_rows(zk, carry_ref[last:last + 1, W:2 * W]) - zk) * mu_ref[1:2, :]
    v = zv + (_shift_rows(zv, carry_ref[last:last + 1, 2 * W:3 * W]) - zv) * mu_ref[2:3, :]
    carry_ref[:, 0:W] = zr[R - V7X_SUBLANES:R, :]
    carry_ref[:, W:2 * W] = zk[R - V7X_SUBLANES:R, :]
    carry_ref[:, 2 * W:3 * W] = zv[R - V7X_SUBLANES:R, :]

    lw = -math.exp(-0.5) * jax.nn.sigmoid(w0_ref[...] + wp_ref[...])
    a = jax.nn.sigmoid(a0_ref[...] + ap_ref[...])
    kk = k * kk_ref[...]
    kk = kk * lax.rsqrt(jnp.maximum(_dot_exact_rhs(kk * kk, seg_ref[...]), 1e-24))
    k2 = k * (1.0 + (a - 1.0) * ka_ref[...])
    a_s = -kk
    b_s = kk * a

    rr = lax.broadcasted_iota(jnp.int32, (R, R), 0)
    cc = lax.broadcasted_iota(jnp.int32, (R, R), 1)
    same_chunk = (rr // C) == (cc // C)
    L = _dot_exact_lhs((same_chunk & (cc <= rr)).astype(BF16), lw)
    bonus_w = r * k2 * rk_ref[...]

    P2 = 2 * C
    sr = lax.broadcasted_iota(jnp.int32, (P2, HP), 0)
    sc = lax.broadcasted_iota(jnp.int32, (P2, HP), 1)
    stack_mask = (sr < C) == (sc < N)
    br = lax.broadcasted_iota(jnp.int32, (P2, P2), 0)
    bc = lax.broadcasted_iota(jnp.int32, (P2, P2), 1)
    same_head = (br < C) == (bc < C)
    tr = jnp.where(br >= C, br - C, br)
    tc = jnp.where(bc >= C, bc - C, bc)
    strict = same_head & (tc < tr)
    lower = same_head & (tc <= tr)
    eye_p = (br == bc).astype(F32)
    kr = lax.broadcasted_iota(jnp.int32, (HP, HP), 0)
    kc = lax.broadcasted_iota(jnp.int32, (HP, HP), 1)
    eye_k = kr == kc

    def dup(x):
        return jnp.concatenate([x, x], axis=0)

    def stack(x):
        return jnp.where(stack_mask, dup(x), 0.0)

    n_sq = int(math.log2(C)) - 1
    items = [(c, p) for c in range(nch) for p in range(n_pairs)]

    def prep(c, p):
        rows = slice(c * C, (c + 1) * C)
        lanes = slice(p * HP, (p + 1) * HP)
        Lc = L[rows, lanes]
        lwc = lw[rows, lanes]
        LC = Lc[C - 1:C, :]
        enL = jnp.exp(-Lc)
        eCL = jnp.exp(LC - Lc)
        b_c, k_c = b_s[rows, lanes], k2[rows, lanes]
        return dict(
            xa=stack(a_s[rows, lanes] * jnp.exp(Lc - lwc)), xr=stack(r[rows, lanes] * jnp.exp(Lc)),
            bt=dup(b_c * enL), kt=dup(k_c * enL), bh=stack(b_c * eCL), kh=stack(k_c * eCL),
            vs=stack(v[rows, lanes]), pc=jnp.exp(LC))

    d = [prep(c, p) for c, p in items]
    aa = [_dot_nt(jnp.concatenate([e["xa"], e["xr"]], axis=0),
                  jnp.concatenate([e["bt"], e["kt"]], axis=0)) for e in d]
    a_ab = [jnp.where(strict, m[:P2, :P2], 0.0) for m in aa]
    a_ak = [jnp.where(strict, m[:P2, P2:], 0.0) for m in aa]
    a_rb = [jnp.where(lower, m[P2:, :P2], 0.0) for m in aa]
    a_rk = [jnp.where(lower, m[P2:, P2:], 0.0) for m in aa]
    tm = [eye_p + m for m in a_ab]
    npw = a_ab
    for _ in range(n_sq):
        npw = [_dot(m, m) for m in npw]
        tm = [t + _dot(t, m) for t, m in zip(tm, npw)]
    av = [_dot(m, e["vs"]) for m, e in zip(a_ak, d)]
    z = [_dot(t, jnp.concatenate([e["xa"], x], axis=1)) for t, e, x in zip(tm, d, av)]
    w = [_dot(m, x) for m, x in zip(a_rb, z)]
    rkv = [_dot(m, e["vs"]) for m, e in zip(a_rk, d)]
    gh = [_dot_tn(e["bh"], x) for e, x in zip(d, z)]
    khv = [_dot_tn(e["kh"], e["vs"]) for e in d]

    for idx, (c, p) in enumerate(items):
        e = d[idx]
        rows = slice(c * C, (c + 1) * C)
        lanes = slice(p * HP, (p + 1) * HP)
        rp = e["xr"] + w[idx][:, :HP]
        y0 = w[idx][:, HP:] + rkv[idx]
        gm = gh[idx][:, :HP]
        hm = gh[idx][:, HP:] + khv[idx]
        st = state_ref[p]
        yg = _dot(jnp.concatenate([rp, gm], axis=0), st)
        pc_col = jnp.sum(jnp.where(eye_k, e["pc"], 0.0), axis=1, keepdims=True)
        state_ref[p] = pc_col * st + yg[P2:] + hm
        ys = yg[:P2] + y0
        mu = jnp.sum(ys, axis=-1, keepdims=True) * (1.0 / N)
        yc = jnp.where(stack_mask, ys - mu, 0.0)
        var = jnp.sum(yc * yc, axis=-1, keepdims=True) * (1.0 / N)
        yn = yc * lax.rsqrt(var + GN_EPS)
        bonus = jnp.sum(stack(bonus_w[rows, lanes]), axis=-1, keepdims=True) * e["vs"]
        yn = yn[:C] + yn[C:]
        bonus = bonus[:C] + bonus[C:]
        o_ref[rows, lanes] = ((yn * gnw_ref[:, lanes] + gnb_ref[:, lanes] + bonus)
                              * gg_ref[rows, lanes]).astype(o_ref.dtype)


def _rwkv(zr, zk, zv, wp, ap, gg, mu_rkv, w0, a0, k_k, k_a, r_k, gn_w, gn_b, seg, B, S):
    T = zr.shape[0]
    C = min(RWKV_ROWS, S)
    nc = S // C
    W = RWKV_WIDTH
    HP = 2 * RWKV_HEAD
    row = lambda b, i: (b * nc + i, 0)
    full = lambda b, i: (0, 0)

    def fs(a):
        return pl.BlockSpec(a.shape, full)

    return pl.pallas_call(
        _rwkv_kernel,
        grid=(B, nc),
        in_specs=[pl.BlockSpec((C, W), row)] * 6 + [fs(mu_rkv), fs(w0), fs(a0), fs(k_k), fs(k_a),
                                                    fs(r_k), fs(gn_w), fs(gn_b), fs(seg)],
        out_specs=pl.BlockSpec((C, W), row),
        out_shape=jax.ShapeDtypeStruct((T, W), BF16),
        scratch_shapes=[pltpu.VMEM((W // HP, HP, HP), F32),
                        pltpu.VMEM((V7X_SUBLANES, 3 * W), F32)],
        compiler_params=_cparams(("arbitrary", "arbitrary")),
        name="rwkv7",
    )(zr, zk, zv, wp, ap, gg, mu_rkv, w0, a0, k_k, k_a, r_k, gn_w, gn_b, seg)


ROUTE_COLS = 8
ROUTER_GROUP_LANE = 0
ROUTER_EXPERT_LANE = N_GROUPS


def _out_router_kernel(x_ref, od_ref, orw_ref, wo_ref, mg_ref, wrh_ref, wrl_ref,
                       x1_out, hm_out, route_out, cnt_out):
    tm = x_ref.shape[0]
    x1 =(x_ref[...] + jnp.dot(od_ref[...], wo_ref[0:DIFF_WIDTH, :], preferred_element_type=F32)
          + jnp.dot(orw_ref[...], wo_ref[DIFF_WIDTH:, :], preferred_element_type=F32))
    x1_out[...] = x1
    hm = _rms(x1, mg_ref[...], NORM_EPS)
    hm_out[...] = hm.astype(hm_out.dtype)

    hi = hm.astype(BF16)
    lo = (hm - hi.astype(F32)).astype(BF16)
    lg = (jnp.dot(hi, wrh_ref[...], preferred_element_type=F32)
          + jnp.dot(hi, wrl_ref[...], preferred_element_type=F32)
          + jnp.dot(lo, wrh_ref[...], preferred_element_type=F32))

    lane = lax.broadcasted_iota(jnp.int32, lg.shape, 1).astype(F32)
    big = float(V7X_LANES)
    ninf = -jnp.inf
    gmask = lane < N_GROUPS
    gmax = jnp.max(jnp.where(gmask, lg, ninf), axis=-1, keepdims=True)
    g_sel = jnp.min(jnp.where(gmask & (lg == gmax), lane, big), axis=-1, keepdims=True)
    g_w = 1.0 / jnp.sum(jnp.where(gmask, jnp.exp(lg - gmax), 0.0), axis=-1, keepdims=True)
    lo_lane = ROUTER_EXPERT_LANE + g_sel * EXPERTS_PER_GROUP
    emask = (lane >= lo_lane) & (lane < lo_lane + EXPERTS_PER_GROUP)
    v1 = jnp.max(jnp.where(emask, lg, ninf), axis=-1, keepdims=True)
    i1 = jnp.min(jnp.where(emask & (lg == v1), lane, big), axis=-1, keepdims=True)
    emask2 = emask & (lane != i1)
    v2 = jnp.max(jnp.where(emask2, lg, ninf), axis=-1, keepdims=True)
    i2 = jnp.min(jnp.where(emask2 & (lg == v2), lane, big), axis=-1, keepdims=True)
    e2x = jnp.exp(v2 - v1)
    den = 1.0 + e2x
    wt1 = (1.0 / den) * g_w
    wt2 = (e2x / den) * g_w

    oh1 = lane == i1
    oh2 = lane == i2
    oh = (oh1 | oh2).astype(F32)
    ri = lax.broadcasted_iota(jnp.int32, (tm, tm), 0)
    ci = lax.broadcasted_iota(jnp.int32, (tm, tm), 1)
    before = (ci < ri).astype(BF16)
    prefix = jnp.dot(before, oh.astype(BF16), preferred_element_type=F32)
    rank1 = jnp.sum(jnp.where(oh1, prefix, 0.0), axis=-1, keepdims=True)
    rank2 = jnp.sum(jnp.where(oh2, prefix, 0.0), axis=-1, keepdims=True)
    cnt_out[...] = jnp.broadcast_to(jnp.sum(oh, axis=0, keepdims=True), cnt_out.shape)

    e1 = i1 - ROUTER_EXPERT_LANE
    e2 = i2 - ROUTER_EXPERT_LANE
    col = lax.broadcasted_iota(jnp.int32, (tm, ROUTE_COLS), 1)
    route = jnp.where(col == 0, e1, jnp.where(col == 1, e2, jnp.where(col == 2, wt1, jnp.where(
        col == 3, wt2, jnp.where(col == 4, rank1, jnp.where(col == 5, rank2, 0.0))))))
    route_out[...] = route


def _out_router(x2, od, orw, w_out, moe_g, wr_hi, wr_lo):
    T, D = x2.shape
    tm = min(PROJ_ROWS, T)
    row = lambda i: (i, 0)
    full = lambda i: (0, 0)

    def fs(a):
        return pl.BlockSpec(a.shape, full)

    return pl.pallas_call(
        _out_router_kernel,
        grid=(T // tm,),
        in_specs=[pl.BlockSpec((tm, D), row), pl.BlockSpec((tm, DIFF_WIDTH), row),
                  pl.BlockSpec((tm, RWKV_WIDTH), row), fs(w_out), fs(moe_g), fs(wr_hi), fs(wr_lo)],
        out_specs=[pl.BlockSpec((tm, D), row), pl.BlockSpec((tm, D), row),
                   pl.BlockSpec((tm, ROUTE_COLS), row), pl.BlockSpec((V7X_SUBLANES, V7X_LANES), row)],
        out_shape=[jax.ShapeDtypeStruct((T, D), F32), jax.ShapeDtypeStruct((T, D), BF16),
                   jax.ShapeDtypeStruct((T, ROUTE_COLS), F32),
                   jax.ShapeDtypeStruct((T // tm * V7X_SUBLANES, V7X_LANES), F32)],
        compiler_params=_cparams(("arbitrary",)),
        name="out_router",
    )(x2, od, orw, w_out, moe_g, wr_hi, wr_lo)


RUN_ALIGN = V7X_SUBLANES
TAB_DST, TAB_SRC, TAB_LEN, TAB_NUSED = 0, N_EXPERTS, 2 * N_EXPERTS, 3 * N_EXPERTS


def _stage_rows(tm):
    need = 2 * tm + N_EXPERTS * (RUN_ALIGN - 1)
    return -(-need // V7X_LANES) * V7X_LANES


def _run_copies(tab_ref, max_len, make_copy, op):
    n_bits = int(math.log2(max_len // RUN_ALIGN)) + 1

    def per_expert(e, c):
        n = tab_ref[0, 0, TAB_LEN + e]
        dst = tab_ref[0, 0, TAB_DST + e]
        src = tab_ref[0, 0, TAB_SRC + e]
        off = jnp.int32(0)
        for bit in reversed(range(n_bits)):
            size = RUN_ALIGN << bit
            hit = (n & size) != 0

            @pl.when(hit)
            def _(off=off, size=size):
                cp = make_copy(pl.multiple_of(src + off, RUN_ALIGN), pl.multiple_of(dst + off, RUN_ALIGN), size)
                getattr(cp, op)()

            off = off + jnp.where(hit, size, 0)
        return c

    lax.fori_loop(0, N_EXPERTS, per_expert, 0)


def _dispatch_kernel(tab_ref, tail_ref, sl_ref, hm_ref, xb_ref, stage_ref, zero_ref, sem, zsem):
    tm = hm_ref.shape[0]
    ns = stage_ref.shape[0]
    bm = zero_ref.shape[0]
    nb = xb_ref.shape[0] // bm
    n_used = tail_ref[0, 0, TAB_NUSED]

    def make_copy(src, dst, size):
        return pltpu.make_async_copy(stage_ref.at[pl.ds(src, size)], xb_ref.at[pl.ds(dst, size)], sem)

    def zero_copy(src, dst, size):
        return pltpu.make_async_copy(zero_ref.at[pl.ds(src, size)], xb_ref.at[pl.ds(dst, size)], zsem)

    def zero_blocks(op):
        def one(b, c):
            getattr(zero_copy(0, pl.multiple_of(b * bm, bm), bm), op)()
            return c
        lax.fori_loop(n_used, nb, one, 0)

    @pl.when(pl.program_id(0) == 0)
    def _():
        zero_ref[...] = jnp.zeros_like(zero_ref)
        _run_copies(tail_ref, bm, zero_copy, "start")
        zero_blocks("start")

    srow = lax.broadcasted_iota(jnp.int32, (ns, tm), 0)
    sel = (srow == sl_ref[0, 0:1, :]) | (srow == sl_ref[0, 1:2, :])
    stage_ref[...] = _pack16(jnp.dot(sel.astype(BF16), hm_ref[...], preferred_element_type=F32))
    _run_copies(tab_ref, tm, make_copy, "start")
    _run_copies(tab_ref, tm, make_copy, "wait")

    @pl.when(pl.program_id(0) == pl.num_programs(0) - 1)
    def _():
        _run_copies(tail_ref, bm, zero_copy, "wait")
        zero_blocks("wait")


def _dispatch(tab, tail, sl_rows, hm, P):
    T, D = hm.shape
    tm = min(PROJ_ROWS, T)
    return pl.pallas_call(
        _dispatch_kernel,
        grid=(T // tm,),
        in_specs=[pl.BlockSpec((1, 1, tab.shape[-1]), lambda i: (i, 0, 0), memory_space=pltpu.SMEM),
                  pl.BlockSpec((1, 1, tail.shape[-1]), lambda i: (0, 0, 0), memory_space=pltpu.SMEM),
                  pl.BlockSpec((1, 2, tm), lambda i: (i, 0, 0)),
                  pl.BlockSpec((tm, D), lambda i: (i, 0))],
        out_specs=pl.BlockSpec(memory_space=pl.ANY),
        out_shape=jax.ShapeDtypeStruct((P, D // 2), PACKED),
        scratch_shapes=[pltpu.VMEM((_stage_rows(tm), D // 2), PACKED), pltpu.VMEM((MOE_ROWS, D // 2), PACKED),
                        pltpu.SemaphoreType.DMA(()), pltpu.SemaphoreType.DMA(())],
        compiler_params=_cparams(("arbitrary",)),
        name="moe_dispatch",
    )(tab, tail, sl_rows, hm)


def _expert_kernel(be_ref, nu_ref, bv_ref, xb_ref, wg_ref, wu_ref, wd_ref, yb_ref, wgb, wub, wdb):
    i = pl.program_id(0)
    prev = be_ref[jnp.maximum(i - 1, 0)]
    changed = (i == 0) | (be_ref[i] != prev)

    @pl.when(changed)
    def _():
        wgb[...] = wg_ref[0].astype(BF16)
        wub[...] = wu_ref[0].astype(BF16)
        wdb[...] = wd_ref[0].astype(BF16)

    valid = jnp.where(i < nu_ref[0], bv_ref[i], 0)
    for s in range(xb_ref.shape[0] // EXPERT_SUB_ROWS):
        rows = slice(s * EXPERT_SUB_ROWS, (s + 1) * EXPERT_SUB_ROWS)

        @pl.when(s * EXPERT_SUB_ROWS < valid)
        def _(rows=rows):
            xb = _unpack16(xb_ref[rows, :]).astype(BF16)
            gate = jnp.dot(xb, wgb[...], preferred_element_type=F32)
            up = jnp.dot(xb, wub[...], preferred_element_type=F32)
            hdn = (gate * jax.nn.sigmoid(gate)) * up
            y = jnp.dot(hdn.astype(BF16), wdb[...], preferred_element_type=F32)
            yb_ref[rows, :] = _pack16(y.astype(BF16).astype(F32))

        @pl.when(s * EXPERT_SUB_ROWS >= valid)
        def _(rows=rows):
            yb_ref[rows, :] = jnp.zeros((EXPERT_SUB_ROWS, yb_ref.shape[1]), yb_ref.dtype)


def _experts(block_e, n_used, block_valid, xb, w_gate, w_up, w_down):
    P, DP = xb.shape
    D = 2 * DP
    bm = MOE_ROWS
    nb = P // bm
    E = D_EXPERT
    grid_spec = pltpu.PrefetchScalarGridSpec(
        num_scalar_prefetch=3,
        grid=(nb,),
        in_specs=[pl.BlockSpec((bm, DP), lambda i, be, nu, bv: (jnp.minimum(i, nu[0] - 1), 0)),
                  pl.BlockSpec((1, D, E), lambda i, be, nu, bv: (be[i], 0, 0)),
                  pl.BlockSpec((1, D, E), lambda i, be, nu, bv: (be[i], 0, 0)),
                  pl.BlockSpec((1, E, D), lambda i, be, nu, bv: (be[i], 0, 0))],
        out_specs=pl.BlockSpec((bm, DP), lambda i, be, nu, bv: (jnp.where(i < nu[0], i, nb - 1), 0)),
        scratch_shapes=[pltpu.VMEM((D, E), BF16), pltpu.VMEM((D, E), BF16), pltpu.VMEM((E, D), BF16)],
    )
    return pl.pallas_call(
        _expert_kernel,
        grid_spec=grid_spec,
        out_shape=jax.ShapeDtypeStruct((P, DP), PACKED),
        input_output_aliases={3: 0},
        compiler_params=_cparams(("arbitrary",)),
        name="moe_experts",
    )(block_e, n_used, block_valid, xb, w_gate, w_up, w_down)


def _final_kernel(tab_ref, x1_ref, route_ref, p_ref, wpl_ref, plg_ref, pgg_ref, wgate_ref, yb_ref,
                  o_ref, stage_ref, sem):
    tm = x1_ref.shape[0]
    ns = stage_ref.shape[0]
    stage_ref[...] = jnp.zeros_like(stage_ref)

    def make_copy(src, dst, size):
        return pltpu.make_async_copy(yb_ref.at[pl.ds(dst, size)], stage_ref.at[pl.ds(src, size)], sem)

    _run_copies(tab_ref, tm, make_copy, "start")
    pe = _rms(jnp.dot(p_ref[...].astype(BF16), wpl_ref[...], preferred_element_type=F32),
              plg_ref[...], NORM_EPS)
    _run_copies(tab_ref, tm, make_copy, "wait")

    route = route_ref[...]
    scol = lax.broadcasted_iota(jnp.int32, (tm, ns), 1).astype(F32)
    wsel = (jnp.where(scol == route[:, 0:1], route[:, 2:3], 0.0)
            + jnp.where(scol == route[:, 1:2], route[:, 3:4], 0.0))
    w_hi = wsel.astype(BF16)
    w_lo = (wsel - w_hi.astype(F32)).astype(BF16)
    yb16 = _unpack16(stage_ref[...]).astype(BF16)
    moe =(jnp.dot(w_hi, yb16, preferred_element_type=F32)
           + jnp.dot(w_lo, yb16, preferred_element_type=F32))
    x2 = x1_ref[...] + moe
    gate = jax.nn.sigmoid(_dot(_rms(x2, pgg_ref[...], NORM_EPS), wgate_ref[...]))
    o_ref[...] = x2 + pe * gate


def _final(tab, x1, route, p2, w_pl, pl_g, pl_gate_g, w_gate, yb):
    T, D = x1.shape
    tm = min(PROJ_ROWS, T)
    row = lambda i: (i, 0)
    full = lambda i: (0, 0)

    def fs(a):
        return pl.BlockSpec(a.shape, full)

    return pl.pallas_call(
        _final_kernel,
        grid=(T // tm,),
        in_specs=[pl.BlockSpec((1, 1, tab.shape[-1]), lambda i: (i, 0, 0), memory_space=pltpu.SMEM),
                  pl.BlockSpec((tm, D), row), pl.BlockSpec((tm, ROUTE_COLS), row),
                  pl.BlockSpec((tm, PL_DIM), row), fs(w_pl), fs(pl_g), fs(pl_gate_g), fs(w_gate),
                  pl.BlockSpec(memory_space=pl.ANY)],
        out_specs=pl.BlockSpec((tm, D), row),
        out_shape=jax.ShapeDtypeStruct((T, D), F32),
        scratch_shapes=[pltpu.VMEM((_stage_rows(tm), D // 2), PACKED), pltpu.SemaphoreType.DMA(())],
        compiler_params=_cparams(("arbitrary",)),
        name="combine_final",
    )(tab, x1, route, p2, w_pl, pl_g, pl_gate_g, w_gate, yb)


def kernel(x, p, positions, attn_norm_g, w_in, q_norm_g, k_norm_g, lambda_q1, lambda_k1, lambda_q2, lambda_k2, diff_out_g, mu_rkv, mu_wag, w0, w_lora_a, w_lora_b, a0, a_lora_a, a_lora_b, g_lora_a, g_lora_b, k_k, k_a, r_k, gn_w, gn_b, w_out, moe_norm_g, w_group, w_expert_router, w_gate, w_up, w_down, w_pl, pl_norm_g, pl_gate_norm_g, w_pl_gate):
    B, S, D = x.shape
    T = B * S
    assert p.shape[0] == 1, "one layer"
    x2 = x.reshape(T, D)
    p2 = p[0].reshape(T, PL_DIM)
    pos = positions.astype(F32).reshape(T, 1)

    def row(a):
        return a.reshape(1, -1).astype(F32)

    half = DIFF_HEAD_DIM // 2
    inv_freq = ROPE_THETA ** (-jnp.arange(half, dtype=F32) / half)
    invf = jnp.tile(inv_freq, V7X_LANES // half).reshape(1, V7X_LANES)
    li = jnp.arange(DIFF_WIDTH) // DIFF_HEAD_DIM
    seg = (li[:, None] == li[None, :]).astype(BF16)
    reps = DIFF_WIDTH // DIFF_HEAD_DIM

    q, k, v, zr, zk, zv, wp, ap, gg = _in_proj(
        x2, pos, row(attn_norm_g[0]), (attn_norm_g[0].astype(F32)[:, None] * w_in[0]).astype(BF16),
        w_lora_a[0].astype(BF16), a_lora_a[0].astype(BF16), g_lora_a[0].astype(BF16),
        w_lora_b[0].astype(BF16), a_lora_b[0].astype(BF16), g_lora_b[0].astype(BF16),
        mu_wag[0].astype(F32), row(jnp.tile(q_norm_g[0], reps)), row(jnp.tile(k_norm_g[0], reps)),
        invf, seg, B, S)

    o_diff = _attention(q, k, v, row(lambda_q1[0]), row(lambda_k1[0]), row(lambda_q2[0]),
                        row(lambda_k2[0]), diff_out_g[0].astype(F32).reshape(-1, 1), B, S)
    o_rwkv = _rwkv(zr, zk, zv, wp, ap, gg, mu_rkv[0].astype(F32), row(w0[0]), row(a0[0]),
                   row(k_k[0]), row(k_a[0]), row(r_k[0]), row(gn_w[0]), row(gn_b[0]), seg, B, S)

    wr = jnp.concatenate([w_group[0], jnp.transpose(w_expert_router[0], (1, 0, 2)).reshape(D, N_EXPERTS)],
                         axis=1).astype(F32)
    wr = jnp.pad(wr, ((0, 0), (0, V7X_LANES - wr.shape[1])))
    wr_hi = wr.astype(BF16)
    wr_lo = (wr - wr_hi.astype(F32)).astype(BF16)
    x1, hm, route, cnt = _out_router(x2, o_diff, o_rwkv, w_out[0].astype(BF16), row(moe_norm_g[0]),
                                     wr_hi, wr_lo)

    bm = MOE_ROWS
    tm = min(PROJ_ROWS, T)
    nt = T // tm
    i32 = jnp.int32
    cnt_te = cnt.reshape(nt, V7X_SUBLANES, V7X_LANES)[:, 0, ROUTER_EXPERT_LANE:ROUTER_EXPERT_LANE + N_EXPERTS]
    len_te = (cnt_te.astype(i32) + RUN_ALIGN - 1) // RUN_ALIGN * RUN_ALIGN
    tot_e = jnp.sum(len_te, axis=0)
    pcounts = (tot_e + bm - 1) // bm * bm
    pends = jnp.cumsum(pcounts)
    pstarts = pends - pcounts
    dst_te = pstarts[None, :] + jnp.cumsum(len_te, axis=0) - len_te
    src_te = jnp.cumsum(len_te, axis=1) - len_te
    tab = jnp.concatenate([dst_te, src_te, len_te, jnp.zeros_like(len_te)], axis=1).reshape(nt, 1, 4 * N_EXPERTS)
    nb = -(-(2 * T + nt * N_EXPERTS * (RUN_ALIGN - 1)) // bm) + N_EXPERTS
    P = nb * bm
    n_used = (pends[-1] // bm).astype(i32).reshape(1)
    block_start = jnp.arange(nb, dtype=i32) * bm
    block_e = jnp.minimum(jnp.sum((pends[None, :] <= block_start[:, None]).astype(i32), axis=1),
                          N_EXPERTS - 1)
    eid = route[:, 0:2].astype(i32).reshape(nt, tm, 2)
    run_src = jnp.sum(jnp.where(eid[..., None] == jnp.arange(N_EXPERTS, dtype=i32), src_te[:, None, None, :], 0),
                      axis=-1)
    sl = run_src + route[:, 4:6].astype(i32).reshape(nt, tm, 2)
    sl_rows = jnp.transpose(sl, (0, 2, 1))
    route2 = jnp.concatenate([sl.reshape(T, 2).astype(F32), route[:, 2:4],
                              jnp.zeros((T, ROUTE_COLS - 4), F32)], axis=1)

    zero_e = jnp.zeros_like(tot_e)
    tail = jnp.concatenate([pstarts + tot_e, zero_e, pcounts - tot_e,
                            jnp.broadcast_to(n_used, (N_EXPERTS,))]).reshape(1, 1, 4 * N_EXPERTS)
    xb = _dispatch(tab, tail, sl_rows, hm, P)
    run_end = (pstarts + tot_e)[block_e]
    block_valid = jnp.clip(run_end - block_start, 0, bm).astype(i32)
    yb = _experts(block_e, n_used, block_valid, xb, w_gate[0], w_up[0], w_down[0])
    out = _final(tab, x1, route2, p2, w_pl[0].astype(BF16), row(pl_norm_g[0]), row(pl_gate_norm_g[0]),
                 w_pl_gate[0].astype(BF16), yb)
    return out.reshape(B, S, D)
```

```python
import functools
import math

import jax
import jax.numpy as jnp
from jax import lax
from jax.experimental import pallas as pl
from jax.experimental.pallas import tpu as pltpu

F32 = jnp.float32
BF16 = jnp.bfloat16

D_MODEL = 1024
PL_DIM = 256
DIFF_WIDTH = 512
RWKV_WIDTH = 512
DIFF_HEAD_DIM = 64
DIFF_HEADS = 4
RWKV_HEAD = 64
RWKV_HEADS = 8
D_DECAY_LORA = 64
D_AAA_LORA = 64
D_GATE_LORA = 160
ROPE_THETA = 10000.0
NORM_EPS = 1e-6
SUBLN_EPS = 1e-5
GN_EPS = 64e-5
N_GROUPS = 4
EXPERTS_PER_GROUP = 8
N_EXPERTS = 32
D_EXPERT = 512
LAM_INIT = 0.8 - 0.6 * math.exp(0.0)

V7X_LANES = 128
V7X_SUBLANES = 8
V7X_VMEM_BYTES = 64 * 1024 * 1024

IN_PROJ_ROWS = 256
PROJ_ROWS = 256
ATTN_ROWS = 256
ATTN_UNROLL = 4
ATTN_HEADS_PER_STEP = 2
RWKV_CHUNK = 64
RWKV_ROWS = 256
MOE_ROWS = 512
VMEM_LIMIT = 56 * 1024 * 1024


def _cparams(sem):
    return pltpu.CompilerParams(dimension_semantics=sem, vmem_limit_bytes=VMEM_LIMIT)


def _dot(a, b):
    return jnp.dot(a.astype(BF16), b.astype(BF16), preferred_element_type=F32)


def _dot_nt(a, b):
    return lax.dot_general(a.astype(BF16), b.astype(BF16), (((1,), (1,)), ((), ())),
                           preferred_element_type=F32)


def _dot_tn(a, b):
    return lax.dot_general(a.astype(BF16), b.astype(BF16), (((0,), (0,)), ((), ())),
                           preferred_element_type=F32)


def _split3(x):
    hi = x.astype(BF16)
    r1 = x - hi.astype(F32)
    mid = r1.astype(BF16)
    lo = (r1 - mid.astype(F32)).astype(BF16)
    return hi, mid, lo


def _dot_exact_rhs(x, m01):
    hi = x.astype(BF16)
    mid = (x - hi.astype(F32)).astype(BF16)
    return (jnp.dot(hi, m01, preferred_element_type=F32)
            + jnp.dot(mid, m01, preferred_element_type=F32))


def _dot_exact_lhs(m01, x):
    hi, mid, lo = _split3(x)
    return (jnp.dot(m01, hi, preferred_element_type=F32)
            + jnp.dot(m01, mid, preferred_element_type=F32)
            + jnp.dot(m01, lo, preferred_element_type=F32))


PACKED = jnp.uint32


def _pack16(x):
    h = x.shape[1] // 2
    lo = lax.bitcast_convert_type(x[:, :h], PACKED) >> 16
    hi = lax.bitcast_convert_type(x[:, h:], PACKED) & jnp.uint32(0xFFFF0000)
    return hi | lo


def _unpack16(w):
    lo = lax.bitcast_convert_type(w << 16, F32)
    hi = lax.bitcast_convert_type(w & jnp.uint32(0xFFFF0000), F32)
    return jnp.concatenate([lo, hi], axis=1)


def _rms(x, g, eps):
    return x * lax.rsqrt(jnp.mean(x * x, axis=-1, keepdims=True) + eps) * g


def _shift_rows(z, prev_row):
    rolled = pltpu.roll(z, 1, axis=0)
    row = lax.broadcasted_iota(jnp.int32, z.shape, 0)
    return jnp.where(row == 0, prev_row, rolled)


def _in_proj_kernel(x_ref, pos_ref, g_ref, win_ref, w1_ref, a1_ref, g1_ref, w2_ref, a2_ref, g2_ref,
                    mu_ref, qg_ref, kg_ref, invf_ref, seg_ref,
                    q_out, k_out, v_out, zr_out, zk_out, zv_out, wp_out, ap_out, gg_out,
                    carry_ref):
    tm = x_ref.shape[0]

    @pl.when(pl.program_id(1) == 0)
    def _():
        carry_ref[...] = jnp.zeros_like(carry_ref)

    hn = _rms(x_ref[...], g_ref[...], NORM_EPS)
    prev = carry_ref[V7X_SUBLANES - 1:V7X_SUBLANES, :]
    dh = _shift_rows(hn, prev) - hn
    carry_ref[...] = hn[tm - V7X_SUBLANES:tm, :]

    hb = hn.astype(BF16)
    w = DIFF_WIDTH

    def proj(c):
        return jnp.dot(hb, win_ref[:, c * w:(c + 1) * w], preferred_element_type=F32)

    ang = pos_ref[...] * invf_ref[...]
    cos1, sin1 = jnp.cos(ang), jnp.sin(ang)
    cosf = jnp.concatenate([cos1] * (w // V7X_LANES), axis=1)
    sinf = jnp.concatenate([sin1] * (w // V7X_LANES), axis=1)
    lane = lax.broadcasted_iota(jnp.int32, (tm, w), 1)
    first_half = (lane % DIFF_HEAD_DIM) < (DIFF_HEAD_DIM // 2)
    sin_signed = jnp.where(first_half, -sinf, sinf)
    half = DIFF_HEAD_DIM // 2

    def norm_rope(z, gain):
        ss = _dot_exact_rhs(z * z, seg_ref[...])
        zn = z * lax.rsqrt(ss * (1.0 / DIFF_HEAD_DIM) + NORM_EPS) * gain
        partner = jnp.where(first_half, pltpu.roll(zn, w - half, axis=1), pltpu.roll(zn, half, axis=1))
        return zn * cosf + partner * sin_signed

    scale = DIFF_HEAD_DIM ** -0.5 * math.log2(math.e)
    q_out[...] = (norm_rope(proj(0), qg_ref[...]) * scale).astype(q_out.dtype)
    k_out[...] = norm_rope(proj(1), kg_ref[...]).astype(k_out.dtype)
    v_out[...] = proj(2).astype(v_out.dtype)
    zr_out[...] = proj(3)
    zk_out[...] = proj(4)
    zv_out[...] = proj(5)

    xw = hn + dh * mu_ref[0:1, :]
    xa = hn + dh * mu_ref[1:2, :]
    xg = hn + dh * mu_ref[2:3, :]
    wp_out[...] = _dot(jnp.tanh(_dot(xw, w1_ref[...])), w2_ref[...])
    ap_out[...] = _dot(_dot(xa, a1_ref[...]), a2_ref[...])
    gg_out[...] = _dot(jax.nn.sigmoid(_dot(xg, g1_ref[...])), g2_ref[...])


def _in_proj(x2, pos, attn_g, w_in, w1, a1, g1, w2, a2, g2, mu_wag, qg, kg, invf, seg, B, S):
    T, D = x2.shape
    tm = min(IN_PROJ_ROWS, S)
    ns = S // tm
    w = DIFF_WIDTH
    row = lambda b, i: (b * ns + i, 0)
    full = lambda b, i: (0, 0)

    def fs(a):
        return pl.BlockSpec(a.shape, full)

    outs = ([jax.ShapeDtypeStruct((T, w), BF16)] * 3 + [jax.ShapeDtypeStruct((T, w), F32)] * 6)
    return pl.pallas_call(
        _in_proj_kernel,
        grid=(B, ns),
        in_specs=[pl.BlockSpec((tm, D), row), pl.BlockSpec((tm, 1), row), fs(attn_g), fs(w_in),
                  fs(w1), fs(a1), fs(g1), fs(w2), fs(a2), fs(g2), fs(mu_wag), fs(qg), fs(kg),
                  fs(invf), fs(seg)],
        out_specs=[pl.BlockSpec((tm, w), row)] * 9,
        out_shape=outs,
        scratch_shapes=[pltpu.VMEM((V7X_SUBLANES, D), F32)],
        compiler_params=_cparams(("arbitrary", "arbitrary")),
        name="in_proj",
    )(x2, pos, attn_g, w_in, w1, a1, g1, w2, a2, g2, mu_wag, qg, kg, invf, seg)


def _attn_kernel(lq1_ref, lk1_ref, lq2_ref, lk2_ref, ogt_ref, q_ref, k_ref, v_ref, o_ref,
                 vt_ref, s_ref, acc_ref):
    nh, nq, hw, tq = vt_ref.shape
    tk = tq
    heads = range(nh)
    lam = (jnp.exp(jnp.sum(lq1_ref[...] * lk1_ref[...], axis=-1, keepdims=True))
           - jnp.exp(jnp.sum(lq2_ref[...] * lk2_ref[...], axis=-1, keepdims=True)) + LAM_INIT)

    for h in heads:
        for c in range(nq):
            vt_ref[h, c] = (v_ref[c * tk:(c + 1) * tk, h * hw:(h + 1) * hw]
                            .astype(F32).T.astype(vt_ref.dtype))

    lane = lax.broadcasted_iota(jnp.int32, (tq, hw), 1)

    def stacked_queries(t):
        out = []
        for h in heads:
            q = q_ref[t * tq:(t + 1) * tq, h * hw:(h + 1) * hw]
            zero = jnp.zeros_like(q)
            out.append(jnp.concatenate([jnp.where(lane < DIFF_HEAD_DIM, q, zero),
                                        jnp.where(lane >= DIFF_HEAD_DIM, q, zero)], axis=0))
        return out

    def scores(j, qs):
        start = pl.multiple_of(j * tk, tk)
        return [lax.dot_general(k_ref[pl.ds(start, tk), h * hw:(h + 1) * hw], qs[h],
                                (((1,), (1,)), ((), ())), preferred_element_type=F32) for h in heads]

    def colmax(m, st):
        return tuple(jnp.maximum(mh, jnp.max(s, axis=0, keepdims=True)) for mh, s in zip(m, st))

    krow = lax.broadcasted_iota(jnp.int32, (tk, 2 * tq), 0)
    qcol = lax.broadcasted_iota(jnp.int32, (tk, 2 * tq), 1)
    causal = krow <= jnp.where(qcol >= tq, qcol - tq, qcol)
    ninf = tuple(jnp.full((1, 2 * tq), -jnp.inf, F32) for _ in heads)
    zeros = tuple(jnp.zeros((1, 2 * tq), F32) for _ in heads)

    def diagonal(t, qs, m):
        st = [jnp.where(causal, s, -jnp.inf) for s in scores(t, qs)]
        for h in heads:
            s_ref[t % 2, h, t] = st[h]
        return colmax(m, st)

    m_cur = diagonal(0, stacked_queries(0), ninf)
    for t in range(nq):
        slot, nslot = t % 2, (t + 1) % 2
        has_next = t + 1 < nq
        qs_next = stacked_queries(t + 1) if has_next else None
        acc_ref[...] = jnp.zeros_like(acc_ref)

        def body(j, carry, slot=slot, nslot=nslot, has_next=has_next, qs_next=qs_next, m_cur=m_cur):
            l, m_next = carry
            pt = [jnp.exp2(s_ref[slot, h, j] - m_cur[h]) for h in heads]
            pv = [jnp.dot(vt_ref[h, j], pt[h].astype(BF16), preferred_element_type=F32) for h in heads]
            for h in heads:
                acc_ref[h] += pv[h]
            l = tuple(lh + jnp.sum(p, axis=0, keepdims=True) for lh, p in zip(l, pt))
            if has_next:
                st = scores(j, qs_next)
                for h in heads:
                    s_ref[nslot, h, j] = st[h]
                m_next = colmax(m_next, st)
            return l, m_next

        lsum, m_next = lax.fori_loop(0, t + 1, body, (zeros, ninf), unroll=ATTN_UNROLL)
        if has_next:
            m_next = diagonal(t + 1, qs_next, m_next)

        for h in heads:
            acc = acc_ref[h]
            l = lsum[h]
            ot = acc[:, :tq] / l[:, :tq] - lam * (acc[:, tq:] / l[:, tq:])
            ot = ot * lax.rsqrt(jnp.mean(ot * ot, axis=0, keepdims=True) + SUBLN_EPS) * ogt_ref[...]
            o_ref[t * tq:(t + 1) * tq, h * hw:(h + 1) * hw] = (ot * (1.0 - LAM_INIT)).T.astype(o_ref.dtype)
        m_cur = m_next


def _attention(q, k, v, lq1, lk1, lq2, lk2, og, B, S):
    T = q.shape[0]
    tq = min(ATTN_ROWS, S)
    nq = S // tq
    hw = 2 * DIFF_HEAD_DIM
    nh = ATTN_HEADS_PER_STEP
    gw = nh * hw
    small = lambda b, h: (0, 0)
    seq = pl.BlockSpec((S, gw), lambda b, h: (b, h))
    return pl.pallas_call(
        _attn_kernel,
        grid=(B, DIFF_HEADS // nh),
        in_specs=[pl.BlockSpec(lq1.shape, small), pl.BlockSpec(lk1.shape, small),
                  pl.BlockSpec(lq2.shape, small), pl.BlockSpec(lk2.shape, small),
                  pl.BlockSpec(og.shape, small), seq, seq, seq],
        out_specs=seq,
        out_shape=jax.ShapeDtypeStruct((T, DIFF_WIDTH), BF16),
        scratch_shapes=[pltpu.VMEM((nh, nq, hw, tq), BF16), pltpu.VMEM((2, nh, nq, tq, 2 * tq), F32),
                        pltpu.VMEM((nh, hw, 2 * tq), F32)],
        compiler_params=_cparams(("arbitrary", "arbitrary")),
        name="diff_attn",
    )(lq1, lk1, lq2, lk2, og, q, k, v)


def _rwkv_kernel(zr_ref, zk_ref, zv_ref, wp_ref, ap_ref, gg_ref, mu_ref, w0_ref, a0_ref, kk_ref,
                 ka_ref, rk_ref, gnw_ref, gnb_ref, seg_ref, o_ref, state_ref, carry_ref):
    R = zr_ref.shape[0]
    C = min(RWKV_CHUNK, R)
    nch = R // C
    N = RWKV_HEAD
    W = RWKV_WIDTH
    HP = 2 * N
    n_pairs = W // HP

    @pl.when(pl.program_id(1) == 0)
    def _():
        state_ref[...] = jnp.zeros_like(state_ref)
        carry_ref[...] = jnp.zeros_like(carry_ref)

    zr, zk, zv = zr_ref[...], zk_ref[...], zv_ref[...]
    last = V7X_SUBLANES - 1
    r = zr + (_shift_rows(zr, carry_ref[last:last + 1, 0:W]) - zr) * mu_ref[0:1, :]
    k = zk + (_shift_rows(zk, carry_ref[last:last + 1, W:2 * W]) - zk) * mu_ref[1:2, :]
    v = zv + (_shift_rows(zv, carry_ref[last:last + 1, 2 * W:3 * W]) - zv) * mu_ref[2:3, :]
    carry_ref[:, 0:W] = zr[R - V7X_SUBLANES:R, :]
    carry_ref[:, W:2 * W] = zk[R - V7X_SUBLANES:R, :]
    carry_ref[:, 2 * W:3 * W] = zv[R - V7X_SUBLANES:R, :]

    lw = -math.exp(-0.5) * jax.nn.sigmoid(w0_ref[...] + wp_ref[...])
    a = jax.nn.sigmoid(a0_ref[...] + ap_ref[...])
    kk = k * kk_ref[...]
    kk = kk * lax.rsqrt(jnp.maximum(_dot_exact_rhs(kk * kk, seg_ref[...]), 1e-24))
    k2 = k * (1.0 + (a - 1.0) * ka_ref[...])
    a_s = -kk
    b_s = kk * a

    rr = lax.broadcasted_iota(jnp.int32, (R, R), 0)
    cc = lax.broadcasted_iota(jnp.int32, (R, R), 1)
    same_chunk = (rr // C) == (cc // C)
    L = _dot_exact_lhs((same_chunk & (cc <= rr)).astype(BF16), lw)
    bonus_w = r * k2 * rk_ref[...]

    P2 = 2 * C
    sr = lax.broadcasted_iota(jnp.int32, (P2, HP), 0)
    sc = lax.broadcasted_iota(jnp.int32, (P2, HP), 1)
    stack_mask = (sr < C) == (sc < N)
    br = lax.broadcasted_iota(jnp.int32, (P2, P2), 0)
    bc = lax.broadcasted_iota(jnp.int32, (P2, P2), 1)
    same_head = (br < C) == (bc < C)
    tr = jnp.where(br >= C, br - C, br)
    tc = jnp.where(bc >= C, bc - C, bc)
    strict = same_head & (tc < tr)
    lower = same_head & (tc <= tr)
    eye_p = (br == bc).astype(F32)
    kr = lax.broadcasted_iota(jnp.int32, (HP, HP), 0)
    kc = lax.broadcasted_iota(jnp.int32, (HP, HP), 1)
    eye_k = kr == kc

    def dup(x):
        return jnp.concatenate([x, x], axis=0)

    def stack(x):
        return jnp.where(stack_mask, dup(x), 0.0)

    n_sq = int(math.log2(C)) - 1
    items = [(c, p) for c in range(nch) for p in range(n_pairs)]

    def prep(c, p):
        rows = slice(c * C, (c + 1) * C)
        lanes = slice(p * HP, (p + 1) * HP)
        Lc = L[rows, lanes]
        lwc = lw[rows, lanes]
        LC = Lc[C - 1:C, :]
        enL = jnp.exp(-Lc)
        eCL = jnp.exp(LC - Lc)
        b_c, k_c = b_s[rows, lanes], k2[rows, lanes]
        return dict(
            xa=stack(a_s[rows, lanes] * jnp.exp(Lc - lwc)), xr=stack(r[rows, lanes] * jnp.exp(Lc)),
            bt=dup(b_c * enL), kt=dup(k_c * enL), bh=stack(b_c * eCL), kh=stack(k_c * eCL),
            vs=stack(v[rows, lanes]), pc=jnp.exp(LC))

    d = [prep(c, p) for c, p in items]
    aa = [_dot_nt(jnp.concatenate([e["xa"], e["xr"]], axis=0),
                  jnp.concatenate([e["bt"], e["kt"]], axis=0)) for e in d]
    a_ab = [jnp.where(strict, m[:P2, :P2], 0.0) for m in aa]
    a_ak = [jnp.where(strict, m[:P2, P2:], 0.0) for m in aa]
    a_rb = [jnp.where(lower, m[P2:, :P2], 0.0) for m in aa]
    a_rk = [jnp.where(lower, m[P2:, P2:], 0.0) for m in aa]
    tm = [eye_p + m for m in a_ab]
    npw = a_ab
    for _ in range(n_sq):
        npw = [_dot(m, m) for m in npw]
        tm = [t + _dot(t, m) for t, m in zip(tm, npw)]
    av = [_dot(m, e["vs"]) for m, e in zip(a_ak, d)]
    z = [_dot(t, jnp.concatenate([e["xa"], x], axis=1)) for t, e, x in zip(tm, d, av)]
    w = [_dot(m, x) for m, x in zip(a_rb, z)]
    rkv = [_dot(m, e["vs"]) for m, e in zip(a_rk, d)]
    gh = [_dot_tn(e["bh"], x) for e, x in zip(d, z)]
    khv = [_dot_tn(e["kh"], e["vs"]) for e in d]

    for idx, (c, p) in enumerate(items):
        e = d[idx]
        rows = slice(c * C, (c + 1) * C)
        lanes = slice(p * HP, (p + 1) * HP)
        rp = e["xr"] + w[idx][:, :HP]
        y0 = w[idx][:, HP:] + rkv[idx]
        gm = gh[idx][:, :HP]
        hm = gh[idx][:, HP:] + khv[idx]
        st = state_ref[p]
        yg = _dot(jnp.concatenate([rp, gm], axis=0), st)
        pc_col = jnp.sum(jnp.where(eye_k, e["pc"], 0.0), axis=1, keepdims=True)
        state_ref[p] = pc_col * st + yg[P2:] + hm
        ys = yg[:P2] + y0
        mu = jnp.sum(ys, axis=-1, keepdims=True) * (1.0 / N)
        yc = jnp.where(stack_mask, ys - mu, 0.0)
        var = jnp.sum(yc * yc, axis=-1, keepdims=True) * (1.0 / N)
        yn = yc * lax.rsqrt(var + GN_EPS)
        bonus = jnp.sum(stack(bonus_w[rows, lanes]), axis=-1, keepdims=True) * e["vs"]
        yn = yn[:C] + yn[C:]
        bonus = bonus[:C] + bonus[C:]
        o_ref[rows, lanes] = ((yn * gnw_ref[:, lanes] + gnb_ref[:, lanes] + bonus)
                              * gg_ref[rows, lanes]).astype(o_ref.dtype)


def _rwkv(zr, zk, zv, wp, ap, gg, mu_rkv, w0, a0, k_k, k_a, r_k, gn_w, gn_b, seg, B, S):
    T = zr.shape[0]
    C = min(RWKV_ROWS, S)
    nc = S // C
    W = RWKV_WIDTH
    HP = 2 * RWKV_HEAD
    row = lambda b, i: (b * nc + i, 0)
    full = lambda b, i: (0, 0)

    def fs(a):
        return pl.BlockSpec(a.shape, full)

    return pl.pallas_call(
        _rwkv_kernel,
        grid=(B, nc),
        in_specs=[pl.BlockSpec((C, W), row)] * 6 + [fs(mu_rkv), fs(w0), fs(a0), fs(k_k), fs(k_a),
                                                    fs(r_k), fs(gn_w), fs(gn_b), fs(seg)],
        out_specs=pl.BlockSpec((C, W), row),
        out_shape=jax.ShapeDtypeStruct((T, W), BF16),
        scratch_shapes=[pltpu.VMEM((W // HP, HP, HP), F32),
                        pltpu.VMEM((V7X_SUBLANES, 3 * W), F32)],
        compiler_params=_cparams(("arbitrary", "arbitrary")),
        name="rwkv7",
    )(zr, zk, zv, wp, ap, gg, mu_rkv, w0, a0, k_k, k_a, r_k, gn_w, gn_b, seg)


ROUTE_COLS = 8
ROUTER_GROUP_LANE = 0
ROUTER_EXPERT_LANE = N_GROUPS


def _out_router_kernel(x_ref, od_ref, orw_ref, wo_ref, mg_ref, wrh_ref, wrl_ref,
                       x1_out, hm_out, route_out, cnt_out):
    tm = x_ref.shape[0]
    x1 =(x_ref[...] + jnp.dot(od_ref[...], wo_ref[0:DIFF_WIDTH, :], preferred_element_type=F32)
          + jnp.dot(orw_ref[...], wo_ref[DIFF_WIDTH:, :], preferred_element_type=F32))
    x1_out[...] = x1
    hm = _rms(x1, mg_ref[...], NORM_EPS)
    hm_out[...] = hm.astype(hm_out.dtype)

    hi = hm.astype(BF16)
    lo = (hm - hi.astype(F32)).astype(BF16)
    lg = (jnp.dot(hi, wrh_ref[...], preferred_element_type=F32)
          + jnp.dot(hi, wrl_ref[...], preferred_element_type=F32)
          + jnp.dot(lo, wrh_ref[...], preferred_element_type=F32))

    lane = lax.broadcasted_iota(jnp.int32, lg.shape, 1).astype(F32)
    big = float(V7X_LANES)
    ninf = -jnp.inf
    gmask = lane < N_GROUPS
    gmax = jnp.max(jnp.where(gmask, lg, ninf), axis=-1, keepdims=True)
    g_sel = jnp.min(jnp.where(gmask & (lg == gmax), lane, big), axis=-1, keepdims=True)
    g_w = 1.0 / jnp.sum(jnp.where(gmask, jnp.exp(lg - gmax), 0.0), axis=-1, keepdims=True)
    lo_lane = ROUTER_EXPERT_LANE + g_sel * EXPERTS_PER_GROUP
    emask = (lane >= lo_lane) & (lane < lo_lane + EXPERTS_PER_GROUP)
    v1 = jnp.max(jnp.where(emask, lg, ninf), axis=-1, keepdims=True)
    i1 = jnp.min(jnp.where(emask & (lg == v1), lane, big), axis=-1, keepdims=True)
    emask2 = emask & (lane != i1)
    v2 = jnp.max(jnp.where(emask2, lg, ninf), axis=-1, keepdims=True)
    i2 = jnp.min(jnp.where(emask2 & (lg == v2), lane, big), axis=-1, keepdims=True)
    e2x = jnp.exp(v2 - v1)
    den = 1.0 + e2x
    wt1 = (1.0 / den) * g_w
    wt2 = (e2x / den) * g_w

    oh1 = lane == i1
    oh2 = lane == i2
    oh = (oh1 | oh2).astype(F32)
    ri = lax.broadcasted_iota(jnp.int32, (tm, tm), 0)
    ci = lax.broadcasted_iota(jnp.int32, (tm, tm), 1)
    before = (ci < ri).astype(BF16)
    prefix = jnp.dot(before, oh.astype(BF16), preferred_element_type=F32)
    rank1 = jnp.sum(jnp.where(oh1, prefix, 0.0), axis=-1, keepdims=True)
    rank2 = jnp.sum(jnp.where(oh2, prefix, 0.0), axis=-1, keepdims=True)
    cnt_out[...] = jnp.broadcast_to(jnp.sum(oh, axis=0, keepdims=True), cnt_out.shape)

    e1 = i1 - ROUTER_EXPERT_LANE
    e2 = i2 - ROUTER_EXPERT_LANE
    col = lax.broadcasted_iota(jnp.int32, (tm, ROUTE_COLS), 1)
    route = jnp.where(col == 0, e1, jnp.where(col == 1, e2, jnp.where(col == 2, wt1, jnp.where(
        col == 3, wt2, jnp.where(col == 4, rank1, jnp.where(col == 5, rank2, 0.0))))))
    route_out[...] = route


def _out_router(x2, od, orw, w_out, moe_g, wr_hi, wr_lo):
    T, D = x2.shape
    tm = min(PROJ_ROWS, T)
    row = lambda i: (i, 0)
    full = lambda i: (0, 0)

    def fs(a):
        return pl.BlockSpec(a.shape, full)

    return pl.pallas_call(
        _out_router_kernel,
        grid=(T // tm,),
        in_specs=[pl.BlockSpec((tm, D), row), pl.BlockSpec((tm, DIFF_WIDTH), row),
                  pl.BlockSpec((tm, RWKV_WIDTH), row), fs(w_out), fs(moe_g), fs(wr_hi), fs(wr_lo)],
        out_specs=[pl.BlockSpec((tm, D), row), pl.BlockSpec((tm, D), row),
                   pl.BlockSpec((tm, ROUTE_COLS), row), pl.BlockSpec((V7X_SUBLANES, V7X_LANES), row)],
        out_shape=[jax.ShapeDtypeStruct((T, D), F32), jax.ShapeDtypeStruct((T, D), BF16),
                   jax.ShapeDtypeStruct((T, ROUTE_COLS), F32),
                   jax.ShapeDtypeStruct((T // tm * V7X_SUBLANES, V7X_LANES), F32)],
        compiler_params=_cparams(("arbitrary",)),
        name="out_router",
    )(x2, od, orw, w_out, moe_g, wr_hi, wr_lo)


RUN_ALIGN = V7X_SUBLANES
TAB_DST, TAB_SRC, TAB_LEN, TAB_NUSED = 0, N_EXPERTS, 2 * N_EXPERTS, 3 * N_EXPERTS


def _stage_rows(tm):
    need = 2 * tm + N_EXPERTS * (RUN_ALIGN - 1)
    return -(-need // V7X_LANES) * V7X_LANES


def _run_copies(tab_ref, max_len, make_copy, op):
    n_bits = int(math.log2(max_len // RUN_ALIGN)) + 1

    def per_expert(e, c):
        n = tab_ref[0, 0, TAB_LEN + e]
        dst = tab_ref[0, 0, TAB_DST + e]
        src = tab_ref[0, 0, TAB_SRC + e]
        off = jnp.int32(0)
        for bit in reversed(range(n_bits)):
            size = RUN_ALIGN << bit
            hit = (n & size) != 0

            @pl.when(hit)
            def _(off=off, size=size):
                cp = make_copy(pl.multiple_of(src + off, RUN_ALIGN), pl.multiple_of(dst + off, RUN_ALIGN), size)
                getattr(cp, op)()

            off = off + jnp.where(hit, size, 0)
        return c

    lax.fori_loop(0, N_EXPERTS, per_expert, 0)


def _dispatch_kernel(tab_ref, tail_ref, sl_ref, hm_ref, xb_ref, stage_ref, zero_ref, sem, zsem):
    tm = hm_ref.shape[0]
    ns = stage_ref.shape[0]
    bm = zero_ref.shape[0]
    nb = xb_ref.shape[0] // bm
    n_used = tail_ref[0, 0, TAB_NUSED]

    def make_copy(src, dst, size):
        return pltpu.make_async_copy(stage_ref.at[pl.ds(src, size)], xb_ref.at[pl.ds(dst, size)], sem)

    def zero_copy(src, dst, size):
        return pltpu.make_async_copy(zero_ref.at[pl.ds(src, size)], xb_ref.at[pl.ds(dst, size)], zsem)

    def zero_blocks(op):
        def one(b, c):
            getattr(zero_copy(0, pl.multiple_of(b * bm, bm), bm), op)()
            return c
        lax.fori_loop(n_used, nb, one, 0)

    @pl.when(pl.program_id(0) == 0)
    def _():
        zero_ref[...] = jnp.zeros_like(zero_ref)
        _run_copies(tail_ref, bm, zero_copy, "start")
        zero_blocks("start")

    srow = lax.broadcasted_iota(jnp.int32, (ns, tm), 0)
    sel = (srow == sl_ref[0, 0:1, :]) | (srow == sl_ref[0, 1:2, :])
    stage_ref[...] = _pack16(jnp.dot(sel.astype(BF16), hm_ref[...], preferred_element_type=F32))
    _run_copies(tab_ref, tm, make_copy, "start")
    _run_copies(tab_ref, tm, make_copy, "wait")

    @pl.when(pl.program_id(0) == pl.num_programs(0) - 1)
    def _():
        _run_copies(tail_ref, bm, zero_copy, "wait")
        zero_blocks("wait")


def _dispatch(tab, tail, sl_rows, hm, P):
    T, D = hm.shape
    tm = min(PROJ_ROWS, T)
    return pl.pallas_call(
        _dispatch_kernel,
        grid=(T // tm,),
        in_specs=[pl.BlockSpec((1, 1, tab.shape[-1]), lambda i: (i, 0, 0), memory_space=pltpu.SMEM),
                  pl.BlockSpec((1, 1, tail.shape[-1]), lambda i: (0, 0, 0), memory_space=pltpu.SMEM),
                  pl.BlockSpec((1, 2, tm), lambda i: (i, 0, 0)),
                  pl.BlockSpec((tm, D), lambda i: (i, 0))],
        out_specs=pl.BlockSpec(memory_space=pl.ANY),
        out_shape=jax.ShapeDtypeStruct((P, D // 2), PACKED),
        scratch_shapes=[pltpu.VMEM((_stage_rows(tm), D // 2), PACKED), pltpu.VMEM((MOE_ROWS, D // 2), PACKED),
                        pltpu.SemaphoreType.DMA(()), pltpu.SemaphoreType.DMA(())],
        compiler_params=_cparams(("arbitrary",)),
        name="moe_dispatch",
    )(tab, tail, sl_rows, hm)


def _expert_kernel(be_ref, nu_ref, xb_ref, wg_ref, wu_ref, wd_ref, yb_ref, wgb, wub, wdb):
    i = pl.program_id(0)
    prev = be_ref[jnp.maximum(i - 1, 0)]
    changed = (i == 0) | (be_ref[i] != prev)

    @pl.when(changed)
    def _():
        wgb[...] = wg_ref[0].astype(BF16)
        wub[...] = wu_ref[0].astype(BF16)
        wdb[...] = wd_ref[0].astype(BF16)

    @pl.when(i < nu_ref[0])
    def _():
        xb = _unpack16(xb_ref[...]).astype(BF16)
        gate = jnp.dot(xb, wgb[...], preferred_element_type=F32)
        up = jnp.dot(xb, wub[...], preferred_element_type=F32)
        hdn = (gate * jax.nn.sigmoid(gate)) * up
        y = jnp.dot(hdn.astype(BF16), wdb[...], preferred_element_type=F32)
        yb_ref[...] = _pack16(y.astype(BF16).astype(F32))

    @pl.when(i >= nu_ref[0])
    def _():
        yb_ref[...] = jnp.zeros_like(yb_ref)


def _experts(block_e, n_used, xb, w_gate, w_up, w_down):
    P, DP = xb.shape
    D = 2 * DP
    bm = MOE_ROWS
    nb = P // bm
    E = D_EXPERT
    grid_spec = pltpu.PrefetchScalarGridSpec(
        num_scalar_prefetch=2,
        grid=(nb,),
        in_specs=[pl.BlockSpec((bm, DP), lambda i, be, nu: (jnp.minimum(i, nu[0] - 1), 0)),
                  pl.BlockSpec((1, D, E), lambda i, be, nu: (be[i], 0, 0)),
                  pl.BlockSpec((1, D, E), lambda i, be, nu: (be[i], 0, 0)),
                  pl.BlockSpec((1, E, D), lambda i, be, nu: (be[i], 0, 0))],
        out_specs=pl.BlockSpec((bm, DP), lambda i, be, nu: (jnp.where(i < nu[0], i, nb - 1), 0)),
        scratch_shapes=[pltpu.VMEM((D, E), BF16), pltpu.VMEM((D, E), BF16), pltpu.VMEM((E, D), BF16)],
    )
    return pl.pallas_call(
        _expert_kernel,
        grid_spec=grid_spec,
        out_shape=jax.ShapeDtypeStruct((P, DP), PACKED),
        input_output_aliases={2: 0},
        compiler_params=_cparams(("arbitrary",)),
        name="moe_experts",
    )(block_e, n_used, xb, w_gate, w_up, w_down)


def _final_kernel(tab_ref, x1_ref, route_ref, p_ref, wpl_ref, plg_ref, pgg_ref, wgate_ref, yb_ref,
                  o_ref, stage_ref, sem):
    tm = x1_ref.shape[0]
    ns = stage_ref.shape[0]
    stage_ref[...] = jnp.zeros_like(stage_ref)

    def make_copy(src, dst, size):
        return pltpu.make_async_copy(yb_ref.at[pl.ds(dst, size)], stage_ref.at[pl.ds(src, size)], sem)

    _run_copies(tab_ref, tm, make_copy, "start")
    pe = _rms(jnp.dot(p_ref[...].astype(BF16), wpl_ref[...], preferred_element_type=F32),
              plg_ref[...], NORM_EPS)
    _run_copies(tab_ref, tm, make_copy, "wait")

    route = route_ref[...]
    scol = lax.broadcasted_iota(jnp.int32, (tm, ns), 1).astype(F32)
    wsel = (jnp.where(scol == route[:, 0:1], route[:, 2:3], 0.0)
            + jnp.where(scol == route[:, 1:2], route[:, 3:4], 0.0))
    w_hi = wsel.astype(BF16)
    w_lo = (wsel - w_hi.astype(F32)).astype(BF16)
    yb16 = _unpack16(stage_ref[...]).astype(BF16)
    moe =(jnp.dot(w_hi, yb16, preferred_element_type=F32)
           + jnp.dot(w_lo, yb16, preferred_element_type=F32))
    x2 = x1_ref[...] + moe
    gate = jax.nn.sigmoid(_dot(_rms(x2, pgg_ref[...], NORM_EPS), wgate_ref[...]))
    o_ref[...] = x2 + pe * gate


def _final(tab, x1, route, p2, w_pl, pl_g, pl_gate_g, w_gate, yb):
    T, D = x1.shape
    tm = min(PROJ_ROWS, T)
    row = lambda i: (i, 0)
    full = lambda i: (0, 0)

    def fs(a):
        return pl.BlockSpec(a.shape, full)

    return pl.pallas_call(
        _final_kernel,
        grid=(T // tm,),
        in_specs=[pl.BlockSpec((1, 1, tab.shape[-1]), lambda i: (i, 0, 0), memory_space=pltpu.SMEM),
                  pl.BlockSpec((tm, D), row), pl.BlockSpec((tm, ROUTE_COLS), row),
                  pl.BlockSpec((tm, PL_DIM), row), fs(w_pl), fs(pl_g), fs(pl_gate_g), fs(w_gate),
                  pl.BlockSpec(memory_space=pl.ANY)],
        out_specs=pl.BlockSpec((tm, D), row),
        out_shape=jax.ShapeDtypeStruct((T, D), F32),
        scratch_shapes=[pltpu.VMEM((_stage_rows(tm), D // 2), PACKED), pltpu.SemaphoreType.DMA(())],
        compiler_params=_cparams(("arbitrary",)),
        name="combine_final",
    )(tab, x1, route, p2, w_pl, pl_g, pl_gate_g, w_gate, yb)


def kernel(x, p, positions, attn_norm_g, w_in, q_norm_g, k_norm_g, lambda_q1, lambda_k1, lambda_q2, lambda_k2, diff_out_g, mu_rkv, mu_wag, w0, w_lora_a, w_lora_b, a0, a_lora_a, a_lora_b, g_lora_a, g_lora_b, k_k, k_a, r_k, gn_w, gn_b, w_out, moe_norm_g, w_group, w_expert_router, w_gate, w_up, w_down, w_pl, pl_norm_g, pl_gate_norm_g, w_pl_gate):
    B, S, D = x.shape
    T = B * S
    assert p.shape[0] == 1, "one layer"
    x2 = x.reshape(T, D)
    p2 = p[0].reshape(T, PL_DIM)
    pos = positions.astype(F32).reshape(T, 1)

    def row(a):
        return a.reshape(1, -1).astype(F32)

    half = DIFF_HEAD_DIM // 2
    inv_freq = ROPE_THETA ** (-jnp.arange(half, dtype=F32) / half)
    invf = jnp.tile(inv_freq, V7X_LANES // half).reshape(1, V7X_LANES)
    li = jnp.arange(DIFF_WIDTH) // DIFF_HEAD_DIM
    seg = (li[:, None] == li[None, :]).astype(BF16)
    reps = DIFF_WIDTH // DIFF_HEAD_DIM

    q, k, v, zr, zk, zv, wp, ap, gg = _in_proj(
        x2, pos, row(attn_norm_g[0]), w_in[0].astype(BF16),
        w_lora_a[0].astype(BF16), a_lora_a[0].astype(BF16), g_lora_a[0].astype(BF16),
        w_lora_b[0].astype(BF16), a_lora_b[0].astype(BF16), g_lora_b[0].astype(BF16),
        mu_wag[0].astype(F32), row(jnp.tile(q_norm_g[0], reps)), row(jnp.tile(k_norm_g[0], reps)),
        invf, seg, B, S)

    o_diff = _attention(q, k, v, row(lambda_q1[0]), row(lambda_k1[0]), row(lambda_q2[0]),
                        row(lambda_k2[0]), diff_out_g[0].astype(F32).reshape(-1, 1), B, S)
    o_rwkv = _rwkv(zr, zk, zv, wp, ap, gg, mu_rkv[0].astype(F32), row(w0[0]), row(a0[0]),
                   row(k_k[0]), row(k_a[0]), row(r_k[0]), row(gn_w[0]), row(gn_b[0]), seg, B, S)

    wr = jnp.concatenate([w_group[0], jnp.transpose(w_expert_router[0], (1, 0, 2)).reshape(D, N_EXPERTS)],
                         axis=1).astype(F32)
    wr = jnp.pad(wr, ((0, 0), (0, V7X_LANES - wr.shape[1])))
    wr_hi = wr.astype(BF16)
    wr_lo = (wr - wr_hi.astype(F32)).astype(BF16)
    x1, hm, route, cnt = _out_router(x2, o_diff, o_rwkv, w_out[0].astype(BF16), row(moe_norm_g[0]),
                                     wr_hi, wr_lo)

    bm = MOE_ROWS
    tm = min(PROJ_ROWS, T)
    nt = T // tm
    i32 = jnp.int32
    cnt_te = cnt.reshape(nt, V7X_SUBLANES, V7X_LANES)[:, 0, ROUTER_EXPERT_LANE:ROUTER_EXPERT_LANE + N_EXPERTS]
    len_te = (cnt_te.astype(i32) + RUN_ALIGN - 1) // RUN_ALIGN * RUN_ALIGN
    tot_e = jnp.sum(len_te, axis=0)
    pcounts = (tot_e + bm - 1) // bm * bm
    pends = jnp.cumsum(pcounts)
    pstarts = pends - pcounts
    dst_te = pstarts[None, :] + jnp.cumsum(len_te, axis=0) - len_te
    src_te = jnp.cumsum(len_te, axis=1) - len_te
    tab = jnp.concatenate([dst_te, src_te, len_te, jnp.zeros_like(len_te)], axis=1).reshape(nt, 1, 4 * N_EXPERTS)
    nb = -(-(2 * T + nt * N_EXPERTS * (RUN_ALIGN - 1)) // bm) + N_EXPERTS
    P = nb * bm
    n_used = (pends[-1] // bm).astype(i32).reshape(1)
    block_start = jnp.arange(nb, dtype=i32) * bm
    block_e = jnp.minimum(jnp.sum((pends[None, :] <= block_start[:, None]).astype(i32), axis=1),
                          N_EXPERTS - 1)
    eid = route[:, 0:2].astype(i32).reshape(nt, tm, 2)
    run_src = jnp.sum(jnp.where(eid[..., None] == jnp.arange(N_EXPERTS, dtype=i32), src_te[:, None, None, :], 0),
                      axis=-1)
    sl = run_src + route[:, 4:6].astype(i32).reshape(nt, tm, 2)
    sl_rows = jnp.transpose(sl, (0, 2, 1))
    route2 = jnp.concatenate([sl.reshape(T, 2).astype(F32), route[:, 2:4],
                              jnp.zeros((T, ROUTE_COLS - 4), F32)], axis=1)

    zero_e = jnp.zeros_like(tot_e)
    tail = jnp.concatenate([pstarts + tot_e, zero_e, pcounts - tot_e,
                            jnp.broadcast_to(n_used, (N_EXPERTS,))]).reshape(1, 1, 4 * N_EXPERTS)
    xb = _dispatch(tab, tail, sl_rows, hm, P)
    yb = _experts(block_e, n_used, xb, w_gate[0], w_up[0], w_down[0])
    out = _final(tab, x1, route2, p2, w_pl[0].astype(BF16), row(pl_norm_g[0]), row(pl_gate_norm_g[0]),
                 w_pl_gate[0].astype(BF16), yb)
    return out.reshape(B, S, D)
```

```python
import functools
import math

import jax
import jax.numpy as jnp
from jax import lax
from jax.experimental import pallas as pl
from jax.experimental.pallas import tpu as pltpu

F32 = jnp.float32
BF16 = jnp.bfloat16

D_MODEL = 1024
PL_DIM = 256
DIFF_WIDTH = 512
RWKV_WIDTH = 512
DIFF_HEAD_DIM = 64
DIFF_HEADS = 4
RWKV_HEAD = 64
RWKV_HEADS = 8
D_DECAY_LORA = 64
D_AAA_LORA = 64
D_GATE_LORA = 160
ROPE_THETA = 10000.0
NORM_EPS = 1e-6
SUBLN_EPS = 1e-5
GN_EPS = 64e-5
N_GROUPS = 4
EXPERTS_PER_GROUP = 8
N_EXPERTS = 32
D_EXPERT = 512
LAM_INIT = 0.8 - 0.6 * math.exp(0.0)

V7X_LANES = 128
V7X_SUBLANES = 8
V7X_VMEM_BYTES = 64 * 1024 * 1024

IN_PROJ_ROWS = 256
PROJ_ROWS = 256
TILES_PER_STEP = 4
ATTN_ROWS = 256
ATTN_UNROLL = 4
ATTN_HEADS_PER_STEP = 2
RWKV_CHUNK = 64
RWKV_ROWS = 256
MOE_ROWS = 512
VMEM_LIMIT = 56 * 1024 * 1024


def _cparams(sem):
    return pltpu.CompilerParams(dimension_semantics=sem, vmem_limit_bytes=VMEM_LIMIT)


def _dot(a, b):
    return jnp.dot(a.astype(BF16), b.astype(BF16), preferred_element_type=F32)


def _dot_nt(a, b):
    return lax.dot_general(a.astype(BF16), b.astype(BF16), (((1,), (1,)), ((), ())),
                           preferred_element_type=F32)


def _dot_tn(a, b):
    return lax.dot_general(a.astype(BF16), b.astype(BF16), (((0,), (0,)), ((), ())),
                           preferred_element_type=F32)


def _split3(x):
    hi = x.astype(BF16)
    r1 = x - hi.astype(F32)
    mid = r1.astype(BF16)
    lo = (r1 - mid.astype(F32)).astype(BF16)
    return hi, mid, lo


def _dot_exact_rhs(x, m01):
    hi = x.astype(BF16)
    mid = (x - hi.astype(F32)).astype(BF16)
    return (jnp.dot(hi, m01, preferred_element_type=F32)
            + jnp.dot(mid, m01, preferred_element_type=F32))


def _dot_exact_lhs(m01, x):
    hi, mid, lo = _split3(x)
    return (jnp.dot(m01, hi, preferred_element_type=F32)
            + jnp.dot(m01, mid, preferred_element_type=F32)
            + jnp.dot(m01, lo, preferred_element_type=F32))


PACKED = jnp.uint32


def _pack16(x):
    h = x.shape[1] // 2
    lo = lax.bitcast_convert_type(x[:, :h], PACKED) >> 16
    hi = lax.bitcast_convert_type(x[:, h:], PACKED) & jnp.uint32(0xFFFF0000)
    return hi | lo


def _unpack16(w):
    lo = lax.bitcast_convert_type(w << 16, F32)
    hi = lax.bitcast_convert_type(w & jnp.uint32(0xFFFF0000), F32)
    return jnp.concatenate([lo, hi], axis=1)


def _rms(x, g, eps):
    return x * lax.rsqrt(jnp.mean(x * x, axis=-1, keepdims=True) + eps) * g


def _shift_rows(z, prev_row):
    rolled = pltpu.roll(z, 1, axis=0)
    row = lax.broadcasted_iota(jnp.int32, z.shape, 0)
    return jnp.where(row == 0, prev_row, rolled)


def _in_proj_kernel(x_ref, pos_ref, g_ref, win_ref, w1_ref, a1_ref, g1_ref, w2_ref, a2_ref, g2_ref,
                    mu_ref, qg_ref, kg_ref, invf_ref, seg_ref,
                    q_out, k_out, v_out, zr_out, zk_out, zv_out, wp_out, ap_out, gg_out,
                    carry_ref):
    tm = x_ref.shape[0]

    @pl.when(pl.program_id(1) == 0)
    def _():
        carry_ref[...] = jnp.zeros_like(carry_ref)

    hn = _rms(x_ref[...], g_ref[...], NORM_EPS)
    prev = carry_ref[V7X_SUBLANES - 1:V7X_SUBLANES, :]
    dh = _shift_rows(hn, prev) - hn
    carry_ref[...] = hn[tm - V7X_SUBLANES:tm, :]

    hb = hn.astype(BF16)
    w = DIFF_WIDTH

    def proj(c):
        return jnp.dot(hb, win_ref[:, c * w:(c + 1) * w], preferred_element_type=F32)

    ang = pos_ref[...] * invf_ref[...]
    cos1, sin1 = jnp.cos(ang), jnp.sin(ang)
    cosf = jnp.concatenate([cos1] * (w // V7X_LANES), axis=1)
    sinf = jnp.concatenate([sin1] * (w // V7X_LANES), axis=1)
    lane = lax.broadcasted_iota(jnp.int32, (tm, w), 1)
    first_half = (lane % DIFF_HEAD_DIM) < (DIFF_HEAD_DIM // 2)
    sin_signed = jnp.where(first_half, -sinf, sinf)
    half = DIFF_HEAD_DIM // 2

    def norm_rope(z, gain):
        ss = _dot_exact_rhs(z * z, seg_ref[...])
        zn = z * lax.rsqrt(ss * (1.0 / DIFF_HEAD_DIM) + NORM_EPS) * gain
        partner = jnp.where(first_half, pltpu.roll(zn, w - half, axis=1), pltpu.roll(zn, half, axis=1))
        return zn * cosf + partner * sin_signed

    scale = DIFF_HEAD_DIM ** -0.5 * math.log2(math.e)
    q_out[...] = (norm_rope(proj(0), qg_ref[...]) * scale).astype(q_out.dtype)
    k_out[...] = norm_rope(proj(1), kg_ref[...]).astype(k_out.dtype)
    v_out[...] = proj(2).astype(v_out.dtype)
    zr_out[...] = proj(3)
    zk_out[...] = proj(4)
    zv_out[...] = proj(5)

    xw = hn + dh * mu_ref[0:1, :]
    xa = hn + dh * mu_ref[1:2, :]
    xg = hn + dh * mu_ref[2:3, :]
    wp_out[...] = _dot(jnp.tanh(_dot(xw, w1_ref[...])), w2_ref[...])
    ap_out[...] = _dot(_dot(xa, a1_ref[...]), a2_ref[...])
    gg_out[...] = _dot(jax.nn.sigmoid(_dot(xg, g1_ref[...])), g2_ref[...])


def _in_proj(x2, pos, attn_g, w_in, w1, a1, g1, w2, a2, g2, mu_wag, qg, kg, invf, seg, B, S):
    T, D = x2.shape
    tm = min(IN_PROJ_ROWS, S)
    ns = S // tm
    w = DIFF_WIDTH
    row = lambda b, i: (b * ns + i, 0)
    full = lambda b, i: (0, 0)

    def fs(a):
        return pl.BlockSpec(a.shape, full)

    outs = ([jax.ShapeDtypeStruct((T, w), BF16)] * 3 + [jax.ShapeDtypeStruct((T, w), F32)] * 6)
    return pl.pallas_call(
        _in_proj_kernel,
        grid=(B, ns),
        in_specs=[pl.BlockSpec((tm, D), row), pl.BlockSpec((tm, 1), row), fs(attn_g), fs(w_in),
                  fs(w1), fs(a1), fs(g1), fs(w2), fs(a2), fs(g2), fs(mu_wag), fs(qg), fs(kg),
                  fs(invf), fs(seg)],
        out_specs=[pl.BlockSpec((tm, w), row)] * 9,
        out_shape=outs,
        scratch_shapes=[pltpu.VMEM((V7X_SUBLANES, D), F32)],
        compiler_params=_cparams(("arbitrary", "arbitrary")),
        name="in_proj",
    )(x2, pos, attn_g, w_in, w1, a1, g1, w2, a2, g2, mu_wag, qg, kg, invf, seg)


def _attn_kernel(lq1_ref, lk1_ref, lq2_ref, lk2_ref, ogt_ref, q_ref, k_ref, v_ref, o_ref,
                 vt_ref, s_ref, acc_ref):
    nh, nq, hw, tq = vt_ref.shape
    tk = tq
    heads = range(nh)
    lam = (jnp.exp(jnp.sum(lq1_ref[...] * lk1_ref[...], axis=-1, keepdims=True))
           - jnp.exp(jnp.sum(lq2_ref[...] * lk2_ref[...], axis=-1, keepdims=True)) + LAM_INIT)

    for h in heads:
        for c in range(nq):
            vt_ref[h, c] = (v_ref[c * tk:(c + 1) * tk, h * hw:(h + 1) * hw]
                            .astype(F32).T.astype(vt_ref.dtype))

    lane = lax.broadcasted_iota(jnp.int32, (tq, hw), 1)

    def stacked_queries(t):
        out = []
        for h in heads:
            q = q_ref[t * tq:(t + 1) * tq, h * hw:(h + 1) * hw]
            zero = jnp.zeros_like(q)
            out.append(jnp.concatenate([jnp.where(lane < DIFF_HEAD_DIM, q, zero),
                                        jnp.where(lane >= DIFF_HEAD_DIM, q, zero)], axis=0))
        return out

    def scores(j, qs):
        start = pl.multiple_of(j * tk, tk)
        return [lax.dot_general(k_ref[pl.ds(start, tk), h * hw:(h + 1) * hw], qs[h],
                                (((1,), (1,)), ((), ())), preferred_element_type=F32) for h in heads]

    def colmax(m, st):
        return tuple(jnp.maximum(mh, jnp.max(s, axis=0, keepdims=True)) for mh, s in zip(m, st))

    krow = lax.broadcasted_iota(jnp.int32, (tk, 2 * tq), 0)
    qcol = lax.broadcasted_iota(jnp.int32, (tk, 2 * tq), 1)
    causal = krow <= jnp.where(qcol >= tq, qcol - tq, qcol)
    ninf = tuple(jnp.full((1, 2 * tq), -jnp.inf, F32) for _ in heads)
    zeros = tuple(jnp.zeros((1, 2 * tq), F32) for _ in heads)

    def diagonal(t, qs, m):
        st = [jnp.where(causal, s, -jnp.inf) for s in scores(t, qs)]
        for h in heads:
            s_ref[t % 2, h, t] = st[h]
        return colmax(m, st)

    m_cur = diagonal(0, stacked_queries(0), ninf)
    for t in range(nq):
        slot, nslot = t % 2, (t + 1) % 2
        has_next = t + 1 < nq
        qs_next = stacked_queries(t + 1) if has_next else None
        acc_ref[...] = jnp.zeros_like(acc_ref)

        def body(j, carry, slot=slot, nslot=nslot, has_next=has_next, qs_next=qs_next, m_cur=m_cur):
            l, m_next = carry
            pt = [jnp.exp2(s_ref[slot, h, j] - m_cur[h]) for h in heads]
            pv = [jnp.dot(vt_ref[h, j], pt[h].astype(BF16), preferred_element_type=F32) for h in heads]
            for h in heads:
                acc_ref[h] += pv[h]
            l = tuple(lh + jnp.sum(p, axis=0, keepdims=True) for lh, p in zip(l, pt))
            if has_next:
                st = scores(j, qs_next)
                for h in heads:
                    s_ref[nslot, h, j] = st[h]
                m_next = colmax(m_next, st)
            return l, m_next

        lsum, m_next = lax.fori_loop(0, t + 1, body, (zeros, ninf), unroll=ATTN_UNROLL)
        if has_next:
            m_next = diagonal(t + 1, qs_next, m_next)

        for h in heads:
            acc = acc_ref[h]
            l = lsum[h]
            ot = acc[:, :tq] / l[:, :tq] - lam * (acc[:, tq:] / l[:, tq:])
            ot = ot * lax.rsqrt(jnp.mean(ot * ot, axis=0, keepdims=True) + SUBLN_EPS) * ogt_ref[...]
            o_ref[t * tq:(t + 1) * tq, h * hw:(h + 1) * hw] = (ot * (1.0 - LAM_INIT)).T.astype(o_ref.dtype)
        m_cur = m_next


def _attention(q, k, v, lq1, lk1, lq2, lk2, og, B, S):
    T = q.shape[0]
    tq = min(ATTN_ROWS, S)
    nq = S // tq
    hw = 2 * DIFF_HEAD_DIM
    nh = ATTN_HEADS_PER_STEP
    gw = nh * hw
    small = lambda b, h: (0, 0)
    seq = pl.BlockSpec((S, gw), lambda b, h: (b, h))
    return pl.pallas_call(
        _attn_kernel,
        grid=(B, DIFF_HEADS // nh),
        in_specs=[pl.BlockSpec(lq1.shape, small), pl.BlockSpec(lk1.shape, small),
                  pl.BlockSpec(lq2.shape, small), pl.BlockSpec(lk2.shape, small),
                  pl.BlockSpec(og.shape, small), seq, seq, seq],
        out_specs=seq,
        out_shape=jax.ShapeDtypeStruct((T, DIFF_WIDTH), BF16),
        scratch_shapes=[pltpu.VMEM((nh, nq, hw, tq), BF16), pltpu.VMEM((2, nh, nq, tq, 2 * tq), F32),
                        pltpu.VMEM((nh, hw, 2 * tq), F32)],
        compiler_params=_cparams(("arbitrary", "arbitrary")),
        name="diff_attn",
    )(lq1, lk1, lq2, lk2, og, q, k, v)


def _rwkv_kernel(zr_ref, zk_ref, zv_ref, wp_ref, ap_ref, gg_ref, mu_ref, w0_ref, a0_ref, kk_ref,
                 ka_ref, rk_ref, gnw_ref, gnb_ref, seg_ref, o_ref, state_ref, carry_ref):
    R = zr_ref.shape[0]
    C = min(RWKV_CHUNK, R)
    nch = R // C
    N = RWKV_HEAD
    W = RWKV_WIDTH
    HP = 2 * N
    n_pairs = W // HP

    @pl.when(pl.program_id(1) == 0)
    def _():
        state_ref[...] = jnp.zeros_like(state_ref)
        carry_ref[...] = jnp.zeros_like(carry_ref)

    zr, zk, zv = zr_ref[...], zk_ref[...], zv_ref[...]
    last = V7X_SUBLANES - 1
    r = zr + (_shift_rows(zr, carry_ref[last:last + 1, 0:W]) - zr) * mu_ref[0:1, :]
    k = zk + (_shift_rows(zk, carry_ref[last:last + 1, W:2 * W]) - zk) * mu_ref[1:2, :]
    v = zv + (_shift_rows(zv, carry_ref[last:last + 1, 2 * W:3 * W]) - zv) * mu_ref[2:3, :]
    carry_ref[:, 0:W] = zr[R - V7X_SUBLANES:R, :]
    carry_ref[:, W:2 * W] = zk[R - V7X_SUBLANES:R, :]
    carry_ref[:, 2 * W:3 * W] = zv[R - V7X_SUBLANES:R, :]

    lw = -math.exp(-0.5) * jax.nn.sigmoid(w0_ref[...] + wp_ref[...])
    a = jax.nn.sigmoid(a0_ref[...] + ap_ref[...])
    kk = k * kk_ref[...]
    kk = kk * lax.rsqrt(jnp.maximum(_dot_exact_rhs(kk * kk, seg_ref[...]), 1e-24))
    k2 = k * (1.0 + (a - 1.0) * ka_ref[...])
    a_s = -kk
    b_s = kk * a

    rr = lax.broadcasted_iota(jnp.int32, (R, R), 0)
    cc = lax.broadcasted_iota(jnp.int32, (R, R), 1)
    same_chunk = (rr // C) == (cc // C)
    L = _dot_exact_lhs((same_chunk & (cc <= rr)).astype(BF16), lw)
    bonus_w = r * k2 * rk_ref[...]

    P2 = 2 * C
    sr = lax.broadcasted_iota(jnp.int32, (P2, HP), 0)
    sc = lax.broadcasted_iota(jnp.int32, (P2, HP), 1)
    stack_mask = (sr < C) == (sc < N)
    br = lax.broadcasted_iota(jnp.int32, (P2, P2), 0)
    bc = lax.broadcasted_iota(jnp.int32, (P2, P2), 1)
    same_head = (br < C) == (bc < C)
    tr = jnp.where(br >= C, br - C, br)
    tc = jnp.where(bc >= C, bc - C, bc)
    strict = same_head & (tc < tr)
    lower = same_head & (tc <= tr)
    eye_p = (br == bc).astype(F32)
    kr = lax.broadcasted_iota(jnp.int32, (HP, HP), 0)
    kc = lax.broadcasted_iota(jnp.int32, (HP, HP), 1)
    eye_k = kr == kc

    def dup(x):
        return jnp.concatenate([x, x], axis=0)

    def stack(x):
        return jnp.where(stack_mask, dup(x), 0.0)

    n_sq = int(math.log2(C)) - 1
    items = [(c, p) for c in range(nch) for p in range(n_pairs)]

    def prep(c, p):
        rows = slice(c * C, (c + 1) * C)
        lanes = slice(p * HP, (p + 1) * HP)
        Lc = L[rows, lanes]
        lwc = lw[rows, lanes]
        LC = Lc[C - 1:C, :]
        enL = jnp.exp(-Lc)
        eCL = jnp.exp(LC - Lc)
        b_c, k_c = b_s[rows, lanes], k2[rows, lanes]
        return dict(
            xa=stack(a_s[rows, lanes] * jnp.exp(Lc - lwc)), xr=stack(r[rows, lanes] * jnp.exp(Lc)),
            bt=dup(b_c * enL), kt=dup(k_c * enL), bh=stack(b_c * eCL), kh=stack(k_c * eCL),
            vs=stack(v[rows, lanes]), pc=jnp.exp(LC))

    d = [prep(c, p) for c, p in items]
    aa = [_dot_nt(jnp.concatenate([e["xa"], e["xr"]], axis=0),
                  jnp.concatenate([e["bt"], e["kt"]], axis=0)) for e in d]
    a_ab = [jnp.where(strict, m[:P2, :P2], 0.0) for m in aa]
    a_ak = [jnp.where(strict, m[:P2, P2:], 0.0) for m in aa]
    a_rb = [jnp.where(lower, m[P2:, :P2], 0.0) for m in aa]
    a_rk = [jnp.where(lower, m[P2:, P2:], 0.0) for m in aa]
    tm = [eye_p + m for m in a_ab]
    npw = a_ab
    for _ in range(n_sq):
        npw = [_dot(m, m) for m in npw]
        tm = [t + _dot(t, m) for t, m in zip(tm, npw)]
    av = [_dot(m, e["vs"]) for m, e in zip(a_ak, d)]
    z = [_dot(t, jnp.concatenate([e["xa"], x], axis=1)) for t, e, x in zip(tm, d, av)]
    w = [_dot(m, x) for m, x in zip(a_rb, z)]
    rkv = [_dot(m, e["vs"]) for m, e in zip(a_rk, d)]
    gh = [_dot_tn(e["bh"], x) for e, x in zip(d, z)]
    khv = [_dot_tn(e["kh"], e["vs"]) for e in d]

    for idx, (c, p) in enumerate(items):
        e = d[idx]
        rows = slice(c * C, (c + 1) * C)
        lanes = slice(p * HP, (p + 1) * HP)
        rp = e["xr"] + w[idx][:, :HP]
        y0 = w[idx][:, HP:] + rkv[idx]
        gm = gh[idx][:, :HP]
        hm = gh[idx][:, HP:] + khv[idx]
        st = state_ref[p]
        yg = _dot(jnp.concatenate([rp, gm], axis=0), st)
        pc_col = jnp.sum(jnp.where(eye_k, e["pc"], 0.0), axis=1, keepdims=True)
        state_ref[p] = pc_col * st + yg[P2:] + hm
        ys = yg[:P2] + y0
        mu = jnp.sum(ys, axis=-1, keepdims=True) * (1.0 / N)
        yc = jnp.where(stack_mask, ys - mu, 0.0)
        var = jnp.sum(yc * yc, axis=-1, keepdims=True) * (1.0 / N)
        yn = yc * lax.rsqrt(var + GN_EPS)
        bonus = jnp.sum(stack(bonus_w[rows, lanes]), axis=-1, keepdims=True) * e["vs"]
        yn = yn[:C] + yn[C:]
        bonus = bonus[:C] + bonus[C:]
        o_ref[rows, lanes] = ((yn * gnw_ref[:, lanes] + gnb_ref[:, lanes] + bonus)
                              * gg_ref[rows, lanes]).astype(o_ref.dtype)


def _rwkv(zr, zk, zv, wp, ap, gg, mu_rkv, w0, a0, k_k, k_a, r_k, gn_w, gn_b, seg, B, S):
    T = zr.shape[0]
    C = min(RWKV_ROWS, S)
    nc = S // C
    W = RWKV_WIDTH
    HP = 2 * RWKV_HEAD
    row = lambda b, i: (b * nc + i, 0)
    full = lambda b, i: (0, 0)

    def fs(a):
        return pl.BlockSpec(a.shape, full)

    return pl.pallas_call(
        _rwkv_kernel,
        grid=(B, nc),
        in_specs=[pl.BlockSpec((C, W), row)] * 6 + [fs(mu_rkv), fs(w0), fs(a0), fs(k_k), fs(k_a),
                                                    fs(r_k), fs(gn_w), fs(gn_b), fs(seg)],
        out_specs=pl.BlockSpec((C, W), row),
        out_shape=jax.ShapeDtypeStruct((T, W), BF16),
        scratch_shapes=[pltpu.VMEM((W // HP, HP, HP), F32),
                        pltpu.VMEM((V7X_SUBLANES, 3 * W), F32)],
        compiler_params=_cparams(("arbitrary", "arbitrary")),
        name="rwkv7",
    )(zr, zk, zv, wp, ap, gg, mu_rkv, w0, a0, k_k, k_a, r_k, gn_w, gn_b, seg)


ROUTE_COLS = 8
ROUTER_GROUP_LANE = 0
ROUTER_EXPERT_LANE = N_GROUPS


def _out_router_kernel(x_ref, od_ref, orw_ref, wo_ref, mg_ref, wrh_ref, wrl_ref,
                       x1_out, hm_out, route_out, cnt_out):
    tm = PROJ_ROWS if x_ref.shape[0] >= PROJ_ROWS else x_ref.shape[0]
    tiles = [slice(t * tm, (t + 1) * tm) for t in range(x_ref.shape[0] // tm)]
    x1 = [x_ref[r, :] + jnp.dot(od_ref[r, :], wo_ref[0:DIFF_WIDTH, :], preferred_element_type=F32)
          + jnp.dot(orw_ref[r, :], wo_ref[DIFF_WIDTH:, :], preferred_element_type=F32) for r in tiles]
    hm = [_rms(v, mg_ref[...], NORM_EPS) for v in x1]
    for r, v, h in zip(tiles, x1, hm):
        x1_out[r, :] = v
        hm_out[r, :] = h.astype(hm_out.dtype)

    def logits(h):
        hi = h.astype(BF16)
        lo = (h - hi.astype(F32)).astype(BF16)
        return (jnp.dot(hi, wrh_ref[...], preferred_element_type=F32)
                + jnp.dot(hi, wrl_ref[...], preferred_element_type=F32)
                + jnp.dot(lo, wrh_ref[...], preferred_element_type=F32))

    lgs = [logits(h) for h in hm]
    lane = lax.broadcasted_iota(jnp.int32, (tm, V7X_LANES), 1).astype(F32)
    big = float(V7X_LANES)
    ninf = -jnp.inf
    gmask = lane < N_GROUPS

    def choose(lg):
        gmax = jnp.max(jnp.where(gmask, lg, ninf), axis=-1, keepdims=True)
        g_sel = jnp.min(jnp.where(gmask & (lg == gmax), lane, big), axis=-1, keepdims=True)
        g_w = 1.0 / jnp.sum(jnp.where(gmask, jnp.exp(lg - gmax), 0.0), axis=-1, keepdims=True)
        lo_lane = ROUTER_EXPERT_LANE + g_sel * EXPERTS_PER_GROUP
        emask = (lane >= lo_lane) & (lane < lo_lane + EXPERTS_PER_GROUP)
        v1 = jnp.max(jnp.where(emask, lg, ninf), axis=-1, keepdims=True)
        i1 = jnp.min(jnp.where(emask & (lg == v1), lane, big), axis=-1, keepdims=True)
        emask2 = emask & (lane != i1)
        v2 = jnp.max(jnp.where(emask2, lg, ninf), axis=-1, keepdims=True)
        i2 = jnp.min(jnp.where(emask2 & (lg == v2), lane, big), axis=-1, keepdims=True)
        e2x = jnp.exp(v2 - v1)
        den = 1.0 + e2x
        return i1, i2, (1.0 / den) * g_w, (e2x / den) * g_w

    chosen = [choose(lg) for lg in lgs]

    ri = lax.broadcasted_iota(jnp.int32, (tm, tm), 0)
    ci = lax.broadcasted_iota(jnp.int32, (tm, tm), 1)
    before = (ci < ri).astype(BF16)
    ohs = [((lane == i1) | (lane == i2)).astype(F32) for i1, i2, _, _ in chosen]
    prefixes = [jnp.dot(before, oh.astype(BF16), preferred_element_type=F32) for oh in ohs]
    col = lax.broadcasted_iota(jnp.int32, (tm, ROUTE_COLS), 1)
    for t, (r, (i1, i2, wt1, wt2), oh, prefix) in enumerate(zip(tiles, chosen, ohs, prefixes)):
        rank1 = jnp.sum(jnp.where(lane == i1, prefix, 0.0), axis=-1, keepdims=True)
        rank2 = jnp.sum(jnp.where(lane == i2, prefix, 0.0), axis=-1, keepdims=True)
        cnt_out[t * V7X_SUBLANES:(t + 1) * V7X_SUBLANES, :] = jnp.broadcast_to(
            jnp.sum(oh, axis=0, keepdims=True), (V7X_SUBLANES, V7X_LANES))
        e1 = i1 - ROUTER_EXPERT_LANE
        e2 = i2 - ROUTER_EXPERT_LANE
        route_out[r, :] = jnp.where(col == 0, e1, jnp.where(col == 1, e2, jnp.where(col == 2, wt1, jnp.where(
            col == 3, wt2, jnp.where(col == 4, rank1, jnp.where(col == 5, rank2, 0.0))))))


def _out_router(x2, od, orw, w_out, moe_g, wr_hi, wr_lo):
    T, D = x2.shape
    tm = min(PROJ_ROWS, T)
    nt = T // tm
    tps = TILES_PER_STEP if nt % TILES_PER_STEP == 0 else 1
    tr = tm * tps
    row = lambda i: (i, 0)
    full = lambda i: (0, 0)

    def fs(a):
        return pl.BlockSpec(a.shape, full)

    return pl.pallas_call(
        _out_router_kernel,
        grid=(nt // tps,),
        in_specs=[pl.BlockSpec((tr, D), row), pl.BlockSpec((tr, DIFF_WIDTH), row),
                  pl.BlockSpec((tr, RWKV_WIDTH), row), fs(w_out), fs(moe_g), fs(wr_hi), fs(wr_lo)],
        out_specs=[pl.BlockSpec((tr, D), row), pl.BlockSpec((tr, D), row),
                   pl.BlockSpec((tr, ROUTE_COLS), row), pl.BlockSpec((tps * V7X_SUBLANES, V7X_LANES), row)],
        out_shape=[jax.ShapeDtypeStruct((T, D), F32), jax.ShapeDtypeStruct((T, D), BF16),
                   jax.ShapeDtypeStruct((T, ROUTE_COLS), F32),
                   jax.ShapeDtypeStruct((nt * V7X_SUBLANES, V7X_LANES), F32)],
        compiler_params=_cparams(("arbitrary",)),
        name="out_router",
    )(x2, od, orw, w_out, moe_g, wr_hi, wr_lo)


RUN_ALIGN = V7X_SUBLANES
TAB_DST, TAB_SRC, TAB_LEN, TAB_NUSED = 0, N_EXPERTS, 2 * N_EXPERTS, 3 * N_EXPERTS


def _stage_rows(tm):
    need = 2 * tm + N_EXPERTS * (RUN_ALIGN - 1)
    return -(-need // V7X_LANES) * V7X_LANES


def _run_copies(tab_ref, tile, max_len, make_copy, op):
    n_bits = int(math.log2(max_len // RUN_ALIGN)) + 1

    def per_expert(e, c):
        n = tab_ref[tile, 0, TAB_LEN + e]
        dst = tab_ref[tile, 0, TAB_DST + e]
        src = tab_ref[tile, 0, TAB_SRC + e]
        off = jnp.int32(0)
        for bit in reversed(range(n_bits)):
            size = RUN_ALIGN << bit
            hit = (n & size) != 0

            @pl.when(hit)
            def _(off=off, size=size):
                cp = make_copy(pl.multiple_of(src + off, RUN_ALIGN), pl.multiple_of(dst + off, RUN_ALIGN), size)
                getattr(cp, op)()

            off = off + jnp.where(hit, size, 0)
        return c

    lax.fori_loop(0, N_EXPERTS, per_expert, 0)


def _dispatch_kernel(tab_ref, tail_ref, sl_ref, hm_ref, xb_ref, stage_ref, zero_ref, sem, zsem):
    tps, ns = stage_ref.shape[0], stage_ref.shape[1]
    tm = hm_ref.shape[0] // tps
    bm = zero_ref.shape[0]
    nb = xb_ref.shape[0] // bm
    n_used = tail_ref[0, 0, TAB_NUSED]

    def tile_copy(t):
        def make_copy(src, dst, size):
            return pltpu.make_async_copy(stage_ref.at[t, pl.ds(src, size)], xb_ref.at[pl.ds(dst, size)], sem)
        return make_copy

    def zero_copy(src, dst, size):
        return pltpu.make_async_copy(zero_ref.at[pl.ds(src, size)], xb_ref.at[pl.ds(dst, size)], zsem)

    def zero_blocks(op):
        def one(b, c):
            getattr(zero_copy(0, pl.multiple_of(b * bm, bm), bm), op)()
            return c
        lax.fori_loop(n_used, nb, one, 0)

    @pl.when(pl.program_id(0) == 0)
    def _():
        zero_ref[...] = jnp.zeros_like(zero_ref)
        _run_copies(tail_ref, 0, bm, zero_copy, "start")
        zero_blocks("start")

    srow = lax.broadcasted_iota(jnp.int32, (ns, tm), 0)
    for t in range(tps):
        sel = (srow == sl_ref[t, 0:1, :]) | (srow == sl_ref[t, 1:2, :])
        stage_ref[t] = _pack16(jnp.dot(sel.astype(BF16), hm_ref[t * tm:(t + 1) * tm, :],
                                       preferred_element_type=F32))
        _run_copies(tab_ref, t, tm, tile_copy(t), "start")
    for t in range(tps):
        _run_copies(tab_ref, t, tm, tile_copy(t), "wait")

    @pl.when(pl.program_id(0) == pl.num_programs(0) - 1)
    def _():
        _run_copies(tail_ref, 0, bm, zero_copy, "wait")
        zero_blocks("wait")


def _dispatch(tab, tail, sl_rows, hm, P):
    T, D = hm.shape
    tm = min(PROJ_ROWS, T)
    nt = T // tm
    tps = TILES_PER_STEP if nt % TILES_PER_STEP == 0 else 1
    return pl.pallas_call(
        _dispatch_kernel,
        grid=(nt // tps,),
        in_specs=[pl.BlockSpec((tps, 1, tab.shape[-1]), lambda i: (i, 0, 0), memory_space=pltpu.SMEM),
                  pl.BlockSpec((1, 1, tail.shape[-1]), lambda i: (0, 0, 0), memory_space=pltpu.SMEM),
                  pl.BlockSpec((tps, 2, tm), lambda i: (i, 0, 0)),
                  pl.BlockSpec((tps * tm, D), lambda i: (i, 0))],
        out_specs=pl.BlockSpec(memory_space=pl.ANY),
        out_shape=jax.ShapeDtypeStruct((P, D // 2), PACKED),
        scratch_shapes=[pltpu.VMEM((tps, _stage_rows(tm), D // 2), PACKED),
                        pltpu.VMEM((MOE_ROWS, D // 2), PACKED),
                        pltpu.SemaphoreType.DMA(()), pltpu.SemaphoreType.DMA(())],
        compiler_params=_cparams(("arbitrary",)),
        name="moe_dispatch",
    )(tab, tail, sl_rows, hm)


def _expert_kernel(be_ref, nu_ref, xb_ref, wg_ref, wu_ref, wd_ref, yb_ref, wgb, wub, wdb):
    i = pl.program_id(0)
    prev = be_ref[jnp.maximum(i - 1, 0)]
    changed = (i == 0) | (be_ref[i] != prev)

    @pl.when(changed)
    def _():
        wgb[...] = wg_ref[0].astype(BF16)
        wub[...] = wu_ref[0].astype(BF16)
        wdb[...] = wd_ref[0].astype(BF16)

    @pl.when(i < nu_ref[0])
    def _():
        xb = _unpack16(xb_ref[...]).astype(BF16)
        gate = jnp.dot(xb, wgb[...], preferred_element_type=F32)
        up = jnp.dot(xb, wub[...], preferred_element_type=F32)
        hdn = (gate * jax.nn.sigmoid(gate)) * up
        y = jnp.dot(hdn.astype(BF16), wdb[...], preferred_element_type=F32)
        yb_ref[...] = _pack16(y.astype(BF16).astype(F32))

    @pl.when(i >= nu_ref[0])
    def _():
        yb_ref[...] = jnp.zeros_like(yb_ref)


def _experts(block_e, n_used, xb, w_gate, w_up, w_down):
    P, DP = xb.shape
    D = 2 * DP
    bm = MOE_ROWS
    nb = P // bm
    E = D_EXPERT
    grid_spec = pltpu.PrefetchScalarGridSpec(
        num_scalar_prefetch=2,
        grid=(nb,),
        in_specs=[pl.BlockSpec((bm, DP), lambda i, be, nu: (jnp.minimum(i, nu[0] - 1), 0)),
                  pl.BlockSpec((1, D, E), lambda i, be, nu: (be[i], 0, 0)),
                  pl.BlockSpec((1, D, E), lambda i, be, nu: (be[i], 0, 0)),
                  pl.BlockSpec((1, E, D), lambda i, be, nu: (be[i], 0, 0))],
        out_specs=pl.BlockSpec((bm, DP), lambda i, be, nu: (jnp.where(i < nu[0], i, nb - 1), 0)),
        scratch_shapes=[pltpu.VMEM((D, E), BF16), pltpu.VMEM((D, E), BF16), pltpu.VMEM((E, D), BF16)],
    )
    return pl.pallas_call(
        _expert_kernel,
        grid_spec=grid_spec,
        out_shape=jax.ShapeDtypeStruct((P, DP), PACKED),
        input_output_aliases={2: 0},
        compiler_params=_cparams(("arbitrary",)),
        name="moe_experts",
    )(block_e, n_used, xb, w_gate, w_up, w_down)


def _final_kernel(tab_ref, x1_ref, route_ref, p_ref, wpl_ref, plg_ref, pgg_ref, wgate_ref, yb_ref,
                  o_ref, stage_ref, sem):
    tps, ns = stage_ref.shape[0], stage_ref.shape[1]
    tm = x1_ref.shape[0] // tps
    tiles = [slice(t * tm, (t + 1) * tm) for t in range(tps)]
    stage_ref[...] = jnp.zeros_like(stage_ref)

    def tile_copy(t):
        def make_copy(src, dst, size):
            return pltpu.make_async_copy(yb_ref.at[pl.ds(dst, size)], stage_ref.at[t, pl.ds(src, size)], sem)
        return make_copy

    for t in range(tps):
        _run_copies(tab_ref, t, tm, tile_copy(t), "start")
    pe = [_rms(jnp.dot(p_ref[r, :].astype(BF16), wpl_ref[...], preferred_element_type=F32),
               plg_ref[...], NORM_EPS) for r in tiles]
    for t in range(tps):
        _run_copies(tab_ref, t, tm, tile_copy(t), "wait")

    scol = lax.broadcasted_iota(jnp.int32, (tm, ns), 1).astype(F32)

    def selection(r):
        route = route_ref[r, :]
        wsel = (jnp.where(scol == route[:, 0:1], route[:, 2:3], 0.0)
                + jnp.where(scol == route[:, 1:2], route[:, 3:4], 0.0))
        w_hi = wsel.astype(BF16)
        return w_hi, (wsel - w_hi.astype(F32)).astype(BF16)

    wsel = [selection(r) for r in tiles]
    yb16 = [_unpack16(stage_ref[t]).astype(BF16) for t in range(tps)]
    moe = [jnp.dot(w_hi, y, preferred_element_type=F32) + jnp.dot(w_lo, y, preferred_element_type=F32)
           for (w_hi, w_lo), y in zip(wsel, yb16)]
    x2 = [x1_ref[r, :] + m for r, m in zip(tiles, moe)]
    gate = [jax.nn.sigmoid(_dot(_rms(v, pgg_ref[...], NORM_EPS), wgate_ref[...])) for v in x2]
    for r, v, e, g in zip(tiles, x2, pe, gate):
        o_ref[r, :] = v + e * g


def _final(tab, x1, route, p2, w_pl, pl_g, pl_gate_g, w_gate, yb):
    T, D = x1.shape
    tm = min(PROJ_ROWS, T)
    nt = T // tm
    tps = TILES_PER_STEP if nt % TILES_PER_STEP == 0 else 1
    tr = tps * tm
    row = lambda i: (i, 0)
    full = lambda i: (0, 0)

    def fs(a):
        return pl.BlockSpec(a.shape, full)

    return pl.pallas_call(
        _final_kernel,
        grid=(nt // tps,),
        in_specs=[pl.BlockSpec((tps, 1, tab.shape[-1]), lambda i: (i, 0, 0), memory_space=pltpu.SMEM),
                  pl.BlockSpec((tr, D), row), pl.BlockSpec((tr, ROUTE_COLS), row),
                  pl.BlockSpec((tr, PL_DIM), row), fs(w_pl), fs(pl_g), fs(pl_gate_g), fs(w_gate),
                  pl.BlockSpec(memory_space=pl.ANY)],
        out_specs=pl.BlockSpec((tr, D), row),
        out_shape=jax.ShapeDtypeStruct((T, D), F32),
        scratch_shapes=[pltpu.VMEM((tps, _stage_rows(tm), D // 2), PACKED), pltpu.SemaphoreType.DMA(())],
        compiler_params=_cparams(("arbitrary",)),
        name="combine_final",
    )(tab, x1, route, p2, w_pl, pl_g, pl_gate_g, w_gate, yb)


def kernel(x, p, positions, attn_norm_g, w_in, q_norm_g, k_norm_g, lambda_q1, lambda_k1, lambda_q2, lambda_k2, diff_out_g, mu_rkv, mu_wag, w0, w_lora_a, w_lora_b, a0, a_lora_a, a_lora_b, g_lora_a, g_lora_b, k_k, k_a, r_k, gn_w, gn_b, w_out, moe_norm_g, w_group, w_expert_router, w_gate, w_up, w_down, w_pl, pl_norm_g, pl_gate_norm_g, w_pl_gate):
    B, S, D = x.shape
    T = B * S
    assert p.shape[0] == 1, "one layer"
    x2 = x.reshape(T, D)
    p2 = p[0].reshape(T, PL_DIM)
    pos = positions.astype(F32).reshape(T, 1)

    def row(a):
        return a.reshape(1, -1).astype(F32)

    half = DIFF_HEAD_DIM // 2
    inv_freq = ROPE_THETA ** (-jnp.arange(half, dtype=F32) / half)
    invf = jnp.tile(inv_freq, V7X_LANES // half).reshape(1, V7X_LANES)
    li = jnp.arange(DIFF_WIDTH) // DIFF_HEAD_DIM
    seg = (li[:, None] == li[None, :]).astype(BF16)
    reps = DIFF_WIDTH // DIFF_HEAD_DIM

    q, k, v, zr, zk, zv, wp, ap, gg = _in_proj(
        x2, pos, row(attn_norm_g[0]), w_in[0].astype(BF16),
        w_lora_a[0].astype(BF16), a_lora_a[0].astype(BF16), g_lora_a[0].astype(BF16),
        w_lora_b[0].astype(BF16), a_lora_b[0].astype(BF16), g_lora_b[0].astype(BF16),
        mu_wag[0].astype(F32), row(jnp.tile(q_norm_g[0], reps)), row(jnp.tile(k_norm_g[0], reps)),
        invf, seg, B, S)

    o_diff = _attention(q, k, v, row(lambda_q1[0]), row(lambda_k1[0]), row(lambda_q2[0]),
                        row(lambda_k2[0]), diff_out_g[0].astype(F32).reshape(-1, 1), B, S)
    o_rwkv = _rwkv(zr, zk, zv, wp, ap, gg, mu_rkv[0].astype(F32), row(w0[0]), row(a0[0]),
                   row(k_k[0]), row(k_a[0]), row(r_k[0]), row(gn_w[0]), row(gn_b[0]), seg, B, S)

    wr = jnp.concatenate([w_group[0], jnp.transpose(w_expert_router[0], (1, 0, 2)).reshape(D, N_EXPERTS)],
                         axis=1).astype(F32)
    wr = jnp.pad(wr, ((0, 0), (0, V7X_LANES - wr.shape[1])))
    wr_hi = wr.astype(BF16)
    wr_lo = (wr - wr_hi.astype(F32)).astype(BF16)
    x1, hm, route, cnt = _out_router(x2, o_diff, o_rwkv, w_out[0].astype(BF16), row(moe_norm_g[0]),
                                     wr_hi, wr_lo)

    bm = MOE_ROWS
    tm = min(PROJ_ROWS, T)
    nt = T // tm
    i32 = jnp.int32
    cnt_te = cnt.reshape(nt, V7X_SUBLANES, V7X_LANES)[:, 0, ROUTER_EXPERT_LANE:ROUTER_EXPERT_LANE + N_EXPERTS]
    len_te = (cnt_te.astype(i32) + RUN_ALIGN - 1) // RUN_ALIGN * RUN_ALIGN
    tot_e = jnp.sum(len_te, axis=0)
    pcounts = (tot_e + bm - 1) // bm * bm
    pends = jnp.cumsum(pcounts)
    pstarts = pends - pcounts
    dst_te = pstarts[None, :] + jnp.cumsum(len_te, axis=0) - len_te
    src_te = jnp.cumsum(len_te, axis=1) - len_te
    tab = jnp.concatenate([dst_te, src_te, len_te, jnp.zeros_like(len_te)], axis=1).reshape(nt, 1, 4 * N_EXPERTS)
    nb = -(-(2 * T + nt * N_EXPERTS * (RUN_ALIGN - 1)) // bm) + N_EXPERTS
    P = nb * bm
    n_used = (pends[-1] // bm).astype(i32).reshape(1)
    block_start = jnp.arange(nb, dtype=i32) * bm
    block_e = jnp.minimum(jnp.sum((pends[None, :] <= block_start[:, None]).astype(i32), axis=1),
                          N_EXPERTS - 1)
    eid = route[:, 0:2].astype(i32).reshape(nt, tm, 2)
    run_src = jnp.sum(jnp.where(eid[..., None] == jnp.arange(N_EXPERTS, dtype=i32), src_te[:, None, None, :], 0),
                      axis=-1)
    sl = run_src + route[:, 4:6].astype(i32).reshape(nt, tm, 2)
    sl_rows = jnp.transpose(sl, (0, 2, 1))
    route2 = jnp.concatenate([sl.reshape(T, 2).astype(F32), route[:, 2:4],
                              jnp.zeros((T, ROUTE_COLS - 4), F32)], axis=1)

    zero_e = jnp.zeros_like(tot_e)
    tail = jnp.concatenate([pstarts + tot_e, zero_e, pcounts - tot_e,
                            jnp.broadcast_to(n_used, (N_EXPERTS,))]).reshape(1, 1, 4 * N_EXPERTS)
    xb = _dispatch(tab, tail, sl_rows, hm, P)
    yb = _experts(block_e, n_used, xb, w_gate[0], w_up[0], w_down[0])
    out = _final(tab, x1, route2, p2, w_pl[0].astype(BF16), row(pl_norm_g[0]), row(pl_gate_norm_g[0]),
                 w_pl_gate[0].astype(BF16), yb)
    return out.reshape(B, S, D)
```

```python
import functools
import math

import jax
import jax.numpy as jnp
from jax import lax
from jax.experimental import pallas as pl
from jax.experimental.pallas import tpu as pltpu

F32 = jnp.float32
BF16 = jnp.bfloat16

D_MODEL = 1024
PL_DIM = 256
DIFF_WIDTH = 512
RWKV_WIDTH = 512
DIFF_HEAD_DIM = 64
DIFF_HEADS = 4
RWKV_HEAD = 64
RWKV_HEADS = 8
D_DECAY_LORA = 64
D_AAA_LORA = 64
D_GATE_LORA = 160
ROPE_THETA = 10000.0
NORM_EPS = 1e-6
SUBLN_EPS = 1e-5
GN_EPS = 64e-5
N_GROUPS = 4
EXPERTS_PER_GROUP = 8
N_EXPERTS = 32
D_EXPERT = 512
LAM_INIT = 0.8 - 0.6 * math.exp(0.0)

V7X_LANES = 128
V7X_SUBLANES = 8
V7X_VMEM_BYTES = 64 * 1024 * 1024

IN_PROJ_ROWS = 256
PROJ_ROWS = 256
TILES_PER_STEP = 4
ATTN_ROWS = 256
ATTN_UNROLL = 4
ATTN_HEADS_PER_STEP = 2
RWKV_CHUNK = 64
RWKV_ROWS = 256
MOE_ROWS = 512
VMEM_LIMIT = 56 * 1024 * 1024


def _cparams(sem):
    return pltpu.CompilerParams(dimension_semantics=sem, vmem_limit_bytes=VMEM_LIMIT)


def _dot(a, b):
    return jnp.dot(a.astype(BF16), b.astype(BF16), preferred_element_type=F32)


def _dot_nt(a, b):
    return lax.dot_general(a.astype(BF16), b.astype(BF16), (((1,), (1,)), ((), ())),
                           preferred_element_type=F32)


def _dot_tn(a, b):
    return lax.dot_general(a.astype(BF16), b.astype(BF16), (((0,), (0,)), ((), ())),
                           preferred_element_type=F32)


def _split3(x):
    hi = x.astype(BF16)
    r1 = x - hi.astype(F32)
    mid = r1.astype(BF16)
    lo = (r1 - mid.astype(F32)).astype(BF16)
    return hi, mid, lo


def _dot_exact_rhs(x, m01):
    hi = x.astype(BF16)
    mid = (x - hi.astype(F32)).astype(BF16)
    return (jnp.dot(hi, m01, preferred_element_type=F32)
            + jnp.dot(mid, m01, preferred_element_type=F32))


def _dot_exact_lhs(m01, x):
    hi, mid, lo = _split3(x)
    return (jnp.dot(m01, hi, preferred_element_type=F32)
            + jnp.dot(m01, mid, preferred_element_type=F32)
            + jnp.dot(m01, lo, preferred_element_type=F32))


PACKED = jnp.uint32


def _pack16(x):
    h = x.shape[1] // 2
    lo = lax.bitcast_convert_type(x[:, :h], PACKED) >> 16
    hi = lax.bitcast_convert_type(x[:, h:], PACKED) & jnp.uint32(0xFFFF0000)
    return hi | lo


def _unpack16(w):
    lo = lax.bitcast_convert_type(w << 16, F32)
    hi = lax.bitcast_convert_type(w & jnp.uint32(0xFFFF0000), F32)
    return jnp.concatenate([lo, hi], axis=1)


def _rms(x, g, eps):
    return x * lax.rsqrt(jnp.mean(x * x, axis=-1, keepdims=True) + eps) * g


def _shift_rows(z, prev_row):
    rolled = pltpu.roll(z, 1, axis=0)
    row = lax.broadcasted_iota(jnp.int32, z.shape, 0)
    return jnp.where(row == 0, prev_row, rolled)


def _in_proj_kernel(x_ref, pos_ref, g_ref, win_ref, w1_ref, a1_ref, g1_ref, w2_ref, a2_ref, g2_ref,
                    mu_ref, qg_ref, kg_ref, invf_ref, seg_ref,
                    q_out, k_out, v_out, zr_out, zk_out, zv_out, wp_out, ap_out, gg_out,
                    carry_ref):
    tm = x_ref.shape[0]

    @pl.when(pl.program_id(1) == 0)
    def _():
        carry_ref[...] = jnp.zeros_like(carry_ref)

    hn = _rms(x_ref[...], g_ref[...], NORM_EPS)
    prev = carry_ref[V7X_SUBLANES - 1:V7X_SUBLANES, :]
    dh = _shift_rows(hn, prev) - hn
    carry_ref[...] = hn[tm - V7X_SUBLANES:tm, :]

    hb = hn.astype(BF16)
    w = DIFF_WIDTH

    def proj(c):
        return jnp.dot(hb, win_ref[:, c * w:(c + 1) * w], preferred_element_type=F32)

    ang = pos_ref[...] * invf_ref[...]
    cos1, sin1 = jnp.cos(ang), jnp.sin(ang)
    cosf = jnp.concatenate([cos1] * (w // V7X_LANES), axis=1)
    sinf = jnp.concatenate([sin1] * (w // V7X_LANES), axis=1)
    lane = lax.broadcasted_iota(jnp.int32, (tm, w), 1)
    first_half = (lane % DIFF_HEAD_DIM) < (DIFF_HEAD_DIM // 2)
    sin_signed = jnp.where(first_half, -sinf, sinf)
    half = DIFF_HEAD_DIM // 2

    def norm_rope(z, gain):
        ss = _dot_exact_rhs(z * z, seg_ref[...])
        zn = z * lax.rsqrt(ss * (1.0 / DIFF_HEAD_DIM) + NORM_EPS) * gain
        partner = jnp.where(first_half, pltpu.roll(zn, w - half, axis=1), pltpu.roll(zn, half, axis=1))
        return zn * cosf + partner * sin_signed

    scale = DIFF_HEAD_DIM ** -0.5 * math.log2(math.e)
    q_out[...] = (norm_rope(proj(0), qg_ref[...]) * scale).astype(q_out.dtype)
    k_out[...] = norm_rope(proj(1), kg_ref[...]).astype(k_out.dtype)
    v_out[...] = proj(2).astype(v_out.dtype)
    zr_out[...] = proj(3)
    zk_out[...] = proj(4)
    zv_out[...] = proj(5)

    xw = hn + dh * mu_ref[0:1, :]
    xa = hn + dh * mu_ref[1:2, :]
    xg = hn + dh * mu_ref[2:3, :]
    wp_out[...] = _dot(jnp.tanh(_dot(xw, w1_ref[...])), w2_ref[...])
    ap_out[...] = _dot(_dot(xa, a1_ref[...]), a2_ref[...])
    gg_out[...] = _dot(jax.nn.sigmoid(_dot(xg, g1_ref[...])), g2_ref[...])


def _in_proj(x2, pos, attn_g, w_in, w1, a1, g1, w2, a2, g2, mu_wag, qg, kg, invf, seg, B, S):
    T, D = x2.shape
    tm = min(IN_PROJ_ROWS, S)
    ns = S // tm
    w = DIFF_WIDTH
    row = lambda b, i: (b * ns + i, 0)
    full = lambda b, i: (0, 0)

    def fs(a):
        return pl.BlockSpec(a.shape, full)

    outs = ([jax.ShapeDtypeStruct((T, w), BF16)] * 3 + [jax.ShapeDtypeStruct((T, w), F32)] * 6)
    return pl.pallas_call(
        _in_proj_kernel,
        grid=(B, ns),
        in_specs=[pl.BlockSpec((tm, D), row), pl.BlockSpec((tm, 1), row), fs(attn_g), fs(w_in),
                  fs(w1), fs(a1), fs(g1), fs(w2), fs(a2), fs(g2), fs(mu_wag), fs(qg), fs(kg),
                  fs(invf), fs(seg)],
        out_specs=[pl.BlockSpec((tm, w), row)] * 9,
        out_shape=outs,
        scratch_shapes=[pltpu.VMEM((V7X_SUBLANES, D), F32)],
        compiler_params=_cparams(("arbitrary", "arbitrary")),
        name="in_proj",
    )(x2, pos, attn_g, w_in, w1, a1, g1, w2, a2, g2, mu_wag, qg, kg, invf, seg)


def _attn_kernel(lq1_ref, lk1_ref, lq2_ref, lk2_ref, ogt_ref, q_ref, k_ref, v_ref, o_ref,
                 vt_ref, s_ref, acc_ref):
    nh, nq, hw, tq = vt_ref.shape
    tk = tq
    heads = range(nh)
    lam = (jnp.exp(jnp.sum(lq1_ref[...] * lk1_ref[...], axis=-1, keepdims=True))
           - jnp.exp(jnp.sum(lq2_ref[...] * lk2_ref[...], axis=-1, keepdims=True)) + LAM_INIT)

    for h in heads:
        for c in range(nq):
            vt_ref[h, c] = (v_ref[c * tk:(c + 1) * tk, h * hw:(h + 1) * hw]
                            .astype(F32).T.astype(vt_ref.dtype))

    lane = lax.broadcasted_iota(jnp.int32, (tq, hw), 1)

    def stacked_queries(t):
        out = []
        for h in heads:
            q = q_ref[t * tq:(t + 1) * tq, h * hw:(h + 1) * hw]
            zero = jnp.zeros_like(q)
            out.append(jnp.concatenate([jnp.where(lane < DIFF_HEAD_DIM, q, zero),
                                        jnp.where(lane >= DIFF_HEAD_DIM, q, zero)], axis=0))
        return out

    def scores(j, qs):
        start = pl.multiple_of(j * tk, tk)
        return [lax.dot_general(k_ref[pl.ds(start, tk), h * hw:(h + 1) * hw], qs[h],
                                (((1,), (1,)), ((), ())), preferred_element_type=F32) for h in heads]

    def colmax(m, st):
        return tuple(jnp.maximum(mh, jnp.max(s, axis=0, keepdims=True)) for mh, s in zip(m, st))

    krow = lax.broadcasted_iota(jnp.int32, (tk, 2 * tq), 0)
    qcol = lax.broadcasted_iota(jnp.int32, (tk, 2 * tq), 1)
    causal = krow <= jnp.where(qcol >= tq, qcol - tq, qcol)
    ninf = tuple(jnp.full((1, 2 * tq), -jnp.inf, F32) for _ in heads)
    zeros = tuple(jnp.zeros((1, 2 * tq), F32) for _ in heads)

    def diagonal(t, qs, m):
        st = [jnp.where(causal, s, -jnp.inf) for s in scores(t, qs)]
        for h in heads:
            s_ref[t % 2, h, t] = st[h]
        return colmax(m, st)

    m_cur = diagonal(0, stacked_queries(0), ninf)
    for t in range(nq):
        slot, nslot = t % 2, (t + 1) % 2
        has_next = t + 1 < nq
        qs_next = stacked_queries(t + 1) if has_next else None
        acc_ref[...] = jnp.zeros_like(acc_ref)

        def body(j, carry, slot=slot, nslot=nslot, has_next=has_next, qs_next=qs_next, m_cur=m_cur):
            l, m_next = carry
            pt = [jnp.exp2(s_ref[slot, h, j] - m_cur[h]) for h in heads]
            pv = [jnp.dot(vt_ref[h, j], pt[h].astype(BF16), preferred_element_type=F32) for h in heads]
            for h in heads:
                acc_ref[h] += pv[h]
            l = tuple(lh + jnp.sum(p, axis=0, keepdims=True) for lh, p in zip(l, pt))
            if has_next:
                st = scores(j, qs_next)
                for h in heads:
                    s_ref[nslot, h, j] = st[h]
                m_next = colmax(m_next, st)
            return l, m_next

        lsum, m_next = lax.fori_loop(0, t + 1, body, (zeros, ninf), unroll=ATTN_UNROLL)
        if has_next:
            m_next = diagonal(t + 1, qs_next, m_next)

        for h in heads:
            acc = acc_ref[h]
            l = lsum[h]
            ot = acc[:, :tq] / l[:, :tq] - lam * (acc[:, tq:] / l[:, tq:])
            ot = ot * lax.rsqrt(jnp.mean(ot * ot, axis=0, keepdims=True) + SUBLN_EPS) * ogt_ref[...]
            o_ref[t * tq:(t + 1) * tq, h * hw:(h + 1) * hw] = (ot * (1.0 - LAM_INIT)).T.astype(o_ref.dtype)
        m_cur = m_next


def _attention(q, k, v, lq1, lk1, lq2, lk2, og, B, S):
    T = q.shape[0]
    tq = min(ATTN_ROWS, S)
    nq = S // tq
    hw = 2 * DIFF_HEAD_DIM
    nh = ATTN_HEADS_PER_STEP
    gw = nh * hw
    small = lambda b, h: (0, 0)
    seq = pl.BlockSpec((S, gw), lambda b, h: (b, h))
    return pl.pallas_call(
        _attn_kernel,
        grid=(B, DIFF_HEADS // nh),
        in_specs=[pl.BlockSpec(lq1.shape, small), pl.BlockSpec(lk1.shape, small),
                  pl.BlockSpec(lq2.shape, small), pl.BlockSpec(lk2.shape, small),
                  pl.BlockSpec(og.shape, small), seq, seq, seq],
        out_specs=seq,
        out_shape=jax.ShapeDtypeStruct((T, DIFF_WIDTH), BF16),
        scratch_shapes=[pltpu.VMEM((nh, nq, hw, tq), BF16), pltpu.VMEM((2, nh, nq, tq, 2 * tq), F32),
                        pltpu.VMEM((nh, hw, 2 * tq), F32)],
        compiler_params=_cparams(("arbitrary", "arbitrary")),
        name="diff_attn",
    )(lq1, lk1, lq2, lk2, og, q, k, v)


def _rwkv_kernel(zr_ref, zk_ref, zv_ref, wp_ref, ap_ref, gg_ref, mu_ref, w0_ref, a0_ref, kk_ref,
                 ka_ref, rk_ref, gnw_ref, gnb_ref, seg_ref, o_ref, state_ref, carry_ref):
    R = zr_ref.shape[0]
    C = min(RWKV_CHUNK, R)
    nch = R // C
    N = RWKV_HEAD
    W = RWKV_WIDTH
    HP = 2 * N
    n_pairs = W // HP

    @pl.when(pl.program_id(1) == 0)
    def _():
        state_ref[...] = jnp.zeros_like(state_ref)
        carry_ref[...] = jnp.zeros_like(carry_ref)

    zr, zk, zv = zr_ref[...], zk_ref[...], zv_ref[...]
    last = V7X_SUBLANES - 1
    r = zr + (_shift_rows(zr, carry_ref[last:last + 1, 0:W]) - zr) * mu_ref[0:1, :]
    k = zk + (_shift_rows(zk, carry_ref[last:last + 1, W:2 * W]) - zk) * mu_ref[1:2, :]
    v = zv + (_shift_rows(zv, carry_ref[last:last + 1, 2 * W:3 * W]) - zv) * mu_ref[2:3, :]
    carry_ref[:, 0:W] = zr[R - V7X_SUBLANES:R, :]
    carry_ref[:, W:2 * W] = zk[R - V7X_SUBLANES:R, :]
    carry_ref[:, 2 * W:3 * W] = zv[R - V7X_SUBLANES:R, :]

    lw = -math.exp(-0.5) * jax.nn.sigmoid(w0_ref[...] + wp_ref[...])
    a = jax.nn.sigmoid(a0_ref[...] + ap_ref[...])
    kk = k * kk_ref[...]
    kk = kk * lax.rsqrt(jnp.maximum(_dot_exact_rhs(kk * kk, seg_ref[...]), 1e-24))
    k2 = k * (1.0 + (a - 1.0) * ka_ref[...])
    a_s = -kk
    b_s = kk * a

    rr = lax.broadcasted_iota(jnp.int32, (R, R), 0)
    cc = lax.broadcasted_iota(jnp.int32, (R, R), 1)
    same_chunk = (rr // C) == (cc // C)
    L = _dot_exact_lhs((same_chunk & (cc <= rr)).astype(BF16), lw)
    bonus_w = r * k2 * rk_ref[...]

    P2 = 2 * C
    sr = lax.broadcasted_iota(jnp.int32, (P2, HP), 0)
    sc = lax.broadcasted_iota(jnp.int32, (P2, HP), 1)
    stack_mask = (sr < C) == (sc < N)
    br = lax.broadcasted_iota(jnp.int32, (P2, P2), 0)
    bc = lax.broadcasted_iota(jnp.int32, (P2, P2), 1)
    same_head = (br < C) == (bc < C)
    tr = jnp.where(br >= C, br - C, br)
    tc = jnp.where(bc >= C, bc - C, bc)
    strict = same_head & (tc < tr)
    lower = same_head & (tc <= tr)
    eye_p = (br == bc).astype(F32)
    kr = lax.broadcasted_iota(jnp.int32, (HP, HP), 0)
    kc = lax.broadcasted_iota(jnp.int32, (HP, HP), 1)
    eye_k = kr == kc

    def dup(x):
        return jnp.concatenate([x, x], axis=0)

    def stack(x):
        return jnp.where(stack_mask, dup(x), 0.0)

    n_sq = int(math.log2(C)) - 1
    items = [(c, p) for c in range(nch) for p in range(n_pairs)]

    def prep(c, p):
        rows = slice(c * C, (c + 1) * C)
        lanes = slice(p * HP, (p + 1) * HP)
        Lc = L[rows, lanes]
        lwc = lw[rows, lanes]
        LC = Lc[C - 1:C, :]
        enL = jnp.exp(-Lc)
        eCL = jnp.exp(LC - Lc)
        b_c, k_c = b_s[rows, lanes], k2[rows, lanes]
        return dict(
            xa=stack(a_s[rows, lanes] * jnp.exp(Lc - lwc)), xr=stack(r[rows, lanes] * jnp.exp(Lc)),
            bt=dup(b_c * enL), kt=dup(k_c * enL), bh=stack(b_c * eCL), kh=stack(k_c * eCL),
            vs=stack(v[rows, lanes]), pc=jnp.exp(LC))

    d = [prep(c, p) for c, p in items]
    aa = [_dot_nt(jnp.concatenate([e["xa"], e["xr"]], axis=0),
                  jnp.concatenate([e["bt"], e["kt"]], axis=0)) for e in d]
    a_ab = [jnp.where(strict, m[:P2, :P2], 0.0) for m in aa]
    a_ak = [jnp.where(strict, m[:P2, P2:], 0.0) for m in aa]
    a_rb = [jnp.where(lower, m[P2:, :P2], 0.0) for m in aa]
    a_rk = [jnp.where(lower, m[P2:, P2:], 0.0) for m in aa]
    tm = [eye_p + m for m in a_ab]
    npw = a_ab
    for _ in range(n_sq):
        npw = [_dot(m, m) for m in npw]
        tm = [t + _dot(t, m) for t, m in zip(tm, npw)]
    av = [_dot(m, e["vs"]) for m, e in zip(a_ak, d)]
    z = [_dot(t, jnp.concatenate([e["xa"], x], axis=1)) for t, e, x in zip(tm, d, av)]
    w = [_dot(m, x) for m, x in zip(a_rb, z)]
    rkv = [_dot(m, e["vs"]) for m, e in zip(a_rk, d)]
    gh = [_dot_tn(e["bh"], x) for e, x in zip(d, z)]
    khv = [_dot_tn(e["kh"], e["vs"]) for e in d]

    for idx, (c, p) in enumerate(items):
        e = d[idx]
        rows = slice(c * C, (c + 1) * C)
        lanes = slice(p * HP, (p + 1) * HP)
        rp = e["xr"] + w[idx][:, :HP]
        y0 = w[idx][:, HP:] + rkv[idx]
        gm = gh[idx][:, :HP]
        hm = gh[idx][:, HP:] + khv[idx]
        st = state_ref[p]
        yg = _dot(jnp.concatenate([rp, gm], axis=0), st)
        pc_col = jnp.sum(jnp.where(eye_k, e["pc"], 0.0), axis=1, keepdims=True)
        state_ref[p] = pc_col * st + yg[P2:] + hm
        ys = yg[:P2] + y0
        mu = jnp.sum(ys, axis=-1, keepdims=True) * (1.0 / N)
        yc = jnp.where(stack_mask, ys - mu, 0.0)
        var = jnp.sum(yc * yc, axis=-1, keepdims=True) * (1.0 / N)
        yn = yc * lax.rsqrt(var + GN_EPS)
        bonus = jnp.sum(stack(bonus_w[rows, lanes]), axis=-1, keepdims=True) * e["vs"]
        yn = yn[:C] + yn[C:]
        bonus = bonus[:C] + bonus[C:]
        o_ref[rows, lanes] = ((yn * gnw_ref[:, lanes] + gnb_ref[:, lanes] + bonus)
                              * gg_ref[rows, lanes]).astype(o_ref.dtype)


def _rwkv(zr, zk, zv, wp, ap, gg, mu_rkv, w0, a0, k_k, k_a, r_k, gn_w, gn_b, seg, B, S):
    T = zr.shape[0]
    C = min(RWKV_ROWS, S)
    nc = S // C
    W = RWKV_WIDTH
    HP = 2 * RWKV_HEAD
    row = lambda b, i: (b * nc + i, 0)
    full = lambda b, i: (0, 0)

    def fs(a):
        return pl.BlockSpec(a.shape, full)

    return pl.pallas_call(
        _rwkv_kernel,
        grid=(B, nc),
        in_specs=[pl.BlockSpec((C, W), row)] * 6 + [fs(mu_rkv), fs(w0), fs(a0), fs(k_k), fs(k_a),
                                                    fs(r_k), fs(gn_w), fs(gn_b), fs(seg)],
        out_specs=pl.BlockSpec((C, W), row),
        out_shape=jax.ShapeDtypeStruct((T, W), BF16),
        scratch_shapes=[pltpu.VMEM((W // HP, HP, HP), F32),
                        pltpu.VMEM((V7X_SUBLANES, 3 * W), F32)],
        compiler_params=_cparams(("arbitrary", "arbitrary")),
        name="rwkv7",
    )(zr, zk, zv, wp, ap, gg, mu_rkv, w0, a0, k_k, k_a, r_k, gn_w, gn_b, seg)


ROUTE_COLS = 8
ROUTER_GROUP_LANE = 0
ROUTER_EXPERT_LANE = N_GROUPS


def _out_router_kernel(x_ref, od_ref, orw_ref, wo_ref, mg_ref, wrh_ref, wrl_ref,
                       x1_out, hm_out, route_out, cnt_out):
    tm = PROJ_ROWS if x_ref.shape[0] >= PROJ_ROWS else x_ref.shape[0]
    tiles = [slice(t * tm, (t + 1) * tm) for t in range(x_ref.shape[0] // tm)]
    x1 = [x_ref[r, :] + jnp.dot(od_ref[r, :], wo_ref[0:DIFF_WIDTH, :], preferred_element_type=F32)
          + jnp.dot(orw_ref[r, :], wo_ref[DIFF_WIDTH:, :], preferred_element_type=F32) for r in tiles]
    hm = [_rms(v, mg_ref[...], NORM_EPS) for v in x1]
    for r, v, h in zip(tiles, x1, hm):
        x1_out[r, :] = v
        hm_out[r, :] = h.astype(hm_out.dtype)

    def logits(h):
        hi = h.astype(BF16)
        lo = (h - hi.astype(F32)).astype(BF16)
        return (jnp.dot(hi, wrh_ref[...], preferred_element_type=F32)
                + jnp.dot(hi, wrl_ref[...], preferred_element_type=F32)
                + jnp.dot(lo, wrh_ref[...], preferred_element_type=F32))

    lgs = [logits(h) for h in hm]
    lane = lax.broadcasted_iota(jnp.int32, (tm, V7X_LANES), 1).astype(F32)
    big = float(V7X_LANES)
    ninf = -jnp.inf
    gmask = lane < N_GROUPS

    def choose(lg):
        gmax = jnp.max(jnp.where(gmask, lg, ninf), axis=-1, keepdims=True)
        g_sel = jnp.min(jnp.where(gmask & (lg == gmax), lane, big), axis=-1, keepdims=True)
        g_w = 1.0 / jnp.sum(jnp.where(gmask, jnp.exp(lg - gmax), 0.0), axis=-1, keepdims=True)
        lo_lane = ROUTER_EXPERT_LANE + g_sel * EXPERTS_PER_GROUP
        emask = (lane >= lo_lane) & (lane < lo_lane + EXPERTS_PER_GROUP)
        v1 = jnp.max(jnp.where(emask, lg, ninf), axis=-1, keepdims=True)
        i1 = jnp.min(jnp.where(emask & (lg == v1), lane, big), axis=-1, keepdims=True)
        emask2 = emask & (lane != i1)
        v2 = jnp.max(jnp.where(emask2, lg, ninf), axis=-1, keepdims=True)
        i2 = jnp.min(jnp.where(emask2 & (lg == v2), lane, big), axis=-1, keepdims=True)
        e2x = jnp.exp(v2 - v1)
        den = 1.0 + e2x
        return i1, i2, (1.0 / den) * g_w, (e2x / den) * g_w

    chosen = [choose(lg) for lg in lgs]

    ri = lax.broadcasted_iota(jnp.int32, (tm, tm), 0)
    ci = lax.broadcasted_iota(jnp.int32, (tm, tm), 1)
    before = (ci < ri).astype(BF16)
    ohs = [((lane == i1) | (lane == i2)).astype(F32) for i1, i2, _, _ in chosen]
    prefixes = [jnp.dot(before, oh.astype(BF16), preferred_element_type=F32) for oh in ohs]
    col = lax.broadcasted_iota(jnp.int32, (tm, ROUTE_COLS), 1)
    for t, (r, (i1, i2, wt1, wt2), oh, prefix) in enumerate(zip(tiles, chosen, ohs, prefixes)):
        rank1 = jnp.sum(jnp.where(lane == i1, prefix, 0.0), axis=-1, keepdims=True)
        rank2 = jnp.sum(jnp.where(lane == i2, prefix, 0.0), axis=-1, keepdims=True)
        cnt_out[t * V7X_SUBLANES:(t + 1) * V7X_SUBLANES, :] = jnp.broadcast_to(
            jnp.sum(oh, axis=0, keepdims=True), (V7X_SUBLANES, V7X_LANES))
        e1 = i1 - ROUTER_EXPERT_LANE
        e2 = i2 - ROUTER_EXPERT_LANE
        route_out[r, :] = jnp.where(col == 0, e1, jnp.where(col == 1, e2, jnp.where(col == 2, wt1, jnp.where(
            col == 3, wt2, jnp.where(col == 4, rank1, jnp.where(col == 5, rank2, 0.0))))))


def _out_router(x2, od, orw, w_out, moe_g, wr_hi, wr_lo):
    T, D = x2.shape
    tm = min(PROJ_ROWS, T)
    nt = T // tm
    tps = TILES_PER_STEP if nt % TILES_PER_STEP == 0 else 1
    tr = tm * tps
    row = lambda i: (i, 0)
    full = lambda i: (0, 0)

    def fs(a):
        return pl.BlockSpec(a.shape, full)

    return pl.pallas_call(
        _out_router_kernel,
        grid=(nt // tps,),
        in_specs=[pl.BlockSpec((tr, D), row), pl.BlockSpec((tr, DIFF_WIDTH), row),
                  pl.BlockSpec((tr, RWKV_WIDTH), row), fs(w_out), fs(moe_g), fs(wr_hi), fs(wr_lo)],
        out_specs=[pl.BlockSpec((tr, D), row), pl.BlockSpec((tr, D), row),
                   pl.BlockSpec((tr, ROUTE_COLS), row), pl.BlockSpec((tps * V7X_SUBLANES, V7X_LANES), row)],
        out_shape=[jax.ShapeDtypeStruct((T, D), F32), jax.ShapeDtypeStruct((T, D), BF16),
                   jax.ShapeDtypeStruct((T, ROUTE_COLS), F32),
                   jax.ShapeDtypeStruct((nt * V7X_SUBLANES, V7X_LANES), F32)],
        compiler_params=_cparams(("arbitrary",)),
        name="out_router",
    )(x2, od, orw, w_out, moe_g, wr_hi, wr_lo)


RUN_ALIGN = V7X_SUBLANES
RUN_CLASSES = 6
TAB_COUNT, TAB_NUSED = 0, RUN_CLASSES
TAB_SRC = V7X_SUBLANES
TAB_DST = TAB_SRC + RUN_CLASSES * N_EXPERTS
TAB_WORDS = 4 * V7X_LANES


def _stage_rows(tm):
    need = 2 * tm + N_EXPERTS * (RUN_ALIGN - 1)
    return -(-need // V7X_LANES) * V7X_LANES


def _run_table(src, dst, length, spare):
    i32 = jnp.int32
    n = src.shape[0]
    cls = jnp.arange(RUN_CLASSES, dtype=i32)[None, :, None]
    ln = length[:, None, :]
    has = ((ln >> 3) >> cls) & 1
    off = ln & ~((2 * RUN_ALIGN << cls) - 1)
    pos = jnp.cumsum(has, axis=-1) - 1
    k = jnp.arange(N_EXPERTS, dtype=i32)
    pick = (has[..., None] == 1) & (pos[..., None] == k)

    def compact(v):
        return jnp.sum(jnp.where(pick, (v[:, None, :] + off)[..., None], 0), axis=2).reshape(n, -1)

    head = jnp.concatenate([jnp.sum(has, axis=-1), jnp.broadcast_to(spare, (n, 1)).astype(i32),
                            jnp.zeros((n, TAB_SRC - RUN_CLASSES - 1), i32)], axis=1)
    tab = jnp.concatenate([head, compact(src), compact(dst)], axis=1)
    return jnp.pad(tab, ((0, 0), (0, TAB_WORDS - tab.shape[1]))).reshape(n, 1, TAB_WORDS)


def _run_copies(tab_ref, tile, make_copy, op):
    for c in range(RUN_CLASSES):
        size = RUN_ALIGN << c

        def one(k, carry, c=c, size=size):
            src = tab_ref[tile, 0, TAB_SRC + c * N_EXPERTS + k]
            dst = tab_ref[tile, 0, TAB_DST + c * N_EXPERTS + k]
            getattr(make_copy(pl.multiple_of(src, RUN_ALIGN), pl.multiple_of(dst, RUN_ALIGN), size), op)()
            return carry

        lax.fori_loop(0, tab_ref[tile, 0, TAB_COUNT + c], one, 0)


def _dispatch_kernel(tab_ref, tail_ref, sl_ref, hm_ref, xb_ref, stage_ref, zero_ref, sem, zsem):
    tps, ns = stage_ref.shape[0], stage_ref.shape[1]
    tm = hm_ref.shape[0] // tps
    bm = zero_ref.shape[0]
    nb = xb_ref.shape[0] // bm
    n_used = tail_ref[0, 0, TAB_NUSED]

    def tile_copy(t):
        def make_copy(src, dst, size):
            return pltpu.make_async_copy(stage_ref.at[t, pl.ds(src, size)], xb_ref.at[pl.ds(dst, size)], sem)
        return make_copy

    def zero_copy(src, dst, size):
        return pltpu.make_async_copy(zero_ref.at[pl.ds(src, size)], xb_ref.at[pl.ds(dst, size)], zsem)

    def zero_blocks(op):
        def one(b, c):
            getattr(zero_copy(0, pl.multiple_of(b * bm, bm), bm), op)()
            return c
        lax.fori_loop(n_used, nb, one, 0)

    @pl.when(pl.program_id(0) == 0)
    def _():
        zero_ref[...] = jnp.zeros_like(zero_ref)
        _run_copies(tail_ref, 0, zero_copy,"start")
        zero_blocks("start")

    srow = lax.broadcasted_iota(jnp.int32, (ns, tm), 0)
    for t in range(tps):
        sel = (srow == sl_ref[t, 0:1, :]) | (srow == sl_ref[t, 1:2, :])
        stage_ref[t] = _pack16(jnp.dot(sel.astype(BF16), hm_ref[t * tm:(t + 1) * tm, :],
                                       preferred_element_type=F32))
        _run_copies(tab_ref, t, tile_copy(t),"start")
    for t in range(tps):
        _run_copies(tab_ref, t, tile_copy(t),"wait")

    @pl.when(pl.program_id(0) == pl.num_programs(0) - 1)
    def _():
        _run_copies(tail_ref, 0, zero_copy,"wait")
        zero_blocks("wait")


def _dispatch(tab, tail, sl_rows, hm, P):
    T, D = hm.shape
    tm = min(PROJ_ROWS, T)
    nt = T // tm
    tps = TILES_PER_STEP if nt % TILES_PER_STEP == 0 else 1
    return pl.pallas_call(
        _dispatch_kernel,
        grid=(nt // tps,),
        in_specs=[pl.BlockSpec((tps, 1, tab.shape[-1]), lambda i: (i, 0, 0), memory_space=pltpu.SMEM),
                  pl.BlockSpec((1, 1, tail.shape[-1]), lambda i: (0, 0, 0), memory_space=pltpu.SMEM),
                  pl.BlockSpec((tps, 2, tm), lambda i: (i, 0, 0)),
                  pl.BlockSpec((tps * tm, D), lambda i: (i, 0))],
        out_specs=pl.BlockSpec(memory_space=pl.ANY),
        out_shape=jax.ShapeDtypeStruct((P, D // 2), PACKED),
        scratch_shapes=[pltpu.VMEM((tps, _stage_rows(tm), D // 2), PACKED),
                        pltpu.VMEM((MOE_ROWS, D // 2), PACKED),
                        pltpu.SemaphoreType.DMA(()), pltpu.SemaphoreType.DMA(())],
        compiler_params=_cparams(("arbitrary",)),
        name="moe_dispatch",
    )(tab, tail, sl_rows, hm)


def _expert_kernel(be_ref, nu_ref, xb_ref, wg_ref, wu_ref, wd_ref, yb_ref, wgb, wub, wdb):
    i = pl.program_id(0)
    prev = be_ref[jnp.maximum(i - 1, 0)]
    changed = (i == 0) | (be_ref[i] != prev)

    @pl.when(changed)
    def _():
        wgb[...] = wg_ref[0].astype(BF16)
        wub[...] = wu_ref[0].astype(BF16)
        wdb[...] = wd_ref[0].astype(BF16)

    @pl.when(i < nu_ref[0])
    def _():
        xb = _unpack16(xb_ref[...]).astype(BF16)
        gate = jnp.dot(xb, wgb[...], preferred_element_type=F32)
        up = jnp.dot(xb, wub[...], preferred_element_type=F32)
        hdn = (gate * jax.nn.sigmoid(gate)) * up
        y = jnp.dot(hdn.astype(BF16), wdb[...], preferred_element_type=F32)
        yb_ref[...] = _pack16(y.astype(BF16).astype(F32))

    @pl.when(i >= nu_ref[0])
    def _():
        yb_ref[...] = jnp.zeros_like(yb_ref)


def _experts(block_e, n_used, xb, w_gate, w_up, w_down):
    P, DP = xb.shape
    D = 2 * DP
    bm = MOE_ROWS
    nb = P // bm
    E = D_EXPERT
    grid_spec = pltpu.PrefetchScalarGridSpec(
        num_scalar_prefetch=2,
        grid=(nb,),
        in_specs=[pl.BlockSpec((bm, DP), lambda i, be, nu: (jnp.minimum(i, nu[0] - 1), 0)),
                  pl.BlockSpec((1, D, E), lambda i, be, nu: (be[i], 0, 0)),
                  pl.BlockSpec((1, D, E), lambda i, be, nu: (be[i], 0, 0)),
                  pl.BlockSpec((1, E, D), lambda i, be, nu: (be[i], 0, 0))],
        out_specs=pl.BlockSpec((bm, DP), lambda i, be, nu: (jnp.where(i < nu[0], i, nb - 1), 0)),
        scratch_shapes=[pltpu.VMEM((D, E), BF16), pltpu.VMEM((D, E), BF16), pltpu.VMEM((E, D), BF16)],
    )
    return pl.pallas_call(
        _expert_kernel,
        grid_spec=grid_spec,
        out_shape=jax.ShapeDtypeStruct((P, DP), PACKED),
        input_output_aliases={2: 0},
        compiler_params=_cparams(("arbitrary",)),
        name="moe_experts",
    )(block_e, n_used, xb, w_gate, w_up, w_down)


def _final_kernel(tab_ref, x1_ref, route_ref, p_ref, wpl_ref, plg_ref, pgg_ref, wgate_ref, yb_ref,
                  o_ref, stage_ref, sem):
    tps, ns = stage_ref.shape[0], stage_ref.shape[1]
    tm = x1_ref.shape[0] // tps
    tiles = [slice(t * tm, (t + 1) * tm) for t in range(tps)]
    stage_ref[...] = jnp.zeros_like(stage_ref)

    def tile_copy(t):
        def make_copy(src, dst, size):
            return pltpu.make_async_copy(yb_ref.at[pl.ds(dst, size)], stage_ref.at[t, pl.ds(src, size)], sem)
        return make_copy

    for t in range(tps):
        _run_copies(tab_ref, t, tile_copy(t),"start")
    pe = [_rms(jnp.dot(p_ref[r, :].astype(BF16), wpl_ref[...], preferred_element_type=F32),
               plg_ref[...], NORM_EPS) for r in tiles]
    for t in range(tps):
        _run_copies(tab_ref, t, tile_copy(t),"wait")

    scol = lax.broadcasted_iota(jnp.int32, (tm, ns), 1).astype(F32)

    def selection(r):
        route = route_ref[r, :]
        wsel = (jnp.where(scol == route[:, 0:1], route[:, 2:3], 0.0)
                + jnp.where(scol == route[:, 1:2], route[:, 3:4], 0.0))
        w_hi = wsel.astype(BF16)
        return w_hi, (wsel - w_hi.astype(F32)).astype(BF16)

    wsel = [selection(r) for r in tiles]
    yb16 = [_unpack16(stage_ref[t]).astype(BF16) for t in range(tps)]
    moe = [jnp.dot(w_hi, y, preferred_element_type=F32) + jnp.dot(w_lo, y, preferred_element_type=F32)
           for (w_hi, w_lo), y in zip(wsel, yb16)]
    x2 = [x1_ref[r, :] + m for r, m in zip(tiles, moe)]
    gate = [jax.nn.sigmoid(_dot(_rms(v, pgg_ref[...], NORM_EPS), wgate_ref[...])) for v in x2]
    for r, v, e, g in zip(tiles, x2, pe, gate):
        o_ref[r, :] = v + e * g


def _final(tab, x1, route, p2, w_pl, pl_g, pl_gate_g, w_gate, yb):
    T, D = x1.shape
    tm = min(PROJ_ROWS, T)
    nt = T // tm
    tps = TILES_PER_STEP if nt % TILES_PER_STEP == 0 else 1
    tr = tps * tm
    row = lambda i: (i, 0)
    full = lambda i: (0, 0)

    def fs(a):
        return pl.BlockSpec(a.shape, full)

    return pl.pallas_call(
        _final_kernel,
        grid=(nt // tps,),
        in_specs=[pl.BlockSpec((tps, 1, tab.shape[-1]), lambda i: (i, 0, 0), memory_space=pltpu.SMEM),
                  pl.BlockSpec((tr, D), row), pl.BlockSpec((tr, ROUTE_COLS), row),
                  pl.BlockSpec((tr, PL_DIM), row), fs(w_pl), fs(pl_g), fs(pl_gate_g), fs(w_gate),
                  pl.BlockSpec(memory_space=pl.ANY)],
        out_specs=pl.BlockSpec((tr, D), row),
        out_shape=jax.ShapeDtypeStruct((T, D), F32),
        scratch_shapes=[pltpu.VMEM((tps, _stage_rows(tm), D // 2), PACKED), pltpu.SemaphoreType.DMA(())],
        compiler_params=_cparams(("arbitrary",)),
        name="combine_final",
    )(tab, x1, route, p2, w_pl, pl_g, pl_gate_g, w_gate, yb)


def kernel(x, p, positions, attn_norm_g, w_in, q_norm_g, k_norm_g, lambda_q1, lambda_k1, lambda_q2, lambda_k2, diff_out_g, mu_rkv, mu_wag, w0, w_lora_a, w_lora_b, a0, a_lora_a, a_lora_b, g_lora_a, g_lora_b, k_k, k_a, r_k, gn_w, gn_b, w_out, moe_norm_g, w_group, w_expert_router, w_gate, w_up, w_down, w_pl, pl_norm_g, pl_gate_norm_g, w_pl_gate):
    B, S, D = x.shape
    T = B * S
    assert p.shape[0] == 1, "one layer"
    x2 = x.reshape(T, D)
    p2 = p[0].reshape(T, PL_DIM)
    pos = positions.astype(F32).reshape(T, 1)

    def row(a):
        return a.reshape(1, -1).astype(F32)

    half = DIFF_HEAD_DIM // 2
    inv_freq = ROPE_THETA ** (-jnp.arange(half, dtype=F32) / half)
    invf = jnp.tile(inv_freq, V7X_LANES // half).reshape(1, V7X_LANES)
    li = jnp.arange(DIFF_WIDTH) // DIFF_HEAD_DIM
    seg = (li[:, None] == li[None, :]).astype(BF16)
    reps = DIFF_WIDTH // DIFF_HEAD_DIM

    q, k, v, zr, zk, zv, wp, ap, gg = _in_proj(
        x2, pos, row(attn_norm_g[0]), w_in[0].astype(BF16),
        w_lora_a[0].astype(BF16), a_lora_a[0].astype(BF16), g_lora_a[0].astype(BF16),
        w_lora_b[0].astype(BF16), a_lora_b[0].astype(BF16), g_lora_b[0].astype(BF16),
        mu_wag[0].astype(F32), row(jnp.tile(q_norm_g[0], reps)), row(jnp.tile(k_norm_g[0], reps)),
        invf, seg, B, S)

    o_diff = _attention(q, k, v, row(lambda_q1[0]), row(lambda_k1[0]), row(lambda_q2[0]),
                        row(lambda_k2[0]), diff_out_g[0].astype(F32).reshape(-1, 1), B, S)
    o_rwkv = _rwkv(zr, zk, zv, wp, ap, gg, mu_rkv[0].astype(F32), row(w0[0]), row(a0[0]),
                   row(k_k[0]), row(k_a[0]), row(r_k[0]), row(gn_w[0]), row(gn_b[0]), seg, B, S)

    wr = jnp.concatenate([w_group[0], jnp.transpose(w_expert_router[0], (1, 0, 2)).reshape(D, N_EXPERTS)],
                         axis=1).astype(F32)
    wr = jnp.pad(wr, ((0, 0), (0, V7X_LANES - wr.shape[1])))
    wr_hi = wr.astype(BF16)
    wr_lo = (wr - wr_hi.astype(F32)).astype(BF16)
    x1, hm, route, cnt = _out_router(x2, o_diff, o_rwkv, w_out[0].astype(BF16), row(moe_norm_g[0]),
                                     wr_hi, wr_lo)

    bm = MOE_ROWS
    tm = min(PROJ_ROWS, T)
    nt = T // tm
    i32 = jnp.int32
    cnt_te = cnt.reshape(nt, V7X_SUBLANES, V7X_LANES)[:, 0, ROUTER_EXPERT_LANE:ROUTER_EXPERT_LANE + N_EXPERTS]
    len_te = (cnt_te.astype(i32) + RUN_ALIGN - 1) // RUN_ALIGN * RUN_ALIGN
    tot_e = jnp.sum(len_te, axis=0)
    pcounts = (tot_e + bm - 1) // bm * bm
    pends = jnp.cumsum(pcounts)
    pstarts = pends - pcounts
    dst_te = pstarts[None, :] + jnp.cumsum(len_te, axis=0) - len_te
    src_te = jnp.cumsum(len_te, axis=1) - len_te
    tab = _run_table(src_te, dst_te, len_te, 0)
    nb = -(-(2 * T + nt * N_EXPERTS * (RUN_ALIGN - 1)) // bm) + N_EXPERTS
    P = nb * bm
    n_used = (pends[-1] // bm).astype(i32).reshape(1)
    block_start = jnp.arange(nb, dtype=i32) * bm
    block_e = jnp.minimum(jnp.sum((pends[None, :] <= block_start[:, None]).astype(i32), axis=1),
                          N_EXPERTS - 1)
    eid = route[:, 0:2].astype(i32).reshape(nt, tm, 2)
    run_src = jnp.sum(jnp.where(eid[..., None] == jnp.arange(N_EXPERTS, dtype=i32), src_te[:, None, None, :], 0),
                      axis=-1)
    sl = run_src + route[:, 4:6].astype(i32).reshape(nt, tm, 2)
    sl_rows = jnp.transpose(sl, (0, 2, 1))
    route2 = jnp.concatenate([sl.reshape(T, 2).astype(F32), route[:, 2:4],
                              jnp.zeros((T, ROUTE_COLS - 4), F32)], axis=1)

    tail = _run_table(jnp.zeros((1, N_EXPERTS), i32), (pstarts + tot_e)[None, :], (pcounts - tot_e)[None, :],
                      n_used)
    xb = _dispatch(tab, tail, sl_rows, hm, P)
    yb = _experts(block_e, n_used, xb, w_gate[0], w_up[0], w_down[0])
    out = _final(tab, x1, route2, p2, w_pl[0].astype(BF16), row(pl_norm_g[0]), row(pl_gate_norm_g[0]),
                 w_pl_gate[0].astype(BF16), yb)
    return out.reshape(B, S, D)
```

```python
import functools
import math

import jax
import jax.numpy as jnp
from jax import lax
from jax.experimental import pallas as pl
from jax.experimental.pallas import tpu as pltpu

F32 = jnp.float32
BF16 = jnp.bfloat16

D_MODEL = 1024
PL_DIM = 256
DIFF_WIDTH = 512
RWKV_WIDTH = 512
DIFF_HEAD_DIM = 64
DIFF_HEADS = 4
RWKV_HEAD = 64
RWKV_HEADS = 8
D_DECAY_LORA = 64
D_AAA_LORA = 64
D_GATE_LORA = 160
ROPE_THETA = 10000.0
NORM_EPS = 1e-6
SUBLN_EPS = 1e-5
GN_EPS = 64e-5
N_GROUPS = 4
EXPERTS_PER_GROUP = 8
N_EXPERTS = 32
D_EXPERT = 512
LAM_INIT = 0.8 - 0.6 * math.exp(0.0)

V7X_LANES = 128
V7X_SUBLANES = 8
V7X_VMEM_BYTES = 64 * 1024 * 1024

IN_PROJ_ROWS = 256
IN_PROJ_TILES_PER_STEP = 2
PROJ_ROWS = 256
TILES_PER_STEP = 4
ATTN_ROWS = 256
ATTN_UNROLL = 4
ATTN_HEADS_PER_STEP = 2
RWKV_CHUNK = 64
RWKV_ROWS = 256
MOE_ROWS = 512
VMEM_LIMIT = 56 * 1024 * 1024


def _cparams(sem):
    return pltpu.CompilerParams(dimension_semantics=sem, vmem_limit_bytes=VMEM_LIMIT)


def _dot(a, b):
    return jnp.dot(a.astype(BF16), b.astype(BF16), preferred_element_type=F32)


def _dot_nt(a, b):
    return lax.dot_general(a.astype(BF16), b.astype(BF16), (((1,), (1,)), ((), ())),
                           preferred_element_type=F32)


def _dot_tn(a, b):
    return lax.dot_general(a.astype(BF16), b.astype(BF16), (((0,), (0,)), ((), ())),
                           preferred_element_type=F32)


def _split3(x):
    hi = x.astype(BF16)
    r1 = x - hi.astype(F32)
    mid = r1.astype(BF16)
    lo = (r1 - mid.astype(F32)).astype(BF16)
    return hi, mid, lo


def _dot_exact_rhs(x, m01):
    hi = x.astype(BF16)
    mid = (x - hi.astype(F32)).astype(BF16)
    return (jnp.dot(hi, m01, preferred_element_type=F32)
            + jnp.dot(mid, m01, preferred_element_type=F32))


def _dot_exact_lhs(m01, x):
    hi, mid, lo = _split3(x)
    return (jnp.dot(m01, hi, preferred_element_type=F32)
            + jnp.dot(m01, mid, preferred_element_type=F32)
            + jnp.dot(m01, lo, preferred_element_type=F32))


PACKED = jnp.uint32


def _pack16(x):
    h = x.shape[1] // 2
    lo = lax.bitcast_convert_type(x[:, :h], PACKED) >> 16
    hi = lax.bitcast_convert_type(x[:, h:], PACKED) & jnp.uint32(0xFFFF0000)
    return hi | lo


def _unpack16(w):
    lo = lax.bitcast_convert_type(w << 16, F32)
    hi = lax.bitcast_convert_type(w & jnp.uint32(0xFFFF0000), F32)
    return jnp.concatenate([lo, hi], axis=1)


def _rms(x, g, eps):
    return x * lax.rsqrt(jnp.mean(x * x, axis=-1, keepdims=True) + eps) * g


def _shift_rows(z, prev_row):
    rolled = pltpu.roll(z, 1, axis=0)
    row = lax.broadcasted_iota(jnp.int32, z.shape, 0)
    return jnp.where(row == 0, prev_row, rolled)


def _in_proj_kernel(x_ref, pos_ref, g_ref, win_ref, w1_ref, a1_ref, g1_ref, w2_ref, a2_ref, g2_ref,
                    mu_ref, qg_ref, kg_ref, invf_ref, seg_ref,
                    q_out, k_out, v_out, zr_out, zk_out, zv_out, wp_out, ap_out, gg_out,
                    carry_ref):
    tm = IN_PROJ_ROWS if x_ref.shape[0] >= IN_PROJ_ROWS else x_ref.shape[0]
    tiles = [slice(t * tm, (t + 1) * tm) for t in range(x_ref.shape[0] // tm)]
    w = DIFF_WIDTH
    half = DIFF_HEAD_DIM // 2

    @pl.when(pl.program_id(1) == 0)
    def _():
        carry_ref[...] = jnp.zeros_like(carry_ref)

    prev = carry_ref[V7X_SUBLANES - 1:V7X_SUBLANES, :]
    hns, dhs = [], []
    for r in tiles:
        hn = _rms(x_ref[r, :], g_ref[...], NORM_EPS)
        dhs.append(_shift_rows(hn, prev) - hn)
        hns.append(hn)
        prev = hn[tm - 1:tm, :]
    carry_ref[...] = hns[-1][tm - V7X_SUBLANES:tm, :]

    zq, zk, low = [], [], []
    for r, hn, dh in zip(tiles, hns, dhs):
        hb = hn.astype(BF16)

        def proj(c, hb=hb):
            return jnp.dot(hb, win_ref[:, c * w:(c + 1) * w], preferred_element_type=F32)

        zq.append(proj(0))
        zk.append(proj(1))
        low.append((_dot(hn + dh * mu_ref[0:1, :], w1_ref[...]), _dot(hn + dh * mu_ref[1:2, :], a1_ref[...]),
                    _dot(hn + dh * mu_ref[2:3, :], g1_ref[...])))
        v_out[r, :] = proj(2).astype(v_out.dtype)
        zr_out[r, :] = proj(3)
        zk_out[r, :] = proj(4)
        zv_out[r, :] = proj(5)

    lane = lax.broadcasted_iota(jnp.int32, (tm, w), 1)
    first_half = (lane % DIFF_HEAD_DIM) < half
    scale = DIFF_HEAD_DIM ** -0.5 * math.log2(math.e)
    for r, q, k, (lw, la, lg) in zip(tiles, zq, zk, low):
        ang = pos_ref[r, :] * invf_ref[...]
        cosf = jnp.concatenate([jnp.cos(ang)] * (w // V7X_LANES), axis=1)
        sinf = jnp.concatenate([jnp.sin(ang)] * (w // V7X_LANES), axis=1)
        sin_signed = jnp.where(first_half, -sinf, sinf)

        def norm_rope(z, gain, cosf=cosf, sin_signed=sin_signed):
            ss = _dot_exact_rhs(z * z, seg_ref[...])
            zn = z * lax.rsqrt(ss * (1.0 / DIFF_HEAD_DIM) + NORM_EPS) * gain
            partner = jnp.where(first_half, pltpu.roll(zn, w - half, axis=1), pltpu.roll(zn, half, axis=1))
            return zn * cosf + partner * sin_signed

        q_out[r, :] = (norm_rope(q, qg_ref[...]) * scale).astype(q_out.dtype)
        k_out[r, :] = norm_rope(k, kg_ref[...]).astype(k_out.dtype)
        wp_out[r, :] = _dot(jnp.tanh(lw), w2_ref[...])
        ap_out[r, :] = _dot(la, a2_ref[...])
        gg_out[r, :] = _dot(jax.nn.sigmoid(lg), g2_ref[...])


def _in_proj(x2, pos, attn_g, w_in, w1, a1, g1, w2, a2, g2, mu_wag, qg, kg, invf, seg, B, S):
    T, D = x2.shape
    tile = min(IN_PROJ_ROWS, S)
    tps = IN_PROJ_TILES_PER_STEP if (S // tile) % IN_PROJ_TILES_PER_STEP == 0 else 1
    tm = tile * tps
    ns = S // tm
    w = DIFF_WIDTH
    row = lambda b, i: (b * ns + i, 0)
    full = lambda b, i: (0, 0)

    def fs(a):
        return pl.BlockSpec(a.shape, full)

    outs = ([jax.ShapeDtypeStruct((T, w), BF16)] * 3 + [jax.ShapeDtypeStruct((T, w), F32)] * 6)
    return pl.pallas_call(
        _in_proj_kernel,
        grid=(B, ns),
        in_specs=[pl.BlockSpec((tm, D), row), pl.BlockSpec((tm, 1), row), fs(attn_g), fs(w_in),
                  fs(w1), fs(a1), fs(g1), fs(w2), fs(a2), fs(g2), fs(mu_wag), fs(qg), fs(kg),
                  fs(invf), fs(seg)],
        out_specs=[pl.BlockSpec((tm, w), row)] * 9,
        out_shape=outs,
        scratch_shapes=[pltpu.VMEM((V7X_SUBLANES, D), F32)],
        compiler_params=_cparams(("arbitrary", "arbitrary")),
        name="in_proj",
    )(x2, pos, attn_g, w_in, w1, a1, g1, w2, a2, g2, mu_wag, qg, kg, invf, seg)


def _attn_kernel(lq1_ref, lk1_ref, lq2_ref, lk2_ref, ogt_ref, q_ref, k_ref, v_ref, o_ref,
                 vt_ref, s_ref, acc_ref):
    nh, nq, hw, tq = vt_ref.shape
    tk = tq
    heads = range(nh)
    lam = (jnp.exp(jnp.sum(lq1_ref[...] * lk1_ref[...], axis=-1, keepdims=True))
           - jnp.exp(jnp.sum(lq2_ref[...] * lk2_ref[...], axis=-1, keepdims=True)) + LAM_INIT)

    for h in heads:
        for c in range(nq):
            vt_ref[h, c] = (v_ref[c * tk:(c + 1) * tk, h * hw:(h + 1) * hw]
                            .astype(F32).T.astype(vt_ref.dtype))

    lane = lax.broadcasted_iota(jnp.int32, (tq, hw), 1)

    def stacked_queries(t):
        out = []
        for h in heads:
            q = q_ref[t * tq:(t + 1) * tq, h * hw:(h + 1) * hw]
            zero = jnp.zeros_like(q)
            out.append(jnp.concatenate([jnp.where(lane < DIFF_HEAD_DIM, q, zero),
                                        jnp.where(lane >= DIFF_HEAD_DIM, q, zero)], axis=0))
        return out

    def scores(j, qs):
        start = pl.multiple_of(j * tk, tk)
        return [lax.dot_general(k_ref[pl.ds(start, tk), h * hw:(h + 1) * hw], qs[h],
                                (((1,), (1,)), ((), ())), preferred_element_type=F32) for h in heads]

    def colmax(m, st):
        return tuple(jnp.maximum(mh, jnp.max(s, axis=0, keepdims=True)) for mh, s in zip(m, st))

    krow = lax.broadcasted_iota(jnp.int32, (tk, 2 * tq), 0)
    qcol = lax.broadcasted_iota(jnp.int32, (tk, 2 * tq), 1)
    causal = krow <= jnp.where(qcol >= tq, qcol - tq, qcol)
    ninf = tuple(jnp.full((1, 2 * tq), -jnp.inf, F32) for _ in heads)
    zeros = tuple(jnp.zeros((1, 2 * tq), F32) for _ in heads)

    def diagonal(t, qs, m):
        st = [jnp.where(causal, s, -jnp.inf) for s in scores(t, qs)]
        for h in heads:
            s_ref[t % 2, h, t] = st[h]
        return colmax(m, st)

    m_cur = diagonal(0, stacked_queries(0), ninf)
    for t in range(nq):
        slot, nslot = t % 2, (t + 1) % 2
        has_next = t + 1 < nq
        qs_next = stacked_queries(t + 1) if has_next else None
        acc_ref[...] = jnp.zeros_like(acc_ref)

        def body(j, carry, slot=slot, nslot=nslot, has_next=has_next, qs_next=qs_next, m_cur=m_cur):
            l, m_next = carry
            pt = [jnp.exp2(s_ref[slot, h, j] - m_cur[h]) for h in heads]
            pv = [jnp.dot(vt_ref[h, j], pt[h].astype(BF16), preferred_element_type=F32) for h in heads]
            for h in heads:
                acc_ref[h] += pv[h]
            l = tuple(lh + jnp.sum(p, axis=0, keepdims=True) for lh, p in zip(l, pt))
            if has_next:
                st = scores(j, qs_next)
                for h in heads:
                    s_ref[nslot, h, j] = st[h]
                m_next = colmax(m_next, st)
            return l, m_next

        lsum, m_next = lax.fori_loop(0, t + 1, body, (zeros, ninf), unroll=ATTN_UNROLL)
        if has_next:
            m_next = diagonal(t + 1, qs_next, m_next)

        for h in heads:
            acc = acc_ref[h]
            l = lsum[h]
            ot = acc[:, :tq] / l[:, :tq] - lam * (acc[:, tq:] / l[:, tq:])
            ot = ot * lax.rsqrt(jnp.mean(ot * ot, axis=0, keepdims=True) + SUBLN_EPS) * ogt_ref[...]
            o_ref[t * tq:(t + 1) * tq, h * hw:(h + 1) * hw] = (ot * (1.0 - LAM_INIT)).T.astype(o_ref.dtype)
        m_cur = m_next


def _attention(q, k, v, lq1, lk1, lq2, lk2, og, B, S):
    T = q.shape[0]
    tq = min(ATTN_ROWS, S)
    nq = S // tq
    hw = 2 * DIFF_HEAD_DIM
    nh = ATTN_HEADS_PER_STEP
    gw = nh * hw
    small = lambda b, h: (0, 0)
    seq = pl.BlockSpec((S, gw), lambda b, h: (b, h))
    return pl.pallas_call(
        _attn_kernel,
        grid=(B, DIFF_HEADS // nh),
        in_specs=[pl.BlockSpec(lq1.shape, small), pl.BlockSpec(lk1.shape, small),
                  pl.BlockSpec(lq2.shape, small), pl.BlockSpec(lk2.shape, small),
                  pl.BlockSpec(og.shape, small), seq, seq, seq],
        out_specs=seq,
        out_shape=jax.ShapeDtypeStruct((T, DIFF_WIDTH), BF16),
        scratch_shapes=[pltpu.VMEM((nh, nq, hw, tq), BF16), pltpu.VMEM((2, nh, nq, tq, 2 * tq), F32),
                        pltpu.VMEM((nh, hw, 2 * tq), F32)],
        compiler_params=_cparams(("arbitrary", "arbitrary")),
        name="diff_attn",
    )(lq1, lk1, lq2, lk2, og, q, k, v)


def _rwkv_kernel(zr_ref, zk_ref, zv_ref, wp_ref, ap_ref, gg_ref, mu_ref, w0_ref, a0_ref, kk_ref,
                 ka_ref, rk_ref, gnw_ref, gnb_ref, seg_ref, o_ref, state_ref, carry_ref):
    R = zr_ref.shape[0]
    C = min(RWKV_CHUNK, R)
    nch = R // C
    N = RWKV_HEAD
    W = RWKV_WIDTH
    HP = 2 * N
    n_pairs = W // HP

    @pl.when(pl.program_id(1) == 0)
    def _():
        state_ref[...] = jnp.zeros_like(state_ref)
        carry_ref[...] = jnp.zeros_like(carry_ref)

    zr, zk, zv = zr_ref[...], zk_ref[...], zv_ref[...]
    last = V7X_SUBLANES - 1
    r = zr + (_shift_rows(zr, carry_ref[last:last + 1, 0:W]) - zr) * mu_ref[0:1, :]
    k = zk + (_shift_rows(zk, carry_ref[last:last + 1, W:2 * W]) - zk) * mu_ref[1:2, :]
    v = zv + (_shift_rows(zv, carry_ref[last:last + 1, 2 * W:3 * W]) - zv) * mu_ref[2:3, :]
    carry_ref[:, 0:W] = zr[R - V7X_SUBLANES:R, :]
    carry_ref[:, W:2 * W] = zk[R - V7X_SUBLANES:R, :]
    carry_ref[:, 2 * W:3 * W] = zv[R - V7X_SUBLANES:R, :]

    lw = -math.exp(-0.5) * jax.nn.sigmoid(w0_ref[...] + wp_ref[...])
    a = jax.nn.sigmoid(a0_ref[...] + ap_ref[...])
    kk = k * kk_ref[...]
    kk = kk * lax.rsqrt(jnp.maximum(_dot_exact_rhs(kk * kk, seg_ref[...]), 1e-24))
    k2 = k * (1.0 + (a - 1.0) * ka_ref[...])
    a_s = -kk
    b_s = kk * a

    rr = lax.broadcasted_iota(jnp.int32, (R, R), 0)
    cc = lax.broadcasted_iota(jnp.int32, (R, R), 1)
    same_chunk = (rr // C) == (cc // C)
    L = _dot_exact_lhs((same_chunk & (cc <= rr)).astype(BF16), lw)
    bonus_w = r * k2 * rk_ref[...]

    P2 = 2 * C
    sr = lax.broadcasted_iota(jnp.int32, (P2, HP), 0)
    sc = lax.broadcasted_iota(jnp.int32, (P2, HP), 1)
    stack_mask = (sr < C) == (sc < N)
    br = lax.broadcasted_iota(jnp.int32, (P2, P2), 0)
    bc = lax.broadcasted_iota(jnp.int32, (P2, P2), 1)
    same_head = (br < C) == (bc < C)
    tr = jnp.where(br >= C, br - C, br)
    tc = jnp.where(bc >= C, bc - C, bc)
    strict = same_head & (tc < tr)
    lower = same_head & (tc <= tr)
    eye_p = (br == bc).astype(F32)
    kr = lax.broadcasted_iota(jnp.int32, (HP, HP), 0)
    kc = lax.broadcasted_iota(jnp.int32, (HP, HP), 1)
    eye_k = kr == kc

    def dup(x):
        return jnp.concatenate([x, x], axis=0)

    def stack(x):
        return jnp.where(stack_mask, dup(x), 0.0)

    n_sq = int(math.log2(C)) - 1
    items = [(c, p) for c in range(nch) for p in range(n_pairs)]

    def prep(c, p):
        rows = slice(c * C, (c + 1) * C)
        lanes = slice(p * HP, (p + 1) * HP)
        Lc = L[rows, lanes]
        lwc = lw[rows, lanes]
        LC = Lc[C - 1:C, :]
        enL = jnp.exp(-Lc)
        eCL = jnp.exp(LC - Lc)
        b_c, k_c = b_s[rows, lanes], k2[rows, lanes]
        return dict(
            xa=stack(a_s[rows, lanes] * jnp.exp(Lc - lwc)), xr=stack(r[rows, lanes] * jnp.exp(Lc)),
            bt=dup(b_c * enL), kt=dup(k_c * enL), bh=stack(b_c * eCL), kh=stack(k_c * eCL),
            vs=stack(v[rows, lanes]), pc=jnp.exp(LC))

    d = [prep(c, p) for c, p in items]
    aa = [_dot_nt(jnp.concatenate([e["xa"], e["xr"]], axis=0),
                  jnp.concatenate([e["bt"], e["kt"]], axis=0)) for e in d]
    a_ab = [jnp.where(strict, m[:P2, :P2], 0.0) for m in aa]
    a_ak = [jnp.where(strict, m[:P2, P2:], 0.0) for m in aa]
    a_rb = [jnp.where(lower, m[P2:, :P2], 0.0) for m in aa]
    a_rk = [jnp.where(lower, m[P2:, P2:], 0.0) for m in aa]
    tm = [eye_p + m for m in a_ab]
    npw = a_ab
    for _ in range(n_sq):
        npw = [_dot(m, m) for m in npw]
        tm = [t + _dot(t, m) for t, m in zip(tm, npw)]
    av = [_dot(m, e["vs"]) for m, e in zip(a_ak, d)]
    z = [_dot(t, jnp.concatenate([e["xa"], x], axis=1)) for t, e, x in zip(tm, d, av)]
    zero_v = jnp.zeros((P2, HP), F32)
    rhs = [jnp.concatenate([x, jnp.concatenate([zero_v, e["vs"]], axis=1)], axis=0) for e, x in zip(d, z)]
    lhs = [jnp.concatenate([jnp.concatenate([m1, m2], axis=1),
                            jnp.concatenate([e["bh"], e["kh"]], axis=0).T], axis=0)
           for m1, m2, e in zip(a_rb, a_rk, d)]
    wg = [_dot(a, b) for a, b in zip(lhs, rhs)]

    for idx, (c, p) in enumerate(items):
        e = d[idx]
        rows = slice(c * C, (c + 1) * C)
        lanes = slice(p * HP, (p + 1) * HP)
        rp = e["xr"] + wg[idx][:P2, :HP]
        y0 = wg[idx][:P2, HP:]
        gm = wg[idx][P2:, :HP]
        hm = wg[idx][P2:, HP:]
        st = state_ref[p]
        yg = _dot(jnp.concatenate([rp, gm], axis=0), st)
        pc_col = jnp.sum(jnp.where(eye_k, e["pc"], 0.0), axis=1, keepdims=True)
        state_ref[p] = pc_col * st + yg[P2:] + hm
        ys = yg[:P2] + y0
        mu = jnp.sum(ys, axis=-1, keepdims=True) * (1.0 / N)
        yc = jnp.where(stack_mask, ys - mu, 0.0)
        var = jnp.sum(yc * yc, axis=-1, keepdims=True) * (1.0 / N)
        yn = yc * lax.rsqrt(var + GN_EPS)
        bonus = jnp.sum(stack(bonus_w[rows, lanes]), axis=-1, keepdims=True) * e["vs"]
        yn = yn[:C] + yn[C:]
        bonus = bonus[:C] + bonus[C:]
        o_ref[rows, lanes] = ((yn * gnw_ref[:, lanes] + gnb_ref[:, lanes] + bonus)
                              * gg_ref[rows, lanes]).astype(o_ref.dtype)


def _rwkv(zr, zk, zv, wp, ap, gg, mu_rkv, w0, a0, k_k, k_a, r_k, gn_w, gn_b, seg, B, S):
    T = zr.shape[0]
    C = min(RWKV_ROWS, S)
    nc = S // C
    W = RWKV_WIDTH
    HP = 2 * RWKV_HEAD
    row = lambda b, i: (b * nc + i, 0)
    full = lambda b, i: (0, 0)

    def fs(a):
        return pl.BlockSpec(a.shape, full)

    return pl.pallas_call(
        _rwkv_kernel,
        grid=(B, nc),
        in_specs=[pl.BlockSpec((C, W), row)] * 6 + [fs(mu_rkv), fs(w0), fs(a0), fs(k_k), fs(k_a),
                                                    fs(r_k), fs(gn_w), fs(gn_b), fs(seg)],
        out_specs=pl.BlockSpec((C, W), row),
        out_shape=jax.ShapeDtypeStruct((T, W), BF16),
        scratch_shapes=[pltpu.VMEM((W // HP, HP, HP), F32),
                        pltpu.VMEM((V7X_SUBLANES, 3 * W), F32)],
        compiler_params=_cparams(("arbitrary", "arbitrary")),
        name="rwkv7",
    )(zr, zk, zv, wp, ap, gg, mu_rkv, w0, a0, k_k, k_a, r_k, gn_w, gn_b, seg)


ROUTE_COLS = 8
ROUTER_GROUP_LANE = 0
ROUTER_EXPERT_LANE = N_GROUPS


def _out_router_kernel(x_ref, od_ref, orw_ref, wo_ref, mg_ref, wrh_ref, wrl_ref,
                       x1_out, hm_out, route_out, cnt_out):
    tm = PROJ_ROWS if x_ref.shape[0] >= PROJ_ROWS else x_ref.shape[0]
    tiles = [slice(t * tm, (t + 1) * tm) for t in range(x_ref.shape[0] // tm)]
    x1 = [x_ref[r, :] + jnp.dot(od_ref[r, :], wo_ref[0:DIFF_WIDTH, :], preferred_element_type=F32)
          + jnp.dot(orw_ref[r, :], wo_ref[DIFF_WIDTH:, :], preferred_element_type=F32) for r in tiles]
    hm = [_rms(v, mg_ref[...], NORM_EPS) for v in x1]
    for r, v, h in zip(tiles, x1, hm):
        x1_out[r, :] = v
        hm_out[r, :] = h.astype(hm_out.dtype)

    def logits(h):
        hi = h.astype(BF16)
        lo = (h - hi.astype(F32)).astype(BF16)
        return (jnp.dot(hi, wrh_ref[...], preferred_element_type=F32)
                + jnp.dot(hi, wrl_ref[...], preferred_element_type=F32)
                + jnp.dot(lo, wrh_ref[...], preferred_element_type=F32))

    lgs = [logits(h) for h in hm]
    lane = lax.broadcasted_iota(jnp.int32, (tm, V7X_LANES), 1).astype(F32)
    big = float(V7X_LANES)
    ninf = -jnp.inf
    gmask = lane < N_GROUPS

    def choose(lg):
        gmax = jnp.max(jnp.where(gmask, lg, ninf), axis=-1, keepdims=True)
        g_sel = jnp.min(jnp.where(gmask & (lg == gmax), lane, big), axis=-1, keepdims=True)
        g_w = 1.0 / jnp.sum(jnp.where(gmask, jnp.exp(lg - gmax), 0.0), axis=-1, keepdims=True)
        lo_lane = ROUTER_EXPERT_LANE + g_sel * EXPERTS_PER_GROUP
        emask = (lane >= lo_lane) & (lane < lo_lane + EXPERTS_PER_GROUP)
        v1 = jnp.max(jnp.where(emask, lg, ninf), axis=-1, keepdims=True)
        i1 = jnp.min(jnp.where(emask & (lg == v1), lane, big), axis=-1, keepdims=True)
        emask2 = emask & (lane != i1)
        v2 = jnp.max(jnp.where(emask2, lg, ninf), axis=-1, keepdims=True)
        i2 = jnp.min(jnp.where(emask2 & (lg == v2), lane, big), axis=-1, keepdims=True)
        e2x = jnp.exp(v2 - v1)
        den = 1.0 + e2x
        return i1, i2, (1.0 / den) * g_w, (e2x / den) * g_w

    chosen = [choose(lg) for lg in lgs]

    ri = lax.broadcasted_iota(jnp.int32, (tm, tm), 0)
    ci = lax.broadcasted_iota(jnp.int32, (tm, tm), 1)
    before = (ci < ri).astype(BF16)
    ohs = [((lane == i1) | (lane == i2)).astype(F32) for i1, i2, _, _ in chosen]
    prefixes = [jnp.dot(before, oh.astype(BF16), preferred_element_type=F32) for oh in ohs]
    col = lax.broadcasted_iota(jnp.int32, (tm, ROUTE_COLS), 1)
    for t, (r, (i1, i2, wt1, wt2), oh, prefix) in enumerate(zip(tiles, chosen, ohs, prefixes)):
        rank1 = jnp.sum(jnp.where(lane == i1, prefix, 0.0), axis=-1, keepdims=True)
        rank2 = jnp.sum(jnp.where(lane == i2, prefix, 0.0), axis=-1, keepdims=True)
        cnt_out[t * V7X_SUBLANES:(t + 1) * V7X_SUBLANES, :] = jnp.broadcast_to(
            jnp.sum(oh, axis=0, keepdims=True), (V7X_SUBLANES, V7X_LANES))
        e1 = i1 - ROUTER_EXPERT_LANE
        e2 = i2 - ROUTER_EXPERT_LANE
        route_out[r, :] = jnp.where(col == 0, e1, jnp.where(col == 1, e2, jnp.where(col == 2, wt1, jnp.where(
            col == 3, wt2, jnp.where(col == 4, rank1, jnp.where(col == 5, rank2, 0.0))))))


def _out_router(x2, od, orw, w_out, moe_g, wr_hi, wr_lo):
    T, D = x2.shape
    tm = min(PROJ_ROWS, T)
    nt = T // tm
    tps = TILES_PER_STEP if nt % TILES_PER_STEP == 0 else 1
    tr = tm * tps
    row = lambda i: (i, 0)
    full = lambda i: (0, 0)

    def fs(a):
        return pl.BlockSpec(a.shape, full)

    return pl.pallas_call(
        _out_router_kernel,
        grid=(nt // tps,),
        in_specs=[pl.BlockSpec((tr, D), row), pl.BlockSpec((tr, DIFF_WIDTH), row),
                  pl.BlockSpec((tr, RWKV_WIDTH), row), fs(w_out), fs(moe_g), fs(wr_hi), fs(wr_lo)],
        out_specs=[pl.BlockSpec((tr, D), row), pl.BlockSpec((tr, D), row),
                   pl.BlockSpec((tr, ROUTE_COLS), row), pl.BlockSpec((tps * V7X_SUBLANES, V7X_LANES), row)],
        out_shape=[jax.ShapeDtypeStruct((T, D), F32), jax.ShapeDtypeStruct((T, D), BF16),
                   jax.ShapeDtypeStruct((T, ROUTE_COLS), F32),
                   jax.ShapeDtypeStruct((nt * V7X_SUBLANES, V7X_LANES), F32)],
        compiler_params=_cparams(("arbitrary",)),
        name="out_router",
    )(x2, od, orw, w_out, moe_g, wr_hi, wr_lo)


RUN_ALIGN = V7X_SUBLANES
RUN_CLASSES = 6
TAB_COUNT, TAB_NUSED = 0, RUN_CLASSES
TAB_SRC = V7X_SUBLANES
TAB_DST = TAB_SRC + RUN_CLASSES * N_EXPERTS
TAB_WORDS = 4 * V7X_LANES


def _stage_rows(tm):
    need = 2 * tm + N_EXPERTS * (RUN_ALIGN - 1)
    return -(-need // V7X_LANES) * V7X_LANES


def _run_table(src, dst, length, spare):
    i32 = jnp.int32
    n = src.shape[0]
    cls = jnp.arange(RUN_CLASSES, dtype=i32)[None, :, None]
    ln = length[:, None, :]
    has = ((ln >> 3) >> cls) & 1
    off = ln & ~((2 * RUN_ALIGN << cls) - 1)
    pos = jnp.cumsum(has, axis=-1) - 1
    k = jnp.arange(N_EXPERTS, dtype=i32)
    pick = (has[..., None] == 1) & (pos[..., None] == k)

    def compact(v):
        return jnp.sum(jnp.where(pick, (v[:, None, :] + off)[..., None], 0), axis=2).reshape(n, -1)

    head = jnp.concatenate([jnp.sum(has, axis=-1), jnp.broadcast_to(spare, (n, 1)).astype(i32),
                            jnp.zeros((n, TAB_SRC - RUN_CLASSES - 1), i32)], axis=1)
    tab = jnp.concatenate([head, compact(src), compact(dst)], axis=1)
    return jnp.pad(tab, ((0, 0), (0, TAB_WORDS - tab.shape[1]))).reshape(n, 1, TAB_WORDS)


def _run_copies(tab_ref, tile, make_copy, op):
    for c in range(RUN_CLASSES):
        size = RUN_ALIGN << c

        def one(k, carry, c=c, size=size):
            src = tab_ref[tile, 0, TAB_SRC + c * N_EXPERTS + k]
            dst = tab_ref[tile, 0, TAB_DST + c * N_EXPERTS + k]
            getattr(make_copy(pl.multiple_of(src, RUN_ALIGN), pl.multiple_of(dst, RUN_ALIGN), size), op)()
            return carry

        lax.fori_loop(0, tab_ref[tile, 0, TAB_COUNT + c], one, 0)


def _dispatch_kernel(tab_ref, tail_ref, sl_ref, hm_ref, xb_ref, stage_ref, zero_ref, sem, zsem):
    tps, ns = stage_ref.shape[0], stage_ref.shape[1]
    tm = hm_ref.shape[0] // tps
    bm = zero_ref.shape[0]
    nb = xb_ref.shape[0] // bm
    n_used = tail_ref[0, 0, TAB_NUSED]

    def tile_copy(t):
        def make_copy(src, dst, size):
            return pltpu.make_async_copy(stage_ref.at[t, pl.ds(src, size)], xb_ref.at[pl.ds(dst, size)], sem)
        return make_copy

    def zero_copy(src, dst, size):
        return pltpu.make_async_copy(zero_ref.at[pl.ds(src, size)], xb_ref.at[pl.ds(dst, size)], zsem)

    def zero_blocks(op):
        def one(b, c):
            getattr(zero_copy(0, pl.multiple_of(b * bm, bm), bm), op)()
            return c
        lax.fori_loop(n_used, nb, one, 0)

    @pl.when(pl.program_id(0) == 0)
    def _():
        zero_ref[...] = jnp.zeros_like(zero_ref)
        _run_copies(tail_ref, 0, zero_copy,"start")
        zero_blocks("start")

    srow = lax.broadcasted_iota(jnp.int32, (ns, tm), 0)
    for t in range(tps):
        sel = (srow == sl_ref[t, 0:1, :]) | (srow == sl_ref[t, 1:2, :])
        stage_ref[t] = _pack16(jnp.dot(sel.astype(BF16), hm_ref[t * tm:(t + 1) * tm, :],
                                       preferred_element_type=F32))
        _run_copies(tab_ref, t, tile_copy(t),"start")
    for t in range(tps):
        _run_copies(tab_ref, t, tile_copy(t),"wait")

    @pl.when(pl.program_id(0) == pl.num_programs(0) - 1)
    def _():
        _run_copies(tail_ref, 0, zero_copy,"wait")
        zero_blocks("wait")


def _dispatch(tab, tail, sl_rows, hm, P):
    T, D = hm.shape
    tm = min(PROJ_ROWS, T)
    nt = T // tm
    tps = TILES_PER_STEP if nt % TILES_PER_STEP == 0 else 1
    return pl.pallas_call(
        _dispatch_kernel,
        grid=(nt // tps,),
        in_specs=[pl.BlockSpec((tps, 1, tab.shape[-1]), lambda i: (i, 0, 0), memory_space=pltpu.SMEM),
                  pl.BlockSpec((1, 1, tail.shape[-1]), lambda i: (0, 0, 0), memory_space=pltpu.SMEM),
                  pl.BlockSpec((tps, 2, tm), lambda i: (i, 0, 0)),
                  pl.BlockSpec((tps * tm, D), lambda i: (i, 0))],
        out_specs=pl.BlockSpec(memory_space=pl.ANY),
        out_shape=jax.ShapeDtypeStruct((P, D // 2), PACKED),
        scratch_shapes=[pltpu.VMEM((tps, _stage_rows(tm), D // 2), PACKED),
                        pltpu.VMEM((MOE_ROWS, D // 2), PACKED),
                        pltpu.SemaphoreType.DMA(()), pltpu.SemaphoreType.DMA(())],
        compiler_params=_cparams(("arbitrary",)),
        name="moe_dispatch",
    )(tab, tail, sl_rows, hm)


def _expert_kernel(be_ref, nu_ref, xb_ref, wg_ref, wu_ref, wd_ref, yb_ref, wgb, wub, wdb):
    i = pl.program_id(0)
    prev = be_ref[jnp.maximum(i - 1, 0)]
    changed = (i == 0) | (be_ref[i] != prev)

    @pl.when(changed)
    def _():
        wgb[...] = wg_ref[0].astype(BF16)
        wub[...] = wu_ref[0].astype(BF16)
        wdb[...] = wd_ref[0].astype(BF16)

    @pl.when(i < nu_ref[0])
    def _():
        xb = _unpack16(xb_ref[...]).astype(BF16)
        gate = jnp.dot(xb, wgb[...], preferred_element_type=F32)
        up = jnp.dot(xb, wub[...], preferred_element_type=F32)
        hdn = (gate * jax.nn.sigmoid(gate)) * up
        y = jnp.dot(hdn.astype(BF16), wdb[...], preferred_element_type=F32)
        yb_ref[...] = _pack16(y.astype(BF16).astype(F32))

    @pl.when(i >= nu_ref[0])
    def _():
        yb_ref[...] = jnp.zeros_like(yb_ref)


def _experts(block_e, n_used, xb, w_gate, w_up, w_down):
    P, DP = xb.shape
    D = 2 * DP
    bm = MOE_ROWS
    nb = P // bm
    E = D_EXPERT
    grid_spec = pltpu.PrefetchScalarGridSpec(
        num_scalar_prefetch=2,
        grid=(nb,),
        in_specs=[pl.BlockSpec((bm, DP), lambda i, be, nu: (jnp.minimum(i, nu[0] - 1), 0)),
                  pl.BlockSpec((1, D, E), lambda i, be, nu: (be[i], 0, 0)),
                  pl.BlockSpec((1, D, E), lambda i, be, nu: (be[i], 0, 0)),
                  pl.BlockSpec((1, E, D), lambda i, be, nu: (be[i], 0, 0))],
        out_specs=pl.BlockSpec((bm, DP), lambda i, be, nu: (jnp.where(i < nu[0], i, nb - 1), 0)),
        scratch_shapes=[pltpu.VMEM((D, E), BF16), pltpu.VMEM((D, E), BF16), pltpu.VMEM((E, D), BF16)],
    )
    return pl.pallas_call(
        _expert_kernel,
        grid_spec=grid_spec,
        out_shape=jax.ShapeDtypeStruct((P, DP), PACKED),
        input_output_aliases={2: 0},
        compiler_params=_cparams(("arbitrary",)),
        name="moe_experts",
    )(block_e, n_used, xb, w_gate, w_up, w_down)


def _final_kernel(tab_ref, x1_ref, route_ref, p_ref, wpl_ref, plg_ref, pgg_ref, wgate_ref, yb_ref,
                  o_ref, stage_ref, sem):
    tps, ns = stage_ref.shape[0], stage_ref.shape[1]
    tm = x1_ref.shape[0] // tps
    tiles = [slice(t * tm, (t + 1) * tm) for t in range(tps)]
    stage_ref[...] = jnp.zeros_like(stage_ref)

    def tile_copy(t):
        def make_copy(src, dst, size):
            return pltpu.make_async_copy(yb_ref.at[pl.ds(dst, size)], stage_ref.at[t, pl.ds(src, size)], sem)
        return make_copy

    for t in range(tps):
        _run_copies(tab_ref, t, tile_copy(t),"start")
    pe = [_rms(jnp.dot(p_ref[r, :].astype(BF16), wpl_ref[...], preferred_element_type=F32),
               plg_ref[...], NORM_EPS) for r in tiles]
    for t in range(tps):
        _run_copies(tab_ref, t, tile_copy(t),"wait")

    scol = lax.broadcasted_iota(jnp.int32, (tm, ns), 1).astype(F32)

    def selection(r):
        route = route_ref[r, :]
        wsel = (jnp.where(scol == route[:, 0:1], route[:, 2:3], 0.0)
                + jnp.where(scol == route[:, 1:2], route[:, 3:4], 0.0))
        w_hi = wsel.astype(BF16)
        return w_hi, (wsel - w_hi.astype(F32)).astype(BF16)

    wsel = [selection(r) for r in tiles]
    yb16 = [_unpack16(stage_ref[t]).astype(BF16) for t in range(tps)]
    moe = [jnp.dot(w_hi, y, preferred_element_type=F32) + jnp.dot(w_lo, y, preferred_element_type=F32)
           for (w_hi, w_lo), y in zip(wsel, yb16)]
    x2 = [x1_ref[r, :] + m for r, m in zip(tiles, moe)]
    gate = [jax.nn.sigmoid(_dot(_rms(v, pgg_ref[...], NORM_EPS), wgate_ref[...])) for v in x2]
    for r, v, e, g in zip(tiles, x2, pe, gate):
        o_ref[r, :] = v + e * g


def _final(tab, x1, route, p2, w_pl, pl_g, pl_gate_g, w_gate, yb):
    T, D = x1.shape
    tm = min(PROJ_ROWS, T)
    nt = T // tm
    tps = TILES_PER_STEP if nt % TILES_PER_STEP == 0 else 1
    tr = tps * tm
    row = lambda i: (i, 0)
    full = lambda i: (0, 0)

    def fs(a):
        return pl.BlockSpec(a.shape, full)

    return pl.pallas_call(
        _final_kernel,
        grid=(nt // tps,),
        in_specs=[pl.BlockSpec((tps, 1, tab.shape[-1]), lambda i: (i, 0, 0), memory_space=pltpu.SMEM),
                  pl.BlockSpec((tr, D), row), pl.BlockSpec((tr, ROUTE_COLS), row),
                  pl.BlockSpec((tr, PL_DIM), row), fs(w_pl), fs(pl_g), fs(pl_gate_g), fs(w_gate),
                  pl.BlockSpec(memory_space=pl.ANY)],
        out_specs=pl.BlockSpec((tr, D), row),
        out_shape=jax.ShapeDtypeStruct((T, D), F32),
        scratch_shapes=[pltpu.VMEM((tps, _stage_rows(tm), D // 2), PACKED), pltpu.SemaphoreType.DMA(())],
        compiler_params=_cparams(("arbitrary",)),
        name="combine_final",
    )(tab, x1, route, p2, w_pl, pl_g, pl_gate_g, w_gate, yb)


def kernel(x, p, positions, attn_norm_g, w_in, q_norm_g, k_norm_g, lambda_q1, lambda_k1, lambda_q2, lambda_k2, diff_out_g, mu_rkv, mu_wag, w0, w_lora_a, w_lora_b, a0, a_lora_a, a_lora_b, g_lora_a, g_lora_b, k_k, k_a, r_k, gn_w, gn_b, w_out, moe_norm_g, w_group, w_expert_router, w_gate, w_up, w_down, w_pl, pl_norm_g, pl_gate_norm_g, w_pl_gate):
    B, S, D = x.shape
    T = B * S
    assert p.shape[0] == 1, "one layer"
    x2 = x.reshape(T, D)
    p2 = p[0].reshape(T, PL_DIM)
    pos = positions.astype(F32).reshape(T, 1)

    def row(a):
        return a.reshape(1, -1).astype(F32)

    half = DIFF_HEAD_DIM // 2
    inv_freq = ROPE_THETA ** (-jnp.arange(half, dtype=F32) / half)
    invf = jnp.tile(inv_freq, V7X_LANES // half).reshape(1, V7X_LANES)
    li = jnp.arange(DIFF_WIDTH) // DIFF_HEAD_DIM
    seg = (li[:, None] == li[None, :]).astype(BF16)
    reps = DIFF_WIDTH // DIFF_HEAD_DIM

    q, k, v, zr, zk, zv, wp, ap, gg = _in_proj(
        x2, pos, row(attn_norm_g[0]), w_in[0].astype(BF16),
        w_lora_a[0].astype(BF16), a_lora_a[0].astype(BF16), g_lora_a[0].astype(BF16),
        w_lora_b[0].astype(BF16), a_lora_b[0].astype(BF16), g_lora_b[0].astype(BF16),
        mu_wag[0].astype(F32), row(jnp.tile(q_norm_g[0], reps)), row(jnp.tile(k_norm_g[0], reps)),
        invf, seg, B, S)

    o_diff = _attention(q, k, v, row(lambda_q1[0]), row(lambda_k1[0]), row(lambda_q2[0]),
                        row(lambda_k2[0]), diff_out_g[0].astype(F32).reshape(-1, 1), B, S)
    o_rwkv = _rwkv(zr, zk, zv, wp, ap, gg, mu_rkv[0].astype(F32), row(w0[0]), row(a0[0]),
                   row(k_k[0]), row(k_a[0]), row(r_k[0]), row(gn_w[0]), row(gn_b[0]), seg, B, S)

    wr = jnp.concatenate([w_group[0], jnp.transpose(w_expert_router[0], (1, 0, 2)).reshape(D, N_EXPERTS)],
                         axis=1).astype(F32)
    wr = jnp.pad(wr, ((0, 0), (0, V7X_LANES - wr.shape[1])))
    wr_hi = wr.astype(BF16)
    wr_lo = (wr - wr_hi.astype(F32)).astype(BF16)
    x1, hm, route, cnt = _out_router(x2, o_diff, o_rwkv, w_out[0].astype(BF16), row(moe_norm_g[0]),
                                     wr_hi, wr_lo)

    bm = MOE_ROWS
    tm = min(PROJ_ROWS, T)
    nt = T // tm
    i32 = jnp.int32
    cnt_te = cnt.reshape(nt, V7X_SUBLANES, V7X_LANES)[:, 0, ROUTER_EXPERT_LANE:ROUTER_EXPERT_LANE + N_EXPERTS]
    len_te = (cnt_te.astype(i32) + RUN_ALIGN - 1) // RUN_ALIGN * RUN_ALIGN
    tot_e = jnp.sum(len_te, axis=0)
    pcounts = (tot_e + bm - 1) // bm * bm
    pends = jnp.cumsum(pcounts)
    pstarts = pends - pcounts
    dst_te = pstarts[None, :] + jnp.cumsum(len_te, axis=0) - len_te
    src_te = jnp.cumsum(len_te, axis=1) - len_te
    tab = _run_table(src_te, dst_te, len_te, 0)
    nb = -(-(2 * T + nt * N_EXPERTS * (RUN_ALIGN - 1)) // bm) + N_EXPERTS
    P = nb * bm
    n_used = (pends[-1] // bm).astype(i32).reshape(1)
    block_start = jnp.arange(nb, dtype=i32) * bm
    block_e = jnp.minimum(jnp.sum((pends[None, :] <= block_start[:, None]).astype(i32), axis=1),
                          N_EXPERTS - 1)
    eid = route[:, 0:2].astype(i32).reshape(nt, tm, 2)
    run_src = jnp.sum(jnp.where(eid[..., None] == jnp.arange(N_EXPERTS, dtype=i32), src_te[:, None, None, :], 0),
                      axis=-1)
    sl = run_src + route[:, 4:6].astype(i32).reshape(nt, tm, 2)
    sl_rows = jnp.transpose(sl, (0, 2, 1))
    route2 = jnp.concatenate([sl.reshape(T, 2).astype(F32), route[:, 2:4],
                              jnp.zeros((T, ROUTE_COLS - 4), F32)], axis=1)

    tail = _run_table(jnp.zeros((1, N_EXPERTS), i32), (pstarts + tot_e)[None, :], (pcounts - tot_e)[None, :],
                      n_used)
    xb = _dispatch(tab, tail, sl_rows, hm, P)
    yb = _experts(block_e, n_used, xb, w_gate[0], w_up[0], w_down[0])
    out = _final(tab, x1, route2, p2, w_pl[0].astype(BF16), row(pl_norm_g[0]), row(pl_gate_norm_g[0]),
                 w_pl_gate[0].astype(BF16), yb)
    return out.reshape(B, S, D)
```

```python
import functools
import math

import jax
import jax.numpy as jnp
from jax import lax
from jax.experimental import pallas as pl
from jax.experimental.pallas import tpu as pltpu

F32 = jnp.float32
BF16 = jnp.bfloat16

D_MODEL = 1024
PL_DIM = 256
DIFF_WIDTH = 512
RWKV_WIDTH = 512
DIFF_HEAD_DIM = 64
DIFF_HEADS = 4
RWKV_HEAD = 64
RWKV_HEADS = 8
D_DECAY_LORA = 64
D_AAA_LORA = 64
D_GATE_LORA = 160
ROPE_THETA = 10000.0
NORM_EPS = 1e-6
SUBLN_EPS = 1e-5
GN_EPS = 64e-5
N_GROUPS = 4
EXPERTS_PER_GROUP = 8
N_EXPERTS = 32
D_EXPERT = 512
LAM_INIT = 0.8 - 0.6 * math.exp(0.0)

V7X_LANES = 128
V7X_SUBLANES = 8
V7X_VMEM_BYTES = 64 * 1024 * 1024

IN_PROJ_ROWS = 256
IN_PROJ_TILES_PER_STEP = 2
PROJ_ROWS = 256
TILES_PER_STEP = 4
ATTN_ROWS = 256
ATTN_UNROLL = 4
ATTN_HEADS_PER_STEP = 2
RWKV_CHUNK = 64
RWKV_ROWS = 256
MOE_ROWS = 512
VMEM_LIMIT = 56 * 1024 * 1024


def _cparams(sem):
    return pltpu.CompilerParams(dimension_semantics=sem, vmem_limit_bytes=VMEM_LIMIT)


def _dot(a, b):
    return jnp.dot(a.astype(BF16), b.astype(BF16), preferred_element_type=F32)


def _dot_nt(a, b):
    return lax.dot_general(a.astype(BF16), b.astype(BF16), (((1,), (1,)), ((), ())),
                           preferred_element_type=F32)


def _dot_tn(a, b):
    return lax.dot_general(a.astype(BF16), b.astype(BF16), (((0,), (0,)), ((), ())),
                           preferred_element_type=F32)


def _split3(x):
    hi = x.astype(BF16)
    r1 = x - hi.astype(F32)
    mid = r1.astype(BF16)
    lo = (r1 - mid.astype(F32)).astype(BF16)
    return hi, mid, lo


def _dot_exact_rhs(x, m01):
    hi = x.astype(BF16)
    mid = (x - hi.astype(F32)).astype(BF16)
    return (jnp.dot(hi, m01, preferred_element_type=F32)
            + jnp.dot(mid, m01, preferred_element_type=F32))


def _dot_exact_lhs(m01, x):
    hi, mid, lo = _split3(x)
    return (jnp.dot(m01, hi, preferred_element_type=F32)
            + jnp.dot(m01, mid, preferred_element_type=F32)
            + jnp.dot(m01, lo, preferred_element_type=F32))


PACKED = jnp.uint32


def _pack16(x):
    h = x.shape[1] // 2
    lo = lax.bitcast_convert_type(x[:, :h], PACKED) >> 16
    hi = lax.bitcast_convert_type(x[:, h:], PACKED) & jnp.uint32(0xFFFF0000)
    return hi | lo


def _unpack16(w):
    lo = lax.bitcast_convert_type(w << 16, F32)
    hi = lax.bitcast_convert_type(w & jnp.uint32(0xFFFF0000), F32)
    return jnp.concatenate([lo, hi], axis=1)


def _rms(x, g, eps):
    return x * lax.rsqrt(jnp.mean(x * x, axis=-1, keepdims=True) + eps) * g


def _shift_rows(z, prev_row):
    rolled = pltpu.roll(z, 1, axis=0)
    row = lax.broadcasted_iota(jnp.int32, z.shape, 0)
    return jnp.where(row == 0, prev_row, rolled)


def _in_proj_kernel(x_ref, pos_ref, g_ref, win_ref, w1_ref, a1_ref, g1_ref, w2_ref, a2_ref, g2_ref,
                    mu_ref, qg_ref, kg_ref, invf_ref, seg_ref,
                    q_out, k_out, v_out, zr_out, zk_out, zv_out, wp_out, ap_out, gg_out,
                    carry_ref):
    tm = IN_PROJ_ROWS if x_ref.shape[0] >= IN_PROJ_ROWS else x_ref.shape[0]
    tiles = [slice(t * tm, (t + 1) * tm) for t in range(x_ref.shape[0] // tm)]
    w = DIFF_WIDTH
    half = DIFF_HEAD_DIM // 2

    @pl.when(pl.program_id(1) == 0)
    def _():
        carry_ref[...] = jnp.zeros_like(carry_ref)

    prev = carry_ref[V7X_SUBLANES - 1:V7X_SUBLANES, :]
    hns, dhs = [], []
    for r in tiles:
        hn = _rms(x_ref[r, :], g_ref[...], NORM_EPS)
        dhs.append(_shift_rows(hn, prev) - hn)
        hns.append(hn)
        prev = hn[tm - 1:tm, :]
    carry_ref[...] = hns[-1][tm - V7X_SUBLANES:tm, :]

    zq, zk, low = [], [], []
    for r, hn, dh in zip(tiles, hns, dhs):
        hb = hn.astype(BF16)

        def proj(c, hb=hb):
            return jnp.dot(hb, win_ref[:, c * w:(c + 1) * w], preferred_element_type=F32)

        zq.append(proj(0))
        zk.append(proj(1))
        low.append((_dot(hn + dh * mu_ref[0:1, :], w1_ref[...]), _dot(hn + dh * mu_ref[1:2, :], a1_ref[...]),
                    _dot(hn + dh * mu_ref[2:3, :], g1_ref[...])))
        v_out[r, :] = proj(2).astype(v_out.dtype)
        zr_out[r, :] = proj(3)
        zk_out[r, :] = proj(4)
        zv_out[r, :] = proj(5)

    lane = lax.broadcasted_iota(jnp.int32, (tm, w), 1)
    first_half = (lane % DIFF_HEAD_DIM) < half
    scale = DIFF_HEAD_DIM ** -0.5 * math.log2(math.e)
    for r, q, k, (lw, la, lg) in zip(tiles, zq, zk, low):
        ang = pos_ref[r, :] * invf_ref[...]
        cosf = jnp.concatenate([jnp.cos(ang)] * (w // V7X_LANES), axis=1)
        sinf = jnp.concatenate([jnp.sin(ang)] * (w // V7X_LANES), axis=1)
        sin_signed = jnp.where(first_half, -sinf, sinf)

        def norm_rope(z, gain, cosf=cosf, sin_signed=sin_signed):
            ss = _dot_exact_rhs(z * z, seg_ref[...])
            zn = z * lax.rsqrt(ss * (1.0 / DIFF_HEAD_DIM) + NORM_EPS) * gain
            partner = jnp.where(first_half, pltpu.roll(zn, w - half, axis=1), pltpu.roll(zn, half, axis=1))
            return zn * cosf + partner * sin_signed

        q_out[r, :] = (norm_rope(q, qg_ref[...]) * scale).astype(q_out.dtype)
        k_out[r, :] = norm_rope(k, kg_ref[...]).astype(k_out.dtype)
        wp_out[r, :] = _dot(jnp.tanh(lw), w2_ref[...])
        ap_out[r, :] = _dot(la, a2_ref[...])
        gg_out[r, :] = _dot(jax.nn.sigmoid(lg), g2_ref[...])


def _in_proj(x2, pos, attn_g, w_in, w1, a1, g1, w2, a2, g2, mu_wag, qg, kg, invf, seg, B, S):
    T, D = x2.shape
    tile = min(IN_PROJ_ROWS, S)
    tps = IN_PROJ_TILES_PER_STEP if (S // tile) % IN_PROJ_TILES_PER_STEP == 0 else 1
    tm = tile * tps
    ns = S // tm
    w = DIFF_WIDTH
    row = lambda b, i: (b * ns + i, 0)
    full = lambda b, i: (0, 0)

    def fs(a):
        return pl.BlockSpec(a.shape, full)

    outs = ([jax.ShapeDtypeStruct((T, w), BF16)] * 3 + [jax.ShapeDtypeStruct((T, w), F32)] * 6)
    return pl.pallas_call(
        _in_proj_kernel,
        grid=(B, ns),
        in_specs=[pl.BlockSpec((tm, D), row), pl.BlockSpec((tm, 1), row), fs(attn_g), fs(w_in),
                  fs(w1), fs(a1), fs(g1), fs(w2), fs(a2), fs(g2), fs(mu_wag), fs(qg), fs(kg),
                  fs(invf), fs(seg)],
        out_specs=[pl.BlockSpec((tm, w), row)] * 9,
        out_shape=outs,
        scratch_shapes=[pltpu.VMEM((V7X_SUBLANES, D), F32)],
        compiler_params=_cparams(("arbitrary", "arbitrary")),
        name="in_proj",
    )(x2, pos, attn_g, w_in, w1, a1, g1, w2, a2, g2, mu_wag, qg, kg, invf, seg)


def _attn_kernel(lq1_ref, lk1_ref, lq2_ref, lk2_ref, ogt_ref, q_ref, k_ref, v_ref, o_ref,
                 vt_ref, s_ref, acc_ref):
    nh, nq, hw, tq = vt_ref.shape
    tk = tq
    heads = range(nh)
    lam = (jnp.exp(jnp.sum(lq1_ref[...] * lk1_ref[...], axis=-1, keepdims=True))
           - jnp.exp(jnp.sum(lq2_ref[...] * lk2_ref[...], axis=-1, keepdims=True)) + LAM_INIT)

    for h in heads:
        for c in range(nq):
            vt_ref[h, c] = (v_ref[c * tk:(c + 1) * tk, h * hw:(h + 1) * hw]
                            .astype(F32).T.astype(vt_ref.dtype))

    lane = lax.broadcasted_iota(jnp.int32, (tq, hw), 1)

    def stacked_queries(t):
        out = []
        for h in heads:
            q = q_ref[t * tq:(t + 1) * tq, h * hw:(h + 1) * hw]
            zero = jnp.zeros_like(q)
            out.append(jnp.concatenate([jnp.where(lane < DIFF_HEAD_DIM, q, zero),
                                        jnp.where(lane >= DIFF_HEAD_DIM, q, zero)], axis=0))
        return out

    def scores(j, qs):
        start = pl.multiple_of(j * tk, tk)
        return [lax.dot_general(k_ref[pl.ds(start, tk), h * hw:(h + 1) * hw], qs[h],
                                (((1,), (1,)), ((), ())), preferred_element_type=F32) for h in heads]

    def colmax(m, st):
        return tuple(jnp.maximum(mh, jnp.max(s, axis=0, keepdims=True)) for mh, s in zip(m, st))

    krow = lax.broadcasted_iota(jnp.int32, (tk, 2 * tq), 0)
    qcol = lax.broadcasted_iota(jnp.int32, (tk, 2 * tq), 1)
    causal = krow <= jnp.where(qcol >= tq, qcol - tq, qcol)
    ninf = tuple(jnp.full((1, 2 * tq), -jnp.inf, F32) for _ in heads)
    zeros = tuple(jnp.zeros((1, 2 * tq), F32) for _ in heads)

    def diagonal(t, qs, m):
        st = [jnp.where(causal, s, -jnp.inf) for s in scores(t, qs)]
        for h in heads:
            s_ref[t % 2, h, t] = st[h]
        return colmax(m, st)

    m_cur = diagonal(0, stacked_queries(0), ninf)
    for t in range(nq):
        slot, nslot = t % 2, (t + 1) % 2
        has_next = t + 1 < nq
        qs_next = stacked_queries(t + 1) if has_next else None
        acc_ref[...] = jnp.zeros_like(acc_ref)

        def body(j, carry, slot=slot, nslot=nslot, has_next=has_next, qs_next=qs_next, m_cur=m_cur):
            l, m_next = carry
            pt = [jnp.exp2(s_ref[slot, h, j] - m_cur[h]) for h in heads]
            pv = [jnp.dot(vt_ref[h, j], pt[h].astype(BF16), preferred_element_type=F32) for h in heads]
            for h in heads:
                acc_ref[h] += pv[h]
            l = tuple(lh + jnp.sum(p, axis=0, keepdims=True) for lh, p in zip(l, pt))
            if has_next:
                st = scores(j, qs_next)
                for h in heads:
                    s_ref[nslot, h, j] = st[h]
                m_next = colmax(m_next, st)
            return l, m_next

        lsum, m_next = lax.fori_loop(0, t + 1, body, (zeros, ninf), unroll=ATTN_UNROLL)
        if has_next:
            m_next = diagonal(t + 1, qs_next, m_next)

        for h in heads:
            acc = acc_ref[h]
            l = lsum[h]
            ot = acc[:, :tq] / l[:, :tq] - lam * (acc[:, tq:] / l[:, tq:])
            ot = ot * lax.rsqrt(jnp.mean(ot * ot, axis=0, keepdims=True) + SUBLN_EPS) * ogt_ref[...]
            o_ref[t * tq:(t + 1) * tq, h * hw:(h + 1) * hw] = (ot * (1.0 - LAM_INIT)).T.astype(o_ref.dtype)
        m_cur = m_next


def _attention(q, k, v, lq1, lk1, lq2, lk2, og, B, S):
    T = q.shape[0]
    tq = min(ATTN_ROWS, S)
    nq = S // tq
    hw = 2 * DIFF_HEAD_DIM
    nh = ATTN_HEADS_PER_STEP
    gw = nh * hw
    small = lambda b, h: (0, 0)
    seq = pl.BlockSpec((S, gw), lambda b, h: (b, h))
    return pl.pallas_call(
        _attn_kernel,
        grid=(B, DIFF_HEADS // nh),
        in_specs=[pl.BlockSpec(lq1.shape, small), pl.BlockSpec(lk1.shape, small),
                  pl.BlockSpec(lq2.shape, small), pl.BlockSpec(lk2.shape, small),
                  pl.BlockSpec(og.shape, small), seq, seq, seq],
        out_specs=seq,
        out_shape=jax.ShapeDtypeStruct((T, DIFF_WIDTH), BF16),
        scratch_shapes=[pltpu.VMEM((nh, nq, hw, tq), BF16), pltpu.VMEM((2, nh, nq, tq, 2 * tq), F32),
                        pltpu.VMEM((nh, hw, 2 * tq), F32)],
        compiler_params=_cparams(("arbitrary", "arbitrary")),
        name="diff_attn",
    )(lq1, lk1, lq2, lk2, og, q, k, v)


def _rwkv_kernel(zr_ref, zk_ref, zv_ref, wp_ref, ap_ref, gg_ref, mu_ref, w0_ref, a0_ref, kk_ref,
                 ka_ref, rk_ref, gnw_ref, gnb_ref, o_ref, state_ref, carry_ref):
    R = zr_ref.shape[0]
    C = min(RWKV_CHUNK, R)
    nch = R // C
    N = RWKV_HEAD
    W = RWKV_WIDTH
    HP = 2 * N
    n_pairs = W // HP

    @pl.when(pl.program_id(1) == 0)
    def _():
        state_ref[...] = jnp.zeros_like(state_ref)
        carry_ref[...] = jnp.zeros_like(carry_ref)

    zr, zk, zv = zr_ref[...], zk_ref[...], zv_ref[...]
    last = V7X_SUBLANES - 1
    r = zr + (_shift_rows(zr, carry_ref[last:last + 1, 0:W]) - zr) * mu_ref[0:1, :]
    k = zk + (_shift_rows(zk, carry_ref[last:last + 1, W:2 * W]) - zk) * mu_ref[1:2, :]
    v = zv + (_shift_rows(zv, carry_ref[last:last + 1, 2 * W:3 * W]) - zv) * mu_ref[2:3, :]
    carry_ref[:, 0:W] = zr[R - V7X_SUBLANES:R, :]
    carry_ref[:, W:2 * W] = zk[R - V7X_SUBLANES:R, :]
    carry_ref[:, 2 * W:3 * W] = zv[R - V7X_SUBLANES:R, :]

    lw = -math.exp(-0.5) * jax.nn.sigmoid(w0_ref[...] + wp_ref[...])
    a = jax.nn.sigmoid(a0_ref[...] + ap_ref[...])
    kk = k * kk_ref[...]
    kk2 = kk * kk
    pair_lane = lax.broadcasted_iota(jnp.int32, (R, HP), 1)
    first_head = pair_lane < N
    ss = []
    for p in range(n_pairs):
        sq = kk2[:, p * HP:(p + 1) * HP]
        s0 = jnp.sum(jnp.where(first_head, sq, 0.0), axis=-1, keepdims=True)
        s1 = jnp.sum(jnp.where(first_head, 0.0, sq), axis=-1, keepdims=True)
        ss.append(jnp.where(first_head, s0, s1))
    kk = kk * lax.rsqrt(jnp.maximum(jnp.concatenate(ss, axis=1), 1e-24))
    k2 = k * (1.0 + (a - 1.0) * ka_ref[...])
    a_s = -kk
    b_s = kk * a

    rr = lax.broadcasted_iota(jnp.int32, (R, R), 0)
    cc = lax.broadcasted_iota(jnp.int32, (R, R), 1)
    same_chunk = (rr // C) == (cc // C)
    L = _dot_exact_lhs((same_chunk & (cc <= rr)).astype(BF16), lw)
    bonus_w = r * k2 * rk_ref[...]

    P2 = 2 * C
    sr = lax.broadcasted_iota(jnp.int32, (P2, HP), 0)
    sc = lax.broadcasted_iota(jnp.int32, (P2, HP), 1)
    stack_mask = (sr < C) == (sc < N)
    br = lax.broadcasted_iota(jnp.int32, (P2, P2), 0)
    bc = lax.broadcasted_iota(jnp.int32, (P2, P2), 1)
    same_head = (br < C) == (bc < C)
    tr = jnp.where(br >= C, br - C, br)
    tc = jnp.where(bc >= C, bc - C, bc)
    strict = same_head & (tc < tr)
    lower = same_head & (tc <= tr)
    eye_p = (br == bc).astype(F32)
    kr = lax.broadcasted_iota(jnp.int32, (HP, HP), 0)
    kc = lax.broadcasted_iota(jnp.int32, (HP, HP), 1)
    eye_k = kr == kc

    def dup(x):
        return jnp.concatenate([x, x], axis=0)

    def stack(x):
        return jnp.where(stack_mask, dup(x), 0.0)

    n_sq = int(math.log2(C)) - 1
    items = [(c, p) for c in range(nch) for p in range(n_pairs)]

    def prep(c, p):
        rows = slice(c * C, (c + 1) * C)
        lanes = slice(p * HP, (p + 1) * HP)
        Lc = L[rows, lanes]
        lwc = lw[rows, lanes]
        LC = Lc[C - 1:C, :]
        enL = jnp.exp(-Lc)
        eCL = jnp.exp(LC - Lc)
        b_c, k_c = b_s[rows, lanes], k2[rows, lanes]
        return dict(
            xa=stack(a_s[rows, lanes] * jnp.exp(Lc - lwc)), xr=stack(r[rows, lanes] * jnp.exp(Lc)),
            bt=dup(b_c * enL), kt=dup(k_c * enL), bh=stack(b_c * eCL), kh=stack(k_c * eCL),
            vs=stack(v[rows, lanes]), pc=jnp.exp(LC))

    d = [prep(c, p) for c, p in items]
    aa = [_dot_nt(jnp.concatenate([e["xa"], e["xr"]], axis=0),
                  jnp.concatenate([e["bt"], e["kt"]], axis=0)) for e in d]
    a_ab = [jnp.where(strict, m[:P2, :P2], 0.0) for m in aa]
    a_ak = [jnp.where(strict, m[:P2, P2:], 0.0) for m in aa]
    a_rb = [jnp.where(lower, m[P2:, :P2], 0.0) for m in aa]
    a_rk = [jnp.where(lower, m[P2:, P2:], 0.0) for m in aa]
    tm = [eye_p + m for m in a_ab]
    npw = a_ab
    for _ in range(n_sq):
        npw = [_dot(m, m) for m in npw]
        tm = [t + _dot(t, m) for t, m in zip(tm, npw)]
    av = [_dot(m, e["vs"]) for m, e in zip(a_ak, d)]
    z = [_dot(t, jnp.concatenate([e["xa"], x], axis=1)) for t, e, x in zip(tm, d, av)]
    zero_v = jnp.zeros((P2, HP), F32)
    rhs = [jnp.concatenate([x, jnp.concatenate([zero_v, e["vs"]], axis=1)], axis=0) for e, x in zip(d, z)]
    lhs = [jnp.concatenate([jnp.concatenate([m1, m2], axis=1),
                            jnp.concatenate([e["bh"], e["kh"]], axis=0).T], axis=0)
           for m1, m2, e in zip(a_rb, a_rk, d)]
    wg = [_dot(a, b) for a, b in zip(lhs, rhs)]

    for idx, (c, p) in enumerate(items):
        e = d[idx]
        rows = slice(c * C, (c + 1) * C)
        lanes = slice(p * HP, (p + 1) * HP)
        rp = e["xr"] + wg[idx][:P2, :HP]
        y0 = wg[idx][:P2, HP:]
        gm = wg[idx][P2:, :HP]
        hm = wg[idx][P2:, HP:]
        st = state_ref[p]
        yg = _dot(jnp.concatenate([rp, gm], axis=0), st)
        pc_col = jnp.sum(jnp.where(eye_k, e["pc"], 0.0), axis=1, keepdims=True)
        state_ref[p] = pc_col * st + yg[P2:] + hm
        ys = yg[:P2] + y0
        mu = jnp.sum(ys, axis=-1, keepdims=True) * (1.0 / N)
        yc = jnp.where(stack_mask, ys - mu, 0.0)
        var = jnp.sum(yc * yc, axis=-1, keepdims=True) * (1.0 / N)
        yn = yc * lax.rsqrt(var + GN_EPS)
        bonus = jnp.sum(stack(bonus_w[rows, lanes]), axis=-1, keepdims=True) * e["vs"]
        yn = yn[:C] + yn[C:]
        bonus = bonus[:C] + bonus[C:]
        o_ref[rows, lanes] = ((yn * gnw_ref[:, lanes] + gnb_ref[:, lanes] + bonus)
                              * gg_ref[rows, lanes]).astype(o_ref.dtype)


def _rwkv(zr, zk, zv, wp, ap, gg, mu_rkv, w0, a0, k_k, k_a, r_k, gn_w, gn_b, B, S):
    T = zr.shape[0]
    C = min(RWKV_ROWS, S)
    nc = S // C
    W = RWKV_WIDTH
    HP = 2 * RWKV_HEAD
    row = lambda b, i: (b * nc + i, 0)
    full = lambda b, i: (0, 0)

    def fs(a):
        return pl.BlockSpec(a.shape, full)

    return pl.pallas_call(
        _rwkv_kernel,
        grid=(B, nc),
        in_specs=[pl.BlockSpec((C, W), row)] * 6 + [fs(mu_rkv), fs(w0), fs(a0), fs(k_k), fs(k_a),
                                                    fs(r_k), fs(gn_w), fs(gn_b)],
        out_specs=pl.BlockSpec((C, W), row),
        out_shape=jax.ShapeDtypeStruct((T, W), BF16),
        scratch_shapes=[pltpu.VMEM((W // HP, HP, HP), F32),
                        pltpu.VMEM((V7X_SUBLANES, 3 * W), F32)],
        compiler_params=_cparams(("arbitrary", "arbitrary")),
        name="rwkv7",
    )(zr, zk, zv, wp, ap, gg, mu_rkv, w0, a0, k_k, k_a, r_k, gn_w, gn_b)


ROUTE_COLS = 8
ROUTER_GROUP_LANE = 0
ROUTER_EXPERT_LANE = N_GROUPS


def _out_router_kernel(x_ref, od_ref, orw_ref, wo_ref, mg_ref, wrh_ref, wrl_ref,
                       x1_out, hm_out, route_out, cnt_out, sl_out):
    tm = PROJ_ROWS if x_ref.shape[0] >= PROJ_ROWS else x_ref.shape[0]
    tiles = [slice(t * tm, (t + 1) * tm) for t in range(x_ref.shape[0] // tm)]
    x1 = [x_ref[r, :] + jnp.dot(od_ref[r, :], wo_ref[0:DIFF_WIDTH, :], preferred_element_type=F32)
          + jnp.dot(orw_ref[r, :], wo_ref[DIFF_WIDTH:, :], preferred_element_type=F32) for r in tiles]
    hm = [_rms(v, mg_ref[...], NORM_EPS) for v in x1]
    for r, v, h in zip(tiles, x1, hm):
        x1_out[r, :] = v
        hm_out[r, :] = h.astype(hm_out.dtype)

    def logits(h):
        hi = h.astype(BF16)
        lo = (h - hi.astype(F32)).astype(BF16)
        return (jnp.dot(hi, wrh_ref[...], preferred_element_type=F32)
                + jnp.dot(hi, wrl_ref[...], preferred_element_type=F32)
                + jnp.dot(lo, wrh_ref[...], preferred_element_type=F32))

    lgs = [logits(h) for h in hm]
    lane = lax.broadcasted_iota(jnp.int32, (tm, V7X_LANES), 1).astype(F32)
    big = float(V7X_LANES)
    ninf = -jnp.inf
    gmask = lane < N_GROUPS

    def choose(lg):
        gmax = jnp.max(jnp.where(gmask, lg, ninf), axis=-1, keepdims=True)
        g_sel = jnp.min(jnp.where(gmask & (lg == gmax), lane, big), axis=-1, keepdims=True)
        g_w = 1.0 / jnp.sum(jnp.where(gmask, jnp.exp(lg - gmax), 0.0), axis=-1, keepdims=True)
        lo_lane = ROUTER_EXPERT_LANE + g_sel * EXPERTS_PER_GROUP
        emask = (lane >= lo_lane) & (lane < lo_lane + EXPERTS_PER_GROUP)
        v1 = jnp.max(jnp.where(emask, lg, ninf), axis=-1, keepdims=True)
        i1 = jnp.min(jnp.where(emask & (lg == v1), lane, big), axis=-1, keepdims=True)
        emask2 = emask & (lane != i1)
        v2 = jnp.max(jnp.where(emask2, lg, ninf), axis=-1, keepdims=True)
        i2 = jnp.min(jnp.where(emask2 & (lg == v2), lane, big), axis=-1, keepdims=True)
        e2x = jnp.exp(v2 - v1)
        den = 1.0 + e2x
        return i1, i2, (1.0 / den) * g_w, (e2x / den) * g_w

    chosen = [choose(lg) for lg in lgs]

    ri = lax.broadcasted_iota(jnp.int32, (tm, tm), 0)
    ci = lax.broadcasted_iota(jnp.int32, (tm, tm), 1)
    before = (ci < ri).astype(BF16)
    li = lax.broadcasted_iota(jnp.int32, (V7X_LANES, V7X_LANES), 0)
    lj = lax.broadcasted_iota(jnp.int32, (V7X_LANES, V7X_LANES), 1)
    lanes_before = (li < lj).astype(BF16)
    ohs = [((lane == i1) | (lane == i2)).astype(F32) for i1, i2, _, _ in chosen]
    prefixes = [jnp.dot(before, oh.astype(BF16), preferred_element_type=F32) for oh in ohs]
    counts = [jnp.broadcast_to(jnp.sum(oh, axis=0, keepdims=True), (V7X_SUBLANES, V7X_LANES)) for oh in ohs]
    padded = [jnp.floor((c + (RUN_ALIGN - 1)) * (1.0 / RUN_ALIGN)) * RUN_ALIGN for c in counts]
    starts = [jnp.dot(pc.astype(BF16), lanes_before, preferred_element_type=F32) for pc in padded]
    col = lax.broadcasted_iota(jnp.int32, (tm, V7X_LANES), 1)
    for t, (r, (i1, i2, wt1, wt2), cnt, start, prefix) in enumerate(zip(tiles, chosen, counts, starts, prefixes)):
        row = prefix + start[0:1, :]
        sl1 = jnp.sum(jnp.where(lane == i1, row, 0.0), axis=-1, keepdims=True)
        sl2 = jnp.sum(jnp.where(lane == i2, row, 0.0), axis=-1, keepdims=True)
        cnt_out[t * V7X_SUBLANES:(t + 1) * V7X_SUBLANES, :] = cnt
        wide = jnp.where(col == 0, sl1, jnp.where(col == 1, sl2, jnp.where(col == 2, wt1, jnp.where(
            col == 3, wt2, 0.0))))
        route_out[r, :] = wide[:, 0:ROUTE_COLS]
        sl_out[t] = wide.T[0:V7X_SUBLANES, :].astype(sl_out.dtype)


def _out_router(x2, od, orw, w_out, moe_g, wr_hi, wr_lo):
    T, D = x2.shape
    tm = min(PROJ_ROWS, T)
    nt = T // tm
    tps = TILES_PER_STEP if nt % TILES_PER_STEP == 0 else 1
    tr = tm * tps
    row = lambda i: (i, 0)
    full = lambda i: (0, 0)

    def fs(a):
        return pl.BlockSpec(a.shape, full)

    return pl.pallas_call(
        _out_router_kernel,
        grid=(nt // tps,),
        in_specs=[pl.BlockSpec((tr, D), row), pl.BlockSpec((tr, DIFF_WIDTH), row),
                  pl.BlockSpec((tr, RWKV_WIDTH), row), fs(w_out), fs(moe_g), fs(wr_hi), fs(wr_lo)],
        out_specs=[pl.BlockSpec((tr, D), row), pl.BlockSpec((tr, D), row),
                   pl.BlockSpec((tr, ROUTE_COLS), row), pl.BlockSpec((tps * V7X_SUBLANES, V7X_LANES), row),
                   pl.BlockSpec((tps, V7X_SUBLANES, tm), lambda i: (i, 0, 0))],
        out_shape=[jax.ShapeDtypeStruct((T, D), F32), jax.ShapeDtypeStruct((T, D), BF16),
                   jax.ShapeDtypeStruct((T, ROUTE_COLS), F32),
                   jax.ShapeDtypeStruct((nt * V7X_SUBLANES, V7X_LANES), F32),
                   jax.ShapeDtypeStruct((nt, V7X_SUBLANES, tm), jnp.int32)],
        compiler_params=_cparams(("arbitrary",)),
        name="out_router",
    )(x2, od, orw, w_out, moe_g, wr_hi, wr_lo)


RUN_ALIGN = V7X_SUBLANES
RUN_CLASSES = 6
TAB_COUNT, TAB_NUSED = 0, RUN_CLASSES
TAB_SRC = V7X_SUBLANES
TAB_DST = TAB_SRC + RUN_CLASSES * N_EXPERTS
TAB_WORDS = 4 * V7X_LANES


def _stage_rows(tm):
    need = 2 * tm + N_EXPERTS * (RUN_ALIGN - 1)
    return -(-need // V7X_LANES) * V7X_LANES


def _run_table(src, dst, length, spare):
    i32 = jnp.int32
    n = src.shape[0]
    cls = jnp.arange(RUN_CLASSES, dtype=i32)[None, :, None]
    ln = length[:, None, :]
    has = ((ln >> 3) >> cls) & 1
    off = ln & ~((2 * RUN_ALIGN << cls) - 1)
    pos = jnp.cumsum(has, axis=-1) - 1
    k = jnp.arange(N_EXPERTS, dtype=i32)
    pick = (has[..., None] == 1) & (pos[..., None] == k)

    def compact(v):
        return jnp.sum(jnp.where(pick, (v[:, None, :] + off)[..., None], 0), axis=2).reshape(n, -1)

    head = jnp.concatenate([jnp.sum(has, axis=-1), jnp.broadcast_to(spare, (n, 1)).astype(i32),
                            jnp.zeros((n, TAB_SRC - RUN_CLASSES - 1), i32)], axis=1)
    tab = jnp.concatenate([head, compact(src), compact(dst)], axis=1)
    return jnp.pad(tab, ((0, 0), (0, TAB_WORDS - tab.shape[1]))).reshape(n, 1, TAB_WORDS)


def _run_copies(tab_ref, tile, make_copy, op):
    for c in range(RUN_CLASSES):
        size = RUN_ALIGN << c

        def one(k, carry, c=c, size=size):
            src = tab_ref[tile, 0, TAB_SRC + c * N_EXPERTS + k]
            dst = tab_ref[tile, 0, TAB_DST + c * N_EXPERTS + k]
            getattr(make_copy(pl.multiple_of(src, RUN_ALIGN), pl.multiple_of(dst, RUN_ALIGN), size), op)()
            return carry

        lax.fori_loop(0, tab_ref[tile, 0, TAB_COUNT + c], one, 0)


def _dispatch_kernel(tab_ref, tail_ref, sl_ref, hm_ref, xb_ref, stage_ref, zero_ref, sem, zsem):
    tps, ns = stage_ref.shape[0], stage_ref.shape[1]
    tm = hm_ref.shape[0] // tps
    bm = zero_ref.shape[0]
    nb = xb_ref.shape[0] // bm
    n_used = tail_ref[0, 0, TAB_NUSED]

    def tile_copy(t):
        def make_copy(src, dst, size):
            return pltpu.make_async_copy(stage_ref.at[t, pl.ds(src, size)], xb_ref.at[pl.ds(dst, size)], sem)
        return make_copy

    def zero_copy(src, dst, size):
        return pltpu.make_async_copy(zero_ref.at[pl.ds(src, size)], xb_ref.at[pl.ds(dst, size)], zsem)

    def zero_blocks(op):
        def one(b, c):
            getattr(zero_copy(0, pl.multiple_of(b * bm, bm), bm), op)()
            return c
        lax.fori_loop(n_used, nb, one, 0)

    @pl.when(pl.program_id(0) == 0)
    def _():
        zero_ref[...] = jnp.zeros_like(zero_ref)
        _run_copies(tail_ref, 0, zero_copy,"start")
        zero_blocks("start")

    srow = lax.broadcasted_iota(jnp.int32, (ns, tm), 0)
    for t in range(tps):
        sel = (srow == sl_ref[t, 0:1, :]) | (srow == sl_ref[t, 1:2, :])
        stage_ref[t] = _pack16(jnp.dot(sel.astype(BF16), hm_ref[t * tm:(t + 1) * tm, :],
                                       preferred_element_type=F32))
        _run_copies(tab_ref, t, tile_copy(t),"start")
    for t in range(tps):
        _run_copies(tab_ref, t, tile_copy(t),"wait")

    @pl.when(pl.program_id(0) == pl.num_programs(0) - 1)
    def _():
        _run_copies(tail_ref, 0, zero_copy,"wait")
        zero_blocks("wait")


def _dispatch(tab, tail, sl_rows, hm, P):
    T, D = hm.shape
    tm = min(PROJ_ROWS, T)
    nt = T // tm
    tps = TILES_PER_STEP if nt % TILES_PER_STEP == 0 else 1
    return pl.pallas_call(
        _dispatch_kernel,
        grid=(nt // tps,),
        in_specs=[pl.BlockSpec((tps, 1, tab.shape[-1]), lambda i: (i, 0, 0), memory_space=pltpu.SMEM),
                  pl.BlockSpec((1, 1, tail.shape[-1]), lambda i: (0, 0, 0), memory_space=pltpu.SMEM),
                  pl.BlockSpec((tps, V7X_SUBLANES, tm), lambda i: (i, 0, 0)),
                  pl.BlockSpec((tps * tm, D), lambda i: (i, 0))],
        out_specs=pl.BlockSpec(memory_space=pl.ANY),
        out_shape=jax.ShapeDtypeStruct((P, D // 2), PACKED),
        scratch_shapes=[pltpu.VMEM((tps, _stage_rows(tm), D // 2), PACKED),
                        pltpu.VMEM((MOE_ROWS, D // 2), PACKED),
                        pltpu.SemaphoreType.DMA(()), pltpu.SemaphoreType.DMA(())],
        compiler_params=_cparams(("arbitrary",)),
        name="moe_dispatch",
    )(tab, tail, sl_rows, hm)


def _expert_kernel(be_ref, nu_ref, xb_ref, wg_ref, wu_ref, wd_ref, yb_ref, wgb, wub, wdb):
    i = pl.program_id(0)
    prev = be_ref[jnp.maximum(i - 1, 0)]
    changed = (i == 0) | (be_ref[i] != prev)

    @pl.when(changed)
    def _():
        wgb[...] = wg_ref[0].astype(BF16)
        wub[...] = wu_ref[0].astype(BF16)
        wdb[...] = wd_ref[0].astype(BF16)

    @pl.when(i < nu_ref[0])
    def _():
        xb = _unpack16(xb_ref[...]).astype(BF16)
        gate = jnp.dot(xb, wgb[...], preferred_element_type=F32)
        up = jnp.dot(xb, wub[...], preferred_element_type=F32)
        hdn = (gate * jax.nn.sigmoid(gate)) * up
        y = jnp.dot(hdn.astype(BF16), wdb[...], preferred_element_type=F32)
        yb_ref[...] = _pack16(y.astype(BF16).astype(F32))

    @pl.when(i >= nu_ref[0])
    def _():
        yb_ref[...] = jnp.zeros_like(yb_ref)


def _experts(block_e, n_used, xb, w_gate, w_up, w_down):
    P, DP = xb.shape
    D = 2 * DP
    bm = MOE_ROWS
    nb = P // bm
    E = D_EXPERT
    grid_spec = pltpu.PrefetchScalarGridSpec(
        num_scalar_prefetch=2,
        grid=(nb,),
        in_specs=[pl.BlockSpec((bm, DP), lambda i, be, nu: (jnp.minimum(i, nu[0] - 1), 0)),
                  pl.BlockSpec((1, D, E), lambda i, be, nu: (be[i], 0, 0)),
                  pl.BlockSpec((1, D, E), lambda i, be, nu: (be[i], 0, 0)),
                  pl.BlockSpec((1, E, D), lambda i, be, nu: (be[i], 0, 0))],
        out_specs=pl.BlockSpec((bm, DP), lambda i, be, nu: (jnp.where(i < nu[0], i, nb - 1), 0)),
        scratch_shapes=[pltpu.VMEM((D, E), BF16), pltpu.VMEM((D, E), BF16), pltpu.VMEM((E, D), BF16)],
    )
    return pl.pallas_call(
        _expert_kernel,
        grid_spec=grid_spec,
        out_shape=jax.ShapeDtypeStruct((P, DP), PACKED),
        input_output_aliases={2: 0},
        compiler_params=_cparams(("arbitrary",)),
        name="moe_experts",
    )(block_e, n_used, xb, w_gate, w_up, w_down)


def _final_kernel(tab_ref, x1_ref, route_ref, p_ref, wpl_ref, plg_ref, pgg_ref, wgate_ref, yb_ref,
                  o_ref, stage_ref, sem):
    tps, ns = stage_ref.shape[0], stage_ref.shape[1]
    tm = x1_ref.shape[0] // tps
    tiles = [slice(t * tm, (t + 1) * tm) for t in range(tps)]
    stage_ref[...] = jnp.zeros_like(stage_ref)

    def tile_copy(t):
        def make_copy(src, dst, size):
            return pltpu.make_async_copy(yb_ref.at[pl.ds(dst, size)], stage_ref.at[t, pl.ds(src, size)], sem)
        return make_copy

    for t in range(tps):
        _run_copies(tab_ref, t, tile_copy(t),"start")
    pe = [_rms(jnp.dot(p_ref[r, :].astype(BF16), wpl_ref[...], preferred_element_type=F32),
               plg_ref[...], NORM_EPS) for r in tiles]
    for t in range(tps):
        _run_copies(tab_ref, t, tile_copy(t),"wait")

    scol = lax.broadcasted_iota(jnp.int32, (tm, ns), 1).astype(F32)

    def selection(r):
        route = route_ref[r, :]
        wsel = (jnp.where(scol == route[:, 0:1], route[:, 2:3], 0.0)
                + jnp.where(scol == route[:, 1:2], route[:, 3:4], 0.0))
        w_hi = wsel.astype(BF16)
        return w_hi, (wsel - w_hi.astype(F32)).astype(BF16)

    wsel = [selection(r) for r in tiles]
    yb16 = [_unpack16(stage_ref[t]).astype(BF16) for t in range(tps)]
    moe = [jnp.dot(w_hi, y, preferred_element_type=F32) + jnp.dot(w_lo, y, preferred_element_type=F32)
           for (w_hi, w_lo), y in zip(wsel, yb16)]
    x2 = [x1_ref[r, :] + m for r, m in zip(tiles, moe)]
    gate = [jax.nn.sigmoid(_dot(_rms(v, pgg_ref[...], NORM_EPS), wgate_ref[...])) for v in x2]
    for r, v, e, g in zip(tiles, x2, pe, gate):
        o_ref[r, :] = v + e * g


def _final(tab, x1, route, p2, w_pl, pl_g, pl_gate_g, w_gate, yb):
    T, D = x1.shape
    tm = min(PROJ_ROWS, T)
    nt = T // tm
    tps = TILES_PER_STEP if nt % TILES_PER_STEP == 0 else 1
    tr = tps * tm
    row = lambda i: (i, 0)
    full = lambda i: (0, 0)

    def fs(a):
        return pl.BlockSpec(a.shape, full)

    return pl.pallas_call(
        _final_kernel,
        grid=(nt // tps,),
        in_specs=[pl.BlockSpec((tps, 1, tab.shape[-1]), lambda i: (i, 0, 0), memory_space=pltpu.SMEM),
                  pl.BlockSpec((tr, D), row), pl.BlockSpec((tr, ROUTE_COLS), row),
                  pl.BlockSpec((tr, PL_DIM), row), fs(w_pl), fs(pl_g), fs(pl_gate_g), fs(w_gate),
                  pl.BlockSpec(memory_space=pl.ANY)],
        out_specs=pl.BlockSpec((tr, D), row),
        out_shape=jax.ShapeDtypeStruct((T, D), F32),
        scratch_shapes=[pltpu.VMEM((tps, _stage_rows(tm), D // 2), PACKED), pltpu.SemaphoreType.DMA(())],
        compiler_params=_cparams(("arbitrary",)),
        name="combine_final",
    )(tab, x1, route, p2, w_pl, pl_g, pl_gate_g, w_gate, yb)


def kernel(x, p, positions, attn_norm_g, w_in, q_norm_g, k_norm_g, lambda_q1, lambda_k1, lambda_q2, lambda_k2, diff_out_g, mu_rkv, mu_wag, w0, w_lora_a, w_lora_b, a0, a_lora_a, a_lora_b, g_lora_a, g_lora_b, k_k, k_a, r_k, gn_w, gn_b, w_out, moe_norm_g, w_group, w_expert_router, w_gate, w_up, w_down, w_pl, pl_norm_g, pl_gate_norm_g, w_pl_gate):
    B, S, D = x.shape
    T = B * S
    assert p.shape[0] == 1, "one layer"
    x2 = x.reshape(T, D)
    p2 = p[0].reshape(T, PL_DIM)
    pos = positions.astype(F32).reshape(T, 1)

    def row(a):
        return a.reshape(1, -1).astype(F32)

    half = DIFF_HEAD_DIM // 2
    inv_freq = ROPE_THETA ** (-jnp.arange(half, dtype=F32) / half)
    invf = jnp.tile(inv_freq, V7X_LANES // half).reshape(1, V7X_LANES)
    li = jnp.arange(DIFF_WIDTH) // DIFF_HEAD_DIM
    seg = (li[:, None] == li[None, :]).astype(BF16)
    reps = DIFF_WIDTH // DIFF_HEAD_DIM

    q, k, v, zr, zk, zv, wp, ap, gg = _in_proj(
        x2, pos, row(attn_norm_g[0]), w_in[0].astype(BF16),
        w_lora_a[0].astype(BF16), a_lora_a[0].astype(BF16), g_lora_a[0].astype(BF16),
        w_lora_b[0].astype(BF16), a_lora_b[0].astype(BF16), g_lora_b[0].astype(BF16),
        mu_wag[0].astype(F32), row(jnp.tile(q_norm_g[0], reps)), row(jnp.tile(k_norm_g[0], reps)),
        invf, seg, B, S)

    o_diff = _attention(q, k, v, row(lambda_q1[0]), row(lambda_k1[0]), row(lambda_q2[0]),
                        row(lambda_k2[0]), diff_out_g[0].astype(F32).reshape(-1, 1), B, S)
    o_rwkv = _rwkv(zr, zk, zv, wp, ap, gg, mu_rkv[0].astype(F32), row(w0[0]), row(a0[0]),
                   row(k_k[0]), row(k_a[0]), row(r_k[0]), row(gn_w[0]), row(gn_b[0]), B, S)

    wr = jnp.concatenate([w_group[0], jnp.transpose(w_expert_router[0], (1, 0, 2)).reshape(D, N_EXPERTS)],
                         axis=1).astype(F32)
    wr = jnp.pad(wr, ((0, 0), (0, V7X_LANES - wr.shape[1])))
    wr_hi = wr.astype(BF16)
    wr_lo = (wr - wr_hi.astype(F32)).astype(BF16)
    x1, hm, route, cnt, sl_rows = _out_router(x2, o_diff, o_rwkv, w_out[0].astype(BF16), row(moe_norm_g[0]),
                                     wr_hi, wr_lo)

    bm = MOE_ROWS
    tm = min(PROJ_ROWS, T)
    nt = T // tm
    i32 = jnp.int32
    cnt_te = cnt.reshape(nt, V7X_SUBLANES, V7X_LANES)[:, 0, ROUTER_EXPERT_LANE:ROUTER_EXPERT_LANE + N_EXPERTS]
    len_te = (cnt_te.astype(i32) + RUN_ALIGN - 1) // RUN_ALIGN * RUN_ALIGN
    tot_e = jnp.sum(len_te, axis=0)
    pcounts = (tot_e + bm - 1) // bm * bm
    pends = jnp.cumsum(pcounts)
    pstarts = pends - pcounts
    dst_te = pstarts[None, :] + jnp.cumsum(len_te, axis=0) - len_te
    src_te = jnp.cumsum(len_te, axis=1) - len_te
    tab = _run_table(src_te, dst_te, len_te, 0)
    nb = -(-(2 * T + nt * N_EXPERTS * (RUN_ALIGN - 1)) // bm) + N_EXPERTS
    P = nb * bm
    n_used = (pends[-1] // bm).astype(i32).reshape(1)
    block_start = jnp.arange(nb, dtype=i32) * bm
    block_e = jnp.minimum(jnp.sum((pends[None, :] <= block_start[:, None]).astype(i32), axis=1),
                          N_EXPERTS - 1)
    tail = _run_table(jnp.zeros((1, N_EXPERTS), i32), (pstarts + tot_e)[None, :], (pcounts - tot_e)[None, :],
                      n_used)
    xb = _dispatch(tab, tail, sl_rows, hm, P)
    yb = _experts(block_e, n_used, xb, w_gate[0], w_up[0], w_down[0])
    out = _final(tab, x1, route, p2, w_pl[0].astype(BF16), row(pl_norm_g[0]), row(pl_gate_norm_g[0]),
                 w_pl_gate[0].astype(BF16), yb)
    return out.reshape(B, S, D)
```

```python
import functools
import math

import jax
import jax.numpy as jnp
from jax import lax
from jax.experimental import pallas as pl
from jax.experimental.pallas import tpu as pltpu

F32 = jnp.float32
BF16 = jnp.bfloat16

D_MODEL = 1024
PL_DIM = 256
DIFF_WIDTH = 512
RWKV_WIDTH = 512
DIFF_HEAD_DIM = 64
DIFF_HEADS = 4
RWKV_HEAD = 64
RWKV_HEADS = 8
D_DECAY_LORA = 64
D_AAA_LORA = 64
D_GATE_LORA = 160
ROPE_THETA = 10000.0
NORM_EPS = 1e-6
SUBLN_EPS = 1e-5
GN_EPS = 64e-5
N_GROUPS = 4
EXPERTS_PER_GROUP = 8
N_EXPERTS = 32
D_EXPERT = 512
LAM_INIT = 0.8 - 0.6 * math.exp(0.0)

V7X_LANES = 128
V7X_SUBLANES = 8
V7X_VMEM_BYTES = 64 * 1024 * 1024

IN_PROJ_ROWS = 256
IN_PROJ_TILES_PER_STEP = 2
PROJ_ROWS = 256
TILES_PER_STEP = 4
ATTN_ROWS = 256
ATTN_UNROLL = 4
ATTN_HEADS_PER_STEP = 2
RWKV_CHUNK = 64
RWKV_ROWS = 256
MOE_ROWS = 512
VMEM_LIMIT = 56 * 1024 * 1024


def _cparams(sem):
    return pltpu.CompilerParams(dimension_semantics=sem, vmem_limit_bytes=VMEM_LIMIT)


def _dot(a, b):
    return jnp.dot(a.astype(BF16), b.astype(BF16), preferred_element_type=F32)


def _dot_nt(a, b):
    return lax.dot_general(a.astype(BF16), b.astype(BF16), (((1,), (1,)), ((), ())),
                           preferred_element_type=F32)


def _dot_tn(a, b):
    return lax.dot_general(a.astype(BF16), b.astype(BF16), (((0,), (0,)), ((), ())),
                           preferred_element_type=F32)


def _split3(x):
    hi = x.astype(BF16)
    r1 = x - hi.astype(F32)
    mid = r1.astype(BF16)
    lo = (r1 - mid.astype(F32)).astype(BF16)
    return hi, mid, lo


def _dot_exact_lhs(m01, x):
    hi, mid, lo = _split3(x)
    return (jnp.dot(m01, hi, preferred_element_type=F32)
            + jnp.dot(m01, mid, preferred_element_type=F32)
            + jnp.dot(m01, lo, preferred_element_type=F32))


PACKED = jnp.uint32


def _pack16(x):
    h = x.shape[1] // 2
    lo = lax.bitcast_convert_type(x[:, :h], PACKED) >> 16
    hi = lax.bitcast_convert_type(x[:, h:], PACKED) & jnp.uint32(0xFFFF0000)
    return hi | lo


def _unpack16(w):
    lo = lax.bitcast_convert_type(w << 16, F32)
    hi = lax.bitcast_convert_type(w & jnp.uint32(0xFFFF0000), F32)
    return jnp.concatenate([lo, hi], axis=1)


def _sumsq_64(z):
    m, n = z.shape
    group = V7X_LANES // 2
    first = lax.broadcasted_iota(jnp.int32, (m, V7X_LANES), 1) < group
    out = []
    for c in range(0, n, V7X_LANES):
        sq = z[:, c:c + V7X_LANES] * z[:, c:c + V7X_LANES]
        s0 = jnp.sum(jnp.where(first, sq, 0.0), axis=-1, keepdims=True)
        s1 = jnp.sum(jnp.where(first, 0.0, sq), axis=-1, keepdims=True)
        out.append(jnp.where(first, s0, s1))
    return jnp.concatenate(out, axis=1)


def _rms(x, g, eps):
    return x * lax.rsqrt(jnp.mean(x * x, axis=-1, keepdims=True) + eps) * g


def _shift_rows(z, prev_row):
    rolled = pltpu.roll(z, 1, axis=0)
    row = lax.broadcasted_iota(jnp.int32, z.shape, 0)
    return jnp.where(row == 0, prev_row, rolled)


def _in_proj_kernel(x_ref, pos_ref, g_ref, win_ref, w1_ref, a1_ref, g1_ref, w2_ref, a2_ref, g2_ref,
                    mu_ref, qg_ref, kg_ref, invf_ref,
                    q_out, k_out, v_out, zr_out, zk_out, zv_out, wp_out, ap_out, gg_out,
                    carry_ref):
    tm = IN_PROJ_ROWS if x_ref.shape[0] >= IN_PROJ_ROWS else x_ref.shape[0]
    tiles = [slice(t * tm, (t + 1) * tm) for t in range(x_ref.shape[0] // tm)]
    w = DIFF_WIDTH
    half = DIFF_HEAD_DIM // 2

    @pl.when(pl.program_id(1) == 0)
    def _():
        carry_ref[...] = jnp.zeros_like(carry_ref)

    prev = carry_ref[V7X_SUBLANES - 1:V7X_SUBLANES, :]
    hns, dhs = [], []
    for r in tiles:
        hn = _rms(x_ref[r, :], g_ref[...], NORM_EPS)
        dhs.append(_shift_rows(hn, prev) - hn)
        hns.append(hn)
        prev = hn[tm - 1:tm, :]
    carry_ref[...] = hns[-1][tm - V7X_SUBLANES:tm, :]

    zq, zk, low = [], [], []
    for r, hn, dh in zip(tiles, hns, dhs):
        hb = hn.astype(BF16)

        def proj(c, hb=hb):
            return jnp.dot(hb, win_ref[:, c * w:(c + 1) * w], preferred_element_type=F32)

        zq.append(proj(0))
        zk.append(proj(1))
        low.append((_dot(hn + dh * mu_ref[0:1, :], w1_ref[...]), _dot(hn + dh * mu_ref[1:2, :], a1_ref[...]),
                    _dot(hn + dh * mu_ref[2:3, :], g1_ref[...])))
        v_out[r, :] = proj(2).astype(v_out.dtype)
        zr_out[r, :] = proj(3)
        zk_out[r, :] = proj(4)
        zv_out[r, :] = proj(5)

    lane = lax.broadcasted_iota(jnp.int32, (tm, w), 1)
    first_half = (lane % DIFF_HEAD_DIM) < half
    scale = DIFF_HEAD_DIM ** -0.5 * math.log2(math.e)
    for r, q, k, (lw, la, lg) in zip(tiles, zq, zk, low):
        ang = pos_ref[r, :] * invf_ref[...]
        cosf = jnp.concatenate([jnp.cos(ang)] * (w // V7X_LANES), axis=1)
        sinf = jnp.concatenate([jnp.sin(ang)] * (w // V7X_LANES), axis=1)
        sin_signed = jnp.where(first_half, -sinf, sinf)

        def norm_rope(z, gain, cosf=cosf, sin_signed=sin_signed):
            zn = z * lax.rsqrt(_sumsq_64(z) * (1.0 / DIFF_HEAD_DIM) + NORM_EPS) * gain
            partner = jnp.where(first_half, pltpu.roll(zn, w - half, axis=1), pltpu.roll(zn, half, axis=1))
            return zn * cosf + partner * sin_signed

        q_out[r, :] = (norm_rope(q, qg_ref[...]) * scale).astype(q_out.dtype)
        k_out[r, :] = norm_rope(k, kg_ref[...]).astype(k_out.dtype)
        wp_out[r, :] = _dot(jnp.tanh(lw), w2_ref[...])
        ap_out[r, :] = _dot(la, a2_ref[...])
        gg_out[r, :] = _dot(jax.nn.sigmoid(lg), g2_ref[...])


def _in_proj(x2, pos, attn_g, w_in, w1, a1, g1, w2, a2, g2, mu_wag, qg, kg, invf, B, S):
    T, D = x2.shape
    tile = min(IN_PROJ_ROWS, S)
    tps = IN_PROJ_TILES_PER_STEP if (S // tile) % IN_PROJ_TILES_PER_STEP == 0 else 1
    tm = tile * tps
    ns = S // tm
    w = DIFF_WIDTH
    row = lambda b, i: (b * ns + i, 0)
    full = lambda b, i: (0, 0)

    def fs(a):
        return pl.BlockSpec(a.shape, full)

    outs = ([jax.ShapeDtypeStruct((T, w), BF16)] * 3 + [jax.ShapeDtypeStruct((T, w), F32)] * 6)
    return pl.pallas_call(
        _in_proj_kernel,
        grid=(B, ns),
        in_specs=[pl.BlockSpec((tm, D), row), pl.BlockSpec((tm, 1), row), fs(attn_g), fs(w_in),
                  fs(w1), fs(a1), fs(g1), fs(w2), fs(a2), fs(g2), fs(mu_wag), fs(qg), fs(kg),
                  fs(invf)],
        out_specs=[pl.BlockSpec((tm, w), row)] * 9,
        out_shape=outs,
        scratch_shapes=[pltpu.VMEM((V7X_SUBLANES, D), F32)],
        compiler_params=_cparams(("arbitrary", "arbitrary")),
        name="in_proj",
    )(x2, pos, attn_g, w_in, w1, a1, g1, w2, a2, g2, mu_wag, qg, kg, invf)


def _attn_kernel(lq1_ref, lk1_ref, lq2_ref, lk2_ref, ogt_ref, q_ref, k_ref, v_ref, o_ref,
                 vt_ref, s_ref, acc_ref):
    nh, nq, hw, tq = vt_ref.shape
    tk = tq
    heads = range(nh)
    lam = (jnp.exp(jnp.sum(lq1_ref[...] * lk1_ref[...], axis=-1, keepdims=True))
           - jnp.exp(jnp.sum(lq2_ref[...] * lk2_ref[...], axis=-1, keepdims=True)) + LAM_INIT)

    for h in heads:
        for c in range(nq):
            vt_ref[h, c] = (v_ref[c * tk:(c + 1) * tk, h * hw:(h + 1) * hw]
                            .astype(F32).T.astype(vt_ref.dtype))

    lane = lax.broadcasted_iota(jnp.int32, (tq, hw), 1)

    def stacked_queries(t):
        out = []
        for h in heads:
            q = q_ref[t * tq:(t + 1) * tq, h * hw:(h + 1) * hw]
            zero = jnp.zeros_like(q)
            out.append(jnp.concatenate([jnp.where(lane < DIFF_HEAD_DIM, q, zero),
                                        jnp.where(lane >= DIFF_HEAD_DIM, q, zero)], axis=0))
        return out

    def scores(j, qs):
        start = pl.multiple_of(j * tk, tk)
        return [lax.dot_general(k_ref[pl.ds(start, tk), h * hw:(h + 1) * hw], qs[h],
                                (((1,), (1,)), ((), ())), preferred_element_type=F32) for h in heads]

    def colmax(m, st):
        return tuple(jnp.maximum(mh, jnp.max(s, axis=0, keepdims=True)) for mh, s in zip(m, st))

    krow = lax.broadcasted_iota(jnp.int32, (tk, 2 * tq), 0)
    qcol = lax.broadcasted_iota(jnp.int32, (tk, 2 * tq), 1)
    causal = krow <= jnp.where(qcol >= tq, qcol - tq, qcol)
    ninf = tuple(jnp.full((1, 2 * tq), -jnp.inf, F32) for _ in heads)
    zeros = tuple(jnp.zeros((1, 2 * tq), F32) for _ in heads)

    def diagonal(t, qs, m):
        st = [jnp.where(causal, s, -jnp.inf) for s in scores(t, qs)]
        for h in heads:
            s_ref[t % 2, h, t] = st[h]
        return colmax(m, st)

    m_cur = diagonal(0, stacked_queries(0), ninf)
    for t in range(nq):
        slot, nslot = t % 2, (t + 1) % 2
        has_next = t + 1 < nq
        qs_next = stacked_queries(t + 1) if has_next else None
        acc_ref[...] = jnp.zeros_like(acc_ref)

        def body(j, carry, slot=slot, nslot=nslot, has_next=has_next, qs_next=qs_next, m_cur=m_cur):
            l, m_next = carry
            pt = [jnp.exp2(s_ref[slot, h, j] - m_cur[h]) for h in heads]
            pv = [jnp.dot(vt_ref[h, j], pt[h].astype(BF16), preferred_element_type=F32) for h in heads]
            for h in heads:
                acc_ref[h] += pv[h]
            l = tuple(lh + jnp.sum(p, axis=0, keepdims=True) for lh, p in zip(l, pt))
            if has_next:
                st = scores(j, qs_next)
                for h in heads:
                    s_ref[nslot, h, j] = st[h]
                m_next = colmax(m_next, st)
            return l, m_next

        lsum, m_next = lax.fori_loop(0, t + 1, body, (zeros, ninf), unroll=ATTN_UNROLL)
        if has_next:
            m_next = diagonal(t + 1, qs_next, m_next)

        for h in heads:
            acc = acc_ref[h]
            l = lsum[h]
            ot = acc[:, :tq] / l[:, :tq] - lam * (acc[:, tq:] / l[:, tq:])
            ot = ot * lax.rsqrt(jnp.mean(ot * ot, axis=0, keepdims=True) + SUBLN_EPS) * ogt_ref[...]
            o_ref[t * tq:(t + 1) * tq, h * hw:(h + 1) * hw] = (ot * (1.0 - LAM_INIT)).T.astype(o_ref.dtype)
        m_cur = m_next


def _attention(q, k, v, lq1, lk1, lq2, lk2, og, B, S):
    T = q.shape[0]
    tq = min(ATTN_ROWS, S)
    nq = S // tq
    hw = 2 * DIFF_HEAD_DIM
    nh = ATTN_HEADS_PER_STEP
    gw = nh * hw
    small = lambda b, h: (0, 0)
    seq = pl.BlockSpec((S, gw), lambda b, h: (b, h))
    return pl.pallas_call(
        _attn_kernel,
        grid=(B, DIFF_HEADS // nh),
        in_specs=[pl.BlockSpec(lq1.shape, small), pl.BlockSpec(lk1.shape, small),
                  pl.BlockSpec(lq2.shape, small), pl.BlockSpec(lk2.shape, small),
                  pl.BlockSpec(og.shape, small), seq, seq, seq],
        out_specs=seq,
        out_shape=jax.ShapeDtypeStruct((T, DIFF_WIDTH), BF16),
        scratch_shapes=[pltpu.VMEM((nh, nq, hw, tq), BF16), pltpu.VMEM((2, nh, nq, tq, 2 * tq), F32),
                        pltpu.VMEM((nh, hw, 2 * tq), F32)],
        compiler_params=_cparams(("arbitrary", "arbitrary")),
        name="diff_attn",
    )(lq1, lk1, lq2, lk2, og, q, k, v)


def _rwkv_kernel(zr_ref, zk_ref, zv_ref, wp_ref, ap_ref, gg_ref, mu_ref, w0_ref, a0_ref, kk_ref,
                 ka_ref, rk_ref, gnw_ref, gnb_ref, o_ref, state_ref, carry_ref):
    R = zr_ref.shape[0]
    C = min(RWKV_CHUNK, R)
    nch = R // C
    N = RWKV_HEAD
    W = RWKV_WIDTH
    HP = 2 * N
    n_pairs = W // HP

    @pl.when(pl.program_id(1) == 0)
    def _():
        state_ref[...] = jnp.zeros_like(state_ref)
        carry_ref[...] = jnp.zeros_like(carry_ref)

    zr, zk, zv = zr_ref[...], zk_ref[...], zv_ref[...]
    last = V7X_SUBLANES - 1
    r = zr + (_shift_rows(zr, carry_ref[last:last + 1, 0:W]) - zr) * mu_ref[0:1, :]
    k = zk + (_shift_rows(zk, carry_ref[last:last + 1, W:2 * W]) - zk) * mu_ref[1:2, :]
    v = zv + (_shift_rows(zv, carry_ref[last:last + 1, 2 * W:3 * W]) - zv) * mu_ref[2:3, :]
    carry_ref[:, 0:W] = zr[R - V7X_SUBLANES:R, :]
    carry_ref[:, W:2 * W] = zk[R - V7X_SUBLANES:R, :]
    carry_ref[:, 2 * W:3 * W] = zv[R - V7X_SUBLANES:R, :]

    lw = -math.exp(-0.5) * jax.nn.sigmoid(w0_ref[...] + wp_ref[...])
    a = jax.nn.sigmoid(a0_ref[...] + ap_ref[...])
    kk = k * kk_ref[...]
    kk = kk * lax.rsqrt(jnp.maximum(_sumsq_64(kk), 1e-24))
    k2 = k * (1.0 + (a - 1.0) * ka_ref[...])
    a_s = -kk
    b_s = kk * a

    rr = lax.broadcasted_iota(jnp.int32, (R, R), 0)
    cc = lax.broadcasted_iota(jnp.int32, (R, R), 1)
    same_chunk = (rr // C) == (cc // C)
    L = _dot_exact_lhs((same_chunk & (cc <= rr)).astype(BF16), lw)
    bonus_w = r * k2 * rk_ref[...]

    P2 = 2 * C
    sr = lax.broadcasted_iota(jnp.int32, (P2, HP), 0)
    sc = lax.broadcasted_iota(jnp.int32, (P2, HP), 1)
    stack_mask = (sr < C) == (sc < N)
    br = lax.broadcasted_iota(jnp.int32, (P2, P2), 0)
    bc = lax.broadcasted_iota(jnp.int32, (P2, P2), 1)
    same_head = (br < C) == (bc < C)
    tr = jnp.where(br >= C, br - C, br)
    tc = jnp.where(bc >= C, bc - C, bc)
    strict = same_head & (tc < tr)
    lower = same_head & (tc <= tr)
    eye_p = (br == bc).astype(F32)
    kr = lax.broadcasted_iota(jnp.int32, (HP, HP), 0)
    kc = lax.broadcasted_iota(jnp.int32, (HP, HP), 1)
    eye_k = kr == kc

    def dup(x):
        return jnp.concatenate([x, x], axis=0)

    def stack(x):
        return jnp.where(stack_mask, dup(x), 0.0)

    n_sq = int(math.log2(C)) - 1
    items = [(c, p) for c in range(nch) for p in range(n_pairs)]

    def prep(c, p):
        rows = slice(c * C, (c + 1) * C)
        lanes = slice(p * HP, (p + 1) * HP)
        Lc = L[rows, lanes]
        lwc = lw[rows, lanes]
        LC = Lc[C - 1:C, :]
        enL = jnp.exp(-Lc)
        eCL = jnp.exp(LC - Lc)
        b_c, k_c = b_s[rows, lanes], k2[rows, lanes]
        return dict(
            xa=stack(a_s[rows, lanes] * jnp.exp(Lc - lwc)), xr=stack(r[rows, lanes] * jnp.exp(Lc)),
            bt=dup(b_c * enL), kt=dup(k_c * enL), bh=stack(b_c * eCL), kh=stack(k_c * eCL),
            vs=stack(v[rows, lanes]), pc=jnp.exp(LC))

    d = [prep(c, p) for c, p in items]
    aa = [_dot_nt(jnp.concatenate([e["xa"], e["xr"]], axis=0),
                  jnp.concatenate([e["bt"], e["kt"]], axis=0)) for e in d]
    a_ab = [jnp.where(strict, m[:P2, :P2], 0.0) for m in aa]
    a_ak = [jnp.where(strict, m[:P2, P2:], 0.0) for m in aa]
    a_rb = [jnp.where(lower, m[P2:, :P2], 0.0) for m in aa]
    a_rk = [jnp.where(lower, m[P2:, P2:], 0.0) for m in aa]
    tm = [eye_p + m for m in a_ab]
    npw = a_ab
    for _ in range(n_sq):
        npw = [_dot(m, m) for m in npw]
        tm = [t + _dot(t, m) for t, m in zip(tm, npw)]
    av = [_dot(m, e["vs"]) for m, e in zip(a_ak, d)]
    z = [_dot(t, jnp.concatenate([e["xa"], x], axis=1)) for t, e, x in zip(tm, d, av)]
    zero_v = jnp.zeros((P2, HP), F32)
    rhs = [jnp.concatenate([x, jnp.concatenate([zero_v, e["vs"]], axis=1)], axis=0) for e, x in zip(d, z)]
    lhs = [jnp.concatenate([jnp.concatenate([m1, m2], axis=1),
                            jnp.concatenate([e["bh"], e["kh"]], axis=0).T], axis=0)
           for m1, m2, e in zip(a_rb, a_rk, d)]
    wg = [_dot(a, b) for a, b in zip(lhs, rhs)]

    for idx, (c, p) in enumerate(items):
        e = d[idx]
        rows = slice(c * C, (c + 1) * C)
        lanes = slice(p * HP, (p + 1) * HP)
        rp = e["xr"] + wg[idx][:P2, :HP]
        y0 = wg[idx][:P2, HP:]
        gm = wg[idx][P2:, :HP]
        hm = wg[idx][P2:, HP:]
        st = state_ref[p]
        yg = _dot(jnp.concatenate([rp, gm], axis=0), st)
        pc_col = jnp.sum(jnp.where(eye_k, e["pc"], 0.0), axis=1, keepdims=True)
        state_ref[p] = pc_col * st + yg[P2:] + hm
        ys = yg[:P2] + y0
        mu = jnp.sum(ys, axis=-1, keepdims=True) * (1.0 / N)
        yc = jnp.where(stack_mask, ys - mu, 0.0)
        var = jnp.sum(yc * yc, axis=-1, keepdims=True) * (1.0 / N)
        yn = yc * lax.rsqrt(var + GN_EPS)
        bonus = jnp.sum(stack(bonus_w[rows, lanes]), axis=-1, keepdims=True) * e["vs"]
        yn = yn[:C] + yn[C:]
        bonus = bonus[:C] + bonus[C:]
        o_ref[rows, lanes] = ((yn * gnw_ref[:, lanes] + gnb_ref[:, lanes] + bonus)
                              * gg_ref[rows, lanes]).astype(o_ref.dtype)


def _rwkv(zr, zk, zv, wp, ap, gg, mu_rkv, w0, a0, k_k, k_a, r_k, gn_w, gn_b, B, S):
    T = zr.shape[0]
    C = min(RWKV_ROWS, S)
    nc = S // C
    W = RWKV_WIDTH
    HP = 2 * RWKV_HEAD
    row = lambda b, i: (b * nc + i, 0)
    full = lambda b, i: (0, 0)

    def fs(a):
        return pl.BlockSpec(a.shape, full)

    return pl.pallas_call(
        _rwkv_kernel,
        grid=(B, nc),
        in_specs=[pl.BlockSpec((C, W), row)] * 6 + [fs(mu_rkv), fs(w0), fs(a0), fs(k_k), fs(k_a),
                                                    fs(r_k), fs(gn_w), fs(gn_b)],
        out_specs=pl.BlockSpec((C, W), row),
        out_shape=jax.ShapeDtypeStruct((T, W), BF16),
        scratch_shapes=[pltpu.VMEM((W // HP, HP, HP), F32),
                        pltpu.VMEM((V7X_SUBLANES, 3 * W), F32)],
        compiler_params=_cparams(("arbitrary", "arbitrary")),
        name="rwkv7",
    )(zr, zk, zv, wp, ap, gg, mu_rkv, w0, a0, k_k, k_a, r_k, gn_w, gn_b)


ROUTE_COLS = 8
ROUTER_GROUP_LANE = 0
ROUTER_EXPERT_LANE = N_GROUPS


def _out_router_kernel(x_ref, od_ref, orw_ref, wo_ref, mg_ref, wrh_ref, wrl_ref,
                       x1_out, hm_out, route_out, cnt_out, sl_out):
    tm = PROJ_ROWS if x_ref.shape[0] >= PROJ_ROWS else x_ref.shape[0]
    tiles = [slice(t * tm, (t + 1) * tm) for t in range(x_ref.shape[0] // tm)]
    x1 = [x_ref[r, :] + jnp.dot(od_ref[r, :], wo_ref[0:DIFF_WIDTH, :], preferred_element_type=F32)
          + jnp.dot(orw_ref[r, :], wo_ref[DIFF_WIDTH:, :], preferred_element_type=F32) for r in tiles]
    hm = [_rms(v, mg_ref[...], NORM_EPS) for v in x1]
    for r, v, h in zip(tiles, x1, hm):
        x1_out[r, :] = v
        hm_out[r, :] = h.astype(hm_out.dtype)

    def logits(h):
        hi = h.astype(BF16)
        lo = (h - hi.astype(F32)).astype(BF16)
        return (jnp.dot(hi, wrh_ref[...], preferred_element_type=F32)
                + jnp.dot(hi, wrl_ref[...], preferred_element_type=F32)
                + jnp.dot(lo, wrh_ref[...], preferred_element_type=F32))

    lgs = [logits(h) for h in hm]
    lane = lax.broadcasted_iota(jnp.int32, (tm, V7X_LANES), 1).astype(F32)
    big = float(V7X_LANES)
    ninf = -jnp.inf
    gmask = lane < N_GROUPS

    def choose(lg):
        gmax = jnp.max(jnp.where(gmask, lg, ninf), axis=-1, keepdims=True)
        g_sel = jnp.min(jnp.where(gmask & (lg == gmax), lane, big), axis=-1, keepdims=True)
        g_w = 1.0 / jnp.sum(jnp.where(gmask, jnp.exp(lg - gmax), 0.0), axis=-1, keepdims=True)
        lo_lane = ROUTER_EXPERT_LANE + g_sel * EXPERTS_PER_GROUP
        emask = (lane >= lo_lane) & (lane < lo_lane + EXPERTS_PER_GROUP)
        v1 = jnp.max(jnp.where(emask, lg, ninf), axis=-1, keepdims=True)
        i1 = jnp.min(jnp.where(emask & (lg == v1), lane, big), axis=-1, keepdims=True)
        emask2 = emask & (lane != i1)
        v2 = jnp.max(jnp.where(emask2, lg, ninf), axis=-1, keepdims=True)
        i2 = jnp.min(jnp.where(emask2 & (lg == v2), lane, big), axis=-1, keepdims=True)
        e2x = jnp.exp(v2 - v1)
        den = 1.0 + e2x
        return i1, i2, (1.0 / den) * g_w, (e2x / den) * g_w

    chosen = [choose(lg) for lg in lgs]

    ri = lax.broadcasted_iota(jnp.int32, (tm, tm), 0)
    ci = lax.broadcasted_iota(jnp.int32, (tm, tm), 1)
    before = (ci < ri).astype(BF16)
    li = lax.broadcasted_iota(jnp.int32, (V7X_LANES, V7X_LANES), 0)
    lj = lax.broadcasted_iota(jnp.int32, (V7X_LANES, V7X_LANES), 1)
    lanes_before = (li < lj).astype(BF16)
    ohs = [((lane == i1) | (lane == i2)).astype(F32) for i1, i2, _, _ in chosen]
    prefixes = [jnp.dot(before, oh.astype(BF16), preferred_element_type=F32) for oh in ohs]
    counts = [jnp.broadcast_to(jnp.sum(oh, axis=0, keepdims=True), (V7X_SUBLANES, V7X_LANES)) for oh in ohs]
    padded = [jnp.floor((c + (RUN_ALIGN - 1)) * (1.0 / RUN_ALIGN)) * RUN_ALIGN for c in counts]
    starts = [jnp.dot(pc.astype(BF16), lanes_before, preferred_element_type=F32) for pc in padded]
    col = lax.broadcasted_iota(jnp.int32, (tm, V7X_LANES), 1)
    for t, (r, (i1, i2, wt1, wt2), cnt, start, prefix) in enumerate(zip(tiles, chosen, counts, starts, prefixes)):
        row = prefix + start[0:1, :]
        sl1 = jnp.sum(jnp.where(lane == i1, row, 0.0), axis=-1, keepdims=True)
        sl2 = jnp.sum(jnp.where(lane == i2, row, 0.0), axis=-1, keepdims=True)
        cnt_out[t * V7X_SUBLANES:(t + 1) * V7X_SUBLANES, :] = cnt
        wide = jnp.where(col == 0, sl1, jnp.where(col == 1, sl2, jnp.where(col == 2, wt1, jnp.where(
            col == 3, wt2, 0.0))))
        route_out[r, :] = wide[:, 0:ROUTE_COLS]
        sl_out[t] = wide.T[0:V7X_SUBLANES, :].astype(sl_out.dtype)


def _out_router(x2, od, orw, w_out, moe_g, wr_hi, wr_lo):
    T, D = x2.shape
    tm = min(PROJ_ROWS, T)
    nt = T // tm
    tps = TILES_PER_STEP if nt % TILES_PER_STEP == 0 else 1
    tr = tm * tps
    row = lambda i: (i, 0)
    full = lambda i: (0, 0)

    def fs(a):
        return pl.BlockSpec(a.shape, full)

    return pl.pallas_call(
        _out_router_kernel,
        grid=(nt // tps,),
        in_specs=[pl.BlockSpec((tr, D), row), pl.BlockSpec((tr, DIFF_WIDTH), row),
                  pl.BlockSpec((tr, RWKV_WIDTH), row), fs(w_out), fs(moe_g), fs(wr_hi), fs(wr_lo)],
        out_specs=[pl.BlockSpec((tr, D), row), pl.BlockSpec((tr, D), row),
                   pl.BlockSpec((tr, ROUTE_COLS), row), pl.BlockSpec((tps * V7X_SUBLANES, V7X_LANES), row),
                   pl.BlockSpec((tps, V7X_SUBLANES, tm), lambda i: (i, 0, 0))],
        out_shape=[jax.ShapeDtypeStruct((T, D), F32), jax.ShapeDtypeStruct((T, D), BF16),
                   jax.ShapeDtypeStruct((T, ROUTE_COLS), F32),
                   jax.ShapeDtypeStruct((nt * V7X_SUBLANES, V7X_LANES), F32),
                   jax.ShapeDtypeStruct((nt, V7X_SUBLANES, tm), jnp.int32)],
        compiler_params=_cparams(("arbitrary",)),
        name="out_router",
    )(x2, od, orw, w_out, moe_g, wr_hi, wr_lo)


RUN_ALIGN = V7X_SUBLANES
RUN_CLASSES = 6
TAB_COUNT, TAB_NUSED = 0, RUN_CLASSES
TAB_SRC = V7X_SUBLANES
TAB_DST = TAB_SRC + RUN_CLASSES * N_EXPERTS
TAB_WORDS = 4 * V7X_LANES


def _stage_rows(tm):
    need = 2 * tm + N_EXPERTS * (RUN_ALIGN - 1)
    return -(-need // V7X_LANES) * V7X_LANES


def _run_table(src, dst, length, spare):
    i32 = jnp.int32
    n = src.shape[0]
    cls = jnp.arange(RUN_CLASSES, dtype=i32)[None, :, None]
    ln = length[:, None, :]
    has = ((ln >> 3) >> cls) & 1
    off = ln & ~((2 * RUN_ALIGN << cls) - 1)
    pos = jnp.cumsum(has, axis=-1) - 1
    k = jnp.arange(N_EXPERTS, dtype=i32)
    pick = (has[..., None] == 1) & (pos[..., None] == k)

    def compact(v):
        return jnp.sum(jnp.where(pick, (v[:, None, :] + off)[..., None], 0), axis=2).reshape(n, -1)

    head = jnp.concatenate([jnp.sum(has, axis=-1), jnp.broadcast_to(spare, (n, 1)).astype(i32),
                            jnp.zeros((n, TAB_SRC - RUN_CLASSES - 1), i32)], axis=1)
    tab = jnp.concatenate([head, compact(src), compact(dst)], axis=1)
    return jnp.pad(tab, ((0, 0), (0, TAB_WORDS - tab.shape[1]))).reshape(n, 1, TAB_WORDS)


def _run_copies(tab_ref, tile, make_copy, op):
    for c in range(RUN_CLASSES):
        size = RUN_ALIGN << c

        def one(k, carry, c=c, size=size):
            src = tab_ref[tile, 0, TAB_SRC + c * N_EXPERTS + k]
            dst = tab_ref[tile, 0, TAB_DST + c * N_EXPERTS + k]
            getattr(make_copy(pl.multiple_of(src, RUN_ALIGN), pl.multiple_of(dst, RUN_ALIGN), size), op)()
            return carry

        lax.fori_loop(0, tab_ref[tile, 0, TAB_COUNT + c], one, 0)


def _dispatch_kernel(tab_ref, tail_ref, sl_ref, hm_ref, xb_ref, stage_ref, zero_ref, sem, zsem):
    tps, ns = stage_ref.shape[0], stage_ref.shape[1]
    tm = hm_ref.shape[0] // tps
    bm = zero_ref.shape[0]
    nb = xb_ref.shape[0] // bm
    n_used = tail_ref[0, 0, TAB_NUSED]

    def tile_copy(t):
        def make_copy(src, dst, size):
            return pltpu.make_async_copy(stage_ref.at[t, pl.ds(src, size)], xb_ref.at[pl.ds(dst, size)], sem)
        return make_copy

    def zero_copy(src, dst, size):
        return pltpu.make_async_copy(zero_ref.at[pl.ds(src, size)], xb_ref.at[pl.ds(dst, size)], zsem)

    def zero_blocks(op):
        def one(b, c):
            getattr(zero_copy(0, pl.multiple_of(b * bm, bm), bm), op)()
            return c
        lax.fori_loop(n_used, nb, one, 0)

    @pl.when(pl.program_id(0) == 0)
    def _():
        zero_ref[...] = jnp.zeros_like(zero_ref)
        _run_copies(tail_ref, 0, zero_copy,"start")
        zero_blocks("start")

    srow = lax.broadcasted_iota(jnp.int32, (ns, tm), 0)
    for t in range(tps):
        sel = (srow == sl_ref[t, 0:1, :]) | (srow == sl_ref[t, 1:2, :])
        stage_ref[t] = _pack16(jnp.dot(sel.astype(BF16), hm_ref[t * tm:(t + 1) * tm, :],
                                       preferred_element_type=F32))
        _run_copies(tab_ref, t, tile_copy(t),"start")
    for t in range(tps):
        _run_copies(tab_ref, t, tile_copy(t),"wait")

    @pl.when(pl.program_id(0) == pl.num_programs(0) - 1)
    def _():
        _run_copies(tail_ref, 0, zero_copy,"wait")
        zero_blocks("wait")


def _dispatch(tab, tail, sl_rows, hm, P):
    T, D = hm.shape
    tm = min(PROJ_ROWS, T)
    nt = T // tm
    tps = TILES_PER_STEP if nt % TILES_PER_STEP == 0 else 1
    return pl.pallas_call(
        _dispatch_kernel,
        grid=(nt // tps,),
        in_specs=[pl.BlockSpec((tps, 1, tab.shape[-1]), lambda i: (i, 0, 0), memory_space=pltpu.SMEM),
                  pl.BlockSpec((1, 1, tail.shape[-1]), lambda i: (0, 0, 0), memory_space=pltpu.SMEM),
                  pl.BlockSpec((tps, V7X_SUBLANES, tm), lambda i: (i, 0, 0)),
                  pl.BlockSpec((tps * tm, D), lambda i: (i, 0))],
        out_specs=pl.BlockSpec(memory_space=pl.ANY),
        out_shape=jax.ShapeDtypeStruct((P, D // 2), PACKED),
        scratch_shapes=[pltpu.VMEM((tps, _stage_rows(tm), D // 2), PACKED),
                        pltpu.VMEM((MOE_ROWS, D // 2), PACKED),
                        pltpu.SemaphoreType.DMA(()), pltpu.SemaphoreType.DMA(())],
        compiler_params=_cparams(("arbitrary",)),
        name="moe_dispatch",
    )(tab, tail, sl_rows, hm)


def _expert_kernel(be_ref, nu_ref, xb_ref, wg_ref, wu_ref, wd_ref, yb_ref, wgb, wub, wdb):
    i = pl.program_id(0)
    prev = be_ref[jnp.maximum(i - 1, 0)]
    changed = (i == 0) | (be_ref[i] != prev)

    @pl.when(changed)
    def _():
        wgb[...] = wg_ref[0].astype(BF16)
        wub[...] = wu_ref[0].astype(BF16)
        wdb[...] = wd_ref[0].astype(BF16)

    @pl.when(i < nu_ref[0])
    def _():
        xb = _unpack16(xb_ref[...]).astype(BF16)
        gate = jnp.dot(xb, wgb[...], preferred_element_type=F32)
        up = jnp.dot(xb, wub[...], preferred_element_type=F32)
        hdn = (gate * jax.nn.sigmoid(gate)) * up
        y = jnp.dot(hdn.astype(BF16), wdb[...], preferred_element_type=F32)
        yb_ref[...] = _pack16(y.astype(BF16).astype(F32))

    @pl.when(i >= nu_ref[0])
    def _():
        yb_ref[...] = jnp.zeros_like(yb_ref)


def _experts(block_e, n_used, xb, w_gate, w_up, w_down):
    P, DP = xb.shape
    D = 2 * DP
    bm = MOE_ROWS
    nb = P // bm
    E = D_EXPERT
    grid_spec = pltpu.PrefetchScalarGridSpec(
        num_scalar_prefetch=2,
        grid=(nb,),
        in_specs=[pl.BlockSpec((bm, DP), lambda i, be, nu: (jnp.minimum(i, nu[0] - 1), 0)),
                  pl.BlockSpec((1, D, E), lambda i, be, nu: (be[i], 0, 0)),
                  pl.BlockSpec((1, D, E), lambda i, be, nu: (be[i], 0, 0)),
                  pl.BlockSpec((1, E, D), lambda i, be, nu: (be[i], 0, 0))],
        out_specs=pl.BlockSpec((bm, DP), lambda i, be, nu: (jnp.where(i < nu[0], i, nb - 1), 0)),
        scratch_shapes=[pltpu.VMEM((D, E), BF16), pltpu.VMEM((D, E), BF16), pltpu.VMEM((E, D), BF16)],
    )
    return pl.pallas_call(
        _expert_kernel,
        grid_spec=grid_spec,
        out_shape=jax.ShapeDtypeStruct((P, DP), PACKED),
        input_output_aliases={2: 0},
        compiler_params=_cparams(("arbitrary",)),
        name="moe_experts",
    )(block_e, n_used, xb, w_gate, w_up, w_down)


def _final_kernel(tab_ref, x1_ref, route_ref, p_ref, wpl_ref, plg_ref, pgg_ref, wgate_ref, yb_ref,
                  o_ref, stage_ref, sem):
    tps, ns = stage_ref.shape[0], stage_ref.shape[1]
    tm = x1_ref.shape[0] // tps
    tiles = [slice(t * tm, (t + 1) * tm) for t in range(tps)]
    stage_ref[...] = jnp.zeros_like(stage_ref)

    def tile_copy(t):
        def make_copy(src, dst, size):
            return pltpu.make_async_copy(yb_ref.at[pl.ds(dst, size)], stage_ref.at[t, pl.ds(src, size)], sem)
        return make_copy

    for t in range(tps):
        _run_copies(tab_ref, t, tile_copy(t),"start")
    pe = [_rms(jnp.dot(p_ref[r, :].astype(BF16), wpl_ref[...], preferred_element_type=F32),
               plg_ref[...], NORM_EPS) for r in tiles]
    for t in range(tps):
        _run_copies(tab_ref, t, tile_copy(t),"wait")

    scol = lax.broadcasted_iota(jnp.int32, (tm, ns), 1).astype(F32)

    def selection(r):
        route = route_ref[r, :]
        wsel = (jnp.where(scol == route[:, 0:1], route[:, 2:3], 0.0)
                + jnp.where(scol == route[:, 1:2], route[:, 3:4], 0.0))
        w_hi = wsel.astype(BF16)
        return w_hi, (wsel - w_hi.astype(F32)).astype(BF16)

    wsel = [selection(r) for r in tiles]
    yb16 = [_unpack16(stage_ref[t]).astype(BF16) for t in range(tps)]
    moe = [jnp.dot(w_hi, y, preferred_element_type=F32) + jnp.dot(w_lo, y, preferred_element_type=F32)
           for (w_hi, w_lo), y in zip(wsel, yb16)]
    x2 = [x1_ref[r, :] + m for r, m in zip(tiles, moe)]
    gate = [jax.nn.sigmoid(_dot(_rms(v, pgg_ref[...], NORM_EPS), wgate_ref[...])) for v in x2]
    for r, v, e, g in zip(tiles, x2, pe, gate):
        o_ref[r, :] = v + e * g


def _final(tab, x1, route, p2, w_pl, pl_g, pl_gate_g, w_gate, yb):
    T, D = x1.shape
    tm = min(PROJ_ROWS, T)
    nt = T // tm
    tps = TILES_PER_STEP if nt % TILES_PER_STEP == 0 else 1
    tr = tps * tm
    row = lambda i: (i, 0)
    full = lambda i: (0, 0)

    def fs(a):
        return pl.BlockSpec(a.shape, full)

    return pl.pallas_call(
        _final_kernel,
        grid=(nt // tps,),
        in_specs=[pl.BlockSpec((tps, 1, tab.shape[-1]), lambda i: (i, 0, 0), memory_space=pltpu.SMEM),
                  pl.BlockSpec((tr, D), row), pl.BlockSpec((tr, ROUTE_COLS), row),
                  pl.BlockSpec((tr, PL_DIM), row), fs(w_pl), fs(pl_g), fs(pl_gate_g), fs(w_gate),
                  pl.BlockSpec(memory_space=pl.ANY)],
        out_specs=pl.BlockSpec((tr, D), row),
        out_shape=jax.ShapeDtypeStruct((T, D), F32),
        scratch_shapes=[pltpu.VMEM((tps, _stage_rows(tm), D // 2), PACKED), pltpu.SemaphoreType.DMA(())],
        compiler_params=_cparams(("arbitrary",)),
        name="combine_final",
    )(tab, x1, route, p2, w_pl, pl_g, pl_gate_g, w_gate, yb)


def kernel(x, p, positions, attn_norm_g, w_in, q_norm_g, k_norm_g, lambda_q1, lambda_k1, lambda_q2, lambda_k2, diff_out_g, mu_rkv, mu_wag, w0, w_lora_a, w_lora_b, a0, a_lora_a, a_lora_b, g_lora_a, g_lora_b, k_k, k_a, r_k, gn_w, gn_b, w_out, moe_norm_g, w_group, w_expert_router, w_gate, w_up, w_down, w_pl, pl_norm_g, pl_gate_norm_g, w_pl_gate):
    B, S, D = x.shape
    T = B * S
    assert p.shape[0] == 1, "one layer"
    x2 = x.reshape(T, D)
    p2 = p[0].reshape(T, PL_DIM)
    pos = positions.astype(F32).reshape(T, 1)

    def row(a):
        return a.reshape(1, -1).astype(F32)

    half = DIFF_HEAD_DIM // 2
    inv_freq = ROPE_THETA ** (-jnp.arange(half, dtype=F32) / half)
    invf = jnp.tile(inv_freq, V7X_LANES // half).reshape(1, V7X_LANES)
    reps = DIFF_WIDTH // DIFF_HEAD_DIM

    q, k, v, zr, zk, zv, wp, ap, gg = _in_proj(
        x2, pos, row(attn_norm_g[0]), w_in[0].astype(BF16),
        w_lora_a[0].astype(BF16), a_lora_a[0].astype(BF16), g_lora_a[0].astype(BF16),
        w_lora_b[0].astype(BF16), a_lora_b[0].astype(BF16), g_lora_b[0].astype(BF16),
        mu_wag[0].astype(F32), row(jnp.tile(q_norm_g[0], reps)), row(jnp.tile(k_norm_g[0], reps)),
        invf, B, S)

    o_diff = _attention(q, k, v, row(lambda_q1[0]), row(lambda_k1[0]), row(lambda_q2[0]),
                        row(lambda_k2[0]), diff_out_g[0].astype(F32).reshape(-1, 1), B, S)
    o_rwkv = _rwkv(zr, zk, zv, wp, ap, gg, mu_rkv[0].astype(F32), row(w0[0]), row(a0[0]),
                   row(k_k[0]), row(k_a[0]), row(r_k[0]), row(gn_w[0]), row(gn_b[0]), B, S)

    wr = jnp.concatenate([w_group[0], jnp.transpose(w_expert_router[0], (1, 0, 2)).reshape(D, N_EXPERTS)],
                         axis=1).astype(F32)
    wr = jnp.pad(wr, ((0, 0), (0, V7X_LANES - wr.shape[1])))
    wr_hi = wr.astype(BF16)
    wr_lo = (wr - wr_hi.astype(F32)).astype(BF16)
    x1, hm, route, cnt, sl_rows = _out_router(x2, o_diff, o_rwkv, w_out[0].astype(BF16), row(moe_norm_g[0]),
                                     wr_hi, wr_lo)

    bm = MOE_ROWS
    tm = min(PROJ_ROWS, T)
    nt = T // tm
    i32 = jnp.int32
    cnt_te = cnt.reshape(nt, V7X_SUBLANES, V7X_LANES)[:, 0, ROUTER_EXPERT_LANE:ROUTER_EXPERT_LANE + N_EXPERTS]
    len_te = (cnt_te.astype(i32) + RUN_ALIGN - 1) // RUN_ALIGN * RUN_ALIGN
    tot_e = jnp.sum(len_te, axis=0)
    pcounts = (tot_e + bm - 1) // bm * bm
    pends = jnp.cumsum(pcounts)
    pstarts = pends - pcounts
    dst_te = pstarts[None, :] + jnp.cumsum(len_te, axis=0) - len_te
    src_te = jnp.cumsum(len_te, axis=1) - len_te
    tab = _run_table(src_te, dst_te, len_te, 0)
    nb = -(-(2 * T + nt * N_EXPERTS * (RUN_ALIGN - 1)) // bm) + N_EXPERTS
    P = nb * bm
    n_used = (pends[-1] // bm).astype(i32).reshape(1)
    block_start = jnp.arange(nb, dtype=i32) * bm
    block_e = jnp.minimum(jnp.sum((pends[None, :] <= block_start[:, None]).astype(i32), axis=1),
                          N_EXPERTS - 1)
    tail = _run_table(jnp.zeros((1, N_EXPERTS), i32), (pstarts + tot_e)[None, :], (pcounts - tot_e)[None, :],
                      n_used)
    xb = _dispatch(tab, tail, sl_rows, hm, P)
    yb = _experts(block_e, n_used, xb, w_gate[0], w_up[0], w_down[0])
    out = _final(tab, x1, route, p2, w_pl[0].astype(BF16), row(pl_norm_g[0]), row(pl_gate_norm_g[0]),
                 w_pl_gate[0].astype(BF16), yb)
    return out.reshape(B, S, D)
```

```python
import functools
import math

import jax
import jax.numpy as jnp
from jax import lax
from jax.experimental import pallas as pl
from jax.experimental.pallas import tpu as pltpu

F32 = jnp.float32
BF16 = jnp.bfloat16

D_MODEL = 1024
PL_DIM = 256
DIFF_WIDTH = 512
RWKV_WIDTH = 512
DIFF_HEAD_DIM = 64
DIFF_HEADS = 4
RWKV_HEAD = 64
RWKV_HEADS = 8
D_DECAY_LORA = 64
D_AAA_LORA = 64
D_GATE_LORA = 160
ROPE_THETA = 10000.0
NORM_EPS = 1e-6
SUBLN_EPS = 1e-5
GN_EPS = 64e-5
N_GROUPS = 4
EXPERTS_PER_GROUP = 8
N_EXPERTS = 32
D_EXPERT = 512
LAM_INIT = 0.8 - 0.6 * math.exp(0.0)

V7X_LANES = 128
V7X_SUBLANES = 8
V7X_VMEM_BYTES = 64 * 1024 * 1024

IN_PROJ_ROWS = 256
IN_PROJ_TILES_PER_STEP = 2
PROJ_ROWS = 256
TILES_PER_STEP = 4
ATTN_ROWS = 256
ATTN_UNROLL = 4
ATTN_HEADS_PER_STEP = 2
RWKV_CHUNK = 64
RWKV_ROWS = 256
MOE_ROWS = 512
VMEM_LIMIT = 56 * 1024 * 1024


def _cparams(sem):
    return pltpu.CompilerParams(dimension_semantics=sem, vmem_limit_bytes=VMEM_LIMIT)


def _dot(a, b):
    return jnp.dot(a.astype(BF16), b.astype(BF16), preferred_element_type=F32)


def _dot_nt(a, b):
    return lax.dot_general(a.astype(BF16), b.astype(BF16), (((1,), (1,)), ((), ())),
                           preferred_element_type=F32)


def _dot_tn(a, b):
    return lax.dot_general(a.astype(BF16), b.astype(BF16), (((0,), (0,)), ((), ())),
                           preferred_element_type=F32)


def _split3(x):
    hi = x.astype(BF16)
    r1 = x - hi.astype(F32)
    mid = r1.astype(BF16)
    lo = (r1 - mid.astype(F32)).astype(BF16)
    return hi, mid, lo


def _dot_exact_lhs(m01, x):
    hi, mid, lo = _split3(x)
    return (jnp.dot(m01, hi, preferred_element_type=F32)
            + jnp.dot(m01, mid, preferred_element_type=F32)
            + jnp.dot(m01, lo, preferred_element_type=F32))


PACKED = jnp.uint32


def _pack16(x):
    h = x.shape[1] // 2
    lo = lax.bitcast_convert_type(x[:, :h], PACKED) >> 16
    hi = lax.bitcast_convert_type(x[:, h:], PACKED) & jnp.uint32(0xFFFF0000)
    return hi | lo


def _unpack16(w):
    lo = lax.bitcast_convert_type(w << 16, F32)
    hi = lax.bitcast_convert_type(w & jnp.uint32(0xFFFF0000), F32)
    return jnp.concatenate([lo, hi], axis=1)


def _sumsq_64(z):
    m, n = z.shape
    group = V7X_LANES // 2
    first = lax.broadcasted_iota(jnp.int32, (m, V7X_LANES), 1) < group
    out = []
    for c in range(0, n, V7X_LANES):
        sq = z[:, c:c + V7X_LANES] * z[:, c:c + V7X_LANES]
        s0 = jnp.sum(jnp.where(first, sq, 0.0), axis=-1, keepdims=True)
        s1 = jnp.sum(jnp.where(first, 0.0, sq), axis=-1, keepdims=True)
        out.append(jnp.where(first, s0, s1))
    return jnp.concatenate(out, axis=1)


def _rms(x, g, eps):
    return x * lax.rsqrt(jnp.mean(x * x, axis=-1, keepdims=True) + eps) * g


def _shift_rows(z, prev_row):
    rolled = pltpu.roll(z, 1, axis=0)
    row = lax.broadcasted_iota(jnp.int32, z.shape, 0)
    return jnp.where(row == 0, prev_row, rolled)


def _in_proj_kernel(x_ref, pos_ref, g_ref, win_ref, w1_ref, a1_ref, g1_ref, w2_ref, a2_ref, g2_ref,
                    mu_ref, qg_ref, kg_ref, invf_ref,
                    q_out, k_out, v_out, zr_out, zk_out, zv_out, wp_out, ap_out, gg_out,
                    carry_ref):
    tm = IN_PROJ_ROWS if x_ref.shape[0] >= IN_PROJ_ROWS else x_ref.shape[0]
    tiles = [slice(t * tm, (t + 1) * tm) for t in range(x_ref.shape[0] // tm)]
    w = DIFF_WIDTH
    half = DIFF_HEAD_DIM // 2

    @pl.when(pl.program_id(1) == 0)
    def _():
        carry_ref[...] = jnp.zeros_like(carry_ref)

    prev = carry_ref[V7X_SUBLANES - 1:V7X_SUBLANES, :]
    hns, dhs = [], []
    for r in tiles:
        hn = _rms(x_ref[r, :], g_ref[...], NORM_EPS)
        dhs.append(_shift_rows(hn, prev) - hn)
        hns.append(hn)
        prev = hn[tm - 1:tm, :]
    carry_ref[...] = hns[-1][tm - V7X_SUBLANES:tm, :]

    zq, zk, low = [], [], []
    for r, hn, dh in zip(tiles, hns, dhs):
        hb = hn.astype(BF16)

        def proj(c, hb=hb):
            return jnp.dot(hb, win_ref[:, c * w:(c + 1) * w], preferred_element_type=F32)

        zq.append(proj(0))
        zk.append(proj(1))
        low.append((_dot(hn + dh * mu_ref[0:1, :], w1_ref[...]), _dot(hn + dh * mu_ref[1:2, :], a1_ref[...]),
                    _dot(hn + dh * mu_ref[2:3, :], g1_ref[...])))
        v_out[r, :] = proj(2).astype(v_out.dtype)
        zr_out[r, :] = proj(3).astype(zr_out.dtype)
        zk_out[r, :] = proj(4).astype(zk_out.dtype)
        zv_out[r, :] = proj(5).astype(zv_out.dtype)

    lane = lax.broadcasted_iota(jnp.int32, (tm, w), 1)
    first_half = (lane % DIFF_HEAD_DIM) < half
    scale = DIFF_HEAD_DIM ** -0.5 * math.log2(math.e)
    for r, q, k, (lw, la, lg) in zip(tiles, zq, zk, low):
        ang = pos_ref[r, :] * invf_ref[...]
        cosf = jnp.concatenate([jnp.cos(ang)] * (w // V7X_LANES), axis=1)
        sinf = jnp.concatenate([jnp.sin(ang)] * (w // V7X_LANES), axis=1)
        sin_signed = jnp.where(first_half, -sinf, sinf)

        def norm_rope(z, gain, cosf=cosf, sin_signed=sin_signed):
            zn = z * lax.rsqrt(_sumsq_64(z) * (1.0 / DIFF_HEAD_DIM) + NORM_EPS) * gain
            partner = jnp.where(first_half, pltpu.roll(zn, w - half, axis=1), pltpu.roll(zn, half, axis=1))
            return zn * cosf + partner * sin_signed

        q_out[r, :] = (norm_rope(q, qg_ref[...]) * scale).astype(q_out.dtype)
        k_out[r, :] = norm_rope(k, kg_ref[...]).astype(k_out.dtype)
        wp_out[r, :] = _dot(jnp.tanh(lw), w2_ref[...])
        ap_out[r, :] = _dot(la, a2_ref[...])
        gg_out[r, :] = _dot(jax.nn.sigmoid(lg), g2_ref[...]).astype(gg_out.dtype)


def _in_proj(x2, pos, attn_g, w_in, w1, a1, g1, w2, a2, g2, mu_wag, qg, kg, invf, B, S):
    T, D = x2.shape
    tile = min(IN_PROJ_ROWS, S)
    tps = IN_PROJ_TILES_PER_STEP if (S // tile) % IN_PROJ_TILES_PER_STEP == 0 else 1
    tm = tile * tps
    ns = S // tm
    w = DIFF_WIDTH
    row = lambda b, i: (b * ns + i, 0)
    full = lambda b, i: (0, 0)

    def fs(a):
        return pl.BlockSpec(a.shape, full)

    outs = [jax.ShapeDtypeStruct((T, w), d) for d in (BF16,) * 6 + (F32, F32, BF16)]
    return pl.pallas_call(
        _in_proj_kernel,
        grid=(B, ns),
        in_specs=[pl.BlockSpec((tm, D), row), pl.BlockSpec((tm, 1), row), fs(attn_g), fs(w_in),
                  fs(w1), fs(a1), fs(g1), fs(w2), fs(a2), fs(g2), fs(mu_wag), fs(qg), fs(kg),
                  fs(invf)],
        out_specs=[pl.BlockSpec((tm, w), row)] * 9,
        out_shape=outs,
        scratch_shapes=[pltpu.VMEM((V7X_SUBLANES, D), F32)],
        compiler_params=_cparams(("arbitrary", "arbitrary")),
        name="in_proj",
    )(x2, pos, attn_g, w_in, w1, a1, g1, w2, a2, g2, mu_wag, qg, kg, invf)


def _attn_kernel(lq1_ref, lk1_ref, lq2_ref, lk2_ref, ogt_ref, q_ref, k_ref, v_ref, o_ref,
                 vt_ref, s_ref, acc_ref):
    nh, nq, hw, tq = vt_ref.shape
    tk = tq
    heads = range(nh)
    lam = (jnp.exp(jnp.sum(lq1_ref[...] * lk1_ref[...], axis=-1, keepdims=True))
           - jnp.exp(jnp.sum(lq2_ref[...] * lk2_ref[...], axis=-1, keepdims=True)) + LAM_INIT)

    for h in heads:
        for c in range(nq):
            vt_ref[h, c] = (v_ref[c * tk:(c + 1) * tk, h * hw:(h + 1) * hw]
                            .astype(F32).T.astype(vt_ref.dtype))

    lane = lax.broadcasted_iota(jnp.int32, (tq, hw), 1)

    def stacked_queries(t):
        out = []
        for h in heads:
            q = q_ref[t * tq:(t + 1) * tq, h * hw:(h + 1) * hw]
            zero = jnp.zeros_like(q)
            out.append(jnp.concatenate([jnp.where(lane < DIFF_HEAD_DIM, q, zero),
                                        jnp.where(lane >= DIFF_HEAD_DIM, q, zero)], axis=0))
        return out

    def scores(j, qs):
        start = pl.multiple_of(j * tk, tk)
        return [lax.dot_general(k_ref[pl.ds(start, tk), h * hw:(h + 1) * hw], qs[h],
                                (((1,), (1,)), ((), ())), preferred_element_type=F32) for h in heads]

    def colmax(m, st):
        return tuple(jnp.maximum(mh, jnp.max(s, axis=0, keepdims=True)) for mh, s in zip(m, st))

    krow = lax.broadcasted_iota(jnp.int32, (tk, 2 * tq), 0)
    qcol = lax.broadcasted_iota(jnp.int32, (tk, 2 * tq), 1)
    causal = krow <= jnp.where(qcol >= tq, qcol - tq, qcol)
    ninf = tuple(jnp.full((1, 2 * tq), -jnp.inf, F32) for _ in heads)
    zeros = tuple(jnp.zeros((1, 2 * tq), F32) for _ in heads)

    def diagonal(t, qs, m):
        st = [jnp.where(causal, s, -jnp.inf) for s in scores(t, qs)]
        for h in heads:
            s_ref[t % 2, h, t] = st[h]
        return colmax(m, st)

    m_cur = diagonal(0, stacked_queries(0), ninf)
    for t in range(nq):
        slot, nslot = t % 2, (t + 1) % 2
        has_next = t + 1 < nq
        qs_next = stacked_queries(t + 1) if has_next else None
        acc_ref[...] = jnp.zeros_like(acc_ref)

        def body(j, carry, slot=slot, nslot=nslot, has_next=has_next, qs_next=qs_next, m_cur=m_cur):
            l, m_next = carry
            pt = [jnp.exp2(s_ref[slot, h, j] - m_cur[h]) for h in heads]
            pv = [jnp.dot(vt_ref[h, j], pt[h].astype(BF16), preferred_element_type=F32) for h in heads]
            for h in heads:
                acc_ref[h] += pv[h]
            l = tuple(lh + jnp.sum(p, axis=0, keepdims=True) for lh, p in zip(l, pt))
            if has_next:
                st = scores(j, qs_next)
                for h in heads:
                    s_ref[nslot, h, j] = st[h]
                m_next = colmax(m_next, st)
            return l, m_next

        lsum, m_next = lax.fori_loop(0, t + 1, body, (zeros, ninf), unroll=ATTN_UNROLL)
        if has_next:
            m_next = diagonal(t + 1, qs_next, m_next)

        for h in heads:
            acc = acc_ref[h]
            l = lsum[h]
            ot = acc[:, :tq] / l[:, :tq] - lam * (acc[:, tq:] / l[:, tq:])
            ot = ot * lax.rsqrt(jnp.mean(ot * ot, axis=0, keepdims=True) + SUBLN_EPS) * ogt_ref[...]
            o_ref[t * tq:(t + 1) * tq, h * hw:(h + 1) * hw] = (ot * (1.0 - LAM_INIT)).T.astype(o_ref.dtype)
        m_cur = m_next


def _attention(q, k, v, lq1, lk1, lq2, lk2, og, B, S):
    T = q.shape[0]
    tq = min(ATTN_ROWS, S)
    nq = S // tq
    hw = 2 * DIFF_HEAD_DIM
    nh = ATTN_HEADS_PER_STEP
    gw = nh * hw
    small = lambda b, h: (0, 0)
    seq = pl.BlockSpec((S, gw), lambda b, h: (b, h))
    return pl.pallas_call(
        _attn_kernel,
        grid=(B, DIFF_HEADS // nh),
        in_specs=[pl.BlockSpec(lq1.shape, small), pl.BlockSpec(lk1.shape, small),
                  pl.BlockSpec(lq2.shape, small), pl.BlockSpec(lk2.shape, small),
                  pl.BlockSpec(og.shape, small), seq, seq, seq],
        out_specs=seq,
        out_shape=jax.ShapeDtypeStruct((T, DIFF_WIDTH), BF16),
        scratch_shapes=[pltpu.VMEM((nh, nq, hw, tq), BF16), pltpu.VMEM((2, nh, nq, tq, 2 * tq), F32),
                        pltpu.VMEM((nh, hw, 2 * tq), F32)],
        compiler_params=_cparams(("arbitrary", "arbitrary")),
        name="diff_attn",
    )(lq1, lk1, lq2, lk2, og, q, k, v)


def _rwkv_kernel(zr_ref, zk_ref, zv_ref, wp_ref, ap_ref, gg_ref, mu_ref, w0_ref, a0_ref, kk_ref,
                 ka_ref, rk_ref, gnw_ref, gnb_ref, o_ref, state_ref, carry_ref):
    R = zr_ref.shape[0]
    C = min(RWKV_CHUNK, R)
    nch = R // C
    N = RWKV_HEAD
    W = RWKV_WIDTH
    HP = 2 * N
    n_pairs = W // HP

    @pl.when(pl.program_id(1) == 0)
    def _():
        state_ref[...] = jnp.zeros_like(state_ref)
        carry_ref[...] = jnp.zeros_like(carry_ref)

    zr, zk, zv = zr_ref[...].astype(F32), zk_ref[...].astype(F32), zv_ref[...].astype(F32)
    last = V7X_SUBLANES - 1
    r = zr + (_shift_rows(zr, carry_ref[last:last + 1, 0:W]) - zr) * mu_ref[0:1, :]
    k = zk + (_shift_rows(zk, carry_ref[last:last + 1, W:2 * W]) - zk) * mu_ref[1:2, :]
    v = zv + (_shift_rows(zv, carry_ref[last:last + 1, 2 * W:3 * W]) - zv) * mu_ref[2:3, :]
    carry_ref[:, 0:W] = zr[R - V7X_SUBLANES:R, :]
    carry_ref[:, W:2 * W] = zk[R - V7X_SUBLANES:R, :]
    carry_ref[:, 2 * W:3 * W] = zv[R - V7X_SUBLANES:R, :]

    lw = -math.exp(-0.5) * jax.nn.sigmoid(w0_ref[...] + wp_ref[...])
    a = jax.nn.sigmoid(a0_ref[...] + ap_ref[...])
    kk = k * kk_ref[...]
    kk = kk * lax.rsqrt(jnp.maximum(_sumsq_64(kk), 1e-24))
    k2 = k * (1.0 + (a - 1.0) * ka_ref[...])
    a_s = -kk
    b_s = kk * a

    rr = lax.broadcasted_iota(jnp.int32, (R, R), 0)
    cc = lax.broadcasted_iota(jnp.int32, (R, R), 1)
    same_chunk = (rr // C) == (cc // C)
    L = _dot_exact_lhs((same_chunk & (cc <= rr)).astype(BF16), lw)
    bonus_w = r * k2 * rk_ref[...]

    P2 = 2 * C
    sr = lax.broadcasted_iota(jnp.int32, (P2, HP), 0)
    sc = lax.broadcasted_iota(jnp.int32, (P2, HP), 1)
    stack_mask = (sr < C) == (sc < N)
    br = lax.broadcasted_iota(jnp.int32, (P2, P2), 0)
    bc = lax.broadcasted_iota(jnp.int32, (P2, P2), 1)
    same_head = (br < C) == (bc < C)
    tr = jnp.where(br >= C, br - C, br)
    tc = jnp.where(bc >= C, bc - C, bc)
    strict = same_head & (tc < tr)
    lower = same_head & (tc <= tr)
    eye_p = (br == bc).astype(F32)
    kr = lax.broadcasted_iota(jnp.int32, (HP, HP), 0)
    kc = lax.broadcasted_iota(jnp.int32, (HP, HP), 1)
    eye_k = kr == kc

    def dup(x):
        return jnp.concatenate([x, x], axis=0)

    def stack(x):
        return jnp.where(stack_mask, dup(x), 0.0)

    n_sq = int(math.log2(C)) - 1
    items = [(c, p) for c in range(nch) for p in range(n_pairs)]

    def prep(c, p):
        rows = slice(c * C, (c + 1) * C)
        lanes = slice(p * HP, (p + 1) * HP)
        Lc = L[rows, lanes]
        lwc = lw[rows, lanes]
        LC = Lc[C - 1:C, :]
        enL = jnp.exp(-Lc)
        eCL = jnp.exp(LC - Lc)
        b_c, k_c = b_s[rows, lanes], k2[rows, lanes]
        return dict(
            xa=stack(a_s[rows, lanes] * jnp.exp(Lc - lwc)), xr=stack(r[rows, lanes] * jnp.exp(Lc)),
            bt=dup(b_c * enL), kt=dup(k_c * enL), bh=stack(b_c * eCL), kh=stack(k_c * eCL),
            vs=stack(v[rows, lanes]), pc=jnp.exp(LC))

    d = [prep(c, p) for c, p in items]
    aa = [_dot_nt(jnp.concatenate([e["xa"], e["xr"]], axis=0),
                  jnp.concatenate([e["bt"], e["kt"]], axis=0)) for e in d]
    a_ab = [jnp.where(strict, m[:P2, :P2], 0.0) for m in aa]
    a_ak = [jnp.where(strict, m[:P2, P2:], 0.0) for m in aa]
    a_rb = [jnp.where(lower, m[P2:, :P2], 0.0) for m in aa]
    a_rk = [jnp.where(lower, m[P2:, P2:], 0.0) for m in aa]
    tm = [eye_p + m for m in a_ab]
    npw = a_ab
    for _ in range(n_sq):
        npw = [_dot(m, m) for m in npw]
        tm = [t + _dot(t, m) for t, m in zip(tm, npw)]
    av = [_dot(m, e["vs"]) for m, e in zip(a_ak, d)]
    z = [_dot(t, jnp.concatenate([e["xa"], x], axis=1)) for t, e, x in zip(tm, d, av)]
    zero_v = jnp.zeros((P2, HP), F32)
    rhs = [jnp.concatenate([x, jnp.concatenate([zero_v, e["vs"]], axis=1)], axis=0) for e, x in zip(d, z)]
    lhs = [jnp.concatenate([jnp.concatenate([m1, m2], axis=1),
                            jnp.concatenate([e["bh"], e["kh"]], axis=0).T], axis=0)
           for m1, m2, e in zip(a_rb, a_rk, d)]
    wg = [_dot(a, b) for a, b in zip(lhs, rhs)]

    for idx, (c, p) in enumerate(items):
        e = d[idx]
        rows = slice(c * C, (c + 1) * C)
        lanes = slice(p * HP, (p + 1) * HP)
        rp = e["xr"] + wg[idx][:P2, :HP]
        y0 = wg[idx][:P2, HP:]
        gm = wg[idx][P2:, :HP]
        hm = wg[idx][P2:, HP:]
        st = state_ref[p]
        yg = _dot(jnp.concatenate([rp, gm], axis=0), st)
        pc_col = jnp.sum(jnp.where(eye_k, e["pc"], 0.0), axis=1, keepdims=True)
        state_ref[p] = pc_col * st + yg[P2:] + hm
        ys = yg[:P2] + y0
        mu = jnp.sum(ys, axis=-1, keepdims=True) * (1.0 / N)
        yc = jnp.where(stack_mask, ys - mu, 0.0)
        var = jnp.sum(yc * yc, axis=-1, keepdims=True) * (1.0 / N)
        yn = yc * lax.rsqrt(var + GN_EPS)
        bonus = jnp.sum(stack(bonus_w[rows, lanes]), axis=-1, keepdims=True) * e["vs"]
        yn = yn[:C] + yn[C:]
        bonus = bonus[:C] + bonus[C:]
        o_ref[rows, lanes] = ((yn * gnw_ref[:, lanes] + gnb_ref[:, lanes] + bonus)
                              * gg_ref[rows, lanes].astype(F32)).astype(o_ref.dtype)


def _rwkv(zr, zk, zv, wp, ap, gg, mu_rkv, w0, a0, k_k, k_a, r_k, gn_w, gn_b, B, S):
    T = zr.shape[0]
    C = min(RWKV_ROWS, S)
    nc = S // C
    W = RWKV_WIDTH
    HP = 2 * RWKV_HEAD
    row = lambda b, i: (b * nc + i, 0)
    full = lambda b, i: (0, 0)

    def fs(a):
        return pl.BlockSpec(a.shape, full)

    return pl.pallas_call(
        _rwkv_kernel,
        grid=(B, nc),
        in_specs=[pl.BlockSpec((C, W), row)] * 6 + [fs(mu_rkv), fs(w0), fs(a0), fs(k_k), fs(k_a),
                                                    fs(r_k), fs(gn_w), fs(gn_b)],
        out_specs=pl.BlockSpec((C, W), row),
        out_shape=jax.ShapeDtypeStruct((T, W), BF16),
        scratch_shapes=[pltpu.VMEM((W // HP, HP, HP), F32),
                        pltpu.VMEM((V7X_SUBLANES, 3 * W), F32)],
        compiler_params=_cparams(("arbitrary", "arbitrary")),
        name="rwkv7",
    )(zr, zk, zv, wp, ap, gg, mu_rkv, w0, a0, k_k, k_a, r_k, gn_w, gn_b)


ROUTE_COLS = 8
ROUTER_GROUP_LANE = 0
ROUTER_EXPERT_LANE = N_GROUPS


def _out_router_kernel(x_ref, od_ref, orw_ref, wo_ref, mg_ref, wrh_ref, wrl_ref,
                       x1_out, hm_out, route_out, cnt_out, sl_out):
    tm = PROJ_ROWS if x_ref.shape[0] >= PROJ_ROWS else x_ref.shape[0]
    tiles = [slice(t * tm, (t + 1) * tm) for t in range(x_ref.shape[0] // tm)]
    x1 = [x_ref[r, :] + jnp.dot(od_ref[r, :], wo_ref[0:DIFF_WIDTH, :], preferred_element_type=F32)
          + jnp.dot(orw_ref[r, :], wo_ref[DIFF_WIDTH:, :], preferred_element_type=F32) for r in tiles]
    hm = [_rms(v, mg_ref[...], NORM_EPS) for v in x1]
    for r, v, h in zip(tiles, x1, hm):
        x1_out[r, :] = v
        hm_out[r, :] = h.astype(hm_out.dtype)

    def logits(h):
        hi = h.astype(BF16)
        lo = (h - hi.astype(F32)).astype(BF16)
        return (jnp.dot(hi, wrh_ref[...], preferred_element_type=F32)
                + jnp.dot(hi, wrl_ref[...], preferred_element_type=F32)
                + jnp.dot(lo, wrh_ref[...], preferred_element_type=F32))

    lgs = [logits(h) for h in hm]
    lane = lax.broadcasted_iota(jnp.int32, (tm, V7X_LANES), 1).astype(F32)
    big = float(V7X_LANES)
    ninf = -jnp.inf
    gmask = lane < N_GROUPS

    def choose(lg):
        gmax = jnp.max(jnp.where(gmask, lg, ninf), axis=-1, keepdims=True)
        g_sel = jnp.min(jnp.where(gmask & (lg == gmax), lane, big), axis=-1, keepdims=True)
        g_w = 1.0 / jnp.sum(jnp.where(gmask, jnp.exp(lg - gmax), 0.0), axis=-1, keepdims=True)
        lo_lane = ROUTER_EXPERT_LANE + g_sel * EXPERTS_PER_GROUP
        emask = (lane >= lo_lane) & (lane < lo_lane + EXPERTS_PER_GROUP)
        v1 = jnp.max(jnp.where(emask, lg, ninf), axis=-1, keepdims=True)
        i1 = jnp.min(jnp.where(emask & (lg == v1), lane, big), axis=-1, keepdims=True)
        emask2 = emask & (lane != i1)
        v2 = jnp.max(jnp.where(emask2, lg, ninf), axis=-1, keepdims=True)
        i2 = jnp.min(jnp.where(emask2 & (lg == v2), lane, big), axis=-1, keepdims=True)
        e2x = jnp.exp(v2 - v1)
        den = 1.0 + e2x
        return i1, i2, (1.0 / den) * g_w, (e2x / den) * g_w

    chosen = [choose(lg) for lg in lgs]

    ri = lax.broadcasted_iota(jnp.int32, (tm, tm), 0)
    ci = lax.broadcasted_iota(jnp.int32, (tm, tm), 1)
    before = (ci < ri).astype(BF16)
    li = lax.broadcasted_iota(jnp.int32, (V7X_LANES, V7X_LANES), 0)
    lj = lax.broadcasted_iota(jnp.int32, (V7X_LANES, V7X_LANES), 1)
    lanes_before = (li < lj).astype(BF16)
    ohs = [((lane == i1) | (lane == i2)).astype(F32) for i1, i2, _, _ in chosen]
    prefixes = [jnp.dot(before, oh.astype(BF16), preferred_element_type=F32) for oh in ohs]
    counts = [jnp.broadcast_to(jnp.sum(oh, axis=0, keepdims=True), (V7X_SUBLANES, V7X_LANES)) for oh in ohs]
    padded = [jnp.floor((c + (RUN_ALIGN - 1)) * (1.0 / RUN_ALIGN)) * RUN_ALIGN for c in counts]
    starts = [jnp.dot(pc.astype(BF16), lanes_before, preferred_element_type=F32) for pc in padded]
    col = lax.broadcasted_iota(jnp.int32, (tm, V7X_LANES), 1)
    for t, (r, (i1, i2, wt1, wt2), cnt, start, prefix) in enumerate(zip(tiles, chosen, counts, starts, prefixes)):
        row = prefix + start[0:1, :]
        sl1 = jnp.sum(jnp.where(lane == i1, row, 0.0), axis=-1, keepdims=True)
        sl2 = jnp.sum(jnp.where(lane == i2, row, 0.0), axis=-1, keepdims=True)
        cnt_out[t * V7X_SUBLANES:(t + 1) * V7X_SUBLANES, :] = cnt
        wide = jnp.where(col == 0, sl1, jnp.where(col == 1, sl2, jnp.where(col == 2, wt1, jnp.where(
            col == 3, wt2, 0.0))))
        route_out[r, :] = wide[:, 0:ROUTE_COLS]
        sl_out[t] = wide.T[0:V7X_SUBLANES, :].astype(sl_out.dtype)


def _out_router(x2, od, orw, w_out, moe_g, wr_hi, wr_lo):
    T, D = x2.shape
    tm = min(PROJ_ROWS, T)
    nt = T // tm
    tps = TILES_PER_STEP if nt % TILES_PER_STEP == 0 else 1
    tr = tm * tps
    row = lambda i: (i, 0)
    full = lambda i: (0, 0)

    def fs(a):
        return pl.BlockSpec(a.shape, full)

    return pl.pallas_call(
        _out_router_kernel,
        grid=(nt // tps,),
        in_specs=[pl.BlockSpec((tr, D), row), pl.BlockSpec((tr, DIFF_WIDTH), row),
                  pl.BlockSpec((tr, RWKV_WIDTH), row), fs(w_out), fs(moe_g), fs(wr_hi), fs(wr_lo)],
        out_specs=[pl.BlockSpec((tr, D), row), pl.BlockSpec((tr, D), row),
                   pl.BlockSpec((tr, ROUTE_COLS), row), pl.BlockSpec((tps * V7X_SUBLANES, V7X_LANES), row),
                   pl.BlockSpec((tps, V7X_SUBLANES, tm), lambda i: (i, 0, 0))],
        out_shape=[jax.ShapeDtypeStruct((T, D), F32), jax.ShapeDtypeStruct((T, D), BF16),
                   jax.ShapeDtypeStruct((T, ROUTE_COLS), F32),
                   jax.ShapeDtypeStruct((nt * V7X_SUBLANES, V7X_LANES), F32),
                   jax.ShapeDtypeStruct((nt, V7X_SUBLANES, tm), jnp.int32)],
        compiler_params=_cparams(("arbitrary",)),
        name="out_router",
    )(x2, od, orw, w_out, moe_g, wr_hi, wr_lo)


RUN_ALIGN = V7X_SUBLANES
RUN_CLASSES = 6
TAB_COUNT, TAB_NUSED = 0, RUN_CLASSES
TAB_SRC = V7X_SUBLANES
TAB_DST = TAB_SRC + RUN_CLASSES * N_EXPERTS
TAB_WORDS = 4 * V7X_LANES


def _stage_rows(tm):
    need = 2 * tm + N_EXPERTS * (RUN_ALIGN - 1)
    return -(-need // V7X_LANES) * V7X_LANES


def _run_table(src, dst, length, spare):
    i32 = jnp.int32
    n = src.shape[0]
    cls = jnp.arange(RUN_CLASSES, dtype=i32)[None, :, None]
    ln = length[:, None, :]
    has = ((ln >> 3) >> cls) & 1
    off = ln & ~((2 * RUN_ALIGN << cls) - 1)
    pos = jnp.cumsum(has, axis=-1) - 1
    k = jnp.arange(N_EXPERTS, dtype=i32)
    pick = (has[..., None] == 1) & (pos[..., None] == k)

    def compact(v):
        return jnp.sum(jnp.where(pick, (v[:, None, :] + off)[..., None], 0), axis=2).reshape(n, -1)

    head = jnp.concatenate([jnp.sum(has, axis=-1), jnp.broadcast_to(spare, (n, 1)).astype(i32),
                            jnp.zeros((n, TAB_SRC - RUN_CLASSES - 1), i32)], axis=1)
    tab = jnp.concatenate([head, compact(src), compact(dst)], axis=1)
    return jnp.pad(tab, ((0, 0), (0, TAB_WORDS - tab.shape[1]))).reshape(n, 1, TAB_WORDS)


def _run_copies(tab_ref, tile, make_copy, op):
    for c in range(RUN_CLASSES):
        size = RUN_ALIGN << c

        def one(k, carry, c=c, size=size):
            src = tab_ref[tile, 0, TAB_SRC + c * N_EXPERTS + k]
            dst = tab_ref[tile, 0, TAB_DST + c * N_EXPERTS + k]
            getattr(make_copy(pl.multiple_of(src, RUN_ALIGN), pl.multiple_of(dst, RUN_ALIGN), size), op)()
            return carry

        lax.fori_loop(0, tab_ref[tile, 0, TAB_COUNT + c], one, 0)


def _dispatch_kernel(tab_ref, tail_ref, sl_ref, hm_ref, xb_ref, stage_ref, zero_ref, sem, zsem):
    tps, ns = stage_ref.shape[0], stage_ref.shape[1]
    tm = hm_ref.shape[0] // tps
    bm = zero_ref.shape[0]
    nb = xb_ref.shape[0] // bm
    n_used = tail_ref[0, 0, TAB_NUSED]

    def tile_copy(t):
        def make_copy(src, dst, size):
            return pltpu.make_async_copy(stage_ref.at[t, pl.ds(src, size)], xb_ref.at[pl.ds(dst, size)], sem)
        return make_copy

    def zero_copy(src, dst, size):
        return pltpu.make_async_copy(zero_ref.at[pl.ds(src, size)], xb_ref.at[pl.ds(dst, size)], zsem)

    def zero_blocks(op):
        def one(b, c):
            getattr(zero_copy(0, pl.multiple_of(b * bm, bm), bm), op)()
            return c
        lax.fori_loop(n_used, nb, one, 0)

    @pl.when(pl.program_id(0) == 0)
    def _():
        zero_ref[...] = jnp.zeros_like(zero_ref)
        _run_copies(tail_ref, 0, zero_copy,"start")
        zero_blocks("start")

    srow = lax.broadcasted_iota(jnp.int32, (ns, tm), 0)
    for t in range(tps):
        sel = (srow == sl_ref[t, 0:1, :]) | (srow == sl_ref[t, 1:2, :])
        stage_ref[t] = _pack16(jnp.dot(sel.astype(BF16), hm_ref[t * tm:(t + 1) * tm, :],
                                       preferred_element_type=F32))
        _run_copies(tab_ref, t, tile_copy(t),"start")
    for t in range(tps):
        _run_copies(tab_ref, t, tile_copy(t),"wait")

    @pl.when(pl.program_id(0) == pl.num_programs(0) - 1)
    def _():
        _run_copies(tail_ref, 0, zero_copy,"wait")
        zero_blocks("wait")


def _dispatch(tab, tail, sl_rows, hm, P):
    T, D = hm.shape
    tm = min(PROJ_ROWS, T)
    nt = T // tm
    tps = TILES_PER_STEP if nt % TILES_PER_STEP == 0 else 1
    return pl.pallas_call(
        _dispatch_kernel,
        grid=(nt // tps,),
        in_specs=[pl.BlockSpec((tps, 1, tab.shape[-1]), lambda i: (i, 0, 0), memory_space=pltpu.SMEM),
                  pl.BlockSpec((1, 1, tail.shape[-1]), lambda i: (0, 0, 0), memory_space=pltpu.SMEM),
                  pl.BlockSpec((tps, V7X_SUBLANES, tm), lambda i: (i, 0, 0)),
                  pl.BlockSpec((tps * tm, D), lambda i: (i, 0))],
        out_specs=pl.BlockSpec(memory_space=pl.ANY),
        out_shape=jax.ShapeDtypeStruct((P, D // 2), PACKED),
        scratch_shapes=[pltpu.VMEM((tps, _stage_rows(tm), D // 2), PACKED),
                        pltpu.VMEM((MOE_ROWS, D // 2), PACKED),
                        pltpu.SemaphoreType.DMA(()), pltpu.SemaphoreType.DMA(())],
        compiler_params=_cparams(("arbitrary",)),
        name="moe_dispatch",
    )(tab, tail, sl_rows, hm)


def _expert_kernel(be_ref, nu_ref, xb_ref, wg_ref, wu_ref, wd_ref, yb_ref, wgb, wub, wdb):
    i = pl.program_id(0)
    prev = be_ref[jnp.maximum(i - 1, 0)]
    changed = (i == 0) | (be_ref[i] != prev)

    @pl.when(changed)
    def _():
        wgb[...] = wg_ref[0].astype(BF16)
        wub[...] = wu_ref[0].astype(BF16)
        wdb[...] = wd_ref[0].astype(BF16)

    @pl.when(i < nu_ref[0])
    def _():
        xb = _unpack16(xb_ref[...]).astype(BF16)
        gate = jnp.dot(xb, wgb[...], preferred_element_type=F32)
        up = jnp.dot(xb, wub[...], preferred_element_type=F32)
        hdn = (gate * jax.nn.sigmoid(gate)) * up
        y = jnp.dot(hdn.astype(BF16), wdb[...], preferred_element_type=F32)
        yb_ref[...] = _pack16(y.astype(BF16).astype(F32))

    @pl.when(i >= nu_ref[0])
    def _():
        yb_ref[...] = jnp.zeros_like(yb_ref)


def _experts(block_e, n_used, xb, w_gate, w_up, w_down):
    P, DP = xb.shape
    D = 2 * DP
    bm = MOE_ROWS
    nb = P // bm
    E = D_EXPERT
    grid_spec = pltpu.PrefetchScalarGridSpec(
        num_scalar_prefetch=2,
        grid=(nb,),
        in_specs=[pl.BlockSpec((bm, DP), lambda i, be, nu: (jnp.minimum(i, nu[0] - 1), 0)),
                  pl.BlockSpec((1, D, E), lambda i, be, nu: (be[i], 0, 0)),
                  pl.BlockSpec((1, D, E), lambda i, be, nu: (be[i], 0, 0)),
                  pl.BlockSpec((1, E, D), lambda i, be, nu: (be[i], 0, 0))],
        out_specs=pl.BlockSpec((bm, DP), lambda i, be, nu: (jnp.where(i < nu[0], i, nb - 1), 0)),
        scratch_shapes=[pltpu.VMEM((D, E), BF16), pltpu.VMEM((D, E), BF16), pltpu.VMEM((E, D), BF16)],
    )
    return pl.pallas_call(
        _expert_kernel,
        grid_spec=grid_spec,
        out_shape=jax.ShapeDtypeStruct((P, DP), PACKED),
        input_output_aliases={2: 0},
        compiler_params=_cparams(("arbitrary",)),
        name="moe_experts",
    )(block_e, n_used, xb, w_gate, w_up, w_down)


def _final_kernel(tab_ref, x1_ref, route_ref, p_ref, wpl_ref, plg_ref, pgg_ref, wgate_ref, yb_ref,
                  o_ref, stage_ref, sem):
    tps, ns = stage_ref.shape[0], stage_ref.shape[1]
    tm = x1_ref.shape[0] // tps
    tiles = [slice(t * tm, (t + 1) * tm) for t in range(tps)]
    stage_ref[...] = jnp.zeros_like(stage_ref)

    def tile_copy(t):
        def make_copy(src, dst, size):
            return pltpu.make_async_copy(yb_ref.at[pl.ds(dst, size)], stage_ref.at[t, pl.ds(src, size)], sem)
        return make_copy

    for t in range(tps):
        _run_copies(tab_ref, t, tile_copy(t),"start")
    pe = [_rms(jnp.dot(p_ref[r, :].astype(BF16), wpl_ref[...], preferred_element_type=F32),
               plg_ref[...], NORM_EPS) for r in tiles]
    for t in range(tps):
        _run_copies(tab_ref, t, tile_copy(t),"wait")

    scol = lax.broadcasted_iota(jnp.int32, (tm, ns), 1).astype(F32)

    def selection(r):
        route = route_ref[r, :]
        wsel = (jnp.where(scol == route[:, 0:1], route[:, 2:3], 0.0)
                + jnp.where(scol == route[:, 1:2], route[:, 3:4], 0.0))
        w_hi = wsel.astype(BF16)
        return w_hi, (wsel - w_hi.astype(F32)).astype(BF16)

    wsel = [selection(r) for r in tiles]
    yb16 = [_unpack16(stage_ref[t]).astype(BF16) for t in range(tps)]
    moe = [jnp.dot(w_hi, y, preferred_element_type=F32) + jnp.dot(w_lo, y, preferred_element_type=F32)
           for (w_hi, w_lo), y in zip(wsel, yb16)]
    x2 = [x1_ref[r, :] + m for r, m in zip(tiles, moe)]
    gate = [jax.nn.sigmoid(_dot(_rms(v, pgg_ref[...], NORM_EPS), wgate_ref[...])) for v in x2]
    for r, v, e, g in zip(tiles, x2, pe, gate):
        o_ref[r, :] = v + e * g


def _final(tab, x1, route, p2, w_pl, pl_g, pl_gate_g, w_gate, yb):
    T, D = x1.shape
    tm = min(PROJ_ROWS, T)
    nt = T // tm
    tps = TILES_PER_STEP if nt % TILES_PER_STEP == 0 else 1
    tr = tps * tm
    row = lambda i: (i, 0)
    full = lambda i: (0, 0)

    def fs(a):
        return pl.BlockSpec(a.shape, full)

    return pl.pallas_call(
        _final_kernel,
        grid=(nt // tps,),
        in_specs=[pl.BlockSpec((tps, 1, tab.shape[-1]), lambda i: (i, 0, 0), memory_space=pltpu.SMEM),
                  pl.BlockSpec((tr, D), row), pl.BlockSpec((tr, ROUTE_COLS), row),
                  pl.BlockSpec((tr, PL_DIM), row), fs(w_pl), fs(pl_g), fs(pl_gate_g), fs(w_gate),
                  pl.BlockSpec(memory_space=pl.ANY)],
        out_specs=pl.BlockSpec((tr, D), row),
        out_shape=jax.ShapeDtypeStruct((T, D), F32),
        scratch_shapes=[pltpu.VMEM((tps, _stage_rows(tm), D // 2), PACKED), pltpu.SemaphoreType.DMA(())],
        compiler_params=_cparams(("arbitrary",)),
        name="combine_final",
    )(tab, x1, route, p2, w_pl, pl_g, pl_gate_g, w_gate, yb)


def kernel(x, p, positions, attn_norm_g, w_in, q_norm_g, k_norm_g, lambda_q1, lambda_k1, lambda_q2, lambda_k2, diff_out_g, mu_rkv, mu_wag, w0, w_lora_a, w_lora_b, a0, a_lora_a, a_lora_b, g_lora_a, g_lora_b, k_k, k_a, r_k, gn_w, gn_b, w_out, moe_norm_g, w_group, w_expert_router, w_gate, w_up, w_down, w_pl, pl_norm_g, pl_gate_norm_g, w_pl_gate):
    B, S, D = x.shape
    T = B * S
    assert p.shape[0] == 1, "one layer"
    x2 = x.reshape(T, D)
    p2 = p[0].reshape(T, PL_DIM)
    pos = positions.astype(F32).reshape(T, 1)

    def row(a):
        return a.reshape(1, -1).astype(F32)

    half = DIFF_HEAD_DIM // 2
    inv_freq = ROPE_THETA ** (-jnp.arange(half, dtype=F32) / half)
    invf = jnp.tile(inv_freq, V7X_LANES // half).reshape(1, V7X_LANES)
    reps = DIFF_WIDTH // DIFF_HEAD_DIM

    q, k, v, zr, zk, zv, wp, ap, gg = _in_proj(
        x2, pos, row(attn_norm_g[0]), w_in[0].astype(BF16),
        w_lora_a[0].astype(BF16), a_lora_a[0].astype(BF16), g_lora_a[0].astype(BF16),
        w_lora_b[0].astype(BF16), a_lora_b[0].astype(BF16), g_lora_b[0].astype(BF16),
        mu_wag[0].astype(F32), row(jnp.tile(q_norm_g[0], reps)), row(jnp.tile(k_norm_g[0], reps)),
        invf, B, S)

    o_diff = _attention(q, k, v, row(lambda_q1[0]), row(lambda_k1[0]), row(lambda_q2[0]),
                        row(lambda_k2[0]), diff_out_g[0].astype(F32).reshape(-1, 1), B, S)
    o_rwkv = _rwkv(zr, zk, zv, wp, ap, gg, mu_rkv[0].astype(F32), row(w0[0]), row(a0[0]),
                   row(k_k[0]), row(k_a[0]), row(r_k[0]), row(gn_w[0]), row(gn_b[0]), B, S)

    wr = jnp.concatenate([w_group[0], jnp.transpose(w_expert_router[0], (1, 0, 2)).reshape(D, N_EXPERTS)],
                         axis=1).astype(F32)
    wr = jnp.pad(wr, ((0, 0), (0, V7X_LANES - wr.shape[1])))
    wr_hi = wr.astype(BF16)
    wr_lo = (wr - wr_hi.astype(F32)).astype(BF16)
    x1, hm, route, cnt, sl_rows = _out_router(x2, o_diff, o_rwkv, w_out[0].astype(BF16), row(moe_norm_g[0]),
                                     wr_hi, wr_lo)

    bm = MOE_ROWS
    tm = min(PROJ_ROWS, T)
    nt = T // tm
    i32 = jnp.int32
    cnt_te = cnt.reshape(nt, V7X_SUBLANES, V7X_LANES)[:, 0, ROUTER_EXPERT_LANE:ROUTER_EXPERT_LANE + N_EXPERTS]
    len_te = (cnt_te.astype(i32) + RUN_ALIGN - 1) // RUN_ALIGN * RUN_ALIGN
    tot_e = jnp.sum(len_te, axis=0)
    pcounts = (tot_e + bm - 1) // bm * bm
    pends = jnp.cumsum(pcounts)
    pstarts = pends - pcounts
    dst_te = pstarts[None, :] + jnp.cumsum(len_te, axis=0) - len_te
    src_te = jnp.cumsum(len_te, axis=1) - len_te
    tab = _run_table(src_te, dst_te, len_te, 0)
    nb = -(-(2 * T + nt * N_EXPERTS * (RUN_ALIGN - 1)) // bm) + N_EXPERTS
    P = nb * bm
    n_used = (pends[-1] // bm).astype(i32).reshape(1)
    block_start = jnp.arange(nb, dtype=i32) * bm
    block_e = jnp.minimum(jnp.sum((pends[None, :] <= block_start[:, None]).astype(i32), axis=1),
                          N_EXPERTS - 1)
    tail = _run_table(jnp.zeros((1, N_EXPERTS), i32), (pstarts + tot_e)[None, :], (pcounts - tot_e)[None, :],
                      n_used)
    xb = _dispatch(tab, tail, sl_rows, hm, P)
    yb = _experts(block_e, n_used, xb, w_gate[0], w_up[0], w_down[0])
    out = _final(tab, x1, route, p2, w_pl[0].astype(BF16), row(pl_norm_g[0]), row(pl_gate_norm_g[0]),
                 w_pl_gate[0].astype(BF16), yb)
    return out.reshape(B, S, D)
```

```python
import functools
import math

import jax
import jax.numpy as jnp
from jax import lax
from jax.experimental import pallas as pl
from jax.experimental.pallas import tpu as pltpu

F32 = jnp.float32
BF16 = jnp.bfloat16

D_MODEL = 1024
PL_DIM = 256
DIFF_WIDTH = 512
RWKV_WIDTH = 512
DIFF_HEAD_DIM = 64
DIFF_HEADS = 4
RWKV_HEAD = 64
RWKV_HEADS = 8
D_DECAY_LORA = 64
D_AAA_LORA = 64
D_GATE_LORA = 160
ROPE_THETA = 10000.0
NORM_EPS = 1e-6
SUBLN_EPS = 1e-5
GN_EPS = 64e-5
N_GROUPS = 4
EXPERTS_PER_GROUP = 8
N_EXPERTS = 32
D_EXPERT = 512
LAM_INIT = 0.8 - 0.6 * math.exp(0.0)

V7X_LANES = 128
V7X_SUBLANES = 8
V7X_VMEM_BYTES = 64 * 1024 * 1024

IN_PROJ_ROWS = 256
IN_PROJ_TILES_PER_STEP = 2
PROJ_ROWS = 256
TILES_PER_STEP = 4
ATTN_ROWS = 256
ATTN_UNROLL = 4
ATTN_HEADS_PER_STEP = 2
RWKV_CHUNK = 64
RWKV_GROUP = 16
RWKV_ROWS = 256
MOE_ROWS = 512
VMEM_LIMIT = 56 * 1024 * 1024


def _cparams(sem):
    return pltpu.CompilerParams(dimension_semantics=sem, vmem_limit_bytes=VMEM_LIMIT)


def _dot(a, b):
    return jnp.dot(a.astype(BF16), b.astype(BF16), preferred_element_type=F32)


def _dot_nt(a, b):
    return lax.dot_general(a.astype(BF16), b.astype(BF16), (((1,), (1,)), ((), ())),
                           preferred_element_type=F32)


def _dot_tn(a, b):
    return lax.dot_general(a.astype(BF16), b.astype(BF16), (((0,), (0,)), ((), ())),
                           preferred_element_type=F32)


def _split3(x):
    hi = x.astype(BF16)
    r1 = x - hi.astype(F32)
    mid = r1.astype(BF16)
    lo = (r1 - mid.astype(F32)).astype(BF16)
    return hi, mid, lo


def _dot_exact_lhs(m01, x):
    hi, mid, lo = _split3(x)
    return (jnp.dot(m01, hi, preferred_element_type=F32)
            + jnp.dot(m01, mid, preferred_element_type=F32)
            + jnp.dot(m01, lo, preferred_element_type=F32))


PACKED = jnp.uint32


def _pack16(x):
    h = x.shape[1] // 2
    lo = lax.bitcast_convert_type(x[:, :h], PACKED) >> 16
    hi = lax.bitcast_convert_type(x[:, h:], PACKED) & jnp.uint32(0xFFFF0000)
    return hi | lo


def _unpack16(w):
    lo = lax.bitcast_convert_type(w << 16, F32)
    hi = lax.bitcast_convert_type(w & jnp.uint32(0xFFFF0000), F32)
    return jnp.concatenate([lo, hi], axis=1)


def _sumsq_64(z):
    m, n = z.shape
    group = V7X_LANES // 2
    first = lax.broadcasted_iota(jnp.int32, (m, V7X_LANES), 1) < group
    out = []
    for c in range(0, n, V7X_LANES):
        sq = z[:, c:c + V7X_LANES] * z[:, c:c + V7X_LANES]
        s0 = jnp.sum(jnp.where(first, sq, 0.0), axis=-1, keepdims=True)
        s1 = jnp.sum(jnp.where(first, 0.0, sq), axis=-1, keepdims=True)
        out.append(jnp.where(first, s0, s1))
    return jnp.concatenate(out, axis=1)


def _rms(x, g, eps):
    return x * lax.rsqrt(jnp.mean(x * x, axis=-1, keepdims=True) + eps) * g


def _shift_rows(z, prev_row):
    rolled = pltpu.roll(z, 1, axis=0)
    row = lax.broadcasted_iota(jnp.int32, z.shape, 0)
    return jnp.where(row == 0, prev_row, rolled)


def _in_proj_kernel(x_ref, pos_ref, g_ref, win_ref, w1_ref, a1_ref, g1_ref, w2_ref, a2_ref, g2_ref,
                    mu_ref, qg_ref, kg_ref, invf_ref,
                    q_out, k_out, v_out, zr_out, zk_out, zv_out, wp_out, ap_out, gg_out,
                    carry_ref):
    tm = IN_PROJ_ROWS if x_ref.shape[0] >= IN_PROJ_ROWS else x_ref.shape[0]
    tiles = [slice(t * tm, (t + 1) * tm) for t in range(x_ref.shape[0] // tm)]
    w = DIFF_WIDTH
    half = DIFF_HEAD_DIM // 2

    @pl.when(pl.program_id(1) == 0)
    def _():
        carry_ref[...] = jnp.zeros_like(carry_ref)

    prev = carry_ref[V7X_SUBLANES - 1:V7X_SUBLANES, :]
    hns, dhs = [], []
    for r in tiles:
        hn = _rms(x_ref[r, :], g_ref[...], NORM_EPS)
        dhs.append(_shift_rows(hn, prev) - hn)
        hns.append(hn)
        prev = hn[tm - 1:tm, :]
    carry_ref[...] = hns[-1][tm - V7X_SUBLANES:tm, :]

    zq, zk, low = [], [], []
    for r, hn, dh in zip(tiles, hns, dhs):
        hb = hn.astype(BF16)

        def proj(c, hb=hb):
            return jnp.dot(hb, win_ref[:, c * w:(c + 1) * w], preferred_element_type=F32)

        zq.append(proj(0))
        zk.append(proj(1))
        low.append((_dot(hn + dh * mu_ref[0:1, :], w1_ref[...]), _dot(hn + dh * mu_ref[1:2, :], a1_ref[...]),
                    _dot(hn + dh * mu_ref[2:3, :], g1_ref[...])))
        v_out[r, :] = proj(2).astype(v_out.dtype)
        zr_out[r, :] = proj(3)
        zk_out[r, :] = proj(4)
        zv_out[r, :] = proj(5)

    lane = lax.broadcasted_iota(jnp.int32, (tm, w), 1)
    first_half = (lane % DIFF_HEAD_DIM) < half
    scale = DIFF_HEAD_DIM ** -0.5 * math.log2(math.e)
    for r, q, k, (lw, la, lg) in zip(tiles, zq, zk, low):
        ang = pos_ref[r, :] * invf_ref[...]
        cosf = jnp.concatenate([jnp.cos(ang)] * (w // V7X_LANES), axis=1)
        sinf = jnp.concatenate([jnp.sin(ang)] * (w // V7X_LANES), axis=1)
        sin_signed = jnp.where(first_half, -sinf, sinf)

        def norm_rope(z, gain, cosf=cosf, sin_signed=sin_signed):
            zn = z * lax.rsqrt(_sumsq_64(z) * (1.0 / DIFF_HEAD_DIM) + NORM_EPS) * gain
            partner = jnp.where(first_half, pltpu.roll(zn, w - half, axis=1), pltpu.roll(zn, half, axis=1))
            return zn * cosf + partner * sin_signed

        q_out[r, :] = (norm_rope(q, qg_ref[...]) * scale).astype(q_out.dtype)
        k_out[r, :] = norm_rope(k, kg_ref[...]).astype(k_out.dtype)
        wp_out[r, :] = _dot(jnp.tanh(lw), w2_ref[...])
        ap_out[r, :] = _dot(la, a2_ref[...])
        gg_out[r, :] = _dot(jax.nn.sigmoid(lg), g2_ref[...])


def _in_proj(x2, pos, attn_g, w_in, w1, a1, g1, w2, a2, g2, mu_wag, qg, kg, invf, B, S):
    T, D = x2.shape
    tile = min(IN_PROJ_ROWS, S)
    tps = IN_PROJ_TILES_PER_STEP if (S // tile) % IN_PROJ_TILES_PER_STEP == 0 else 1
    tm = tile * tps
    ns = S // tm
    w = DIFF_WIDTH
    row = lambda b, i: (b * ns + i, 0)
    full = lambda b, i: (0, 0)

    def fs(a):
        return pl.BlockSpec(a.shape, full)

    outs = ([jax.ShapeDtypeStruct((T, w), BF16)] * 3 + [jax.ShapeDtypeStruct((T, w), F32)] * 6)
    return pl.pallas_call(
        _in_proj_kernel,
        grid=(B, ns),
        in_specs=[pl.BlockSpec((tm, D), row), pl.BlockSpec((tm, 1), row), fs(attn_g), fs(w_in),
                  fs(w1), fs(a1), fs(g1), fs(w2), fs(a2), fs(g2), fs(mu_wag), fs(qg), fs(kg),
                  fs(invf)],
        out_specs=[pl.BlockSpec((tm, w), row)] * 9,
        out_shape=outs,
        scratch_shapes=[pltpu.VMEM((V7X_SUBLANES, D), F32)],
        compiler_params=_cparams(("arbitrary", "arbitrary")),
        name="in_proj",
    )(x2, pos, attn_g, w_in, w1, a1, g1, w2, a2, g2, mu_wag, qg, kg, invf)


def _attn_kernel(lq1_ref, lk1_ref, lq2_ref, lk2_ref, ogt_ref, q_ref, k_ref, v_ref, o_ref,
                 vt_ref, s_ref, acc_ref):
    nh, nq, hw, tq = vt_ref.shape
    tk = tq
    heads = range(nh)
    lam = (jnp.exp(jnp.sum(lq1_ref[...] * lk1_ref[...], axis=-1, keepdims=True))
           - jnp.exp(jnp.sum(lq2_ref[...] * lk2_ref[...], axis=-1, keepdims=True)) + LAM_INIT)

    for h in heads:
        for c in range(nq):
            vt_ref[h, c] = (v_ref[c * tk:(c + 1) * tk, h * hw:(h + 1) * hw]
                            .astype(F32).T.astype(vt_ref.dtype))

    lane = lax.broadcasted_iota(jnp.int32, (tq, hw), 1)

    def stacked_queries(t):
        out = []
        for h in heads:
            q = q_ref[t * tq:(t + 1) * tq, h * hw:(h + 1) * hw]
            zero = jnp.zeros_like(q)
            out.append(jnp.concatenate([jnp.where(lane < DIFF_HEAD_DIM, q, zero),
                                        jnp.where(lane >= DIFF_HEAD_DIM, q, zero)], axis=0))
        return out

    def scores(j, qs):
        start = pl.multiple_of(j * tk, tk)
        return [lax.dot_general(k_ref[pl.ds(start, tk), h * hw:(h + 1) * hw], qs[h],
                                (((1,), (1,)), ((), ())), preferred_element_type=F32) for h in heads]

    def colmax(m, st):
        return tuple(jnp.maximum(mh, jnp.max(s, axis=0, keepdims=True)) for mh, s in zip(m, st))

    krow = lax.broadcasted_iota(jnp.int32, (tk, 2 * tq), 0)
    qcol = lax.broadcasted_iota(jnp.int32, (tk, 2 * tq), 1)
    causal = krow <= jnp.where(qcol >= tq, qcol - tq, qcol)
    ninf = tuple(jnp.full((1, 2 * tq), -jnp.inf, F32) for _ in heads)
    zeros = tuple(jnp.zeros((1, 2 * tq), F32) for _ in heads)

    def diagonal(t, qs, m):
        st = [jnp.where(causal, s, -jnp.inf) for s in scores(t, qs)]
        for h in heads:
            s_ref[t % 2, h, t] = st[h]
        return colmax(m, st)

    m_cur = diagonal(0, stacked_queries(0), ninf)
    for t in range(nq):
        slot, nslot = t % 2, (t + 1) % 2
        has_next = t + 1 < nq
        qs_next = stacked_queries(t + 1) if has_next else None
        acc_ref[...] = jnp.zeros_like(acc_ref)

        def body(j, carry, slot=slot, nslot=nslot, has_next=has_next, qs_next=qs_next, m_cur=m_cur):
            l, m_next = carry
            pt = [jnp.exp2(s_ref[slot, h, j] - m_cur[h]) for h in heads]
            pv = [jnp.dot(vt_ref[h, j], pt[h].astype(BF16), preferred_element_type=F32) for h in heads]
            for h in heads:
                acc_ref[h] += pv[h]
            l = tuple(lh + jnp.sum(p, axis=0, keepdims=True) for lh, p in zip(l, pt))
            if has_next:
                st = scores(j, qs_next)
                for h in heads:
                    s_ref[nslot, h, j] = st[h]
                m_next = colmax(m_next, st)
            return l, m_next

        lsum, m_next = lax.fori_loop(0, t + 1, body, (zeros, ninf), unroll=ATTN_UNROLL)
        if has_next:
            m_next = diagonal(t + 1, qs_next, m_next)

        for h in heads:
            acc = acc_ref[h]
            l = lsum[h]
            ot = acc[:, :tq] / l[:, :tq] - lam * (acc[:, tq:] / l[:, tq:])
            ot = ot * lax.rsqrt(jnp.mean(ot * ot, axis=0, keepdims=True) + SUBLN_EPS) * ogt_ref[...]
            o_ref[t * tq:(t + 1) * tq, h * hw:(h + 1) * hw] = (ot * (1.0 - LAM_INIT)).T.astype(o_ref.dtype)
        m_cur = m_next


def _attention(q, k, v, lq1, lk1, lq2, lk2, og, B, S):
    T = q.shape[0]
    tq = min(ATTN_ROWS, S)
    nq = S // tq
    hw = 2 * DIFF_HEAD_DIM
    nh = ATTN_HEADS_PER_STEP
    gw = nh * hw
    small = lambda b, h: (0, 0)
    seq = pl.BlockSpec((S, gw), lambda b, h: (b, h))
    return pl.pallas_call(
        _attn_kernel,
        grid=(B, DIFF_HEADS // nh),
        in_specs=[pl.BlockSpec(lq1.shape, small), pl.BlockSpec(lk1.shape, small),
                  pl.BlockSpec(lq2.shape, small), pl.BlockSpec(lk2.shape, small),
                  pl.BlockSpec(og.shape, small), seq, seq, seq],
        out_specs=seq,
        out_shape=jax.ShapeDtypeStruct((T, DIFF_WIDTH), BF16),
        scratch_shapes=[pltpu.VMEM((nh, nq, hw, tq), BF16), pltpu.VMEM((2, nh, nq, tq, 2 * tq), F32),
                        pltpu.VMEM((nh, hw, 2 * tq), F32)],
        compiler_params=_cparams(("arbitrary", "arbitrary")),
        name="diff_attn",
    )(lq1, lk1, lq2, lk2, og, q, k, v)


def _rwkv_kernel(zr_ref, zk_ref, zv_ref, wp_ref, ap_ref, gg_ref, mu_ref, w0_ref, a0_ref, kk_ref,
                 ka_ref, rk_ref, gnw_ref, gnb_ref, o_ref, state_ref, carry_ref):
    R = zr_ref.shape[0]
    C = min(RWKV_CHUNK, R)
    nch = R // C
    N = RWKV_HEAD
    W = RWKV_WIDTH
    HP = 2 * N
    n_pairs = W // HP

    @pl.when(pl.program_id(1) == 0)
    def _():
        state_ref[...] = jnp.zeros_like(state_ref)
        carry_ref[...] = jnp.zeros_like(carry_ref)

    zr, zk, zv = zr_ref[...], zk_ref[...], zv_ref[...]
    last = V7X_SUBLANES - 1
    r = zr + (_shift_rows(zr, carry_ref[last:last + 1, 0:W]) - zr) * mu_ref[0:1, :]
    k = zk + (_shift_rows(zk, carry_ref[last:last + 1, W:2 * W]) - zk) * mu_ref[1:2, :]
    v = zv + (_shift_rows(zv, carry_ref[last:last + 1, 2 * W:3 * W]) - zv) * mu_ref[2:3, :]
    carry_ref[:, 0:W] = zr[R - V7X_SUBLANES:R, :]
    carry_ref[:, W:2 * W] = zk[R - V7X_SUBLANES:R, :]
    carry_ref[:, 2 * W:3 * W] = zv[R - V7X_SUBLANES:R, :]

    lw = -math.exp(-0.5) * jax.nn.sigmoid(w0_ref[...] + wp_ref[...])
    a = jax.nn.sigmoid(a0_ref[...] + ap_ref[...])
    kk = k * kk_ref[...]
    kk = kk * lax.rsqrt(jnp.maximum(_sumsq_64(kk), 1e-24))
    k2 = k * (1.0 + (a - 1.0) * ka_ref[...])
    a_s = -kk
    b_s = kk * a

    rr = lax.broadcasted_iota(jnp.int32, (R, R), 0)
    cc = lax.broadcasted_iota(jnp.int32, (R, R), 1)
    same_chunk = (rr // C) == (cc // C)
    L = _dot_exact_lhs((same_chunk & (cc <= rr)).astype(BF16), lw)
    bonus_w = r * k2 * rk_ref[...]

    P2 = 2 * C
    sr = lax.broadcasted_iota(jnp.int32, (P2, HP), 0)
    sc = lax.broadcasted_iota(jnp.int32, (P2, HP), 1)
    stack_mask = (sr < C) == (sc < N)
    br = lax.broadcasted_iota(jnp.int32, (P2, P2), 0)
    bc = lax.broadcasted_iota(jnp.int32, (P2, P2), 1)
    same_head = (br < C) == (bc < C)
    tr = jnp.where(br >= C, br - C, br)
    tc = jnp.where(bc >= C, bc - C, bc)
    strict = same_head & (tc < tr)
    lower = same_head & (tc <= tr)
    eye_p = (br == bc).astype(F32)
    kr = lax.broadcasted_iota(jnp.int32, (HP, HP), 0)
    kc = lax.broadcasted_iota(jnp.int32, (HP, HP), 1)
    eye_k = kr == kc

    def dup(x):
        return jnp.concatenate([x, x], axis=0)

    def stack(x):
        return jnp.where(stack_mask, dup(x), 0.0)

    n_sq = int(math.log2(C)) - 1
    items = [(c, p) for c in range(nch) for p in range(n_pairs)]

    def prep(c, p):
        rows = slice(c * C, (c + 1) * C)
        lanes = slice(p * HP, (p + 1) * HP)
        Lc = L[rows, lanes]
        lwc = lw[rows, lanes]
        LC = Lc[C - 1:C, :]
        enL = jnp.exp(-Lc)
        eCL = jnp.exp(LC - Lc)
        b_c, k_c = b_s[rows, lanes], k2[rows, lanes]
        return dict(
            xa=stack(a_s[rows, lanes] * jnp.exp(Lc - lwc)).astype(BF16), xr=stack(r[rows, lanes] * jnp.exp(Lc)),
            bt=dup(b_c * enL).astype(BF16), kt=dup(k_c * enL).astype(BF16),
            bh=stack(b_c * eCL).astype(BF16), kh=stack(k_c * eCL).astype(BF16),
            vs=stack(v[rows, lanes]).astype(BF16), pc=jnp.exp(LC))

    for g0 in range(0, len(items), RWKV_GROUP):
        group = items[g0:g0 + RWKV_GROUP]
        d = [prep(c, p) for c, p in group]
        aa = [_dot_nt(jnp.concatenate([e["xa"], e["xr"].astype(BF16)], axis=0),
                      jnp.concatenate([e["bt"], e["kt"]], axis=0)) for e in d]
        a_ab = [jnp.where(strict, m[:P2, :P2], 0.0) for m in aa]
        a_ak = [jnp.where(strict, m[:P2, P2:], 0.0).astype(BF16) for m in aa]
        a_rb = [jnp.where(lower, m[P2:, :P2], 0.0).astype(BF16) for m in aa]
        a_rk = [jnp.where(lower, m[P2:, P2:], 0.0).astype(BF16) for m in aa]
        tm = [eye_p + m for m in a_ab]
        npw = a_ab
        for _ in range(n_sq):
            npw = [_dot(m, m) for m in npw]
            tm = [t + _dot(t, m) for t, m in zip(tm, npw)]
        av = [_dot(m, e["vs"]) for m, e in zip(a_ak, d)]
        z = [_dot(t, jnp.concatenate([e["xa"], x.astype(BF16)], axis=1))
             for t, e, x in zip(tm, d, av)]
        zero_v = jnp.zeros((P2, HP), BF16)
        rhs = [jnp.concatenate([x.astype(BF16), jnp.concatenate([zero_v, e["vs"]], axis=1)], axis=0)
               for e, x in zip(d, z)]
        lhs = [jnp.concatenate([jnp.concatenate([m1, m2], axis=1),
                                jnp.concatenate([e["bh"], e["kh"]], axis=0).astype(F32).T.astype(BF16)], axis=0)
               for m1, m2, e in zip(a_rb, a_rk, d)]
        wg = [_dot(a, b) for a, b in zip(lhs, rhs)]

        for idx, (c, p) in enumerate(group):
            e = d[idx]
            rows = slice(c * C, (c + 1) * C)
            lanes = slice(p * HP, (p + 1) * HP)
            rp = e["xr"] + wg[idx][:P2, :HP]
            y0 = wg[idx][:P2, HP:]
            gm = wg[idx][P2:, :HP]
            hm = wg[idx][P2:, HP:]
            st = state_ref[p]
            yg = _dot(jnp.concatenate([rp, gm], axis=0), st)
            pc_col = jnp.sum(jnp.where(eye_k, e["pc"], 0.0), axis=1, keepdims=True)
            state_ref[p] = pc_col * st + yg[P2:] + hm
            ys = yg[:P2] + y0
            mu = jnp.sum(ys, axis=-1, keepdims=True) * (1.0 / N)
            yc = jnp.where(stack_mask, ys - mu, 0.0)
            var = jnp.sum(yc * yc, axis=-1, keepdims=True) * (1.0 / N)
            yn = yc * lax.rsqrt(var + GN_EPS)
            bonus = jnp.sum(stack(bonus_w[rows, lanes]), axis=-1, keepdims=True) * stack(v[rows, lanes])
            yn = yn[:C] + yn[C:]
            bonus = bonus[:C] + bonus[C:]
            o_ref[rows, lanes] = ((yn * gnw_ref[:, lanes] + gnb_ref[:, lanes] + bonus)
                                  * gg_ref[rows, lanes]).astype(o_ref.dtype)


def _rwkv(zr, zk, zv, wp, ap, gg, mu_rkv, w0, a0, k_k, k_a, r_k, gn_w, gn_b, B, S):
    T = zr.shape[0]
    C = min(RWKV_ROWS, S)
    nc = S // C
    W = RWKV_WIDTH
    HP = 2 * RWKV_HEAD
    row = lambda b, i: (b * nc + i, 0)
    full = lambda b, i: (0, 0)

    def fs(a):
        return pl.BlockSpec(a.shape, full)

    return pl.pallas_call(
        _rwkv_kernel,
        grid=(B, nc),
        in_specs=[pl.BlockSpec((C, W), row)] * 6 + [fs(mu_rkv), fs(w0), fs(a0), fs(k_k), fs(k_a),
                                                    fs(r_k), fs(gn_w), fs(gn_b)],
        out_specs=pl.BlockSpec((C, W), row),
        out_shape=jax.ShapeDtypeStruct((T, W), BF16),
        scratch_shapes=[pltpu.VMEM((W // HP, HP, HP), F32),
                        pltpu.VMEM((V7X_SUBLANES, 3 * W), F32)],
        compiler_params=_cparams(("arbitrary", "arbitrary")),
        name="rwkv7",
    )(zr, zk, zv, wp, ap, gg, mu_rkv, w0, a0, k_k, k_a, r_k, gn_w, gn_b)


ROUTE_COLS = 8
ROUTER_GROUP_LANE = 0
ROUTER_EXPERT_LANE = N_GROUPS


def _out_router_kernel(x_ref, od_ref, orw_ref, wo_ref, mg_ref, wrh_ref, wrl_ref,
                       x1_out, hm_out, route_out, cnt_out, sl_out):
    tm = PROJ_ROWS if x_ref.shape[0] >= PROJ_ROWS else x_ref.shape[0]
    tiles = [slice(t * tm, (t + 1) * tm) for t in range(x_ref.shape[0] // tm)]
    x1 = [x_ref[r, :] + jnp.dot(od_ref[r, :], wo_ref[0:DIFF_WIDTH, :], preferred_element_type=F32)
          + jnp.dot(orw_ref[r, :], wo_ref[DIFF_WIDTH:, :], preferred_element_type=F32) for r in tiles]
    hm = [_rms(v, mg_ref[...], NORM_EPS) for v in x1]
    for r, v, h in zip(tiles, x1, hm):
        x1_out[r, :] = v
        hm_out[r, :] = h.astype(hm_out.dtype)

    def logits(h):
        hi = h.astype(BF16)
        lo = (h - hi.astype(F32)).astype(BF16)
        return (jnp.dot(hi, wrh_ref[...], preferred_element_type=F32)
                + jnp.dot(hi, wrl_ref[...], preferred_element_type=F32)
                + jnp.dot(lo, wrh_ref[...], preferred_element_type=F32))

    lgs = [logits(h) for h in hm]
    lane = lax.broadcasted_iota(jnp.int32, (tm, V7X_LANES), 1).astype(F32)
    big = float(V7X_LANES)
    ninf = -jnp.inf
    gmask = lane < N_GROUPS

    def choose(lg):
        gmax = jnp.max(jnp.where(gmask, lg, ninf), axis=-1, keepdims=True)
        g_sel = jnp.min(jnp.where(gmask & (lg == gmax), lane, big), axis=-1, keepdims=True)
        g_w = 1.0 / jnp.sum(jnp.where(gmask, jnp.exp(lg - gmax), 0.0), axis=-1, keepdims=True)
        lo_lane = ROUTER_EXPERT_LANE + g_sel * EXPERTS_PER_GROUP
        emask = (lane >= lo_lane) & (lane < lo_lane + EXPERTS_PER_GROUP)
        v1 = jnp.max(jnp.where(emask, lg, ninf), axis=-1, keepdims=True)
        i1 = jnp.min(jnp.where(emask & (lg == v1), lane, big), axis=-1, keepdims=True)
        emask2 = emask & (lane != i1)
        v2 = jnp.max(jnp.where(emask2, lg, ninf), axis=-1, keepdims=True)
        i2 = jnp.min(jnp.where(emask2 & (lg == v2), lane, big), axis=-1, keepdims=True)
        e2x = jnp.exp(v2 - v1)
        den = 1.0 + e2x
        return i1, i2, (1.0 / den) * g_w, (e2x / den) * g_w

    chosen = [choose(lg) for lg in lgs]

    ri = lax.broadcasted_iota(jnp.int32, (tm, tm), 0)
    ci = lax.broadcasted_iota(jnp.int32, (tm, tm), 1)
    before = (ci < ri).astype(BF16)
    li = lax.broadcasted_iota(jnp.int32, (V7X_LANES, V7X_LANES), 0)
    lj = lax.broadcasted_iota(jnp.int32, (V7X_LANES, V7X_LANES), 1)
    lanes_before = (li < lj).astype(BF16)
    ohs = [((lane == i1) | (lane == i2)).astype(F32) for i1, i2, _, _ in chosen]
    prefixes = [jnp.dot(before, oh.astype(BF16), preferred_element_type=F32) for oh in ohs]
    counts = [jnp.broadcast_to(jnp.sum(oh, axis=0, keepdims=True), (V7X_SUBLANES, V7X_LANES)) for oh in ohs]
    padded = [jnp.floor((c + (RUN_ALIGN - 1)) * (1.0 / RUN_ALIGN)) * RUN_ALIGN for c in counts]
    starts = [jnp.dot(pc.astype(BF16), lanes_before, preferred_element_type=F32) for pc in padded]
    col = lax.broadcasted_iota(jnp.int32, (tm, V7X_LANES), 1)
    for t, (r, (i1, i2, wt1, wt2), cnt, start, prefix) in enumerate(zip(tiles, chosen, counts, starts, prefixes)):
        row = prefix + start[0:1, :]
        sl1 = jnp.sum(jnp.where(lane == i1, row, 0.0), axis=-1, keepdims=True)
        sl2 = jnp.sum(jnp.where(lane == i2, row, 0.0), axis=-1, keepdims=True)
        cnt_out[t * V7X_SUBLANES:(t + 1) * V7X_SUBLANES, :] = cnt
        wide = jnp.where(col == 0, sl1, jnp.where(col == 1, sl2, jnp.where(col == 2, wt1, jnp.where(
            col == 3, wt2, 0.0))))
        route_out[r, :] = wide[:, 0:ROUTE_COLS]
        sl_out[t] = wide.T[0:V7X_SUBLANES, :].astype(sl_out.dtype)


def _out_router(x2, od, orw, w_out, moe_g, wr_hi, wr_lo):
    T, D = x2.shape
    tm = min(PROJ_ROWS, T)
    nt = T // tm
    tps = TILES_PER_STEP if nt % TILES_PER_STEP == 0 else 1
    tr = tm * tps
    row = lambda i: (i, 0)
    full = lambda i: (0, 0)

    def fs(a):
        return pl.BlockSpec(a.shape, full)

    return pl.pallas_call(
        _out_router_kernel,
        grid=(nt // tps,),
        in_specs=[pl.BlockSpec((tr, D), row), pl.BlockSpec((tr, DIFF_WIDTH), row),
                  pl.BlockSpec((tr, RWKV_WIDTH), row), fs(w_out), fs(moe_g), fs(wr_hi), fs(wr_lo)],
        out_specs=[pl.BlockSpec((tr, D), row), pl.BlockSpec((tr, D), row),
                   pl.BlockSpec((tr, ROUTE_COLS), row), pl.BlockSpec((tps * V7X_SUBLANES, V7X_LANES), row),
                   pl.BlockSpec((tps, V7X_SUBLANES, tm), lambda i: (i, 0, 0))],
        out_shape=[jax.ShapeDtypeStruct((T, D), F32), jax.ShapeDtypeStruct((T, D), BF16),
                   jax.ShapeDtypeStruct((T, ROUTE_COLS), F32),
                   jax.ShapeDtypeStruct((nt * V7X_SUBLANES, V7X_LANES), F32),
                   jax.ShapeDtypeStruct((nt, V7X_SUBLANES, tm), jnp.int32)],
        compiler_params=_cparams(("arbitrary",)),
        name="out_router",
    )(x2, od, orw, w_out, moe_g, wr_hi, wr_lo)


RUN_ALIGN = V7X_SUBLANES
RUN_CLASSES = 6
TAB_COUNT, TAB_NUSED = 0, RUN_CLASSES
TAB_SRC = V7X_SUBLANES
TAB_DST = TAB_SRC + RUN_CLASSES * N_EXPERTS
TAB_WORDS = 4 * V7X_LANES


def _stage_rows(tm):
    need = 2 * tm + N_EXPERTS * (RUN_ALIGN - 1)
    return -(-need // V7X_LANES) * V7X_LANES


def _run_table(src, dst, length, spare):
    i32 = jnp.int32
    n = src.shape[0]
    cls = jnp.arange(RUN_CLASSES, dtype=i32)[None, :, None]
    ln = length[:, None, :]
    has = ((ln >> 3) >> cls) & 1
    off = ln & ~((2 * RUN_ALIGN << cls) - 1)
    pos = jnp.cumsum(has, axis=-1) - 1
    k = jnp.arange(N_EXPERTS, dtype=i32)
    pick = (has[..., None] == 1) & (pos[..., None] == k)

    def compact(v):
        return jnp.sum(jnp.where(pick, (v[:, None, :] + off)[..., None], 0), axis=2).reshape(n, -1)

    head = jnp.concatenate([jnp.sum(has, axis=-1), jnp.broadcast_to(spare, (n, 1)).astype(i32),
                            jnp.zeros((n, TAB_SRC - RUN_CLASSES - 1), i32)], axis=1)
    tab = jnp.concatenate([head, compact(src), compact(dst)], axis=1)
    return jnp.pad(tab, ((0, 0), (0, TAB_WORDS - tab.shape[1]))).reshape(n, 1, TAB_WORDS)


def _run_copies(tab_ref, tile, make_copy, op):
    for c in range(RUN_CLASSES):
        size = RUN_ALIGN << c

        def one(k, carry, c=c, size=size):
            src = tab_ref[tile, 0, TAB_SRC + c * N_EXPERTS + k]
            dst = tab_ref[tile, 0, TAB_DST + c * N_EXPERTS + k]
            getattr(make_copy(pl.multiple_of(src, RUN_ALIGN), pl.multiple_of(dst, RUN_ALIGN), size), op)()
            return carry

        lax.fori_loop(0, tab_ref[tile, 0, TAB_COUNT + c], one, 0)


def _dispatch_kernel(tab_ref, tail_ref, sl_ref, hm_ref, xb_ref, stage_ref, zero_ref, sem, zsem):
    tps, ns = stage_ref.shape[0], stage_ref.shape[1]
    tm = hm_ref.shape[0] // tps
    bm = zero_ref.shape[0]
    nb = xb_ref.shape[0] // bm
    n_used = tail_ref[0, 0, TAB_NUSED]

    def tile_copy(t):
        def make_copy(src, dst, size):
            return pltpu.make_async_copy(stage_ref.at[t, pl.ds(src, size)], xb_ref.at[pl.ds(dst, size)], sem)
        return make_copy

    def zero_copy(src, dst, size):
        return pltpu.make_async_copy(zero_ref.at[pl.ds(src, size)], xb_ref.at[pl.ds(dst, size)], zsem)

    def zero_blocks(op):
        def one(b, c):
            getattr(zero_copy(0, pl.multiple_of(b * bm, bm), bm), op)()
            return c
        lax.fori_loop(n_used, nb, one, 0)

    @pl.when(pl.program_id(0) == 0)
    def _():
        zero_ref[...] = jnp.zeros_like(zero_ref)
        _run_copies(tail_ref, 0, zero_copy,"start")
        zero_blocks("start")

    srow = lax.broadcasted_iota(jnp.int32, (ns, tm), 0)
    for t in range(tps):
        sel = (srow == sl_ref[t, 0:1, :]) | (srow == sl_ref[t, 1:2, :])
        stage_ref[t] = _pack16(jnp.dot(sel.astype(BF16), hm_ref[t * tm:(t + 1) * tm, :],
                                       preferred_element_type=F32))
        _run_copies(tab_ref, t, tile_copy(t),"start")
    for t in range(tps):
        _run_copies(tab_ref, t, tile_copy(t),"wait")

    @pl.when(pl.program_id(0) == pl.num_programs(0) - 1)
    def _():
        _run_copies(tail_ref, 0, zero_copy,"wait")
        zero_blocks("wait")


def _dispatch(tab, tail, sl_rows, hm, P):
    T, D = hm.shape
    tm = min(PROJ_ROWS, T)
    nt = T // tm
    tps = TILES_PER_STEP if nt % TILES_PER_STEP == 0 else 1
    return pl.pallas_call(
        _dispatch_kernel,
        grid=(nt // tps,),
        in_specs=[pl.BlockSpec((tps, 1, tab.shape[-1]), lambda i: (i, 0, 0), memory_space=pltpu.SMEM),
                  pl.BlockSpec((1, 1, tail.shape[-1]), lambda i: (0, 0, 0), memory_space=pltpu.SMEM),
                  pl.BlockSpec((tps, V7X_SUBLANES, tm), lambda i: (i, 0, 0)),
                  pl.BlockSpec((tps * tm, D), lambda i: (i, 0))],
        out_specs=pl.BlockSpec(memory_space=pl.ANY),
        out_shape=jax.ShapeDtypeStruct((P, D // 2), PACKED),
        scratch_shapes=[pltpu.VMEM((tps, _stage_rows(tm), D // 2), PACKED),
                        pltpu.VMEM((MOE_ROWS, D // 2), PACKED),
                        pltpu.SemaphoreType.DMA(()), pltpu.SemaphoreType.DMA(())],
        compiler_params=_cparams(("arbitrary",)),
        name="moe_dispatch",
    )(tab, tail, sl_rows, hm)


def _expert_kernel(be_ref, nu_ref, first_ref, nxt_ref, slot_ref, xb_ref, wg_hbm, wu_hbm, wd_hbm, yb_ref,
                   wgf, wuf, wdf, wgb, wub, wdb, sem):
    i = pl.program_id(0)

    def weight_copies(e, s):
        return (pltpu.make_async_copy(wg_hbm.at[e], wgf.at[s], sem.at[s, 0]),
                pltpu.make_async_copy(wu_hbm.at[e], wuf.at[s], sem.at[s, 1]),
                pltpu.make_async_copy(wd_hbm.at[e], wdf.at[s], sem.at[s, 2]))

    @pl.when(i == 0)
    def _():
        for cp in weight_copies(be_ref[0], 0):
            cp.start()

    @pl.when(first_ref[i] == 1)
    def _():
        s = slot_ref[i]
        for cp in weight_copies(be_ref[i], s):
            cp.wait()
        wgb[...] = wgf[s].astype(BF16)
        wub[...] = wuf[s].astype(BF16)
        wdb[...] = wdf[s].astype(BF16)

        @pl.when(nxt_ref[i] >= 0)
        def _():
            for cp in weight_copies(nxt_ref[i], 1 - s):
                cp.start()

    @pl.when(i < nu_ref[0])
    def _():
        xb = _unpack16(xb_ref[...]).astype(BF16)
        gate = jnp.dot(xb, wgb[...], preferred_element_type=F32)
        up = jnp.dot(xb, wub[...], preferred_element_type=F32)
        hdn = (gate * jax.nn.sigmoid(gate)) * up
        y = jnp.dot(hdn.astype(BF16), wdb[...], preferred_element_type=F32)
        yb_ref[...] = _pack16(y.astype(BF16).astype(F32))

    @pl.when(i >= nu_ref[0])
    def _():
        yb_ref[...] = jnp.zeros_like(yb_ref)


def _experts(block_e, n_used, first, nxt, slot, xb, w_gate, w_up, w_down):
    P, DP = xb.shape
    D = 2 * DP
    bm = MOE_ROWS
    nb = P // bm
    E = D_EXPERT
    hbm = pl.BlockSpec(memory_space=pl.ANY)
    grid_spec = pltpu.PrefetchScalarGridSpec(
        num_scalar_prefetch=5,
        grid=(nb,),
        in_specs=[pl.BlockSpec((bm, DP), lambda i, be, nu, *_: (jnp.minimum(i, nu[0] - 1), 0)), hbm, hbm, hbm],
        out_specs=pl.BlockSpec((bm, DP), lambda i, be, nu, *_: (jnp.where(i < nu[0], i, nb - 1), 0)),
        scratch_shapes=[pltpu.VMEM((2, D, E), F32), pltpu.VMEM((2, D, E), F32), pltpu.VMEM((2, E, D), F32),
                        pltpu.VMEM((D, E), BF16), pltpu.VMEM((D, E), BF16), pltpu.VMEM((E, D), BF16),
                        pltpu.SemaphoreType.DMA((2, 3))],
    )
    return pl.pallas_call(
        _expert_kernel,
        grid_spec=grid_spec,
        out_shape=jax.ShapeDtypeStruct((P, DP), PACKED),
        input_output_aliases={5: 0},
        compiler_params=_cparams(("arbitrary",)),
        name="moe_experts",
    )(block_e, n_used, first, nxt, slot, xb, w_gate, w_up, w_down)


def _final_kernel(tab_ref, x1_ref, route_ref, p_ref, wpl_ref, plg_ref, pgg_ref, wgate_ref, yb_ref,
                  o_ref, stage_ref, sem):
    tps, ns = stage_ref.shape[0], stage_ref.shape[1]
    tm = x1_ref.shape[0] // tps
    tiles = [slice(t * tm, (t + 1) * tm) for t in range(tps)]
    stage_ref[...] = jnp.zeros_like(stage_ref)

    def tile_copy(t):
        def make_copy(src, dst, size):
            return pltpu.make_async_copy(yb_ref.at[pl.ds(dst, size)], stage_ref.at[t, pl.ds(src, size)], sem)
        return make_copy

    for t in range(tps):
        _run_copies(tab_ref, t, tile_copy(t),"start")
    pe = [_rms(jnp.dot(p_ref[r, :].astype(BF16), wpl_ref[...], preferred_element_type=F32),
               plg_ref[...], NORM_EPS) for r in tiles]
    for t in range(tps):
        _run_copies(tab_ref, t, tile_copy(t),"wait")

    scol = lax.broadcasted_iota(jnp.int32, (tm, ns), 1).astype(F32)

    def selection(r):
        route = route_ref[r, :]
        wsel = (jnp.where(scol == route[:, 0:1], route[:, 2:3], 0.0)
                + jnp.where(scol == route[:, 1:2], route[:, 3:4], 0.0))
        w_hi = wsel.astype(BF16)
        return w_hi, (wsel - w_hi.astype(F32)).astype(BF16)

    wsel = [selection(r) for r in tiles]
    yb16 = [_unpack16(stage_ref[t]).astype(BF16) for t in range(tps)]
    moe = [jnp.dot(w_hi, y, preferred_element_type=F32) + jnp.dot(w_lo, y, preferred_element_type=F32)
           for (w_hi, w_lo), y in zip(wsel, yb16)]
    x2 = [x1_ref[r, :] + m for r, m in zip(tiles, moe)]
    gate = [jax.nn.sigmoid(_dot(_rms(v, pgg_ref[...], NORM_EPS), wgate_ref[...])) for v in x2]
    for r, v, e, g in zip(tiles, x2, pe, gate):
        o_ref[r, :] = v + e * g


def _final(tab, x1, route, p2, w_pl, pl_g, pl_gate_g, w_gate, yb):
    T, D = x1.shape
    tm = min(PROJ_ROWS, T)
    nt = T // tm
    tps = TILES_PER_STEP if nt % TILES_PER_STEP == 0 else 1
    tr = tps * tm
    row = lambda i: (i, 0)
    full = lambda i: (0, 0)

    def fs(a):
        return pl.BlockSpec(a.shape, full)

    return pl.pallas_call(
        _final_kernel,
        grid=(nt // tps,),
        in_specs=[pl.BlockSpec((tps, 1, tab.shape[-1]), lambda i: (i, 0, 0), memory_space=pltpu.SMEM),
                  pl.BlockSpec((tr, D), row), pl.BlockSpec((tr, ROUTE_COLS), row),
                  pl.BlockSpec((tr, PL_DIM), row), fs(w_pl), fs(pl_g), fs(pl_gate_g), fs(w_gate),
                  pl.BlockSpec(memory_space=pl.ANY)],
        out_specs=pl.BlockSpec((tr, D), row),
        out_shape=jax.ShapeDtypeStruct((T, D), F32),
        scratch_shapes=[pltpu.VMEM((tps, _stage_rows(tm), D // 2), PACKED), pltpu.SemaphoreType.DMA(())],
        compiler_params=_cparams(("arbitrary",)),
        name="combine_final",
    )(tab, x1, route, p2, w_pl, pl_g, pl_gate_g, w_gate, yb)


def kernel(x, p, positions, attn_norm_g, w_in, q_norm_g, k_norm_g, lambda_q1, lambda_k1, lambda_q2, lambda_k2, diff_out_g, mu_rkv, mu_wag, w0, w_lora_a, w_lora_b, a0, a_lora_a, a_lora_b, g_lora_a, g_lora_b, k_k, k_a, r_k, gn_w, gn_b, w_out, moe_norm_g, w_group, w_expert_router, w_gate, w_up, w_down, w_pl, pl_norm_g, pl_gate_norm_g, w_pl_gate):
    B, S, D = x.shape
    T = B * S
    assert p.shape[0] == 1, "one layer"
    x2 = x.reshape(T, D)
    p2 = p[0].reshape(T, PL_DIM)
    pos = positions.astype(F32).reshape(T, 1)

    def row(a):
        return a.reshape(1, -1).astype(F32)

    half = DIFF_HEAD_DIM // 2
    inv_freq = ROPE_THETA ** (-jnp.arange(half, dtype=F32) / half)
    invf = jnp.tile(inv_freq, V7X_LANES // half).reshape(1, V7X_LANES)
    reps = DIFF_WIDTH // DIFF_HEAD_DIM

    q, k, v, zr, zk, zv, wp, ap, gg = _in_proj(
        x2, pos, row(attn_norm_g[0]), w_in[0].astype(BF16),
        w_lora_a[0].astype(BF16), a_lora_a[0].astype(BF16), g_lora_a[0].astype(BF16),
        w_lora_b[0].astype(BF16), a_lora_b[0].astype(BF16), g_lora_b[0].astype(BF16),
        mu_wag[0].astype(F32), row(jnp.tile(q_norm_g[0], reps)), row(jnp.tile(k_norm_g[0], reps)),
        invf, B, S)

    o_diff = _attention(q, k, v, row(lambda_q1[0]), row(lambda_k1[0]), row(lambda_q2[0]),
                        row(lambda_k2[0]), diff_out_g[0].astype(F32).reshape(-1, 1), B, S)
    o_rwkv = _rwkv(zr, zk, zv, wp, ap, gg, mu_rkv[0].astype(F32), row(w0[0]), row(a0[0]),
                   row(k_k[0]), row(k_a[0]), row(r_k[0]), row(gn_w[0]), row(gn_b[0]), B, S)

    wr = jnp.concatenate([w_group[0], jnp.transpose(w_expert_router[0], (1, 0, 2)).reshape(D, N_EXPERTS)],
                         axis=1).astype(F32)
    wr = jnp.pad(wr, ((0, 0), (0, V7X_LANES - wr.shape[1])))
    wr_hi = wr.astype(BF16)
    wr_lo = (wr - wr_hi.astype(F32)).astype(BF16)
    x1, hm, route, cnt, sl_rows = _out_router(x2, o_diff, o_rwkv, w_out[0].astype(BF16), row(moe_norm_g[0]),
                                     wr_hi, wr_lo)

    bm = MOE_ROWS
    tm = min(PROJ_ROWS, T)
    nt = T // tm
    i32 = jnp.int32
    cnt_te = cnt.reshape(nt, V7X_SUBLANES, V7X_LANES)[:, 0, ROUTER_EXPERT_LANE:ROUTER_EXPERT_LANE + N_EXPERTS]
    len_te = (cnt_te.astype(i32) + RUN_ALIGN - 1) // RUN_ALIGN * RUN_ALIGN
    tot_e = jnp.sum(len_te, axis=0)
    pcounts = (tot_e + bm - 1) // bm * bm
    pends = jnp.cumsum(pcounts)
    pstarts = pends - pcounts
    dst_te = pstarts[None, :] + jnp.cumsum(len_te, axis=0) - len_te
    src_te = jnp.cumsum(len_te, axis=1) - len_te
    tab = _run_table(src_te, dst_te, len_te, 0)
    nb = -(-(2 * T + nt * N_EXPERTS * (RUN_ALIGN - 1)) // bm) + N_EXPERTS
    P = nb * bm
    n_used = (pends[-1] // bm).astype(i32).reshape(1)
    block_start = jnp.arange(nb, dtype=i32) * bm
    block_e = jnp.minimum(jnp.sum((pends[None, :] <= block_start[:, None]).astype(i32), axis=1),
                          N_EXPERTS - 1)
    tail = _run_table(jnp.zeros((1, N_EXPERTS), i32), (pstarts + tot_e)[None, :], (pcounts - tot_e)[None, :],
                      n_used)
    xb = _dispatch(tab, tail, sl_rows, hm, P)
    blk = jnp.arange(nb, dtype=i32)
    first = ((blk < n_used[0]) & ((blk == 0) | (block_e != jnp.roll(block_e, 1)))).astype(i32)
    slot = (jnp.cumsum(first) - 1) % 2
    eidx = jnp.arange(N_EXPERTS, dtype=i32)
    later = jnp.where((pcounts > 0)[None, :] & (eidx[None, :] > eidx[:, None]), eidx[None, :], N_EXPERTS)
    nxt_e = jnp.min(later, axis=1)
    nxt = jnp.where(nxt_e == N_EXPERTS, -1, nxt_e)[block_e]
    yb = _experts(block_e, n_used, first, nxt, slot.astype(i32), xb, w_gate[0], w_up[0], w_down[0])
    out = _final(tab, x1, route, p2, w_pl[0].astype(BF16), row(pl_norm_g[0]), row(pl_gate_norm_g[0]),
                 w_pl_gate[0].astype(BF16), yb)
    return out.reshape(B, S, D)
```

```python
import functools
import math

import jax
import jax.numpy as jnp
from jax import lax
from jax.experimental import pallas as pl
from jax.experimental.pallas import tpu as pltpu

F32 = jnp.float32
BF16 = jnp.bfloat16

D_MODEL = 1024
PL_DIM = 256
DIFF_WIDTH = 512
RWKV_WIDTH = 512
DIFF_HEAD_DIM = 64
DIFF_HEADS = 4
RWKV_HEAD = 64
RWKV_HEADS = 8
D_DECAY_LORA = 64
D_AAA_LORA = 64
D_GATE_LORA = 160
ROPE_THETA = 10000.0
NORM_EPS = 1e-6
SUBLN_EPS = 1e-5
GN_EPS = 64e-5
N_GROUPS = 4
EXPERTS_PER_GROUP = 8
N_EXPERTS = 32
D_EXPERT = 512
LAM_INIT = 0.8 - 0.6 * math.exp(0.0)

V7X_LANES = 128
V7X_SUBLANES = 8
V7X_VMEM_BYTES = 64 * 1024 * 1024

IN_PROJ_ROWS = 256
IN_PROJ_TILES_PER_STEP = 2
PROJ_ROWS = 256
TILES_PER_STEP = 4
ATTN_ROWS = 256
ATTN_UNROLL = 4
ATTN_HEADS_PER_STEP = 2
RWKV_CHUNK = 64
RWKV_GROUP = 16
RWKV_ROWS = 256
MOE_ROWS = 512
VMEM_LIMIT = 56 * 1024 * 1024


def _cparams(sem):
    return pltpu.CompilerParams(dimension_semantics=sem, vmem_limit_bytes=VMEM_LIMIT)


def _dot(a, b):
    return jnp.dot(a.astype(BF16), b.astype(BF16), preferred_element_type=F32)


def _dot_nt(a, b):
    return lax.dot_general(a.astype(BF16), b.astype(BF16), (((1,), (1,)), ((), ())),
                           preferred_element_type=F32)


def _dot_tn(a, b):
    return lax.dot_general(a.astype(BF16), b.astype(BF16), (((0,), (0,)), ((), ())),
                           preferred_element_type=F32)


def _split3(x):
    hi = x.astype(BF16)
    r1 = x - hi.astype(F32)
    mid = r1.astype(BF16)
    lo = (r1 - mid.astype(F32)).astype(BF16)
    return hi, mid, lo


def _dot_exact_lhs(m01, x):
    hi, mid, lo = _split3(x)
    return (jnp.dot(m01, hi, preferred_element_type=F32)
            + jnp.dot(m01, mid, preferred_element_type=F32)
            + jnp.dot(m01, lo, preferred_element_type=F32))


PACKED = jnp.uint32


def _pack16(x):
    h = x.shape[1] // 2
    lo = lax.bitcast_convert_type(x[:, :h], PACKED) >> 16
    hi = lax.bitcast_convert_type(x[:, h:], PACKED) & jnp.uint32(0xFFFF0000)
    return hi | lo


def _unpack16(w):
    lo = lax.bitcast_convert_type(w << 16, F32)
    hi = lax.bitcast_convert_type(w & jnp.uint32(0xFFFF0000), F32)
    return jnp.concatenate([lo, hi], axis=1)


def _sumsq_64(z):
    m, n = z.shape
    group = V7X_LANES // 2
    first = lax.broadcasted_iota(jnp.int32, (m, V7X_LANES), 1) < group
    out = []
    for c in range(0, n, V7X_LANES):
        sq = z[:, c:c + V7X_LANES] * z[:, c:c + V7X_LANES]
        s0 = jnp.sum(jnp.where(first, sq, 0.0), axis=-1, keepdims=True)
        s1 = jnp.sum(jnp.where(first, 0.0, sq), axis=-1, keepdims=True)
        out.append(jnp.where(first, s0, s1))
    return jnp.concatenate(out, axis=1)


def _rms(x, g, eps):
    return x * lax.rsqrt(jnp.mean(x * x, axis=-1, keepdims=True) + eps) * g


def _shift_rows(z, prev_row):
    rolled = pltpu.roll(z, 1, axis=0)
    row = lax.broadcasted_iota(jnp.int32, z.shape, 0)
    return jnp.where(row == 0, prev_row, rolled)


def _in_proj_kernel(x_ref, pos_ref, g_ref, win_ref, w1_ref, a1_ref, g1_ref, w2_ref, a2_ref, g2_ref,
                    mu_ref, qg_ref, kg_ref, invf_ref,
                    q_out, k_out, v_out, zr_out, zk_out, zv_out, wp_out, ap_out, gg_out,
                    carry_ref):
    tm = IN_PROJ_ROWS if x_ref.shape[0] >= IN_PROJ_ROWS else x_ref.shape[0]
    tiles = [slice(t * tm, (t + 1) * tm) for t in range(x_ref.shape[0] // tm)]
    w = DIFF_WIDTH
    half = DIFF_HEAD_DIM // 2

    @pl.when(pl.program_id(1) == 0)
    def _():
        carry_ref[...] = jnp.zeros_like(carry_ref)

    prev = carry_ref[V7X_SUBLANES - 1:V7X_SUBLANES, :]
    hns, dhs = [], []
    for r in tiles:
        hn = _rms(x_ref[r, :], g_ref[...], NORM_EPS)
        dhs.append(_shift_rows(hn, prev) - hn)
        hns.append(hn)
        prev = hn[tm - 1:tm, :]
    carry_ref[...] = hns[-1][tm - V7X_SUBLANES:tm, :]

    zq, zk, low = [], [], []
    for r, hn, dh in zip(tiles, hns, dhs):
        hb = hn.astype(BF16)

        def proj(c, hb=hb):
            return jnp.dot(hb, win_ref[:, c * w:(c + 1) * w], preferred_element_type=F32)

        zq.append(proj(0))
        zk.append(proj(1))
        low.append((_dot(hn + dh * mu_ref[0:1, :], w1_ref[...]), _dot(hn + dh * mu_ref[1:2, :], a1_ref[...]),
                    _dot(hn + dh * mu_ref[2:3, :], g1_ref[...])))
        v_out[r, :] = proj(2).astype(v_out.dtype)
        zr_out[r, :] = proj(3)
        zk_out[r, :] = proj(4)
        zv_out[r, :] = proj(5)

    lane = lax.broadcasted_iota(jnp.int32, (tm, w), 1)
    first_half = (lane % DIFF_HEAD_DIM) < half
    scale = DIFF_HEAD_DIM ** -0.5 * math.log2(math.e)
    for r, q, k, (lw, la, lg) in zip(tiles, zq, zk, low):
        ang = pos_ref[r, :] * invf_ref[...]
        cosf = jnp.concatenate([jnp.cos(ang)] * (w // V7X_LANES), axis=1)
        sinf = jnp.concatenate([jnp.sin(ang)] * (w // V7X_LANES), axis=1)
        sin_signed = jnp.where(first_half, -sinf, sinf)

        def norm_rope(z, gain, cosf=cosf, sin_signed=sin_signed):
            zn = z * lax.rsqrt(_sumsq_64(z) * (1.0 / DIFF_HEAD_DIM) + NORM_EPS) * gain
            partner = jnp.where(first_half, pltpu.roll(zn, w - half, axis=1), pltpu.roll(zn, half, axis=1))
            return zn * cosf + partner * sin_signed

        q_out[r, :] = (norm_rope(q, qg_ref[...]) * scale).astype(q_out.dtype)
        k_out[r, :] = norm_rope(k, kg_ref[...]).astype(k_out.dtype)
        wp_out[r, :] = _dot(jnp.tanh(lw), w2_ref[...])
        ap_out[r, :] = _dot(la, a2_ref[...])
        gg_out[r, :] = _dot(jax.nn.sigmoid(lg), g2_ref[...])


def _in_proj(x2, pos, attn_g, w_in, w1, a1, g1, w2, a2, g2, mu_wag, qg, kg, invf, B, S):
    T, D = x2.shape
    tile = min(IN_PROJ_ROWS, S)
    tps = IN_PROJ_TILES_PER_STEP if (S // tile) % IN_PROJ_TILES_PER_STEP == 0 else 1
    tm = tile * tps
    ns = S // tm
    w = DIFF_WIDTH
    row = lambda b, i: (b * ns + i, 0)
    full = lambda b, i: (0, 0)

    def fs(a):
        return pl.BlockSpec(a.shape, full)

    outs = ([jax.ShapeDtypeStruct((T, w), BF16)] * 3 + [jax.ShapeDtypeStruct((T, w), F32)] * 6)
    return pl.pallas_call(
        _in_proj_kernel,
        grid=(B, ns),
        in_specs=[pl.BlockSpec((tm, D), row), pl.BlockSpec((tm, 1), row), fs(attn_g), fs(w_in),
                  fs(w1), fs(a1), fs(g1), fs(w2), fs(a2), fs(g2), fs(mu_wag), fs(qg), fs(kg),
                  fs(invf)],
        out_specs=[pl.BlockSpec((tm, w), row)] * 9,
        out_shape=outs,
        scratch_shapes=[pltpu.VMEM((V7X_SUBLANES, D), F32)],
        compiler_params=_cparams(("arbitrary", "arbitrary")),
        name="in_proj",
    )(x2, pos, attn_g, w_in, w1, a1, g1, w2, a2, g2, mu_wag, qg, kg, invf)


def _attn_kernel(lq1_ref, lk1_ref, lq2_ref, lk2_ref, ogt_ref, q_ref, k_ref, v_ref, o_ref,
                 vt_ref, s_ref, acc_ref):
    nh, nq, hw, tq = vt_ref.shape
    tk = tq
    heads = range(nh)
    lam = (jnp.exp(jnp.sum(lq1_ref[...] * lk1_ref[...], axis=-1, keepdims=True))
           - jnp.exp(jnp.sum(lq2_ref[...] * lk2_ref[...], axis=-1, keepdims=True)) + LAM_INIT)

    for h in heads:
        for c in range(nq):
            vt_ref[h, c] = (v_ref[c * tk:(c + 1) * tk, h * hw:(h + 1) * hw]
                            .astype(F32).T.astype(vt_ref.dtype))

    lane = lax.broadcasted_iota(jnp.int32, (tq, hw), 1)

    def stacked_queries(t):
        out = []
        for h in heads:
            q = q_ref[t * tq:(t + 1) * tq, h * hw:(h + 1) * hw]
            zero = jnp.zeros_like(q)
            out.append(jnp.concatenate([jnp.where(lane < DIFF_HEAD_DIM, q, zero),
                                        jnp.where(lane >= DIFF_HEAD_DIM, q, zero)], axis=0))
        return out

    def scores(j, qs):
        start = pl.multiple_of(j * tk, tk)
        return [lax.dot_general(k_ref[pl.ds(start, tk), h * hw:(h + 1) * hw], qs[h],
                                (((1,), (1,)), ((), ())), preferred_element_type=F32) for h in heads]

    def colmax(m, st):
        return tuple(jnp.maximum(mh, jnp.max(s, axis=0, keepdims=True)) for mh, s in zip(m, st))

    krow = lax.broadcasted_iota(jnp.int32, (tk, 2 * tq), 0)
    qcol = lax.broadcasted_iota(jnp.int32, (tk, 2 * tq), 1)
    causal = krow <= jnp.where(qcol >= tq, qcol - tq, qcol)
    ninf = tuple(jnp.full((1, 2 * tq), -jnp.inf, F32) for _ in heads)
    zeros = tuple(jnp.zeros((1, 2 * tq), F32) for _ in heads)

    def diagonal(t, qs, m):
        st = [jnp.where(causal, s, -jnp.inf) for s in scores(t, qs)]
        for h in heads:
            s_ref[t % 2, h, t] = st[h]
        return colmax(m, st)

    m_cur = diagonal(0, stacked_queries(0), ninf)
    for t in range(nq):
        slot, nslot = t % 2, (t + 1) % 2
        has_next = t + 1 < nq
        qs_next = stacked_queries(t + 1) if has_next else None
        acc_ref[...] = jnp.zeros_like(acc_ref)

        def body(j, carry, slot=slot, nslot=nslot, has_next=has_next, qs_next=qs_next, m_cur=m_cur):
            l, m_next = carry
            pt = [jnp.exp2(s_ref[slot, h, j] - m_cur[h]) for h in heads]
            pv = [jnp.dot(vt_ref[h, j], pt[h].astype(BF16), preferred_element_type=F32) for h in heads]
            for h in heads:
                acc_ref[h] += pv[h]
            l = tuple(lh + jnp.sum(p, axis=0, keepdims=True) for lh, p in zip(l, pt))
            if has_next:
                st = scores(j, qs_next)
                for h in heads:
                    s_ref[nslot, h, j] = st[h]
                m_next = colmax(m_next, st)
            return l, m_next

        lsum, m_next = lax.fori_loop(0, t + 1, body, (zeros, ninf), unroll=ATTN_UNROLL)
        if has_next:
            m_next = diagonal(t + 1, qs_next, m_next)

        for h in heads:
            acc = acc_ref[h]
            l = lsum[h]
            ot = acc[:, :tq] / l[:, :tq] - lam * (acc[:, tq:] / l[:, tq:])
            ot = ot * lax.rsqrt(jnp.mean(ot * ot, axis=0, keepdims=True) + SUBLN_EPS) * ogt_ref[...]
            o_ref[t * tq:(t + 1) * tq, h * hw:(h + 1) * hw] = (ot * (1.0 - LAM_INIT)).T.astype(o_ref.dtype)
        m_cur = m_next


def _attention(q, k, v, lq1, lk1, lq2, lk2, og, B, S):
    T = q.shape[0]
    tq = min(ATTN_ROWS, S)
    nq = S // tq
    hw = 2 * DIFF_HEAD_DIM
    nh = ATTN_HEADS_PER_STEP
    gw = nh * hw
    small = lambda b, h: (0, 0)
    seq = pl.BlockSpec((S, gw), lambda b, h: (b, h))
    return pl.pallas_call(
        _attn_kernel,
        grid=(B, DIFF_HEADS // nh),
        in_specs=[pl.BlockSpec(lq1.shape, small), pl.BlockSpec(lk1.shape, small),
                  pl.BlockSpec(lq2.shape, small), pl.BlockSpec(lk2.shape, small),
                  pl.BlockSpec(og.shape, small), seq, seq, seq],
        out_specs=seq,
        out_shape=jax.ShapeDtypeStruct((T, DIFF_WIDTH), BF16),
        scratch_shapes=[pltpu.VMEM((nh, nq, hw, tq), BF16), pltpu.VMEM((2, nh, nq, tq, 2 * tq), F32),
                        pltpu.VMEM((nh, hw, 2 * tq), F32)],
        compiler_params=_cparams(("arbitrary", "arbitrary")),
        name="diff_attn",
    )(lq1, lk1, lq2, lk2, og, q, k, v)


def _rwkv_kernel(zr_ref, zk_ref, zv_ref, wp_ref, ap_ref, gg_ref, mu_ref, w0_ref, a0_ref, kk_ref,
                 ka_ref, rk_ref, gnw_ref, gnb_ref, o_ref, state_ref, carry_ref):
    R = zr_ref.shape[0]
    C = min(RWKV_CHUNK, R)
    nch = R // C
    N = RWKV_HEAD
    W = RWKV_WIDTH
    HP = 2 * N
    n_pairs = W // HP

    @pl.when(pl.program_id(1) == 0)
    def _():
        state_ref[...] = jnp.zeros_like(state_ref)
        carry_ref[...] = jnp.zeros_like(carry_ref)

    zr, zk, zv = zr_ref[...], zk_ref[...], zv_ref[...]
    last = V7X_SUBLANES - 1
    r = zr + (_shift_rows(zr, carry_ref[last:last + 1, 0:W]) - zr) * mu_ref[0:1, :]
    k = zk + (_shift_rows(zk, carry_ref[last:last + 1, W:2 * W]) - zk) * mu_ref[1:2, :]
    v = zv + (_shift_rows(zv, carry_ref[last:last + 1, 2 * W:3 * W]) - zv) * mu_ref[2:3, :]
    carry_ref[:, 0:W] = zr[R - V7X_SUBLANES:R, :]
    carry_ref[:, W:2 * W] = zk[R - V7X_SUBLANES:R, :]
    carry_ref[:, 2 * W:3 * W] = zv[R - V7X_SUBLANES:R, :]

    lw = -math.exp(-0.5) * jax.nn.sigmoid(w0_ref[...] + wp_ref[...])
    a = jax.nn.sigmoid(a0_ref[...] + ap_ref[...])
    kk = k * kk_ref[...]
    kk = kk * lax.rsqrt(jnp.maximum(_sumsq_64(kk), 1e-24))
    k2 = k * (1.0 + (a - 1.0) * ka_ref[...])
    a_s = -kk
    b_s = kk * a

    rr = lax.broadcasted_iota(jnp.int32, (R, R), 0)
    cc = lax.broadcasted_iota(jnp.int32, (R, R), 1)
    same_chunk = (rr // C) == (cc // C)
    L = _dot_exact_lhs((same_chunk & (cc <= rr)).astype(BF16), lw)
    bonus_w = r * k2 * rk_ref[...]

    P2 = 2 * C
    sr = lax.broadcasted_iota(jnp.int32, (P2, HP), 0)
    sc = lax.broadcasted_iota(jnp.int32, (P2, HP), 1)
    stack_mask = (sr < C) == (sc < N)
    br = lax.broadcasted_iota(jnp.int32, (P2, P2), 0)
    bc = lax.broadcasted_iota(jnp.int32, (P2, P2), 1)
    same_head = (br < C) == (bc < C)
    tr = jnp.where(br >= C, br - C, br)
    tc = jnp.where(bc >= C, bc - C, bc)
    strict = same_head & (tc < tr)
    lower = same_head & (tc <= tr)
    eye_p = (br == bc).astype(F32)
    kr = lax.broadcasted_iota(jnp.int32, (HP, HP), 0)
    kc = lax.broadcasted_iota(jnp.int32, (HP, HP), 1)
    eye_k = kr == kc

    def dup(x):
        return jnp.concatenate([x, x], axis=0)

    def stack(x):
        return jnp.where(stack_mask, dup(x), 0.0)

    n_sq = int(math.log2(C)) - 1
    items = [(c, p) for c in range(nch) for p in range(n_pairs)]

    def prep(c, p):
        rows = slice(c * C, (c + 1) * C)
        lanes = slice(p * HP, (p + 1) * HP)
        Lc = L[rows, lanes]
        lwc = lw[rows, lanes]
        LC = Lc[C - 1:C, :]
        enL = jnp.exp(-Lc)
        eCL = jnp.exp(LC - Lc)
        b_c, k_c = b_s[rows, lanes], k2[rows, lanes]
        return dict(
            xa=stack(a_s[rows, lanes] * jnp.exp(Lc - lwc)).astype(BF16), xr=stack(r[rows, lanes] * jnp.exp(Lc)),
            bt=dup(b_c * enL).astype(BF16), kt=dup(k_c * enL).astype(BF16),
            bh=stack(b_c * eCL).astype(BF16), kh=stack(k_c * eCL).astype(BF16),
            vs=stack(v[rows, lanes]).astype(BF16), pc=jnp.exp(LC))

    for g0 in range(0, len(items), RWKV_GROUP):
        group = items[g0:g0 + RWKV_GROUP]
        d = [prep(c, p) for c, p in group]
        aa = [_dot_nt(jnp.concatenate([e["xa"], e["xr"].astype(BF16)], axis=0),
                      jnp.concatenate([e["bt"], e["kt"]], axis=0)) for e in d]
        a_ab = [jnp.where(strict, m[:P2, :P2], 0.0) for m in aa]
        a_ak = [jnp.where(strict, m[:P2, P2:], 0.0).astype(BF16) for m in aa]
        a_rb = [jnp.where(lower, m[P2:, :P2], 0.0).astype(BF16) for m in aa]
        a_rk = [jnp.where(lower, m[P2:, P2:], 0.0).astype(BF16) for m in aa]
        tm = [eye_p + m for m in a_ab]
        npw = a_ab
        for _ in range(n_sq):
            npw = [_dot(m, m) for m in npw]
            tm = [t + _dot(t, m) for t, m in zip(tm, npw)]
        av = [_dot(m, e["vs"]) for m, e in zip(a_ak, d)]
        z = [_dot(t, jnp.concatenate([e["xa"], x.astype(BF16)], axis=1))
             for t, e, x in zip(tm, d, av)]
        zero_v = jnp.zeros((P2, HP), BF16)
        rhs = [jnp.concatenate([x.astype(BF16), jnp.concatenate([zero_v, e["vs"]], axis=1)], axis=0)
               for e, x in zip(d, z)]
        lhs = [jnp.concatenate([jnp.concatenate([m1, m2], axis=1),
                                jnp.concatenate([e["bh"], e["kh"]], axis=0).astype(F32).T.astype(BF16)], axis=0)
               for m1, m2, e in zip(a_rb, a_rk, d)]
        wg = [_dot(a, b) for a, b in zip(lhs, rhs)]

        for idx, (c, p) in enumerate(group):
            e = d[idx]
            rows = slice(c * C, (c + 1) * C)
            lanes = slice(p * HP, (p + 1) * HP)
            rp = e["xr"] + wg[idx][:P2, :HP]
            y0 = wg[idx][:P2, HP:]
            gm = wg[idx][P2:, :HP]
            hm = wg[idx][P2:, HP:]
            st = state_ref[p]
            yg = _dot(jnp.concatenate([rp, gm], axis=0), st)
            pc_col = jnp.sum(jnp.where(eye_k, e["pc"], 0.0), axis=1, keepdims=True)
            state_ref[p] = pc_col * st + yg[P2:] + hm
            ys = yg[:P2] + y0
            mu = jnp.sum(ys, axis=-1, keepdims=True) * (1.0 / N)
            yc = jnp.where(stack_mask, ys - mu, 0.0)
            var = jnp.sum(yc * yc, axis=-1, keepdims=True) * (1.0 / N)
            yn = yc * lax.rsqrt(var + GN_EPS)
            bonus = jnp.sum(stack(bonus_w[rows, lanes]), axis=-1, keepdims=True) * stack(v[rows, lanes])
            yn = yn[:C] + yn[C:]
            bonus = bonus[:C] + bonus[C:]
            o_ref[rows, lanes] = ((yn * gnw_ref[:, lanes] + gnb_ref[:, lanes] + bonus)
                                  * gg_ref[rows, lanes]).astype(o_ref.dtype)


def _rwkv(zr, zk, zv, wp, ap, gg, mu_rkv, w0, a0, k_k, k_a, r_k, gn_w, gn_b, B, S):
    T = zr.shape[0]
    C = min(RWKV_ROWS, S)
    nc = S // C
    W = RWKV_WIDTH
    HP = 2 * RWKV_HEAD
    row = lambda b, i: (b * nc + i, 0)
    full = lambda b, i: (0, 0)

    def fs(a):
        return pl.BlockSpec(a.shape, full)

    return pl.pallas_call(
        _rwkv_kernel,
        grid=(B, nc),
        in_specs=[pl.BlockSpec((C, W), row)] * 6 + [fs(mu_rkv), fs(w0), fs(a0), fs(k_k), fs(k_a),
                                                    fs(r_k), fs(gn_w), fs(gn_b)],
        out_specs=pl.BlockSpec((C, W), row),
        out_shape=jax.ShapeDtypeStruct((T, W), BF16),
        scratch_shapes=[pltpu.VMEM((W // HP, HP, HP), F32),
                        pltpu.VMEM((V7X_SUBLANES, 3 * W), F32)],
        compiler_params=_cparams(("arbitrary", "arbitrary")),
        name="rwkv7",
    )(zr, zk, zv, wp, ap, gg, mu_rkv, w0, a0, k_k, k_a, r_k, gn_w, gn_b)


ROUTE_COLS = 8
ROUTER_GROUP_LANE = 0
ROUTER_EXPERT_LANE = N_GROUPS


def _out_router_kernel(x_ref, od_ref, orw_ref, wo_ref, mg_ref, wrh_ref, wrl_ref,
                       x1_out, hm_out, route_out, cnt_out, sl_out):
    tm = PROJ_ROWS if x_ref.shape[0] >= PROJ_ROWS else x_ref.shape[0]
    tiles = [slice(t * tm, (t + 1) * tm) for t in range(x_ref.shape[0] // tm)]
    x1 = [x_ref[r, :] + jnp.dot(od_ref[r, :], wo_ref[0:DIFF_WIDTH, :], preferred_element_type=F32)
          + jnp.dot(orw_ref[r, :], wo_ref[DIFF_WIDTH:, :], preferred_element_type=F32) for r in tiles]
    hm = [_rms(v, mg_ref[...], NORM_EPS) for v in x1]
    for r, v, h in zip(tiles, x1, hm):
        x1_out[r, :] = v
        hm_out[r, :] = h.astype(hm_out.dtype)

    def logits(h):
        hi = h.astype(BF16)
        lo = (h - hi.astype(F32)).astype(BF16)
        return (jnp.dot(hi, wrh_ref[...], preferred_element_type=F32)
                + jnp.dot(hi, wrl_ref[...], preferred_element_type=F32)
                + jnp.dot(lo, wrh_ref[...], preferred_element_type=F32))

    lgs = [logits(h) for h in hm]
    lane = lax.broadcasted_iota(jnp.int32, (tm, V7X_LANES), 1).astype(F32)
    big = float(V7X_LANES)
    ninf = -jnp.inf
    gmask = lane < N_GROUPS

    def choose(lg):
        gmax = jnp.max(jnp.where(gmask, lg, ninf), axis=-1, keepdims=True)
        g_sel = jnp.min(jnp.where(gmask & (lg == gmax), lane, big), axis=-1, keepdims=True)
        g_w = 1.0 / jnp.sum(jnp.where(gmask, jnp.exp(lg - gmax), 0.0), axis=-1, keepdims=True)
        lo_lane = ROUTER_EXPERT_LANE + g_sel * EXPERTS_PER_GROUP
        emask = (lane >= lo_lane) & (lane < lo_lane + EXPERTS_PER_GROUP)
        v1 = jnp.max(jnp.where(emask, lg, ninf), axis=-1, keepdims=True)
        i1 = jnp.min(jnp.where(emask & (lg == v1), lane, big), axis=-1, keepdims=True)
        emask2 = emask & (lane != i1)
        v2 = jnp.max(jnp.where(emask2, lg, ninf), axis=-1, keepdims=True)
        i2 = jnp.min(jnp.where(emask2 & (lg == v2), lane, big), axis=-1, keepdims=True)
        e2x = jnp.exp(v2 - v1)
        den = 1.0 + e2x
        return i1, i2, (1.0 / den) * g_w, (e2x / den) * g_w

    chosen = [choose(lg) for lg in lgs]

    ri = lax.broadcasted_iota(jnp.int32, (tm, tm), 0)
    ci = lax.broadcasted_iota(jnp.int32, (tm, tm), 1)
    before = (ci < ri).astype(BF16)
    li = lax.broadcasted_iota(jnp.int32, (V7X_LANES, V7X_LANES), 0)
    lj = lax.broadcasted_iota(jnp.int32, (V7X_LANES, V7X_LANES), 1)
    lanes_before = (li < lj).astype(BF16)
    ohs = [((lane == i1) | (lane == i2)).astype(F32) for i1, i2, _, _ in chosen]
    prefixes = [jnp.dot(before, oh.astype(BF16), preferred_element_type=F32) for oh in ohs]
    counts = [jnp.broadcast_to(jnp.sum(oh, axis=0, keepdims=True), (V7X_SUBLANES, V7X_LANES)) for oh in ohs]
    padded = [jnp.floor((c + (RUN_ALIGN - 1)) * (1.0 / RUN_ALIGN)) * RUN_ALIGN for c in counts]
    starts = [jnp.dot(pc.astype(BF16), lanes_before, preferred_element_type=F32) for pc in padded]
    col = lax.broadcasted_iota(jnp.int32, (tm, V7X_LANES), 1)
    for t, (r, (i1, i2, wt1, wt2), cnt, start, prefix) in enumerate(zip(tiles, chosen, counts, starts, prefixes)):
        row = prefix + start[0:1, :]
        sl1 = jnp.sum(jnp.where(lane == i1, row, 0.0), axis=-1, keepdims=True)
        sl2 = jnp.sum(jnp.where(lane == i2, row, 0.0), axis=-1, keepdims=True)
        cnt_out[t * V7X_SUBLANES:(t + 1) * V7X_SUBLANES, :] = cnt
        wide = jnp.where(col == 0, sl1, jnp.where(col == 1, sl2, jnp.where(col == 2, wt1, jnp.where(
            col == 3, wt2, 0.0))))
        route_out[r, :] = wide[:, 0:ROUTE_COLS]
        sl_out[t] = wide.T[0:V7X_SUBLANES, :].astype(sl_out.dtype)


def _out_router(x2, od, orw, w_out, moe_g, wr_hi, wr_lo):
    T, D = x2.shape
    tm = min(PROJ_ROWS, T)
    nt = T // tm
    tps = TILES_PER_STEP if nt % TILES_PER_STEP == 0 else 1
    tr = tm * tps
    row = lambda i: (i, 0)
    full = lambda i: (0, 0)

    def fs(a):
        return pl.BlockSpec(a.shape, full)

    return pl.pallas_call(
        _out_router_kernel,
        grid=(nt // tps,),
        in_specs=[pl.BlockSpec((tr, D), row), pl.BlockSpec((tr, DIFF_WIDTH), row),
                  pl.BlockSpec((tr, RWKV_WIDTH), row), fs(w_out), fs(moe_g), fs(wr_hi), fs(wr_lo)],
        out_specs=[pl.BlockSpec((tr, D), row), pl.BlockSpec((tr, D), row),
                   pl.BlockSpec((tr, ROUTE_COLS), row), pl.BlockSpec((tps * V7X_SUBLANES, V7X_LANES), row),
                   pl.BlockSpec((tps, V7X_SUBLANES, tm), lambda i: (i, 0, 0))],
        out_shape=[jax.ShapeDtypeStruct((T, D), F32), jax.ShapeDtypeStruct((T, D), BF16),
                   jax.ShapeDtypeStruct((T, ROUTE_COLS), F32),
                   jax.ShapeDtypeStruct((nt * V7X_SUBLANES, V7X_LANES), F32),
                   jax.ShapeDtypeStruct((nt, V7X_SUBLANES, tm), jnp.int32)],
        compiler_params=_cparams(("arbitrary",)),
        name="out_router",
    )(x2, od, orw, w_out, moe_g, wr_hi, wr_lo)


RUN_ALIGN = V7X_SUBLANES
RUN_CLASSES = 6
TAB_COUNT, TAB_NUSED = 0, RUN_CLASSES
TAB_SRC = V7X_SUBLANES
TAB_DST = TAB_SRC + RUN_CLASSES * N_EXPERTS
TAB_WORDS = 4 * V7X_LANES


def _stage_rows(tm):
    need = 2 * tm + N_EXPERTS * (RUN_ALIGN - 1)
    return -(-need // V7X_LANES) * V7X_LANES


def _run_table(src, dst, length, spare):
    i32 = jnp.int32
    n = src.shape[0]
    cls = jnp.arange(RUN_CLASSES, dtype=i32)[None, :, None]
    ln = length[:, None, :]
    has = ((ln >> 3) >> cls) & 1
    off = ln & ~((2 * RUN_ALIGN << cls) - 1)
    pos = jnp.cumsum(has, axis=-1) - 1
    k = jnp.arange(N_EXPERTS, dtype=i32)
    pick = (has[..., None] == 1) & (pos[..., None] == k)

    def compact(v):
        return jnp.sum(jnp.where(pick, (v[:, None, :] + off)[..., None], 0), axis=2).reshape(n, -1)

    head = jnp.concatenate([jnp.sum(has, axis=-1), jnp.broadcast_to(spare, (n, 1)).astype(i32),
                            jnp.zeros((n, TAB_SRC - RUN_CLASSES - 1), i32)], axis=1)
    tab = jnp.concatenate([head, compact(src), compact(dst)], axis=1)
    return jnp.pad(tab, ((0, 0), (0, TAB_WORDS - tab.shape[1]))).reshape(n, 1, TAB_WORDS)


def _run_copies(tab_ref, tile, make_copy, op):
    for c in range(RUN_CLASSES):
        size = RUN_ALIGN << c

        def one(k, carry, c=c, size=size):
            src = tab_ref[tile, 0, TAB_SRC + c * N_EXPERTS + k]
            dst = tab_ref[tile, 0, TAB_DST + c * N_EXPERTS + k]
            getattr(make_copy(pl.multiple_of(src, RUN_ALIGN), pl.multiple_of(dst, RUN_ALIGN), size), op)()
            return carry

        lax.fori_loop(0, tab_ref[tile, 0, TAB_COUNT + c], one, 0)


def _dispatch_kernel(tab_ref, tail_ref, sl_ref, hm_ref, xb_ref, stage_ref, zero_ref, sem, zsem):
    tps, ns = stage_ref.shape[0], stage_ref.shape[1]
    tm = hm_ref.shape[0] // tps
    bm = zero_ref.shape[0]
    nb = xb_ref.shape[0] // bm
    n_used = tail_ref[0, 0, TAB_NUSED]

    def tile_copy(t):
        def make_copy(src, dst, size):
            return pltpu.make_async_copy(stage_ref.at[t, pl.ds(src, size)], xb_ref.at[pl.ds(dst, size)], sem)
        return make_copy

    def zero_copy(src, dst, size):
        return pltpu.make_async_copy(zero_ref.at[pl.ds(src, size)], xb_ref.at[pl.ds(dst, size)], zsem)

    def zero_blocks(op):
        def one(b, c):
            getattr(zero_copy(0, pl.multiple_of(b * bm, bm), bm), op)()
            return c
        lax.fori_loop(n_used, nb, one, 0)

    @pl.when(pl.program_id(0) == 0)
    def _():
        zero_ref[...] = jnp.zeros_like(zero_ref)
        _run_copies(tail_ref, 0, zero_copy,"start")
        zero_blocks("start")

    srow = lax.broadcasted_iota(jnp.int32, (ns, tm), 0)
    for t in range(tps):
        sel = (srow == sl_ref[t, 0:1, :]) | (srow == sl_ref[t, 1:2, :])
        stage_ref[t] = _pack16(jnp.dot(sel.astype(BF16), hm_ref[t * tm:(t + 1) * tm, :],
                                       preferred_element_type=F32))
        _run_copies(tab_ref, t, tile_copy(t),"start")
    for t in range(tps):
        _run_copies(tab_ref, t, tile_copy(t),"wait")

    @pl.when(pl.program_id(0) == pl.num_programs(0) - 1)
    def _():
        _run_copies(tail_ref, 0, zero_copy,"wait")
        zero_blocks("wait")


def _dispatch(tab, tail, sl_rows, hm, P):
    T, D = hm.shape
    tm = min(PROJ_ROWS, T)
    nt = T // tm
    tps = TILES_PER_STEP if nt % TILES_PER_STEP == 0 else 1
    return pl.pallas_call(
        _dispatch_kernel,
        grid=(nt // tps,),
        in_specs=[pl.BlockSpec((tps, 1, tab.shape[-1]), lambda i: (i, 0, 0), memory_space=pltpu.SMEM),
                  pl.BlockSpec((1, 1, tail.shape[-1]), lambda i: (0, 0, 0), memory_space=pltpu.SMEM),
                  pl.BlockSpec((tps, V7X_SUBLANES, tm), lambda i: (i, 0, 0)),
                  pl.BlockSpec((tps * tm, D), lambda i: (i, 0))],
        out_specs=pl.BlockSpec(memory_space=pl.ANY),
        out_shape=jax.ShapeDtypeStruct((P, D // 2), PACKED),
        scratch_shapes=[pltpu.VMEM((tps, _stage_rows(tm), D // 2), PACKED),
                        pltpu.VMEM((MOE_ROWS, D // 2), PACKED),
                        pltpu.SemaphoreType.DMA(()), pltpu.SemaphoreType.DMA(())],
        compiler_params=_cparams(("arbitrary",)),
        name="moe_dispatch",
    )(tab, tail, sl_rows, hm)


def _expert_kernel(be_ref, nu_ref, first_ref, nxt_ref, slot_ref, xb_ref, wg_hbm, wu_hbm, wd_hbm, yb_ref,
                   wgf, wuf, wdf, wgb, wub, wdb, sem):
    i = pl.program_id(0)

    def weight_copies(e, s):
        return (pltpu.make_async_copy(wg_hbm.at[e], wgf.at[s], sem.at[s, 0]),
                pltpu.make_async_copy(wu_hbm.at[e], wuf.at[s], sem.at[s, 1]),
                pltpu.make_async_copy(wd_hbm.at[e], wdf.at[s], sem.at[s, 2]))

    @pl.when(i == 0)
    def _():
        for cp in weight_copies(be_ref[0], 0):
            cp.start()

    @pl.when(first_ref[i] == 1)
    def _():
        s = slot_ref[i]
        for cp in weight_copies(be_ref[i], s):
            cp.wait()
        wgb[...] = wgf[s].astype(BF16)
        wub[...] = wuf[s].astype(BF16)
        wdb[...] = wdf[s].astype(BF16)

        @pl.when(nxt_ref[i] >= 0)
        def _():
            for cp in weight_copies(nxt_ref[i], 1 - s):
                cp.start()

    @pl.when(i < nu_ref[0])
    def _():
        xb = _unpack16(xb_ref[...]).astype(BF16)
        gate = jnp.dot(xb, wgb[...], preferred_element_type=F32)
        up = jnp.dot(xb, wub[...], preferred_element_type=F32)
        hdn = (gate * jax.nn.sigmoid(gate)) * up
        y = jnp.dot(hdn.astype(BF16), wdb[...], preferred_element_type=F32)
        yb_ref[...] = _pack16(y.astype(BF16).astype(F32))

    @pl.when(i >= nu_ref[0])
    def _():
        yb_ref[...] = jnp.zeros_like(yb_ref)


def _experts(block_e, n_used, first, nxt, slot, xb, w_gate, w_up, w_down):
    P, DP = xb.shape
    D = 2 * DP
    bm = MOE_ROWS
    nb = P // bm
    E = D_EXPERT
    hbm = pl.BlockSpec(memory_space=pl.ANY)
    grid_spec = pltpu.PrefetchScalarGridSpec(
        num_scalar_prefetch=5,
        grid=(nb,),
        in_specs=[pl.BlockSpec((bm, DP), lambda i, be, nu, *_: (jnp.minimum(i, nu[0] - 1), 0)), hbm, hbm, hbm],
        out_specs=pl.BlockSpec((bm, DP), lambda i, be, nu, *_: (jnp.where(i < nu[0], i, nb - 1), 0)),
        scratch_shapes=[pltpu.VMEM((2, D, E), F32), pltpu.VMEM((2, D, E), F32), pltpu.VMEM((2, E, D), F32),
                        pltpu.VMEM((D, E), BF16), pltpu.VMEM((D, E), BF16), pltpu.VMEM((E, D), BF16),
                        pltpu.SemaphoreType.DMA((2, 3))],
    )
    return pl.pallas_call(
        _expert_kernel,
        grid_spec=grid_spec,
        out_shape=jax.ShapeDtypeStruct((P, DP), PACKED),
        input_output_aliases={5: 0},
        compiler_params=_cparams(("arbitrary",)),
        name="moe_experts",
    )(block_e, n_used, first, nxt, slot, xb, w_gate, w_up, w_down)


def _final_kernel(tab_ref, tabn_ref, x1_ref, route_ref, p_ref, wpl_ref, plg_ref, pgg_ref, wgate_ref, yb_ref,
                  o_ref, stage_ref, sem):
    tps, ns = stage_ref.shape[1], stage_ref.shape[2]
    tm = x1_ref.shape[0] // tps
    tiles = [slice(t * tm, (t + 1) * tm) for t in range(tps)]
    i = pl.program_id(0)
    cur = i % 2

    def gather(tab, slot, op):
        for t in range(tps):
            def make_copy(src, dst, size, t=t):
                return pltpu.make_async_copy(yb_ref.at[pl.ds(dst, size)],
                                             stage_ref.at[slot, t, pl.ds(src, size)], sem.at[slot])
            _run_copies(tab, t, make_copy, op)

    def fetch(tab, slot):
        stage_ref[slot] = jnp.zeros(stage_ref.shape[1:], stage_ref.dtype)
        gather(tab, slot, "start")

    @pl.when(i == 0)
    def _():
        fetch(tab_ref, 0)

    @pl.when(i + 1 < pl.num_programs(0))
    def _():
        fetch(tabn_ref, 1 - cur)

    pe = [_rms(jnp.dot(p_ref[r, :].astype(BF16), wpl_ref[...], preferred_element_type=F32),
               plg_ref[...], NORM_EPS) for r in tiles]
    gather(tab_ref, cur, "wait")

    scol = lax.broadcasted_iota(jnp.int32, (tm, ns), 1).astype(F32)

    def selection(r):
        route = route_ref[r, :]
        wsel = (jnp.where(scol == route[:, 0:1], route[:, 2:3], 0.0)
                + jnp.where(scol == route[:, 1:2], route[:, 3:4], 0.0))
        w_hi = wsel.astype(BF16)
        return w_hi, (wsel - w_hi.astype(F32)).astype(BF16)

    wsel = [selection(r) for r in tiles]
    yb16 = [_unpack16(stage_ref[cur, t]).astype(BF16) for t in range(tps)]
    moe = [jnp.dot(w_hi, y, preferred_element_type=F32) + jnp.dot(w_lo, y, preferred_element_type=F32)
           for (w_hi, w_lo), y in zip(wsel, yb16)]
    x2 = [x1_ref[r, :] + m for r, m in zip(tiles, moe)]
    gate = [jax.nn.sigmoid(_dot(_rms(v, pgg_ref[...], NORM_EPS), wgate_ref[...])) for v in x2]
    for r, v, e, g in zip(tiles, x2, pe, gate):
        o_ref[r, :] = v + e * g


def _final(tab, x1, route, p2, w_pl, pl_g, pl_gate_g, w_gate, yb):
    T, D = x1.shape
    tm = min(PROJ_ROWS, T)
    nt = T // tm
    tps = TILES_PER_STEP if nt % TILES_PER_STEP == 0 else 1
    tr = tps * tm
    row = lambda i: (i, 0)
    full = lambda i: (0, 0)

    def fs(a):
        return pl.BlockSpec(a.shape, full)

    return pl.pallas_call(
        _final_kernel,
        grid=(nt // tps,),
        in_specs=[pl.BlockSpec((tps, 1, tab.shape[-1]), lambda i: (i, 0, 0), memory_space=pltpu.SMEM),
                  pl.BlockSpec((tps, 1, tab.shape[-1]), lambda i: (jnp.minimum(i + 1, nt // tps - 1), 0, 0),
                               memory_space=pltpu.SMEM),
                  pl.BlockSpec((tr, D), row), pl.BlockSpec((tr, ROUTE_COLS), row),
                  pl.BlockSpec((tr, PL_DIM), row), fs(w_pl), fs(pl_g), fs(pl_gate_g), fs(w_gate),
                  pl.BlockSpec(memory_space=pl.ANY)],
        out_specs=pl.BlockSpec((tr, D), row),
        out_shape=jax.ShapeDtypeStruct((T, D), F32),
        scratch_shapes=[pltpu.VMEM((2, tps, _stage_rows(tm), D // 2), PACKED), pltpu.SemaphoreType.DMA((2,))],
        compiler_params=_cparams(("arbitrary",)),
        name="combine_final",
    )(tab, tab, x1, route, p2, w_pl, pl_g, pl_gate_g, w_gate, yb)


def kernel(x, p, positions, attn_norm_g, w_in, q_norm_g, k_norm_g, lambda_q1, lambda_k1, lambda_q2, lambda_k2, diff_out_g, mu_rkv, mu_wag, w0, w_lora_a, w_lora_b, a0, a_lora_a, a_lora_b, g_lora_a, g_lora_b, k_k, k_a, r_k, gn_w, gn_b, w_out, moe_norm_g, w_group, w_expert_router, w_gate, w_up, w_down, w_pl, pl_norm_g, pl_gate_norm_g, w_pl_gate):
    B, S, D = x.shape
    T = B * S
    assert p.shape[0] == 1, "one layer"
    x2 = x.reshape(T, D)
    p2 = p[0].reshape(T, PL_DIM)
    pos = positions.astype(F32).reshape(T, 1)

    def row(a):
        return a.reshape(1, -1).astype(F32)

    half = DIFF_HEAD_DIM // 2
    inv_freq = ROPE_THETA ** (-jnp.arange(half, dtype=F32) / half)
    invf = jnp.tile(inv_freq, V7X_LANES // half).reshape(1, V7X_LANES)
    reps = DIFF_WIDTH // DIFF_HEAD_DIM

    q, k, v, zr, zk, zv, wp, ap, gg = _in_proj(
        x2, pos, row(attn_norm_g[0]), w_in[0].astype(BF16),
        w_lora_a[0].astype(BF16), a_lora_a[0].astype(BF16), g_lora_a[0].astype(BF16),
        w_lora_b[0].astype(BF16), a_lora_b[0].astype(BF16), g_lora_b[0].astype(BF16),
        mu_wag[0].astype(F32), row(jnp.tile(q_norm_g[0], reps)), row(jnp.tile(k_norm_g[0], reps)),
        invf, B, S)

    o_diff = _attention(q, k, v, row(lambda_q1[0]), row(lambda_k1[0]), row(lambda_q2[0]),
                        row(lambda_k2[0]), diff_out_g[0].astype(F32).reshape(-1, 1), B, S)
    o_rwkv = _rwkv(zr, zk, zv, wp, ap, gg, mu_rkv[0].astype(F32), row(w0[0]), row(a0[0]),
                   row(k_k[0]), row(k_a[0]), row(r_k[0]), row(gn_w[0]), row(gn_b[0]), B, S)

    wr = jnp.concatenate([w_group[0], jnp.transpose(w_expert_router[0], (1, 0, 2)).reshape(D, N_EXPERTS)],
                         axis=1).astype(F32)
    wr = jnp.pad(wr, ((0, 0), (0, V7X_LANES - wr.shape[1])))
    wr_hi = wr.astype(BF16)
    wr_lo = (wr - wr_hi.astype(F32)).astype(BF16)
    x1, hm, route, cnt, sl_rows = _out_router(x2, o_diff, o_rwkv, w_out[0].astype(BF16), row(moe_norm_g[0]),
                                     wr_hi, wr_lo)

    bm = MOE_ROWS
    tm = min(PROJ_ROWS, T)
    nt = T // tm
    i32 = jnp.int32
    cnt_te = cnt.reshape(nt, V7X_SUBLANES, V7X_LANES)[:, 0, ROUTER_EXPERT_LANE:ROUTER_EXPERT_LANE + N_EXPERTS]
    len_te = (cnt_te.astype(i32) + RUN_ALIGN - 1) // RUN_ALIGN * RUN_ALIGN
    tot_e = jnp.sum(len_te, axis=0)
    pcounts = (tot_e + bm - 1) // bm * bm
    pends = jnp.cumsum(pcounts)
    pstarts = pends - pcounts
    dst_te = pstarts[None, :] + jnp.cumsum(len_te, axis=0) - len_te
    src_te = jnp.cumsum(len_te, axis=1) - len_te
    tab = _run_table(src_te, dst_te, len_te, 0)
    nb = -(-(2 * T + nt * N_EXPERTS * (RUN_ALIGN - 1)) // bm) + N_EXPERTS
    P = nb * bm
    n_used = (pends[-1] // bm).astype(i32).reshape(1)
    block_start = jnp.arange(nb, dtype=i32) * bm
    block_e = jnp.minimum(jnp.sum((pends[None, :] <= block_start[:, None]).astype(i32), axis=1),
                          N_EXPERTS - 1)
    tail = _run_table(jnp.zeros((1, N_EXPERTS), i32), (pstarts + tot_e)[None, :], (pcounts - tot_e)[None, :],
                      n_used)
    xb = _dispatch(tab, tail, sl_rows, hm, P)
    blk = jnp.arange(nb, dtype=i32)
    first = ((blk < n_used[0]) & ((blk == 0) | (block_e != jnp.roll(block_e, 1)))).astype(i32)
    slot = (jnp.cumsum(first) - 1) % 2
    eidx = jnp.arange(N_EXPERTS, dtype=i32)
    later = jnp.where((pcounts > 0)[None, :] & (eidx[None, :] > eidx[:, None]), eidx[None, :], N_EXPERTS)
    nxt_e = jnp.min(later, axis=1)
    nxt = jnp.where(nxt_e == N_EXPERTS, -1, nxt_e)[block_e]
    yb = _experts(block_e, n_used, first, nxt, slot.astype(i32), xb, w_gate[0], w_up[0], w_down[0])
    out = _final(tab, x1, route, p2, w_pl[0].astype(BF16), row(pl_norm_g[0]), row(pl_gate_norm_g[0]),
                 w_pl_gate[0].astype(BF16), yb)
    return out.reshape(B, S, D)
```

```python
import functools
import math

import jax
import jax.numpy as jnp
from jax import lax
from jax.experimental import pallas as pl
from jax.experimental.pallas import tpu as pltpu

F32 = jnp.float32
BF16 = jnp.bfloat16

D_MODEL = 1024
PL_DIM = 256
DIFF_WIDTH = 512
RWKV_WIDTH = 512
DIFF_HEAD_DIM = 64
DIFF_HEADS = 4
RWKV_HEAD = 64
RWKV_HEADS = 8
D_DECAY_LORA = 64
D_AAA_LORA = 64
D_GATE_LORA = 160
ROPE_THETA = 10000.0
NORM_EPS = 1e-6
SUBLN_EPS = 1e-5
GN_EPS = 64e-5
N_GROUPS = 4
EXPERTS_PER_GROUP = 8
N_EXPERTS = 32
D_EXPERT = 512
LAM_INIT = 0.8 - 0.6 * math.exp(0.0)

V7X_LANES = 128
V7X_SUBLANES = 8
V7X_VMEM_BYTES = 64 * 1024 * 1024

IN_PROJ_ROWS = 256
IN_PROJ_TILES_PER_STEP = 2
PROJ_ROWS = 256
TILES_PER_STEP = 4
ATTN_ROWS = 256
ATTN_UNROLL = 4
ATTN_HEADS_PER_STEP = 2
RWKV_CHUNK = 64
RWKV_GROUP = 16
RWKV_ROWS = 256
MOE_ROWS = 512
VMEM_LIMIT = 56 * 1024 * 1024


def _cparams(sem):
    return pltpu.CompilerParams(dimension_semantics=sem, vmem_limit_bytes=VMEM_LIMIT)


def _dot(a, b):
    return jnp.dot(a.astype(BF16), b.astype(BF16), preferred_element_type=F32)


def _dot_nt(a, b):
    return lax.dot_general(a.astype(BF16), b.astype(BF16), (((1,), (1,)), ((), ())),
                           preferred_element_type=F32)


def _dot_tn(a, b):
    return lax.dot_general(a.astype(BF16), b.astype(BF16), (((0,), (0,)), ((), ())),
                           preferred_element_type=F32)


def _split3(x):
    hi = x.astype(BF16)
    r1 = x - hi.astype(F32)
    mid = r1.astype(BF16)
    lo = (r1 - mid.astype(F32)).astype(BF16)
    return hi, mid, lo


def _dot_exact_lhs(m01, x):
    hi, mid, lo = _split3(x)
    return (jnp.dot(m01, hi, preferred_element_type=F32)
            + jnp.dot(m01, mid, preferred_element_type=F32)
            + jnp.dot(m01, lo, preferred_element_type=F32))


PACKED = jnp.uint32


def _pack16(x):
    h = x.shape[1] // 2
    lo = lax.bitcast_convert_type(x[:, :h], PACKED) >> 16
    hi = lax.bitcast_convert_type(x[:, h:], PACKED) & jnp.uint32(0xFFFF0000)
    return hi | lo


def _unpack16(w):
    lo = lax.bitcast_convert_type(w << 16, F32)
    hi = lax.bitcast_convert_type(w & jnp.uint32(0xFFFF0000), F32)
    return jnp.concatenate([lo, hi], axis=1)


def _sumsq_64(z):
    m, n = z.shape
    group = V7X_LANES // 2
    first = lax.broadcasted_iota(jnp.int32, (m, V7X_LANES), 1) < group
    out = []
    for c in range(0, n, V7X_LANES):
        sq = z[:, c:c + V7X_LANES] * z[:, c:c + V7X_LANES]
        s0 = jnp.sum(jnp.where(first, sq, 0.0), axis=-1, keepdims=True)
        s1 = jnp.sum(jnp.where(first, 0.0, sq), axis=-1, keepdims=True)
        out.append(jnp.where(first, s0, s1))
    return jnp.concatenate(out, axis=1)


def _rms(x, g, eps):
    return x * lax.rsqrt(jnp.mean(x * x, axis=-1, keepdims=True) + eps) * g


def _shift_rows(z, prev_row):
    rolled = pltpu.roll(z, 1, axis=0)
    row = lax.broadcasted_iota(jnp.int32, z.shape, 0)
    return jnp.where(row == 0, prev_row, rolled)


def _in_proj_kernel(x_ref, pos_ref, g_ref, win_ref, w1_ref, a1_ref, g1_ref, w2_ref, a2_ref, g2_ref,
                    mu_ref, qg_ref, kg_ref, invf_ref,
                    q_out, k_out, v_out, zr_out, zk_out, zv_out, wp_out, ap_out, gg_out,
                    carry_ref):
    tm = IN_PROJ_ROWS if x_ref.shape[0] >= IN_PROJ_ROWS else x_ref.shape[0]
    tiles = [slice(t * tm, (t + 1) * tm) for t in range(x_ref.shape[0] // tm)]
    w = DIFF_WIDTH
    half = DIFF_HEAD_DIM // 2

    @pl.when(pl.program_id(1) == 0)
    def _():
        carry_ref[...] = jnp.zeros_like(carry_ref)

    prev = carry_ref[V7X_SUBLANES - 1:V7X_SUBLANES, :]
    hns, dhs = [], []
    for r in tiles:
        hn = _rms(x_ref[r, :], g_ref[...], NORM_EPS)
        dhs.append(_shift_rows(hn, prev) - hn)
        hns.append(hn)
        prev = hn[tm - 1:tm, :]
    carry_ref[...] = hns[-1][tm - V7X_SUBLANES:tm, :]

    zq, zk, low = [], [], []
    for r, hn, dh in zip(tiles, hns, dhs):
        hb = hn.astype(BF16)

        def proj(c, hb=hb):
            return jnp.dot(hb, win_ref[:, c * w:(c + 1) * w], preferred_element_type=F32)

        zq.append(proj(0))
        zk.append(proj(1))
        low.append((_dot(hn + dh * mu_ref[0:1, :], w1_ref[...]), _dot(hn + dh * mu_ref[1:2, :], a1_ref[...]),
                    _dot(hn + dh * mu_ref[2:3, :], g1_ref[...])))
        v_out[r, :] = proj(2).astype(v_out.dtype)
        zr_out[r, :] = proj(3)
        zk_out[r, :] = proj(4)
        zv_out[r, :] = proj(5)

    lane = lax.broadcasted_iota(jnp.int32, (tm, w), 1)
    first_half = (lane % DIFF_HEAD_DIM) < half
    scale = DIFF_HEAD_DIM ** -0.5 * math.log2(math.e)
    for r, q, k, (lw, la, lg) in zip(tiles, zq, zk, low):
        ang = pos_ref[r, :] * invf_ref[...]
        cosf = jnp.concatenate([jnp.cos(ang)] * (w // V7X_LANES), axis=1)
        sinf = jnp.concatenate([jnp.sin(ang)] * (w // V7X_LANES), axis=1)
        sin_signed = jnp.where(first_half, -sinf, sinf)

        def norm_rope(z, gain, cosf=cosf, sin_signed=sin_signed):
            zn = z * lax.rsqrt(_sumsq_64(z) * (1.0 / DIFF_HEAD_DIM) + NORM_EPS) * gain
            partner = jnp.where(first_half, pltpu.roll(zn, w - half, axis=1), pltpu.roll(zn, half, axis=1))
            return zn * cosf + partner * sin_signed

        q_out[r, :] = (norm_rope(q, qg_ref[...]) * scale).astype(q_out.dtype)
        k_out[r, :] = norm_rope(k, kg_ref[...]).astype(k_out.dtype)
        wp_out[r, :] = _dot(jnp.tanh(lw), w2_ref[...])
        ap_out[r, :] = _dot(la, a2_ref[...])
        gg_out[r, :] = _dot(jax.nn.sigmoid(lg), g2_ref[...])


def _in_proj(x2, pos, attn_g, w_in, w1, a1, g1, w2, a2, g2, mu_wag, qg, kg, invf, B, S):
    T, D = x2.shape
    tile = min(IN_PROJ_ROWS, S)
    tps = IN_PROJ_TILES_PER_STEP if (S // tile) % IN_PROJ_TILES_PER_STEP == 0 else 1
    tm = tile * tps
    ns = S // tm
    w = DIFF_WIDTH
    row = lambda b, i: (b * ns + i, 0)
    full = lambda b, i: (0, 0)

    def fs(a):
        return pl.BlockSpec(a.shape, full)

    outs = ([jax.ShapeDtypeStruct((T, w), BF16)] * 3 + [jax.ShapeDtypeStruct((T, w), F32)] * 6)
    return pl.pallas_call(
        _in_proj_kernel,
        grid=(B, ns),
        in_specs=[pl.BlockSpec((tm, D), row), pl.BlockSpec((tm, 1), row), fs(attn_g), fs(w_in),
                  fs(w1), fs(a1), fs(g1), fs(w2), fs(a2), fs(g2), fs(mu_wag), fs(qg), fs(kg),
                  fs(invf)],
        out_specs=[pl.BlockSpec((tm, w), row)] * 9,
        out_shape=outs,
        scratch_shapes=[pltpu.VMEM((V7X_SUBLANES, D), F32)],
        compiler_params=_cparams(("arbitrary", "arbitrary")),
        name="in_proj",
    )(x2, pos, attn_g, w_in, w1, a1, g1, w2, a2, g2, mu_wag, qg, kg, invf)


def _attn_kernel(lq1_ref, lk1_ref, lq2_ref, lk2_ref, ogt_ref, q_ref, k_ref, v_ref, o_ref,
                 vt_ref, s_ref, acc_ref):
    nh, nq, hw, tq = vt_ref.shape
    tk = tq
    heads = range(nh)
    lam = (jnp.exp(jnp.sum(lq1_ref[...] * lk1_ref[...], axis=-1, keepdims=True))
           - jnp.exp(jnp.sum(lq2_ref[...] * lk2_ref[...], axis=-1, keepdims=True)) + LAM_INIT)

    for h in heads:
        for c in range(nq):
            vt_ref[h, c] = (v_ref[c * tk:(c + 1) * tk, h * hw:(h + 1) * hw]
                            .astype(F32).T.astype(vt_ref.dtype))

    lane = lax.broadcasted_iota(jnp.int32, (tq, hw), 1)

    def stacked_queries(t):
        out = []
        for h in heads:
            q = q_ref[t * tq:(t + 1) * tq, h * hw:(h + 1) * hw]
            zero = jnp.zeros_like(q)
            out.append(jnp.concatenate([jnp.where(lane < DIFF_HEAD_DIM, q, zero),
                                        jnp.where(lane >= DIFF_HEAD_DIM, q, zero)], axis=0))
        return out

    def scores(j, qs):
        start = pl.multiple_of(j * tk, tk)
        return [lax.dot_general(k_ref[pl.ds(start, tk), h * hw:(h + 1) * hw], qs[h],
                                (((1,), (1,)), ((), ())), preferred_element_type=F32) for h in heads]

    def colmax(m, st):
        return tuple(jnp.maximum(mh, jnp.max(s, axis=0, keepdims=True)) for mh, s in zip(m, st))

    krow = lax.broadcasted_iota(jnp.int32, (tk, 2 * tq), 0)
    qcol = lax.broadcasted_iota(jnp.int32, (tk, 2 * tq), 1)
    causal = krow <= jnp.where(qcol >= tq, qcol - tq, qcol)
    ninf = tuple(jnp.full((1, 2 * tq), -jnp.inf, F32) for _ in heads)
    zeros = tuple(jnp.zeros((1, 2 * tq), F32) for _ in heads)

    def diagonal(t, qs, m):
        st = [jnp.where(causal, s, -jnp.inf) for s in scores(t, qs)]
        for h in heads:
            s_ref[t % 2, h, t] = st[h]
        return colmax(m, st)

    m_cur = diagonal(0, stacked_queries(0), ninf)
    for t in range(nq):
        slot, nslot = t % 2, (t + 1) % 2
        has_next = t + 1 < nq
        qs_next = stacked_queries(t + 1) if has_next else None
        acc_ref[...] = jnp.zeros_like(acc_ref)

        def body(j, carry, slot=slot, nslot=nslot, has_next=has_next, qs_next=qs_next, m_cur=m_cur):
            l, m_next = carry
            pt = [jnp.exp2(s_ref[slot, h, j] - m_cur[h]) for h in heads]
            pv = [jnp.dot(vt_ref[h, j], pt[h].astype(BF16), preferred_element_type=F32) for h in heads]
            for h in heads:
                acc_ref[h] += pv[h]
            l = tuple(lh + jnp.sum(p, axis=0, keepdims=True) for lh, p in zip(l, pt))
            if has_next:
                st = scores(j, qs_next)
                for h in heads:
                    s_ref[nslot, h, j] = st[h]
                m_next = colmax(m_next, st)
            return l, m_next

        lsum, m_next = lax.fori_loop(0, t + 1, body, (zeros, ninf), unroll=ATTN_UNROLL)
        if has_next:
            m_next = diagonal(t + 1, qs_next, m_next)

        for h in heads:
            acc = acc_ref[h]
            l = lsum[h]
            ot = acc[:, :tq] / l[:, :tq] - lam * (acc[:, tq:] / l[:, tq:])
            ot = ot * lax.rsqrt(jnp.mean(ot * ot, axis=0, keepdims=True) + SUBLN_EPS) * ogt_ref[...]
            o_ref[t * tq:(t + 1) * tq, h * hw:(h + 1) * hw] = (ot * (1.0 - LAM_INIT)).T.astype(o_ref.dtype)
        m_cur = m_next


def _attention(q, k, v, lq1, lk1, lq2, lk2, og, B, S):
    T = q.shape[0]
    tq = min(ATTN_ROWS, S)
    nq = S // tq
    hw = 2 * DIFF_HEAD_DIM
    nh = ATTN_HEADS_PER_STEP
    gw = nh * hw
    small = lambda b, h: (0, 0)
    seq = pl.BlockSpec((S, gw), lambda b, h: (b, h))
    return pl.pallas_call(
        _attn_kernel,
        grid=(B, DIFF_HEADS // nh),
        in_specs=[pl.BlockSpec(lq1.shape, small), pl.BlockSpec(lk1.shape, small),
                  pl.BlockSpec(lq2.shape, small), pl.BlockSpec(lk2.shape, small),
                  pl.BlockSpec(og.shape, small), seq, seq, seq],
        out_specs=seq,
        out_shape=jax.ShapeDtypeStruct((T, DIFF_WIDTH), BF16),
        scratch_shapes=[pltpu.VMEM((nh, nq, hw, tq), BF16), pltpu.VMEM((2, nh, nq, tq, 2 * tq), F32),
                        pltpu.VMEM((nh, hw, 2 * tq), F32)],
        compiler_params=_cparams(("arbitrary", "arbitrary")),
        name="diff_attn",
    )(lq1, lk1, lq2, lk2, og, q, k, v)


def _rwkv_kernel(zr_ref, zk_ref, zv_ref, wp_ref, ap_ref, gg_ref, mu_ref, w0_ref, a0_ref, kk_ref,
                 ka_ref, rk_ref, gnw_ref, gnb_ref, o_ref, state_ref, carry_ref):
    R = zr_ref.shape[0]
    C = min(RWKV_CHUNK, R)
    nch = R // C
    N = RWKV_HEAD
    W = RWKV_WIDTH
    HP = 2 * N
    n_pairs = W // HP

    @pl.when(pl.program_id(1) == 0)
    def _():
        state_ref[...] = jnp.zeros_like(state_ref)
        carry_ref[...] = jnp.zeros_like(carry_ref)

    zr, zk, zv = zr_ref[...], zk_ref[...], zv_ref[...]
    last = V7X_SUBLANES - 1
    r = zr + (_shift_rows(zr, carry_ref[last:last + 1, 0:W]) - zr) * mu_ref[0:1, :]
    k = zk + (_shift_rows(zk, carry_ref[last:last + 1, W:2 * W]) - zk) * mu_ref[1:2, :]
    v = zv + (_shift_rows(zv, carry_ref[last:last + 1, 2 * W:3 * W]) - zv) * mu_ref[2:3, :]
    carry_ref[:, 0:W] = zr[R - V7X_SUBLANES:R, :]
    carry_ref[:, W:2 * W] = zk[R - V7X_SUBLANES:R, :]
    carry_ref[:, 2 * W:3 * W] = zv[R - V7X_SUBLANES:R, :]

    lw = -math.exp(-0.5) * jax.nn.sigmoid(w0_ref[...] + wp_ref[...])
    a = jax.nn.sigmoid(a0_ref[...] + ap_ref[...])
    kk = k * kk_ref[...]
    kk = kk * lax.rsqrt(jnp.maximum(_sumsq_64(kk), 1e-24))
    k2 = k * (1.0 + (a - 1.0) * ka_ref[...])
    a_s = -kk
    b_s = kk * a

    rr = lax.broadcasted_iota(jnp.int32, (R, R), 0)
    cc = lax.broadcasted_iota(jnp.int32, (R, R), 1)
    same_chunk = (rr // C) == (cc // C)
    L = _dot_exact_lhs((same_chunk & (cc <= rr)).astype(BF16), lw)
    bonus_w = r * k2 * rk_ref[...]

    P2 = 2 * C
    sr = lax.broadcasted_iota(jnp.int32, (P2, HP), 0)
    sc = lax.broadcasted_iota(jnp.int32, (P2, HP), 1)
    stack_mask = (sr < C) == (sc < N)
    br = lax.broadcasted_iota(jnp.int32, (P2, P2), 0)
    bc = lax.broadcasted_iota(jnp.int32, (P2, P2), 1)
    same_head = (br < C) == (bc < C)
    tr = jnp.where(br >= C, br - C, br)
    tc = jnp.where(bc >= C, bc - C, bc)
    strict = same_head & (tc < tr)
    lower = same_head & (tc <= tr)
    eye_p = (br == bc).astype(F32)
    kr = lax.broadcasted_iota(jnp.int32, (HP, HP), 0)
    kc = lax.broadcasted_iota(jnp.int32, (HP, HP), 1)
    eye_k = kr == kc

    def dup(x):
        return jnp.concatenate([x, x], axis=0)

    def stack(x):
        return jnp.where(stack_mask, dup(x), 0.0)

    n_sq = int(math.log2(C)) - 1
    items = [(c, p) for c in range(nch) for p in range(n_pairs)]

    def prep(c, p):
        rows = slice(c * C, (c + 1) * C)
        lanes = slice(p * HP, (p + 1) * HP)
        Lc = L[rows, lanes]
        lwc = lw[rows, lanes]
        LC = Lc[C - 1:C, :]
        enL = jnp.exp(-Lc)
        eCL = jnp.exp(LC - Lc)
        b_c, k_c = b_s[rows, lanes], k2[rows, lanes]
        return dict(
            xa=stack(a_s[rows, lanes] * jnp.exp(Lc - lwc)).astype(BF16), xr=stack(r[rows, lanes] * jnp.exp(Lc)),
            bt=dup(b_c * enL).astype(BF16), kt=dup(k_c * enL).astype(BF16),
            bh=stack(b_c * eCL).astype(BF16), kh=stack(k_c * eCL).astype(BF16),
            vs=stack(v[rows, lanes]).astype(BF16), pc=jnp.exp(LC))

    for g0 in range(0, len(items), RWKV_GROUP):
        group = items[g0:g0 + RWKV_GROUP]
        d = [prep(c, p) for c, p in group]
        aa = [_dot_nt(jnp.concatenate([e["xa"], e["xr"].astype(BF16)], axis=0),
                      jnp.concatenate([e["bt"], e["kt"]], axis=0)) for e in d]
        a_ab = [jnp.where(strict, m[:P2, :P2], 0.0) for m in aa]
        a_ak = [jnp.where(strict, m[:P2, P2:], 0.0).astype(BF16) for m in aa]
        a_rb = [jnp.where(lower, m[P2:, :P2], 0.0).astype(BF16) for m in aa]
        a_rk = [jnp.where(lower, m[P2:, P2:], 0.0).astype(BF16) for m in aa]
        tm = [eye_p + m for m in a_ab]
        npw = a_ab
        for _ in range(n_sq):
            npw = [_dot(m, m) for m in npw]
            tm = [t + _dot(t, m) for t, m in zip(tm, npw)]
        av = [_dot(m, e["vs"]) for m, e in zip(a_ak, d)]
        z = [_dot(t, jnp.concatenate([e["xa"], x.astype(BF16)], axis=1))
             for t, e, x in zip(tm, d, av)]
        zero_v = jnp.zeros((P2, HP), BF16)
        rhs = [jnp.concatenate([x.astype(BF16), jnp.concatenate([zero_v, e["vs"]], axis=1)], axis=0)
               for e, x in zip(d, z)]
        lhs = [jnp.concatenate([jnp.concatenate([m1, m2], axis=1),
                                jnp.concatenate([e["bh"], e["kh"]], axis=0).astype(F32).T.astype(BF16)], axis=0)
               for m1, m2, e in zip(a_rb, a_rk, d)]
        wg = [_dot(a, b) for a, b in zip(lhs, rhs)]

        for idx, (c, p) in enumerate(group):
            e = d[idx]
            rows = slice(c * C, (c + 1) * C)
            lanes = slice(p * HP, (p + 1) * HP)
            rp = e["xr"] + wg[idx][:P2, :HP]
            y0 = wg[idx][:P2, HP:]
            gm = wg[idx][P2:, :HP]
            hm = wg[idx][P2:, HP:]
            st = state_ref[p]
            yg = _dot(jnp.concatenate([rp, gm], axis=0), st)
            pc_col = jnp.sum(jnp.where(eye_k, e["pc"], 0.0), axis=1, keepdims=True)
            state_ref[p] = pc_col * st + yg[P2:] + hm
            ys = yg[:P2] + y0
            mu = jnp.sum(ys, axis=-1, keepdims=True) * (1.0 / N)
            yc = jnp.where(stack_mask, ys - mu, 0.0)
            var = jnp.sum(yc * yc, axis=-1, keepdims=True) * (1.0 / N)
            yn = yc * lax.rsqrt(var + GN_EPS)
            bonus = jnp.sum(stack(bonus_w[rows, lanes]), axis=-1, keepdims=True) * stack(v[rows, lanes])
            yn = yn[:C] + yn[C:]
            bonus = bonus[:C] + bonus[C:]
            o_ref[rows, lanes] = ((yn * gnw_ref[:, lanes] + gnb_ref[:, lanes] + bonus)
                                  * gg_ref[rows, lanes]).astype(o_ref.dtype)


def _rwkv(zr, zk, zv, wp, ap, gg, mu_rkv, w0, a0, k_k, k_a, r_k, gn_w, gn_b, B, S):
    T = zr.shape[0]
    C = min(RWKV_ROWS, S)
    nc = S // C
    W = RWKV_WIDTH
    HP = 2 * RWKV_HEAD
    row = lambda b, i: (b * nc + i, 0)
    full = lambda b, i: (0, 0)

    def fs(a):
        return pl.BlockSpec(a.shape, full)

    return pl.pallas_call(
        _rwkv_kernel,
        grid=(B, nc),
        in_specs=[pl.BlockSpec((C, W), row)] * 6 + [fs(mu_rkv), fs(w0), fs(a0), fs(k_k), fs(k_a),
                                                    fs(r_k), fs(gn_w), fs(gn_b)],
        out_specs=pl.BlockSpec((C, W), row),
        out_shape=jax.ShapeDtypeStruct((T, W), BF16),
        scratch_shapes=[pltpu.VMEM((W // HP, HP, HP), F32),
                        pltpu.VMEM((V7X_SUBLANES, 3 * W), F32)],
        compiler_params=_cparams(("arbitrary", "arbitrary")),
        name="rwkv7",
    )(zr, zk, zv, wp, ap, gg, mu_rkv, w0, a0, k_k, k_a, r_k, gn_w, gn_b)


ROUTE_COLS = 8
ROUTER_GROUP_LANE = 0
ROUTER_EXPERT_LANE = N_GROUPS


def _out_router_kernel(x_ref, od_ref, orw_ref, wo_ref, mg_ref, wrh_ref, wrl_ref,
                       x1_out, hm_out, route_out, cnt_out, sl_out):
    tm = PROJ_ROWS if x_ref.shape[0] >= PROJ_ROWS else x_ref.shape[0]
    tiles = [slice(t * tm, (t + 1) * tm) for t in range(x_ref.shape[0] // tm)]
    x1 = [x_ref[r, :] + jnp.dot(od_ref[r, :], wo_ref[0:DIFF_WIDTH, :], preferred_element_type=F32)
          + jnp.dot(orw_ref[r, :], wo_ref[DIFF_WIDTH:, :], preferred_element_type=F32) for r in tiles]
    hm = [_rms(v, mg_ref[...], NORM_EPS) for v in x1]
    for r, v, h in zip(tiles, x1, hm):
        x1_out[r, :] = v
        hm_out[r, :] = h.astype(hm_out.dtype)

    def logits(h):
        hi = h.astype(BF16)
        lo = (h - hi.astype(F32)).astype(BF16)
        return (jnp.dot(hi, wrh_ref[...], preferred_element_type=F32)
                + jnp.dot(hi, wrl_ref[...], preferred_element_type=F32)
                + jnp.dot(lo, wrh_ref[...], preferred_element_type=F32))

    lgs = [logits(h) for h in hm]
    lane = lax.broadcasted_iota(jnp.int32, (tm, V7X_LANES), 1).astype(F32)
    big = float(V7X_LANES)
    ninf = -jnp.inf
    gmask = lane < N_GROUPS

    def choose(lg):
        gmax = jnp.max(jnp.where(gmask, lg, ninf), axis=-1, keepdims=True)
        g_sel = jnp.min(jnp.where(gmask & (lg == gmax), lane, big), axis=-1, keepdims=True)
        g_w = 1.0 / jnp.sum(jnp.where(gmask, jnp.exp(lg - gmax), 0.0), axis=-1, keepdims=True)
        lo_lane = ROUTER_EXPERT_LANE + g_sel * EXPERTS_PER_GROUP
        emask = (lane >= lo_lane) & (lane < lo_lane + EXPERTS_PER_GROUP)
        v1 = jnp.max(jnp.where(emask, lg, ninf), axis=-1, keepdims=True)
        i1 = jnp.min(jnp.where(emask & (lg == v1), lane, big), axis=-1, keepdims=True)
        emask2 = emask & (lane != i1)
        v2 = jnp.max(jnp.where(emask2, lg, ninf), axis=-1, keepdims=True)
        i2 = jnp.min(jnp.where(emask2 & (lg == v2), lane, big), axis=-1, keepdims=True)
        e2x = jnp.exp(v2 - v1)
        den = 1.0 + e2x
        return i1, i2, (1.0 / den) * g_w, (e2x / den) * g_w

    chosen = [choose(lg) for lg in lgs]

    ri = lax.broadcasted_iota(jnp.int32, (tm, tm), 0)
    ci = lax.broadcasted_iota(jnp.int32, (tm, tm), 1)
    before = (ci < ri).astype(BF16)
    li = lax.broadcasted_iota(jnp.int32, (V7X_LANES, V7X_LANES), 0)
    lj = lax.broadcasted_iota(jnp.int32, (V7X_LANES, V7X_LANES), 1)
    lanes_before = (li < lj).astype(BF16)
    ohs = [((lane == i1) | (lane == i2)).astype(F32) for i1, i2, _, _ in chosen]
    prefixes = [jnp.dot(before, oh.astype(BF16), preferred_element_type=F32) for oh in ohs]
    counts = [jnp.broadcast_to(jnp.sum(oh, axis=0, keepdims=True), (V7X_SUBLANES, V7X_LANES)) for oh in ohs]
    padded = [jnp.floor((c + (RUN_ALIGN - 1)) * (1.0 / RUN_ALIGN)) * RUN_ALIGN for c in counts]
    starts = [jnp.dot(pc.astype(BF16), lanes_before, preferred_element_type=F32) for pc in padded]
    col = lax.broadcasted_iota(jnp.int32, (tm, V7X_LANES), 1)
    for t, (r, (i1, i2, wt1, wt2), cnt, start, prefix) in enumerate(zip(tiles, chosen, counts, starts, prefixes)):
        row = prefix + start[0:1, :]
        sl1 = jnp.sum(jnp.where(lane == i1, row, 0.0), axis=-1, keepdims=True)
        sl2 = jnp.sum(jnp.where(lane == i2, row, 0.0), axis=-1, keepdims=True)
        cnt_out[t * V7X_SUBLANES:(t + 1) * V7X_SUBLANES, :] = cnt
        wide = jnp.where(col == 0, sl1, jnp.where(col == 1, sl2, jnp.where(col == 2, wt1, jnp.where(
            col == 3, wt2, 0.0))))
        route_out[r, :] = wide[:, 0:ROUTE_COLS]
        sl_out[t] = wide.T[0:V7X_SUBLANES, :].astype(sl_out.dtype)


def _out_router(x2, od, orw, w_out, moe_g, wr_hi, wr_lo):
    T, D = x2.shape
    tm = min(PROJ_ROWS, T)
    nt = T // tm
    tps = TILES_PER_STEP if nt % TILES_PER_STEP == 0 else 1
    tr = tm * tps
    row = lambda i: (i, 0)
    full = lambda i: (0, 0)

    def fs(a):
        return pl.BlockSpec(a.shape, full)

    return pl.pallas_call(
        _out_router_kernel,
        grid=(nt // tps,),
        in_specs=[pl.BlockSpec((tr, D), row), pl.BlockSpec((tr, DIFF_WIDTH), row),
                  pl.BlockSpec((tr, RWKV_WIDTH), row), fs(w_out), fs(moe_g), fs(wr_hi), fs(wr_lo)],
        out_specs=[pl.BlockSpec((tr, D), row), pl.BlockSpec((tr, D), row),
                   pl.BlockSpec((tr, ROUTE_COLS), row), pl.BlockSpec((tps * V7X_SUBLANES, V7X_LANES), row),
                   pl.BlockSpec((tps, V7X_SUBLANES, tm), lambda i: (i, 0, 0))],
        out_shape=[jax.ShapeDtypeStruct((T, D), F32), jax.ShapeDtypeStruct((T, D), BF16),
                   jax.ShapeDtypeStruct((T, ROUTE_COLS), F32),
                   jax.ShapeDtypeStruct((nt * V7X_SUBLANES, V7X_LANES), F32),
                   jax.ShapeDtypeStruct((nt, V7X_SUBLANES, tm), jnp.int32)],
        compiler_params=_cparams(("arbitrary",)),
        name="out_router",
    )(x2, od, orw, w_out, moe_g, wr_hi, wr_lo)


RUN_ALIGN = V7X_SUBLANES
RUN_CLASSES = 6
TAB_COUNT, TAB_NUSED = 0, RUN_CLASSES
TAB_SRC = V7X_SUBLANES
TAB_DST = TAB_SRC + RUN_CLASSES * N_EXPERTS
TAB_WORDS = 4 * V7X_LANES


def _stage_rows(tm):
    need = 2 * tm + N_EXPERTS * (RUN_ALIGN - 1)
    return -(-need // V7X_LANES) * V7X_LANES


def _run_table(src, dst, length, spare):
    i32 = jnp.int32
    n = src.shape[0]
    cls = jnp.arange(RUN_CLASSES, dtype=i32)[None, :, None]
    ln = length[:, None, :]
    has = ((ln >> 3) >> cls) & 1
    off = ln & ~((2 * RUN_ALIGN << cls) - 1)
    pos = jnp.cumsum(has, axis=-1) - 1
    k = jnp.arange(N_EXPERTS, dtype=i32)
    pick = (has[..., None] == 1) & (pos[..., None] == k)

    def compact(v):
        return jnp.sum(jnp.where(pick, (v[:, None, :] + off)[..., None], 0), axis=2).reshape(n, -1)

    head = jnp.concatenate([jnp.sum(has, axis=-1), jnp.broadcast_to(spare, (n, 1)).astype(i32),
                            jnp.zeros((n, TAB_SRC - RUN_CLASSES - 1), i32)], axis=1)
    tab = jnp.concatenate([head, compact(src), compact(dst)], axis=1)
    return jnp.pad(tab, ((0, 0), (0, TAB_WORDS - tab.shape[1]))).reshape(n, 1, TAB_WORDS)


def _run_copies(tab_ref, tile, make_copy, op):
    for c in range(RUN_CLASSES):
        size = RUN_ALIGN << c

        def one(k, carry, c=c, size=size):
            src = tab_ref[tile, 0, TAB_SRC + c * N_EXPERTS + k]
            dst = tab_ref[tile, 0, TAB_DST + c * N_EXPERTS + k]
            getattr(make_copy(pl.multiple_of(src, RUN_ALIGN), pl.multiple_of(dst, RUN_ALIGN), size), op)()
            return carry

        lax.fori_loop(0, tab_ref[tile, 0, TAB_COUNT + c], one, 0)


def _dispatch_kernel(tab_ref, tabp_ref, tail_ref, sl_ref, hm_ref, xb_ref, stage_ref, zero_ref, sem, zsem):
    tps, ns = stage_ref.shape[1], stage_ref.shape[2]
    tm = hm_ref.shape[0] // tps
    bm = zero_ref.shape[0]
    nb = xb_ref.shape[0] // bm
    n_used = tail_ref[0, 0, TAB_NUSED]
    i = pl.program_id(0)
    cur = i % 2

    def tile_copy(slot, t):
        def make_copy(src, dst, size):
            return pltpu.make_async_copy(stage_ref.at[slot, t, pl.ds(src, size)], xb_ref.at[pl.ds(dst, size)],
                                         sem.at[slot])
        return make_copy

    def zero_copy(src, dst, size):
        return pltpu.make_async_copy(zero_ref.at[pl.ds(src, size)], xb_ref.at[pl.ds(dst, size)], zsem)

    def zero_blocks(op):
        def one(b, c):
            getattr(zero_copy(0, pl.multiple_of(b * bm, bm), bm), op)()
            return c
        lax.fori_loop(n_used, nb, one, 0)

    @pl.when(i == 0)
    def _():
        zero_ref[...] = jnp.zeros_like(zero_ref)
        _run_copies(tail_ref, 0, zero_copy, "start")
        zero_blocks("start")

    srow = lax.broadcasted_iota(jnp.int32, (ns, tm), 0)
    for t in range(tps):
        sel = (srow == sl_ref[t, 0:1, :]) | (srow == sl_ref[t, 1:2, :])
        stage_ref[cur, t] = _pack16(jnp.dot(sel.astype(BF16), hm_ref[t * tm:(t + 1) * tm, :],
                                            preferred_element_type=F32))
        _run_copies(tab_ref, t, tile_copy(cur, t), "start")

    @pl.when(i > 0)
    def _():
        for t in range(tps):
            _run_copies(tabp_ref, t, tile_copy(1 - cur, t), "wait")

    @pl.when(i == pl.num_programs(0) - 1)
    def _():
        for t in range(tps):
            _run_copies(tab_ref, t, tile_copy(cur, t), "wait")
        _run_copies(tail_ref, 0, zero_copy, "wait")
        zero_blocks("wait")


def _dispatch(tab, tail, sl_rows, hm, P):
    T, D = hm.shape
    tm = min(PROJ_ROWS, T)
    nt = T // tm
    tps = TILES_PER_STEP if nt % TILES_PER_STEP == 0 else 1
    return pl.pallas_call(
        _dispatch_kernel,
        grid=(nt // tps,),
        in_specs=[pl.BlockSpec((tps, 1, tab.shape[-1]), lambda i: (i, 0, 0), memory_space=pltpu.SMEM),
                  pl.BlockSpec((tps, 1, tab.shape[-1]), lambda i: (jnp.maximum(i - 1, 0), 0, 0),
                               memory_space=pltpu.SMEM),
                  pl.BlockSpec((1, 1, tail.shape[-1]), lambda i: (0, 0, 0), memory_space=pltpu.SMEM),
                  pl.BlockSpec((tps, V7X_SUBLANES, tm), lambda i: (i, 0, 0)),
                  pl.BlockSpec((tps * tm, D), lambda i: (i, 0))],
        out_specs=pl.BlockSpec(memory_space=pl.ANY),
        out_shape=jax.ShapeDtypeStruct((P, D // 2), PACKED),
        scratch_shapes=[pltpu.VMEM((2, tps, _stage_rows(tm), D // 2), PACKED),
                        pltpu.VMEM((MOE_ROWS, D // 2), PACKED),
                        pltpu.SemaphoreType.DMA((2,)), pltpu.SemaphoreType.DMA(())],
        compiler_params=_cparams(("arbitrary",)),
        name="moe_dispatch",
    )(tab, tab, tail, sl_rows, hm)


def _expert_kernel(be_ref, nu_ref, first_ref, nxt_ref, slot_ref, xb_ref, wg_hbm, wu_hbm, wd_hbm, yb_ref,
                   wgf, wuf, wdf, wgb, wub, wdb, sem):
    i = pl.program_id(0)

    def weight_copies(e, s):
        return (pltpu.make_async_copy(wg_hbm.at[e], wgf.at[s], sem.at[s, 0]),
                pltpu.make_async_copy(wu_hbm.at[e], wuf.at[s], sem.at[s, 1]),
                pltpu.make_async_copy(wd_hbm.at[e], wdf.at[s], sem.at[s, 2]))

    @pl.when(i == 0)
    def _():
        for cp in weight_copies(be_ref[0], 0):
            cp.start()

    @pl.when(first_ref[i] == 1)
    def _():
        s = slot_ref[i]
        for cp in weight_copies(be_ref[i], s):
            cp.wait()
        wgb[...] = wgf[s].astype(BF16)
        wub[...] = wuf[s].astype(BF16)
        wdb[...] = wdf[s].astype(BF16)

        @pl.when(nxt_ref[i] >= 0)
        def _():
            for cp in weight_copies(nxt_ref[i], 1 - s):
                cp.start()

    @pl.when(i < nu_ref[0])
    def _():
        xb = _unpack16(xb_ref[...]).astype(BF16)
        gate = jnp.dot(xb, wgb[...], preferred_element_type=F32)
        up = jnp.dot(xb, wub[...], preferred_element_type=F32)
        hdn = (gate * jax.nn.sigmoid(gate)) * up
        y = jnp.dot(hdn.astype(BF16), wdb[...], preferred_element_type=F32)
        yb_ref[...] = _pack16(y.astype(BF16).astype(F32))

    @pl.when(i >= nu_ref[0])
    def _():
        yb_ref[...] = jnp.zeros_like(yb_ref)


def _experts(block_e, n_used, first, nxt, slot, xb, w_gate, w_up, w_down):
    P, DP = xb.shape
    D = 2 * DP
    bm = MOE_ROWS
    nb = P // bm
    E = D_EXPERT
    hbm = pl.BlockSpec(memory_space=pl.ANY)
    grid_spec = pltpu.PrefetchScalarGridSpec(
        num_scalar_prefetch=5,
        grid=(nb,),
        in_specs=[pl.BlockSpec((bm, DP), lambda i, be, nu, *_: (jnp.minimum(i, nu[0] - 1), 0)), hbm, hbm, hbm],
        out_specs=pl.BlockSpec((bm, DP), lambda i, be, nu, *_: (jnp.where(i < nu[0], i, nb - 1), 0)),
        scratch_shapes=[pltpu.VMEM((2, D, E), F32), pltpu.VMEM((2, D, E), F32), pltpu.VMEM((2, E, D), F32),
                        pltpu.VMEM((D, E), BF16), pltpu.VMEM((D, E), BF16), pltpu.VMEM((E, D), BF16),
                        pltpu.SemaphoreType.DMA((2, 3))],
    )
    return pl.pallas_call(
        _expert_kernel,
        grid_spec=grid_spec,
        out_shape=jax.ShapeDtypeStruct((P, DP), PACKED),
        input_output_aliases={5: 0},
        compiler_params=_cparams(("arbitrary",)),
        name="moe_experts",
    )(block_e, n_used, first, nxt, slot, xb, w_gate, w_up, w_down)


def _final_kernel(tab_ref, tabn_ref, x1_ref, route_ref, p_ref, wpl_ref, plg_ref, pgg_ref, wgate_ref, yb_ref,
                  o_ref, stage_ref, sem):
    tps, ns = stage_ref.shape[1], stage_ref.shape[2]
    tm = x1_ref.shape[0] // tps
    tiles = [slice(t * tm, (t + 1) * tm) for t in range(tps)]
    i = pl.program_id(0)
    cur = i % 2

    def gather(tab, slot, op):
        for t in range(tps):
            def make_copy(src, dst, size, t=t):
                return pltpu.make_async_copy(yb_ref.at[pl.ds(dst, size)],
                                             stage_ref.at[slot, t, pl.ds(src, size)], sem.at[slot])
            _run_copies(tab, t, make_copy, op)

    def fetch(tab, slot):
        stage_ref[slot] = jnp.zeros(stage_ref.shape[1:], stage_ref.dtype)
        gather(tab, slot, "start")

    @pl.when(i == 0)
    def _():
        fetch(tab_ref, 0)

    @pl.when(i + 1 < pl.num_programs(0))
    def _():
        fetch(tabn_ref, 1 - cur)

    pe = [_rms(jnp.dot(p_ref[r, :].astype(BF16), wpl_ref[...], preferred_element_type=F32),
               plg_ref[...], NORM_EPS) for r in tiles]
    gather(tab_ref, cur, "wait")

    scol = lax.broadcasted_iota(jnp.int32, (tm, ns), 1).astype(F32)

    def selection(r):
        route = route_ref[r, :]
        wsel = (jnp.where(scol == route[:, 0:1], route[:, 2:3], 0.0)
                + jnp.where(scol == route[:, 1:2], route[:, 3:4], 0.0))
        w_hi = wsel.astype(BF16)
        return w_hi, (wsel - w_hi.astype(F32)).astype(BF16)

    wsel = [selection(r) for r in tiles]
    yb16 = [_unpack16(stage_ref[cur, t]).astype(BF16) for t in range(tps)]
    moe = [jnp.dot(w_hi, y, preferred_element_type=F32) + jnp.dot(w_lo, y, preferred_element_type=F32)
           for (w_hi, w_lo), y in zip(wsel, yb16)]
    x2 = [x1_ref[r, :] + m for r, m in zip(tiles, moe)]
    gate = [jax.nn.sigmoid(_dot(_rms(v, pgg_ref[...], NORM_EPS), wgate_ref[...])) for v in x2]
    for r, v, e, g in zip(tiles, x2, pe, gate):
        o_ref[r, :] = v + e * g


def _final(tab, x1, route, p2, w_pl, pl_g, pl_gate_g, w_gate, yb):
    T, D = x1.shape
    tm = min(PROJ_ROWS, T)
    nt = T // tm
    tps = TILES_PER_STEP if nt % TILES_PER_STEP == 0 else 1
    tr = tps * tm
    row = lambda i: (i, 0)
    full = lambda i: (0, 0)

    def fs(a):
        return pl.BlockSpec(a.shape, full)

    return pl.pallas_call(
        _final_kernel,
        grid=(nt // tps,),
        in_specs=[pl.BlockSpec((tps, 1, tab.shape[-1]), lambda i: (i, 0, 0), memory_space=pltpu.SMEM),
                  pl.BlockSpec((tps, 1, tab.shape[-1]), lambda i: (jnp.minimum(i + 1, nt // tps - 1), 0, 0),
                               memory_space=pltpu.SMEM),
                  pl.BlockSpec((tr, D), row), pl.BlockSpec((tr, ROUTE_COLS), row),
                  pl.BlockSpec((tr, PL_DIM), row), fs(w_pl), fs(pl_g), fs(pl_gate_g), fs(w_gate),
                  pl.BlockSpec(memory_space=pl.ANY)],
        out_specs=pl.BlockSpec((tr, D), row),
        out_shape=jax.ShapeDtypeStruct((T, D), F32),
        scratch_shapes=[pltpu.VMEM((2, tps, _stage_rows(tm), D // 2), PACKED), pltpu.SemaphoreType.DMA((2,))],
        compiler_params=_cparams(("arbitrary",)),
        name="combine_final",
    )(tab, tab, x1, route, p2, w_pl, pl_g, pl_gate_g, w_gate, yb)


def kernel(x, p, positions, attn_norm_g, w_in, q_norm_g, k_norm_g, lambda_q1, lambda_k1, lambda_q2, lambda_k2, diff_out_g, mu_rkv, mu_wag, w0, w_lora_a, w_lora_b, a0, a_lora_a, a_lora_b, g_lora_a, g_lora_b, k_k, k_a, r_k, gn_w, gn_b, w_out, moe_norm_g, w_group, w_expert_router, w_gate, w_up, w_down, w_pl, pl_norm_g, pl_gate_norm_g, w_pl_gate):
    B, S, D = x.shape
    T = B * S
    assert p.shape[0] == 1, "one layer"
    x2 = x.reshape(T, D)
    p2 = p[0].reshape(T, PL_DIM)
    pos = positions.astype(F32).reshape(T, 1)

    def row(a):
        return a.reshape(1, -1).astype(F32)

    half = DIFF_HEAD_DIM // 2
    inv_freq = ROPE_THETA ** (-jnp.arange(half, dtype=F32) / half)
    invf = jnp.tile(inv_freq, V7X_LANES // half).reshape(1, V7X_LANES)
    reps = DIFF_WIDTH // DIFF_HEAD_DIM

    q, k, v, zr, zk, zv, wp, ap, gg = _in_proj(
        x2, pos, row(attn_norm_g[0]), w_in[0].astype(BF16),
        w_lora_a[0].astype(BF16), a_lora_a[0].astype(BF16), g_lora_a[0].astype(BF16),
        w_lora_b[0].astype(BF16), a_lora_b[0].astype(BF16), g_lora_b[0].astype(BF16),
        mu_wag[0].astype(F32), row(jnp.tile(q_norm_g[0], reps)), row(jnp.tile(k_norm_g[0], reps)),
        invf, B, S)

    o_diff = _attention(q, k, v, row(lambda_q1[0]), row(lambda_k1[0]), row(lambda_q2[0]),
                        row(lambda_k2[0]), diff_out_g[0].astype(F32).reshape(-1, 1), B, S)
    o_rwkv = _rwkv(zr, zk, zv, wp, ap, gg, mu_rkv[0].astype(F32), row(w0[0]), row(a0[0]),
                   row(k_k[0]), row(k_a[0]), row(r_k[0]), row(gn_w[0]), row(gn_b[0]), B, S)

    wr = jnp.concatenate([w_group[0], jnp.transpose(w_expert_router[0], (1, 0, 2)).reshape(D, N_EXPERTS)],
                         axis=1).astype(F32)
    wr = jnp.pad(wr, ((0, 0), (0, V7X_LANES - wr.shape[1])))
    wr_hi = wr.astype(BF16)
    wr_lo = (wr - wr_hi.astype(F32)).astype(BF16)
    x1, hm, route, cnt, sl_rows = _out_router(x2, o_diff, o_rwkv, w_out[0].astype(BF16), row(moe_norm_g[0]),
                                     wr_hi, wr_lo)

    bm = MOE_ROWS
    tm = min(PROJ_ROWS, T)
    nt = T // tm
    i32 = jnp.int32
    cnt_te = cnt.reshape(nt, V7X_SUBLANES, V7X_LANES)[:, 0, ROUTER_EXPERT_LANE:ROUTER_EXPERT_LANE + N_EXPERTS]
    len_te = (cnt_te.astype(i32) + RUN_ALIGN - 1) // RUN_ALIGN * RUN_ALIGN
    tot_e = jnp.sum(len_te, axis=0)
    pcounts = (tot_e + bm - 1) // bm * bm
    pends = jnp.cumsum(pcounts)
    pstarts = pends - pcounts
    dst_te = pstarts[None, :] + jnp.cumsum(len_te, axis=0) - len_te
    src_te = jnp.cumsum(len_te, axis=1) - len_te
    tab = _run_table(src_te, dst_te, len_te, 0)
    nb = -(-(2 * T + nt * N_EXPERTS * (RUN_ALIGN - 1)) // bm) + N_EXPERTS
    P = nb * bm
    n_used = (pends[-1] // bm).astype(i32).reshape(1)
    block_start = jnp.arange(nb, dtype=i32) * bm
    block_e = jnp.minimum(jnp.sum((pends[None, :] <= block_start[:, None]).astype(i32), axis=1),
                          N_EXPERTS - 1)
    tail = _run_table(jnp.zeros((1, N_EXPERTS), i32), (pstarts + tot_e)[None, :], (pcounts - tot_e)[None, :],
                      n_used)
    xb = _dispatch(tab, tail, sl_rows, hm, P)
    blk = jnp.arange(nb, dtype=i32)
    first = ((blk < n_used[0]) & ((blk == 0) | (block_e != jnp.roll(block_e, 1)))).astype(i32)
    slot = (jnp.cumsum(first) - 1) % 2
    eidx = jnp.arange(N_EXPERTS, dtype=i32)
    later = jnp.where((pcounts > 0)[None, :] & (eidx[None, :] > eidx[:, None]), eidx[None, :], N_EXPERTS)
    nxt_e = jnp.min(later, axis=1)
    nxt = jnp.where(nxt_e == N_EXPERTS, -1, nxt_e)[block_e]
    yb = _experts(block_e, n_used, first, nxt, slot.astype(i32), xb, w_gate[0], w_up[0], w_down[0])
    out = _final(tab, x1, route, p2, w_pl[0].astype(BF16), row(pl_norm_g[0]), row(pl_gate_norm_g[0]),
                 w_pl_gate[0].astype(BF16), yb)
    return out.reshape(B, S, D)
```

```python
import math

import jax
import jax.numpy as jnp
from jax import lax
from jax.experimental import pallas as pl
from jax.experimental.pallas import tpu as pltpu

F32 = jnp.float32
BF16 = jnp.bfloat16

D_MODEL = 1024
PL_DIM = 256
DIFF_WIDTH = 512
RWKV_WIDTH = 512
DIFF_HEAD_DIM = 64
DIFF_HEADS = 4
RWKV_HEAD = 64
ROPE_THETA = 10000.0
NORM_EPS = 1e-6
SUBLN_EPS = 1e-5
GN_EPS = 64e-5
N_GROUPS = 4
EXPERTS_PER_GROUP = 8
N_EXPERTS = 32
D_EXPERT = 512
LAM_INIT = 0.8 - 0.6 * math.exp(0.0)

V7X_LANES = 128
V7X_SUBLANES = 8
V7X_VMEM_BYTES = 64 * 1024 * 1024

IN_PROJ_ROWS = 256
IN_PROJ_TILES_PER_STEP = 2
PROJ_ROWS = 256
TILES_PER_STEP = 4
ATTN_ROWS = 256
ATTN_UNROLL = 4
ATTN_HEADS_PER_STEP = 2
RWKV_CHUNK = 64
RWKV_ROWS = 256
MOE_ROWS = 512
VMEM_LIMIT = V7X_VMEM_BYTES - 8 * 1024 * 1024


def _cparams(sem):
    return pltpu.CompilerParams(dimension_semantics=sem, vmem_limit_bytes=VMEM_LIMIT)


def _dot(a, b):
    return jnp.dot(a.astype(BF16), b.astype(BF16), preferred_element_type=F32)


def _dot_nt(a, b):
    return lax.dot_general(a.astype(BF16), b.astype(BF16), (((1,), (1,)), ((), ())),
                           preferred_element_type=F32)


def _split3(x):
    hi = x.astype(BF16)
    r1 = x - hi.astype(F32)
    mid = r1.astype(BF16)
    lo = (r1 - mid.astype(F32)).astype(BF16)
    return hi, mid, lo


def _dot_exact_lhs(m01, x):
    hi, mid, lo = _split3(x)
    return (jnp.dot(m01, hi, preferred_element_type=F32)
            + jnp.dot(m01, mid, preferred_element_type=F32)
            + jnp.dot(m01, lo, preferred_element_type=F32))


PACKED = jnp.uint32


def _pack16(x):
    h = x.shape[1] // 2
    lo = lax.bitcast_convert_type(x[:, :h], PACKED) >> 16
    hi = lax.bitcast_convert_type(x[:, h:], PACKED) & jnp.uint32(0xFFFF0000)
    return hi | lo


def _unpack16(w):
    lo = lax.bitcast_convert_type(w << 16, F32)
    hi = lax.bitcast_convert_type(w & jnp.uint32(0xFFFF0000), F32)
    return jnp.concatenate([lo, hi], axis=1)


def _sumsq_64(z):
    m, n = z.shape
    group = V7X_LANES // 2
    first = lax.broadcasted_iota(jnp.int32, (m, V7X_LANES), 1) < group
    out = []
    for c in range(0, n, V7X_LANES):
        sq = z[:, c:c + V7X_LANES] * z[:, c:c + V7X_LANES]
        s0 = jnp.sum(jnp.where(first, sq, 0.0), axis=-1, keepdims=True)
        s1 = jnp.sum(jnp.where(first, 0.0, sq), axis=-1, keepdims=True)
        out.append(jnp.where(first, s0, s1))
    return jnp.concatenate(out, axis=1)


def _rms(x, g, eps):
    return x * lax.rsqrt(jnp.mean(x * x, axis=-1, keepdims=True) + eps) * g


def _shift_rows(z, prev_row):
    rolled = pltpu.roll(z, 1, axis=0)
    row = lax.broadcasted_iota(jnp.int32, z.shape, 0)
    return jnp.where(row == 0, prev_row, rolled)


def _in_proj_kernel(x_ref, pos_ref, g_ref, win_ref, w1_ref, a1_ref, g1_ref, w2_ref, a2_ref, g2_ref,
                    mu_ref, qg_ref, kg_ref, invf_ref,
                    q_out, k_out, v_out, zr_out, zk_out, zv_out, wp_out, ap_out, gg_out,
                    carry_ref):
    tm = IN_PROJ_ROWS if x_ref.shape[0] >= IN_PROJ_ROWS else x_ref.shape[0]
    tiles = [slice(t * tm, (t + 1) * tm) for t in range(x_ref.shape[0] // tm)]
    w = DIFF_WIDTH
    half = DIFF_HEAD_DIM // 2

    @pl.when(pl.program_id(1) == 0)
    def _():
        carry_ref[...] = jnp.zeros_like(carry_ref)

    prev = carry_ref[V7X_SUBLANES - 1:V7X_SUBLANES, :]
    hns, dhs = [], []
    for r in tiles:
        hn = _rms(x_ref[r, :], g_ref[...], NORM_EPS)
        dhs.append(_shift_rows(hn, prev) - hn)
        hns.append(hn)
        prev = hn[tm - 1:tm, :]
    carry_ref[...] = hns[-1][tm - V7X_SUBLANES:tm, :]

    zq, zk, low = [], [], []
    for r, hn, dh in zip(tiles, hns, dhs):
        hb = hn.astype(BF16)

        def proj(c, hb=hb):
            return jnp.dot(hb, win_ref[:, c * w:(c + 1) * w], preferred_element_type=F32)

        zq.append(proj(0))
        zk.append(proj(1))
        low.append((_dot(hn + dh * mu_ref[0:1, :], w1_ref[...]), _dot(hn + dh * mu_ref[1:2, :], a1_ref[...]),
                    _dot(hn + dh * mu_ref[2:3, :], g1_ref[...])))
        v_out[r, :] = proj(2).astype(v_out.dtype)
        zr_out[r, :] = proj(3)
        zk_out[r, :] = proj(4)
        zv_out[r, :] = proj(5)

    lane = lax.broadcasted_iota(jnp.int32, (tm, w), 1)
    first_half = (lane % DIFF_HEAD_DIM) < half
    scale = DIFF_HEAD_DIM ** -0.5 * math.log2(math.e)
    for r, q, k, (lw, la, lg) in zip(tiles, zq, zk, low):
        ang = pos_ref[r, :] * invf_ref[...]
        cosf = jnp.concatenate([jnp.cos(ang)] * (w // V7X_LANES), axis=1)
        sinf = jnp.concatenate([jnp.sin(ang)] * (w // V7X_LANES), axis=1)
        sin_signed = jnp.where(first_half, -sinf, sinf)

        def norm_rope(z, gain, cosf=cosf, sin_signed=sin_signed):
            zn = z * lax.rsqrt(_sumsq_64(z) * (1.0 / DIFF_HEAD_DIM) + NORM_EPS) * gain
            partner = jnp.where(first_half, pltpu.roll(zn, w - half, axis=1), pltpu.roll(zn, half, axis=1))
            return zn * cosf + partner * sin_signed

        q_out[r, :] = (norm_rope(q, qg_ref[...]) * scale).astype(q_out.dtype)
        k_out[r, :] = norm_rope(k, kg_ref[...]).astype(k_out.dtype)
        wp_out[r, :] = _dot(jnp.tanh(lw), w2_ref[...])
        ap_out[r, :] = _dot(la, a2_ref[...])
        gg_out[r, :] = _dot(jax.nn.sigmoid(lg), g2_ref[...])


def _in_proj(x2, pos, attn_g, w_in, w1, a1, g1, w2, a2, g2, mu_wag, qg, kg, invf, B, S):
    T, D = x2.shape
    tile = min(IN_PROJ_ROWS, S)
    tps = IN_PROJ_TILES_PER_STEP if (S // tile) % IN_PROJ_TILES_PER_STEP == 0 else 1
    tm = tile * tps
    ns = S // tm
    w = DIFF_WIDTH
    row = lambda b, i: (b * ns + i, 0)
    full = lambda b, i: (0, 0)

    def fs(a):
        return pl.BlockSpec(a.shape, full)

    outs = ([jax.ShapeDtypeStruct((T, w), BF16)] * 3 + [jax.ShapeDtypeStruct((T, w), F32)] * 6)
    return pl.pallas_call(
        _in_proj_kernel,
        grid=(B, ns),
        in_specs=[pl.BlockSpec((tm, D), row), pl.BlockSpec((tm, 1), row), fs(attn_g), fs(w_in),
                  fs(w1), fs(a1), fs(g1), fs(w2), fs(a2), fs(g2), fs(mu_wag), fs(qg), fs(kg),
                  fs(invf)],
        out_specs=[pl.BlockSpec((tm, w), row)] * 9,
        out_shape=outs,
        scratch_shapes=[pltpu.VMEM((V7X_SUBLANES, D), F32)],
        compiler_params=_cparams(("arbitrary", "arbitrary")),
        name="in_proj",
    )(x2, pos, attn_g, w_in, w1, a1, g1, w2, a2, g2, mu_wag, qg, kg, invf)


def _attn_kernel(lq1_ref, lk1_ref, lq2_ref, lk2_ref, ogt_ref, q_ref, k_ref, v_ref, o_ref,
                 vt_ref, s_ref, acc_ref):
    nh, nq, hw, tq = vt_ref.shape
    tk = tq
    heads = range(nh)
    lam = (jnp.exp(jnp.sum(lq1_ref[...] * lk1_ref[...], axis=-1, keepdims=True))
           - jnp.exp(jnp.sum(lq2_ref[...] * lk2_ref[...], axis=-1, keepdims=True)) + LAM_INIT)

    for h in heads:
        for c in range(nq):
            vt_ref[h, c] = (v_ref[c * tk:(c + 1) * tk, h * hw:(h + 1) * hw]
                            .astype(F32).T.astype(vt_ref.dtype))

    lane = lax.broadcasted_iota(jnp.int32, (tq, hw), 1)

    def stacked_queries(t):
        out = []
        for h in heads:
            q = q_ref[t * tq:(t + 1) * tq, h * hw:(h + 1) * hw]
            zero = jnp.zeros_like(q)
            out.append(jnp.concatenate([jnp.where(lane < DIFF_HEAD_DIM, q, zero),
                                        jnp.where(lane >= DIFF_HEAD_DIM, q, zero)], axis=0))
        return out

    def scores(j, qs):
        start = pl.multiple_of(j * tk, tk)
        return [lax.dot_general(k_ref[pl.ds(start, tk), h * hw:(h + 1) * hw], qs[h],
                                (((1,), (1,)), ((), ())), preferred_element_type=F32) for h in heads]

    def colmax(m, st):
        return tuple(jnp.maximum(mh, jnp.max(s, axis=0, keepdims=True)) for mh, s in zip(m, st))

    krow = lax.broadcasted_iota(jnp.int32, (tk, 2 * tq), 0)
    qcol = lax.broadcasted_iota(jnp.int32, (tk, 2 * tq), 1)
    causal = krow <= jnp.where(qcol >= tq, qcol - tq, qcol)
    ninf = tuple(jnp.full((1, 2 * tq), -jnp.inf, F32) for _ in heads)
    zeros = tuple(jnp.zeros((1, 2 * tq), F32) for _ in heads)

    def diagonal(t, qs, m):
        st = [jnp.where(causal, s, -jnp.inf) for s in scores(t, qs)]
        for h in heads:
            s_ref[t % 2, h, t] = st[h]
        return colmax(m, st)

    m_cur = diagonal(0, stacked_queries(0), ninf)
    for t in range(nq):
        slot, nslot = t % 2, (t + 1) % 2
        has_next = t + 1 < nq
        qs_next = stacked_queries(t + 1) if has_next else None
        acc_ref[...] = jnp.zeros_like(acc_ref)

        def body(j, carry, slot=slot, nslot=nslot, has_next=has_next, qs_next=qs_next, m_cur=m_cur):
            l, m_next = carry
            pt = [jnp.exp2(s_ref[slot, h, j] - m_cur[h]) for h in heads]
            pv = [jnp.dot(vt_ref[h, j], pt[h].astype(BF16), preferred_element_type=F32) for h in heads]
            for h in heads:
                acc_ref[h] += pv[h]
            l = tuple(lh + jnp.sum(p, axis=0, keepdims=True) for lh, p in zip(l, pt))
            if has_next:
                st = scores(j, qs_next)
                for h in heads:
                    s_ref[nslot, h, j] = st[h]
                m_next = colmax(m_next, st)
            return l, m_next

        lsum, m_next = lax.fori_loop(0, t + 1, body, (zeros, ninf), unroll=ATTN_UNROLL)
        if has_next:
            m_next = diagonal(t + 1, qs_next, m_next)

        for h in heads:
            acc = acc_ref[h]
            l = lsum[h]
            ot = acc[:, :tq] / l[:, :tq] - lam * (acc[:, tq:] / l[:, tq:])
            ot = ot * lax.rsqrt(jnp.mean(ot * ot, axis=0, keepdims=True) + SUBLN_EPS) * ogt_ref[...]
            o_ref[t * tq:(t + 1) * tq, h * hw:(h + 1) * hw] = (ot * (1.0 - LAM_INIT)).T.astype(o_ref.dtype)
        m_cur = m_next


def _attention(q, k, v, lq1, lk1, lq2, lk2, og, B, S):
    T = q.shape[0]
    tq = min(ATTN_ROWS, S)
    nq = S // tq
    hw = 2 * DIFF_HEAD_DIM
    nh = ATTN_HEADS_PER_STEP
    gw = nh * hw
    small = lambda b, h: (0, 0)
    seq = pl.BlockSpec((S, gw), lambda b, h: (b, h))
    return pl.pallas_call(
        _attn_kernel,
        grid=(B, DIFF_HEADS // nh),
        in_specs=[pl.BlockSpec(lq1.shape, small), pl.BlockSpec(lk1.shape, small),
                  pl.BlockSpec(lq2.shape, small), pl.BlockSpec(lk2.shape, small),
                  pl.BlockSpec(og.shape, small), seq, seq, seq],
        out_specs=seq,
        out_shape=jax.ShapeDtypeStruct((T, DIFF_WIDTH), BF16),
        scratch_shapes=[pltpu.VMEM((nh, nq, hw, tq), BF16), pltpu.VMEM((2, nh, nq, tq, 2 * tq), F32),
                        pltpu.VMEM((nh, hw, 2 * tq), F32)],
        compiler_params=_cparams(("arbitrary", "arbitrary")),
        name="diff_attn",
    )(lq1, lk1, lq2, lk2, og, q, k, v)


def _rwkv_kernel(zr_ref, zk_ref, zv_ref, wp_ref, ap_ref, gg_ref, mu_ref, w0_ref, a0_ref, kk_ref,
                 ka_ref, rk_ref, gnw_ref, gnb_ref, o_ref, state_ref, carry_ref):
    R = zr_ref.shape[0]
    C = min(RWKV_CHUNK, R)
    nch = R // C
    N = RWKV_HEAD
    W = RWKV_WIDTH
    HP = 2 * N
    n_pairs = W // HP

    @pl.when(pl.program_id(1) == 0)
    def _():
        state_ref[...] = jnp.zeros_like(state_ref)
        carry_ref[...] = jnp.zeros_like(carry_ref)

    zr, zk, zv = zr_ref[...], zk_ref[...], zv_ref[...]
    last = V7X_SUBLANES - 1
    r = zr + (_shift_rows(zr, carry_ref[last:last + 1, 0:W]) - zr) * mu_ref[0:1, :]
    k = zk + (_shift_rows(zk, carry_ref[last:last + 1, W:2 * W]) - zk) * mu_ref[1:2, :]
    v = zv + (_shift_rows(zv, carry_ref[last:last + 1, 2 * W:3 * W]) - zv) * mu_ref[2:3, :]
    carry_ref[:, 0:W] = zr[R - V7X_SUBLANES:R, :]
    carry_ref[:, W:2 * W] = zk[R - V7X_SUBLANES:R, :]
    carry_ref[:, 2 * W:3 * W] = zv[R - V7X_SUBLANES:R, :]

    lw = -math.exp(-0.5) * jax.nn.sigmoid(w0_ref[...] + wp_ref[...])
    a = jax.nn.sigmoid(a0_ref[...] + ap_ref[...])
    kk = k * kk_ref[...]
    kk = kk * lax.rsqrt(jnp.maximum(_sumsq_64(kk), 1e-24))
    k2 = k * (1.0 + (a - 1.0) * ka_ref[...])
    a_s = -kk
    b_s = kk * a

    rr = lax.broadcasted_iota(jnp.int32, (R, R), 0)
    cc = lax.broadcasted_iota(jnp.int32, (R, R), 1)
    same_chunk = (rr // C) == (cc // C)
    L = _dot_exact_lhs((same_chunk & (cc <= rr)).astype(BF16), lw)
    bonus_w = r * k2 * rk_ref[...]

    P2 = 2 * C
    sr = lax.broadcasted_iota(jnp.int32, (P2, HP), 0)
    sc = lax.broadcasted_iota(jnp.int32, (P2, HP), 1)
    stack_mask = (sr < C) == (sc < N)
    br = lax.broadcasted_iota(jnp.int32, (P2, P2), 0)
    bc = lax.broadcasted_iota(jnp.int32, (P2, P2), 1)
    same_head = (br < C) == (bc < C)
    tr = jnp.where(br >= C, br - C, br)
    tc = jnp.where(bc >= C, bc - C, bc)
    strict = same_head & (tc < tr)
    lower = same_head & (tc <= tr)
    eye_p = (br == bc).astype(F32)
    kr = lax.broadcasted_iota(jnp.int32, (HP, HP), 0)
    kc = lax.broadcasted_iota(jnp.int32, (HP, HP), 1)
    eye_k = kr == kc

    def dup(x):
        return jnp.concatenate([x, x], axis=0)

    def stack(x):
        return jnp.where(stack_mask, dup(x), 0.0)

    n_sq = int(math.log2(C)) - 1
    items = [(c, p) for c in range(nch) for p in range(n_pairs)]

    def prep(c, p):
        rows = slice(c * C, (c + 1) * C)
        lanes = slice(p * HP, (p + 1) * HP)
        Lc = L[rows, lanes]
        lwc = lw[rows, lanes]
        LC = Lc[C - 1:C, :]
        enL = jnp.exp(-Lc)
        eCL = jnp.exp(LC - Lc)
        b_c, k_c = b_s[rows, lanes], k2[rows, lanes]
        return dict(
            xa=stack(a_s[rows, lanes] * jnp.exp(Lc - lwc)), xr=stack(r[rows, lanes] * jnp.exp(Lc)),
            bt=dup(b_c * enL), kt=dup(k_c * enL), bh=stack(b_c * eCL), kh=stack(k_c * eCL),
            vs=stack(v[rows, lanes]), pc=jnp.exp(LC))

    d = [prep(c, p) for c, p in items]
    aa = [_dot_nt(jnp.concatenate([e["xa"], e["xr"]], axis=0),
                  jnp.concatenate([e["bt"], e["kt"]], axis=0)) for e in d]
    a_ab = [jnp.where(strict, m[:P2, :P2], 0.0) for m in aa]
    a_ak = [jnp.where(strict, m[:P2, P2:], 0.0) for m in aa]
    a_rb = [jnp.where(lower, m[P2:, :P2], 0.0) for m in aa]
    a_rk = [jnp.where(lower, m[P2:, P2:], 0.0) for m in aa]
    tm = [eye_p + m for m in a_ab]
    npw = a_ab
    for _ in range(n_sq):
        npw = [_dot(m, m) for m in npw]
        tm = [t + _dot(t, m) for t, m in zip(tm, npw)]
    av = [_dot(m, e["vs"]) for m, e in zip(a_ak, d)]
    z = [_dot(t, jnp.concatenate([e["xa"], x], axis=1)) for t, e, x in zip(tm, d, av)]
    zero_v = jnp.zeros((P2, HP), F32)
    rhs = [jnp.concatenate([x, jnp.concatenate([zero_v, e["vs"]], axis=1)], axis=0) for e, x in zip(d, z)]
    lhs = [jnp.concatenate([jnp.concatenate([m1, m2], axis=1),
                            jnp.concatenate([e["bh"], e["kh"]], axis=0).T], axis=0)
           for m1, m2, e in zip(a_rb, a_rk, d)]
    wg = [_dot(a, b) for a, b in zip(lhs, rhs)]

    for idx, (c, p) in enumerate(items):
        e = d[idx]
        rows = slice(c * C, (c + 1) * C)
        lanes = slice(p * HP, (p + 1) * HP)
        rp = e["xr"] + wg[idx][:P2, :HP]
        y0 = wg[idx][:P2, HP:]
        gm = wg[idx][P2:, :HP]
        hm = wg[idx][P2:, HP:]
        st = state_ref[p]
        yg = _dot(jnp.concatenate([rp, gm], axis=0), st)
        pc_col = jnp.sum(jnp.where(eye_k, e["pc"], 0.0), axis=1, keepdims=True)
        state_ref[p] = pc_col * st + yg[P2:] + hm
        ys = yg[:P2] + y0
        mu = jnp.sum(ys, axis=-1, keepdims=True) * (1.0 / N)
        yc = jnp.where(stack_mask, ys - mu, 0.0)
        var = jnp.sum(yc * yc, axis=-1, keepdims=True) * (1.0 / N)
        yn = yc * lax.rsqrt(var + GN_EPS)
        bonus = jnp.sum(stack(bonus_w[rows, lanes]), axis=-1, keepdims=True) * e["vs"]
        yn = yn[:C] + yn[C:]
        bonus = bonus[:C] + bonus[C:]
        o_ref[rows, lanes] = ((yn * gnw_ref[:, lanes] + gnb_ref[:, lanes] + bonus)
                              * gg_ref[rows, lanes]).astype(o_ref.dtype)


def _rwkv(zr, zk, zv, wp, ap, gg, mu_rkv, w0, a0, k_k, k_a, r_k, gn_w, gn_b, B, S):
    T = zr.shape[0]
    C = min(RWKV_ROWS, S)
    nc = S // C
    W = RWKV_WIDTH
    HP = 2 * RWKV_HEAD
    row = lambda b, i: (b * nc + i, 0)
    full = lambda b, i: (0, 0)

    def fs(a):
        return pl.BlockSpec(a.shape, full)

    return pl.pallas_call(
        _rwkv_kernel,
        grid=(B, nc),
        in_specs=[pl.BlockSpec((C, W), row)] * 6 + [fs(mu_rkv), fs(w0), fs(a0), fs(k_k), fs(k_a),
                                                    fs(r_k), fs(gn_w), fs(gn_b)],
        out_specs=pl.BlockSpec((C, W), row),
        out_shape=jax.ShapeDtypeStruct((T, W), BF16),
        scratch_shapes=[pltpu.VMEM((W // HP, HP, HP), F32),
                        pltpu.VMEM((V7X_SUBLANES, 3 * W), F32)],
        compiler_params=_cparams(("arbitrary", "arbitrary")),
        name="rwkv7",
    )(zr, zk, zv, wp, ap, gg, mu_rkv, w0, a0, k_k, k_a, r_k, gn_w, gn_b)


ROUTE_COLS = 8
ROUTER_EXPERT_LANE = N_GROUPS


def _out_router_kernel(x_ref, od_ref, orw_ref, wo_ref, mg_ref, wrh_ref, wrl_ref,
                       x1_out, hm_out, route_out, cnt_out, sl_out):
    tm = PROJ_ROWS if x_ref.shape[0] >= PROJ_ROWS else x_ref.shape[0]
    tiles = [slice(t * tm, (t + 1) * tm) for t in range(x_ref.shape[0] // tm)]
    x1 = [x_ref[r, :] + jnp.dot(od_ref[r, :], wo_ref[0:DIFF_WIDTH, :], preferred_element_type=F32)
          + jnp.dot(orw_ref[r, :], wo_ref[DIFF_WIDTH:, :], preferred_element_type=F32) for r in tiles]
    hm = [_rms(v, mg_ref[...], NORM_EPS) for v in x1]
    for r, v, h in zip(tiles, x1, hm):
        x1_out[r, :] = v
        hm_out[r, :] = h.astype(hm_out.dtype)

    def logits(h):
        hi = h.astype(BF16)
        lo = (h - hi.astype(F32)).astype(BF16)
        return (jnp.dot(hi, wrh_ref[...], preferred_element_type=F32)
                + jnp.dot(hi, wrl_ref[...], preferred_element_type=F32)
                + jnp.dot(lo, wrh_ref[...], preferred_element_type=F32))

    lgs = [logits(h) for h in hm]
    lane = lax.broadcasted_iota(jnp.int32, (tm, V7X_LANES), 1).astype(F32)
    big = float(V7X_LANES)
    ninf = -jnp.inf
    gmask = lane < N_GROUPS

    def choose(lg):
        gmax = jnp.max(jnp.where(gmask, lg, ninf), axis=-1, keepdims=True)
        g_sel = jnp.min(jnp.where(gmask & (lg == gmax), lane, big), axis=-1, keepdims=True)
        g_w = 1.0 / jnp.sum(jnp.where(gmask, jnp.exp(lg - gmax), 0.0), axis=-1, keepdims=True)
        lo_lane = ROUTER_EXPERT_LANE + g_sel * EXPERTS_PER_GROUP
        emask = (lane >= lo_lane) & (lane < lo_lane + EXPERTS_PER_GROUP)
        v1 = jnp.max(jnp.where(emask, lg, ninf), axis=-1, keepdims=True)
        i1 = jnp.min(jnp.where(emask & (lg == v1), lane, big), axis=-1, keepdims=True)
        emask2 = emask & (lane != i1)
        v2 = jnp.max(jnp.where(emask2, lg, ninf), axis=-1, keepdims=True)
        i2 = jnp.min(jnp.where(emask2 & (lg == v2), lane, big), axis=-1, keepdims=True)
        e2x = jnp.exp(v2 - v1)
        den = 1.0 + e2x
        return i1, i2, (1.0 / den) * g_w, (e2x / den) * g_w

    chosen = [choose(lg) for lg in lgs]

    ri = lax.broadcasted_iota(jnp.int32, (tm, tm), 0)
    ci = lax.broadcasted_iota(jnp.int32, (tm, tm), 1)
    before = (ci < ri).astype(BF16)
    li = lax.broadcasted_iota(jnp.int32, (V7X_LANES, V7X_LANES), 0)
    lj = lax.broadcasted_iota(jnp.int32, (V7X_LANES, V7X_LANES), 1)
    lanes_before = (li < lj).astype(BF16)
    ohs = [((lane == i1) | (lane == i2)).astype(F32) for i1, i2, _, _ in chosen]
    prefixes = [jnp.dot(before, oh.astype(BF16), preferred_element_type=F32) for oh in ohs]
    counts = [jnp.broadcast_to(jnp.sum(oh, axis=0, keepdims=True), (V7X_SUBLANES, V7X_LANES)) for oh in ohs]
    padded = [jnp.floor((c + (RUN_ALIGN - 1)) * (1.0 / RUN_ALIGN)) * RUN_ALIGN for c in counts]
    starts = [jnp.dot(pc.astype(BF16), lanes_before, preferred_element_type=F32) for pc in padded]
    col = lax.broadcasted_iota(jnp.int32, (tm, V7X_LANES), 1)
    for t, (r, (i1, i2, wt1, wt2), cnt, start, prefix) in enumerate(zip(tiles, chosen, counts, starts, prefixes)):
        row = prefix + start[0:1, :]
        sl1 = jnp.sum(jnp.where(lane == i1, row, 0.0), axis=-1, keepdims=True)
        sl2 = jnp.sum(jnp.where(lane == i2, row, 0.0), axis=-1, keepdims=True)
        cnt_out[t * V7X_SUBLANES:(t + 1) * V7X_SUBLANES, :] = cnt
        wide = jnp.where(col == 0, sl1, jnp.where(col == 1, sl2, jnp.where(col == 2, wt1, jnp.where(
            col == 3, wt2, 0.0))))
        route_out[r, :] = wide[:, 0:ROUTE_COLS]
        sl_out[t] = wide.T[0:V7X_SUBLANES, :].astype(sl_out.dtype)


def _out_router(x2, od, orw, w_out, moe_g, wr_hi, wr_lo):
    T, D = x2.shape
    tm = min(PROJ_ROWS, T)
    nt = T // tm
    tps = TILES_PER_STEP if nt % TILES_PER_STEP == 0 else 1
    tr = tm * tps
    row = lambda i: (i, 0)
    full = lambda i: (0, 0)

    def fs(a):
        return pl.BlockSpec(a.shape, full)

    return pl.pallas_call(
        _out_router_kernel,
        grid=(nt // tps,),
        in_specs=[pl.BlockSpec((tr, D), row), pl.BlockSpec((tr, DIFF_WIDTH), row),
                  pl.BlockSpec((tr, RWKV_WIDTH), row), fs(w_out), fs(moe_g), fs(wr_hi), fs(wr_lo)],
        out_specs=[pl.BlockSpec((tr, D), row), pl.BlockSpec((tr, D), row),
                   pl.BlockSpec((tr, ROUTE_COLS), row), pl.BlockSpec((tps * V7X_SUBLANES, V7X_LANES), row),
                   pl.BlockSpec((tps, V7X_SUBLANES, tm), lambda i: (i, 0, 0))],
        out_shape=[jax.ShapeDtypeStruct((T, D), F32), jax.ShapeDtypeStruct((T, D), BF16),
                   jax.ShapeDtypeStruct((T, ROUTE_COLS), F32),
                   jax.ShapeDtypeStruct((nt * V7X_SUBLANES, V7X_LANES), F32),
                   jax.ShapeDtypeStruct((nt, V7X_SUBLANES, tm), jnp.int32)],
        compiler_params=_cparams(("arbitrary",)),
        name="out_router",
    )(x2, od, orw, w_out, moe_g, wr_hi, wr_lo)


RUN_ALIGN = V7X_SUBLANES
RUN_CLASSES = 6
TAB_COUNT, TAB_NUSED = 0, RUN_CLASSES
TAB_SRC = V7X_SUBLANES
TAB_DST = TAB_SRC + RUN_CLASSES * N_EXPERTS
TAB_WORDS = 4 * V7X_LANES


def _stage_rows(tm):
    need = 2 * tm + N_EXPERTS * (RUN_ALIGN - 1)
    return -(-need // V7X_LANES) * V7X_LANES


def _run_table(src, dst, length, spare):
    i32 = jnp.int32
    n = src.shape[0]
    cls = jnp.arange(RUN_CLASSES, dtype=i32)[None, :, None]
    ln = length[:, None, :]
    has = ((ln >> 3) >> cls) & 1
    off = ln & ~((2 * RUN_ALIGN << cls) - 1)
    pos = jnp.cumsum(has, axis=-1) - 1
    k = jnp.arange(N_EXPERTS, dtype=i32)
    pick = (has[..., None] == 1) & (pos[..., None] == k)

    def compact(v):
        return jnp.sum(jnp.where(pick, (v[:, None, :] + off)[..., None], 0), axis=2).reshape(n, -1)

    head = jnp.concatenate([jnp.sum(has, axis=-1), jnp.broadcast_to(spare, (n, 1)).astype(i32),
                            jnp.zeros((n, TAB_SRC - RUN_CLASSES - 1), i32)], axis=1)
    tab = jnp.concatenate([head, compact(src), compact(dst)], axis=1)
    return jnp.pad(tab, ((0, 0), (0, TAB_WORDS - tab.shape[1]))).reshape(n, 1, TAB_WORDS)


def _run_copies(tab_ref, tile, make_copy, op):
    for c in range(RUN_CLASSES):
        size = RUN_ALIGN << c

        def one(k, carry, c=c, size=size):
            src = tab_ref[tile, 0, TAB_SRC + c * N_EXPERTS + k]
            dst = tab_ref[tile, 0, TAB_DST + c * N_EXPERTS + k]
            getattr(make_copy(pl.multiple_of(src, RUN_ALIGN), pl.multiple_of(dst, RUN_ALIGN), size), op)()
            return carry

        lax.fori_loop(0, tab_ref[tile, 0, TAB_COUNT + c], one, 0)


def _dispatch_kernel(tab_ref, tail_ref, sl_ref, hm_ref, xb_ref, stage_ref, zero_ref, sem, zsem):
    tps, ns = stage_ref.shape[0], stage_ref.shape[1]
    tm = hm_ref.shape[0] // tps
    bm = zero_ref.shape[0]
    nb = xb_ref.shape[0] // bm
    n_used = tail_ref[0, 0, TAB_NUSED]

    def tile_copy(t):
        def make_copy(src, dst, size):
            return pltpu.make_async_copy(stage_ref.at[t, pl.ds(src, size)], xb_ref.at[pl.ds(dst, size)], sem)
        return make_copy

    def zero_copy(src, dst, size):
        return pltpu.make_async_copy(zero_ref.at[pl.ds(src, size)], xb_ref.at[pl.ds(dst, size)], zsem)

    def zero_blocks(op):
        def one(b, c):
            getattr(zero_copy(0, pl.multiple_of(b * bm, bm), bm), op)()
            return c
        lax.fori_loop(n_used, nb, one, 0)

    @pl.when(pl.program_id(0) == 0)
    def _():
        zero_ref[...] = jnp.zeros_like(zero_ref)
        _run_copies(tail_ref, 0, zero_copy, "start")
        zero_blocks("start")

    srow = lax.broadcasted_iota(jnp.int32, (ns, tm), 0)
    for t in range(tps):
        sel = (srow == sl_ref[t, 0:1, :]) | (srow == sl_ref[t, 1:2, :])
        stage_ref[t] = _pack16(jnp.dot(sel.astype(BF16), hm_ref[t * tm:(t + 1) * tm, :],
                                       preferred_element_type=F32))
        _run_copies(tab_ref, t, tile_copy(t), "start")
    for t in range(tps):
        _run_copies(tab_ref, t, tile_copy(t), "wait")

    @pl.when(pl.program_id(0) == pl.num_programs(0) - 1)
    def _():
        _run_copies(tail_ref, 0, zero_copy, "wait")
        zero_blocks("wait")


def _dispatch(tab, tail, sl_rows, hm, P):
    T, D = hm.shape
    tm = min(PROJ_ROWS, T)
    nt = T // tm
    tps = TILES_PER_STEP if nt % TILES_PER_STEP == 0 else 1
    return pl.pallas_call(
        _dispatch_kernel,
        grid=(nt // tps,),
        in_specs=[pl.BlockSpec((tps, 1, tab.shape[-1]), lambda i: (i, 0, 0), memory_space=pltpu.SMEM),
                  pl.BlockSpec((1, 1, tail.shape[-1]), lambda i: (0, 0, 0), memory_space=pltpu.SMEM),
                  pl.BlockSpec((tps, V7X_SUBLANES, tm), lambda i: (i, 0, 0)),
                  pl.BlockSpec((tps * tm, D), lambda i: (i, 0))],
        out_specs=pl.BlockSpec(memory_space=pl.ANY),
        out_shape=jax.ShapeDtypeStruct((P, D // 2), PACKED),
        scratch_shapes=[pltpu.VMEM((tps, _stage_rows(tm), D // 2), PACKED),
                        pltpu.VMEM((MOE_ROWS, D // 2), PACKED),
                        pltpu.SemaphoreType.DMA(()), pltpu.SemaphoreType.DMA(())],
        compiler_params=_cparams(("arbitrary",)),
        name="moe_dispatch",
    )(tab, tail, sl_rows, hm)


def _expert_kernel(be_ref, nu_ref, first_ref, nxt_ref, slot_ref, xb_ref, wg_hbm, wu_hbm, wd_hbm, yb_ref,
                   wgf, wuf, wdf, wgb, wub, wdb, sem):
    i = pl.program_id(0)

    def weight_copies(e, s):
        return (pltpu.make_async_copy(wg_hbm.at[e], wgf.at[s], sem.at[s, 0]),
                pltpu.make_async_copy(wu_hbm.at[e], wuf.at[s], sem.at[s, 1]),
                pltpu.make_async_copy(wd_hbm.at[e], wdf.at[s], sem.at[s, 2]))

    @pl.when(i == 0)
    def _():
        for cp in weight_copies(be_ref[0], 0):
            cp.start()

    @pl.when(first_ref[i] == 1)
    def _():
        s = slot_ref[i]
        for cp in weight_copies(be_ref[i], s):
            cp.wait()
        wgb[...] = wgf[s].astype(BF16)
        wub[...] = wuf[s].astype(BF16)
        wdb[...] = wdf[s].astype(BF16)

        @pl.when(nxt_ref[i] >= 0)
        def _():
            for cp in weight_copies(nxt_ref[i], 1 - s):
                cp.start()

    @pl.when(i < nu_ref[0])
    def _():
        xb = _unpack16(xb_ref[...]).astype(BF16)
        gate = jnp.dot(xb, wgb[...], preferred_element_type=F32)
        up = jnp.dot(xb, wub[...], preferred_element_type=F32)
        hdn = (gate * jax.nn.sigmoid(gate)) * up
        y = jnp.dot(hdn.astype(BF16), wdb[...], preferred_element_type=F32)
        yb_ref[...] = _pack16(y.astype(BF16).astype(F32))

    @pl.when(i >= nu_ref[0])
    def _():
        yb_ref[...] = jnp.zeros_like(yb_ref)


def _experts(block_e, n_used, first, nxt, slot, xb, w_gate, w_up, w_down):
    P, DP = xb.shape
    D = 2 * DP
    bm = MOE_ROWS
    nb = P // bm
    E = D_EXPERT
    hbm = pl.BlockSpec(memory_space=pl.ANY)
    grid_spec = pltpu.PrefetchScalarGridSpec(
        num_scalar_prefetch=5,
        grid=(nb,),
        in_specs=[pl.BlockSpec((bm, DP), lambda i, be, nu, *_: (jnp.minimum(i, nu[0] - 1), 0)), hbm, hbm, hbm],
        out_specs=pl.BlockSpec((bm, DP), lambda i, be, nu, *_: (jnp.where(i < nu[0], i, nb - 1), 0)),
        scratch_shapes=[pltpu.VMEM((2, D, E), F32), pltpu.VMEM((2, D, E), F32), pltpu.VMEM((2, E, D), F32),
                        pltpu.VMEM((D, E), BF16), pltpu.VMEM((D, E), BF16), pltpu.VMEM((E, D), BF16),
                        pltpu.SemaphoreType.DMA((2, 3))],
    )
    return pl.pallas_call(
        _expert_kernel,
        grid_spec=grid_spec,
        out_shape=jax.ShapeDtypeStruct((P, DP), PACKED),
        input_output_aliases={5: 0},
        compiler_params=_cparams(("arbitrary",)),
        name="moe_experts",
    )(block_e, n_used, first, nxt, slot, xb, w_gate, w_up, w_down)


def _final_kernel(tab_ref, tabn_ref, x1_ref, route_ref, p_ref, wpl_ref, plg_ref, pgg_ref, wgate_ref, yb_ref,
                  o_ref, stage_ref, sem):
    tps, ns = stage_ref.shape[1], stage_ref.shape[2]
    tm = x1_ref.shape[0] // tps
    tiles = [slice(t * tm, (t + 1) * tm) for t in range(tps)]
    i = pl.program_id(0)
    cur = i % 2

    def gather(tab, slot, op):
        for t in range(tps):
            def make_copy(src, dst, size, t=t):
                return pltpu.make_async_copy(yb_ref.at[pl.ds(dst, size)],
                                             stage_ref.at[slot, t, pl.ds(src, size)], sem.at[slot])
            _run_copies(tab, t, make_copy, op)

    def fetch(tab, slot):
        stage_ref[slot] = jnp.zeros(stage_ref.shape[1:], stage_ref.dtype)
        gather(tab, slot, "start")

    @pl.when(i == 0)
    def _():
        fetch(tab_ref, 0)

    @pl.when(i + 1 < pl.num_programs(0))
    def _():
        fetch(tabn_ref, 1 - cur)

    pe = [_rms(jnp.dot(p_ref[r, :].astype(BF16), wpl_ref[...], preferred_element_type=F32),
               plg_ref[...], NORM_EPS) for r in tiles]
    gather(tab_ref, cur, "wait")

    scol = lax.broadcasted_iota(jnp.int32, (tm, ns), 1).astype(F32)

    def selection(r):
        route = route_ref[r, :]
        wsel = (jnp.where(scol == route[:, 0:1], route[:, 2:3], 0.0)
                + jnp.where(scol == route[:, 1:2], route[:, 3:4], 0.0))
        w_hi = wsel.astype(BF16)
        return w_hi, (wsel - w_hi.astype(F32)).astype(BF16)

    wsel = [selection(r) for r in tiles]
    yb16 = [_unpack16(stage_ref[cur, t]).astype(BF16) for t in range(tps)]
    moe = [jnp.dot(w_hi, y, preferred_element_type=F32) + jnp.dot(w_lo, y, preferred_element_type=F32)
           for (w_hi, w_lo), y in zip(wsel, yb16)]
    x2 = [x1_ref[r, :] + m for r, m in zip(tiles, moe)]
    gate = [jax.nn.sigmoid(_dot(_rms(v, pgg_ref[...], NORM_EPS), wgate_ref[...])) for v in x2]
    for r, v, e, g in zip(tiles, x2, pe, gate):
        o_ref[r, :] = v + e * g


def _final(tab, x1, route, p2, w_pl, pl_g, pl_gate_g, w_gate, yb):
    T, D = x1.shape
    tm = min(PROJ_ROWS, T)
    nt = T // tm
    tps = TILES_PER_STEP if nt % TILES_PER_STEP == 0 else 1
    tr = tps * tm
    row = lambda i: (i, 0)
    full = lambda i: (0, 0)

    def fs(a):
        return pl.BlockSpec(a.shape, full)

    return pl.pallas_call(
        _final_kernel,
        grid=(nt // tps,),
        in_specs=[pl.BlockSpec((tps, 1, tab.shape[-1]), lambda i: (i, 0, 0), memory_space=pltpu.SMEM),
                  pl.BlockSpec((tps, 1, tab.shape[-1]), lambda i: (jnp.minimum(i + 1, nt // tps - 1), 0, 0),
                               memory_space=pltpu.SMEM),
                  pl.BlockSpec((tr, D), row), pl.BlockSpec((tr, ROUTE_COLS), row),
                  pl.BlockSpec((tr, PL_DIM), row), fs(w_pl), fs(pl_g), fs(pl_gate_g), fs(w_gate),
                  pl.BlockSpec(memory_space=pl.ANY)],
        out_specs=pl.BlockSpec((tr, D), row),
        out_shape=jax.ShapeDtypeStruct((T, D), F32),
        scratch_shapes=[pltpu.VMEM((2, tps, _stage_rows(tm), D // 2), PACKED), pltpu.SemaphoreType.DMA((2,))],
        compiler_params=_cparams(("arbitrary",)),
        name="combine_final",
    )(tab, tab, x1, route, p2, w_pl, pl_g, pl_gate_g, w_gate, yb)


def kernel(x, p, positions, attn_norm_g, w_in, q_norm_g, k_norm_g, lambda_q1, lambda_k1, lambda_q2, lambda_k2, diff_out_g, mu_rkv, mu_wag, w0, w_lora_a, w_lora_b, a0, a_lora_a, a_lora_b, g_lora_a, g_lora_b, k_k, k_a, r_k, gn_w, gn_b, w_out, moe_norm_g, w_group, w_expert_router, w_gate, w_up, w_down, w_pl, pl_norm_g, pl_gate_norm_g, w_pl_gate):
    B, S, D = x.shape
    T = B * S
    assert p.shape[0] == 1, "one layer"
    x2 = x.reshape(T, D)
    p2 = p[0].reshape(T, PL_DIM)
    pos = positions.astype(F32).reshape(T, 1)

    def row(a):
        return a.reshape(1, -1).astype(F32)

    half = DIFF_HEAD_DIM // 2
    inv_freq = ROPE_THETA ** (-jnp.arange(half, dtype=F32) / half)
    invf = jnp.tile(inv_freq, V7X_LANES // half).reshape(1, V7X_LANES)
    reps = DIFF_WIDTH // DIFF_HEAD_DIM

    q, k, v, zr, zk, zv, wp, ap, gg = _in_proj(
        x2, pos, row(attn_norm_g[0]), w_in[0].astype(BF16),
        w_lora_a[0].astype(BF16), a_lora_a[0].astype(BF16), g_lora_a[0].astype(BF16),
        w_lora_b[0].astype(BF16), a_lora_b[0].astype(BF16), g_lora_b[0].astype(BF16),
        mu_wag[0].astype(F32), row(jnp.tile(q_norm_g[0], reps)), row(jnp.tile(k_norm_g[0], reps)),
        invf, B, S)

    o_diff = _attention(q, k, v, row(lambda_q1[0]), row(lambda_k1[0]), row(lambda_q2[0]),
                        row(lambda_k2[0]), diff_out_g[0].astype(F32).reshape(-1, 1), B, S)
    o_rwkv = _rwkv(zr, zk, zv, wp, ap, gg, mu_rkv[0].astype(F32), row(w0[0]), row(a0[0]),
                   row(k_k[0]), row(k_a[0]), row(r_k[0]), row(gn_w[0]), row(gn_b[0]), B, S)

    wr = jnp.concatenate([w_group[0], jnp.transpose(w_expert_router[0], (1, 0, 2)).reshape(D, N_EXPERTS)],
                         axis=1).astype(F32)
    wr = jnp.pad(wr, ((0, 0), (0, V7X_LANES - wr.shape[1])))
    wr_hi = wr.astype(BF16)
    wr_lo = (wr - wr_hi.astype(F32)).astype(BF16)
    x1, hm, route, cnt, sl_rows = _out_router(x2, o_diff, o_rwkv, w_out[0].astype(BF16), row(moe_norm_g[0]),
                                     wr_hi, wr_lo)

    bm = MOE_ROWS
    tm = min(PROJ_ROWS, T)
    nt = T // tm
    i32 = jnp.int32
    cnt_te = cnt.reshape(nt, V7X_SUBLANES, V7X_LANES)[:, 0, ROUTER_EXPERT_LANE:ROUTER_EXPERT_LANE + N_EXPERTS]
    len_te = (cnt_te.astype(i32) + RUN_ALIGN - 1) // RUN_ALIGN * RUN_ALIGN
    tot_e = jnp.sum(len_te, axis=0)
    pcounts = (tot_e + bm - 1) // bm * bm
    pends = jnp.cumsum(pcounts)
    pstarts = pends - pcounts
    dst_te = pstarts[None, :] + jnp.cumsum(len_te, axis=0) - len_te
    src_te = jnp.cumsum(len_te, axis=1) - len_te
    tab = _run_table(src_te, dst_te, len_te, 0)
    nb = -(-(2 * T + nt * N_EXPERTS * (RUN_ALIGN - 1)) // bm) + N_EXPERTS
    P = nb * bm
    n_used = (pends[-1] // bm).astype(i32).reshape(1)
    block_start = jnp.arange(nb, dtype=i32) * bm
    block_e = jnp.minimum(jnp.sum((pends[None, :] <= block_start[:, None]).astype(i32), axis=1),
                          N_EXPERTS - 1)
    tail = _run_table(jnp.zeros((1, N_EXPERTS), i32), (pstarts + tot_e)[None, :], (pcounts - tot_e)[None, :],
                      n_used)
    xb = _dispatch(tab, tail, sl_rows, hm, P)
    blk = jnp.arange(nb, dtype=i32)
    first = ((blk < n_used[0]) & ((blk == 0) | (block_e != jnp.roll(block_e, 1)))).astype(i32)
    slot = (jnp.cumsum(first) - 1) % 2
    eidx = jnp.arange(N_EXPERTS, dtype=i32)
    later = jnp.where((pcounts > 0)[None, :] & (eidx[None, :] > eidx[:, None]), eidx[None, :], N_EXPERTS)
    nxt_e = jnp.min(later, axis=1)
    nxt = jnp.where(nxt_e == N_EXPERTS, -1, nxt_e)[block_e]
    yb = _experts(block_e, n_used, first, nxt, slot.astype(i32), xb, w_gate[0], w_up[0], w_down[0])
    out = _final(tab, x1, route, p2, w_pl[0].astype(BF16), row(pl_norm_g[0]), row(pl_gate_norm_g[0]),
                 w_pl_gate[0].astype(BF16), yb)
    return out.reshape(B, S, D)
```

```python
import math

import jax
import jax.numpy as jnp
from jax import lax
from jax.experimental import pallas as pl
from jax.experimental.pallas import tpu as pltpu

F32 = jnp.float32
BF16 = jnp.bfloat16

D_MODEL = 1024
PL_DIM = 256
DIFF_WIDTH = 512
RWKV_WIDTH = 512
DIFF_HEAD_DIM = 64
DIFF_HEADS = 4
RWKV_HEAD = 64
ROPE_THETA = 10000.0
NORM_EPS = 1e-6
SUBLN_EPS = 1e-5
GN_EPS = 64e-5
N_GROUPS = 4
EXPERTS_PER_GROUP = 8
N_EXPERTS = 32
D_EXPERT = 512
LAM_INIT = 0.8 - 0.6 * math.exp(0.0)

V7X_LANES = 128
V7X_SUBLANES = 8
V7X_VMEM_BYTES = 64 * 1024 * 1024

IN_PROJ_ROWS = 256
IN_PROJ_TILES_PER_STEP = 2
PROJ_ROWS = 256
TILES_PER_STEP = 4
ATTN_ROWS = 256
ATTN_UNROLL = 4
ATTN_HEADS_PER_STEP = 2
RWKV_CHUNK = 64
RWKV_ROWS = 256
MOE_ROWS = 256
VMEM_LIMIT = V7X_VMEM_BYTES - 8 * 1024 * 1024


def _cparams(sem):
    return pltpu.CompilerParams(dimension_semantics=sem, vmem_limit_bytes=VMEM_LIMIT)


def _dot(a, b):
    return jnp.dot(a.astype(BF16), b.astype(BF16), preferred_element_type=F32)


def _dot_nt(a, b):
    return lax.dot_general(a.astype(BF16), b.astype(BF16), (((1,), (1,)), ((), ())),
                           preferred_element_type=F32)


def _split3(x):
    hi = x.astype(BF16)
    r1 = x - hi.astype(F32)
    mid = r1.astype(BF16)
    lo = (r1 - mid.astype(F32)).astype(BF16)
    return hi, mid, lo


def _dot_exact_lhs(m01, x):
    hi, mid, lo = _split3(x)
    return (jnp.dot(m01, hi, preferred_element_type=F32)
            + jnp.dot(m01, mid, preferred_element_type=F32)
            + jnp.dot(m01, lo, preferred_element_type=F32))


PACKED = jnp.uint32


def _pack16(x):
    h = x.shape[1] // 2
    lo = lax.bitcast_convert_type(x[:, :h], PACKED) >> 16
    hi = lax.bitcast_convert_type(x[:, h:], PACKED) & jnp.uint32(0xFFFF0000)
    return hi | lo


def _unpack16(w):
    lo = lax.bitcast_convert_type(w << 16, F32)
    hi = lax.bitcast_convert_type(w & jnp.uint32(0xFFFF0000), F32)
    return jnp.concatenate([lo, hi], axis=1)


def _sumsq_64(z):
    m, n = z.shape
    group = V7X_LANES // 2
    first = lax.broadcasted_iota(jnp.int32, (m, V7X_LANES), 1) < group
    out = []
    for c in range(0, n, V7X_LANES):
        sq = z[:, c:c + V7X_LANES] * z[:, c:c + V7X_LANES]
        s0 = jnp.sum(jnp.where(first, sq, 0.0), axis=-1, keepdims=True)
        s1 = jnp.sum(jnp.where(first, 0.0, sq), axis=-1, keepdims=True)
        out.append(jnp.where(first, s0, s1))
    return jnp.concatenate(out, axis=1)


def _rms(x, g, eps):
    return x * lax.rsqrt(jnp.mean(x * x, axis=-1, keepdims=True) + eps) * g


def _shift_rows(z, prev_row):
    rolled = pltpu.roll(z, 1, axis=0)
    row = lax.broadcasted_iota(jnp.int32, z.shape, 0)
    return jnp.where(row == 0, prev_row, rolled)


def _in_proj_kernel(x_ref, pos_ref, g_ref, win_ref, w1_ref, a1_ref, g1_ref, w2_ref, a2_ref, g2_ref,
                    mu_ref, qg_ref, kg_ref, invf_ref,
                    q_out, k_out, v_out, zr_out, zk_out, zv_out, wp_out, ap_out, gg_out,
                    carry_ref):
    tm = IN_PROJ_ROWS if x_ref.shape[0] >= IN_PROJ_ROWS else x_ref.shape[0]
    tiles = [slice(t * tm, (t + 1) * tm) for t in range(x_ref.shape[0] // tm)]
    w = DIFF_WIDTH
    half = DIFF_HEAD_DIM // 2

    @pl.when(pl.program_id(1) == 0)
    def _():
        carry_ref[...] = jnp.zeros_like(carry_ref)

    prev = carry_ref[V7X_SUBLANES - 1:V7X_SUBLANES, :]
    hns, dhs = [], []
    for r in tiles:
        hn = _rms(x_ref[r, :], g_ref[...], NORM_EPS)
        dhs.append(_shift_rows(hn, prev) - hn)
        hns.append(hn)
        prev = hn[tm - 1:tm, :]
    carry_ref[...] = hns[-1][tm - V7X_SUBLANES:tm, :]

    zq, zk, low = [], [], []
    for r, hn, dh in zip(tiles, hns, dhs):
        hb = hn.astype(BF16)

        def proj(c, hb=hb):
            return jnp.dot(hb, win_ref[:, c * w:(c + 1) * w], preferred_element_type=F32)

        zq.append(proj(0))
        zk.append(proj(1))
        low.append((_dot(hn + dh * mu_ref[0:1, :], w1_ref[...]), _dot(hn + dh * mu_ref[1:2, :], a1_ref[...]),
                    _dot(hn + dh * mu_ref[2:3, :], g1_ref[...])))
        v_out[r, :] = proj(2).astype(v_out.dtype)
        zr_out[r, :] = proj(3)
        zk_out[r, :] = proj(4)
        zv_out[r, :] = proj(5)

    lane = lax.broadcasted_iota(jnp.int32, (tm, w), 1)
    first_half = (lane % DIFF_HEAD_DIM) < half
    scale = DIFF_HEAD_DIM ** -0.5 * math.log2(math.e)
    for r, q, k, (lw, la, lg) in zip(tiles, zq, zk, low):
        ang = pos_ref[r, :] * invf_ref[...]
        cosf = jnp.concatenate([jnp.cos(ang)] * (w // V7X_LANES), axis=1)
        sinf = jnp.concatenate([jnp.sin(ang)] * (w // V7X_LANES), axis=1)
        sin_signed = jnp.where(first_half, -sinf, sinf)

        def norm_rope(z, gain, cosf=cosf, sin_signed=sin_signed):
            zn = z * lax.rsqrt(_sumsq_64(z) * (1.0 / DIFF_HEAD_DIM) + NORM_EPS) * gain
            partner = jnp.where(first_half, pltpu.roll(zn, w - half, axis=1), pltpu.roll(zn, half, axis=1))
            return zn * cosf + partner * sin_signed

        q_out[r, :] = (norm_rope(q, qg_ref[...]) * scale).astype(q_out.dtype)
        k_out[r, :] = norm_rope(k, kg_ref[...]).astype(k_out.dtype)
        wp_out[r, :] = _dot(jnp.tanh(lw), w2_ref[...])
        ap_out[r, :] = _dot(la, a2_ref[...])
        gg_out[r, :] = _dot(jax.nn.sigmoid(lg), g2_ref[...])


def _in_proj(x2, pos, attn_g, w_in, w1, a1, g1, w2, a2, g2, mu_wag, qg, kg, invf, B, S):
    T, D = x2.shape
    tile = min(IN_PROJ_ROWS, S)
    tps = IN_PROJ_TILES_PER_STEP if (S // tile) % IN_PROJ_TILES_PER_STEP == 0 else 1
    tm = tile * tps
    ns = S // tm
    w = DIFF_WIDTH
    row = lambda b, i: (b * ns + i, 0)
    full = lambda b, i: (0, 0)

    def fs(a):
        return pl.BlockSpec(a.shape, full)

    outs = ([jax.ShapeDtypeStruct((T, w), BF16)] * 3 + [jax.ShapeDtypeStruct((T, w), F32)] * 6)
    return pl.pallas_call(
        _in_proj_kernel,
        grid=(B, ns),
        in_specs=[pl.BlockSpec((tm, D), row), pl.BlockSpec((tm, 1), row), fs(attn_g), fs(w_in),
                  fs(w1), fs(a1), fs(g1), fs(w2), fs(a2), fs(g2), fs(mu_wag), fs(qg), fs(kg),
                  fs(invf)],
        out_specs=[pl.BlockSpec((tm, w), row)] * 9,
        out_shape=outs,
        scratch_shapes=[pltpu.VMEM((V7X_SUBLANES, D), F32)],
        compiler_params=_cparams(("arbitrary", "arbitrary")),
        name="in_proj",
    )(x2, pos, attn_g, w_in, w1, a1, g1, w2, a2, g2, mu_wag, qg, kg, invf)


def _attn_kernel(lq1_ref, lk1_ref, lq2_ref, lk2_ref, ogt_ref, q_ref, k_ref, v_ref, o_ref,
                 vt_ref, s_ref, acc_ref):
    nh, nq, hw, tq = vt_ref.shape
    tk = tq
    heads = range(nh)
    lam = (jnp.exp(jnp.sum(lq1_ref[...] * lk1_ref[...], axis=-1, keepdims=True))
           - jnp.exp(jnp.sum(lq2_ref[...] * lk2_ref[...], axis=-1, keepdims=True)) + LAM_INIT)

    for h in heads:
        for c in range(nq):
            vt_ref[h, c] = (v_ref[c * tk:(c + 1) * tk, h * hw:(h + 1) * hw]
                            .astype(F32).T.astype(vt_ref.dtype))

    lane = lax.broadcasted_iota(jnp.int32, (tq, hw), 1)

    def stacked_queries(t):
        out = []
        for h in heads:
            q = q_ref[t * tq:(t + 1) * tq, h * hw:(h + 1) * hw]
            zero = jnp.zeros_like(q)
            out.append(jnp.concatenate([jnp.where(lane < DIFF_HEAD_DIM, q, zero),
                                        jnp.where(lane >= DIFF_HEAD_DIM, q, zero)], axis=0))
        return out

    def scores(j, qs):
        start = pl.multiple_of(j * tk, tk)
        return [lax.dot_general(k_ref[pl.ds(start, tk), h * hw:(h + 1) * hw], qs[h],
                                (((1,), (1,)), ((), ())), preferred_element_type=F32) for h in heads]

    def colmax(m, st):
        return tuple(jnp.maximum(mh, jnp.max(s, axis=0, keepdims=True)) for mh, s in zip(m, st))

    krow = lax.broadcasted_iota(jnp.int32, (tk, 2 * tq), 0)
    qcol = lax.broadcasted_iota(jnp.int32, (tk, 2 * tq), 1)
    causal = krow <= jnp.where(qcol >= tq, qcol - tq, qcol)
    ninf = tuple(jnp.full((1, 2 * tq), -jnp.inf, F32) for _ in heads)
    zeros = tuple(jnp.zeros((1, 2 * tq), F32) for _ in heads)

    def diagonal(t, qs, m):
        st = [jnp.where(causal, s, -jnp.inf) for s in scores(t, qs)]
        for h in heads:
            s_ref[t % 2, h, t] = st[h]
        return colmax(m, st)

    m_cur = diagonal(0, stacked_queries(0), ninf)
    for t in range(nq):
        slot, nslot = t % 2, (t + 1) % 2
        has_next = t + 1 < nq
        qs_next = stacked_queries(t + 1) if has_next else None
        acc_ref[...] = jnp.zeros_like(acc_ref)

        def body(j, carry, slot=slot, nslot=nslot, has_next=has_next, qs_next=qs_next, m_cur=m_cur):
            l, m_next = carry
            pt = [jnp.exp2(s_ref[slot, h, j] - m_cur[h]) for h in heads]
            pv = [jnp.dot(vt_ref[h, j], pt[h].astype(BF16), preferred_element_type=F32) for h in heads]
            for h in heads:
                acc_ref[h] += pv[h]
            l = tuple(lh + jnp.sum(p, axis=0, keepdims=True) for lh, p in zip(l, pt))
            if has_next:
                st = scores(j, qs_next)
                for h in heads:
                    s_ref[nslot, h, j] = st[h]
                m_next = colmax(m_next, st)
            return l, m_next

        lsum, m_next = lax.fori_loop(0, t + 1, body, (zeros, ninf), unroll=ATTN_UNROLL)
        if has_next:
            m_next = diagonal(t + 1, qs_next, m_next)

        for h in heads:
            acc = acc_ref[h]
            l = lsum[h]
            ot = acc[:, :tq] / l[:, :tq] - lam * (acc[:, tq:] / l[:, tq:])
            ot = ot * lax.rsqrt(jnp.mean(ot * ot, axis=0, keepdims=True) + SUBLN_EPS) * ogt_ref[...]
            o_ref[t * tq:(t + 1) * tq, h * hw:(h + 1) * hw] = (ot * (1.0 - LAM_INIT)).T.astype(o_ref.dtype)
        m_cur = m_next


def _attention(q, k, v, lq1, lk1, lq2, lk2, og, B, S):
    T = q.shape[0]
    tq = min(ATTN_ROWS, S)
    nq = S // tq
    hw = 2 * DIFF_HEAD_DIM
    nh = ATTN_HEADS_PER_STEP
    gw = nh * hw
    small = lambda b, h: (0, 0)
    seq = pl.BlockSpec((S, gw), lambda b, h: (b, h))
    return pl.pallas_call(
        _attn_kernel,
        grid=(B, DIFF_HEADS // nh),
        in_specs=[pl.BlockSpec(lq1.shape, small), pl.BlockSpec(lk1.shape, small),
                  pl.BlockSpec(lq2.shape, small), pl.BlockSpec(lk2.shape, small),
                  pl.BlockSpec(og.shape, small), seq, seq, seq],
        out_specs=seq,
        out_shape=jax.ShapeDtypeStruct((T, DIFF_WIDTH), BF16),
        scratch_shapes=[pltpu.VMEM((nh, nq, hw, tq), BF16), pltpu.VMEM((2, nh, nq, tq, 2 * tq), F32),
                        pltpu.VMEM((nh, hw, 2 * tq), F32)],
        compiler_params=_cparams(("arbitrary", "arbitrary")),
        name="diff_attn",
    )(lq1, lk1, lq2, lk2, og, q, k, v)


def _rwkv_kernel(zr_ref, zk_ref, zv_ref, wp_ref, ap_ref, gg_ref, mu_ref, w0_ref, a0_ref, kk_ref,
                 ka_ref, rk_ref, gnw_ref, gnb_ref, o_ref, state_ref, carry_ref):
    R = zr_ref.shape[0]
    C = min(RWKV_CHUNK, R)
    nch = R // C
    N = RWKV_HEAD
    W = RWKV_WIDTH
    HP = 2 * N
    n_pairs = W // HP

    @pl.when(pl.program_id(1) == 0)
    def _():
        state_ref[...] = jnp.zeros_like(state_ref)
        carry_ref[...] = jnp.zeros_like(carry_ref)

    zr, zk, zv = zr_ref[...], zk_ref[...], zv_ref[...]
    last = V7X_SUBLANES - 1
    r = zr + (_shift_rows(zr, carry_ref[last:last + 1, 0:W]) - zr) * mu_ref[0:1, :]
    k = zk + (_shift_rows(zk, carry_ref[last:last + 1, W:2 * W]) - zk) * mu_ref[1:2, :]
    v = zv + (_shift_rows(zv, carry_ref[last:last + 1, 2 * W:3 * W]) - zv) * mu_ref[2:3, :]
    carry_ref[:, 0:W] = zr[R - V7X_SUBLANES:R, :]
    carry_ref[:, W:2 * W] = zk[R - V7X_SUBLANES:R, :]
    carry_ref[:, 2 * W:3 * W] = zv[R - V7X_SUBLANES:R, :]

    lw = -math.exp(-0.5) * jax.nn.sigmoid(w0_ref[...] + wp_ref[...])
    a = jax.nn.sigmoid(a0_ref[...] + ap_ref[...])
    kk = k * kk_ref[...]
    kk = kk * lax.rsqrt(jnp.maximum(_sumsq_64(kk), 1e-24))
    k2 = k * (1.0 + (a - 1.0) * ka_ref[...])
    a_s = -kk
    b_s = kk * a

    rr = lax.broadcasted_iota(jnp.int32, (R, R), 0)
    cc = lax.broadcasted_iota(jnp.int32, (R, R), 1)
    same_chunk = (rr // C) == (cc // C)
    L = _dot_exact_lhs((same_chunk & (cc <= rr)).astype(BF16), lw)
    bonus_w = r * k2 * rk_ref[...]

    P2 = 2 * C
    sr = lax.broadcasted_iota(jnp.int32, (P2, HP), 0)
    sc = lax.broadcasted_iota(jnp.int32, (P2, HP), 1)
    stack_mask = (sr < C) == (sc < N)
    br = lax.broadcasted_iota(jnp.int32, (P2, P2), 0)
    bc = lax.broadcasted_iota(jnp.int32, (P2, P2), 1)
    same_head = (br < C) == (bc < C)
    tr = jnp.where(br >= C, br - C, br)
    tc = jnp.where(bc >= C, bc - C, bc)
    strict = same_head & (tc < tr)
    lower = same_head & (tc <= tr)
    eye_p = (br == bc).astype(F32)
    kr = lax.broadcasted_iota(jnp.int32, (HP, HP), 0)
    kc = lax.broadcasted_iota(jnp.int32, (HP, HP), 1)
    eye_k = kr == kc

    def dup(x):
        return jnp.concatenate([x, x], axis=0)

    def stack(x):
        return jnp.where(stack_mask, dup(x), 0.0)

    n_sq = int(math.log2(C)) - 1
    items = [(c, p) for c in range(nch) for p in range(n_pairs)]

    def prep(c, p):
        rows = slice(c * C, (c + 1) * C)
        lanes = slice(p * HP, (p + 1) * HP)
        Lc = L[rows, lanes]
        lwc = lw[rows, lanes]
        LC = Lc[C - 1:C, :]
        enL = jnp.exp(-Lc)
        eCL = jnp.exp(LC - Lc)
        b_c, k_c = b_s[rows, lanes], k2[rows, lanes]
        return dict(
            xa=stack(a_s[rows, lanes] * jnp.exp(Lc - lwc)), xr=stack(r[rows, lanes] * jnp.exp(Lc)),
            bt=dup(b_c * enL), kt=dup(k_c * enL), bh=stack(b_c * eCL), kh=stack(k_c * eCL),
            vs=stack(v[rows, lanes]), pc=jnp.exp(LC))

    d = [prep(c, p) for c, p in items]
    aa = [_dot_nt(jnp.concatenate([e["xa"], e["xr"]], axis=0),
                  jnp.concatenate([e["bt"], e["kt"]], axis=0)) for e in d]
    a_ab = [jnp.where(strict, m[:P2, :P2], 0.0) for m in aa]
    a_ak = [jnp.where(strict, m[:P2, P2:], 0.0) for m in aa]
    a_rb = [jnp.where(lower, m[P2:, :P2], 0.0) for m in aa]
    a_rk = [jnp.where(lower, m[P2:, P2:], 0.0) for m in aa]
    tm = [eye_p + m for m in a_ab]
    npw = a_ab
    for _ in range(n_sq):
        npw = [_dot(m, m) for m in npw]
        tm = [t + _dot(t, m) for t, m in zip(tm, npw)]
    av = [_dot(m, e["vs"]) for m, e in zip(a_ak, d)]
    z = [_dot(t, jnp.concatenate([e["xa"], x], axis=1)) for t, e, x in zip(tm, d, av)]
    zero_v = jnp.zeros((P2, HP), F32)
    rhs = [jnp.concatenate([x, jnp.concatenate([zero_v, e["vs"]], axis=1)], axis=0) for e, x in zip(d, z)]
    lhs = [jnp.concatenate([jnp.concatenate([m1, m2], axis=1),
                            jnp.concatenate([e["bh"], e["kh"]], axis=0).T], axis=0)
           for m1, m2, e in zip(a_rb, a_rk, d)]
    wg = [_dot(a, b) for a, b in zip(lhs, rhs)]

    for idx, (c, p) in enumerate(items):
        e = d[idx]
        rows = slice(c * C, (c + 1) * C)
        lanes = slice(p * HP, (p + 1) * HP)
        rp = e["xr"] + wg[idx][:P2, :HP]
        y0 = wg[idx][:P2, HP:]
        gm = wg[idx][P2:, :HP]
        hm = wg[idx][P2:, HP:]
        st = state_ref[p]
        yg = _dot(jnp.concatenate([rp, gm], axis=0), st)
        pc_col = jnp.sum(jnp.where(eye_k, e["pc"], 0.0), axis=1, keepdims=True)
        state_ref[p] = pc_col * st + yg[P2:] + hm
        ys = yg[:P2] + y0
        mu = jnp.sum(ys, axis=-1, keepdims=True) * (1.0 / N)
        yc = jnp.where(stack_mask, ys - mu, 0.0)
        var = jnp.sum(yc * yc, axis=-1, keepdims=True) * (1.0 / N)
        yn = yc * lax.rsqrt(var + GN_EPS)
        bonus = jnp.sum(stack(bonus_w[rows, lanes]), axis=-1, keepdims=True) * e["vs"]
        yn = yn[:C] + yn[C:]
        bonus = bonus[:C] + bonus[C:]
        o_ref[rows, lanes] = ((yn * gnw_ref[:, lanes] + gnb_ref[:, lanes] + bonus)
                              * gg_ref[rows, lanes]).astype(o_ref.dtype)


def _rwkv(zr, zk, zv, wp, ap, gg, mu_rkv, w0, a0, k_k, k_a, r_k, gn_w, gn_b, B, S):
    T = zr.shape[0]
    C = min(RWKV_ROWS, S)
    nc = S // C
    W = RWKV_WIDTH
    HP = 2 * RWKV_HEAD
    row = lambda b, i: (b * nc + i, 0)
    full = lambda b, i: (0, 0)

    def fs(a):
        return pl.BlockSpec(a.shape, full)

    return pl.pallas_call(
        _rwkv_kernel,
        grid=(B, nc),
        in_specs=[pl.BlockSpec((C, W), row)] * 6 + [fs(mu_rkv), fs(w0), fs(a0), fs(k_k), fs(k_a),
                                                    fs(r_k), fs(gn_w), fs(gn_b)],
        out_specs=pl.BlockSpec((C, W), row),
        out_shape=jax.ShapeDtypeStruct((T, W), BF16),
        scratch_shapes=[pltpu.VMEM((W // HP, HP, HP), F32),
                        pltpu.VMEM((V7X_SUBLANES, 3 * W), F32)],
        compiler_params=_cparams(("arbitrary", "arbitrary")),
        name="rwkv7",
    )(zr, zk, zv, wp, ap, gg, mu_rkv, w0, a0, k_k, k_a, r_k, gn_w, gn_b)


ROUTE_COLS = 8
ROUTER_EXPERT_LANE = N_GROUPS


def _out_router_kernel(x_ref, od_ref, orw_ref, wo_ref, mg_ref, wrh_ref, wrl_ref,
                       x1_out, hm_out, route_out, cnt_out, sl_out):
    tm = PROJ_ROWS if x_ref.shape[0] >= PROJ_ROWS else x_ref.shape[0]
    tiles = [slice(t * tm, (t + 1) * tm) for t in range(x_ref.shape[0] // tm)]
    x1 = [x_ref[r, :] + jnp.dot(od_ref[r, :], wo_ref[0:DIFF_WIDTH, :], preferred_element_type=F32)
          + jnp.dot(orw_ref[r, :], wo_ref[DIFF_WIDTH:, :], preferred_element_type=F32) for r in tiles]
    hm = [_rms(v, mg_ref[...], NORM_EPS) for v in x1]
    for r, v, h in zip(tiles, x1, hm):
        x1_out[r, :] = v
        hm_out[r, :] = h.astype(hm_out.dtype)

    def logits(h):
        hi = h.astype(BF16)
        lo = (h - hi.astype(F32)).astype(BF16)
        return (jnp.dot(hi, wrh_ref[...], preferred_element_type=F32)
                + jnp.dot(hi, wrl_ref[...], preferred_element_type=F32)
                + jnp.dot(lo, wrh_ref[...], preferred_element_type=F32))

    lgs = [logits(h) for h in hm]
    lane = lax.broadcasted_iota(jnp.int32, (tm, V7X_LANES), 1).astype(F32)
    big = float(V7X_LANES)
    ninf = -jnp.inf
    gmask = lane < N_GROUPS

    def choose(lg):
        gmax = jnp.max(jnp.where(gmask, lg, ninf), axis=-1, keepdims=True)
        g_sel = jnp.min(jnp.where(gmask & (lg == gmax), lane, big), axis=-1, keepdims=True)
        g_w = 1.0 / jnp.sum(jnp.where(gmask, jnp.exp(lg - gmax), 0.0), axis=-1, keepdims=True)
        lo_lane = ROUTER_EXPERT_LANE + g_sel * EXPERTS_PER_GROUP
        emask = (lane >= lo_lane) & (lane < lo_lane + EXPERTS_PER_GROUP)
        v1 = jnp.max(jnp.where(emask, lg, ninf), axis=-1, keepdims=True)
        i1 = jnp.min(jnp.where(emask & (lg == v1), lane, big), axis=-1, keepdims=True)
        emask2 = emask & (lane != i1)
        v2 = jnp.max(jnp.where(emask2, lg, ninf), axis=-1, keepdims=True)
        i2 = jnp.min(jnp.where(emask2 & (lg == v2), lane, big), axis=-1, keepdims=True)
        e2x = jnp.exp(v2 - v1)
        den = 1.0 + e2x
        return i1, i2, (1.0 / den) * g_w, (e2x / den) * g_w

    chosen = [choose(lg) for lg in lgs]

    ri = lax.broadcasted_iota(jnp.int32, (tm, tm), 0)
    ci = lax.broadcasted_iota(jnp.int32, (tm, tm), 1)
    before = (ci < ri).astype(BF16)
    li = lax.broadcasted_iota(jnp.int32, (V7X_LANES, V7X_LANES), 0)
    lj = lax.broadcasted_iota(jnp.int32, (V7X_LANES, V7X_LANES), 1)
    lanes_before = (li < lj).astype(BF16)
    ohs = [((lane == i1) | (lane == i2)).astype(F32) for i1, i2, _, _ in chosen]
    prefixes = [jnp.dot(before, oh.astype(BF16), preferred_element_type=F32) for oh in ohs]
    counts = [jnp.broadcast_to(jnp.sum(oh, axis=0, keepdims=True), (V7X_SUBLANES, V7X_LANES)) for oh in ohs]
    padded = [jnp.floor((c + (RUN_ALIGN - 1)) * (1.0 / RUN_ALIGN)) * RUN_ALIGN for c in counts]
    starts = [jnp.dot(pc.astype(BF16), lanes_before, preferred_element_type=F32) for pc in padded]
    col = lax.broadcasted_iota(jnp.int32, (tm, V7X_LANES), 1)
    for t, (r, (i1, i2, wt1, wt2), cnt, start, prefix) in enumerate(zip(tiles, chosen, counts, starts, prefixes)):
        row = prefix + start[0:1, :]
        sl1 = jnp.sum(jnp.where(lane == i1, row, 0.0), axis=-1, keepdims=True)
        sl2 = jnp.sum(jnp.where(lane == i2, row, 0.0), axis=-1, keepdims=True)
        cnt_out[t * V7X_SUBLANES:(t + 1) * V7X_SUBLANES, :] = cnt
        wide = jnp.where(col == 0, sl1, jnp.where(col == 1, sl2, jnp.where(col == 2, wt1, jnp.where(
            col == 3, wt2, 0.0))))
        route_out[r, :] = wide[:, 0:ROUTE_COLS]
        sl_out[t] = wide.T[0:V7X_SUBLANES, :].astype(sl_out.dtype)


def _out_router(x2, od, orw, w_out, moe_g, wr_hi, wr_lo):
    T, D = x2.shape
    tm = min(PROJ_ROWS, T)
    nt = T // tm
    tps = TILES_PER_STEP if nt % TILES_PER_STEP == 0 else 1
    tr = tm * tps
    row = lambda i: (i, 0)
    full = lambda i: (0, 0)

    def fs(a):
        return pl.BlockSpec(a.shape, full)

    return pl.pallas_call(
        _out_router_kernel,
        grid=(nt // tps,),
        in_specs=[pl.BlockSpec((tr, D), row), pl.BlockSpec((tr, DIFF_WIDTH), row),
                  pl.BlockSpec((tr, RWKV_WIDTH), row), fs(w_out), fs(moe_g), fs(wr_hi), fs(wr_lo)],
        out_specs=[pl.BlockSpec((tr, D), row), pl.BlockSpec((tr, D), row),
                   pl.BlockSpec((tr, ROUTE_COLS), row), pl.BlockSpec((tps * V7X_SUBLANES, V7X_LANES), row),
                   pl.BlockSpec((tps, V7X_SUBLANES, tm), lambda i: (i, 0, 0))],
        out_shape=[jax.ShapeDtypeStruct((T, D), F32), jax.ShapeDtypeStruct((T, D), BF16),
                   jax.ShapeDtypeStruct((T, ROUTE_COLS), F32),
                   jax.ShapeDtypeStruct((nt * V7X_SUBLANES, V7X_LANES), F32),
                   jax.ShapeDtypeStruct((nt, V7X_SUBLANES, tm), jnp.int32)],
        compiler_params=_cparams(("arbitrary",)),
        name="out_router",
    )(x2, od, orw, w_out, moe_g, wr_hi, wr_lo)


RUN_ALIGN = V7X_SUBLANES
RUN_CLASSES = 6
TAB_COUNT, TAB_NUSED = 0, RUN_CLASSES
TAB_SRC = V7X_SUBLANES
TAB_DST = TAB_SRC + RUN_CLASSES * N_EXPERTS
TAB_WORDS = 4 * V7X_LANES


def _stage_rows(tm):
    need = 2 * tm + N_EXPERTS * (RUN_ALIGN - 1)
    return -(-need // V7X_LANES) * V7X_LANES


def _run_table(src, dst, length, spare):
    i32 = jnp.int32
    n = src.shape[0]
    cls = jnp.arange(RUN_CLASSES, dtype=i32)[None, :, None]
    ln = length[:, None, :]
    has = ((ln >> 3) >> cls) & 1
    off = ln & ~((2 * RUN_ALIGN << cls) - 1)
    pos = jnp.cumsum(has, axis=-1) - 1
    k = jnp.arange(N_EXPERTS, dtype=i32)
    pick = (has[..., None] == 1) & (pos[..., None] == k)

    def compact(v):
        return jnp.sum(jnp.where(pick, (v[:, None, :] + off)[..., None], 0), axis=2).reshape(n, -1)

    head = jnp.concatenate([jnp.sum(has, axis=-1), jnp.broadcast_to(spare, (n, 1)).astype(i32),
                            jnp.zeros((n, TAB_SRC - RUN_CLASSES - 1), i32)], axis=1)
    tab = jnp.concatenate([head, compact(src), compact(dst)], axis=1)
    return jnp.pad(tab, ((0, 0), (0, TAB_WORDS - tab.shape[1]))).reshape(n, 1, TAB_WORDS)


def _run_copies(tab_ref, tile, make_copy, op):
    for c in range(RUN_CLASSES):
        size = RUN_ALIGN << c

        def one(k, carry, c=c, size=size):
            src = tab_ref[tile, 0, TAB_SRC + c * N_EXPERTS + k]
            dst = tab_ref[tile, 0, TAB_DST + c * N_EXPERTS + k]
            getattr(make_copy(pl.multiple_of(src, RUN_ALIGN), pl.multiple_of(dst, RUN_ALIGN), size), op)()
            return carry

        lax.fori_loop(0, tab_ref[tile, 0, TAB_COUNT + c], one, 0)


def _dispatch_kernel(tab_ref, tail_ref, sl_ref, hm_ref, xb_ref, stage_ref, zero_ref, sem, zsem):
    tps, ns = stage_ref.shape[0], stage_ref.shape[1]
    tm = hm_ref.shape[0] // tps
    bm = zero_ref.shape[0]
    nb = xb_ref.shape[0] // bm
    n_used = tail_ref[0, 0, TAB_NUSED]

    def tile_copy(t):
        def make_copy(src, dst, size):
            return pltpu.make_async_copy(stage_ref.at[t, pl.ds(src, size)], xb_ref.at[pl.ds(dst, size)], sem)
        return make_copy

    def zero_copy(src, dst, size):
        return pltpu.make_async_copy(zero_ref.at[pl.ds(src, size)], xb_ref.at[pl.ds(dst, size)], zsem)

    def zero_blocks(op):
        def one(b, c):
            getattr(zero_copy(0, pl.multiple_of(b * bm, bm), bm), op)()
            return c
        lax.fori_loop(n_used, nb, one, 0)

    @pl.when(pl.program_id(0) == 0)
    def _():
        zero_ref[...] = jnp.zeros_like(zero_ref)
        _run_copies(tail_ref, 0, zero_copy, "start")
        zero_blocks("start")

    srow = lax.broadcasted_iota(jnp.int32, (ns, tm), 0)
    for t in range(tps):
        sel = (srow == sl_ref[t, 0:1, :]) | (srow == sl_ref[t, 1:2, :])
        stage_ref[t] = _pack16(jnp.dot(sel.astype(BF16), hm_ref[t * tm:(t + 1) * tm, :],
                                       preferred_element_type=F32))
        _run_copies(tab_ref, t, tile_copy(t), "start")
    for t in range(tps):
        _run_copies(tab_ref, t, tile_copy(t), "wait")

    @pl.when(pl.program_id(0) == pl.num_programs(0) - 1)
    def _():
        _run_copies(tail_ref, 0, zero_copy, "wait")
        zero_blocks("wait")


def _dispatch(tab, tail, sl_rows, hm, P):
    T, D = hm.shape
    tm = min(PROJ_ROWS, T)
    nt = T // tm
    tps = TILES_PER_STEP if nt % TILES_PER_STEP == 0 else 1
    return pl.pallas_call(
        _dispatch_kernel,
        grid=(nt // tps,),
        in_specs=[pl.BlockSpec((tps, 1, tab.shape[-1]), lambda i: (i, 0, 0), memory_space=pltpu.SMEM),
                  pl.BlockSpec((1, 1, tail.shape[-1]), lambda i: (0, 0, 0), memory_space=pltpu.SMEM),
                  pl.BlockSpec((tps, V7X_SUBLANES, tm), lambda i: (i, 0, 0)),
                  pl.BlockSpec((tps * tm, D), lambda i: (i, 0))],
        out_specs=pl.BlockSpec(memory_space=pl.ANY),
        out_shape=jax.ShapeDtypeStruct((P, D // 2), PACKED),
        scratch_shapes=[pltpu.VMEM((tps, _stage_rows(tm), D // 2), PACKED),
                        pltpu.VMEM((MOE_ROWS, D // 2), PACKED),
                        pltpu.SemaphoreType.DMA(()), pltpu.SemaphoreType.DMA(())],
        compiler_params=_cparams(("arbitrary",)),
        name="moe_dispatch",
    )(tab, tail, sl_rows, hm)


def _expert_kernel(be_ref, nu_ref, first_ref, nxt_ref, slot_ref, xb_ref, wg_hbm, wu_hbm, wd_hbm, yb_ref,
                   wgf, wuf, wdf, wgb, wub, wdb, sem):
    i = pl.program_id(0)

    def weight_copies(e, s):
        return (pltpu.make_async_copy(wg_hbm.at[e], wgf.at[s], sem.at[s, 0]),
                pltpu.make_async_copy(wu_hbm.at[e], wuf.at[s], sem.at[s, 1]),
                pltpu.make_async_copy(wd_hbm.at[e], wdf.at[s], sem.at[s, 2]))

    @pl.when(i == 0)
    def _():
        for cp in weight_copies(be_ref[0], 0):
            cp.start()

    @pl.when(first_ref[i] == 1)
    def _():
        s = slot_ref[i]
        for cp in weight_copies(be_ref[i], s):
            cp.wait()
        wgb[...] = wgf[s].astype(BF16)
        wub[...] = wuf[s].astype(BF16)
        wdb[...] = wdf[s].astype(BF16)

        @pl.when(nxt_ref[i] >= 0)
        def _():
            for cp in weight_copies(nxt_ref[i], 1 - s):
                cp.start()

    @pl.when(i < nu_ref[0])
    def _():
        xb = _unpack16(xb_ref[...]).astype(BF16)
        gate = jnp.dot(xb, wgb[...], preferred_element_type=F32)
        up = jnp.dot(xb, wub[...], preferred_element_type=F32)
        hdn = (gate * jax.nn.sigmoid(gate)) * up
        y = jnp.dot(hdn.astype(BF16), wdb[...], preferred_element_type=F32)
        yb_ref[...] = _pack16(y.astype(BF16).astype(F32))

    @pl.when(i >= nu_ref[0])
    def _():
        yb_ref[...] = jnp.zeros_like(yb_ref)


def _experts(block_e, n_used, first, nxt, slot, xb, w_gate, w_up, w_down):
    P, DP = xb.shape
    D = 2 * DP
    bm = MOE_ROWS
    nb = P // bm
    E = D_EXPERT
    hbm = pl.BlockSpec(memory_space=pl.ANY)
    grid_spec = pltpu.PrefetchScalarGridSpec(
        num_scalar_prefetch=5,
        grid=(nb,),
        in_specs=[pl.BlockSpec((bm, DP), lambda i, be, nu, *_: (jnp.minimum(i, nu[0] - 1), 0)), hbm, hbm, hbm],
        out_specs=pl.BlockSpec((bm, DP), lambda i, be, nu, *_: (jnp.where(i < nu[0], i, nb - 1), 0)),
        scratch_shapes=[pltpu.VMEM((2, D, E), F32), pltpu.VMEM((2, D, E), F32), pltpu.VMEM((2, E, D), F32),
                        pltpu.VMEM((D, E), BF16), pltpu.VMEM((D, E), BF16), pltpu.VMEM((E, D), BF16),
                        pltpu.SemaphoreType.DMA((2, 3))],
    )
    return pl.pallas_call(
        _expert_kernel,
        grid_spec=grid_spec,
        out_shape=jax.ShapeDtypeStruct((P, DP), PACKED),
        input_output_aliases={5: 0},
        compiler_params=_cparams(("arbitrary",)),
        name="moe_experts",
    )(block_e, n_used, first, nxt, slot, xb, w_gate, w_up, w_down)


def _final_kernel(tab_ref, tabn_ref, x1_ref, route_ref, p_ref, wpl_ref, plg_ref, pgg_ref, wgate_ref, yb_ref,
                  o_ref, stage_ref, sem):
    tps, ns = stage_ref.shape[1], stage_ref.shape[2]
    tm = x1_ref.shape[0] // tps
    tiles = [slice(t * tm, (t + 1) * tm) for t in range(tps)]
    i = pl.program_id(0)
    cur = i % 2

    def gather(tab, slot, op):
        for t in range(tps):
            def make_copy(src, dst, size, t=t):
                return pltpu.make_async_copy(yb_ref.at[pl.ds(dst, size)],
                                             stage_ref.at[slot, t, pl.ds(src, size)], sem.at[slot])
            _run_copies(tab, t, make_copy, op)

    def fetch(tab, slot):
        stage_ref[slot] = jnp.zeros(stage_ref.shape[1:], stage_ref.dtype)
        gather(tab, slot, "start")

    @pl.when(i == 0)
    def _():
        fetch(tab_ref, 0)

    @pl.when(i + 1 < pl.num_programs(0))
    def _():
        fetch(tabn_ref, 1 - cur)

    pe = [_rms(jnp.dot(p_ref[r, :].astype(BF16), wpl_ref[...], preferred_element_type=F32),
               plg_ref[...], NORM_EPS) for r in tiles]
    gather(tab_ref, cur, "wait")

    scol = lax.broadcasted_iota(jnp.int32, (tm, ns), 1).astype(F32)

    def selection(r):
        route = route_ref[r, :]
        wsel = (jnp.where(scol == route[:, 0:1], route[:, 2:3], 0.0)
                + jnp.where(scol == route[:, 1:2], route[:, 3:4], 0.0))
        w_hi = wsel.astype(BF16)
        return w_hi, (wsel - w_hi.astype(F32)).astype(BF16)

    wsel = [selection(r) for r in tiles]
    yb16 = [_unpack16(stage_ref[cur, t]).astype(BF16) for t in range(tps)]
    moe = [jnp.dot(w_hi, y, preferred_element_type=F32) + jnp.dot(w_lo, y, preferred_element_type=F32)
           for (w_hi, w_lo), y in zip(wsel, yb16)]
    x2 = [x1_ref[r, :] + m for r, m in zip(tiles, moe)]
    gate = [jax.nn.sigmoid(_dot(_rms(v, pgg_ref[...], NORM_EPS), wgate_ref[...])) for v in x2]
    for r, v, e, g in zip(tiles, x2, pe, gate):
        o_ref[r, :] = v + e * g


def _final(tab, x1, route, p2, w_pl, pl_g, pl_gate_g, w_gate, yb):
    T, D = x1.shape
    tm = min(PROJ_ROWS, T)
    nt = T // tm
    tps = TILES_PER_STEP if nt % TILES_PER_STEP == 0 else 1
    tr = tps * tm
    row = lambda i: (i, 0)
    full = lambda i: (0, 0)

    def fs(a):
        return pl.BlockSpec(a.shape, full)

    return pl.pallas_call(
        _final_kernel,
        grid=(nt // tps,),
        in_specs=[pl.BlockSpec((tps, 1, tab.shape[-1]), lambda i: (i, 0, 0), memory_space=pltpu.SMEM),
                  pl.BlockSpec((tps, 1, tab.shape[-1]), lambda i: (jnp.minimum(i + 1, nt // tps - 1), 0, 0),
                               memory_space=pltpu.SMEM),
                  pl.BlockSpec((tr, D), row), pl.BlockSpec((tr, ROUTE_COLS), row),
                  pl.BlockSpec((tr, PL_DIM), row), fs(w_pl), fs(pl_g), fs(pl_gate_g), fs(w_gate),
                  pl.BlockSpec(memory_space=pl.ANY)],
        out_specs=pl.BlockSpec((tr, D), row),
        out_shape=jax.ShapeDtypeStruct((T, D), F32),
        scratch_shapes=[pltpu.VMEM((2, tps, _stage_rows(tm), D // 2), PACKED), pltpu.SemaphoreType.DMA((2,))],
        compiler_params=_cparams(("arbitrary",)),
        name="combine_final",
    )(tab, tab, x1, route, p2, w_pl, pl_g, pl_gate_g, w_gate, yb)


def kernel(x, p, positions, attn_norm_g, w_in, q_norm_g, k_norm_g, lambda_q1, lambda_k1, lambda_q2, lambda_k2, diff_out_g, mu_rkv, mu_wag, w0, w_lora_a, w_lora_b, a0, a_lora_a, a_lora_b, g_lora_a, g_lora_b, k_k, k_a, r_k, gn_w, gn_b, w_out, moe_norm_g, w_group, w_expert_router, w_gate, w_up, w_down, w_pl, pl_norm_g, pl_gate_norm_g, w_pl_gate):
    B, S, D = x.shape
    T = B * S
    assert p.shape[0] == 1, "one layer"
    x2 = x.reshape(T, D)
    p2 = p[0].reshape(T, PL_DIM)
    pos = positions.astype(F32).reshape(T, 1)

    def row(a):
        return a.reshape(1, -1).astype(F32)

    half = DIFF_HEAD_DIM // 2
    inv_freq = ROPE_THETA ** (-jnp.arange(half, dtype=F32) / half)
    invf = jnp.tile(inv_freq, V7X_LANES // half).reshape(1, V7X_LANES)
    reps = DIFF_WIDTH // DIFF_HEAD_DIM

    q, k, v, zr, zk, zv, wp, ap, gg = _in_proj(
        x2, pos, row(attn_norm_g[0]), w_in[0].astype(BF16),
        w_lora_a[0].astype(BF16), a_lora_a[0].astype(BF16), g_lora_a[0].astype(BF16),
        w_lora_b[0].astype(BF16), a_lora_b[0].astype(BF16), g_lora_b[0].astype(BF16),
        mu_wag[0].astype(F32), row(jnp.tile(q_norm_g[0], reps)), row(jnp.tile(k_norm_g[0], reps)),
        invf, B, S)

    o_diff = _attention(q, k, v, row(lambda_q1[0]), row(lambda_k1[0]), row(lambda_q2[0]),
                        row(lambda_k2[0]), diff_out_g[0].astype(F32).reshape(-1, 1), B, S)
    o_rwkv = _rwkv(zr, zk, zv, wp, ap, gg, mu_rkv[0].astype(F32), row(w0[0]), row(a0[0]),
                   row(k_k[0]), row(k_a[0]), row(r_k[0]), row(gn_w[0]), row(gn_b[0]), B, S)

    wr = jnp.concatenate([w_group[0], jnp.transpose(w_expert_router[0], (1, 0, 2)).reshape(D, N_EXPERTS)],
                         axis=1).astype(F32)
    wr = jnp.pad(wr, ((0, 0), (0, V7X_LANES - wr.shape[1])))
    wr_hi = wr.astype(BF16)
    wr_lo = (wr - wr_hi.astype(F32)).astype(BF16)
    x1, hm, route, cnt, sl_rows = _out_router(x2, o_diff, o_rwkv, w_out[0].astype(BF16), row(moe_norm_g[0]),
                                     wr_hi, wr_lo)

    bm = MOE_ROWS
    tm = min(PROJ_ROWS, T)
    nt = T // tm
    i32 = jnp.int32
    cnt_te = cnt.reshape(nt, V7X_SUBLANES, V7X_LANES)[:, 0, ROUTER_EXPERT_LANE:ROUTER_EXPERT_LANE + N_EXPERTS]
    len_te = (cnt_te.astype(i32) + RUN_ALIGN - 1) // RUN_ALIGN * RUN_ALIGN
    tot_e = jnp.sum(len_te, axis=0)
    pcounts = (tot_e + bm - 1) // bm * bm
    pends = jnp.cumsum(pcounts)
    pstarts = pends - pcounts
    dst_te = pstarts[None, :] + jnp.cumsum(len_te, axis=0) - len_te
    src_te = jnp.cumsum(len_te, axis=1) - len_te
    tab = _run_table(src_te, dst_te, len_te, 0)
    nb = -(-(2 * T + nt * N_EXPERTS * (RUN_ALIGN - 1)) // bm) + N_EXPERTS
    P = nb * bm
    n_used = (pends[-1] // bm).astype(i32).reshape(1)
    block_start = jnp.arange(nb, dtype=i32) * bm
    block_e = jnp.minimum(jnp.sum((pends[None, :] <= block_start[:, None]).astype(i32), axis=1),
                          N_EXPERTS - 1)
    tail = _run_table(jnp.zeros((1, N_EXPERTS), i32), (pstarts + tot_e)[None, :], (pcounts - tot_e)[None, :],
                      n_used)
    xb = _dispatch(tab, tail, sl_rows, hm, P)
    blk = jnp.arange(nb, dtype=i32)
    first = ((blk < n_used[0]) & ((blk == 0) | (block_e != jnp.roll(block_e, 1)))).astype(i32)
    slot = (jnp.cumsum(first) - 1) % 2
    eidx = jnp.arange(N_EXPERTS, dtype=i32)
    later = jnp.where((pcounts > 0)[None, :] & (eidx[None, :] > eidx[:, None]), eidx[None, :], N_EXPERTS)
    nxt_e = jnp.min(later, axis=1)
    nxt = jnp.where(nxt_e == N_EXPERTS, -1, nxt_e)[block_e]
    yb = _experts(block_e, n_used, first, nxt, slot.astype(i32), xb, w_gate[0], w_up[0], w_down[0])
    out = _final(tab, x1, route, p2, w_pl[0].astype(BF16), row(pl_norm_g[0]), row(pl_gate_norm_g[0]),
                 w_pl_gate[0].astype(BF16), yb)
    return out.reshape(B, S, D)
```

```python
import math

import jax
import jax.numpy as jnp
from jax import lax
from jax.experimental import pallas as pl
from jax.experimental.pallas import tpu as pltpu

F32 = jnp.float32
BF16 = jnp.bfloat16

D_MODEL = 1024
PL_DIM = 256
DIFF_WIDTH = 512
RWKV_WIDTH = 512
DIFF_HEAD_DIM = 64
DIFF_HEADS = 4
RWKV_HEAD = 64
ROPE_THETA = 10000.0
NORM_EPS = 1e-6
SUBLN_EPS = 1e-5
GN_EPS = 64e-5
N_GROUPS = 4
EXPERTS_PER_GROUP = 8
N_EXPERTS = 32
D_EXPERT = 512
LAM_INIT = 0.8 - 0.6 * math.exp(0.0)

V7X_LANES = 128
V7X_SUBLANES = 8
V7X_VMEM_BYTES = 64 * 1024 * 1024

IN_PROJ_ROWS = 256
IN_PROJ_TILES_PER_STEP = 2
PROJ_ROWS = 256
TILES_PER_STEP = 4
ATTN_ROWS = 256
ATTN_UNROLL = 4
ATTN_HEADS_PER_STEP = 2
RWKV_CHUNK = 64
RWKV_ROWS = 256
MOE_ROWS = 512
VMEM_LIMIT = V7X_VMEM_BYTES - 8 * 1024 * 1024


def _cparams(sem):
    return pltpu.CompilerParams(dimension_semantics=sem, vmem_limit_bytes=VMEM_LIMIT)


def _dot(a, b):
    return jnp.dot(a.astype(BF16), b.astype(BF16), preferred_element_type=F32)


def _dot_nt(a, b):
    return lax.dot_general(a.astype(BF16), b.astype(BF16), (((1,), (1,)), ((), ())),
                           preferred_element_type=F32)


def _split3(x):
    hi = x.astype(BF16)
    r1 = x - hi.astype(F32)
    mid = r1.astype(BF16)
    lo = (r1 - mid.astype(F32)).astype(BF16)
    return hi, mid, lo


def _dot_exact_lhs(m01, x):
    hi, mid, lo = _split3(x)
    return (jnp.dot(m01, hi, preferred_element_type=F32)
            + jnp.dot(m01, mid, preferred_element_type=F32)
            + jnp.dot(m01, lo, preferred_element_type=F32))


PACKED = jnp.uint32


def _pack16(x):
    h = x.shape[1] // 2
    lo = lax.bitcast_convert_type(x[:, :h], PACKED) >> 16
    hi = lax.bitcast_convert_type(x[:, h:], PACKED) & jnp.uint32(0xFFFF0000)
    return hi | lo


def _unpack16(w):
    lo = lax.bitcast_convert_type(w << 16, F32)
    hi = lax.bitcast_convert_type(w & jnp.uint32(0xFFFF0000), F32)
    return jnp.concatenate([lo, hi], axis=1)


def _sumsq_64(z):
    m, n = z.shape
    group = V7X_LANES // 2
    first = lax.broadcasted_iota(jnp.int32, (m, V7X_LANES), 1) < group
    out = []
    for c in range(0, n, V7X_LANES):
        sq = z[:, c:c + V7X_LANES] * z[:, c:c + V7X_LANES]
        s0 = jnp.sum(jnp.where(first, sq, 0.0), axis=-1, keepdims=True)
        s1 = jnp.sum(jnp.where(first, 0.0, sq), axis=-1, keepdims=True)
        out.append(jnp.where(first, s0, s1))
    return jnp.concatenate(out, axis=1)


def _rms(x, g, eps):
    return x * lax.rsqrt(jnp.mean(x * x, axis=-1, keepdims=True) + eps) * g


def _shift_rows(z, prev_row):
    rolled = pltpu.roll(z, 1, axis=0)
    row = lax.broadcasted_iota(jnp.int32, z.shape, 0)
    return jnp.where(row == 0, prev_row, rolled)


def _in_proj_kernel(x_ref, pos_ref, g_ref, win_ref, w1_ref, a1_ref, g1_ref, w2_ref, a2_ref, g2_ref,
                    mu_ref, qg_ref, kg_ref, invf_ref,
                    q_out, k_out, v_out, zr_out, zk_out, zv_out, wp_out, ap_out, gg_out,
                    carry_ref):
    tm = IN_PROJ_ROWS if x_ref.shape[0] >= IN_PROJ_ROWS else x_ref.shape[0]
    tiles = [slice(t * tm, (t + 1) * tm) for t in range(x_ref.shape[0] // tm)]
    w = DIFF_WIDTH
    half = DIFF_HEAD_DIM // 2

    @pl.when(pl.program_id(1) == 0)
    def _():
        carry_ref[...] = jnp.zeros_like(carry_ref)

    prev = carry_ref[V7X_SUBLANES - 1:V7X_SUBLANES, :]
    hns, dhs = [], []
    for r in tiles:
        hn = _rms(x_ref[r, :], g_ref[...], NORM_EPS)
        dhs.append(_shift_rows(hn, prev) - hn)
        hns.append(hn)
        prev = hn[tm - 1:tm, :]
    carry_ref[...] = hns[-1][tm - V7X_SUBLANES:tm, :]

    zq, zk, low = [], [], []
    for r, hn, dh in zip(tiles, hns, dhs):
        hb = hn.astype(BF16)

        def proj(c, hb=hb):
            return jnp.dot(hb, win_ref[:, c * w:(c + 1) * w], preferred_element_type=F32)

        zq.append(proj(0))
        zk.append(proj(1))
        low.append((_dot(hn + dh * mu_ref[0:1, :], w1_ref[...]), _dot(hn + dh * mu_ref[1:2, :], a1_ref[...]),
                    _dot(hn + dh * mu_ref[2:3, :], g1_ref[...])))
        v_out[r, :] = proj(2).astype(v_out.dtype)
        zr_out[r, :] = proj(3)
        zk_out[r, :] = proj(4)
        zv_out[r, :] = proj(5)

    lane = lax.broadcasted_iota(jnp.int32, (tm, w), 1)
    first_half = (lane % DIFF_HEAD_DIM) < half
    scale = DIFF_HEAD_DIM ** -0.5 * math.log2(math.e)
    hrow = lax.broadcasted_iota(jnp.int32, (tm // 2, V7X_LANES), 1) < DIFF_HEAD_DIM
    pr = lax.broadcasted_iota(jnp.int32, (tm, tm // 2), 0)
    pc = lax.broadcasted_iota(jnp.int32, (tm, tm // 2), 1)
    dup_rows = (pc == (pr >> 1)).astype(BF16)
    er = lax.broadcasted_iota(jnp.int32, (tm, V7X_LANES), 0)
    el = lax.broadcasted_iota(jnp.int32, (tm, V7X_LANES), 1)
    own_half = (el < DIFF_HEAD_DIM) == ((er & 1) == 0)

    def expand(t2):
        y = _dot_exact_lhs(dup_rows, t2)
        return jnp.where(own_half, y, pltpu.roll(y, DIFF_HEAD_DIM, axis=1))

    for ti, (r, q, k, (lw, la, lg)) in enumerate(zip(tiles, zq, zk, low)):
        p2 = pos_ref[ti * (tm // 2):(ti + 1) * (tm // 2), :]
        ang = jnp.where(hrow, p2[:, 0:1], p2[:, 1:2]) * invf_ref[...]
        cosf = jnp.concatenate([expand(jnp.cos(ang))] * (w // V7X_LANES), axis=1)
        sinf = jnp.concatenate([expand(jnp.sin(ang))] * (w // V7X_LANES), axis=1)
        sin_signed = jnp.where(first_half, -sinf, sinf)

        def norm_rope(z, gain, cosf=cosf, sin_signed=sin_signed):
            zn = z * lax.rsqrt(_sumsq_64(z) * (1.0 / DIFF_HEAD_DIM) + NORM_EPS) * gain
            partner = jnp.where(first_half, pltpu.roll(zn, w - half, axis=1), pltpu.roll(zn, half, axis=1))
            return zn * cosf + partner * sin_signed

        q_out[r, :] = (norm_rope(q, qg_ref[...]) * scale).astype(q_out.dtype)
        k_out[r, :] = norm_rope(k, kg_ref[...]).astype(k_out.dtype)
        wp_out[r, :] = _dot(jnp.tanh(lw), w2_ref[...])
        ap_out[r, :] = _dot(la, a2_ref[...])
        gg_out[r, :] = _dot(jax.nn.sigmoid(lg), g2_ref[...])


def _in_proj(x2, pos, attn_g, w_in, w1, a1, g1, w2, a2, g2, mu_wag, qg, kg, invf, B, S):
    T, D = x2.shape
    tile = min(IN_PROJ_ROWS, S)
    tps = IN_PROJ_TILES_PER_STEP if (S // tile) % IN_PROJ_TILES_PER_STEP == 0 else 1
    tm = tile * tps
    ns = S // tm
    w = DIFF_WIDTH
    row = lambda b, i: (b * ns + i, 0)
    full = lambda b, i: (0, 0)

    def fs(a):
        return pl.BlockSpec(a.shape, full)

    outs = ([jax.ShapeDtypeStruct((T, w), BF16)] * 3 + [jax.ShapeDtypeStruct((T, w), F32)] * 6)
    return pl.pallas_call(
        _in_proj_kernel,
        grid=(B, ns),
        in_specs=[pl.BlockSpec((tm, D), row), pl.BlockSpec((tm // 2, 2), row), fs(attn_g), fs(w_in),
                  fs(w1), fs(a1), fs(g1), fs(w2), fs(a2), fs(g2), fs(mu_wag), fs(qg), fs(kg),
                  fs(invf)],
        out_specs=[pl.BlockSpec((tm, w), row)] * 9,
        out_shape=outs,
        scratch_shapes=[pltpu.VMEM((V7X_SUBLANES, D), F32)],
        compiler_params=_cparams(("arbitrary", "arbitrary")),
        name="in_proj",
    )(x2, pos, attn_g, w_in, w1, a1, g1, w2, a2, g2, mu_wag, qg, kg, invf)


def _attn_kernel(lq1_ref, lk1_ref, lq2_ref, lk2_ref, ogt_ref, q_ref, k_ref, v_ref, o_ref,
                 vt_ref, s_ref, acc_ref):
    nh, nq, hw, tq = vt_ref.shape
    tk = tq
    heads = range(nh)
    lam = (jnp.exp(jnp.sum(lq1_ref[...] * lk1_ref[...], axis=-1, keepdims=True))
           - jnp.exp(jnp.sum(lq2_ref[...] * lk2_ref[...], axis=-1, keepdims=True)) + LAM_INIT)

    for h in heads:
        for c in range(nq):
            vt_ref[h, c] = (v_ref[c * tk:(c + 1) * tk, h * hw:(h + 1) * hw]
                            .astype(F32).T.astype(vt_ref.dtype))

    lane = lax.broadcasted_iota(jnp.int32, (tq, hw), 1)

    def stacked_queries(t):
        out = []
        for h in heads:
            q = q_ref[t * tq:(t + 1) * tq, h * hw:(h + 1) * hw]
            zero = jnp.zeros_like(q)
            out.append(jnp.concatenate([jnp.where(lane < DIFF_HEAD_DIM, q, zero),
                                        jnp.where(lane >= DIFF_HEAD_DIM, q, zero)], axis=0))
        return out

    def scores(j, qs):
        start = pl.multiple_of(j * tk, tk)
        return [lax.dot_general(k_ref[pl.ds(start, tk), h * hw:(h + 1) * hw], qs[h],
                                (((1,), (1,)), ((), ())), preferred_element_type=F32) for h in heads]

    def colmax(m, st):
        return tuple(jnp.maximum(mh, jnp.max(s, axis=0, keepdims=True)) for mh, s in zip(m, st))

    krow = lax.broadcasted_iota(jnp.int32, (tk, 2 * tq), 0)
    qcol = lax.broadcasted_iota(jnp.int32, (tk, 2 * tq), 1)
    causal = krow <= jnp.where(qcol >= tq, qcol - tq, qcol)
    ninf = tuple(jnp.full((1, 2 * tq), -jnp.inf, F32) for _ in heads)
    zeros = tuple(jnp.zeros((1, 2 * tq), F32) for _ in heads)

    def diagonal(t, qs, m):
        st = [jnp.where(causal, s, -jnp.inf) for s in scores(t, qs)]
        for h in heads:
            s_ref[t % 2, h, t] = st[h]
        return colmax(m, st)

    m_cur = diagonal(0, stacked_queries(0), ninf)
    for t in range(nq):
        slot, nslot = t % 2, (t + 1) % 2
        has_next = t + 1 < nq
        qs_next = stacked_queries(t + 1) if has_next else None
        acc_ref[...] = jnp.zeros_like(acc_ref)

        def body(j, carry, slot=slot, nslot=nslot, has_next=has_next, qs_next=qs_next, m_cur=m_cur):
            l, m_next = carry
            pt = [jnp.exp2(s_ref[slot, h, j] - m_cur[h]) for h in heads]
            pv = [jnp.dot(vt_ref[h, j], pt[h].astype(BF16), preferred_element_type=F32) for h in heads]
            for h in heads:
                acc_ref[h] += pv[h]
            l = tuple(lh + jnp.sum(p, axis=0, keepdims=True) for lh, p in zip(l, pt))
            if has_next:
                st = scores(j, qs_next)
                for h in heads:
                    s_ref[nslot, h, j] = st[h]
                m_next = colmax(m_next, st)
            return l, m_next

        lsum, m_next = lax.fori_loop(0, t + 1, body, (zeros, ninf), unroll=ATTN_UNROLL)
        if has_next:
            m_next = diagonal(t + 1, qs_next, m_next)

        for h in heads:
            acc = acc_ref[h]
            l = lsum[h]
            ot = acc[:, :tq] / l[:, :tq] - lam * (acc[:, tq:] / l[:, tq:])
            ot = ot * lax.rsqrt(jnp.mean(ot * ot, axis=0, keepdims=True) + SUBLN_EPS) * ogt_ref[...]
            o_ref[t * tq:(t + 1) * tq, h * hw:(h + 1) * hw] = (ot * (1.0 - LAM_INIT)).T.astype(o_ref.dtype)
        m_cur = m_next


def _attention(q, k, v, lq1, lk1, lq2, lk2, og, B, S):
    T = q.shape[0]
    tq = min(ATTN_ROWS, S)
    nq = S // tq
    hw = 2 * DIFF_HEAD_DIM
    nh = ATTN_HEADS_PER_STEP
    gw = nh * hw
    small = lambda b, h: (0, 0)
    seq = pl.BlockSpec((S, gw), lambda b, h: (b, h))
    return pl.pallas_call(
        _attn_kernel,
        grid=(B, DIFF_HEADS // nh),
        in_specs=[pl.BlockSpec(lq1.shape, small), pl.BlockSpec(lk1.shape, small),
                  pl.BlockSpec(lq2.shape, small), pl.BlockSpec(lk2.shape, small),
                  pl.BlockSpec(og.shape, small), seq, seq, seq],
        out_specs=seq,
        out_shape=jax.ShapeDtypeStruct((T, DIFF_WIDTH), BF16),
        scratch_shapes=[pltpu.VMEM((nh, nq, hw, tq), BF16), pltpu.VMEM((2, nh, nq, tq, 2 * tq), F32),
                        pltpu.VMEM((nh, hw, 2 * tq), F32)],
        compiler_params=_cparams(("arbitrary", "arbitrary")),
        name="diff_attn",
    )(lq1, lk1, lq2, lk2, og, q, k, v)


def _rwkv_kernel(zr_ref, zk_ref, zv_ref, wp_ref, ap_ref, gg_ref, mu_ref, w0_ref, a0_ref, kk_ref,
                 ka_ref, rk_ref, gnw_ref, gnb_ref, o_ref, state_ref, carry_ref):
    R = zr_ref.shape[0]
    C = min(RWKV_CHUNK, R)
    nch = R // C
    N = RWKV_HEAD
    W = RWKV_WIDTH
    HP = 2 * N
    n_pairs = W // HP

    @pl.when(pl.program_id(1) == 0)
    def _():
        state_ref[...] = jnp.zeros_like(state_ref)
        carry_ref[...] = jnp.zeros_like(carry_ref)

    zr, zk, zv = zr_ref[...], zk_ref[...], zv_ref[...]
    last = V7X_SUBLANES - 1
    r = zr + (_shift_rows(zr, carry_ref[last:last + 1, 0:W]) - zr) * mu_ref[0:1, :]
    k = zk + (_shift_rows(zk, carry_ref[last:last + 1, W:2 * W]) - zk) * mu_ref[1:2, :]
    v = zv + (_shift_rows(zv, carry_ref[last:last + 1, 2 * W:3 * W]) - zv) * mu_ref[2:3, :]
    carry_ref[:, 0:W] = zr[R - V7X_SUBLANES:R, :]
    carry_ref[:, W:2 * W] = zk[R - V7X_SUBLANES:R, :]
    carry_ref[:, 2 * W:3 * W] = zv[R - V7X_SUBLANES:R, :]

    lw = -math.exp(-0.5) * jax.nn.sigmoid(w0_ref[...] + wp_ref[...])
    a = jax.nn.sigmoid(a0_ref[...] + ap_ref[...])
    kk = k * kk_ref[...]
    kk = kk * lax.rsqrt(jnp.maximum(_sumsq_64(kk), 1e-24))
    k2 = k * (1.0 + (a - 1.0) * ka_ref[...])
    a_s = -kk
    b_s = kk * a

    rr = lax.broadcasted_iota(jnp.int32, (R, R), 0)
    cc = lax.broadcasted_iota(jnp.int32, (R, R), 1)
    same_chunk = (rr // C) == (cc // C)
    L = _dot_exact_lhs((same_chunk & (cc <= rr)).astype(BF16), lw)
    bonus_w = r * k2 * rk_ref[...]

    P2 = 2 * C
    sr = lax.broadcasted_iota(jnp.int32, (P2, HP), 0)
    sc = lax.broadcasted_iota(jnp.int32, (P2, HP), 1)
    stack_mask = (sr < C) == (sc < N)
    br = lax.broadcasted_iota(jnp.int32, (P2, P2), 0)
    bc = lax.broadcasted_iota(jnp.int32, (P2, P2), 1)
    same_head = (br < C) == (bc < C)
    tr = jnp.where(br >= C, br - C, br)
    tc = jnp.where(bc >= C, bc - C, bc)
    strict = same_head & (tc < tr)
    lower = same_head & (tc <= tr)
    eye_p = (br == bc).astype(F32)
    kr = lax.broadcasted_iota(jnp.int32, (HP, HP), 0)
    kc = lax.broadcasted_iota(jnp.int32, (HP, HP), 1)
    eye_k = kr == kc

    def dup(x):
        return jnp.concatenate([x, x], axis=0)

    def stack(x):
        return jnp.where(stack_mask, dup(x), 0.0)

    n_sq = int(math.log2(C)) - 1
    items = [(c, p) for c in range(nch) for p in range(n_pairs)]

    def prep(c, p):
        rows = slice(c * C, (c + 1) * C)
        lanes = slice(p * HP, (p + 1) * HP)
        Lc = L[rows, lanes]
        lwc = lw[rows, lanes]
        LC = Lc[C - 1:C, :]
        enL = jnp.exp(-Lc)
        eCL = jnp.exp(LC - Lc)
        b_c, k_c = b_s[rows, lanes], k2[rows, lanes]
        return dict(
            xa=stack(a_s[rows, lanes] * jnp.exp(Lc - lwc)), xr=stack(r[rows, lanes] * jnp.exp(Lc)),
            bt=dup(b_c * enL), kt=dup(k_c * enL), bh=stack(b_c * eCL), kh=stack(k_c * eCL),
            vs=stack(v[rows, lanes]), pc=jnp.exp(LC))

    d = [prep(c, p) for c, p in items]
    aa = [_dot_nt(jnp.concatenate([e["xa"], e["xr"]], axis=0),
                  jnp.concatenate([e["bt"], e["kt"]], axis=0)) for e in d]
    a_ab = [jnp.where(strict, m[:P2, :P2], 0.0) for m in aa]
    a_ak = [jnp.where(strict, m[:P2, P2:], 0.0) for m in aa]
    a_rb = [jnp.where(lower, m[P2:, :P2], 0.0) for m in aa]
    a_rk = [jnp.where(lower, m[P2:, P2:], 0.0) for m in aa]
    tm = [eye_p + m for m in a_ab]
    npw = a_ab
    for _ in range(n_sq):
        npw = [_dot(m, m) for m in npw]
        tm = [t + _dot(t, m) for t, m in zip(tm, npw)]
    av = [_dot(m, e["vs"]) for m, e in zip(a_ak, d)]
    z = [_dot(t, jnp.concatenate([e["xa"], x], axis=1)) for t, e, x in zip(tm, d, av)]
    zero_v = jnp.zeros((P2, HP), F32)
    rhs = [jnp.concatenate([x, jnp.concatenate([zero_v, e["vs"]], axis=1)], axis=0) for e, x in zip(d, z)]
    lhs = [jnp.concatenate([jnp.concatenate([m1, m2], axis=1),
                            jnp.concatenate([e["bh"], e["kh"]], axis=0).T], axis=0)
           for m1, m2, e in zip(a_rb, a_rk, d)]
    wg = [_dot(a, b) for a, b in zip(lhs, rhs)]

    for idx, (c, p) in enumerate(items):
        e = d[idx]
        rows = slice(c * C, (c + 1) * C)
        lanes = slice(p * HP, (p + 1) * HP)
        rp = e["xr"] + wg[idx][:P2, :HP]
        y0 = wg[idx][:P2, HP:]
        gm = wg[idx][P2:, :HP]
        hm = wg[idx][P2:, HP:]
        st = state_ref[p]
        yg = _dot(jnp.concatenate([rp, gm], axis=0), st)
        pc_col = jnp.sum(jnp.where(eye_k, e["pc"], 0.0), axis=1, keepdims=True)
        state_ref[p] = pc_col * st + yg[P2:] + hm
        ys = yg[:P2] + y0
        mu = jnp.sum(ys, axis=-1, keepdims=True) * (1.0 / N)
        yc = jnp.where(stack_mask, ys - mu, 0.0)
        var = jnp.sum(yc * yc, axis=-1, keepdims=True) * (1.0 / N)
        yn = yc * lax.rsqrt(var + GN_EPS)
        bonus = jnp.sum(stack(bonus_w[rows, lanes]), axis=-1, keepdims=True) * e["vs"]
        yn = yn[:C] + yn[C:]
        bonus = bonus[:C] + bonus[C:]
        o_ref[rows, lanes] = ((yn * gnw_ref[:, lanes] + gnb_ref[:, lanes] + bonus)
                              * gg_ref[rows, lanes]).astype(o_ref.dtype)


def _rwkv(zr, zk, zv, wp, ap, gg, mu_rkv, w0, a0, k_k, k_a, r_k, gn_w, gn_b, B, S):
    T = zr.shape[0]
    C = min(RWKV_ROWS, S)
    nc = S // C
    W = RWKV_WIDTH
    HP = 2 * RWKV_HEAD
    row = lambda b, i: (b * nc + i, 0)
    full = lambda b, i: (0, 0)

    def fs(a):
        return pl.BlockSpec(a.shape, full)

    return pl.pallas_call(
        _rwkv_kernel,
        grid=(B, nc),
        in_specs=[pl.BlockSpec((C, W), row)] * 6 + [fs(mu_rkv), fs(w0), fs(a0), fs(k_k), fs(k_a),
                                                    fs(r_k), fs(gn_w), fs(gn_b)],
        out_specs=pl.BlockSpec((C, W), row),
        out_shape=jax.ShapeDtypeStruct((T, W), BF16),
        scratch_shapes=[pltpu.VMEM((W // HP, HP, HP), F32),
                        pltpu.VMEM((V7X_SUBLANES, 3 * W), F32)],
        compiler_params=_cparams(("arbitrary", "arbitrary")),
        name="rwkv7",
    )(zr, zk, zv, wp, ap, gg, mu_rkv, w0, a0, k_k, k_a, r_k, gn_w, gn_b)


ROUTE_COLS = 8
ROUTER_EXPERT_LANE = N_GROUPS


def _out_router_kernel(x_ref, od_ref, orw_ref, wo_ref, mg_ref, wrh_ref, wrl_ref,
                       x1_out, hm_out, route_out, cnt_out, sl_out):
    tm = PROJ_ROWS if x_ref.shape[0] >= PROJ_ROWS else x_ref.shape[0]
    tiles = [slice(t * tm, (t + 1) * tm) for t in range(x_ref.shape[0] // tm)]
    x1 = [x_ref[r, :] + jnp.dot(od_ref[r, :], wo_ref[0:DIFF_WIDTH, :], preferred_element_type=F32)
          + jnp.dot(orw_ref[r, :], wo_ref[DIFF_WIDTH:, :], preferred_element_type=F32) for r in tiles]
    hm = [_rms(v, mg_ref[...], NORM_EPS) for v in x1]
    for r, v, h in zip(tiles, x1, hm):
        x1_out[r, :] = v
        hm_out[r, :] = h.astype(hm_out.dtype)

    def logits(h):
        hi = h.astype(BF16)
        lo = (h - hi.astype(F32)).astype(BF16)
        return (jnp.dot(hi, wrh_ref[...], preferred_element_type=F32)
                + jnp.dot(hi, wrl_ref[...], preferred_element_type=F32)
                + jnp.dot(lo, wrh_ref[...], preferred_element_type=F32))

    lgs = [logits(h) for h in hm]
    lane = lax.broadcasted_iota(jnp.int32, (tm, V7X_LANES), 1).astype(F32)
    big = float(V7X_LANES)
    ninf = -jnp.inf
    gmask = lane < N_GROUPS

    def choose(lg):
        gmax = jnp.max(jnp.where(gmask, lg, ninf), axis=-1, keepdims=True)
        g_sel = jnp.min(jnp.where(gmask & (lg == gmax), lane, big), axis=-1, keepdims=True)
        g_w = 1.0 / jnp.sum(jnp.where(gmask, jnp.exp(lg - gmax), 0.0), axis=-1, keepdims=True)
        lo_lane = ROUTER_EXPERT_LANE + g_sel * EXPERTS_PER_GROUP
        emask = (lane >= lo_lane) & (lane < lo_lane + EXPERTS_PER_GROUP)
        v1 = jnp.max(jnp.where(emask, lg, ninf), axis=-1, keepdims=True)
        i1 = jnp.min(jnp.where(emask & (lg == v1), lane, big), axis=-1, keepdims=True)
        emask2 = emask & (lane != i1)
        v2 = jnp.max(jnp.where(emask2, lg, ninf), axis=-1, keepdims=True)
        i2 = jnp.min(jnp.where(emask2 & (lg == v2), lane, big), axis=-1, keepdims=True)
        e2x = jnp.exp(v2 - v1)
        den = 1.0 + e2x
        return i1, i2, (1.0 / den) * g_w, (e2x / den) * g_w

    chosen = [choose(lg) for lg in lgs]

    ri = lax.broadcasted_iota(jnp.int32, (tm, tm), 0)
    ci = lax.broadcasted_iota(jnp.int32, (tm, tm), 1)
    before = (ci < ri).astype(BF16)
    li = lax.broadcasted_iota(jnp.int32, (V7X_LANES, V7X_LANES), 0)
    lj = lax.broadcasted_iota(jnp.int32, (V7X_LANES, V7X_LANES), 1)
    lanes_before = (li < lj).astype(BF16)
    ohs = [((lane == i1) | (lane == i2)).astype(F32) for i1, i2, _, _ in chosen]
    prefixes = [jnp.dot(before, oh.astype(BF16), preferred_element_type=F32) for oh in ohs]
    counts = [jnp.broadcast_to(jnp.sum(oh, axis=0, keepdims=True), (V7X_SUBLANES, V7X_LANES)) for oh in ohs]
    padded = [jnp.floor((c + (RUN_ALIGN - 1)) * (1.0 / RUN_ALIGN)) * RUN_ALIGN for c in counts]
    starts = [jnp.dot(pc.astype(BF16), lanes_before, preferred_element_type=F32) for pc in padded]
    col = lax.broadcasted_iota(jnp.int32, (tm, V7X_LANES), 1)
    for t, (r, (i1, i2, wt1, wt2), cnt, start, prefix) in enumerate(zip(tiles, chosen, counts, starts, prefixes)):
        row = prefix + start[0:1, :]
        sl1 = jnp.sum(jnp.where(lane == i1, row, 0.0), axis=-1, keepdims=True)
        sl2 = jnp.sum(jnp.where(lane == i2, row, 0.0), axis=-1, keepdims=True)
        cnt_out[t * V7X_SUBLANES:(t + 1) * V7X_SUBLANES, :] = cnt
        wide = jnp.where(col == 0, sl1, jnp.where(col == 1, sl2, jnp.where(col == 2, wt1, jnp.where(
            col == 3, wt2, 0.0))))
        route_out[r, :] = wide[:, 0:ROUTE_COLS]
        sl_out[t] = wide.T[0:V7X_SUBLANES, :].astype(sl_out.dtype)


def _out_router(x2, od, orw, w_out, moe_g, wr_hi, wr_lo):
    T, D = x2.shape
    tm = min(PROJ_ROWS, T)
    nt = T // tm
    tps = TILES_PER_STEP if nt % TILES_PER_STEP == 0 else 1
    tr = tm * tps
    row = lambda i: (i, 0)
    full = lambda i: (0, 0)

    def fs(a):
        return pl.BlockSpec(a.shape, full)

    return pl.pallas_call(
        _out_router_kernel,
        grid=(nt // tps,),
        in_specs=[pl.BlockSpec((tr, D), row), pl.BlockSpec((tr, DIFF_WIDTH), row),
                  pl.BlockSpec((tr, RWKV_WIDTH), row), fs(w_out), fs(moe_g), fs(wr_hi), fs(wr_lo)],
        out_specs=[pl.BlockSpec((tr, D), row), pl.BlockSpec((tr, D), row),
                   pl.BlockSpec((tr, ROUTE_COLS), row), pl.BlockSpec((tps * V7X_SUBLANES, V7X_LANES), row),
                   pl.BlockSpec((tps, V7X_SUBLANES, tm), lambda i: (i, 0, 0))],
        out_shape=[jax.ShapeDtypeStruct((T, D), F32), jax.ShapeDtypeStruct((T, D), BF16),
                   jax.ShapeDtypeStruct((T, ROUTE_COLS), F32),
                   jax.ShapeDtypeStruct((nt * V7X_SUBLANES, V7X_LANES), F32),
                   jax.ShapeDtypeStruct((nt, V7X_SUBLANES, tm), jnp.int32)],
        compiler_params=_cparams(("arbitrary",)),
        name="out_router",
    )(x2, od, orw, w_out, moe_g, wr_hi, wr_lo)


RUN_ALIGN = V7X_SUBLANES
RUN_CLASSES = 6
TAB_COUNT, TAB_NUSED = 0, RUN_CLASSES
TAB_SRC = V7X_SUBLANES
TAB_DST = TAB_SRC + RUN_CLASSES * N_EXPERTS
TAB_WORDS = 4 * V7X_LANES


def _stage_rows(tm):
    need = 2 * tm + N_EXPERTS * (RUN_ALIGN - 1)
    return -(-need // V7X_LANES) * V7X_LANES


def _run_table(src, dst, length, spare):
    i32 = jnp.int32
    n = src.shape[0]
    cls = jnp.arange(RUN_CLASSES, dtype=i32)[None, :, None]
    ln = length[:, None, :]
    has = ((ln >> 3) >> cls) & 1
    off = ln & ~((2 * RUN_ALIGN << cls) - 1)
    pos = jnp.cumsum(has, axis=-1) - 1
    k = jnp.arange(N_EXPERTS, dtype=i32)
    pick = (has[..., None] == 1) & (pos[..., None] == k)

    def compact(v):
        return jnp.sum(jnp.where(pick, (v[:, None, :] + off)[..., None], 0), axis=2).reshape(n, -1)

    head = jnp.concatenate([jnp.sum(has, axis=-1), jnp.broadcast_to(spare, (n, 1)).astype(i32),
                            jnp.zeros((n, TAB_SRC - RUN_CLASSES - 1), i32)], axis=1)
    tab = jnp.concatenate([head, compact(src), compact(dst)], axis=1)
    return jnp.pad(tab, ((0, 0), (0, TAB_WORDS - tab.shape[1]))).reshape(n, 1, TAB_WORDS)


def _run_copies(tab_ref, tile, make_copy, op):
    for c in range(RUN_CLASSES):
        size = RUN_ALIGN << c

        def one(k, carry, c=c, size=size):
            src = tab_ref[tile, 0, TAB_SRC + c * N_EXPERTS + k]
            dst = tab_ref[tile, 0, TAB_DST + c * N_EXPERTS + k]
            getattr(make_copy(pl.multiple_of(src, RUN_ALIGN), pl.multiple_of(dst, RUN_ALIGN), size), op)()
            return carry

        lax.fori_loop(0, tab_ref[tile, 0, TAB_COUNT + c], one, 0)


def _dispatch_kernel(tab_ref, tail_ref, sl_ref, hm_ref, xb_ref, stage_ref, zero_ref, sem, zsem):
    tps, ns = stage_ref.shape[0], stage_ref.shape[1]
    tm = hm_ref.shape[0] // tps
    bm = zero_ref.shape[0]
    nb = xb_ref.shape[0] // bm
    n_used = tail_ref[0, 0, TAB_NUSED]

    def tile_copy(t):
        def make_copy(src, dst, size):
            return pltpu.make_async_copy(stage_ref.at[t, pl.ds(src, size)], xb_ref.at[pl.ds(dst, size)], sem)
        return make_copy

    def zero_copy(src, dst, size):
        return pltpu.make_async_copy(zero_ref.at[pl.ds(src, size)], xb_ref.at[pl.ds(dst, size)], zsem)

    def zero_blocks(op):
        def one(b, c):
            getattr(zero_copy(0, pl.multiple_of(b * bm, bm), bm), op)()
            return c
        lax.fori_loop(n_used, nb, one, 0)

    @pl.when(pl.program_id(0) == 0)
    def _():
        zero_ref[...] = jnp.zeros_like(zero_ref)
        _run_copies(tail_ref, 0, zero_copy, "start")
        zero_blocks("start")

    srow = lax.broadcasted_iota(jnp.int32, (ns, tm), 0)
    for t in range(tps):
        sel = (srow == sl_ref[t, 0:1, :]) | (srow == sl_ref[t, 1:2, :])
        stage_ref[t] = _pack16(jnp.dot(sel.astype(BF16), hm_ref[t * tm:(t + 1) * tm, :],
                                       preferred_element_type=F32))
        _run_copies(tab_ref, t, tile_copy(t), "start")
    for t in range(tps):
        _run_copies(tab_ref, t, tile_copy(t), "wait")

    @pl.when(pl.program_id(0) == pl.num_programs(0) - 1)
    def _():
        _run_copies(tail_ref, 0, zero_copy, "wait")
        zero_blocks("wait")


def _dispatch(tab, tail, sl_rows, hm, P):
    T, D = hm.shape
    tm = min(PROJ_ROWS, T)
    nt = T // tm
    tps = TILES_PER_STEP if nt % TILES_PER_STEP == 0 else 1
    return pl.pallas_call(
        _dispatch_kernel,
        grid=(nt // tps,),
        in_specs=[pl.BlockSpec((tps, 1, tab.shape[-1]), lambda i: (i, 0, 0), memory_space=pltpu.SMEM),
                  pl.BlockSpec((1, 1, tail.shape[-1]), lambda i: (0, 0, 0), memory_space=pltpu.SMEM),
                  pl.BlockSpec((tps, V7X_SUBLANES, tm), lambda i: (i, 0, 0)),
                  pl.BlockSpec((tps * tm, D), lambda i: (i, 0))],
        out_specs=pl.BlockSpec(memory_space=pl.ANY),
        out_shape=jax.ShapeDtypeStruct((P, D // 2), PACKED),
        scratch_shapes=[pltpu.VMEM((tps, _stage_rows(tm), D // 2), PACKED),
                        pltpu.VMEM((MOE_ROWS, D // 2), PACKED),
                        pltpu.SemaphoreType.DMA(()), pltpu.SemaphoreType.DMA(())],
        compiler_params=_cparams(("arbitrary",)),
        name="moe_dispatch",
    )(tab, tail, sl_rows, hm)


def _expert_kernel(be_ref, nu_ref, first_ref, nxt_ref, slot_ref, xb_ref, wg_hbm, wu_hbm, wd_hbm, yb_ref,
                   wgf, wuf, wdf, wgb, wub, wdb, sem):
    i = pl.program_id(0)

    def weight_copies(e, s):
        return (pltpu.make_async_copy(wg_hbm.at[e], wgf.at[s], sem.at[s, 0]),
                pltpu.make_async_copy(wu_hbm.at[e], wuf.at[s], sem.at[s, 1]),
                pltpu.make_async_copy(wd_hbm.at[e], wdf.at[s], sem.at[s, 2]))

    @pl.when(i == 0)
    def _():
        for cp in weight_copies(be_ref[0], 0):
            cp.start()

    @pl.when(first_ref[i] == 1)
    def _():
        s = slot_ref[i]
        for cp in weight_copies(be_ref[i], s):
            cp.wait()
        wgb[...] = wgf[s].astype(BF16)
        wub[...] = wuf[s].astype(BF16)
        wdb[...] = wdf[s].astype(BF16)

        @pl.when(nxt_ref[i] >= 0)
        def _():
            for cp in weight_copies(nxt_ref[i], 1 - s):
                cp.start()

    @pl.when(i < nu_ref[0])
    def _():
        xb = _unpack16(xb_ref[...]).astype(BF16)
        gate = jnp.dot(xb, wgb[...], preferred_element_type=F32)
        up = jnp.dot(xb, wub[...], preferred_element_type=F32)
        hdn = (gate * jax.nn.sigmoid(gate)) * up
        y = jnp.dot(hdn.astype(BF16), wdb[...], preferred_element_type=F32)
        yb_ref[...] = _pack16(y.astype(BF16).astype(F32))

    @pl.when(i >= nu_ref[0])
    def _():
        yb_ref[...] = jnp.zeros_like(yb_ref)


def _experts(block_e, n_used, first, nxt, slot, xb, w_gate, w_up, w_down):
    P, DP = xb.shape
    D = 2 * DP
    bm = MOE_ROWS
    nb = P // bm
    E = D_EXPERT
    hbm = pl.BlockSpec(memory_space=pl.ANY)
    grid_spec = pltpu.PrefetchScalarGridSpec(
        num_scalar_prefetch=5,
        grid=(nb,),
        in_specs=[pl.BlockSpec((bm, DP), lambda i, be, nu, *_: (jnp.minimum(i, nu[0] - 1), 0)), hbm, hbm, hbm],
        out_specs=pl.BlockSpec((bm, DP), lambda i, be, nu, *_: (jnp.where(i < nu[0], i, nb - 1), 0)),
        scratch_shapes=[pltpu.VMEM((2, D, E), F32), pltpu.VMEM((2, D, E), F32), pltpu.VMEM((2, E, D), F32),
                        pltpu.VMEM((D, E), BF16), pltpu.VMEM((D, E), BF16), pltpu.VMEM((E, D), BF16),
                        pltpu.SemaphoreType.DMA((2, 3))],
    )
    return pl.pallas_call(
        _expert_kernel,
        grid_spec=grid_spec,
        out_shape=jax.ShapeDtypeStruct((P, DP), PACKED),
        input_output_aliases={5: 0},
        compiler_params=_cparams(("arbitrary",)),
        name="moe_experts",
    )(block_e, n_used, first, nxt, slot, xb, w_gate, w_up, w_down)


def _final_kernel(tab_ref, tabn_ref, x1_ref, route_ref, p_ref, wpl_ref, plg_ref, pgg_ref, wgate_ref, yb_ref,
                  o_ref, stage_ref, sem):
    tps, ns = stage_ref.shape[1], stage_ref.shape[2]
    tm = x1_ref.shape[0] // tps
    tiles = [slice(t * tm, (t + 1) * tm) for t in range(tps)]
    i = pl.program_id(0)
    cur = i % 2

    def gather(tab, slot, op):
        for t in range(tps):
            def make_copy(src, dst, size, t=t):
                return pltpu.make_async_copy(yb_ref.at[pl.ds(dst, size)],
                                             stage_ref.at[slot, t, pl.ds(src, size)], sem.at[slot])
            _run_copies(tab, t, make_copy, op)

    def fetch(tab, slot):
        stage_ref[slot] = jnp.zeros(stage_ref.shape[1:], stage_ref.dtype)
        gather(tab, slot, "start")

    @pl.when(i == 0)
    def _():
        fetch(tab_ref, 0)

    @pl.when(i + 1 < pl.num_programs(0))
    def _():
        fetch(tabn_ref, 1 - cur)

    pe = [_rms(jnp.dot(p_ref[r, :].astype(BF16), wpl_ref[...], preferred_element_type=F32),
               plg_ref[...], NORM_EPS) for r in tiles]
    gather(tab_ref, cur, "wait")

    scol = lax.broadcasted_iota(jnp.int32, (tm, ns), 1).astype(F32)

    def selection(r):
        route = route_ref[r, :]
        wsel = (jnp.where(scol == route[:, 0:1], route[:, 2:3], 0.0)
                + jnp.where(scol == route[:, 1:2], route[:, 3:4], 0.0))
        w_hi = wsel.astype(BF16)
        return w_hi, (wsel - w_hi.astype(F32)).astype(BF16)

    wsel = [selection(r) for r in tiles]
    yb16 = [_unpack16(stage_ref[cur, t]).astype(BF16) for t in range(tps)]
    moe = [jnp.dot(w_hi, y, preferred_element_type=F32) + jnp.dot(w_lo, y, preferred_element_type=F32)
           for (w_hi, w_lo), y in zip(wsel, yb16)]
    x2 = [x1_ref[r, :] + m for r, m in zip(tiles, moe)]
    gate = [jax.nn.sigmoid(_dot(_rms(v, pgg_ref[...], NORM_EPS), wgate_ref[...])) for v in x2]
    for r, v, e, g in zip(tiles, x2, pe, gate):
        o_ref[r, :] = v + e * g


def _final(tab, x1, route, p2, w_pl, pl_g, pl_gate_g, w_gate, yb):
    T, D = x1.shape
    tm = min(PROJ_ROWS, T)
    nt = T // tm
    tps = TILES_PER_STEP if nt % TILES_PER_STEP == 0 else 1
    tr = tps * tm
    row = lambda i: (i, 0)
    full = lambda i: (0, 0)

    def fs(a):
        return pl.BlockSpec(a.shape, full)

    return pl.pallas_call(
        _final_kernel,
        grid=(nt // tps,),
        in_specs=[pl.BlockSpec((tps, 1, tab.shape[-1]), lambda i: (i, 0, 0), memory_space=pltpu.SMEM),
                  pl.BlockSpec((tps, 1, tab.shape[-1]), lambda i: (jnp.minimum(i + 1, nt // tps - 1), 0, 0),
                               memory_space=pltpu.SMEM),
                  pl.BlockSpec((tr, D), row), pl.BlockSpec((tr, ROUTE_COLS), row),
                  pl.BlockSpec((tr, PL_DIM), row), fs(w_pl), fs(pl_g), fs(pl_gate_g), fs(w_gate),
                  pl.BlockSpec(memory_space=pl.ANY)],
        out_specs=pl.BlockSpec((tr, D), row),
        out_shape=jax.ShapeDtypeStruct((T, D), F32),
        scratch_shapes=[pltpu.VMEM((2, tps, _stage_rows(tm), D // 2), PACKED), pltpu.SemaphoreType.DMA((2,))],
        compiler_params=_cparams(("arbitrary",)),
        name="combine_final",
    )(tab, tab, x1, route, p2, w_pl, pl_g, pl_gate_g, w_gate, yb)


def kernel(x, p, positions, attn_norm_g, w_in, q_norm_g, k_norm_g, lambda_q1, lambda_k1, lambda_q2, lambda_k2, diff_out_g, mu_rkv, mu_wag, w0, w_lora_a, w_lora_b, a0, a_lora_a, a_lora_b, g_lora_a, g_lora_b, k_k, k_a, r_k, gn_w, gn_b, w_out, moe_norm_g, w_group, w_expert_router, w_gate, w_up, w_down, w_pl, pl_norm_g, pl_gate_norm_g, w_pl_gate):
    B, S, D = x.shape
    T = B * S
    assert p.shape[0] == 1, "one layer"
    x2 = x.reshape(T, D)
    p2 = p[0].reshape(T, PL_DIM)
    pos = positions.astype(F32).reshape(T // 2, 2)

    def row(a):
        return a.reshape(1, -1).astype(F32)

    half = DIFF_HEAD_DIM // 2
    inv_freq = ROPE_THETA ** (-jnp.arange(half, dtype=F32) / half)
    invf = jnp.tile(inv_freq, V7X_LANES // half).reshape(1, V7X_LANES)
    reps = DIFF_WIDTH // DIFF_HEAD_DIM

    q, k, v, zr, zk, zv, wp, ap, gg = _in_proj(
        x2, pos, row(attn_norm_g[0]), w_in[0].astype(BF16),
        w_lora_a[0].astype(BF16), a_lora_a[0].astype(BF16), g_lora_a[0].astype(BF16),
        w_lora_b[0].astype(BF16), a_lora_b[0].astype(BF16), g_lora_b[0].astype(BF16),
        mu_wag[0].astype(F32), row(jnp.tile(q_norm_g[0], reps)), row(jnp.tile(k_norm_g[0], reps)),
        invf, B, S)

    o_diff = _attention(q, k, v, row(lambda_q1[0]), row(lambda_k1[0]), row(lambda_q2[0]),
                        row(lambda_k2[0]), diff_out_g[0].astype(F32).reshape(-1, 1), B, S)
    o_rwkv = _rwkv(zr, zk, zv, wp, ap, gg, mu_rkv[0].astype(F32), row(w0[0]), row(a0[0]),
                   row(k_k[0]), row(k_a[0]), row(r_k[0]), row(gn_w[0]), row(gn_b[0]), B, S)

    wr = jnp.concatenate([w_group[0], jnp.transpose(w_expert_router[0], (1, 0, 2)).reshape(D, N_EXPERTS)],
                         axis=1).astype(F32)
    wr = jnp.pad(wr, ((0, 0), (0, V7X_LANES - wr.shape[1])))
    wr_hi = wr.astype(BF16)
    wr_lo = (wr - wr_hi.astype(F32)).astype(BF16)
    x1, hm, route, cnt, sl_rows = _out_router(x2, o_diff, o_rwkv, w_out[0].astype(BF16), row(moe_norm_g[0]),
                                     wr_hi, wr_lo)

    bm = MOE_ROWS
    tm = min(PROJ_ROWS, T)
    nt = T // tm
    i32 = jnp.int32
    cnt_te = cnt.reshape(nt, V7X_SUBLANES, V7X_LANES)[:, 0, ROUTER_EXPERT_LANE:ROUTER_EXPERT_LANE + N_EXPERTS]
    len_te = (cnt_te.astype(i32) + RUN_ALIGN - 1) // RUN_ALIGN * RUN_ALIGN
    tot_e = jnp.sum(len_te, axis=0)
    pcounts = (tot_e + bm - 1) // bm * bm
    pends = jnp.cumsum(pcounts)
    pstarts = pends - pcounts
    dst_te = pstarts[None, :] + jnp.cumsum(len_te, axis=0) - len_te
    src_te = jnp.cumsum(len_te, axis=1) - len_te
    tab = _run_table(src_te, dst_te, len_te, 0)
    nb = -(-(2 * T + nt * N_EXPERTS * (RUN_ALIGN - 1)) // bm) + N_EXPERTS
    P = nb * bm
    n_used = (pends[-1] // bm).astype(i32).reshape(1)
    block_start = jnp.arange(nb, dtype=i32) * bm
    block_e = jnp.minimum(jnp.sum((pends[None, :] <= block_start[:, None]).astype(i32), axis=1),
                          N_EXPERTS - 1)
    tail = _run_table(jnp.zeros((1, N_EXPERTS), i32), (pstarts + tot_e)[None, :], (pcounts - tot_e)[None, :],
                      n_used)
    xb = _dispatch(tab, tail, sl_rows, hm, P)
    blk = jnp.arange(nb, dtype=i32)
    first = ((blk < n_used[0]) & ((blk == 0) | (block_e != jnp.roll(block_e, 1)))).astype(i32)
    slot = (jnp.cumsum(first) - 1) % 2
    eidx = jnp.arange(N_EXPERTS, dtype=i32)
    later = jnp.where((pcounts > 0)[None, :] & (eidx[None, :] > eidx[:, None]), eidx[None, :], N_EXPERTS)
    nxt_e = jnp.min(later, axis=1)
    nxt = jnp.where(nxt_e == N_EXPERTS, -1, nxt_e)[block_e]
    yb = _experts(block_e, n_used, first, nxt, slot.astype(i32), xb, w_gate[0], w_up[0], w_down[0])
    out = _final(tab, x1, route, p2, w_pl[0].astype(BF16), row(pl_norm_g[0]), row(pl_gate_norm_g[0]),
                 w_pl_gate[0].astype(BF16), yb)
    return out.reshape(B, S, D)
```

```python
import math

import jax
import jax.numpy as jnp
from jax import lax
from jax.experimental import pallas as pl
from jax.experimental.pallas import tpu as pltpu

F32 = jnp.float32
BF16 = jnp.bfloat16

D_MODEL = 1024
PL_DIM = 256
DIFF_WIDTH = 512
RWKV_WIDTH = 512
DIFF_HEAD_DIM = 64
DIFF_HEADS = 4
RWKV_HEAD = 64
ROPE_THETA = 10000.0
NORM_EPS = 1e-6
SUBLN_EPS = 1e-5
GN_EPS = 64e-5
N_GROUPS = 4
EXPERTS_PER_GROUP = 8
N_EXPERTS = 32
D_EXPERT = 512
LAM_INIT = 0.8 - 0.6 * math.exp(0.0)

V7X_LANES = 128
V7X_SUBLANES = 8
V7X_VMEM_BYTES = 64 * 1024 * 1024

IN_PROJ_ROWS = 256
IN_PROJ_TILES_PER_STEP = 2
PROJ_ROWS = 256
TILES_PER_STEP = 4
ATTN_ROWS = 256
ATTN_UNROLL = 4
ATTN_HEADS_PER_STEP = 2
RWKV_CHUNK = 64
RWKV_ROWS = 256
MOE_ROWS = 512
VMEM_LIMIT = V7X_VMEM_BYTES - 8 * 1024 * 1024


def _cparams(sem):
    return pltpu.CompilerParams(dimension_semantics=sem, vmem_limit_bytes=VMEM_LIMIT)


def _dot(a, b):
    return jnp.dot(a.astype(BF16), b.astype(BF16), preferred_element_type=F32)


def _dot_nt(a, b):
    return lax.dot_general(a.astype(BF16), b.astype(BF16), (((1,), (1,)), ((), ())),
                           preferred_element_type=F32)


def _split3(x):
    hi = x.astype(BF16)
    r1 = x - hi.astype(F32)
    mid = r1.astype(BF16)
    lo = (r1 - mid.astype(F32)).astype(BF16)
    return hi, mid, lo


def _dot_exact_lhs(m01, x):
    hi, mid, lo = _split3(x)
    return (jnp.dot(m01, hi, preferred_element_type=F32)
            + jnp.dot(m01, mid, preferred_element_type=F32)
            + jnp.dot(m01, lo, preferred_element_type=F32))


PACKED = jnp.uint32


def _pack16(x):
    h = x.shape[1] // 2
    lo = lax.bitcast_convert_type(x[:, :h], PACKED) >> 16
    hi = lax.bitcast_convert_type(x[:, h:], PACKED) & jnp.uint32(0xFFFF0000)
    return hi | lo


def _unpack16(w):
    lo = lax.bitcast_convert_type(w << 16, F32)
    hi = lax.bitcast_convert_type(w & jnp.uint32(0xFFFF0000), F32)
    return jnp.concatenate([lo, hi], axis=1)


def _sumsq_64(z):
    m, n = z.shape
    group = V7X_LANES // 2
    first = lax.broadcasted_iota(jnp.int32, (m, V7X_LANES), 1) < group
    out = []
    for c in range(0, n, V7X_LANES):
        sq = z[:, c:c + V7X_LANES] * z[:, c:c + V7X_LANES]
        s0 = jnp.sum(jnp.where(first, sq, 0.0), axis=-1, keepdims=True)
        s1 = jnp.sum(jnp.where(first, 0.0, sq), axis=-1, keepdims=True)
        out.append(jnp.where(first, s0, s1))
    return jnp.concatenate(out, axis=1)


def _rms(x, g, eps):
    return x * lax.rsqrt(jnp.mean(x * x, axis=-1, keepdims=True) + eps) * g


def _shift_rows(z, prev_row):
    rolled = pltpu.roll(z, 1, axis=0)
    row = lax.broadcasted_iota(jnp.int32, z.shape, 0)
    return jnp.where(row == 0, prev_row, rolled)


def _in_proj_kernel(x_ref, pos_ref, g_ref, win_ref, w1_ref, a1_ref, g1_ref, w2_ref, a2_ref, g2_ref,
                    mu_ref, qg_ref, kg_ref, invf_ref,
                    q_out, k_out, v_out, zr_out, zk_out, zv_out, wp_out, ap_out, gg_out,
                    carry_ref):
    tm = IN_PROJ_ROWS if x_ref.shape[0] >= IN_PROJ_ROWS else x_ref.shape[0]
    tiles = [slice(t * tm, (t + 1) * tm) for t in range(x_ref.shape[0] // tm)]
    w = DIFF_WIDTH
    half = DIFF_HEAD_DIM // 2

    @pl.when(pl.program_id(1) == 0)
    def _():
        carry_ref[...] = jnp.zeros_like(carry_ref)

    prev = carry_ref[V7X_SUBLANES - 1:V7X_SUBLANES, :]
    hns, dhs = [], []
    for r in tiles:
        hn = _rms(x_ref[r, :], g_ref[...], NORM_EPS)
        dhs.append(_shift_rows(hn, prev) - hn)
        hns.append(hn)
        prev = hn[tm - 1:tm, :]
    carry_ref[...] = hns[-1][tm - V7X_SUBLANES:tm, :]

    zq, zk, low = [], [], []
    for r, hn, dh in zip(tiles, hns, dhs):
        hb = hn.astype(BF16)

        def proj(c, hb=hb):
            return jnp.dot(hb, win_ref[:, c * w:(c + 1) * w], preferred_element_type=F32)

        zq.append(proj(0))
        zk.append(proj(1))
        low.append((_dot(hn + dh * mu_ref[0:1, :], w1_ref[...]), _dot(hn + dh * mu_ref[1:2, :], a1_ref[...]),
                    _dot(hn + dh * mu_ref[2:3, :], g1_ref[...])))
        v_out[r, :] = proj(2).astype(v_out.dtype)
        zr_out[r, :] = proj(3)
        zk_out[r, :] = proj(4)
        zv_out[r, :] = proj(5)

    lane = lax.broadcasted_iota(jnp.int32, (tm, w), 1)
    first_half = (lane % DIFF_HEAD_DIM) < half
    scale = DIFF_HEAD_DIM ** -0.5 * math.log2(math.e)
    hrow = lax.broadcasted_iota(jnp.int32, (tm // 2, V7X_LANES), 1) < DIFF_HEAD_DIM
    pr = lax.broadcasted_iota(jnp.int32, (tm, tm // 2), 0)
    pc = lax.broadcasted_iota(jnp.int32, (tm, tm // 2), 1)
    dup_rows = (pc == (pr >> 1)).astype(BF16)
    er = lax.broadcasted_iota(jnp.int32, (tm, V7X_LANES), 0)
    el = lax.broadcasted_iota(jnp.int32, (tm, V7X_LANES), 1)
    own_half = (el < DIFF_HEAD_DIM) == ((er & 1) == 0)

    def expand(t2):
        y = _dot_exact_lhs(dup_rows, t2)
        return jnp.where(own_half, y, pltpu.roll(y, DIFF_HEAD_DIM, axis=1))

    for ti, (r, q, k, (lw, la, lg)) in enumerate(zip(tiles, zq, zk, low)):
        p2 = pos_ref[ti * (tm // 2):(ti + 1) * (tm // 2), :]
        ang = jnp.where(hrow, p2[:, 0:1], p2[:, 1:2]) * invf_ref[...]
        cosf = jnp.concatenate([expand(jnp.cos(ang))] * (w // V7X_LANES), axis=1)
        sinf = jnp.concatenate([expand(jnp.sin(ang))] * (w // V7X_LANES), axis=1)
        sin_signed = jnp.where(first_half, -sinf, sinf)

        def norm_rope(z, gain, cosf=cosf, sin_signed=sin_signed):
            zn = z * lax.rsqrt(_sumsq_64(z) * (1.0 / DIFF_HEAD_DIM) + NORM_EPS) * gain
            partner = jnp.where(first_half, pltpu.roll(zn, w - half, axis=1), pltpu.roll(zn, half, axis=1))
            return zn * cosf + partner * sin_signed

        q_out[r, :] = (norm_rope(q, qg_ref[...]) * scale).astype(q_out.dtype)
        k_out[r, :] = norm_rope(k, kg_ref[...]).astype(k_out.dtype)
        wp_out[r, :] = _dot(jnp.tanh(lw), w2_ref[...])
        ap_out[r, :] = _dot(la, a2_ref[...])
        gg_out[r, :] = _dot(jax.nn.sigmoid(lg), g2_ref[...])


def _in_proj(x2, pos, attn_g, w_in, w1, a1, g1, w2, a2, g2, mu_wag, qg, kg, invf, B, S):
    T, D = x2.shape
    tile = min(IN_PROJ_ROWS, S)
    tps = IN_PROJ_TILES_PER_STEP if (S // tile) % IN_PROJ_TILES_PER_STEP == 0 else 1
    tm = tile * tps
    ns = S // tm
    w = DIFF_WIDTH
    row = lambda b, i: (b * ns + i, 0)
    full = lambda b, i: (0, 0)

    def fs(a):
        return pl.BlockSpec(a.shape, full)

    outs = ([jax.ShapeDtypeStruct((T, w), BF16)] * 3 + [jax.ShapeDtypeStruct((T, w), F32)] * 6)
    return pl.pallas_call(
        _in_proj_kernel,
        grid=(B, ns),
        in_specs=[pl.BlockSpec((tm, D), row), pl.BlockSpec((tm // 2, 2), row), fs(attn_g), fs(w_in),
                  fs(w1), fs(a1), fs(g1), fs(w2), fs(a2), fs(g2), fs(mu_wag), fs(qg), fs(kg),
                  fs(invf)],
        out_specs=[pl.BlockSpec((tm, w), row)] * 9,
        out_shape=outs,
        scratch_shapes=[pltpu.VMEM((V7X_SUBLANES, D), F32)],
        compiler_params=_cparams(("arbitrary", "arbitrary")),
        name="in_proj",
    )(x2, pos, attn_g, w_in, w1, a1, g1, w2, a2, g2, mu_wag, qg, kg, invf)


def _attn_kernel(lq1_ref, lk1_ref, lq2_ref, lk2_ref, ogt_ref, q_ref, k_ref, v_ref, o_ref,
                 vt_ref, s_ref, acc_ref):
    nh, nq, hw, tq = vt_ref.shape
    tk = tq
    heads = range(nh)
    lam = (jnp.exp(jnp.sum(lq1_ref[...] * lk1_ref[...], axis=-1, keepdims=True))
           - jnp.exp(jnp.sum(lq2_ref[...] * lk2_ref[...], axis=-1, keepdims=True)) + LAM_INIT)

    for h in heads:
        for c in range(nq):
            vt_ref[h, c] = (v_ref[c * tk:(c + 1) * tk, h * hw:(h + 1) * hw]
                            .astype(F32).T.astype(vt_ref.dtype))

    lane = lax.broadcasted_iota(jnp.int32, (tq, hw), 1)

    def stacked_queries(t):
        out = []
        for h in heads:
            q = q_ref[t * tq:(t + 1) * tq, h * hw:(h + 1) * hw]
            zero = jnp.zeros_like(q)
            out.append(jnp.concatenate([jnp.where(lane < DIFF_HEAD_DIM, q, zero),
                                        jnp.where(lane >= DIFF_HEAD_DIM, q, zero)], axis=0))
        return out

    def scores(j, qs):
        start = pl.multiple_of(j * tk, tk)
        return [lax.dot_general(k_ref[pl.ds(start, tk), h * hw:(h + 1) * hw], qs[h],
                                (((1,), (1,)), ((), ())), preferred_element_type=F32) for h in heads]

    def colmax(m, st):
        return tuple(jnp.maximum(mh, jnp.max(s, axis=0, keepdims=True)) for mh, s in zip(m, st))

    krow = lax.broadcasted_iota(jnp.int32, (tk, 2 * tq), 0)
    qcol = lax.broadcasted_iota(jnp.int32, (tk, 2 * tq), 1)
    causal = krow <= jnp.where(qcol >= tq, qcol - tq, qcol)
    ninf = tuple(jnp.full((1, 2 * tq), -jnp.inf, F32) for _ in heads)
    zeros = tuple(jnp.zeros((1, 2 * tq), F32) for _ in heads)

    def diagonal(t, qs, m):
        st = [jnp.where(causal, s, -jnp.inf) for s in scores(t, qs)]
        for h in heads:
            s_ref[t % 2, h, t] = st[h]
        return colmax(m, st)

    m_cur = diagonal(0, stacked_queries(0), ninf)
    for t in range(nq):
        slot, nslot = t % 2, (t + 1) % 2
        has_next = t + 1 < nq
        qs_next = stacked_queries(t + 1) if has_next else None
        acc_ref[...] = jnp.zeros_like(acc_ref)

        def body(j, carry, slot=slot, nslot=nslot, has_next=has_next, qs_next=qs_next, m_cur=m_cur):
            l, m_next = carry
            pt = [jnp.exp2(s_ref[slot, h, j] - m_cur[h]) for h in heads]
            pv = [jnp.dot(vt_ref[h, j], pt[h].astype(BF16), preferred_element_type=F32) for h in heads]
            for h in heads:
                acc_ref[h] += pv[h]
            l = tuple(lh + jnp.sum(p, axis=0, keepdims=True) for lh, p in zip(l, pt))
            if has_next:
                st = scores(j, qs_next)
                for h in heads:
                    s_ref[nslot, h, j] = st[h]
                m_next = colmax(m_next, st)
            return l, m_next

        lsum, m_next = lax.fori_loop(0, t + 1, body, (zeros, ninf), unroll=ATTN_UNROLL)
        if has_next:
            m_next = diagonal(t + 1, qs_next, m_next)

        for h in heads:
            acc = acc_ref[h]
            l = lsum[h]
            ot = acc[:, :tq] / l[:, :tq] - lam * (acc[:, tq:] / l[:, tq:])
            ot = ot * lax.rsqrt(jnp.mean(ot * ot, axis=0, keepdims=True) + SUBLN_EPS) * ogt_ref[...]
            o_ref[t * tq:(t + 1) * tq, h * hw:(h + 1) * hw] = (ot * (1.0 - LAM_INIT)).T.astype(o_ref.dtype)
        m_cur = m_next


def _attention(q, k, v, lq1, lk1, lq2, lk2, og, B, S):
    T = q.shape[0]
    tq = min(ATTN_ROWS, S)
    nq = S // tq
    hw = 2 * DIFF_HEAD_DIM
    nh = ATTN_HEADS_PER_STEP
    gw = nh * hw
    small = lambda b, h: (0, 0)
    seq = pl.BlockSpec((S, gw), lambda b, h: (b, h))
    return pl.pallas_call(
        _attn_kernel,
        grid=(B, DIFF_HEADS // nh),
        in_specs=[pl.BlockSpec(lq1.shape, small), pl.BlockSpec(lk1.shape, small),
                  pl.BlockSpec(lq2.shape, small), pl.BlockSpec(lk2.shape, small),
                  pl.BlockSpec(og.shape, small), seq, seq, seq],
        out_specs=seq,
        out_shape=jax.ShapeDtypeStruct((T, DIFF_WIDTH), BF16),
        scratch_shapes=[pltpu.VMEM((nh, nq, hw, tq), BF16), pltpu.VMEM((2, nh, nq, tq, 2 * tq), F32),
                        pltpu.VMEM((nh, hw, 2 * tq), F32)],
        compiler_params=_cparams(("arbitrary", "arbitrary")),
        name="diff_attn",
    )(lq1, lk1, lq2, lk2, og, q, k, v)


def _rwkv_kernel(zr_ref, zk_ref, zv_ref, wp_ref, ap_ref, gg_ref, mu_ref, w0_ref, a0_ref, kk_ref,
                 ka_ref, rk_ref, gnw_ref, gnb_ref, o_ref, state_ref, carry_ref):
    R = zr_ref.shape[0]
    C = min(RWKV_CHUNK, R)
    nch = R // C
    N = RWKV_HEAD
    W = RWKV_WIDTH
    HP = 2 * N
    n_pairs = W // HP

    @pl.when(pl.program_id(1) == 0)
    def _():
        state_ref[...] = jnp.zeros_like(state_ref)
        carry_ref[...] = jnp.zeros_like(carry_ref)

    zr, zk, zv = zr_ref[...], zk_ref[...], zv_ref[...]
    last = V7X_SUBLANES - 1
    r = zr + (_shift_rows(zr, carry_ref[last:last + 1, 0:W]) - zr) * mu_ref[0:1, :]
    k = zk + (_shift_rows(zk, carry_ref[last:last + 1, W:2 * W]) - zk) * mu_ref[1:2, :]
    v = zv + (_shift_rows(zv, carry_ref[last:last + 1, 2 * W:3 * W]) - zv) * mu_ref[2:3, :]
    carry_ref[:, 0:W] = zr[R - V7X_SUBLANES:R, :]
    carry_ref[:, W:2 * W] = zk[R - V7X_SUBLANES:R, :]
    carry_ref[:, 2 * W:3 * W] = zv[R - V7X_SUBLANES:R, :]

    lw = -math.exp(-0.5) * jax.nn.sigmoid(w0_ref[...] + wp_ref[...])
    a = jax.nn.sigmoid(a0_ref[...] + ap_ref[...])
    kk = k * kk_ref[...]
    kk = kk * lax.rsqrt(jnp.maximum(_sumsq_64(kk), 1e-24))
    k2 = k * (1.0 + (a - 1.0) * ka_ref[...])
    a_s = -kk
    b_s = kk * a

    rr = lax.broadcasted_iota(jnp.int32, (R, R), 0)
    cc = lax.broadcasted_iota(jnp.int32, (R, R), 1)
    same_chunk = (rr // C) == (cc // C)
    L = _dot_exact_lhs((same_chunk & (cc <= rr)).astype(BF16), lw)
    bonus_w = r * k2 * rk_ref[...]

    P2 = 2 * C
    sr = lax.broadcasted_iota(jnp.int32, (P2, HP), 0)
    sc = lax.broadcasted_iota(jnp.int32, (P2, HP), 1)
    stack_mask = (sr < C) == (sc < N)
    br = lax.broadcasted_iota(jnp.int32, (P2, P2), 0)
    bc = lax.broadcasted_iota(jnp.int32, (P2, P2), 1)
    same_head = (br < C) == (bc < C)
    tr = jnp.where(br >= C, br - C, br)
    tc = jnp.where(bc >= C, bc - C, bc)
    strict = same_head & (tc < tr)
    lower = same_head & (tc <= tr)
    eye_p = (br == bc).astype(F32)
    kr = lax.broadcasted_iota(jnp.int32, (HP, HP), 0)
    kc = lax.broadcasted_iota(jnp.int32, (HP, HP), 1)
    eye_k = kr == kc

    def dup(x):
        return jnp.concatenate([x, x], axis=0)

    def stack(x):
        return jnp.where(stack_mask, dup(x), 0.0)

    n_sq = int(math.log2(C)) - 1
    items = [(c, p) for c in range(nch) for p in range(n_pairs)]

    def prep(c, p):
        rows = slice(c * C, (c + 1) * C)
        lanes = slice(p * HP, (p + 1) * HP)
        Lc = L[rows, lanes]
        lwc = lw[rows, lanes]
        LC = Lc[C - 1:C, :]
        enL = jnp.exp(-Lc)
        eCL = jnp.exp(LC - Lc)
        b_c, k_c = b_s[rows, lanes], k2[rows, lanes]
        return dict(
            xa=stack(a_s[rows, lanes] * jnp.exp(Lc - lwc)), xr=stack(r[rows, lanes] * jnp.exp(Lc)),
            bt=dup(b_c * enL), kt=dup(k_c * enL), bh=stack(b_c * eCL), kh=stack(k_c * eCL),
            vs=stack(v[rows, lanes]), pc=jnp.exp(LC))

    d = [prep(c, p) for c, p in items]
    aa = [_dot_nt(jnp.concatenate([e["xa"], e["xr"]], axis=0),
                  jnp.concatenate([e["bt"], e["kt"]], axis=0)) for e in d]
    a_ab = [jnp.where(strict, m[:P2, :P2], 0.0) for m in aa]
    a_ak = [jnp.where(strict, m[:P2, P2:], 0.0) for m in aa]
    a_rb = [jnp.where(lower, m[P2:, :P2], 0.0) for m in aa]
    a_rk = [jnp.where(lower, m[P2:, P2:], 0.0) for m in aa]
    tm = [eye_p + m for m in a_ab]
    npw = a_ab
    for _ in range(n_sq):
        npw = [_dot(m, m) for m in npw]
        tm = [t + _dot(t, m) for t, m in zip(tm, npw)]
    av = [_dot(m, e["vs"]) for m, e in zip(a_ak, d)]
    z = [_dot(t, jnp.concatenate([e["xa"], x], axis=1)) for t, e, x in zip(tm, d, av)]
    zero_v = jnp.zeros((P2, HP), F32)
    rhs = [jnp.concatenate([x, jnp.concatenate([zero_v, e["vs"]], axis=1)], axis=0) for e, x in zip(d, z)]
    lhs = [jnp.concatenate([jnp.concatenate([m1, m2], axis=1),
                            jnp.concatenate([e["bh"], e["kh"]], axis=0).T], axis=0)
           for m1, m2, e in zip(a_rb, a_rk, d)]
    wg = [_dot(a, b) for a, b in zip(lhs, rhs)]

    for idx, (c, p) in enumerate(items):
        e = d[idx]
        rows = slice(c * C, (c + 1) * C)
        lanes = slice(p * HP, (p + 1) * HP)
        rp = e["xr"] + wg[idx][:P2, :HP]
        y0 = wg[idx][:P2, HP:]
        gm = wg[idx][P2:, :HP]
        hm = wg[idx][P2:, HP:]
        st = state_ref[p]
        yg = _dot(jnp.concatenate([rp, gm], axis=0), st)
        pc_col = jnp.sum(jnp.where(eye_k, e["pc"], 0.0), axis=1, keepdims=True)
        state_ref[p] = pc_col * st + yg[P2:] + hm
        ys = yg[:P2] + y0
        mu = jnp.sum(ys, axis=-1, keepdims=True) * (1.0 / N)
        yc = jnp.where(stack_mask, ys - mu, 0.0)
        var = jnp.sum(yc * yc, axis=-1, keepdims=True) * (1.0 / N)
        yn = yc * lax.rsqrt(var + GN_EPS)
        bonus = jnp.sum(stack(bonus_w[rows, lanes]), axis=-1, keepdims=True) * e["vs"]
        yn = yn[:C] + yn[C:]
        bonus = bonus[:C] + bonus[C:]
        o_ref[rows, lanes] = ((yn * gnw_ref[:, lanes] + gnb_ref[:, lanes] + bonus)
                              * gg_ref[rows, lanes]).astype(o_ref.dtype)


def _rwkv(zr, zk, zv, wp, ap, gg, mu_rkv, w0, a0, k_k, k_a, r_k, gn_w, gn_b, B, S):
    T = zr.shape[0]
    C = min(RWKV_ROWS, S)
    nc = S // C
    W = RWKV_WIDTH
    HP = 2 * RWKV_HEAD
    row = lambda b, i: (b * nc + i, 0)
    full = lambda b, i: (0, 0)

    def fs(a):
        return pl.BlockSpec(a.shape, full)

    return pl.pallas_call(
        _rwkv_kernel,
        grid=(B, nc),
        in_specs=[pl.BlockSpec((C, W), row)] * 6 + [fs(mu_rkv), fs(w0), fs(a0), fs(k_k), fs(k_a),
                                                    fs(r_k), fs(gn_w), fs(gn_b)],
        out_specs=pl.BlockSpec((C, W), row),
        out_shape=jax.ShapeDtypeStruct((T, W), BF16),
        scratch_shapes=[pltpu.VMEM((W // HP, HP, HP), F32),
                        pltpu.VMEM((V7X_SUBLANES, 3 * W), F32)],
        compiler_params=_cparams(("arbitrary", "arbitrary")),
        name="rwkv7",
    )(zr, zk, zv, wp, ap, gg, mu_rkv, w0, a0, k_k, k_a, r_k, gn_w, gn_b)


ROUTE_COLS = 8
ROUTER_EXPERT_LANE = N_GROUPS


def _out_router_kernel(x_ref, od_ref, orw_ref, wo_ref, mg_ref, wrh_ref, wrl_ref,
                       x1_out, hm_out, route_out, cnt_out, sl_out):
    tm = PROJ_ROWS if x_ref.shape[0] >= PROJ_ROWS else x_ref.shape[0]
    tiles = [slice(t * tm, (t + 1) * tm) for t in range(x_ref.shape[0] // tm)]
    x1 = [x_ref[r, :] + jnp.dot(od_ref[r, :], wo_ref[0:DIFF_WIDTH, :], preferred_element_type=F32)
          + jnp.dot(orw_ref[r, :], wo_ref[DIFF_WIDTH:, :], preferred_element_type=F32) for r in tiles]
    hm = [_rms(v, mg_ref[...], NORM_EPS) for v in x1]
    for r, v, h in zip(tiles, x1, hm):
        x1_out[r, :] = v
        hm_out[r, :] = h.astype(hm_out.dtype)

    def logits(h):
        hi = h.astype(BF16)
        lo = (h - hi.astype(F32)).astype(BF16)
        return (jnp.dot(hi, wrh_ref[...], preferred_element_type=F32)
                + jnp.dot(hi, wrl_ref[...], preferred_element_type=F32)
                + jnp.dot(lo, wrh_ref[...], preferred_element_type=F32))

    lgs = [logits(h) for h in hm]
    lane = lax.broadcasted_iota(jnp.int32, (tm, V7X_LANES), 1).astype(F32)
    big = float(V7X_LANES)
    ninf = -jnp.inf
    gmask = lane < N_GROUPS

    def choose(lg):
        gmax = jnp.max(jnp.where(gmask, lg, ninf), axis=-1, keepdims=True)
        g_sel = jnp.min(jnp.where(gmask & (lg == gmax), lane, big), axis=-1, keepdims=True)
        g_w = 1.0 / jnp.sum(jnp.where(gmask, jnp.exp(lg - gmax), 0.0), axis=-1, keepdims=True)
        lo_lane = ROUTER_EXPERT_LANE + g_sel * EXPERTS_PER_GROUP
        emask = (lane >= lo_lane) & (lane < lo_lane + EXPERTS_PER_GROUP)
        v1 = jnp.max(jnp.where(emask, lg, ninf), axis=-1, keepdims=True)
        i1 = jnp.min(jnp.where(emask & (lg == v1), lane, big), axis=-1, keepdims=True)
        emask2 = emask & (lane != i1)
        v2 = jnp.max(jnp.where(emask2, lg, ninf), axis=-1, keepdims=True)
        i2 = jnp.min(jnp.where(emask2 & (lg == v2), lane, big), axis=-1, keepdims=True)
        e2x = jnp.exp(v2 - v1)
        den = 1.0 + e2x
        return i1, i2, (1.0 / den) * g_w, (e2x / den) * g_w

    chosen = [choose(lg) for lg in lgs]

    ri = lax.broadcasted_iota(jnp.int32, (tm, tm), 0)
    ci = lax.broadcasted_iota(jnp.int32, (tm, tm), 1)
    before = (ci < ri).astype(BF16)
    li = lax.broadcasted_iota(jnp.int32, (V7X_LANES, V7X_LANES), 0)
    lj = lax.broadcasted_iota(jnp.int32, (V7X_LANES, V7X_LANES), 1)
    lanes_before = (li < lj).astype(BF16)
    ohs = [((lane == i1) | (lane == i2)).astype(F32) for i1, i2, _, _ in chosen]
    prefixes = [jnp.dot(before, oh.astype(BF16), preferred_element_type=F32) for oh in ohs]
    counts = [jnp.broadcast_to(jnp.sum(oh, axis=0, keepdims=True), (V7X_SUBLANES, V7X_LANES)) for oh in ohs]
    padded = [jnp.floor((c + (RUN_ALIGN - 1)) * (1.0 / RUN_ALIGN)) * RUN_ALIGN for c in counts]
    starts = [jnp.dot(pc.astype(BF16), lanes_before, preferred_element_type=F32) for pc in padded]
    col = lax.broadcasted_iota(jnp.int32, (tm, V7X_LANES), 1)
    for t, (r, (i1, i2, wt1, wt2), cnt, start, prefix) in enumerate(zip(tiles, chosen, counts, starts, prefixes)):
        row = prefix + start[0:1, :]
        sl1 = jnp.sum(jnp.where(lane == i1, row, 0.0), axis=-1, keepdims=True)
        sl2 = jnp.sum(jnp.where(lane == i2, row, 0.0), axis=-1, keepdims=True)
        cnt_out[t * V7X_SUBLANES:(t + 1) * V7X_SUBLANES, :] = cnt
        wide = jnp.where(col == 0, sl1, jnp.where(col == 1, sl2, jnp.where(col == 2, wt1, jnp.where(
            col == 3, wt2, 0.0))))
        route_out[r, :] = wide[:, 0:ROUTE_COLS]
        sl_out[t] = wide.T[0:V7X_SUBLANES, :].astype(sl_out.dtype)


def _out_router(x2, od, orw, w_out, moe_g, wr_hi, wr_lo):
    T, D = x2.shape
    tm = min(PROJ_ROWS, T)
    nt = T // tm
    tps = TILES_PER_STEP if nt % TILES_PER_STEP == 0 else 1
    tr = tm * tps
    row = lambda i: (i, 0)
    full = lambda i: (0, 0)

    def fs(a):
        return pl.BlockSpec(a.shape, full)

    return pl.pallas_call(
        _out_router_kernel,
        grid=(nt // tps,),
        in_specs=[pl.BlockSpec((tr, D), row), pl.BlockSpec((tr, DIFF_WIDTH), row),
                  pl.BlockSpec((tr, RWKV_WIDTH), row), fs(w_out), fs(moe_g), fs(wr_hi), fs(wr_lo)],
        out_specs=[pl.BlockSpec((tr, D), row), pl.BlockSpec((tr, D), row),
                   pl.BlockSpec((tr, ROUTE_COLS), row), pl.BlockSpec((tps * V7X_SUBLANES, V7X_LANES), row),
                   pl.BlockSpec((tps, V7X_SUBLANES, tm), lambda i: (i, 0, 0))],
        out_shape=[jax.ShapeDtypeStruct((T, D), F32), jax.ShapeDtypeStruct((T, D), BF16),
                   jax.ShapeDtypeStruct((T, ROUTE_COLS), F32),
                   jax.ShapeDtypeStruct((nt * V7X_SUBLANES, V7X_LANES), F32),
                   jax.ShapeDtypeStruct((nt, V7X_SUBLANES, tm), jnp.int32)],
        compiler_params=_cparams(("arbitrary",)),
        name="out_router",
    )(x2, od, orw, w_out, moe_g, wr_hi, wr_lo)


RUN_ALIGN = V7X_SUBLANES
RUN_CLASSES = 6
TAB_COUNT, TAB_NUSED = 0, RUN_CLASSES
TAB_SRC = V7X_SUBLANES
TAB_DST = TAB_SRC + RUN_CLASSES * N_EXPERTS
TAB_WORDS = 4 * V7X_LANES


def _stage_rows(tm):
    need = 2 * tm + N_EXPERTS * (RUN_ALIGN - 1)
    return -(-need // V7X_LANES) * V7X_LANES


def _run_table(src, dst, length, spare):
    i32 = jnp.int32
    n = src.shape[0]
    cls = jnp.arange(RUN_CLASSES, dtype=i32)[None, :, None]
    ln = length[:, None, :]
    has = ((ln >> 3) >> cls) & 1
    off = ln & ~((2 * RUN_ALIGN << cls) - 1)
    pos = jnp.cumsum(has, axis=-1) - 1
    k = jnp.arange(N_EXPERTS, dtype=i32)
    pick = (has[..., None] == 1) & (pos[..., None] == k)

    def compact(v):
        return jnp.sum(jnp.where(pick, (v[:, None, :] + off)[..., None], 0), axis=2).reshape(n, -1)

    head = jnp.concatenate([jnp.sum(has, axis=-1), jnp.broadcast_to(spare, (n, 1)).astype(i32),
                            jnp.zeros((n, TAB_SRC - RUN_CLASSES - 1), i32)], axis=1)
    tab = jnp.concatenate([head, compact(src), compact(dst)], axis=1)
    return jnp.pad(tab, ((0, 0), (0, TAB_WORDS - tab.shape[1]))).reshape(n, 1, TAB_WORDS)


def _run_copies(tab_ref, tile, make_copy, op):
    for c in range(RUN_CLASSES):
        size = RUN_ALIGN << c

        def one(k, carry, c=c, size=size):
            src = tab_ref[tile, 0, TAB_SRC + c * N_EXPERTS + k]
            dst = tab_ref[tile, 0, TAB_DST + c * N_EXPERTS + k]
            cp = make_copy(pl.multiple_of(src, RUN_ALIGN), pl.multiple_of(dst, RUN_ALIGN), size)
            if op == "start":
                cp.start(priority=c % 2)
            else:
                cp.wait()
            return carry

        lax.fori_loop(0, tab_ref[tile, 0, TAB_COUNT + c], one, 0)


def _dispatch_kernel(tab_ref, tail_ref, sl_ref, hm_ref, xb_ref, stage_ref, zero_ref, sem, zsem):
    tps, ns = stage_ref.shape[0], stage_ref.shape[1]
    tm = hm_ref.shape[0] // tps
    bm = zero_ref.shape[0]
    nb = xb_ref.shape[0] // bm
    n_used = tail_ref[0, 0, TAB_NUSED]

    def tile_copy(t):
        def make_copy(src, dst, size):
            return pltpu.make_async_copy(stage_ref.at[t, pl.ds(src, size)], xb_ref.at[pl.ds(dst, size)], sem)
        return make_copy

    def zero_copy(src, dst, size):
        return pltpu.make_async_copy(zero_ref.at[pl.ds(src, size)], xb_ref.at[pl.ds(dst, size)], zsem)

    def zero_blocks(op):
        def one(b, c):
            getattr(zero_copy(0, pl.multiple_of(b * bm, bm), bm), op)()
            return c
        lax.fori_loop(n_used, nb, one, 0)

    @pl.when(pl.program_id(0) == 0)
    def _():
        zero_ref[...] = jnp.zeros_like(zero_ref)
        _run_copies(tail_ref, 0, zero_copy, "start")
        zero_blocks("start")

    srow = lax.broadcasted_iota(jnp.int32, (ns, tm), 0)
    for t in range(tps):
        sel = (srow == sl_ref[t, 0:1, :]) | (srow == sl_ref[t, 1:2, :])
        stage_ref[t] = _pack16(jnp.dot(sel.astype(BF16), hm_ref[t * tm:(t + 1) * tm, :],
                                       preferred_element_type=F32))
        _run_copies(tab_ref, t, tile_copy(t), "start")
    for t in range(tps):
        _run_copies(tab_ref, t, tile_copy(t), "wait")

    @pl.when(pl.program_id(0) == pl.num_programs(0) - 1)
    def _():
        _run_copies(tail_ref, 0, zero_copy, "wait")
        zero_blocks("wait")


def _dispatch(tab, tail, sl_rows, hm, P):
    T, D = hm.shape
    tm = min(PROJ_ROWS, T)
    nt = T // tm
    tps = TILES_PER_STEP if nt % TILES_PER_STEP == 0 else 1
    return pl.pallas_call(
        _dispatch_kernel,
        grid=(nt // tps,),
        in_specs=[pl.BlockSpec((tps, 1, tab.shape[-1]), lambda i: (i, 0, 0), memory_space=pltpu.SMEM),
                  pl.BlockSpec((1, 1, tail.shape[-1]), lambda i: (0, 0, 0), memory_space=pltpu.SMEM),
                  pl.BlockSpec((tps, V7X_SUBLANES, tm), lambda i: (i, 0, 0)),
                  pl.BlockSpec((tps * tm, D), lambda i: (i, 0))],
        out_specs=pl.BlockSpec(memory_space=pl.ANY),
        out_shape=jax.ShapeDtypeStruct((P, D // 2), PACKED),
        scratch_shapes=[pltpu.VMEM((tps, _stage_rows(tm), D // 2), PACKED),
                        pltpu.VMEM((MOE_ROWS, D // 2), PACKED),
                        pltpu.SemaphoreType.DMA(()), pltpu.SemaphoreType.DMA(())],
        compiler_params=_cparams(("arbitrary",)),
        name="moe_dispatch",
    )(tab, tail, sl_rows, hm)


def _expert_kernel(be_ref, nu_ref, first_ref, nxt_ref, slot_ref, xb_ref, wg_hbm, wu_hbm, wd_hbm, yb_ref,
                   wgf, wuf, wdf, wgb, wub, wdb, sem):
    i = pl.program_id(0)

    def weight_copies(e, s):
        return (pltpu.make_async_copy(wg_hbm.at[e], wgf.at[s], sem.at[s, 0]),
                pltpu.make_async_copy(wu_hbm.at[e], wuf.at[s], sem.at[s, 1]),
                pltpu.make_async_copy(wd_hbm.at[e], wdf.at[s], sem.at[s, 2]))

    @pl.when(i == 0)
    def _():
        for cp in weight_copies(be_ref[0], 0):
            cp.start()

    @pl.when(first_ref[i] == 1)
    def _():
        s = slot_ref[i]
        for cp in weight_copies(be_ref[i], s):
            cp.wait()
        wgb[...] = wgf[s].astype(BF16)
        wub[...] = wuf[s].astype(BF16)
        wdb[...] = wdf[s].astype(BF16)

        @pl.when(nxt_ref[i] >= 0)
        def _():
            for cp in weight_copies(nxt_ref[i], 1 - s):
                cp.start()

    @pl.when(i < nu_ref[0])
    def _():
        xb = _unpack16(xb_ref[...]).astype(BF16)
        gate = jnp.dot(xb, wgb[...], preferred_element_type=F32)
        up = jnp.dot(xb, wub[...], preferred_element_type=F32)
        hdn = (gate * jax.nn.sigmoid(gate)) * up
        y = jnp.dot(hdn.astype(BF16), wdb[...], preferred_element_type=F32)
        yb_ref[...] = _pack16(y.astype(BF16).astype(F32))

    @pl.when(i >= nu_ref[0])
    def _():
        yb_ref[...] = jnp.zeros_like(yb_ref)


def _experts(block_e, n_used, first, nxt, slot, xb, w_gate, w_up, w_down):
    P, DP = xb.shape
    D = 2 * DP
    bm = MOE_ROWS
    nb = P // bm
    E = D_EXPERT
    hbm = pl.BlockSpec(memory_space=pl.ANY)
    grid_spec = pltpu.PrefetchScalarGridSpec(
        num_scalar_prefetch=5,
        grid=(nb,),
        in_specs=[pl.BlockSpec((bm, DP), lambda i, be, nu, *_: (jnp.minimum(i, nu[0] - 1), 0)), hbm, hbm, hbm],
        out_specs=pl.BlockSpec((bm, DP), lambda i, be, nu, *_: (jnp.where(i < nu[0], i, nb - 1), 0)),
        scratch_shapes=[pltpu.VMEM((2, D, E), F32), pltpu.VMEM((2, D, E), F32), pltpu.VMEM((2, E, D), F32),
                        pltpu.VMEM((D, E), BF16), pltpu.VMEM((D, E), BF16), pltpu.VMEM((E, D), BF16),
                        pltpu.SemaphoreType.DMA((2, 3))],
    )
    return pl.pallas_call(
        _expert_kernel,
        grid_spec=grid_spec,
        out_shape=jax.ShapeDtypeStruct((P, DP), PACKED),
        input_output_aliases={5: 0},
        compiler_params=_cparams(("arbitrary",)),
        name="moe_experts",
    )(block_e, n_used, first, nxt, slot, xb, w_gate, w_up, w_down)


def _final_kernel(tab_ref, tabn_ref, x1_ref, route_ref, p_ref, wpl_ref, plg_ref, pgg_ref, wgate_ref, yb_ref,
                  o_ref, stage_ref, sem):
    tps, ns = stage_ref.shape[1], stage_ref.shape[2]
    tm = x1_ref.shape[0] // tps
    tiles = [slice(t * tm, (t + 1) * tm) for t in range(tps)]
    i = pl.program_id(0)
    cur = i % 2

    def gather(tab, slot, op):
        for t in range(tps):
            def make_copy(src, dst, size, t=t):
                return pltpu.make_async_copy(yb_ref.at[pl.ds(dst, size)],
                                             stage_ref.at[slot, t, pl.ds(src, size)], sem.at[slot])
            _run_copies(tab, t, make_copy, op)

    def fetch(tab, slot):
        stage_ref[slot] = jnp.zeros(stage_ref.shape[1:], stage_ref.dtype)
        gather(tab, slot, "start")

    @pl.when(i == 0)
    def _():
        fetch(tab_ref, 0)

    @pl.when(i + 1 < pl.num_programs(0))
    def _():
        fetch(tabn_ref, 1 - cur)

    pe = [_rms(jnp.dot(p_ref[r, :].astype(BF16), wpl_ref[...], preferred_element_type=F32),
               plg_ref[...], NORM_EPS) for r in tiles]
    gather(tab_ref, cur, "wait")

    scol = lax.broadcasted_iota(jnp.int32, (tm, ns), 1).astype(F32)

    def selection(r):
        route = route_ref[r, :]
        wsel = (jnp.where(scol == route[:, 0:1], route[:, 2:3], 0.0)
                + jnp.where(scol == route[:, 1:2], route[:, 3:4], 0.0))
        w_hi = wsel.astype(BF16)
        return w_hi, (wsel - w_hi.astype(F32)).astype(BF16)

    wsel = [selection(r) for r in tiles]
    yb16 = [_unpack16(stage_ref[cur, t]).astype(BF16) for t in range(tps)]
    moe = [jnp.dot(w_hi, y, preferred_element_type=F32) + jnp.dot(w_lo, y, preferred_element_type=F32)
           for (w_hi, w_lo), y in zip(wsel, yb16)]
    x2 = [x1_ref[r, :] + m for r, m in zip(tiles, moe)]
    gate = [jax.nn.sigmoid(_dot(_rms(v, pgg_ref[...], NORM_EPS), wgate_ref[...])) for v in x2]
    for r, v, e, g in zip(tiles, x2, pe, gate):
        o_ref[r, :] = v + e * g


def _final(tab, x1, route, p2, w_pl, pl_g, pl_gate_g, w_gate, yb):
    T, D = x1.shape
    tm = min(PROJ_ROWS, T)
    nt = T // tm
    tps = TILES_PER_STEP if nt % TILES_PER_STEP == 0 else 1
    tr = tps * tm
    row = lambda i: (i, 0)
    full = lambda i: (0, 0)

    def fs(a):
        return pl.BlockSpec(a.shape, full)

    return pl.pallas_call(
        _final_kernel,
        grid=(nt // tps,),
        in_specs=[pl.BlockSpec((tps, 1, tab.shape[-1]), lambda i: (i, 0, 0), memory_space=pltpu.SMEM),
                  pl.BlockSpec((tps, 1, tab.shape[-1]), lambda i: (jnp.minimum(i + 1, nt // tps - 1), 0, 0),
                               memory_space=pltpu.SMEM),
                  pl.BlockSpec((tr, D), row), pl.BlockSpec((tr, ROUTE_COLS), row),
                  pl.BlockSpec((tr, PL_DIM), row), fs(w_pl), fs(pl_g), fs(pl_gate_g), fs(w_gate),
                  pl.BlockSpec(memory_space=pl.ANY)],
        out_specs=pl.BlockSpec((tr, D), row),
        out_shape=jax.ShapeDtypeStruct((T, D), F32),
        scratch_shapes=[pltpu.VMEM((2, tps, _stage_rows(tm), D // 2), PACKED), pltpu.SemaphoreType.DMA((2,))],
        compiler_params=_cparams(("arbitrary",)),
        name="combine_final",
    )(tab, tab, x1, route, p2, w_pl, pl_g, pl_gate_g, w_gate, yb)


def kernel(x, p, positions, attn_norm_g, w_in, q_norm_g, k_norm_g, lambda_q1, lambda_k1, lambda_q2, lambda_k2, diff_out_g, mu_rkv, mu_wag, w0, w_lora_a, w_lora_b, a0, a_lora_a, a_lora_b, g_lora_a, g_lora_b, k_k, k_a, r_k, gn_w, gn_b, w_out, moe_norm_g, w_group, w_expert_router, w_gate, w_up, w_down, w_pl, pl_norm_g, pl_gate_norm_g, w_pl_gate):
    B, S, D = x.shape
    T = B * S
    assert p.shape[0] == 1, "one layer"
    x2 = x.reshape(T, D)
    p2 = p[0].reshape(T, PL_DIM)
    pos = positions.astype(F32).reshape(T // 2, 2)

    def row(a):
        return a.reshape(1, -1).astype(F32)

    half = DIFF_HEAD_DIM // 2
    inv_freq = ROPE_THETA ** (-jnp.arange(half, dtype=F32) / half)
    invf = jnp.tile(inv_freq, V7X_LANES // half).reshape(1, V7X_LANES)
    reps = DIFF_WIDTH // DIFF_HEAD_DIM

    q, k, v, zr, zk, zv, wp, ap, gg = _in_proj(
        x2, pos, row(attn_norm_g[0]), w_in[0].astype(BF16),
        w_lora_a[0].astype(BF16), a_lora_a[0].astype(BF16), g_lora_a[0].astype(BF16),
        w_lora_b[0].astype(BF16), a_lora_b[0].astype(BF16), g_lora_b[0].astype(BF16),
        mu_wag[0].astype(F32), row(jnp.tile(q_norm_g[0], reps)), row(jnp.tile(k_norm_g[0], reps)),
        invf, B, S)

    o_diff = _attention(q, k, v, row(lambda_q1[0]), row(lambda_k1[0]), row(lambda_q2[0]),
                        row(lambda_k2[0]), diff_out_g[0].astype(F32).reshape(-1, 1), B, S)
    o_rwkv = _rwkv(zr, zk, zv, wp, ap, gg, mu_rkv[0].astype(F32), row(w0[0]), row(a0[0]),
                   row(k_k[0]), row(k_a[0]), row(r_k[0]), row(gn_w[0]), row(gn_b[0]), B, S)

    wr = jnp.concatenate([w_group[0], jnp.transpose(w_expert_router[0], (1, 0, 2)).reshape(D, N_EXPERTS)],
                         axis=1).astype(F32)
    wr = jnp.pad(wr, ((0, 0), (0, V7X_LANES - wr.shape[1])))
    wr_hi = wr.astype(BF16)
    wr_lo = (wr - wr_hi.astype(F32)).astype(BF16)
    x1, hm, route, cnt, sl_rows = _out_router(x2, o_diff, o_rwkv, w_out[0].astype(BF16), row(moe_norm_g[0]),
                                     wr_hi, wr_lo)

    bm = MOE_ROWS
    tm = min(PROJ_ROWS, T)
    nt = T // tm
    i32 = jnp.int32
    cnt_te = cnt.reshape(nt, V7X_SUBLANES, V7X_LANES)[:, 0, ROUTER_EXPERT_LANE:ROUTER_EXPERT_LANE + N_EXPERTS]
    len_te = (cnt_te.astype(i32) + RUN_ALIGN - 1) // RUN_ALIGN * RUN_ALIGN
    tot_e = jnp.sum(len_te, axis=0)
    pcounts = (tot_e + bm - 1) // bm * bm
    pends = jnp.cumsum(pcounts)
    pstarts = pends - pcounts
    dst_te = pstarts[None, :] + jnp.cumsum(len_te, axis=0) - len_te
    src_te = jnp.cumsum(len_te, axis=1) - len_te
    tab = _run_table(src_te, dst_te, len_te, 0)
    nb = -(-(2 * T + nt * N_EXPERTS * (RUN_ALIGN - 1)) // bm) + N_EXPERTS
    P = nb * bm
    n_used = (pends[-1] // bm).astype(i32).reshape(1)
    block_start = jnp.arange(nb, dtype=i32) * bm
    block_e = jnp.minimum(jnp.sum((pends[None, :] <= block_start[:, None]).astype(i32), axis=1),
                          N_EXPERTS - 1)
    tail = _run_table(jnp.zeros((1, N_EXPERTS), i32), (pstarts + tot_e)[None, :], (pcounts - tot_e)[None, :],
                      n_used)
    xb = _dispatch(tab, tail, sl_rows, hm, P)
    blk = jnp.arange(nb, dtype=i32)
    first = ((blk < n_used[0]) & ((blk == 0) | (block_e != jnp.roll(block_e, 1)))).astype(i32)
    slot = (jnp.cumsum(first) - 1) % 2
    eidx = jnp.arange(N_EXPERTS, dtype=i32)
    later = jnp.where((pcounts > 0)[None, :] & (eidx[None, :] > eidx[:, None]), eidx[None, :], N_EXPERTS)
    nxt_e = jnp.min(later, axis=1)
    nxt = jnp.where(nxt_e == N_EXPERTS, -1, nxt_e)[block_e]
    yb = _experts(block_e, n_used, first, nxt, slot.astype(i32), xb, w_gate[0], w_up[0], w_down[0])
    out = _final(tab, x1, route, p2, w_pl[0].astype(BF16), row(pl_norm_g[0]), row(pl_gate_norm_g[0]),
                 w_pl_gate[0].astype(BF16), yb)
    return out.reshape(B, S, D)
```
